```python
import math
import jax, jax.numpy as jnp
from jax import lax
import numpy as np

D_MODEL = 1024
BATCH = 8
SEQ = 2048
DEPTH = 4

D_FF = 2816
FFN_HALF = 0.5
DN_HEAD_DIM = 128
DN_HEADS = D_MODEL // 128
DN_WIDTH = DN_HEADS * DN_HEAD_DIM
DN_CONV = 4
DN_CHUNK = 64
SB_HEAD_DIM = 128
SB_HEADS = D_MODEL // 128
SB_WIDTH = SB_HEADS * SB_HEAD_DIM
SB_BLOCK = 128
IN_SIZES = (3 * DN_WIDTH, DN_WIDTH, DN_HEADS, DN_HEADS, SB_WIDTH, SB_WIDTH, SB_WIDTH, D_MODEL, D_MODEL)
N_IN = 4 * DN_WIDTH + 2 * DN_HEADS + 3 * SB_WIDTH + 2 * D_MODEL
RMS_EPS = 1e-6
L2_EPS = 1e-6

kernel_name = "hybrid_deltanet_stickbreaking_macaron"


def rmsnorm(x, gain):
    xf = x.astype(jnp.float32)
    y = xf * lax.rsqrt(jnp.mean(xf * xf, axis=-1, keepdims=True) + RMS_EPS)
    return (y * gain.astype(jnp.float32)).astype(x.dtype)


def l2norm(x):
    xf = x.astype(jnp.float32)
    return xf * lax.rsqrt(jnp.sum(xf * xf, axis=-1, keepdims=True) + L2_EPS)


def swiglu(h, w_in, w_out):
    gate, up = jnp.split(h @ w_in, 2, axis=-1)
    return (jax.nn.silu(gate) * up) @ w_out


def causal_depthwise_conv(x, w):
    width, ch = w.shape
    return lax.conv_general_dilated(
        x, w[:, None, :], window_strides=(1,), padding=((width - 1, 0),),
        dimension_numbers=("NWC", "WIO", "NWC"), feature_group_count=ch)


def split_cols(proj):
    parts, start = [], 0
    for size in IN_SIZES:
        parts.append(proj[..., start:start + size])
        start += size
    return parts


def to_heads(t, n_heads, head_dim):
    b, s, _ = t.shape
    return t.reshape(b, s, n_heads, head_dim).transpose(0, 2, 1, 3)


def gated_delta_rule_chunked(q, k, v, g, beta):
    b, h, t, dk = q.shape
    dv = v.shape[-1]
    c = DN_CHUNK
    n = t // c
    q = (q * (dk ** -0.5)).reshape(b, h, n, c, dk)
    k = k.reshape(b, h, n, c, dk)
    v = v.astype(jnp.float32).reshape(b, h, n, c, dv)
    beta = beta.reshape(b, h, n, c)
    g = lax.cumsum(g.reshape(b, h, n, c), axis=3)
    idx = jnp.arange(c)
    lower_incl = idx[:, None] >= idx[None, :]
    strict = idx[:, None] > idx[None, :]
    decay = jnp.exp(jnp.where(lower_incl, g[..., :, None] - g[..., None, :], -jnp.inf))
    k_beta = k * beta[..., None]
    lmat = jnp.where(strict, jnp.einsum("bhnid,bhnjd->bhnij", k_beta, k) * decay, 0.0)
    amat = lmat + jnp.eye(c, dtype=jnp.float32)
    rhs = jnp.concatenate([v * beta[..., None], k_beta * jnp.exp(g)[..., None]], axis=-1)
    sol = lax.linalg.triangular_solve(amat, rhs, left_side=True, lower=True, unit_diagonal=True)
    u, w = sol[..., :dv], sol[..., dv:]
    attn_intra = jnp.einsum("bhnid,bhnjd->bhnij", q, k) * decay
    q_dec = q * jnp.exp(g)[..., None]
    g_last = g[..., -1]
    k_dec = k * jnp.exp(g_last[..., None] - g)[..., None]

    def chunk_step(state, inp):
        q_i, k_i, u_i, w_i, a_i, gl_i = inp
        v_new = u_i - jnp.einsum("bhcd,bhdv->bhcv", w_i, state)
        o_i = jnp.einsum("bhcd,bhdv->bhcv", q_i, state) + jnp.einsum("bhij,bhjv->bhiv", a_i, v_new)
        state = state * jnp.exp(gl_i)[..., None, None] + jnp.einsum("bhcd,bhcv->bhdv", k_i, v_new)
        return state, o_i

    chunk_first = lambda arr: jnp.moveaxis(arr, 2, 0)
    s0 = jnp.zeros((b, h, dk, dv), jnp.float32)
    _, o = lax.scan(chunk_step, s0, (chunk_first(q_dec), chunk_first(k_dec), chunk_first(u),
                                     chunk_first(w), chunk_first(attn_intra), chunk_first(g_last)))
    return jnp.moveaxis(o, 0, 2).reshape(b, h, t, dv)


def stick_breaking_attention(q, k, v):
    t = q.shape[2]
    scale = q.shape[-1] ** -0.5
    outs = []
    for blk in range(t // SB_BLOCK):
        t0, t1 = blk * SB_BLOCK, (blk + 1) * SB_BLOCK
        z = jnp.einsum("bhqd,bhkd->bhqk", q[:, :, t0:t1], k[:, :, :t1]).astype(jnp.float32) * scale
        causal = jnp.arange(t1)[None, :] < (t0 + jnp.arange(SB_BLOCK))[:, None]
        log_1mb = jnp.where(causal, -jax.nn.softplus(z), 0.0)
        survive = lax.cumsum(log_1mb, axis=3, reverse=True) - log_1mb
        weights = jnp.where(causal, jnp.exp(jax.nn.log_sigmoid(z) + survive), 0.0)
        outs.append(jnp.einsum("bhqk,bhkd->bhqd", weights.astype(v.dtype), v[:, :, :t1]))
    return jnp.concatenate(outs, axis=2)


def hybrid_mixer(h, w_in, conv_w, a_log, dt_bias, dn_out_norm, sb_q_norm, sb_k_norm,
                 w_branch_a, w_branch_b, w_out):
    b, t, _ = h.shape
    dn_qkv, dn_z, dn_b, dn_a, sb_q, sb_k, sb_v, gate_a, gate_b = split_cols(h @ w_in)
    qkv = jax.nn.silu(causal_depthwise_conv(dn_qkv, conv_w))
    q, k, v = jnp.split(qkv, 3, axis=-1)
    q = l2norm(to_heads(q, DN_HEADS, DN_HEAD_DIM))
    k = l2norm(to_heads(k, DN_HEADS, DN_HEAD_DIM))
    v = to_heads(v, DN_HEADS, DN_HEAD_DIM)
    beta = jax.nn.sigmoid(dn_b.astype(jnp.float32)).transpose(0, 2, 1)
    g = (-jnp.exp(a_log.astype(jnp.float32))
         * jax.nn.softplus(dn_a.astype(jnp.float32) + dt_bias.astype(jnp.float32))).transpose(0, 2, 1)
    o_a = gated_delta_rule_chunked(q, k, v, g, beta).transpose(0, 2, 1, 3)
    o_a = rmsnorm(o_a, dn_out_norm) * jax.nn.silu(dn_z.reshape(b, t, DN_HEADS, DN_HEAD_DIM).astype(jnp.float32))
    y_a = o_a.reshape(b, t, DN_WIDTH).astype(h.dtype) @ w_branch_a
    qb = rmsnorm(to_heads(sb_q, SB_HEADS, SB_HEAD_DIM), sb_q_norm)
    kb = rmsnorm(to_heads(sb_k, SB_HEADS, SB_HEAD_DIM), sb_k_norm)
    vb = to_heads(sb_v, SB_HEADS, SB_HEAD_DIM)
    o_b = stick_breaking_attention(qb, kb, vb).transpose(0, 2, 1, 3).reshape(b, t, SB_WIDTH)
    y_b = o_b @ w_branch_b
    merged = jax.nn.sigmoid(gate_a) * y_a + jax.nn.sigmoid(gate_b) * y_b
    return merged @ w_out


def _fwd_setup_inputs(seed: int = 0) -> dict:
    key = jax.random.key(seed)
    ks = jax.random.split(key, 20)
    L = DEPTH

    def dense(k, shape, fan_in):
        return jax.random.normal(k, shape, jnp.float32) * (fan_in ** -0.5)

    def gain(k, shape):
        return 1.0 + 0.1 * jax.random.normal(k, shape, jnp.float32)

    dt = jnp.exp(jax.random.uniform(ks[8], (L, DN_HEADS), jnp.float32, math.log(1e-3), math.log(1e-1)))
    return {
        "x": jax.random.normal(ks[0], (BATCH, SEQ, D_MODEL), jnp.float32),
        "ffn1_norm": gain(ks[1], (L, D_MODEL)),
        "ffn1_w_in": dense(ks[2], (L, D_MODEL, 2 * D_FF), D_MODEL),
        "ffn1_w_out": dense(ks[3], (L, D_FF, D_MODEL), D_FF),
        "mix_norm": gain(ks[4], (L, D_MODEL)),
        "w_in": dense(ks[5], (L, D_MODEL, N_IN), D_MODEL),
        "dn_conv_w": dense(ks[6], (L, DN_CONV, 3 * DN_WIDTH), DN_CONV),
        "dn_a_log": jnp.log(jax.random.uniform(ks[7], (L, DN_HEADS), jnp.float32, 1.0, 16.0)),
        "dn_dt_bias": dt + jnp.log(-jnp.expm1(-dt)),
        "dn_out_norm": gain(ks[9], (L, DN_HEAD_DIM)),
        "sb_q_norm": gain(ks[10], (L, SB_HEAD_DIM)),
        "sb_k_norm": gain(ks[11], (L, SB_HEAD_DIM)),
        "w_branch_a": dense(ks[12], (L, DN_WIDTH, D_MODEL), DN_WIDTH),
        "w_branch_b": dense(ks[13], (L, SB_WIDTH, D_MODEL), SB_WIDTH),
        "w_out": dense(ks[14], (L, D_MODEL, D_MODEL), D_MODEL),
        "ffn2_norm": gain(ks[15], (L, D_MODEL)),
        "ffn2_w_in": dense(ks[16], (L, D_MODEL, 2 * D_FF), D_MODEL),
        "ffn2_w_out": dense(ks[17], (L, D_FF, D_MODEL), D_FF),
    }


def _fwd_reference(x, ffn1_norm, ffn1_w_in, ffn1_w_out, mix_norm, w_in, dn_conv_w, dn_a_log, dn_dt_bias,
              dn_out_norm, sb_q_norm, sb_k_norm, w_branch_a, w_branch_b, w_out,
              ffn2_norm, ffn2_w_in, ffn2_w_out):
    for l in range(DEPTH):
        x = x + FFN_HALF * swiglu(rmsnorm(x, ffn1_norm[l]), ffn1_w_in[l], ffn1_w_out[l])
        x = x + hybrid_mixer(rmsnorm(x, mix_norm[l]), w_in[l], dn_conv_w[l], dn_a_log[l], dn_dt_bias[l],
                             dn_out_norm[l], sb_q_norm[l], sb_k_norm[l],
                             w_branch_a[l], w_branch_b[l], w_out[l])
        x = x + FFN_HALF * swiglu(rmsnorm(x, ffn2_norm[l]), ffn2_w_in[l], ffn2_w_out[l])
    return x


import jax as _jax
import jax.numpy as _jnp

TWIN_FORMAT = 'train_step'
FWD_PARAMS = ['x', 'ffn1_norm', 'ffn1_w_in', 'ffn1_w_out', 'mix_norm', 'w_in', 'dn_conv_w', 'dn_a_log', 'dn_dt_bias', 'dn_out_norm', 'sb_q_norm', 'sb_k_norm', 'w_branch_a', 'w_branch_b', 'w_out', 'ffn2_norm', 'ffn2_w_in', 'ffn2_w_out']
TWIN_WEIGHTS = ['ffn1_norm', 'ffn1_w_in', 'ffn1_w_out', 'mix_norm', 'w_in', 'dn_conv_w', 'dn_a_log', 'dn_dt_bias', 'dn_out_norm', 'sb_q_norm', 'sb_k_norm', 'w_branch_a', 'w_branch_b', 'w_out', 'ffn2_norm', 'ffn2_w_in', 'ffn2_w_out']
TWIN_DIFF_INPUT = 'x'
TWIN_INPUTS = ['x', 'ffn1_norm', 'ffn1_w_in', 'ffn1_w_out', 'mix_norm', 'w_in', 'dn_conv_w', 'dn_a_log', 'dn_dt_bias', 'dn_out_norm', 'sb_q_norm', 'sb_k_norm', 'w_branch_a', 'w_branch_b', 'w_out', 'ffn2_norm', 'ffn2_w_in', 'ffn2_w_out', 'loss_target', 'm_ffn1_norm', 'm_ffn1_w_in', 'm_ffn1_w_out', 'm_mix_norm', 'm_w_in', 'm_dn_conv_w', 'm_dn_a_log', 'm_dn_dt_bias', 'm_dn_out_norm', 'm_sb_q_norm', 'm_sb_k_norm', 'm_w_branch_a', 'm_w_branch_b', 'm_w_out', 'm_ffn2_norm', 'm_ffn2_w_in', 'm_ffn2_w_out', 'v_ffn1_norm', 'v_ffn1_w_in', 'v_ffn1_w_out', 'v_mix_norm', 'v_w_in', 'v_dn_conv_w', 'v_dn_a_log', 'v_dn_dt_bias', 'v_dn_out_norm', 'v_sb_q_norm', 'v_sb_k_norm', 'v_w_branch_a', 'v_w_branch_b', 'v_w_out', 'v_ffn2_norm', 'v_ffn2_w_in', 'v_ffn2_w_out']
TWIN_OUTPUTS = ['loss', 'grad_x', 'grad_ffn1_norm', 'grad_ffn1_w_in', 'grad_ffn1_w_out', 'grad_mix_norm', 'grad_w_in', 'grad_dn_conv_w', 'grad_dn_a_log', 'grad_dn_dt_bias', 'grad_dn_out_norm', 'grad_sb_q_norm', 'grad_sb_k_norm', 'grad_w_branch_a', 'grad_w_branch_b', 'grad_w_out', 'grad_ffn2_norm', 'grad_ffn2_w_in', 'grad_ffn2_w_out', 'delta_ffn1_norm', 'delta_ffn1_w_in', 'delta_ffn1_w_out', 'delta_mix_norm', 'delta_w_in', 'delta_dn_conv_w', 'delta_dn_a_log', 'delta_dn_dt_bias', 'delta_dn_out_norm', 'delta_sb_q_norm', 'delta_sb_k_norm', 'delta_w_branch_a', 'delta_w_branch_b', 'delta_w_out', 'delta_ffn2_norm', 'delta_ffn2_w_in', 'delta_ffn2_w_out', 'new_m_ffn1_norm', 'new_m_ffn1_w_in', 'new_m_ffn1_w_out', 'new_m_mix_norm', 'new_m_w_in', 'new_m_dn_conv_w', 'new_m_dn_a_log', 'new_m_dn_dt_bias', 'new_m_dn_out_norm', 'new_m_sb_q_norm', 'new_m_sb_k_norm', 'new_m_w_branch_a', 'new_m_w_branch_b', 'new_m_w_out', 'new_m_ffn2_norm', 'new_m_ffn2_w_in', 'new_m_ffn2_w_out', 'new_v_ffn1_norm', 'new_v_ffn1_w_in', 'new_v_ffn1_w_out', 'new_v_mix_norm', 'new_v_w_in', 'new_v_dn_conv_w', 'new_v_dn_a_log', 'new_v_dn_dt_bias', 'new_v_dn_out_norm', 'new_v_sb_q_norm', 'new_v_sb_k_norm', 'new_v_w_branch_a', 'new_v_w_branch_b', 'new_v_w_out', 'new_v_ffn2_norm', 'new_v_ffn2_w_in', 'new_v_ffn2_w_out']
TWIN_LEAF_KINDS = {'loss': 'loss', 'grad_x': 'grad_x', 'grad_ffn1_norm': 'grad_w', 'grad_ffn1_w_in': 'grad_w', 'grad_ffn1_w_out': 'grad_w', 'grad_mix_norm': 'grad_w', 'grad_w_in': 'grad_w', 'grad_dn_conv_w': 'grad_w', 'grad_dn_a_log': 'grad_w', 'grad_dn_dt_bias': 'grad_w', 'grad_dn_out_norm': 'grad_w', 'grad_sb_q_norm': 'grad_w', 'grad_sb_k_norm': 'grad_w', 'grad_w_branch_a': 'grad_w', 'grad_w_branch_b': 'grad_w', 'grad_w_out': 'grad_w', 'grad_ffn2_norm': 'grad_w', 'grad_ffn2_w_in': 'grad_w', 'grad_ffn2_w_out': 'grad_w', 'delta_ffn1_norm': 'delta_w', 'delta_ffn1_w_in': 'delta_w', 'delta_ffn1_w_out': 'delta_w', 'delta_mix_norm': 'delta_w', 'delta_w_in': 'delta_w', 'delta_dn_conv_w': 'delta_w', 'delta_dn_a_log': 'delta_w', 'delta_dn_dt_bias': 'delta_w', 'delta_dn_out_norm': 'delta_w', 'delta_sb_q_norm': 'delta_w', 'delta_sb_k_norm': 'delta_w', 'delta_w_branch_a': 'delta_w', 'delta_w_branch_b': 'delta_w', 'delta_w_out': 'delta_w', 'delta_ffn2_norm': 'delta_w', 'delta_ffn2_w_in': 'delta_w', 'delta_ffn2_w_out': 'delta_w', 'new_m_ffn1_norm': 'new_m', 'new_m_ffn1_w_in': 'new_m', 'new_m_ffn1_w_out': 'new_m', 'new_m_mix_norm': 'new_m', 'new_m_w_in': 'new_m', 'new_m_dn_conv_w': 'new_m', 'new_m_dn_a_log': 'new_m', 'new_m_dn_dt_bias': 'new_m', 'new_m_dn_out_norm': 'new_m', 'new_m_sb_q_norm': 'new_m', 'new_m_sb_k_norm': 'new_m', 'new_m_w_branch_a': 'new_m', 'new_m_w_branch_b': 'new_m', 'new_m_w_out': 'new_m', 'new_m_ffn2_norm': 'new_m', 'new_m_ffn2_w_in': 'new_m', 'new_m_ffn2_w_out': 'new_m', 'new_v_ffn1_norm': 'new_v', 'new_v_ffn1_w_in': 'new_v', 'new_v_ffn1_w_out': 'new_v', 'new_v_mix_norm': 'new_v', 'new_v_w_in': 'new_v', 'new_v_dn_conv_w': 'new_v', 'new_v_dn_a_log': 'new_v', 'new_v_dn_dt_bias': 'new_v', 'new_v_dn_out_norm': 'new_v', 'new_v_sb_q_norm': 'new_v', 'new_v_sb_k_norm': 'new_v', 'new_v_w_branch_a': 'new_v', 'new_v_w_branch_b': 'new_v', 'new_v_w_out': 'new_v', 'new_v_ffn2_norm': 'new_v', 'new_v_ffn2_w_in': 'new_v', 'new_v_ffn2_w_out': 'new_v'}


def _forward(args):
    return _fwd_reference(*[args[k] for k in FWD_PARAMS])


def _output_shape():
    out = _jax.eval_shape(lambda: _forward(_fwd_setup_inputs(0)))
    return out.shape, out.dtype

N_MICROBATCH = 1
ADAM_LR = 0.001
ADAM_B1 = 0.9
ADAM_B2 = 0.999
ADAM_EPS = 1e-08
ADAM_WD = 0.01
ADAM_STEP = 10
PER_EXAMPLE_BATCH_AXIS = {'x': 0, 'loss_target': 0}
SHARED_INPUTS = []
_WEIGHT_DTYPES = {'ffn1_norm': _jnp.float32, 'ffn1_w_in': _jnp.float32, 'ffn1_w_out': _jnp.float32, 'mix_norm': _jnp.float32, 'w_in': _jnp.float32, 'dn_conv_w': _jnp.float32, 'dn_a_log': _jnp.float32, 'dn_dt_bias': _jnp.float32, 'dn_out_norm': _jnp.float32, 'sb_q_norm': _jnp.float32, 'sb_k_norm': _jnp.float32, 'w_branch_a': _jnp.float32, 'w_branch_b': _jnp.float32, 'w_out': _jnp.float32, 'ffn2_norm': _jnp.float32, 'ffn2_w_in': _jnp.float32, 'ffn2_w_out': _jnp.float32}
MOMENT_SCALE = {'ffn1_norm': 3.086460e+00, 'ffn1_w_in': 9.412222e-02, 'ffn1_w_out': 1.562593e-01, 'mix_norm': 4.448559e+00, 'w_in': 1.214744e-01, 'dn_conv_w': 1.832050e-01, 'dn_a_log': 4.126222e+00, 'dn_dt_bias': 3.928221e+00, 'dn_out_norm': 1.407419e+01, 'sb_q_norm': 2.265197e+00, 'sb_k_norm': 2.259877e+00, 'w_branch_a': 2.800091e-01, 'w_branch_b': 1.976840e-01, 'w_out': 3.342893e-01, 'ffn2_norm': 3.097391e+00, 'ffn2_w_in': 7.868897e-02, 'ffn2_w_out': 1.309881e-01}


def _to_microbatches(a, axis):
    t = _jnp.moveaxis(a, axis, 0)
    t = t.reshape((N_MICROBATCH, t.shape[0] // N_MICROBATCH) + t.shape[1:])
    return _jnp.moveaxis(t, 1, axis + 1)


def setup_inputs(seed: int = 0) -> dict:
    inp = _fwd_setup_inputs(seed)
    key = _jax.random.fold_in(_jax.random.key(seed), 7919)
    shape, _ = _output_shape()
    out = dict(inp)
    out["loss_target"] = _jax.random.normal(_jax.random.fold_in(key, 0), shape, _jnp.float32)
    for i, name in enumerate(TWIN_WEIGHTS):
        w = inp[name].astype(_jnp.float32)
        if MOMENT_SCALE is None:
            s = _jnp.sqrt(_jnp.mean(_jnp.square(w)) + 1e-30)
        else:
            s = MOMENT_SCALE[name]
        km, kv = _jax.random.split(_jax.random.fold_in(key, i + 1))
        out[name] = w
        out["m_" + name] = s * _jax.random.normal(km, w.shape, _jnp.float32)
        out["v_" + name] = (s * s) * _jax.random.uniform(kv, w.shape, _jnp.float32, 0.5, 1.5)
    if N_MICROBATCH > 1:
        for name, axis in PER_EXAMPLE_BATCH_AXIS.items():
            out[name] = _to_microbatches(out[name], axis)
    return {'x': out['x'], 'ffn1_norm': out['ffn1_norm'], 'ffn1_w_in': out['ffn1_w_in'], 'ffn1_w_out': out['ffn1_w_out'], 'mix_norm': out['mix_norm'], 'w_in': out['w_in'], 'dn_conv_w': out['dn_conv_w'], 'dn_a_log': out['dn_a_log'], 'dn_dt_bias': out['dn_dt_bias'], 'dn_out_norm': out['dn_out_norm'], 'sb_q_norm': out['sb_q_norm'], 'sb_k_norm': out['sb_k_norm'], 'w_branch_a': out['w_branch_a'], 'w_branch_b': out['w_branch_b'], 'w_out': out['w_out'], 'ffn2_norm': out['ffn2_norm'], 'ffn2_w_in': out['ffn2_w_in'], 'ffn2_w_out': out['ffn2_w_out'], 'loss_target': out['loss_target'], 'm_ffn1_norm': out['m_ffn1_norm'], 'm_ffn1_w_in': out['m_ffn1_w_in'], 'm_ffn1_w_out': out['m_ffn1_w_out'], 'm_mix_norm': out['m_mix_norm'], 'm_w_in': out['m_w_in'], 'm_dn_conv_w': out['m_dn_conv_w'], 'm_dn_a_log': out['m_dn_a_log'], 'm_dn_dt_bias': out['m_dn_dt_bias'], 'm_dn_out_norm': out['m_dn_out_norm'], 'm_sb_q_norm': out['m_sb_q_norm'], 'm_sb_k_norm': out['m_sb_k_norm'], 'm_w_branch_a': out['m_w_branch_a'], 'm_w_branch_b': out['m_w_branch_b'], 'm_w_out': out['m_w_out'], 'm_ffn2_norm': out['m_ffn2_norm'], 'm_ffn2_w_in': out['m_ffn2_w_in'], 'm_ffn2_w_out': out['m_ffn2_w_out'], 'v_ffn1_norm': out['v_ffn1_norm'], 'v_ffn1_w_in': out['v_ffn1_w_in'], 'v_ffn1_w_out': out['v_ffn1_w_out'], 'v_mix_norm': out['v_mix_norm'], 'v_w_in': out['v_w_in'], 'v_dn_conv_w': out['v_dn_conv_w'], 'v_dn_a_log': out['v_dn_a_log'], 'v_dn_dt_bias': out['v_dn_dt_bias'], 'v_dn_out_norm': out['v_dn_out_norm'], 'v_sb_q_norm': out['v_sb_q_norm'], 'v_sb_k_norm': out['v_sb_k_norm'], 'v_w_branch_a': out['v_w_branch_a'], 'v_w_branch_b': out['v_w_branch_b'], 'v_w_out': out['v_w_out'], 'v_ffn2_norm': out['v_ffn2_norm'], 'v_ffn2_w_in': out['v_ffn2_w_in'], 'v_ffn2_w_out': out['v_ffn2_w_out']}


def _loss(weights, diff, rest, loss_target):
    with _jax.named_scope("forward"):
        args = {**rest, TWIN_DIFF_INPUT: diff, **{k: w.astype(_WEIGHT_DTYPES[k]) for k, w in weights.items()}}
        y = _forward(args)
    with _jax.named_scope("loss_head"):
        err = _jnp.square(y.astype(_jnp.float32) - loss_target)
        return 0.5 * _jnp.sum(_jnp.mean(err, axis=-1)) if err.ndim else 0.5 * err


def _adamw(w, g, m, v):
    m = ADAM_B1 * m + (1.0 - ADAM_B1) * g
    v = ADAM_B2 * v + (1.0 - ADAM_B2) * _jnp.square(g)
    m_hat = m / (1.0 - ADAM_B1 ** ADAM_STEP)
    v_hat = v / (1.0 - ADAM_B2 ** ADAM_STEP)
    delta = -ADAM_LR * (m_hat / (_jnp.sqrt(v_hat) + ADAM_EPS) + ADAM_WD * w)
    return delta, m, v


def reference(x, ffn1_norm, ffn1_w_in, ffn1_w_out, mix_norm, w_in, dn_conv_w, dn_a_log, dn_dt_bias, dn_out_norm, sb_q_norm, sb_k_norm, w_branch_a, w_branch_b, w_out, ffn2_norm, ffn2_w_in, ffn2_w_out, loss_target, m_ffn1_norm, m_ffn1_w_in, m_ffn1_w_out, m_mix_norm, m_w_in, m_dn_conv_w, m_dn_a_log, m_dn_dt_bias, m_dn_out_norm, m_sb_q_norm, m_sb_k_norm, m_w_branch_a, m_w_branch_b, m_w_out, m_ffn2_norm, m_ffn2_w_in, m_ffn2_w_out, v_ffn1_norm, v_ffn1_w_in, v_ffn1_w_out, v_mix_norm, v_w_in, v_dn_conv_w, v_dn_a_log, v_dn_dt_bias, v_dn_out_norm, v_sb_q_norm, v_sb_k_norm, v_w_branch_a, v_w_branch_b, v_w_out, v_ffn2_norm, v_ffn2_w_in, v_ffn2_w_out):
    given = dict(x=x, ffn1_norm=ffn1_norm, ffn1_w_in=ffn1_w_in, ffn1_w_out=ffn1_w_out, mix_norm=mix_norm, w_in=w_in, dn_conv_w=dn_conv_w, dn_a_log=dn_a_log, dn_dt_bias=dn_dt_bias, dn_out_norm=dn_out_norm, sb_q_norm=sb_q_norm, sb_k_norm=sb_k_norm, w_branch_a=w_branch_a, w_branch_b=w_branch_b, w_out=w_out, ffn2_norm=ffn2_norm, ffn2_w_in=ffn2_w_in, ffn2_w_out=ffn2_w_out, loss_target=loss_target, m_ffn1_norm=m_ffn1_norm, m_ffn1_w_in=m_ffn1_w_in, m_ffn1_w_out=m_ffn1_w_out, m_mix_norm=m_mix_norm, m_w_in=m_w_in, m_dn_conv_w=m_dn_conv_w, m_dn_a_log=m_dn_a_log, m_dn_dt_bias=m_dn_dt_bias, m_dn_out_norm=m_dn_out_norm, m_sb_q_norm=m_sb_q_norm, m_sb_k_norm=m_sb_k_norm, m_w_branch_a=m_w_branch_a, m_w_branch_b=m_w_branch_b, m_w_out=m_w_out, m_ffn2_norm=m_ffn2_norm, m_ffn2_w_in=m_ffn2_w_in, m_ffn2_w_out=m_ffn2_w_out, v_ffn1_norm=v_ffn1_norm, v_ffn1_w_in=v_ffn1_w_in, v_ffn1_w_out=v_ffn1_w_out, v_mix_norm=v_mix_norm, v_w_in=v_w_in, v_dn_conv_w=v_dn_conv_w, v_dn_a_log=v_dn_a_log, v_dn_dt_bias=v_dn_dt_bias, v_dn_out_norm=v_dn_out_norm, v_sb_q_norm=v_sb_q_norm, v_sb_k_norm=v_sb_k_norm, v_w_branch_a=v_w_branch_a, v_w_branch_b=v_w_branch_b, v_w_out=v_w_out, v_ffn2_norm=v_ffn2_norm, v_ffn2_w_in=v_ffn2_w_in, v_ffn2_w_out=v_ffn2_w_out)
    weights = {n: given[n] for n in TWIN_WEIGHTS}
    shared = {n: given[n] for n in SHARED_INPUTS}
    per_example = {n: given[n] for n in ['x']}
    grad_fn = _jax.value_and_grad(_loss, argnums=(0, 1))

    def one_microbatch(ex, loss_target):
        ex = dict(ex)
        diff = ex.pop(TWIN_DIFF_INPUT)
        return grad_fn(weights, diff, {**shared, **ex}, loss_target)

    if N_MICROBATCH == 1:
        loss, (grad_w, grad_x) = one_microbatch(per_example, given["loss_target"])
    else:
        def body(carry, xs):
            loss_sum, grad_sum = carry
            l_k, (gw_k, gx_k) = one_microbatch(xs[0], xs[1])
            with _jax.named_scope("update"):
                return (loss_sum + l_k, _jax.tree.map(_jnp.add, grad_sum, gw_k)), gx_k

        init = (_jnp.zeros((), _jnp.float32), _jax.tree.map(_jnp.zeros_like, weights))
        (loss, grad_w), grad_x = _jax.lax.scan(body, init, (per_example, given["loss_target"]))
    with _jax.named_scope("update"):
        delta_w, new_m, new_v = {}, {}, {}
        for n in TWIN_WEIGHTS:
            delta_w[n], new_m[n], new_v[n] = _adamw(weights[n], grad_w[n], given["m_" + n], given["v_" + n])
    return (loss, grad_x, *[grad_w[n] for n in TWIN_WEIGHTS], *[delta_w[n] for n in TWIN_WEIGHTS],
            *[new_m[n] for n in TWIN_WEIGHTS], *[new_v[n] for n in TWIN_WEIGHTS])
```

```python
import functools
import math

import jax
import jax.numpy as jnp
from jax import lax
from jax.experimental import pallas as pl
from jax.experimental.pallas import tpu as pltpu

F32 = jnp.float32
BF16 = jnp.bfloat16

N_DEV = 8
D_MODEL = 1024
DEPTH = 4
D_FF = 2816
HEADS = 8
HEAD_DIM = 128
DN_CHUNK = 64
DN_CONV = 4
SB_BLOCK = 128
RMS_EPS = 1e-6
L2_EPS = 1e-6
N_IN = 9232
N_MAIN = 9216
N_SCAL = 128
QK_SCALE = HEAD_DIM ** -0.5

ADAM_LR = 0.001
ADAM_B1 = 0.9
ADAM_B2 = 0.999
ADAM_EPS = 1e-08
ADAM_WD = 0.01
ADAM_STEP = 10

V7X_VMEM_LIMIT = 56 * 1024 * 1024
MESH = pl.DeviceIdType.MESH
ANY = pl.BlockSpec(memory_space=pl.ANY)


def _params(sem=None, vmem=V7X_VMEM_LIMIT):
    return pltpu.CompilerParams(dimension_semantics=sem, vmem_limit_bytes=vmem)


def _sigmoid(x):
    return 1.0 / (1.0 + jnp.exp(-x))


def _softplus(x):
    return jnp.maximum(x, 0.0) + jnp.log(1.0 + jnp.exp(-jnp.abs(x)))


def _bdot(a, b, dims=(((1,), (0,)), ((), ()))):
    return lax.dot_general(a.astype(BF16), b.astype(BF16), dims, preferred_element_type=F32)


_NT = (((1,), (1,)), ((), ()))
_TN = (((0,), (0,)), ((), ()))


def _hdot(a, b):
    return jnp.dot(a, b, preferred_element_type=F32, precision=lax.Precision.HIGHEST)


def _hdot_tn(a, b):
    return lax.dot_general(a, b, _TN, preferred_element_type=F32, precision=lax.Precision.HIGHEST)


def _matmul(a, b, *, name, ta=False, tb=False, out_dtype=F32, tm=None, tn=None, tk=None, resid=None, scale=1.0):
    if ta:
        K, M = a.shape
    else:
        M, K = a.shape
    N = b.shape[0] if tb else b.shape[1]
    tm = tm or min(M, 1024)
    tn = tn or min(N, 512)
    tk = tk or K
    assert M % tm == 0 and N % tn == 0 and K % tk == 0, (name, M, N, K, tm, tn, tk)
    nk = K // tk
    dims = (((0 if ta else 1,), (1 if tb else 0,)), ((), ()))

    def body(*refs):
        if resid is not None:
            a_ref, b_ref, r_ref, o_ref = refs[:4]
            scr = refs[4:]
        else:
            a_ref, b_ref, o_ref = refs[:3]
            r_ref = None
            scr = refs[3:]
        part = lax.dot_general(a_ref[...].astype(BF16), b_ref[...].astype(BF16), dims, preferred_element_type=F32)

        def finish(acc):
            if scale != 1.0:
                acc = acc * scale
            if r_ref is not None:
                acc = r_ref[...] + acc
            o_ref[...] = acc.astype(o_ref.dtype)

        if nk == 1:
            finish(part)
        else:
            acc_ref = scr[0]
            k = pl.program_id(2)

            @pl.when(k == 0)
            def _():
                acc_ref[...] = part

            @pl.when(k > 0)
            def _():
                acc_ref[...] += part

            @pl.when(k == nk - 1)
            def _():
                finish(acc_ref[...])

    a_spec = pl.BlockSpec((tk, tm), lambda i, j, k: (k, i)) if ta else pl.BlockSpec((tm, tk), lambda i, j, k: (i, k))
    b_spec = pl.BlockSpec((tn, tk), lambda i, j, k: (j, k)) if tb else pl.BlockSpec((tk, tn), lambda i, j, k: (k, j))
    o_spec = pl.BlockSpec((tm, tn), lambda i, j, k: (i, j))
    in_specs = [a_spec, b_spec] + ([o_spec] if resid is not None else [])
    args = (a, b) + ((resid,) if resid is not None else ())
    return pl.pallas_call(
        body, name=name, grid=(M // tm, N // tn, nk), in_specs=in_specs, out_specs=o_spec,
        out_shape=jax.ShapeDtypeStruct((M, N), out_dtype),
        scratch_shapes=[pltpu.VMEM((tm, tn), F32)] if nk > 1 else [],
        compiler_params=_params(("parallel", "parallel", "arbitrary")),
    )(*args)


ROW_TILE = 256


def _rmsnorm_fwd(x, gain, *, name):
    T, D = x.shape

    def body(x_ref, g_ref, o_ref):
        xf = x_ref[...]
        r = lax.rsqrt(jnp.mean(xf * xf, axis=-1, keepdims=True) + RMS_EPS)
        o_ref[...] = (xf * r * g_ref[...]).astype(o_ref.dtype)

    return pl.pallas_call(
        body, name=name, grid=(T // ROW_TILE,),
        in_specs=[pl.BlockSpec((ROW_TILE, D), lambda i: (i, 0)), pl.BlockSpec((1, D), lambda i: (0, 0))],
        out_specs=pl.BlockSpec((ROW_TILE, D), lambda i: (i, 0)),
        out_shape=jax.ShapeDtypeStruct((T, D), BF16), compiler_params=_params(("parallel",)),
    )(x, gain)


def _rmsnorm_bwd(dh, x, gain, dres, *, name):
    T, D = x.shape

    def body(dh_ref, x_ref, g_ref, res_ref, dx_ref, dg_ref):
        xf = x_ref[...]
        r = lax.rsqrt(jnp.mean(xf * xf, axis=-1, keepdims=True) + RMS_EPS)
        y = xf * r
        dh_v = dh_ref[...].astype(F32)
        dy = dh_v * g_ref[...]
        dx_ref[...] = res_ref[...] + r * (dy - y * jnp.mean(dy * y, axis=-1, keepdims=True))

        @pl.when(pl.program_id(0) == 0)
        def _():
            dg_ref[...] = jnp.zeros_like(dg_ref)

        dg_ref[...] += jnp.sum(dh_v * y, axis=0, keepdims=True)

    row = pl.BlockSpec((ROW_TILE, D), lambda i: (i, 0))
    vec = pl.BlockSpec((1, D), lambda i: (0, 0))
    return pl.pallas_call(
        body, name=name, grid=(T // ROW_TILE,), in_specs=[row, row, vec, row], out_specs=(row, vec),
        out_shape=(jax.ShapeDtypeStruct((T, D), F32), jax.ShapeDtypeStruct((1, D), F32)),
        compiler_params=_params(("arbitrary",)),
    )(dh, x, gain, dres)


SWIGLU_COLS = D_FF // 2


def _swiglu_fwd(p, *, name):
    T = p.shape[0]
    nf = D_FF // SWIGLU_COLS

    def body(g_ref, u_ref, o_ref):
        g = g_ref[...].astype(F32)
        o_ref[...] = (g * _sigmoid(g) * u_ref[...].astype(F32)).astype(o_ref.dtype)

    return pl.pallas_call(
        body, name=name, grid=(T // ROW_TILE, nf),
        in_specs=[pl.BlockSpec((ROW_TILE, SWIGLU_COLS), lambda i, j: (i, j)),
                  pl.BlockSpec((ROW_TILE, SWIGLU_COLS), lambda i, j: (i, j + nf))],
        out_specs=pl.BlockSpec((ROW_TILE, SWIGLU_COLS), lambda i, j: (i, j)),
        out_shape=jax.ShapeDtypeStruct((T, D_FF), BF16), compiler_params=_params(("parallel", "parallel")),
    )(p, p)


def _swiglu_bwd(da, p, *, name):
    T = p.shape[0]
    nf = D_FF // SWIGLU_COLS

    def body(da_ref, g_ref, u_ref, o_ref):
        g = g_ref[...].astype(F32)
        u = u_ref[...].astype(F32)
        d = da_ref[...].astype(F32)
        s = _sigmoid(g)
        dgate = d * u * (s * (1.0 + g * (1.0 - s)))
        dup = d * g * s
        o_ref[...] = jnp.where(pl.program_id(1) < nf, dgate, dup).astype(o_ref.dtype)

    blk = (ROW_TILE, SWIGLU_COLS)
    return pl.pallas_call(
        body, name=name, grid=(T // ROW_TILE, 2 * nf),
        in_specs=[pl.BlockSpec(blk, lambda i, j: (i, j % nf)), pl.BlockSpec(blk, lambda i, j: (i, j % nf)),
                  pl.BlockSpec(blk, lambda i, j: (i, j % nf + nf))],
        out_specs=pl.BlockSpec(blk, lambda i, j: (i, j)),
        out_shape=jax.ShapeDtypeStruct((T, 2 * D_FF), BF16), compiler_params=_params(("parallel", "parallel")),
    )(da, p, p)


COL_GATE_A = 7
COL_GATE_B = 8


def _merge_fwd(ya, yb, proj, *, name):
    T, D = ya.shape

    def body(ya_ref, yb_ref, ga_ref, gb_ref, o_ref):
        o_ref[...] = (_sigmoid(ga_ref[...]) * ya_ref[...] + _sigmoid(gb_ref[...]) * yb_ref[...]).astype(o_ref.dtype)

    row = pl.BlockSpec((ROW_TILE, D), lambda i: (i, 0))
    return pl.pallas_call(
        body, name=name, grid=(T // ROW_TILE,),
        in_specs=[row, row, pl.BlockSpec((ROW_TILE, D), lambda i: (i, COL_GATE_A)),
                  pl.BlockSpec((ROW_TILE, D), lambda i: (i, COL_GATE_B))],
        out_specs=row, out_shape=jax.ShapeDtypeStruct((T, D), BF16), compiler_params=_params(("parallel",)),
    )(ya, yb, proj, proj)


def _merge_bwd(dm, ya, yb, proj, *, name):
    T, D = ya.shape

    def body(dm_ref, ya_ref, yb_ref, ga_ref, gb_ref, dya_ref, dyb_ref, dga_ref, dgb_ref):
        d = dm_ref[...].astype(F32)
        sa = _sigmoid(ga_ref[...])
        sb = _sigmoid(gb_ref[...])
        dya_ref[...] = (d * sa).astype(BF16)
        dyb_ref[...] = (d * sb).astype(BF16)
        dga_ref[...] = (d * ya_ref[...] * sa * (1.0 - sa)).astype(BF16)
        dgb_ref[...] = (d * yb_ref[...] * sb * (1.0 - sb)).astype(BF16)

    row = pl.BlockSpec((ROW_TILE, D), lambda i: (i, 0))
    out = jax.ShapeDtypeStruct((T, D), BF16)
    return pl.pallas_call(
        body, name=name, grid=(T // ROW_TILE,),
        in_specs=[row, row, row, pl.BlockSpec((ROW_TILE, D), lambda i: (i, COL_GATE_A)),
                  pl.BlockSpec((ROW_TILE, D), lambda i: (i, COL_GATE_B))],
        out_specs=(row, row, row, row), out_shape=(out, out, out, out), compiler_params=_params(("parallel",)),
    )(dm, ya, yb, proj, proj)


def _loss_head(y, target, *, name):
    T, D = y.shape

    def body(y_ref, t_ref, loss_ref, dy_ref):
        err = y_ref[...] - t_ref[...]
        dy_ref[...] = err * (1.0 / D)

        @pl.when(pl.program_id(0) == 0)
        def _():
            loss_ref[...] = jnp.zeros_like(loss_ref)

        loss_ref[...] += 0.5 * jnp.sum(jnp.sum(err * err, axis=-1, keepdims=True) * (1.0 / D), axis=0, keepdims=True)

    row = pl.BlockSpec((ROW_TILE, D), lambda i: (i, 0))
    return pl.pallas_call(
        body, name=name, grid=(T // ROW_TILE,), in_specs=[row, row],
        out_specs=(pl.BlockSpec((1, 128), lambda i: (0, 0)), row),
        out_shape=(jax.ShapeDtypeStruct((1, 128), F32), jax.ShapeDtypeStruct((T, D), F32)),
        compiler_params=_params(("arbitrary",)),
    )(y, target)


CONV_PAD = 8


def _conv_taps(w, xp, T, first):
    acc = w[0:1, :] * xp[pl.ds(first, T), :]
    for i in range(1, DN_CONV):
        acc = acc + w[i:i + 1, :] * xp[pl.ds(first + i, T), :]
    return acc


def _conv_fwd(proj, conv_w, *, name):
    T = proj.shape[0]

    def body(x_ref, w_ref, o_ref, xp):
        xp[0:CONV_PAD, :] = jnp.zeros((CONV_PAD, HEAD_DIM), F32)
        xp[CONV_PAD:, :] = x_ref[...]
        y = _conv_taps(w_ref[...], xp, T, CONV_PAD - (DN_CONV - 1))
        s = y * _sigmoid(y)
        n = s * lax.rsqrt(jnp.sum(s * s, axis=-1, keepdims=True) + L2_EPS)
        o_ref[0] = jnp.where(pl.program_id(0) < 2, n, s)

    return pl.pallas_call(
        body, name=name, grid=(3, HEADS),
        in_specs=[pl.BlockSpec((T, HEAD_DIM), lambda c, h: (0, c * HEADS + h)),
                  pl.BlockSpec((DN_CONV, HEAD_DIM), lambda c, h: (0, c * HEADS + h))],
        out_specs=pl.BlockSpec((1, T, HEAD_DIM), lambda c, h: (c, 0, h)),
        out_shape=jax.ShapeDtypeStruct((3, T, D_MODEL), F32),
        scratch_shapes=[pltpu.VMEM((T + CONV_PAD, HEAD_DIM), F32)],
        compiler_params=_params(("parallel", "parallel")),
    )(proj, conv_w)


def _conv_bwd(dqkv, proj, conv_w, *, name):
    T = proj.shape[0]

    def body(d_ref, x_ref, w_ref, dx_ref, dw_ref, xp, dyp):
        xp[0:CONV_PAD, :] = jnp.zeros((CONV_PAD, HEAD_DIM), F32)
        xp[CONV_PAD:, :] = x_ref[...]
        w = w_ref[...]
        y = _conv_taps(w, xp, T, CONV_PAD - (DN_CONV - 1))
        sg = _sigmoid(y)
        s = y * sg
        r = lax.rsqrt(jnp.sum(s * s, axis=-1, keepdims=True) + L2_EPS)
        n = s * r
        d = d_ref[0]
        ds = jnp.where(pl.program_id(0) < 2, r * (d - n * jnp.sum(d * n, axis=-1, keepdims=True)), d)
        dy = ds * (sg * (1.0 + y * (1.0 - sg)))
        dyp[0:T, :] = dy
        dyp[T:, :] = jnp.zeros((CONV_PAD, HEAD_DIM), F32)
        dx = w[0:1, :] * dyp[pl.ds(DN_CONV - 1, T), :]
        for i in range(1, DN_CONV):
            dx = dx + w[i:i + 1, :] * dyp[pl.ds(DN_CONV - 1 - i, T), :]
        dx_ref[...] = dx.astype(dx_ref.dtype)
        for i in range(DN_CONV):
            dw_ref[i:i + 1, :] = jnp.sum(dy * xp[pl.ds(CONV_PAD - (DN_CONV - 1) + i, T), :], axis=0, keepdims=True)

    col = lambda c, h: (0, c * HEADS + h)
    return pl.pallas_call(
        body, name=name, grid=(3, HEADS),
        in_specs=[pl.BlockSpec((1, T, HEAD_DIM), lambda c, h: (c, 0, h)), pl.BlockSpec((T, HEAD_DIM), col),
                  pl.BlockSpec((DN_CONV, HEAD_DIM), col)],
        out_specs=(pl.BlockSpec((T, HEAD_DIM), col), pl.BlockSpec((DN_CONV, HEAD_DIM), col)),
        out_shape=(jax.ShapeDtypeStruct((T, 3 * D_MODEL), BF16), jax.ShapeDtypeStruct((DN_CONV, 3 * D_MODEL), F32)),
        scratch_shapes=[pltpu.VMEM((T + CONV_PAD, HEAD_DIM), F32), pltpu.VMEM((T + CONV_PAD, HEAD_DIM), F32)],
        compiler_params=_params(("parallel", "parallel")),
    )(dqkv, proj, conv_w)


def _inv_unit_lower(low, eye):
    x = eye - low
    power = _hdot(low, low)
    steps = int(math.log2(DN_CHUNK)) - 1
    for s in range(steps):
        x = x + _hdot(x, power)
        if s + 1 < steps:
            power = _hdot(power, power)
    return x


def _dn_chunk_setup(q_ref, k_ref, v_ref, b_ref, a_ref, hp_ref, n):
    C = DN_CHUNK
    r0 = pl.multiple_of(n * C, C)
    q = q_ref[0, pl.ds(r0, C), :] * QK_SCALE
    k = k_ref[1, pl.ds(r0, C), :]
    v = v_ref[2, pl.ds(r0, C), :]
    ii = lax.broadcasted_iota(jnp.int32, (C, C), 0)
    jj = lax.broadcasted_iota(jnp.int32, (C, C), 1)
    eye_mask = ii == jj
    eye = jnp.where(eye_mask, 1.0, 0.0).astype(F32)

    def to_col(row):
        return jnp.sum(jnp.where(eye_mask, jnp.broadcast_to(row, (C, C)), 0.0), axis=1, keepdims=True)

    def to_row(col):
        return jnp.sum(jnp.where(eye_mask, jnp.broadcast_to(col, (C, C)), 0.0), axis=0, keepdims=True)

    b_row = b_ref[0, n]
    a_row = a_ref[0, n]
    a_log = hp_ref[0, 0:1, 0:C]
    dt_b = hp_ref[0, 1:2, 0:C]
    beta_row = _sigmoid(b_row)
    neg_ea = -jnp.exp(a_log)
    g_row = neg_ea * _softplus(a_row + dt_b)
    gc_col = jnp.sum(jnp.where(jj <= ii, jnp.broadcast_to(g_row, (C, C)), 0.0), axis=1, keepdims=True)
    gc_row = to_row(gc_col)
    g_last = jnp.sum(g_row, axis=1, keepdims=True)
    beta = to_col(beta_row)
    low_incl = ii >= jj
    decay = jnp.exp(jnp.where(low_incl, gc_col - gc_row, -jnp.inf))
    eg = jnp.exp(gc_col)
    egl = jnp.exp(g_last - gc_col)
    el = jnp.exp(g_last)
    kb = k * beta
    pmat = _bdot(kb, k, _NT)
    low = jnp.where(ii > jj, pmat * decay, 0.0)
    tinv = _inv_unit_lower(low, eye)
    u = _hdot(tinv, v * beta)
    w = _hdot(tinv, kb * eg)
    qk = _bdot(q, k, _NT)
    attn = qk * decay
    return dict(q=q, k=k, v=v, ii=ii, jj=jj, to_col=to_col, to_row=to_row, b_row=b_row, a_row=a_row, dt_b=dt_b,
                beta_row=beta_row, neg_ea=neg_ea, g_row=g_row, gc_col=gc_col, g_last=g_last, beta=beta,
                decay=decay, eg=eg, egl=egl, el=el, kb=kb, pmat=pmat, tinv=tinv, u=u, w=w, qk=qk, attn=attn,
                qd=q * eg, kd=k * egl, r0=r0)


def _dn_specs(T):
    nc = T // DN_CHUNK
    qkv = pl.BlockSpec((3, T, HEAD_DIM), lambda h: (0, 0, h))
    rows = pl.BlockSpec((1, nc, 1, DN_CHUNK), lambda h: (h, 0, 0, 0))
    hp = pl.BlockSpec((1, 8, 128), lambda h: (h, 0, 0))
    states = pl.BlockSpec((1, nc, HEAD_DIM, HEAD_DIM), lambda h: (h, 0, 0, 0))
    return nc, qkv, rows, hp, states


def _dn_fwd(qkv, b_rows, a_rows, hp, *, name):
    T = qkv.shape[1]
    nc, qkv_spec, row_spec, hp_spec, st_spec = _dn_specs(T)

    def body(qkv_ref, b_ref, a_ref, hp_ref, o_ref, st_ref, s_scr):
        s_scr[...] = jnp.zeros_like(s_scr)

        def step(n, carry):
            c = _dn_chunk_setup(qkv_ref, qkv_ref, qkv_ref, b_ref, a_ref, hp_ref, n)
            state = s_scr[...]
            st_ref[0, n] = state
            v_new = c["u"] - _bdot(c["w"], state)
            o_ref[pl.ds(c["r0"], DN_CHUNK), :] = _bdot(c["qd"], state) + _bdot(c["attn"], v_new)
            s_scr[...] = state * c["el"] + _bdot(c["kd"], v_new, _TN)
            return carry

        lax.fori_loop(0, nc, step, 0)

    return pl.pallas_call(
        body, name=name, grid=(HEADS,), in_specs=[qkv_spec, row_spec, row_spec, hp_spec],
        out_specs=(pl.BlockSpec((T, HEAD_DIM), lambda h: (0, h)), st_spec),
        out_shape=(jax.ShapeDtypeStruct((T, D_MODEL), F32),
                   jax.ShapeDtypeStruct((HEADS, nc, HEAD_DIM, HEAD_DIM), F32)),
        scratch_shapes=[pltpu.VMEM((HEAD_DIM, HEAD_DIM), F32)], compiler_params=_params(("parallel",)),
    )(qkv, b_rows, a_rows, hp)


def _dn_bwd(qkv, b_rows, a_rows, hp, states, do, *, name):
    T = qkv.shape[1]
    C = DN_CHUNK
    nc, qkv_spec, row_spec, hp_spec, st_spec = _dn_specs(T)

    def body(qkv_ref, b_ref, a_ref, hp_ref, st_ref, do_ref, dqkv_ref, db_ref, da_ref, dhp_ref, ds_scr, acc_scr):
        ds_scr[...] = jnp.zeros_like(ds_scr)
        acc_scr[...] = jnp.zeros_like(acc_scr)

        def step(t, carry):
            n = nc - 1 - t
            c = _dn_chunk_setup(qkv_ref, qkv_ref, qkv_ref, b_ref, a_ref, hp_ref, n)
            ii, jj = c["ii"], c["jj"]
            q, k, v, kb, beta = c["q"], c["k"], c["v"], c["kb"], c["beta"]
            decay, eg, egl, el = c["decay"], c["eg"], c["egl"], c["el"]
            u, w, tinv, attn, qd, kd = c["u"], c["w"], c["tinv"], c["attn"], c["qd"], c["kd"]
            state = st_ref[0, n]
            d_o = do_ref[pl.ds(c["r0"], C), :]
            d_state = ds_scr[...]
            v_new = u - _bdot(w, state)
            d_vnew = _bdot(attn, d_o, _TN) + _bdot(kd, d_state)
            d_qd = _bdot(d_o, state, _NT)
            d_attn = _bdot(d_o, v_new, _NT)
            d_kd = _bdot(v_new, d_state, _NT)
            d_el = jnp.sum(jnp.sum(d_state * state, axis=1, keepdims=True), axis=0, keepdims=True)
            d_w = -_bdot(d_vnew, state, _NT)
            ds_scr[...] = d_state * el + _bdot(qd, d_o, _TN) - _bdot(w, d_vnew, _TN)
            d_rv = _hdot_tn(tinv, d_vnew)
            d_rw = _hdot_tn(tinv, d_w)
            d_amat = -(_bdot(d_rv, u, _NT) + _bdot(d_rw, w, _NT))
            d_low = jnp.where(ii > jj, d_amat, 0.0)
            d_p = d_low * decay
            d_qk = d_attn * decay
            e_mat = (d_low * c["pmat"] + d_attn * c["qk"]) * decay
            d_q = _bdot(d_qk, k) + d_qd * eg
            d_kb = _bdot(d_p, k) + d_rw * eg
            d_k = _bdot(d_qk, q, _TN) + _bdot(d_p, kb, _TN) + d_kd * egl + d_kb * beta
            d_beta = jnp.sum(d_kb * k, axis=1, keepdims=True) + jnp.sum(d_rv * v, axis=1, keepdims=True)
            d_v = d_rv * beta
            d_eg = jnp.sum(d_qd * q, axis=1, keepdims=True) + jnp.sum(d_rw * kb, axis=1, keepdims=True)
            d_egl = jnp.sum(d_kd * k, axis=1, keepdims=True)
            d_glast = jnp.sum(d_egl * egl, axis=0, keepdims=True) + d_el * el
            row_sum = jnp.sum(e_mat, axis=1, keepdims=True)
            col_sum = c["to_col"](jnp.sum(e_mat, axis=0, keepdims=True))
            d_gc = row_sum - col_sum + d_eg * eg - d_egl * egl
            d_g_row = jnp.sum(jnp.where(ii >= jj, jnp.broadcast_to(d_gc, (C, C)), 0.0), axis=0, keepdims=True) + d_glast
            beta_row = c["beta_row"]
            d_b_row = c["to_row"](d_beta) * beta_row * (1.0 - beta_row)
            d_a_row = d_g_row * c["neg_ea"] * _sigmoid(c["a_row"] + c["dt_b"])
            dqkv_ref[0, pl.ds(c["r0"], C), :] = d_q * QK_SCALE
            dqkv_ref[1, pl.ds(c["r0"], C), :] = d_k
            dqkv_ref[2, pl.ds(c["r0"], C), :] = d_v
            db_ref[0, n] = d_b_row
            da_ref[0, n] = d_a_row
            acc_scr[0:1, 0:C] += d_g_row * c["g_row"]
            acc_scr[1:2, 0:C] += d_a_row
            return carry

        lax.fori_loop(0, nc, step, 0)
        tot = jnp.sum(acc_scr[...], axis=1, keepdims=True)
        dhp_ref[0] = jnp.broadcast_to(tot, (8, 128))

    return pl.pallas_call(
        body, name=name, grid=(HEADS,),
        in_specs=[qkv_spec, row_spec, row_spec, hp_spec, st_spec, pl.BlockSpec((T, HEAD_DIM), lambda h: (0, h))],
        out_specs=(qkv_spec, row_spec, row_spec, hp_spec),
        out_shape=(jax.ShapeDtypeStruct((3, T, D_MODEL), F32), jax.ShapeDtypeStruct((HEADS, nc, 1, C), F32),
                   jax.ShapeDtypeStruct((HEADS, nc, 1, C), F32), jax.ShapeDtypeStruct((HEADS, 8, 128), F32)),
        scratch_shapes=[pltpu.VMEM((HEAD_DIM, HEAD_DIM), F32), pltpu.VMEM((8, 128), F32)],
        compiler_params=_params(("parallel",)),
    )(qkv, b_rows, a_rows, hp, states, do)


COL_Z = 3 * HEADS


def _gated_norm_fwd(o, proj, gain, *, name):
    T = o.shape[0]

    def body(o_ref, z_ref, g_ref, out_ref):
        x = o_ref[...]
        r = lax.rsqrt(jnp.mean(x * x, axis=-1, keepdims=True) + RMS_EPS)
        z = z_ref[...]
        out_ref[...] = (x * r * g_ref[...] * (z * _sigmoid(z))).astype(out_ref.dtype)

    return pl.pallas_call(
        body, name=name, grid=(HEADS,),
        in_specs=[pl.BlockSpec((T, HEAD_DIM), lambda h: (0, h)), pl.BlockSpec((T, HEAD_DIM), lambda h: (0, COL_Z + h)),
                  pl.BlockSpec((1, HEAD_DIM), lambda h: (0, 0))],
        out_specs=pl.BlockSpec((T, HEAD_DIM), lambda h: (0, h)),
        out_shape=jax.ShapeDtypeStruct((T, D_MODEL), BF16), compiler_params=_params(("parallel",)),
    )(o, proj, gain)


def _gated_norm_bwd(dout, o, proj, gain, *, name):
    T = o.shape[0]

    def body(d_ref, o_ref, z_ref, g_ref, do_ref, dz_ref, dg_ref):
        x = o_ref[...]
        r = lax.rsqrt(jnp.mean(x * x, axis=-1, keepdims=True) + RMS_EPS)
        n = x * r
        z = z_ref[...]
        sg = _sigmoid(z)
        d = d_ref[...].astype(F32)
        g = g_ref[...]
        dz_ref[...] = (d * n * g * (sg * (1.0 + z * (1.0 - sg)))).astype(dz_ref.dtype)
        dy = d * (z * sg)
        dyg = dy * g
        do_ref[...] = r * (dyg - n * jnp.mean(dyg * n, axis=-1, keepdims=True))

        @pl.when(pl.program_id(0) == 0)
        def _():
            dg_ref[...] = jnp.zeros_like(dg_ref)

        dg_ref[...] += jnp.sum(dy * n, axis=0, keepdims=True)

    head = pl.BlockSpec((T, HEAD_DIM), lambda h: (0, h))
    vec = pl.BlockSpec((1, HEAD_DIM), lambda h: (0, 0))
    return pl.pallas_call(
        body, name=name, grid=(HEADS,),
        in_specs=[head, head, pl.BlockSpec((T, HEAD_DIM), lambda h: (0, COL_Z + h)), vec],
        out_specs=(head, head, vec),
        out_shape=(jax.ShapeDtypeStruct((T, D_MODEL), F32), jax.ShapeDtypeStruct((T, D_MODEL), BF16),
                   jax.ShapeDtypeStruct((1, HEAD_DIM), F32)),
        compiler_params=_params(("arbitrary",)),
    )(dout, o, proj, gain)


COL_SBQ = 4 * HEADS
COL_SBK = 5 * HEADS
COL_SBV = 6 * HEADS


def _split_dot(x, mat):
    hi = x.astype(BF16)
    lo = (x - hi.astype(F32)).astype(BF16)
    return jnp.dot(hi, mat, preferred_element_type=F32) + jnp.dot(lo, mat, preferred_element_type=F32)


def _sb_specs(T):
    return (pl.BlockSpec((T, HEAD_DIM), lambda h: (0, COL_SBQ + h)), pl.BlockSpec((T, HEAD_DIM), lambda h: (0, COL_SBK + h)),
            pl.BlockSpec((T, HEAD_DIM), lambda h: (0, COL_SBV + h)), pl.BlockSpec((1, HEAD_DIM), lambda h: (0, 0)))


def _head_rms(x, gain):
    r = lax.rsqrt(jnp.mean(x * x, axis=-1, keepdims=True) + RMS_EPS)
    return x * r, r


def _sb_fwd(proj, q_gain, k_gain, *, name):
    T = proj.shape[0]
    B = SB_BLOCK
    nb = T // B
    q_spec, k_spec, v_spec, g_spec = _sb_specs(T)

    def body(q_ref, k_ref, v_ref, gq_ref, gk_ref, o_ref, lt_ref, qs, ks, vs):
        qs[...] = (_head_rms(q_ref[...], None)[0] * gq_ref[...]).astype(BF16)
        ks[...] = (_head_rms(k_ref[...], None)[0] * gk_ref[...]).astype(BF16)
        vs[...] = v_ref[...].astype(BF16)
        ii = lax.broadcasted_iota(jnp.int32, (B, B), 0)
        jj = lax.broadcasted_iota(jnp.int32, (B, B), 1)
        after = jnp.where(ii > jj, 1.0, 0.0).astype(BF16)

        def q_block(i, carry):
            q = qs[pl.ds(pl.multiple_of(i * B, B), B), :]

            def k_block(step, inner):
                acc, tail = inner
                j = i - step
                c0 = pl.multiple_of(j * B, B)
                z = lax.dot_general(q, ks[pl.ds(c0, B), :], _NT, preferred_element_type=F32) * QK_SCALE
                causal = (j * B + jj) < (i * B + ii)
                sp = _softplus(z)
                log_1mb = jnp.where(causal, -sp, 0.0)
                survive = _split_dot(log_1mb, after) + tail
                wts = jnp.where(causal, jnp.exp(z - sp + survive), 0.0)
                acc = acc + jnp.dot(wts.astype(BF16), vs[pl.ds(c0, B), :], preferred_element_type=F32)
                return acc, tail + jnp.sum(log_1mb, axis=1, keepdims=True)

            acc, tail = lax.fori_loop(0, i + 1, k_block, (jnp.zeros((B, HEAD_DIM), F32), jnp.zeros((B, 1), F32)))
            rows = pl.ds(pl.multiple_of(i * B, B), B)
            o_ref[rows, :] = acc.astype(o_ref.dtype)
            lt_ref[rows, :] = jnp.broadcast_to(tail, (B, HEAD_DIM))
            return carry

        lax.fori_loop(0, nb, q_block, 0)

    head = pl.BlockSpec((T, HEAD_DIM), lambda h: (0, h))
    return pl.pallas_call(
        body, name=name, grid=(HEADS,), in_specs=[q_spec, k_spec, v_spec, g_spec, g_spec],
        out_specs=(head, head),
        out_shape=(jax.ShapeDtypeStruct((T, D_MODEL), BF16), jax.ShapeDtypeStruct((T, D_MODEL), F32)),
        scratch_shapes=[pltpu.VMEM((T, HEAD_DIM), BF16)] * 3, compiler_params=_params(("parallel",)),
    )(proj, proj, proj, q_gain, k_gain)


def _sb_bwd(proj, q_gain, k_gain, ltot, do, *, name):
    T = proj.shape[0]
    B = SB_BLOCK
    nb = T // B
    q_spec, k_spec, v_spec, g_spec = _sb_specs(T)

    def body(q_ref, k_ref, v_ref, gq_ref, gk_ref, lt_ref, do_ref, dq_ref, dk_ref, dv_ref, dgq_ref, dgk_ref,
             qs, ks, vs, dos, dq_acc, dk_acc, dv_acc):
        qn, q_r = _head_rms(q_ref[...], None)
        kn, k_r = _head_rms(k_ref[...], None)
        qs[...] = (qn * gq_ref[...]).astype(BF16)
        ks[...] = (kn * gk_ref[...]).astype(BF16)
        vs[...] = v_ref[...].astype(BF16)
        dos[...] = do_ref[...].astype(BF16)
        dk_acc[...] = jnp.zeros_like(dk_acc)
        dv_acc[...] = jnp.zeros_like(dv_acc)
        ii = lax.broadcasted_iota(jnp.int32, (B, B), 0)
        jj = lax.broadcasted_iota(jnp.int32, (B, B), 1)
        upto = jnp.where(ii <= jj, 1.0, 0.0).astype(BF16)
        before = jnp.where(ii < jj, 1.0, 0.0).astype(BF16)

        def q_block(i, carry):
            rows = pl.ds(pl.multiple_of(i * B, B), B)
            q = qs[rows, :]
            d_o = dos[rows, :]
            total = jnp.max(lt_ref[rows, :], axis=1, keepdims=True)

            def k_block(j, inner):
                dq, head_lb, head_de = inner
                cols = pl.ds(pl.multiple_of(j * B, B), B)
                k = ks[cols, :]
                v = vs[cols, :]
                z = lax.dot_general(q, k, _NT, preferred_element_type=F32) * QK_SCALE
                causal = (j * B + jj) < (i * B + ii)
                sp = _softplus(z)
                log_1mb = jnp.where(causal, -sp, 0.0)
                prefix = _split_dot(log_1mb, upto) + head_lb
                wts = jnp.where(causal, jnp.exp(z - sp + (total - prefix)), 0.0)
                d_w = lax.dot_general(d_o, v, _NT, preferred_element_type=F32)
                d_e = wts * d_w
                cum = _split_dot(d_e, before) + head_de
                sig = jnp.exp(z - sp)
                d_z = jnp.where(causal, d_e * (1.0 - sig) - sig * cum, 0.0) * QK_SCALE
                d_zb = d_z.astype(BF16)
                dq = dq + jnp.dot(d_zb, k, preferred_element_type=F32)
                dk_acc[cols, :] += lax.dot_general(d_zb, q, _TN, preferred_element_type=F32)
                dv_acc[cols, :] += lax.dot_general(wts.astype(BF16), d_o, _TN, preferred_element_type=F32)
                return (dq, head_lb + jnp.sum(log_1mb, axis=1, keepdims=True),
                        head_de + jnp.sum(d_e, axis=1, keepdims=True))

            zero = jnp.zeros((B, 1), F32)
            dq, _, _ = lax.fori_loop(0, i + 1, k_block, (jnp.zeros((B, HEAD_DIM), F32), zero, zero))
            dq_acc[rows, :] = dq
            return carry

        lax.fori_loop(0, nb, q_block, 0)

        def norm_bwd(d_scaled, n, r, gain):
            dn = d_scaled * gain
            return r * (dn - n * jnp.mean(dn * n, axis=-1, keepdims=True)), jnp.sum(d_scaled * n, axis=0, keepdims=True)

        dq_raw, dgq = norm_bwd(dq_acc[...], qn, q_r, gq_ref[...])
        dk_raw, dgk = norm_bwd(dk_acc[...], kn, k_r, gk_ref[...])
        dq_ref[...] = dq_raw.astype(dq_ref.dtype)
        dk_ref[...] = dk_raw.astype(dk_ref.dtype)
        dv_ref[...] = dv_acc[...].astype(dv_ref.dtype)

        @pl.when(pl.program_id(0) == 0)
        def _():
            dgq_ref[...] = jnp.zeros_like(dgq_ref)
            dgk_ref[...] = jnp.zeros_like(dgk_ref)

        dgq_ref[...] += dgq
        dgk_ref[...] += dgk

    head = pl.BlockSpec((T, HEAD_DIM), lambda h: (0, h))
    out = jax.ShapeDtypeStruct((T, D_MODEL), BF16)
    vec = jax.ShapeDtypeStruct((1, HEAD_DIM), F32)
    return pl.pallas_call(
        body, name=name, grid=(HEADS,), in_specs=[q_spec, k_spec, v_spec, g_spec, g_spec, head, head],
        out_specs=(head, head, head, g_spec, g_spec), out_shape=(out, out, out, vec, vec),
        scratch_shapes=[pltpu.VMEM((T, HEAD_DIM), BF16)] * 4 + [pltpu.VMEM((T, HEAD_DIM), F32)] * 3,
        compiler_params=_params(("arbitrary",)),
    )(proj, proj, proj, q_gain, k_gain, ltot, do)


ADAM_ROWS = 128


def _adamw(g_parts, w, m, v, *, name):
    K, R, C = g_parts.shape
    tr = ADAM_ROWS if R % ADAM_ROWS == 0 else R

    def body(g_ref, w_ref, m_ref, v_ref, go_ref, d_ref, mo_ref, vo_ref):
        g = g_ref[0].astype(F32)
        for k in range(1, K):
            g = g + g_ref[k].astype(F32)
        m_new = ADAM_B1 * m_ref[...] + (1.0 - ADAM_B1) * g
        v_new = ADAM_B2 * v_ref[...] + (1.0 - ADAM_B2) * (g * g)
        m_hat = m_new / (1.0 - ADAM_B1 ** ADAM_STEP)
        v_hat = v_new / (1.0 - ADAM_B2 ** ADAM_STEP)
        go_ref[...] = g
        d_ref[...] = -ADAM_LR * (m_hat / (jnp.sqrt(v_hat) + ADAM_EPS) + ADAM_WD * w_ref[...])
        mo_ref[...] = m_new
        vo_ref[...] = v_new

    row = pl.BlockSpec((tr, C), lambda i: (i, 0))
    out = jax.ShapeDtypeStruct((R, C), F32)
    return pl.pallas_call(
        body, name=name, grid=(R // tr,), in_specs=[pl.BlockSpec((K, tr, C), lambda i: (0, i, 0)), row, row, row],
        out_specs=(row, row, row, row), out_shape=(out, out, out, out), compiler_params=_params(("parallel",)),
    )(g_parts, w, m, v)


def _sum_parts(parts, *, name):
    K, R, C = parts.shape

    def body(p_ref, o_ref):
        acc = p_ref[0]
        for k in range(1, K):
            acc = acc + p_ref[k]
        o_ref[...] = acc

    return pl.pallas_call(body, name=name, out_shape=jax.ShapeDtypeStruct((R, C), F32))(parts)


def _position():
    return lax.axis_index("x"), lax.axis_index("y"), lax.axis_index("c")


def _all_gather(shard, *, name):
    R, C = shard.shape

    def body(x_ref, out_ref, send_sems, recv_sems, local_sem):
        x, y, c = _position()
        me, sibling = (x, y, c), (x, y, 1 - c)
        chips = [(1 - x, y), (x, 1 - y), (1 - x, 1 - y)]

        def slot(px, py, pc):
            return out_ref.at[4 * px + 2 * py + pc]

        def copy(k, block, to, src=None):
            return pltpu.make_async_remote_copy(
                src_ref=slot(*block) if src is None else src, dst_ref=slot(*block),
                send_sem=send_sems.at[k], recv_sem=recv_sems.at[k], device_id=to, device_id_type=MESH)

        mine = pltpu.make_async_copy(x_ref, slot(*me), local_sem)
        mine.start()
        first = [copy(0, me, sibling, src=x_ref)]
        first += [copy(1 + j, me, (*chip, c), src=x_ref) for j, chip in enumerate(chips)]
        for cp in first:
            cp.start()
        passed = [copy(4 + j, (*chip, c), sibling) for j, chip in enumerate(chips)]
        for j, chip in enumerate(chips):
            copy(1 + j, (*chip, c), me).wait_recv()
            passed[j].start()
        copy(0, sibling, me).wait_recv()
        for j, chip in enumerate(chips):
            copy(4 + j, (*chip, 1 - c), me).wait_recv()
        for cp in first + passed:
            cp.wait_send()
        mine.wait()

    return pl.pallas_call(
        body, name=name, in_specs=[ANY], out_specs=ANY, out_shape=jax.ShapeDtypeStruct((N_DEV, R, C), shard.dtype),
        scratch_shapes=[pltpu.SemaphoreType.DMA((7,)), pltpu.SemaphoreType.DMA((7,)), pltpu.SemaphoreType.DMA(())],
    )(shard)


def _all_to_all(parts, *, name):
    _, R, C = parts.shape

    def body(x_ref, out_ref, send_sems, recv_sems, local_sem):
        x, y, c = _position()
        me = 4 * x + 2 * y + c
        mine = pltpu.make_async_copy(x_ref.at[me], out_ref.at[me], local_sem)
        mine.start()
        sends, recvs = [], []
        for k in range(1, N_DEV):
            px, py, pc = (x + (k >> 2)) % 2, (y + ((k >> 1) & 1)) % 2, (c + (k & 1)) % 2
            peer = 4 * px + 2 * py + pc
            sends.append(pltpu.make_async_remote_copy(
                src_ref=x_ref.at[peer], dst_ref=out_ref.at[me], send_sem=send_sems.at[k - 1],
                recv_sem=recv_sems.at[k - 1], device_id=(px, py, pc), device_id_type=MESH))
            recvs.append(pltpu.make_async_remote_copy(
                src_ref=x_ref.at[me], dst_ref=out_ref.at[peer], send_sem=send_sems.at[k - 1],
                recv_sem=recv_sems.at[k - 1], device_id=(px, py, pc), device_id_type=MESH))
        for cp in sends:
            cp.start()
        for cp in recvs:
            cp.wait_recv()
        for cp in sends:
            cp.wait_send()
        mine.wait()

    return pl.pallas_call(
        body, name=name, in_specs=[ANY], out_specs=ANY, out_shape=jax.ShapeDtypeStruct(parts.shape, parts.dtype),
        scratch_shapes=[pltpu.SemaphoreType.DMA((7,)), pltpu.SemaphoreType.DMA((7,)), pltpu.SemaphoreType.DMA(())],
    )(parts)


def _ffn_fwd(x, gain, w_in, w_out, tag):
    h = _rmsnorm_fwd(x, gain, name=f"{tag}_norm")
    p = _matmul(h, w_in, name=f"{tag}_in", out_dtype=BF16)
    a = _swiglu_fwd(p, name=f"{tag}_act")
    y = _matmul(a, w_out, name=f"{tag}_out", resid=x, scale=0.5)
    return y, (x, h, p, a)


def _ffn_bwd(dy, saved, gain, w_in, w_out, tag):
    x, h, p, a = saved
    da = _matmul(dy, w_out, tb=True, name=f"{tag}_out_dx", out_dtype=BF16, tn=D_FF // 2, scale=0.5)
    d_w_out = _matmul(a, dy, ta=True, name=f"{tag}_out_dw", out_dtype=BF16, tm=D_FF // 2, scale=0.5)
    dp = _swiglu_bwd(da, p, name=f"{tag}_act_bwd")
    d_w_in = _matmul(h, dp, ta=True, name=f"{tag}_in_dw", out_dtype=BF16)
    dh = _matmul(dp, w_in, tb=True, name=f"{tag}_in_dx", tk=D_FF)
    dx, d_gain = _rmsnorm_bwd(dh, x, gain, dy, name=f"{tag}_norm_bwd")
    return dx, d_gain, d_w_in, d_w_out


def _head_rows(cols, T):
    return cols.T.reshape(HEADS, T // DN_CHUNK, 1, DN_CHUNK)


def _mixer_fwd(x, w, tag):
    T = x.shape[0]
    h = _rmsnorm_fwd(x, w["mix_norm"], name=f"{tag}_norm")
    proj = _matmul(h, w["w_main"], name=f"{tag}_proj")
    scal = _matmul(h, w["w_scal"], name=f"{tag}_proj_scal", tn=N_SCAL)
    qkv = _conv_fwd(proj, w["conv_w"], name=f"{tag}_conv")
    b_rows = _head_rows(scal[:, 0:HEADS], T)
    a_rows = _head_rows(scal[:, HEADS:2 * HEADS], T)
    o_a, states = _dn_fwd(qkv, b_rows, a_rows, w["hp"], name=f"{tag}_dn")
    oa_n = _gated_norm_fwd(o_a, proj, w["dn_out_norm"], name=f"{tag}_dn_norm")
    ya = _matmul(oa_n, w["w_branch_a"], name=f"{tag}_branch_a")
    o_b, ltot = _sb_fwd(proj, w["sb_q_norm"], w["sb_k_norm"], name=f"{tag}_sb")
    yb = _matmul(o_b, w["w_branch_b"], name=f"{tag}_branch_b")
    merged = _merge_fwd(ya, yb, proj, name=f"{tag}_merge")
    y = _matmul(merged, w["w_out"], name=f"{tag}_out", resid=x)
    return y, (x, h, proj, qkv, b_rows, a_rows, o_a, states, oa_n, ya, o_b, ltot, yb, merged)


def _mixer_bwd(dy, saved, w, tag):
    x, h, proj, qkv, b_rows, a_rows, o_a, states, oa_n, ya, o_b, ltot, yb, merged = saved
    T = x.shape[0]
    g = {}
    d_merged = _matmul(dy, w["w_out"], tb=True, name=f"{tag}_out_dx", out_dtype=BF16)
    g["w_out"] = _matmul(merged, dy, ta=True, name=f"{tag}_out_dw", out_dtype=BF16)
    d_ya, d_yb, d_ga, d_gb = _merge_bwd(d_merged, ya, yb, proj, name=f"{tag}_merge_bwd")
    d_oan = _matmul(d_ya, w["w_branch_a"], tb=True, name=f"{tag}_branch_a_dx")
    g["w_branch_a"] = _matmul(oa_n, d_ya, ta=True, name=f"{tag}_branch_a_dw", out_dtype=BF16)
    d_ob = _matmul(d_yb, w["w_branch_b"], tb=True, name=f"{tag}_branch_b_dx")
    g["w_branch_b"] = _matmul(o_b, d_yb, ta=True, name=f"{tag}_branch_b_dw", out_dtype=BF16)
    d_oa, d_z, g["dn_out_norm"] = _gated_norm_bwd(d_oan, o_a, proj, w["dn_out_norm"], name=f"{tag}_dn_norm_bwd")
    d_qkv, d_b_rows, d_a_rows, d_hp = _dn_bwd(qkv, b_rows, a_rows, w["hp"], states, d_oa, name=f"{tag}_dn_bwd")
    g["dn_a_log"] = d_hp[:, 0, 0]
    g["dn_dt_bias"] = d_hp[:, 1, 0]
    d_conv_in, g["conv_w"] = _conv_bwd(d_qkv, proj, w["conv_w"], name=f"{tag}_conv_bwd")
    d_sbq, d_sbk, d_sbv, g["sb_q_norm"], g["sb_k_norm"] = _sb_bwd(
        proj, w["sb_q_norm"], w["sb_k_norm"], ltot, d_ob, name=f"{tag}_sb_bwd")
    d_proj = jnp.concatenate([d_conv_in, d_z, d_sbq, d_sbk, d_sbv, d_ga, d_gb], axis=1)
    d_scal = jnp.concatenate([d_b_rows.reshape(HEADS, T).T, d_a_rows.reshape(HEADS, T).T,
                              jnp.zeros((T, N_SCAL - 2 * HEADS), F32)], axis=1).astype(BF16)
    g["w_main"] = _matmul(h, d_proj, ta=True, name=f"{tag}_proj_dw", out_dtype=BF16)
    g["w_scal"] = _matmul(h, d_scal, ta=True, name=f"{tag}_proj_scal_dw", out_dtype=BF16, tn=N_SCAL)
    dh_scal = _matmul(d_scal, w["w_scal"], tb=True, name=f"{tag}_proj_scal_dx")
    dh = _matmul(d_proj, w["w_main"], tb=True, name=f"{tag}_proj_dx", tk=N_MAIN // 4, resid=dh_scal)
    dx, g["mix_norm"] = _rmsnorm_bwd(dh, x, w["mix_norm"], dy, name=f"{tag}_norm_bwd")
    return dx, g


def _local_step(x, target, layers):
    saved = []
    for l, w in enumerate(layers):
        x, s1 = _ffn_fwd(x, w["ffn1_norm"], w["ffn1_w_in"], w["ffn1_w_out"], f"l{l}_ffn1")
        x, s2 = _mixer_fwd(x, w, f"l{l}_mix")
        x, s3 = _ffn_fwd(x, w["ffn2_norm"], w["ffn2_w_in"], w["ffn2_w_out"], f"l{l}_ffn2")
        saved.append((s1, s2, s3))
    loss, dx = _loss_head(x, target, name="loss_head")
    grads = [None] * len(layers)
    for l in reversed(range(len(layers))):
        w = layers[l]
        s1, s2, s3 = saved[l]
        dx, g_n2, g_in2, g_out2 = _ffn_bwd(dx, s3, w["ffn2_norm"], w["ffn2_w_in"], w["ffn2_w_out"], f"l{l}_ffn2")
        dx, g = _mixer_bwd(dx, s2, w, f"l{l}_mix")
        dx, g_n1, g_in1, g_out1 = _ffn_bwd(dx, s1, w["ffn1_norm"], w["ffn1_w_in"], w["ffn1_w_out"], f"l{l}_ffn1")
        g.update(ffn1_norm=g_n1, ffn1_w_in=g_in1, ffn1_w_out=g_out1, ffn2_norm=g_n2, ffn2_w_in=g_in2, ffn2_w_out=g_out2)
        grads[l] = g
    return loss, dx, grads


_BIG = (("ffn1_w_in", "col"), ("ffn1_w_out", "row"), ("w_in", "col"), ("w_branch_a", "row"), ("w_branch_b", "row"),
        ("w_out", "row"), ("ffn2_w_in", "col"), ("ffn2_w_out", "row"))
_SMALL = ("ffn1_norm", "mix_norm", "ffn2_norm", "dn_a_log", "dn_dt_bias", "dn_out_norm", "sb_q_norm", "sb_k_norm")
_ORDER = ("ffn1_norm", "ffn1_w_in", "ffn1_w_out", "mix_norm", "w_in", "dn_conv_w", "dn_a_log", "dn_dt_bias", "dn_out_norm",
          "sb_q_norm", "sb_k_norm", "w_branch_a", "w_branch_b", "w_out", "ffn2_norm", "ffn2_w_in", "ffn2_w_out")
PACK_COLS = 1024
COL_SCAL = 4 * D_MODEL


def _pad_rows(a, multiple):
    pad = (-a.shape[-2]) % multiple
    return a if pad == 0 else jnp.pad(a, [(0, 0)] * (a.ndim - 2) + [(0, pad), (0, 0)])


def _lane_rows(a):
    flat = a.reshape(-1)
    flat = jnp.pad(flat, (0, (-flat.shape[0]) % 128))
    return flat.reshape(-1, 128)


def _pack_small(named):
    pieces, spans, r = [], {}, 0
    for n, a in named:
        rows = _lane_rows(a)
        spans[n] = (r, r + rows.shape[0], a.shape)
        r += rows.shape[0]
        pieces.append(rows)
    return _pad_rows(jnp.concatenate(pieces, axis=0), 8), spans


def _unpack_small(packed, spans, n):
    r0, r1, shape = spans[n]
    return packed[r0:r1].reshape(-1)[:math.prod(shape)].reshape(shape)


def kernel(x, ffn1_norm, ffn1_w_in, ffn1_w_out, mix_norm, w_in, dn_conv_w, dn_a_log, dn_dt_bias, dn_out_norm, sb_q_norm, sb_k_norm, w_branch_a, w_branch_b, w_out, ffn2_norm, ffn2_w_in, ffn2_w_out, loss_target, m_ffn1_norm, m_ffn1_w_in, m_ffn1_w_out, m_mix_norm, m_w_in, m_dn_conv_w, m_dn_a_log, m_dn_dt_bias, m_dn_out_norm, m_sb_q_norm, m_sb_k_norm, m_w_branch_a, m_w_branch_b, m_w_out, m_ffn2_norm, m_ffn2_w_in, m_ffn2_w_out, v_ffn1_norm, v_ffn1_w_in, v_ffn1_w_out, v_mix_norm, v_w_in, v_dn_conv_w, v_dn_a_log, v_dn_dt_bias, v_dn_out_norm, v_sb_q_norm, v_sb_k_norm, v_w_branch_a, v_w_branch_b, v_w_out, v_ffn2_norm, v_ffn2_w_in, v_ffn2_w_out):
    given = dict(locals())
    weights = {n: given[n] for n in _ORDER}
    mom_m = {n: given["m_" + n] for n in _ORDER}
    mom_v = {n: given["v_" + n] for n in _ORDER}
    L = ffn1_norm.shape[0]
    ax, ay, ac = _position()
    my_slot = 4 * ax + 2 * ay + ac

    shard_rows, pieces = {}, []
    for n, _ in _BIG:
        flat = weights[n].astype(BF16).reshape(-1, PACK_COLS)
        shard_rows[n] = flat.shape[0]
        pieces.append(flat)
    packed = _pad_rows(jnp.concatenate(pieces, axis=0), 16)
    gathered = _all_gather(packed, name="gather_weights")
    full, r = {}, 0
    for n, kind in _BIG:
        _, a, b = weights[n].shape
        blk = gathered[:, r:r + shard_rows[n]].reshape(N_DEV, L, a, b)
        r += shard_rows[n]
        full[n] = (blk.transpose(1, 2, 0, 3).reshape(L, a, N_DEV * b) if kind == "col"
                   else blk.transpose(1, 0, 2, 3).reshape(L, N_DEV * a, b))
    conv_cols = dn_conv_w.shape[-1]
    conv_full = _all_gather(_pad_rows(_lane_rows(dn_conv_w), 8), name="gather_conv")
    conv_full = conv_full.reshape(N_DEV, -1)[:, :L * DN_CONV * conv_cols].reshape(N_DEV, L, DN_CONV, conv_cols)
    conv_full = conv_full.transpose(1, 2, 0, 3).reshape(L, DN_CONV, N_DEV * conv_cols)

    layers = []
    for l in range(L):
        wi = full["w_in"][l]
        hp = jnp.concatenate([jnp.broadcast_to(dn_a_log[l][:, None, None], (HEADS, 1, 128)),
                              jnp.broadcast_to(dn_dt_bias[l][:, None, None], (HEADS, 1, 128)),
                              jnp.zeros((HEADS, 6, 128), F32)], axis=1)
        layers.append(dict(
            ffn1_norm=ffn1_norm[l][None], ffn1_w_in=full["ffn1_w_in"][l], ffn1_w_out=full["ffn1_w_out"][l],
            mix_norm=mix_norm[l][None],
            w_main=jnp.concatenate([wi[:, :COL_SCAL], wi[:, COL_SCAL + 2 * HEADS:]], axis=1),
            w_scal=jnp.pad(wi[:, COL_SCAL:COL_SCAL + 2 * HEADS], ((0, 0), (0, N_SCAL - 2 * HEADS))),
            conv_w=conv_full[l], hp=hp, dn_out_norm=dn_out_norm[l][None], sb_q_norm=sb_q_norm[l][None],
            sb_k_norm=sb_k_norm[l][None], w_branch_a=full["w_branch_a"][l], w_branch_b=full["w_branch_b"][l],
            w_out=full["w_out"][l], ffn2_norm=ffn2_norm[l][None], ffn2_w_in=full["ffn2_w_in"][l],
            ffn2_w_out=full["ffn2_w_out"][l]))

    loss_row, dx, grads = _local_step(x[0], loss_target[0], layers)
    loss = lax.psum(loss_row[0, 0], ("x", "y", "c"))

    def stacked(n):
        if n == "w_in":
            return jnp.stack([jnp.concatenate([g["w_main"][:, :COL_SCAL], g["w_scal"][:, :2 * HEADS],
                                               g["w_main"][:, COL_SCAL:]], axis=1) for g in grads])
        return jnp.stack([g[n] for g in grads])

    pieces = []
    for n, kind in _BIG:
        g = stacked(n)
        _, a, b = weights[n].shape
        if kind == "col":
            blk = g.reshape(L, a, N_DEV, b).transpose(2, 0, 1, 3)
        else:
            blk = g.reshape(L, N_DEV, a, b).transpose(1, 0, 2, 3)
        pieces.append(blk.reshape(N_DEV, -1, PACK_COLS))
    outgoing = _pad_rows(jnp.concatenate(pieces, axis=1), 16)
    incoming = _all_to_all(outgoing, name="scatter_grads")

    out = {}
    r = 0
    for n, _ in _BIG:
        _, a, b = weights[n].shape
        parts = incoming[:, r:r + shard_rows[n]].reshape(N_DEV, L * a, b)
        r += shard_rows[n]
        res = _adamw(parts, weights[n].reshape(L * a, b), mom_m[n].reshape(L * a, b), mom_v[n].reshape(L * a, b),
                     name=f"adamw_{n}")
        out[n] = tuple(t.reshape(L, a, b) for t in res)

    small_grads = [(n, jnp.stack([g[n].reshape(weights[n].shape[1:]) for g in grads])) for n in _SMALL]
    small_packed, spans = _pack_small(small_grads + [("conv", jnp.stack([g["conv_w"] for g in grads]))])
    small_sum = _sum_parts(_all_gather(small_packed, name="gather_small_grads"), name="sum_small_grads")
    rep_rows = spans["conv"][0]
    pack_rep = lambda d: _pad_rows(_pack_small([(n, d[n]) for n in _SMALL])[0], 8)
    rep_pad = (-rep_rows) % 8
    g_rep = jnp.pad(small_sum[:rep_rows], ((0, rep_pad), (0, 0)))
    res = _adamw(g_rep[None], pack_rep(weights), pack_rep(mom_m), pack_rep(mom_v), name="adamw_replicated")
    for n in _SMALL:
        out[n] = tuple(_unpack_small(t, spans, n) for t in res)
    conv_sum = _unpack_small(small_sum, spans, "conv")
    conv_mine = lax.dynamic_slice_in_dim(conv_sum, my_slot * conv_cols, conv_cols, axis=2).reshape(L * DN_CONV, conv_cols)
    flat = lambda t: t.reshape(L * DN_CONV, conv_cols)
    res = _adamw(conv_mine[None], flat(dn_conv_w), flat(m_dn_conv_w), flat(v_dn_conv_w), name="adamw_conv")
    out["dn_conv_w"] = tuple(t.reshape(L, DN_CONV, conv_cols) for t in res)

    return (loss, dx[None], *[out[n][0] for n in _ORDER], *[out[n][1] for n in _ORDER],
            *[out[n][2] for n in _ORDER], *[out[n][3] for n in _ORDER])
```

```python
import functools
import math

import jax
import jax.numpy as jnp
from jax import lax
from jax.experimental import pallas as pl
from jax.experimental.pallas import tpu as pltpu

F32 = jnp.float32
BF16 = jnp.bfloat16

N_DEV = 8
D_MODEL = 1024
DEPTH = 4
D_FF = 2816
HEADS = 8
HEAD_DIM = 128
DN_CHUNK = 64
DN_CONV = 4
DN_UNROLL = 4
SB_BLOCK = 128
SB_KEY_TILE = 512
RMS_EPS = 1e-6
L2_EPS = 1e-6
N_IN = 9232
N_MAIN = 9216
N_SCAL = 128
QK_SCALE = HEAD_DIM ** -0.5

ADAM_LR = 0.001
ADAM_B1 = 0.9
ADAM_B2 = 0.999
ADAM_EPS = 1e-08
ADAM_WD = 0.01
ADAM_STEP = 10

V7X_VMEM_LIMIT = 56 * 1024 * 1024
MESH = pl.DeviceIdType.MESH
ANY = pl.BlockSpec(memory_space=pl.ANY)


def _params(sem=None, vmem=V7X_VMEM_LIMIT):
    return pltpu.CompilerParams(dimension_semantics=sem, vmem_limit_bytes=vmem)


def _sigmoid(x):
    return 1.0 / (1.0 + jnp.exp(-x))


def _softplus(x):
    return jnp.maximum(x, 0.0) + jnp.log(1.0 + jnp.exp(-jnp.abs(x)))


def _bdot(a, b, dims=(((1,), (0,)), ((), ()))):
    return lax.dot_general(a.astype(BF16), b.astype(BF16), dims, preferred_element_type=F32)


_NT = (((1,), (1,)), ((), ()))
_TN = (((0,), (0,)), ((), ()))


def _hdot(a, b, dims=(((1,), (0,)), ((), ()))):
    a_hi = a.astype(BF16)
    b_hi = b.astype(BF16)
    a_lo = (a - a_hi.astype(F32)).astype(BF16)
    b_lo = (b - b_hi.astype(F32)).astype(BF16)
    dot = functools.partial(lax.dot_general, dimension_numbers=dims, preferred_element_type=F32)
    return dot(a_hi, b_hi) + (dot(a_hi, b_lo) + dot(a_lo, b_hi))


def _hdot_tn(a, b):
    return _hdot(a, b, _TN)


def _mm(*, name, grid, a, a_spec, b, b_spec, out_shape, o_spec, tile, ta=False, tb=False, resid=None, scale=1.0):
    nk = grid[2]
    dims = (((0 if ta else 1,), (1 if tb else 0,)), ((), ()))

    def flat(v):
        return v if v.ndim == 2 else v.reshape(-1, v.shape[-1])

    def body(*refs):
        a_ref, b_ref = refs[:2]
        r_ref = refs[2] if resid is not None else None
        o_ref = refs[3] if resid is not None else refs[2]
        part = lax.dot_general(flat(a_ref[...]).astype(BF16), flat(b_ref[...]).astype(BF16), dims,
                               preferred_element_type=F32)

        def finish(acc):
            if scale != 1.0:
                acc = acc * scale
            if r_ref is not None:
                acc = r_ref[...] + acc
            o_ref[...] = acc.astype(o_ref.dtype)

        if nk == 1:
            finish(part)
        else:
            acc_ref = refs[-1]
            k = pl.program_id(2)

            @pl.when(k == 0)
            def _():
                acc_ref[...] = part

            @pl.when(k > 0)
            def _():
                acc_ref[...] += part

            @pl.when(k == nk - 1)
            def _():
                finish(acc_ref[...])

    in_specs = [a_spec, b_spec] + ([pl.BlockSpec(tile, lambda i, j, k: (i, j))] if resid is not None else [])
    args = (a, b) + ((resid,) if resid is not None else ())
    return pl.pallas_call(
        body, name=name, grid=grid, in_specs=in_specs, out_specs=o_spec, out_shape=out_shape,
        scratch_shapes=[pltpu.VMEM(tile, F32)] if nk > 1 else [],
        compiler_params=_params(("parallel", "parallel", "arbitrary")),
    )(*args)


def _matmul(a, b, *, name, ta=False, tb=False, out_dtype=F32, tm=None, tn=None, tk=None, resid=None, scale=1.0):
    if ta:
        K, M = a.shape
    else:
        M, K = a.shape
    N = b.shape[0] if tb else b.shape[1]
    tm = tm or min(M, 1024)
    tn = tn or min(N, 512)
    tk = tk or K
    assert M % tm == 0 and N % tn == 0 and K % tk == 0, (name, M, N, K, tm, tn, tk)
    a_spec = pl.BlockSpec((tk, tm), lambda i, j, k: (k, i)) if ta else pl.BlockSpec((tm, tk), lambda i, j, k: (i, k))
    b_spec = pl.BlockSpec((tn, tk), lambda i, j, k: (j, k)) if tb else pl.BlockSpec((tk, tn), lambda i, j, k: (k, j))
    return _mm(name=name, grid=(M // tm, N // tn, K // tk), a=a, a_spec=a_spec, b=b, b_spec=b_spec,
               out_shape=jax.ShapeDtypeStruct((M, N), out_dtype), o_spec=pl.BlockSpec((tm, tn), lambda i, j, k: (i, j)),
               tile=(tm, tn), ta=ta, tb=tb, resid=resid, scale=scale)


ROW_TILE = 256


def _rmsnorm_fwd(x, gain, *, name):
    T, D = x.shape

    def body(x_ref, g_ref, o_ref):
        xf = x_ref[...]
        r = lax.rsqrt(jnp.mean(xf * xf, axis=-1, keepdims=True) + RMS_EPS)
        o_ref[...] = (xf * r * g_ref[...]).astype(o_ref.dtype)

    return pl.pallas_call(
        body, name=name, grid=(T // ROW_TILE,),
        in_specs=[pl.BlockSpec((ROW_TILE, D), lambda i: (i, 0)), pl.BlockSpec((1, D), lambda i: (0, 0))],
        out_specs=pl.BlockSpec((ROW_TILE, D), lambda i: (i, 0)),
        out_shape=jax.ShapeDtypeStruct((T, D), BF16), compiler_params=_params(("parallel",)),
    )(x, gain)


def _rmsnorm_bwd(dh, x, gain, dres, *, name):
    T, D = x.shape

    def body(dh_ref, x_ref, g_ref, res_ref, dx_ref, dg_ref):
        xf = x_ref[...]
        r = lax.rsqrt(jnp.mean(xf * xf, axis=-1, keepdims=True) + RMS_EPS)
        y = xf * r
        dh_v = dh_ref[...].astype(F32)
        dy = dh_v * g_ref[...]
        dx_ref[...] = res_ref[...] + r * (dy - y * jnp.mean(dy * y, axis=-1, keepdims=True))

        @pl.when(pl.program_id(0) == 0)
        def _():
            dg_ref[...] = jnp.zeros_like(dg_ref)

        dg_ref[...] += jnp.sum(dh_v * y, axis=0, keepdims=True)

    row = pl.BlockSpec((ROW_TILE, D), lambda i: (i, 0))
    vec = pl.BlockSpec((1, D), lambda i: (0, 0))
    return pl.pallas_call(
        body, name=name, grid=(T // ROW_TILE,), in_specs=[row, row, vec, row], out_specs=(row, vec),
        out_shape=(jax.ShapeDtypeStruct((T, D), F32), jax.ShapeDtypeStruct((1, D), F32)),
        compiler_params=_params(("arbitrary",)),
    )(dh, x, gain, dres)


FF_HALF = N_DEV // 2


def _swiglu_fwd(p, *, name):
    _, T, fb = p.shape

    def body(g_ref, u_ref, o_ref):
        g = g_ref[...].astype(F32)
        o_ref[...] = (g * _sigmoid(g) * u_ref[...].astype(F32)).astype(o_ref.dtype)

    blk = (None, ROW_TILE, fb)
    return pl.pallas_call(
        body, name=name, grid=(T // ROW_TILE, FF_HALF),
        in_specs=[pl.BlockSpec(blk, lambda i, j: (j, i, 0)), pl.BlockSpec(blk, lambda i, j: (j + FF_HALF, i, 0))],
        out_specs=pl.BlockSpec(blk, lambda i, j: (j, i, 0)),
        out_shape=jax.ShapeDtypeStruct((FF_HALF, T, fb), BF16), compiler_params=_params(("parallel", "parallel")),
    )(p, p)


def _swiglu_bwd(da, p, *, name):
    _, T, fb = p.shape

    def body(da_ref, g_ref, u_ref, o_ref):
        g = g_ref[...].astype(F32)
        u = u_ref[...].astype(F32)
        d = da_ref[...].astype(F32)
        s = _sigmoid(g)
        dgate = d * u * (s * (1.0 + g * (1.0 - s)))
        dup = d * g * s
        o_ref[...] = jnp.where(pl.program_id(1) < FF_HALF, dgate, dup).astype(o_ref.dtype)

    blk = (None, ROW_TILE, fb)
    return pl.pallas_call(
        body, name=name, grid=(T // ROW_TILE, 2 * FF_HALF),
        in_specs=[pl.BlockSpec(blk, lambda i, j: (j % FF_HALF, i, 0)), pl.BlockSpec(blk, lambda i, j: (j % FF_HALF, i, 0)),
                  pl.BlockSpec(blk, lambda i, j: (j % FF_HALF + FF_HALF, i, 0))],
        out_specs=pl.BlockSpec(blk, lambda i, j: (j, i, 0)),
        out_shape=jax.ShapeDtypeStruct((2 * FF_HALF, T, fb), BF16), compiler_params=_params(("parallel", "parallel")),
    )(da, p, p)


COL_GATE_A = 7
COL_GATE_B = 8


def _merge_fwd(ya, yb, proj, *, name):
    T, D = ya.shape

    def body(ya_ref, yb_ref, ga_ref, gb_ref, o_ref):
        o_ref[...] = (_sigmoid(ga_ref[...]) * ya_ref[...] + _sigmoid(gb_ref[...]) * yb_ref[...]).astype(o_ref.dtype)

    row = pl.BlockSpec((ROW_TILE, D), lambda i: (i, 0))
    return pl.pallas_call(
        body, name=name, grid=(T // ROW_TILE,),
        in_specs=[row, row, pl.BlockSpec((ROW_TILE, D), lambda i: (i, COL_GATE_A)),
                  pl.BlockSpec((ROW_TILE, D), lambda i: (i, COL_GATE_B))],
        out_specs=row, out_shape=jax.ShapeDtypeStruct((T, D), BF16), compiler_params=_params(("parallel",)),
    )(ya, yb, proj, proj)


def _merge_bwd(dm, ya, yb, proj, *, name):
    T, D = ya.shape

    def body(dm_ref, ya_ref, yb_ref, ga_ref, gb_ref, dya_ref, dyb_ref, dga_ref, dgb_ref):
        d = dm_ref[...].astype(F32)
        sa = _sigmoid(ga_ref[...])
        sb = _sigmoid(gb_ref[...])
        dya_ref[...] = (d * sa).astype(BF16)
        dyb_ref[...] = (d * sb).astype(BF16)
        dga_ref[...] = (d * ya_ref[...] * sa * (1.0 - sa)).astype(BF16)
        dgb_ref[...] = (d * yb_ref[...] * sb * (1.0 - sb)).astype(BF16)

    row = pl.BlockSpec((ROW_TILE, D), lambda i: (i, 0))
    out = jax.ShapeDtypeStruct((T, D), BF16)
    return pl.pallas_call(
        body, name=name, grid=(T // ROW_TILE,),
        in_specs=[row, row, row, pl.BlockSpec((ROW_TILE, D), lambda i: (i, COL_GATE_A)),
                  pl.BlockSpec((ROW_TILE, D), lambda i: (i, COL_GATE_B))],
        out_specs=(row, row, row, row), out_shape=(out, out, out, out), compiler_params=_params(("parallel",)),
    )(dm, ya, yb, proj, proj)


def _loss_head(y, target, *, name):
    T, D = y.shape

    def body(y_ref, t_ref, loss_ref, dy_ref):
        err = y_ref[...] - t_ref[...]
        dy_ref[...] = err * (1.0 / D)

        @pl.when(pl.program_id(0) == 0)
        def _():
            loss_ref[...] = jnp.zeros_like(loss_ref)

        loss_ref[...] += 0.5 * jnp.sum(jnp.sum(err * err, axis=-1, keepdims=True) * (1.0 / D), axis=0, keepdims=True)

    row = pl.BlockSpec((ROW_TILE, D), lambda i: (i, 0))
    return pl.pallas_call(
        body, name=name, grid=(T // ROW_TILE,), in_specs=[row, row],
        out_specs=(pl.BlockSpec((1, 128), lambda i: (0, 0)), row),
        out_shape=(jax.ShapeDtypeStruct((1, 128), F32), jax.ShapeDtypeStruct((T, D), F32)),
        compiler_params=_params(("arbitrary",)),
    )(y, target)


CONV_PAD = 8


def _conv_taps(w, xp, T, first):
    acc = w[0:1, :] * xp[pl.ds(first, T), :]
    for i in range(1, DN_CONV):
        acc = acc + w[i:i + 1, :] * xp[pl.ds(first + i, T), :]
    return acc


def _conv_fwd(proj, conv_w, *, name):
    T = proj.shape[0]

    def body(x_ref, w_ref, o_ref, xp):
        xp[0:CONV_PAD, :] = jnp.zeros((CONV_PAD, HEAD_DIM), F32)
        xp[CONV_PAD:, :] = x_ref[...]
        y = _conv_taps(w_ref[...], xp, T, CONV_PAD - (DN_CONV - 1))
        s = y * _sigmoid(y)
        n = s * lax.rsqrt(jnp.sum(s * s, axis=-1, keepdims=True) + L2_EPS)
        o_ref[0] = jnp.where(pl.program_id(0) < 2, n, s)

    return pl.pallas_call(
        body, name=name, grid=(3, HEADS),
        in_specs=[pl.BlockSpec((T, HEAD_DIM), lambda c, h: (0, c * HEADS + h)),
                  pl.BlockSpec((DN_CONV, HEAD_DIM), lambda c, h: (0, c * HEADS + h))],
        out_specs=pl.BlockSpec((1, T, HEAD_DIM), lambda c, h: (c, 0, h)),
        out_shape=jax.ShapeDtypeStruct((3, T, D_MODEL), F32),
        scratch_shapes=[pltpu.VMEM((T + CONV_PAD, HEAD_DIM), F32)],
        compiler_params=_params(("parallel", "parallel")),
    )(proj, conv_w)


def _conv_bwd(dqkv, proj, conv_w, *, name):
    T = proj.shape[0]

    def body(d_ref, x_ref, w_ref, dx_ref, dw_ref, xp, dyp):
        xp[0:CONV_PAD, :] = jnp.zeros((CONV_PAD, HEAD_DIM), F32)
        xp[CONV_PAD:, :] = x_ref[...]
        w = w_ref[...]
        y = _conv_taps(w, xp, T, CONV_PAD - (DN_CONV - 1))
        sg = _sigmoid(y)
        s = y * sg
        r = lax.rsqrt(jnp.sum(s * s, axis=-1, keepdims=True) + L2_EPS)
        n = s * r
        d = d_ref[0]
        ds = jnp.where(pl.program_id(0) < 2, r * (d - n * jnp.sum(d * n, axis=-1, keepdims=True)), d)
        dy = ds * (sg * (1.0 + y * (1.0 - sg)))
        dyp[0:T, :] = dy
        dyp[T:, :] = jnp.zeros((CONV_PAD, HEAD_DIM), F32)
        dx = w[0:1, :] * dyp[pl.ds(DN_CONV - 1, T), :]
        for i in range(1, DN_CONV):
            dx = dx + w[i:i + 1, :] * dyp[pl.ds(DN_CONV - 1 - i, T), :]
        dx_ref[...] = dx.astype(dx_ref.dtype)
        for i in range(DN_CONV):
            dw_ref[i:i + 1, :] = jnp.sum(dy * xp[pl.ds(CONV_PAD - (DN_CONV - 1) + i, T), :], axis=0, keepdims=True)

    col = lambda c, h: (0, c * HEADS + h)
    return pl.pallas_call(
        body, name=name, grid=(3, HEADS),
        in_specs=[pl.BlockSpec((1, T, HEAD_DIM), lambda c, h: (c, 0, h)), pl.BlockSpec((T, HEAD_DIM), col),
                  pl.BlockSpec((DN_CONV, HEAD_DIM), col)],
        out_specs=(pl.BlockSpec((T, HEAD_DIM), col), pl.BlockSpec((DN_CONV, HEAD_DIM), col)),
        out_shape=(jax.ShapeDtypeStruct((T, 3 * D_MODEL), BF16), jax.ShapeDtypeStruct((DN_CONV, 3 * D_MODEL), F32)),
        scratch_shapes=[pltpu.VMEM((T + CONV_PAD, HEAD_DIM), F32), pltpu.VMEM((T + CONV_PAD, HEAD_DIM), F32)],
        compiler_params=_params(("parallel", "parallel")),
    )(dqkv, proj, conv_w)


def _inv_unit_lower(low, eye):
    x = eye - low
    power = _hdot(low, low)
    steps = int(math.log2(DN_CHUNK)) - 1
    for s in range(steps):
        x = x + _hdot(x, power)
        if s + 1 < steps:
            power = _hdot(power, power)
    return x


def _dn_chunk_setup(q_ref, k_ref, v_ref, b_ref, a_ref, hp_ref, n):
    C = DN_CHUNK
    r0 = pl.multiple_of(n * C, C)
    q = q_ref[0, pl.ds(r0, C), :] * QK_SCALE
    k = k_ref[1, pl.ds(r0, C), :]
    v = v_ref[2, pl.ds(r0, C), :]
    ii = lax.broadcasted_iota(jnp.int32, (C, C), 0)
    jj = lax.broadcasted_iota(jnp.int32, (C, C), 1)
    eye_mask = ii == jj
    eye = jnp.where(eye_mask, 1.0, 0.0).astype(F32)

    def to_col(row):
        return jnp.sum(jnp.where(eye_mask, jnp.broadcast_to(row, (C, C)), 0.0), axis=1, keepdims=True)

    def to_row(col):
        return jnp.sum(jnp.where(eye_mask, jnp.broadcast_to(col, (C, C)), 0.0), axis=0, keepdims=True)

    b_row = b_ref[0, n]
    a_row = a_ref[0, n]
    a_log = hp_ref[0, 0:1, 0:C]
    dt_b = hp_ref[0, 1:2, 0:C]
    beta_row = _sigmoid(b_row)
    neg_ea = -jnp.exp(a_log)
    g_row = neg_ea * _softplus(a_row + dt_b)
    gc_col = jnp.sum(jnp.where(jj <= ii, jnp.broadcast_to(g_row, (C, C)), 0.0), axis=1, keepdims=True)
    gc_row = to_row(gc_col)
    g_last = jnp.sum(g_row, axis=1, keepdims=True)
    beta = to_col(beta_row)
    low_incl = ii >= jj
    decay = jnp.exp(jnp.where(low_incl, gc_col - gc_row, -jnp.inf))
    eg = jnp.exp(gc_col)
    egl = jnp.exp(g_last - gc_col)
    el = jnp.exp(g_last)
    kb = k * beta
    pmat = _bdot(kb, k, _NT)
    low = jnp.where(ii > jj, pmat * decay, 0.0)
    tinv = _inv_unit_lower(low, eye)
    u = _hdot(tinv, v * beta)
    w = _hdot(tinv, kb * eg)
    qk = _bdot(q, k, _NT)
    attn = qk * decay
    return dict(q=q, k=k, v=v, ii=ii, jj=jj, to_col=to_col, to_row=to_row, b_row=b_row, a_row=a_row, dt_b=dt_b,
                beta_row=beta_row, neg_ea=neg_ea, g_row=g_row, gc_col=gc_col, g_last=g_last, beta=beta,
                decay=decay, eg=eg, egl=egl, el=el, kb=kb, pmat=pmat, tinv=tinv, u=u, w=w, qk=qk, attn=attn,
                qd=q * eg, kd=k * egl, r0=r0)


def _dn_specs(T):
    nc = T // DN_CHUNK
    qkv = pl.BlockSpec((3, T, HEAD_DIM), lambda h: (0, 0, h))
    rows = pl.BlockSpec((1, nc, 1, DN_CHUNK), lambda h: (h, 0, 0, 0))
    hp = pl.BlockSpec((1, 8, 128), lambda h: (h, 0, 0))
    states = pl.BlockSpec((1, nc, HEAD_DIM, HEAD_DIM), lambda h: (h, 0, 0, 0))
    return nc, qkv, rows, hp, states


def _dn_fwd(qkv, b_rows, a_rows, hp, *, name):
    T = qkv.shape[1]
    nc, qkv_spec, row_spec, hp_spec, st_spec = _dn_specs(T)
    unroll = math.gcd(nc, DN_UNROLL)

    def body(qkv_ref, b_ref, a_ref, hp_ref, o_ref, st_ref, s_scr):
        s_scr[...] = jnp.zeros_like(s_scr)

        def step(t, carry):
            chunks = [_dn_chunk_setup(qkv_ref, qkv_ref, qkv_ref, b_ref, a_ref, hp_ref, t * unroll + s)
                      for s in range(unroll)]
            state = s_scr[...]
            for s, c in enumerate(chunks):
                st_ref[0, t * unroll + s] = state
                v_new = c["u"] - _bdot(c["w"], state)
                o_ref[pl.ds(c["r0"], DN_CHUNK), :] = _bdot(c["qd"], state) + _bdot(c["attn"], v_new)
                state = state * c["el"] + _bdot(c["kd"], v_new, _TN)
            s_scr[...] = state
            return carry

        lax.fori_loop(0, nc // unroll, step, 0)

    return pl.pallas_call(
        body, name=name, grid=(HEADS,), in_specs=[qkv_spec, row_spec, row_spec, hp_spec],
        out_specs=(pl.BlockSpec((T, HEAD_DIM), lambda h: (0, h)), st_spec),
        out_shape=(jax.ShapeDtypeStruct((T, D_MODEL), F32),
                   jax.ShapeDtypeStruct((HEADS, nc, HEAD_DIM, HEAD_DIM), F32)),
        scratch_shapes=[pltpu.VMEM((HEAD_DIM, HEAD_DIM), F32)], compiler_params=_params(("parallel",)),
    )(qkv, b_rows, a_rows, hp)


def _dn_bwd(qkv, b_rows, a_rows, hp, states, do, *, name):
    T = qkv.shape[1]
    C = DN_CHUNK
    nc, qkv_spec, row_spec, hp_spec, st_spec = _dn_specs(T)
    unroll = math.gcd(nc, DN_UNROLL)

    def body(qkv_ref, b_ref, a_ref, hp_ref, st_ref, do_ref, dqkv_ref, db_ref, da_ref, dhp_ref, ds_scr, acc_scr):
        ds_scr[...] = jnp.zeros_like(ds_scr)
        acc_scr[...] = jnp.zeros_like(acc_scr)

        def step(t, carry):
            order = [nc - 1 - (t * unroll + s) for s in range(unroll)]
            chunks = []
            for n in order:
                c = _dn_chunk_setup(qkv_ref, qkv_ref, qkv_ref, b_ref, a_ref, hp_ref, n)
                c["n"] = n
                c["state"] = st_ref[0, n]
                c["d_o"] = do_ref[pl.ds(c["r0"], C), :]
                c["v_new"] = c["u"] - _bdot(c["w"], c["state"])
                c["d_vnew_local"] = _bdot(c["attn"], c["d_o"], _TN)
                c["d_state_local"] = _bdot(c["qd"], c["d_o"], _TN)
                chunks.append(c)
            d_state = ds_scr[...]
            for c in chunks:
                c["d_vnew"] = c["d_vnew_local"] + _bdot(c["kd"], d_state)
                c["d_kd"] = _bdot(c["v_new"], d_state, _NT)
                c["d_el"] = jnp.sum(jnp.sum(d_state * c["state"], axis=1, keepdims=True), axis=0, keepdims=True)
                d_state = d_state * c["el"] + c["d_state_local"] - _bdot(c["w"], c["d_vnew"], _TN)
            ds_scr[...] = d_state
            for c in chunks:
                chunk_grads(c)
            return carry

        def chunk_grads(c):
            n = c["n"]
            ii, jj = c["ii"], c["jj"]
            q, k, v, kb, beta = c["q"], c["k"], c["v"], c["kb"], c["beta"]
            decay, eg, egl, el = c["decay"], c["eg"], c["egl"], c["el"]
            u, w, tinv = c["u"], c["w"], c["tinv"]
            state, d_o, v_new, d_vnew, d_kd, d_el = c["state"], c["d_o"], c["v_new"], c["d_vnew"], c["d_kd"], c["d_el"]
            d_qd = _bdot(d_o, state, _NT)
            d_attn = _bdot(d_o, v_new, _NT)
            d_w = -_bdot(d_vnew, state, _NT)
            d_rv = _hdot_tn(tinv, d_vnew)
            d_rw = _hdot_tn(tinv, d_w)
            d_amat = -(_bdot(d_rv, u, _NT) + _bdot(d_rw, w, _NT))
            d_low = jnp.where(ii > jj, d_amat, 0.0)
            d_p = d_low * decay
            d_qk = d_attn * decay
            e_mat = (d_low * c["pmat"] + d_attn * c["qk"]) * decay
            d_q = _bdot(d_qk, k) + d_qd * eg
            d_kb = _bdot(d_p, k) + d_rw * eg
            d_k = _bdot(d_qk, q, _TN) + _bdot(d_p, kb, _TN) + d_kd * egl + d_kb * beta
            d_beta = jnp.sum(d_kb * k, axis=1, keepdims=True) + jnp.sum(d_rv * v, axis=1, keepdims=True)
            d_v = d_rv * beta
            d_eg = jnp.sum(d_qd * q, axis=1, keepdims=True) + jnp.sum(d_rw * kb, axis=1, keepdims=True)
            d_egl = jnp.sum(d_kd * k, axis=1, keepdims=True)
            d_glast = jnp.sum(d_egl * egl, axis=0, keepdims=True) + d_el * el
            row_sum = jnp.sum(e_mat, axis=1, keepdims=True)
            col_sum = c["to_col"](jnp.sum(e_mat, axis=0, keepdims=True))
            d_gc = row_sum - col_sum + d_eg * eg - d_egl * egl
            d_g_row = jnp.sum(jnp.where(ii >= jj, jnp.broadcast_to(d_gc, (C, C)), 0.0), axis=0, keepdims=True) + d_glast
            beta_row = c["beta_row"]
            d_b_row = c["to_row"](d_beta) * beta_row * (1.0 - beta_row)
            d_a_row = d_g_row * c["neg_ea"] * _sigmoid(c["a_row"] + c["dt_b"])
            dqkv_ref[0, pl.ds(c["r0"], C), :] = d_q * QK_SCALE
            dqkv_ref[1, pl.ds(c["r0"], C), :] = d_k
            dqkv_ref[2, pl.ds(c["r0"], C), :] = d_v
            db_ref[0, n] = d_b_row
            da_ref[0, n] = d_a_row
            acc_scr[0:1, 0:C] += d_g_row * c["g_row"]
            acc_scr[1:2, 0:C] += d_a_row

        lax.fori_loop(0, nc // unroll, step, 0)
        tot = jnp.sum(acc_scr[...], axis=1, keepdims=True)
        dhp_ref[0] = jnp.broadcast_to(tot, (8, 128))

    return pl.pallas_call(
        body, name=name, grid=(HEADS,),
        in_specs=[qkv_spec, row_spec, row_spec, hp_spec, st_spec, pl.BlockSpec((T, HEAD_DIM), lambda h: (0, h))],
        out_specs=(qkv_spec, row_spec, row_spec, hp_spec),
        out_shape=(jax.ShapeDtypeStruct((3, T, D_MODEL), F32), jax.ShapeDtypeStruct((HEADS, nc, 1, C), F32),
                   jax.ShapeDtypeStruct((HEADS, nc, 1, C), F32), jax.ShapeDtypeStruct((HEADS, 8, 128), F32)),
        scratch_shapes=[pltpu.VMEM((HEAD_DIM, HEAD_DIM), F32), pltpu.VMEM((8, 128), F32)],
        compiler_params=_params(("parallel",)),
    )(qkv, b_rows, a_rows, hp, states, do)


COL_Z = 3 * HEADS


def _gated_norm_fwd(o, proj, gain, *, name):
    T = o.shape[0]

    def body(o_ref, z_ref, g_ref, out_ref):
        x = o_ref[...]
        r = lax.rsqrt(jnp.mean(x * x, axis=-1, keepdims=True) + RMS_EPS)
        z = z_ref[...]
        out_ref[...] = (x * r * g_ref[...] * (z * _sigmoid(z))).astype(out_ref.dtype)

    return pl.pallas_call(
        body, name=name, grid=(HEADS,),
        in_specs=[pl.BlockSpec((T, HEAD_DIM), lambda h: (0, h)), pl.BlockSpec((T, HEAD_DIM), lambda h: (0, COL_Z + h)),
                  pl.BlockSpec((1, HEAD_DIM), lambda h: (0, 0))],
        out_specs=pl.BlockSpec((T, HEAD_DIM), lambda h: (0, h)),
        out_shape=jax.ShapeDtypeStruct((T, D_MODEL), BF16), compiler_params=_params(("parallel",)),
    )(o, proj, gain)


def _gated_norm_bwd(dout, o, proj, gain, *, name):
    T = o.shape[0]

    def body(d_ref, o_ref, z_ref, g_ref, do_ref, dz_ref, dg_ref):
        x = o_ref[...]
        r = lax.rsqrt(jnp.mean(x * x, axis=-1, keepdims=True) + RMS_EPS)
        n = x * r
        z = z_ref[...]
        sg = _sigmoid(z)
        d = d_ref[...].astype(F32)
        g = g_ref[...]
        dz_ref[...] = (d * n * g * (sg * (1.0 + z * (1.0 - sg)))).astype(dz_ref.dtype)
        dy = d * (z * sg)
        dyg = dy * g
        do_ref[...] = r * (dyg - n * jnp.mean(dyg * n, axis=-1, keepdims=True))

        @pl.when(pl.program_id(0) == 0)
        def _():
            dg_ref[...] = jnp.zeros_like(dg_ref)

        dg_ref[...] += jnp.sum(dy * n, axis=0, keepdims=True)

    head = pl.BlockSpec((T, HEAD_DIM), lambda h: (0, h))
    vec = pl.BlockSpec((1, HEAD_DIM), lambda h: (0, 0))
    return pl.pallas_call(
        body, name=name, grid=(HEADS,),
        in_specs=[head, head, pl.BlockSpec((T, HEAD_DIM), lambda h: (0, COL_Z + h)), vec],
        out_specs=(head, head, vec),
        out_shape=(jax.ShapeDtypeStruct((T, D_MODEL), F32), jax.ShapeDtypeStruct((T, D_MODEL), BF16),
                   jax.ShapeDtypeStruct((1, HEAD_DIM), F32)),
        compiler_params=_params(("arbitrary",)),
    )(dout, o, proj, gain)


COL_SBQ = 4 * HEADS
COL_SBK = 5 * HEADS
COL_SBV = 6 * HEADS


def _split_dot(x, mat):
    hi = x.astype(BF16)
    lo = (x - hi.astype(F32)).astype(BF16)
    return jnp.dot(hi, mat, preferred_element_type=F32) + jnp.dot(lo, mat, preferred_element_type=F32)


def _sb_specs(T):
    return (pl.BlockSpec((T, HEAD_DIM), lambda h: (0, COL_SBQ + h)), pl.BlockSpec((T, HEAD_DIM), lambda h: (0, COL_SBK + h)),
            pl.BlockSpec((T, HEAD_DIM), lambda h: (0, COL_SBV + h)), pl.BlockSpec((1, HEAD_DIM), lambda h: (0, 0)))


def _head_rms(x, gain):
    r = lax.rsqrt(jnp.mean(x * x, axis=-1, keepdims=True) + RMS_EPS)
    return x * r, r


def _sb_fwd(proj, q_gain, k_gain, *, name):
    T = proj.shape[0]
    B = SB_BLOCK
    nb = T // B
    KT = min(SB_KEY_TILE, T)
    NS = KT // B
    q_spec, k_spec, v_spec, g_spec = _sb_specs(T)

    def body(q_ref, k_ref, v_ref, gq_ref, gk_ref, o_ref, lt_ref, qs, ks, vs):
        qs[...] = (_head_rms(q_ref[...], None)[0] * gq_ref[...]).astype(BF16)
        ks[...] = (_head_rms(k_ref[...], None)[0] * gk_ref[...]).astype(BF16)
        vs[...] = v_ref[...].astype(BF16)
        ii = lax.broadcasted_iota(jnp.int32, (B, B), 0)
        jj = lax.broadcasted_iota(jnp.int32, (B, B), 1)
        after = jnp.where(ii > jj, 1.0, 0.0).astype(BF16)
        row_t = lax.broadcasted_iota(jnp.int32, (B, KT), 0)
        col_t = lax.broadcasted_iota(jnp.int32, (B, KT), 1)

        def q_block(i, carry):
            q = qs[pl.ds(pl.multiple_of(i * B, B), B), :]

            def k_tile(step, inner):
                acc, tail = inner
                c0 = pl.multiple_of((i // NS - step) * KT, KT)
                z = lax.dot_general(q, ks[pl.ds(c0, KT), :], _NT, preferred_element_type=F32) * QK_SCALE
                causal = (c0 + col_t) < (i * B + row_t)
                sp = _softplus(z)
                log_1mb = jnp.where(causal, -sp, 0.0)
                parts = [None] * NS
                for b in reversed(range(NS)):
                    blk = log_1mb[:, b * B:(b + 1) * B]
                    parts[b] = _split_dot(blk, after) + tail
                    tail = tail + jnp.sum(blk, axis=1, keepdims=True)
                survive = parts[0] if NS == 1 else jnp.concatenate(parts, axis=1)
                wts = jnp.where(causal, jnp.exp(z - sp + survive), 0.0)
                acc = acc + jnp.dot(wts.astype(BF16), vs[pl.ds(c0, KT), :], preferred_element_type=F32)
                return acc, tail

            acc, tail = lax.fori_loop(0, i // NS + 1, k_tile, (jnp.zeros((B, HEAD_DIM), F32), jnp.zeros((B, 1), F32)))
            rows = pl.ds(pl.multiple_of(i * B, B), B)
            o_ref[rows, :] = acc.astype(o_ref.dtype)
            lt_ref[rows, :] = jnp.broadcast_to(tail, (B, HEAD_DIM))
            return carry

        lax.fori_loop(0, nb, q_block, 0)

    head = pl.BlockSpec((T, HEAD_DIM), lambda h: (0, h))
    return pl.pallas_call(
        body, name=name, grid=(HEADS,), in_specs=[q_spec, k_spec, v_spec, g_spec, g_spec],
        out_specs=(head, head),
        out_shape=(jax.ShapeDtypeStruct((T, D_MODEL), BF16), jax.ShapeDtypeStruct((T, D_MODEL), F32)),
        scratch_shapes=[pltpu.VMEM((T, HEAD_DIM), BF16)] * 3, compiler_params=_params(("parallel",)),
    )(proj, proj, proj, q_gain, k_gain)


def _sb_bwd(proj, q_gain, k_gain, ltot, do, *, name):
    T = proj.shape[0]
    B = SB_BLOCK
    nb = T // B
    KT = min(SB_KEY_TILE, T)
    NS = KT // B
    q_spec, k_spec, v_spec, g_spec = _sb_specs(T)

    def body(q_ref, k_ref, v_ref, gq_ref, gk_ref, lt_ref, do_ref, dq_ref, dk_ref, dv_ref, dgq_ref, dgk_ref,
             qs, ks, vs, dos, dq_acc, dk_acc, dv_acc):
        qn, q_r = _head_rms(q_ref[...], None)
        kn, k_r = _head_rms(k_ref[...], None)
        qs[...] = (qn * gq_ref[...]).astype(BF16)
        ks[...] = (kn * gk_ref[...]).astype(BF16)
        vs[...] = v_ref[...].astype(BF16)
        dos[...] = do_ref[...].astype(BF16)
        dk_acc[...] = jnp.zeros_like(dk_acc)
        dv_acc[...] = jnp.zeros_like(dv_acc)
        ii = lax.broadcasted_iota(jnp.int32, (B, B), 0)
        jj = lax.broadcasted_iota(jnp.int32, (B, B), 1)
        upto = jnp.where(ii <= jj, 1.0, 0.0).astype(BF16)
        before = jnp.where(ii < jj, 1.0, 0.0).astype(BF16)
        row_t = lax.broadcasted_iota(jnp.int32, (B, KT), 0)
        col_t = lax.broadcasted_iota(jnp.int32, (B, KT), 1)

        def q_block(i, carry):
            rows = pl.ds(pl.multiple_of(i * B, B), B)
            q = qs[rows, :]
            d_o = dos[rows, :]
            total = jnp.max(lt_ref[rows, :], axis=1, keepdims=True)

            def k_tile(t, inner):
                dq, head_lb, head_de = inner
                cols = pl.ds(pl.multiple_of(t * KT, KT), KT)
                k = ks[cols, :]
                v = vs[cols, :]
                z = lax.dot_general(q, k, _NT, preferred_element_type=F32) * QK_SCALE
                causal = (t * KT + col_t) < (i * B + row_t)
                sp = _softplus(z)
                log_1mb = jnp.where(causal, -sp, 0.0)
                parts = [None] * NS
                for b in range(NS):
                    blk = log_1mb[:, b * B:(b + 1) * B]
                    parts[b] = _split_dot(blk, upto) + head_lb
                    head_lb = head_lb + jnp.sum(blk, axis=1, keepdims=True)
                prefix = parts[0] if NS == 1 else jnp.concatenate(parts, axis=1)
                wts = jnp.where(causal, jnp.exp(z - sp + (total - prefix)), 0.0)
                d_w = lax.dot_general(d_o, v, _NT, preferred_element_type=F32)
                d_e = wts * d_w
                for b in range(NS):
                    blk = d_e[:, b * B:(b + 1) * B]
                    parts[b] = _split_dot(blk, before) + head_de
                    head_de = head_de + jnp.sum(blk, axis=1, keepdims=True)
                cum = parts[0] if NS == 1 else jnp.concatenate(parts, axis=1)
                sig = jnp.exp(z - sp)
                d_z = jnp.where(causal, d_e * (1.0 - sig) - sig * cum, 0.0) * QK_SCALE
                d_zb = d_z.astype(BF16)
                dq = dq + jnp.dot(d_zb, k, preferred_element_type=F32)
                dk_acc[cols, :] += lax.dot_general(d_zb, q, _TN, preferred_element_type=F32)
                dv_acc[cols, :] += lax.dot_general(wts.astype(BF16), d_o, _TN, preferred_element_type=F32)
                return dq, head_lb, head_de

            zero = jnp.zeros((B, 1), F32)
            dq, _, _ = lax.fori_loop(0, i // NS + 1, k_tile, (jnp.zeros((B, HEAD_DIM), F32), zero, zero))
            dq_acc[rows, :] = dq
            return carry

        lax.fori_loop(0, nb, q_block, 0)

        def norm_bwd(d_scaled, n, r, gain):
            dn = d_scaled * gain
            return r * (dn - n * jnp.mean(dn * n, axis=-1, keepdims=True)), jnp.sum(d_scaled * n, axis=0, keepdims=True)

        dq_raw, dgq = norm_bwd(dq_acc[...], qn, q_r, gq_ref[...])
        dk_raw, dgk = norm_bwd(dk_acc[...], kn, k_r, gk_ref[...])
        dq_ref[...] = dq_raw.astype(dq_ref.dtype)
        dk_ref[...] = dk_raw.astype(dk_ref.dtype)
        dv_ref[...] = dv_acc[...].astype(dv_ref.dtype)

        @pl.when(pl.program_id(0) == 0)
        def _():
            dgq_ref[...] = jnp.zeros_like(dgq_ref)
            dgk_ref[...] = jnp.zeros_like(dgk_ref)

        dgq_ref[...] += dgq
        dgk_ref[...] += dgk

    head = pl.BlockSpec((T, HEAD_DIM), lambda h: (0, h))
    out = jax.ShapeDtypeStruct((T, D_MODEL), BF16)
    vec = jax.ShapeDtypeStruct((1, HEAD_DIM), F32)
    return pl.pallas_call(
        body, name=name, grid=(HEADS,), in_specs=[q_spec, k_spec, v_spec, g_spec, g_spec, head, head],
        out_specs=(head, head, head, g_spec, g_spec), out_shape=(out, out, out, vec, vec),
        scratch_shapes=[pltpu.VMEM((T, HEAD_DIM), BF16)] * 4 + [pltpu.VMEM((T, HEAD_DIM), F32)] * 3,
        compiler_params=_params(("arbitrary",)),
    )(proj, proj, proj, q_gain, k_gain, ltot, do)


ADAM_ROWS = 128


def _adamw(g_parts, w, m, v, *, name):
    K, R, C = g_parts.shape
    tr = ADAM_ROWS if R % ADAM_ROWS == 0 else R

    def body(g_ref, w_ref, m_ref, v_ref, go_ref, d_ref, mo_ref, vo_ref):
        g = g_ref[0].astype(F32)
        for k in range(1, K):
            g = g + g_ref[k].astype(F32)
        m_new = ADAM_B1 * m_ref[...] + (1.0 - ADAM_B1) * g
        v_new = ADAM_B2 * v_ref[...] + (1.0 - ADAM_B2) * (g * g)
        m_hat = m_new / (1.0 - ADAM_B1 ** ADAM_STEP)
        v_hat = v_new / (1.0 - ADAM_B2 ** ADAM_STEP)
        go_ref[...] = g
        d_ref[...] = -ADAM_LR * (m_hat / (jnp.sqrt(v_hat) + ADAM_EPS) + ADAM_WD * w_ref[...])
        mo_ref[...] = m_new
        vo_ref[...] = v_new

    row = pl.BlockSpec((tr, C), lambda i: (i, 0))
    out = jax.ShapeDtypeStruct((R, C), F32)
    return pl.pallas_call(
        body, name=name, grid=(R // tr,), in_specs=[pl.BlockSpec((K, tr, C), lambda i: (0, i, 0)), row, row, row],
        out_specs=(row, row, row, row), out_shape=(out, out, out, out), compiler_params=_params(("parallel",)),
    )(g_parts, w, m, v)


def _sum_parts(parts, *, name):
    K, R, C = parts.shape

    def body(p_ref, o_ref):
        acc = p_ref[0]
        for k in range(1, K):
            acc = acc + p_ref[k]
        o_ref[...] = acc

    return pl.pallas_call(body, name=name, out_shape=jax.ShapeDtypeStruct((R, C), F32))(parts)


def _position():
    return lax.axis_index("x"), lax.axis_index("y"), lax.axis_index("c")


def _all_gather(shards, *, name):
    n = len(shards)

    def body(*refs):
        x_refs, out_refs = refs[:n], refs[n:2 * n]
        send_sems, recv_sems, local_sems = refs[2 * n:]
        x, y, c = _position()
        me, sibling = (x, y, c), (x, y, 1 - c)
        chips = [(1 - x, y), (x, 1 - y), (1 - x, 1 - y)]

        def slot(a, px, py, pc):
            return out_refs[a].at[4 * px + 2 * py + pc]

        def copy(a, k, block, to, own=False):
            return pltpu.make_async_remote_copy(
                src_ref=x_refs[a] if own else slot(a, *block), dst_ref=slot(a, *block),
                send_sem=send_sems.at[a, k], recv_sem=recv_sems.at[a, k], device_id=to, device_id_type=MESH)

        mine = [pltpu.make_async_copy(x_refs[a], slot(a, *me), local_sems.at[a]) for a in range(n)]
        for cp in mine:
            cp.start()
        first = [copy(a, 1 + j, me, (*chip, c), own=True) for j, chip in enumerate(chips) for a in range(n)]
        first += [copy(a, 0, me, sibling, own=True) for a in range(n)]
        for cp in first:
            cp.start()
        passed = []
        for j, chip in enumerate(chips):
            for a in range(n):
                copy(a, 1 + j, (*chip, c), me).wait_recv()
                passed.append(copy(a, 4 + j, (*chip, c), sibling))
                passed[-1].start()
        for a in range(n):
            copy(a, 0, sibling, me).wait_recv()
        for j, chip in enumerate(chips):
            for a in range(n):
                copy(a, 4 + j, (*chip, 1 - c), me).wait_recv()
        for cp in first + passed:
            cp.wait_send()
        for cp in mine:
            cp.wait()

    return pl.pallas_call(
        body, name=name, in_specs=[ANY] * n, out_specs=[ANY] * n,
        out_shape=[jax.ShapeDtypeStruct((N_DEV,) + s.shape, s.dtype) for s in shards],
        scratch_shapes=[pltpu.SemaphoreType.DMA((n, 7)), pltpu.SemaphoreType.DMA((n, 7)), pltpu.SemaphoreType.DMA((n,))],
    )(*shards)


def _scatter_layer(parts, landing, layer, n_layers, *, name):
    n = len(parts)
    first = landing is None

    def body(*refs):
        x_refs = refs[:n]
        out_refs = refs[n:2 * n] if first else refs[2 * n:3 * n]
        send_sems, recv_sems, local_sems = refs[-3:]
        x, y, c = _position()
        me = 4 * x + 2 * y + c
        mine = [pltpu.make_async_copy(x_refs[a].at[me], out_refs[a].at[me, layer], local_sems.at[a]) for a in range(n)]
        for cp in mine:
            cp.start()
        sends, recvs = [], []
        for k in range(1, N_DEV):
            px, py, pc = (x + (k >> 2)) % 2, (y + ((k >> 1) & 1)) % 2, (c + (k & 1)) % 2
            peer = 4 * px + 2 * py + pc
            for a in range(n):
                sems = dict(send_sem=send_sems.at[a, k - 1], recv_sem=recv_sems.at[a, k - 1],
                            device_id=(px, py, pc), device_id_type=MESH)
                sends.append(pltpu.make_async_remote_copy(
                    src_ref=x_refs[a].at[peer], dst_ref=out_refs[a].at[me, layer], **sems))
                recvs.append(pltpu.make_async_remote_copy(
                    src_ref=x_refs[a].at[me], dst_ref=out_refs[a].at[peer, layer], **sems))
        for cp in sends:
            cp.start()
        for cp in recvs:
            cp.wait_recv()
        for cp in sends:
            cp.wait_send()
        for cp in mine:
            cp.wait()

    out_shape = [jax.ShapeDtypeStruct((N_DEV, n_layers) + p.shape[1:], p.dtype) for p in parts]
    scratch = [pltpu.SemaphoreType.DMA((n, 7)), pltpu.SemaphoreType.DMA((n, 7)), pltpu.SemaphoreType.DMA((n,))]
    if first:
        return pl.pallas_call(body, name=name, in_specs=[ANY] * n, out_specs=[ANY] * n, out_shape=out_shape,
                              scratch_shapes=scratch)(*parts)
    return pl.pallas_call(body, name=name, in_specs=[ANY] * (2 * n), out_specs=[ANY] * n, out_shape=out_shape,
                          input_output_aliases={n + a: a for a in range(n)}, scratch_shapes=scratch)(*parts, *landing)


def _ffn_fwd(x, gain, wg_in, wg_out, l, tag):
    T, D = x.shape
    fb, rb = wg_in.shape[-1], wg_out.shape[-2]
    tm, tn = min(T, 1024), 512
    h = _rmsnorm_fwd(x, gain, name=f"{tag}_norm")
    p = _mm(name=f"{tag}_in", grid=(T // tm, N_DEV, 1), tile=(tm, fb),
            a=h, a_spec=pl.BlockSpec((tm, D), lambda i, j, k: (i, 0)),
            b=wg_in, b_spec=pl.BlockSpec((None, None, D, fb), lambda i, j, k: (j, l, 0, 0)),
            out_shape=jax.ShapeDtypeStruct((N_DEV, T, fb), BF16), o_spec=pl.BlockSpec((None, tm, fb), lambda i, j, k: (j, i, 0)))
    a = _swiglu_fwd(p, name=f"{tag}_act")
    y = _mm(name=f"{tag}_out", grid=(T // tm, D // tn, FF_HALF), tile=(tm, tn), resid=x, scale=0.5,
            a=a, a_spec=pl.BlockSpec((None, tm, fb), lambda i, j, k: (k, i, 0)),
            b=wg_out, b_spec=pl.BlockSpec((2, None, rb, tn), lambda i, j, k: (k, l, 0, j)),
            out_shape=jax.ShapeDtypeStruct((T, D), F32), o_spec=pl.BlockSpec((tm, tn), lambda i, j, k: (i, j)))
    return y, (x, h, p, a)


def _ffn_bwd(dy, saved, gain, wg_in, wg_out, l, tag):
    x, h, p, a = saved
    T, D = x.shape
    fb, rb = wg_in.shape[-1], wg_out.shape[-2]
    tm, tn = min(T, 1024), 512
    da = _mm(name=f"{tag}_out_dx", grid=(T // tm, FF_HALF, 1), tile=(tm, fb), tb=True, scale=0.5,
             a=dy, a_spec=pl.BlockSpec((tm, D), lambda i, j, k: (i, 0)),
             b=wg_out, b_spec=pl.BlockSpec((2, None, rb, D), lambda i, j, k: (j, l, 0, 0)),
             out_shape=jax.ShapeDtypeStruct((FF_HALF, T, fb), BF16), o_spec=pl.BlockSpec((None, tm, fb), lambda i, j, k: (j, i, 0)))
    d_w_out = _mm(name=f"{tag}_out_dw", grid=(FF_HALF, D // tn, 1), tile=(fb, tn), ta=True, scale=0.5,
                  a=a, a_spec=pl.BlockSpec((None, T, fb), lambda i, j, k: (i, 0, 0)),
                  b=dy, b_spec=pl.BlockSpec((T, tn), lambda i, j, k: (0, j)),
                  out_shape=jax.ShapeDtypeStruct((FF_HALF, fb, D), BF16), o_spec=pl.BlockSpec((None, fb, tn), lambda i, j, k: (i, 0, j)))
    dp = _swiglu_bwd(da, p, name=f"{tag}_act_bwd")
    d_w_in = _mm(name=f"{tag}_in_dw", grid=(1, N_DEV, 1), tile=(D, fb), ta=True,
                 a=h, a_spec=pl.BlockSpec((T, D), lambda i, j, k: (0, 0)),
                 b=dp, b_spec=pl.BlockSpec((None, T, fb), lambda i, j, k: (j, 0, 0)),
                 out_shape=jax.ShapeDtypeStruct((N_DEV, D, fb), BF16), o_spec=pl.BlockSpec((None, D, fb), lambda i, j, k: (j, 0, 0)))
    dh = _mm(name=f"{tag}_in_dx", grid=(T // tm, D // tn, N_DEV), tile=(tm, tn), tb=True,
             a=dp, a_spec=pl.BlockSpec((None, tm, fb), lambda i, j, k: (k, i, 0)),
             b=wg_in, b_spec=pl.BlockSpec((None, None, tn, fb), lambda i, j, k: (k, l, j, 0)),
             out_shape=jax.ShapeDtypeStruct((T, D), F32), o_spec=pl.BlockSpec((tm, tn), lambda i, j, k: (i, j)))
    dx, d_gain = _rmsnorm_bwd(dh, x, gain, dy, name=f"{tag}_norm_bwd")
    return dx, d_gain, d_w_in, d_w_out.reshape(N_DEV, rb, D)


def _square_mm(a, wg, l, *, name, transposed=False, out_dtype=F32, resid=None):
    T, D = a.shape
    rb = wg.shape[-2]
    tm, tn = min(T, 1024), 512
    if transposed:
        b_spec = pl.BlockSpec((tn // rb, None, rb, D), lambda i, j, k: (j, l, 0, 0))
    else:
        b_spec = pl.BlockSpec((N_DEV, None, rb, tn), lambda i, j, k: (0, l, 0, j))
    return _mm(name=name, grid=(T // tm, D // tn, 1), tile=(tm, tn), tb=transposed, resid=resid,
               a=a, a_spec=pl.BlockSpec((tm, D), lambda i, j, k: (i, 0)), b=wg, b_spec=b_spec,
               out_shape=jax.ShapeDtypeStruct((T, D), out_dtype), o_spec=pl.BlockSpec((tm, tn), lambda i, j, k: (i, j)))


def _layer_mm(a, w, l, *, name, transposed=False, out_dtype=F32, resid=None, tn=None, tk=None):
    T = a.shape[0]
    _, K, N = w.shape
    tm = min(T, 1024)
    if transposed:
        tn, tk = tn or min(K, 512), tk or N
        b_spec = pl.BlockSpec((None, tn, tk), lambda i, j, k: (l, j, k))
        n_out, n_k = K, N // tk
    else:
        tn, tk = tn or min(N, 512), tk or K
        b_spec = pl.BlockSpec((None, tk, tn), lambda i, j, k: (l, k, j))
        n_out, n_k = N, K // tk
    return _mm(name=name, grid=(T // tm, n_out // tn, n_k), tile=(tm, tn), tb=transposed, resid=resid,
               a=a, a_spec=pl.BlockSpec((tm, tk), lambda i, j, k: (i, k)), b=w, b_spec=b_spec,
               out_shape=jax.ShapeDtypeStruct((T, n_out), out_dtype), o_spec=pl.BlockSpec((tm, tn), lambda i, j, k: (i, j)))


def _head_rows(cols, T):
    return cols.T.reshape(HEADS, T // DN_CHUNK, 1, DN_CHUNK)


def _mixer_fwd(x, w, big, l, tag):
    T = x.shape[0]
    h = _rmsnorm_fwd(x, w["mix_norm"], name=f"{tag}_norm")
    proj = _layer_mm(h, big["w_main"], l, name=f"{tag}_proj")
    scal = _layer_mm(h, big["w_scal"], l, name=f"{tag}_proj_scal", tn=N_SCAL)
    qkv = _conv_fwd(proj, big["conv_w"][l], name=f"{tag}_conv")
    b_rows = _head_rows(scal[:, 0:HEADS], T)
    a_rows = _head_rows(scal[:, HEADS:2 * HEADS], T)
    o_a, states = _dn_fwd(qkv, b_rows, a_rows, w["hp"], name=f"{tag}_dn")
    oa_n = _gated_norm_fwd(o_a, proj, w["dn_out_norm"], name=f"{tag}_dn_norm")
    ya = _square_mm(oa_n, big["w_branch_a"], l, name=f"{tag}_branch_a")
    o_b, ltot = _sb_fwd(proj, w["sb_q_norm"], w["sb_k_norm"], name=f"{tag}_sb")
    yb = _square_mm(o_b, big["w_branch_b"], l, name=f"{tag}_branch_b")
    merged = _merge_fwd(ya, yb, proj, name=f"{tag}_merge")
    y = _square_mm(merged, big["w_out"], l, name=f"{tag}_out", resid=x)
    return y, (x, h, proj, qkv, b_rows, a_rows, o_a, states, oa_n, ya, o_b, ltot, yb, merged)


def _mixer_bwd(dy, saved, w, big, l, tag):
    x, h, proj, qkv, b_rows, a_rows, o_a, states, oa_n, ya, o_b, ltot, yb, merged = saved
    T = x.shape[0]
    g = {}
    d_merged = _square_mm(dy, big["w_out"], l, transposed=True, name=f"{tag}_out_dx", out_dtype=BF16)
    g["w_out"] = _matmul(merged, dy, ta=True, name=f"{tag}_out_dw", out_dtype=BF16)
    d_ya, d_yb, d_ga, d_gb = _merge_bwd(d_merged, ya, yb, proj, name=f"{tag}_merge_bwd")
    d_oan = _square_mm(d_ya, big["w_branch_a"], l, transposed=True, name=f"{tag}_branch_a_dx")
    g["w_branch_a"] = _matmul(oa_n, d_ya, ta=True, name=f"{tag}_branch_a_dw", out_dtype=BF16)
    d_ob = _square_mm(d_yb, big["w_branch_b"], l, transposed=True, name=f"{tag}_branch_b_dx")
    g["w_branch_b"] = _matmul(o_b, d_yb, ta=True, name=f"{tag}_branch_b_dw", out_dtype=BF16)
    d_oa, d_z, g["dn_out_norm"] = _gated_norm_bwd(d_oan, o_a, proj, w["dn_out_norm"], name=f"{tag}_dn_norm_bwd")
    d_qkv, d_b_rows, d_a_rows, d_hp = _dn_bwd(qkv, b_rows, a_rows, w["hp"], states, d_oa, name=f"{tag}_dn_bwd")
    g["dn_a_log"] = d_hp[:, 0, 0]
    g["dn_dt_bias"] = d_hp[:, 1, 0]
    d_conv_in, g["conv_w"] = _conv_bwd(d_qkv, proj, big["conv_w"][l], name=f"{tag}_conv_bwd")
    d_sbq, d_sbk, d_sbv, g["sb_q_norm"], g["sb_k_norm"] = _sb_bwd(
        proj, w["sb_q_norm"], w["sb_k_norm"], ltot, d_ob, name=f"{tag}_sb_bwd")
    d_proj = jnp.concatenate([d_conv_in, d_z, d_sbq, d_sbk, d_sbv, d_ga, d_gb], axis=1)
    d_scal = jnp.concatenate([d_b_rows.reshape(HEADS, T).T, d_a_rows.reshape(HEADS, T).T,
                              jnp.zeros((T, N_SCAL - 2 * HEADS), F32)], axis=1).astype(BF16)
    g["w_main"] = _matmul(h, d_proj, ta=True, name=f"{tag}_proj_dw", out_dtype=BF16)
    g["w_scal"] = _matmul(h, d_scal, ta=True, name=f"{tag}_proj_scal_dw", out_dtype=BF16, tn=N_SCAL)
    dh_scal = _layer_mm(d_scal, big["w_scal"], l, transposed=True, name=f"{tag}_proj_scal_dx")
    dh = _layer_mm(d_proj, big["w_main"], l, transposed=True, name=f"{tag}_proj_dx", tk=N_MAIN // 4, resid=dh_scal)
    dx, g["mix_norm"] = _rmsnorm_bwd(dh, x, w["mix_norm"], dy, name=f"{tag}_norm_bwd")
    return dx, g


def _local_step(x, target, layers, big, on_layer_grads):
    saved = []
    for l, w in enumerate(layers):
        x, s1 = _ffn_fwd(x, w["ffn1_norm"], big["ffn1_w_in"], big["ffn1_w_out"], l, f"l{l}_ffn1")
        x, s2 = _mixer_fwd(x, w, big, l, f"l{l}_mix")
        x, s3 = _ffn_fwd(x, w["ffn2_norm"], big["ffn2_w_in"], big["ffn2_w_out"], l, f"l{l}_ffn2")
        saved.append((s1, s2, s3))
    loss, dx = _loss_head(x, target, name="loss_head")
    for l in reversed(range(len(layers))):
        w = layers[l]
        s1, s2, s3 = saved[l]
        dx, g_n2, g_in2, g_out2 = _ffn_bwd(dx, s3, w["ffn2_norm"], big["ffn2_w_in"], big["ffn2_w_out"], l, f"l{l}_ffn2")
        dx, g = _mixer_bwd(dx, s2, w, big, l, f"l{l}_mix")
        dx, g_n1, g_in1, g_out1 = _ffn_bwd(dx, s1, w["ffn1_norm"], big["ffn1_w_in"], big["ffn1_w_out"], l, f"l{l}_ffn1")
        g.update(ffn1_norm=g_n1, ffn1_w_in=g_in1, ffn1_w_out=g_out1, ffn2_norm=g_n2, ffn2_w_in=g_in2, ffn2_w_out=g_out2)
        on_layer_grads(l, g)
    return loss, dx


_BIG = ("ffn1_w_in", "ffn1_w_out", "w_in", "w_branch_a", "w_branch_b", "w_out", "ffn2_w_in", "ffn2_w_out")
_SMALL = ("ffn1_norm", "mix_norm", "ffn2_norm", "dn_a_log", "dn_dt_bias", "dn_out_norm", "sb_q_norm", "sb_k_norm")
_ORDER = ("ffn1_norm", "ffn1_w_in", "ffn1_w_out", "mix_norm", "w_in", "dn_conv_w", "dn_a_log", "dn_dt_bias", "dn_out_norm",
          "sb_q_norm", "sb_k_norm", "w_branch_a", "w_branch_b", "w_out", "ffn2_norm", "ffn2_w_in", "ffn2_w_out")
COL_SCAL = 4 * D_MODEL


def _pad_rows(a, multiple):
    pad = (-a.shape[-2]) % multiple
    return a if pad == 0 else jnp.pad(a, [(0, 0)] * (a.ndim - 2) + [(0, pad), (0, 0)])


def _lane_rows(a):
    flat = a.reshape(-1)
    flat = jnp.pad(flat, (0, (-flat.shape[0]) % 128))
    return flat.reshape(-1, 128)


def _pack_small(named):
    pieces, spans, r = [], {}, 0
    for n, a in named:
        rows = _lane_rows(a)
        spans[n] = (r, r + rows.shape[0], a.shape)
        r += rows.shape[0]
        pieces.append(rows)
    return _pad_rows(jnp.concatenate(pieces, axis=0), 8), spans


def _unpack_small(packed, spans, n):
    r0, r1, shape = spans[n]
    return packed[r0:r1].reshape(-1)[:math.prod(shape)].reshape(shape)


def kernel(x, ffn1_norm, ffn1_w_in, ffn1_w_out, mix_norm, w_in, dn_conv_w, dn_a_log, dn_dt_bias, dn_out_norm, sb_q_norm, sb_k_norm, w_branch_a, w_branch_b, w_out, ffn2_norm, ffn2_w_in, ffn2_w_out, loss_target, m_ffn1_norm, m_ffn1_w_in, m_ffn1_w_out, m_mix_norm, m_w_in, m_dn_conv_w, m_dn_a_log, m_dn_dt_bias, m_dn_out_norm, m_sb_q_norm, m_sb_k_norm, m_w_branch_a, m_w_branch_b, m_w_out, m_ffn2_norm, m_ffn2_w_in, m_ffn2_w_out, v_ffn1_norm, v_ffn1_w_in, v_ffn1_w_out, v_mix_norm, v_w_in, v_dn_conv_w, v_dn_a_log, v_dn_dt_bias, v_dn_out_norm, v_sb_q_norm, v_sb_k_norm, v_w_branch_a, v_w_branch_b, v_w_out, v_ffn2_norm, v_ffn2_w_in, v_ffn2_w_out):
    given = dict(locals())
    weights = {n: given[n] for n in _ORDER}
    mom_m = {n: given["m_" + n] for n in _ORDER}
    mom_v = {n: given["v_" + n] for n in _ORDER}
    L = ffn1_norm.shape[0]
    ax, ay, ac = _position()
    my_slot = 4 * ax + 2 * ay + ac

    conv_cols = dn_conv_w.shape[-1]
    gathered = _all_gather([weights[n].astype(BF16) for n in _BIG] + [_pad_rows(_lane_rows(dn_conv_w), 8)],
                           name="gather_weights")
    big = dict(zip(_BIG, gathered[:-1]))
    wi = big.pop("w_in").transpose(1, 2, 0, 3).reshape(L, D_MODEL, N_IN)
    big["w_main"] = jnp.concatenate([wi[..., :COL_SCAL], wi[..., COL_SCAL + 2 * HEADS:]], axis=-1)
    big["w_scal"] = jnp.pad(wi[..., COL_SCAL:COL_SCAL + 2 * HEADS], ((0, 0), (0, 0), (0, N_SCAL - 2 * HEADS)))
    conv_full = gathered[-1].reshape(N_DEV, -1)[:, :L * DN_CONV * conv_cols].reshape(N_DEV, L, DN_CONV, conv_cols)
    big["conv_w"] = conv_full.transpose(1, 2, 0, 3).reshape(L, DN_CONV, N_DEV * conv_cols)

    layers = []
    for l in range(L):
        hp = jnp.concatenate([jnp.broadcast_to(dn_a_log[l][:, None, None], (HEADS, 1, 128)),
                              jnp.broadcast_to(dn_dt_bias[l][:, None, None], (HEADS, 1, 128)),
                              jnp.zeros((HEADS, 6, 128), F32)], axis=1)
        layers.append(dict(ffn1_norm=ffn1_norm[l][None], mix_norm=mix_norm[l][None], hp=hp,
                           dn_out_norm=dn_out_norm[l][None], sb_q_norm=sb_q_norm[l][None],
                           sb_k_norm=sb_k_norm[l][None], ffn2_norm=ffn2_norm[l][None]))

    grads = [None] * L
    landing = [None]

    def on_layer_grads(l, g):
        grads[l] = g
        g_w_in = jnp.concatenate([g["w_main"][:, :COL_SCAL], g["w_scal"][:, :2 * HEADS], g["w_main"][:, COL_SCAL:]], axis=1)
        parts = dict(g, w_in=g_w_in.reshape(D_MODEL, N_DEV, N_IN // N_DEV).transpose(1, 0, 2))
        for n in ("w_branch_a", "w_branch_b", "w_out"):
            parts[n] = g[n].reshape(N_DEV, D_MODEL // N_DEV, D_MODEL)
        landing[0] = _scatter_layer([parts[n] for n in _BIG], landing[0], l, L, name=f"scatter_grads_l{l}")

    loss_row, dx = _local_step(x[0], loss_target[0], layers, big, on_layer_grads)
    loss = lax.psum(loss_row[0, 0], ("x", "y", "c"))

    out = {}
    for n, incoming in zip(_BIG, landing[0]):
        _, a, b = weights[n].shape
        res = _adamw(incoming.reshape(N_DEV, L * a, b), weights[n].reshape(L * a, b), mom_m[n].reshape(L * a, b),
                     mom_v[n].reshape(L * a, b), name=f"adamw_{n}")
        out[n] = tuple(t.reshape(L, a, b) for t in res)

    small_grads = [(n, jnp.stack([g[n].reshape(weights[n].shape[1:]) for g in grads])) for n in _SMALL]
    small_packed, spans = _pack_small(small_grads + [("conv", jnp.stack([g["conv_w"] for g in grads]))])
    small_sum = _sum_parts(_all_gather([small_packed], name="gather_small_grads")[0], name="sum_small_grads")
    rep_rows = spans["conv"][0]
    pack_rep = lambda d: _pad_rows(_pack_small([(n, d[n]) for n in _SMALL])[0], 8)
    rep_pad = (-rep_rows) % 8
    g_rep = jnp.pad(small_sum[:rep_rows], ((0, rep_pad), (0, 0)))
    res = _adamw(g_rep[None], pack_rep(weights), pack_rep(mom_m), pack_rep(mom_v), name="adamw_replicated")
    for n in _SMALL:
        out[n] = tuple(_unpack_small(t, spans, n) for t in res)
    conv_sum = _unpack_small(small_sum, spans, "conv")
    conv_mine = lax.dynamic_slice_in_dim(conv_sum, my_slot * conv_cols, conv_cols, axis=2).reshape(L * DN_CONV, conv_cols)
    flat = lambda t: t.reshape(L * DN_CONV, conv_cols)
    res = _adamw(conv_mine[None], flat(dn_conv_w), flat(m_dn_conv_w), flat(v_dn_conv_w), name="adamw_conv")
    out["dn_conv_w"] = tuple(t.reshape(L, DN_CONV, conv_cols) for t in res)

    return (loss, dx[None], *[out[n][0] for n in _ORDER], *[out[n][1] for n in _ORDER],
            *[out[n][2] for n in _ORDER], *[out[n][3] for n in _ORDER])
```

```python
import functools
import math

import jax
import jax.numpy as jnp
from jax import lax
from jax.experimental import pallas as pl
from jax.experimental.pallas import tpu as pltpu

F32 = jnp.float32
BF16 = jnp.bfloat16

N_DEV = 8
D_MODEL = 1024
DEPTH = 4
D_FF = 2816
HEADS = 8
HEAD_DIM = 128
DN_CHUNK = 64
DN_CONV = 4
DN_UNROLL = 4
SB_BLOCK = 128
SB_KEY_TILE = 512
RMS_EPS = 1e-6
L2_EPS = 1e-6
N_IN = 9232
N_MAIN = 9216
N_SCAL = 128
QK_SCALE = HEAD_DIM ** -0.5

ADAM_LR = 0.001
ADAM_B1 = 0.9
ADAM_B2 = 0.999
ADAM_EPS = 1e-08
ADAM_WD = 0.01
ADAM_STEP = 10

V7X_VMEM_LIMIT = 56 * 1024 * 1024
MESH = pl.DeviceIdType.MESH
ANY = pl.BlockSpec(memory_space=pl.ANY)


def _params(sem=None, vmem=V7X_VMEM_LIMIT):
    return pltpu.CompilerParams(dimension_semantics=sem, vmem_limit_bytes=vmem)


def _sigmoid(x):
    return 1.0 / (1.0 + jnp.exp(-x))


def _softplus(x):
    return jnp.maximum(x, 0.0) + jnp.log(1.0 + jnp.exp(-jnp.abs(x)))


def _bdot(a, b, dims=(((1,), (0,)), ((), ()))):
    return lax.dot_general(a.astype(BF16), b.astype(BF16), dims, preferred_element_type=F32)


_NT = (((1,), (1,)), ((), ()))
_TN = (((0,), (0,)), ((), ()))


def _hdot(a, b, dims=(((1,), (0,)), ((), ()))):
    a_hi = a.astype(BF16)
    b_hi = b.astype(BF16)
    a_lo = (a - a_hi.astype(F32)).astype(BF16)
    b_lo = (b - b_hi.astype(F32)).astype(BF16)
    dot = functools.partial(lax.dot_general, dimension_numbers=dims, preferred_element_type=F32)
    return dot(a_hi, b_hi) + (dot(a_hi, b_lo) + dot(a_lo, b_hi))


def _hdot_tn(a, b):
    return _hdot(a, b, _TN)


def _mm(*, name, grid, a, a_spec, b, b_spec, out_shape, o_spec, tile, ta=False, tb=False, resid=None, scale=1.0):
    nk = grid[2]
    dims = (((0 if ta else 1,), (1 if tb else 0,)), ((), ()))

    def flat(v):
        return v if v.ndim == 2 else v.reshape(-1, v.shape[-1])

    def body(*refs):
        a_ref, b_ref = refs[:2]
        r_ref = refs[2] if resid is not None else None
        o_ref = refs[3] if resid is not None else refs[2]
        part = lax.dot_general(flat(a_ref[...]).astype(BF16), flat(b_ref[...]).astype(BF16), dims,
                               preferred_element_type=F32)

        def finish(acc):
            if scale != 1.0:
                acc = acc * scale
            if r_ref is not None:
                acc = r_ref[...] + acc
            o_ref[...] = acc.astype(o_ref.dtype)

        if nk == 1:
            finish(part)
        else:
            acc_ref = refs[-1]
            k = pl.program_id(2)

            @pl.when(k == 0)
            def _():
                acc_ref[...] = part

            @pl.when(k > 0)
            def _():
                acc_ref[...] += part

            @pl.when(k == nk - 1)
            def _():
                finish(acc_ref[...])

    in_specs = [a_spec, b_spec] + ([pl.BlockSpec(tile, lambda i, j, k: (i, j))] if resid is not None else [])
    args = (a, b) + ((resid,) if resid is not None else ())
    return pl.pallas_call(
        body, name=name, grid=grid, in_specs=in_specs, out_specs=o_spec, out_shape=out_shape,
        scratch_shapes=[pltpu.VMEM(tile, F32)] if nk > 1 else [],
        compiler_params=_params(("parallel", "parallel", "arbitrary")),
    )(*args)


def _matmul(a, b, *, name, ta=False, tb=False, out_dtype=F32, tm=None, tn=None, tk=None, resid=None, scale=1.0):
    if ta:
        K, M = a.shape
    else:
        M, K = a.shape
    N = b.shape[0] if tb else b.shape[1]
    tm = tm or min(M, 1024)
    tn = tn or min(N, 512)
    tk = tk or K
    assert M % tm == 0 and N % tn == 0 and K % tk == 0, (name, M, N, K, tm, tn, tk)
    a_spec = pl.BlockSpec((tk, tm), lambda i, j, k: (k, i)) if ta else pl.BlockSpec((tm, tk), lambda i, j, k: (i, k))
    b_spec = pl.BlockSpec((tn, tk), lambda i, j, k: (j, k)) if tb else pl.BlockSpec((tk, tn), lambda i, j, k: (k, j))
    return _mm(name=name, grid=(M // tm, N // tn, K // tk), a=a, a_spec=a_spec, b=b, b_spec=b_spec,
               out_shape=jax.ShapeDtypeStruct((M, N), out_dtype), o_spec=pl.BlockSpec((tm, tn), lambda i, j, k: (i, j)),
               tile=(tm, tn), ta=ta, tb=tb, resid=resid, scale=scale)


ROW_TILE = 256


def _rmsnorm_fwd(x, gain, *, name):
    T, D = x.shape

    def body(x_ref, g_ref, o_ref):
        xf = x_ref[...]
        r = lax.rsqrt(jnp.mean(xf * xf, axis=-1, keepdims=True) + RMS_EPS)
        o_ref[...] = (xf * r * g_ref[...]).astype(o_ref.dtype)

    return pl.pallas_call(
        body, name=name, grid=(T // ROW_TILE,),
        in_specs=[pl.BlockSpec((ROW_TILE, D), lambda i: (i, 0)), pl.BlockSpec((1, D), lambda i: (0, 0))],
        out_specs=pl.BlockSpec((ROW_TILE, D), lambda i: (i, 0)),
        out_shape=jax.ShapeDtypeStruct((T, D), BF16), compiler_params=_params(("parallel",)),
    )(x, gain)


def _rmsnorm_bwd(dh, x, gain, dres, *, name):
    T, D = x.shape

    def body(dh_ref, x_ref, g_ref, res_ref, dx_ref, dg_ref):
        xf = x_ref[...]
        r = lax.rsqrt(jnp.mean(xf * xf, axis=-1, keepdims=True) + RMS_EPS)
        y = xf * r
        dh_v = dh_ref[...].astype(F32)
        dy = dh_v * g_ref[...]
        dx_ref[...] = res_ref[...] + r * (dy - y * jnp.mean(dy * y, axis=-1, keepdims=True))

        @pl.when(pl.program_id(0) == 0)
        def _():
            dg_ref[...] = jnp.zeros_like(dg_ref)

        dg_ref[...] += jnp.sum(dh_v * y, axis=0, keepdims=True)

    row = pl.BlockSpec((ROW_TILE, D), lambda i: (i, 0))
    vec = pl.BlockSpec((1, D), lambda i: (0, 0))
    return pl.pallas_call(
        body, name=name, grid=(T // ROW_TILE,), in_specs=[row, row, vec, row], out_specs=(row, vec),
        out_shape=(jax.ShapeDtypeStruct((T, D), F32), jax.ShapeDtypeStruct((1, D), F32)),
        compiler_params=_params(("arbitrary",)),
    )(dh, x, gain, dres)


FF_HALF = N_DEV // 2


def _swiglu_fwd(p, *, name):
    _, T, fb = p.shape

    def body(g_ref, u_ref, o_ref):
        g = g_ref[...].astype(F32)
        o_ref[...] = (g * _sigmoid(g) * u_ref[...].astype(F32)).astype(o_ref.dtype)

    blk = (None, ROW_TILE, fb)
    return pl.pallas_call(
        body, name=name, grid=(T // ROW_TILE, FF_HALF),
        in_specs=[pl.BlockSpec(blk, lambda i, j: (j, i, 0)), pl.BlockSpec(blk, lambda i, j: (j + FF_HALF, i, 0))],
        out_specs=pl.BlockSpec(blk, lambda i, j: (j, i, 0)),
        out_shape=jax.ShapeDtypeStruct((FF_HALF, T, fb), BF16), compiler_params=_params(("parallel", "parallel")),
    )(p, p)


def _swiglu_bwd(da, p, *, name):
    _, T, fb = p.shape

    def body(da_ref, g_ref, u_ref, o_ref):
        g = g_ref[...].astype(F32)
        u = u_ref[...].astype(F32)
        d = da_ref[...].astype(F32)
        s = _sigmoid(g)
        dgate = d * u * (s * (1.0 + g * (1.0 - s)))
        dup = d * g * s
        o_ref[...] = jnp.where(pl.program_id(1) < FF_HALF, dgate, dup).astype(o_ref.dtype)

    blk = (None, ROW_TILE, fb)
    return pl.pallas_call(
        body, name=name, grid=(T // ROW_TILE, 2 * FF_HALF),
        in_specs=[pl.BlockSpec(blk, lambda i, j: (j % FF_HALF, i, 0)), pl.BlockSpec(blk, lambda i, j: (j % FF_HALF, i, 0)),
                  pl.BlockSpec(blk, lambda i, j: (j % FF_HALF + FF_HALF, i, 0))],
        out_specs=pl.BlockSpec(blk, lambda i, j: (j, i, 0)),
        out_shape=jax.ShapeDtypeStruct((2 * FF_HALF, T, fb), BF16), compiler_params=_params(("parallel", "parallel")),
    )(da, p, p)


COL_GATE_A = 7
COL_GATE_B = 8


def _merge_fwd(ya, yb, proj, *, name):
    T, D = ya.shape

    def body(ya_ref, yb_ref, ga_ref, gb_ref, o_ref):
        o_ref[...] = (_sigmoid(ga_ref[...]) * ya_ref[...] + _sigmoid(gb_ref[...]) * yb_ref[...]).astype(o_ref.dtype)

    row = pl.BlockSpec((ROW_TILE, D), lambda i: (i, 0))
    return pl.pallas_call(
        body, name=name, grid=(T // ROW_TILE,),
        in_specs=[row, row, pl.BlockSpec((ROW_TILE, D), lambda i: (i, COL_GATE_A)),
                  pl.BlockSpec((ROW_TILE, D), lambda i: (i, COL_GATE_B))],
        out_specs=row, out_shape=jax.ShapeDtypeStruct((T, D), BF16), compiler_params=_params(("parallel",)),
    )(ya, yb, proj, proj)


def _merge_bwd(dm, ya, yb, proj, *, name):
    T, D = ya.shape

    def body(dm_ref, ya_ref, yb_ref, ga_ref, gb_ref, dya_ref, dyb_ref, dga_ref, dgb_ref):
        d = dm_ref[...].astype(F32)
        sa = _sigmoid(ga_ref[...])
        sb = _sigmoid(gb_ref[...])
        dya_ref[...] = (d * sa).astype(BF16)
        dyb_ref[...] = (d * sb).astype(BF16)
        dga_ref[...] = (d * ya_ref[...] * sa * (1.0 - sa)).astype(BF16)
        dgb_ref[...] = (d * yb_ref[...] * sb * (1.0 - sb)).astype(BF16)

    row = pl.BlockSpec((ROW_TILE, D), lambda i: (i, 0))
    out = jax.ShapeDtypeStruct((T, D), BF16)
    return pl.pallas_call(
        body, name=name, grid=(T // ROW_TILE,),
        in_specs=[row, row, row, pl.BlockSpec((ROW_TILE, D), lambda i: (i, COL_GATE_A)),
                  pl.BlockSpec((ROW_TILE, D), lambda i: (i, COL_GATE_B))],
        out_specs=(row, row, row, row), out_shape=(out, out, out, out), compiler_params=_params(("parallel",)),
    )(dm, ya, yb, proj, proj)


def _loss_head(y, target, *, name):
    T, D = y.shape

    def body(y_ref, t_ref, loss_ref, dy_ref):
        err = y_ref[...] - t_ref[...]
        dy_ref[...] = err * (1.0 / D)

        @pl.when(pl.program_id(0) == 0)
        def _():
            loss_ref[...] = jnp.zeros_like(loss_ref)

        loss_ref[...] += 0.5 * jnp.sum(jnp.sum(err * err, axis=-1, keepdims=True) * (1.0 / D), axis=0, keepdims=True)

    row = pl.BlockSpec((ROW_TILE, D), lambda i: (i, 0))
    return pl.pallas_call(
        body, name=name, grid=(T // ROW_TILE,), in_specs=[row, row],
        out_specs=(pl.BlockSpec((1, 128), lambda i: (0, 0)), row),
        out_shape=(jax.ShapeDtypeStruct((1, 128), F32), jax.ShapeDtypeStruct((T, D), F32)),
        compiler_params=_params(("arbitrary",)),
    )(y, target)


CONV_PAD = 8


def _conv_taps(w, xp, T, first):
    acc = w[0:1, :] * xp[pl.ds(first, T), :]
    for i in range(1, DN_CONV):
        acc = acc + w[i:i + 1, :] * xp[pl.ds(first + i, T), :]
    return acc


def _conv_fwd(proj, conv_w, *, name):
    T = proj.shape[0]

    def body(x_ref, w_ref, o_ref, xp):
        xp[0:CONV_PAD, :] = jnp.zeros((CONV_PAD, HEAD_DIM), F32)
        xp[CONV_PAD:, :] = x_ref[...]
        y = _conv_taps(w_ref[...], xp, T, CONV_PAD - (DN_CONV - 1))
        s = y * _sigmoid(y)
        n = s * lax.rsqrt(jnp.sum(s * s, axis=-1, keepdims=True) + L2_EPS)
        o_ref[0] = jnp.where(pl.program_id(0) < 2, n, s)

    return pl.pallas_call(
        body, name=name, grid=(3, HEADS),
        in_specs=[pl.BlockSpec((T, HEAD_DIM), lambda c, h: (0, c * HEADS + h)),
                  pl.BlockSpec((DN_CONV, HEAD_DIM), lambda c, h: (0, c * HEADS + h))],
        out_specs=pl.BlockSpec((1, T, HEAD_DIM), lambda c, h: (c, 0, h)),
        out_shape=jax.ShapeDtypeStruct((3, T, D_MODEL), F32),
        scratch_shapes=[pltpu.VMEM((T + CONV_PAD, HEAD_DIM), F32)],
        compiler_params=_params(("parallel", "parallel")),
    )(proj, conv_w)


def _conv_bwd(dqkv, proj, conv_w, *, name):
    T = proj.shape[0]

    def body(d_ref, x_ref, w_ref, dx_ref, dw_ref, xp, dyp):
        xp[0:CONV_PAD, :] = jnp.zeros((CONV_PAD, HEAD_DIM), F32)
        xp[CONV_PAD:, :] = x_ref[...]
        w = w_ref[...]
        y = _conv_taps(w, xp, T, CONV_PAD - (DN_CONV - 1))
        sg = _sigmoid(y)
        s = y * sg
        r = lax.rsqrt(jnp.sum(s * s, axis=-1, keepdims=True) + L2_EPS)
        n = s * r
        d = d_ref[0]
        ds = jnp.where(pl.program_id(0) < 2, r * (d - n * jnp.sum(d * n, axis=-1, keepdims=True)), d)
        dy = ds * (sg * (1.0 + y * (1.0 - sg)))
        dyp[0:T, :] = dy
        dyp[T:, :] = jnp.zeros((CONV_PAD, HEAD_DIM), F32)
        dx = w[0:1, :] * dyp[pl.ds(DN_CONV - 1, T), :]
        for i in range(1, DN_CONV):
            dx = dx + w[i:i + 1, :] * dyp[pl.ds(DN_CONV - 1 - i, T), :]
        dx_ref[...] = dx.astype(dx_ref.dtype)
        for i in range(DN_CONV):
            dw_ref[i:i + 1, :] = jnp.sum(dy * xp[pl.ds(CONV_PAD - (DN_CONV - 1) + i, T), :], axis=0, keepdims=True)

    col = lambda c, h: (0, c * HEADS + h)
    return pl.pallas_call(
        body, name=name, grid=(3, HEADS),
        in_specs=[pl.BlockSpec((1, T, HEAD_DIM), lambda c, h: (c, 0, h)), pl.BlockSpec((T, HEAD_DIM), col),
                  pl.BlockSpec((DN_CONV, HEAD_DIM), col)],
        out_specs=(pl.BlockSpec((T, HEAD_DIM), col), pl.BlockSpec((DN_CONV, HEAD_DIM), col)),
        out_shape=(jax.ShapeDtypeStruct((T, 3 * D_MODEL), BF16), jax.ShapeDtypeStruct((DN_CONV, 3 * D_MODEL), F32)),
        scratch_shapes=[pltpu.VMEM((T + CONV_PAD, HEAD_DIM), F32), pltpu.VMEM((T + CONV_PAD, HEAD_DIM), F32)],
        compiler_params=_params(("parallel", "parallel")),
    )(dqkv, proj, conv_w)


def _inv_unit_lower(low, eye):
    x = eye - low
    power = _hdot(low, low)
    steps = int(math.log2(DN_CHUNK)) - 1
    for s in range(steps):
        x = x + _hdot(x, power)
        if s + 1 < steps:
            power = _hdot(power, power)
    return x


def _dn_chunk_setup(q_ref, k_ref, v_ref, b_ref, a_ref, hp_ref, n):
    C = DN_CHUNK
    r0 = pl.multiple_of(n * C, C)
    q = q_ref[0, pl.ds(r0, C), :] * QK_SCALE
    k = k_ref[1, pl.ds(r0, C), :]
    v = v_ref[2, pl.ds(r0, C), :]
    ii = lax.broadcasted_iota(jnp.int32, (C, C), 0)
    jj = lax.broadcasted_iota(jnp.int32, (C, C), 1)
    eye_mask = ii == jj
    eye = jnp.where(eye_mask, 1.0, 0.0).astype(F32)

    def to_col(row):
        return jnp.sum(jnp.where(eye_mask, jnp.broadcast_to(row, (C, C)), 0.0), axis=1, keepdims=True)

    def to_row(col):
        return jnp.sum(jnp.where(eye_mask, jnp.broadcast_to(col, (C, C)), 0.0), axis=0, keepdims=True)

    b_row = b_ref[0, n]
    a_row = a_ref[0, n]
    a_log = hp_ref[0, 0:1, 0:C]
    dt_b = hp_ref[0, 1:2, 0:C]
    beta_row = _sigmoid(b_row)
    neg_ea = -jnp.exp(a_log)
    g_row = neg_ea * _softplus(a_row + dt_b)
    gc_col = jnp.sum(jnp.where(jj <= ii, jnp.broadcast_to(g_row, (C, C)), 0.0), axis=1, keepdims=True)
    gc_row = to_row(gc_col)
    g_last = jnp.sum(g_row, axis=1, keepdims=True)
    beta = to_col(beta_row)
    low_incl = ii >= jj
    decay = jnp.exp(jnp.where(low_incl, gc_col - gc_row, -jnp.inf))
    eg = jnp.exp(gc_col)
    egl = jnp.exp(g_last - gc_col)
    el = jnp.exp(g_last)
    kb = k * beta
    pmat = _bdot(kb, k, _NT)
    low = jnp.where(ii > jj, pmat * decay, 0.0)
    tinv = _inv_unit_lower(low, eye)
    u = _hdot(tinv, v * beta)
    w = _hdot(tinv, kb * eg)
    qk = _bdot(q, k, _NT)
    attn = qk * decay
    return dict(q=q, k=k, v=v, ii=ii, jj=jj, to_col=to_col, to_row=to_row, b_row=b_row, a_row=a_row, dt_b=dt_b,
                beta_row=beta_row, neg_ea=neg_ea, g_row=g_row, gc_col=gc_col, g_last=g_last, beta=beta,
                decay=decay, eg=eg, egl=egl, el=el, kb=kb, pmat=pmat, tinv=tinv, u=u, w=w, qk=qk, attn=attn,
                qd=q * eg, kd=k * egl, r0=r0)


def _dn_specs(T):
    nc = T // DN_CHUNK
    qkv = pl.BlockSpec((3, T, HEAD_DIM), lambda h: (0, 0, h))
    rows = pl.BlockSpec((1, nc, 1, DN_CHUNK), lambda h: (h, 0, 0, 0))
    hp = pl.BlockSpec((1, 8, 128), lambda h: (h, 0, 0))
    states = pl.BlockSpec((1, nc, HEAD_DIM, HEAD_DIM), lambda h: (h, 0, 0, 0))
    return nc, qkv, rows, hp, states


def _dn_fwd(qkv, b_rows, a_rows, hp, *, name):
    T = qkv.shape[1]
    nc, qkv_spec, row_spec, hp_spec, st_spec = _dn_specs(T)
    unroll = math.gcd(nc, DN_UNROLL)

    def body(qkv_ref, b_ref, a_ref, hp_ref, o_ref, st_ref, s_scr):
        s_scr[...] = jnp.zeros_like(s_scr)

        def step(t, carry):
            chunks = [_dn_chunk_setup(qkv_ref, qkv_ref, qkv_ref, b_ref, a_ref, hp_ref, t * unroll + s)
                      for s in range(unroll)]
            state = s_scr[...]
            for s, c in enumerate(chunks):
                st_ref[0, t * unroll + s] = state
                v_new = c["u"] - _bdot(c["w"], state)
                o_ref[pl.ds(c["r0"], DN_CHUNK), :] = _bdot(c["qd"], state) + _bdot(c["attn"], v_new)
                state = state * c["el"] + _bdot(c["kd"], v_new, _TN)
            s_scr[...] = state
            return carry

        lax.fori_loop(0, nc // unroll, step, 0)

    return pl.pallas_call(
        body, name=name, grid=(HEADS,), in_specs=[qkv_spec, row_spec, row_spec, hp_spec],
        out_specs=(pl.BlockSpec((T, HEAD_DIM), lambda h: (0, h)), st_spec),
        out_shape=(jax.ShapeDtypeStruct((T, D_MODEL), F32),
                   jax.ShapeDtypeStruct((HEADS, nc, HEAD_DIM, HEAD_DIM), F32)),
        scratch_shapes=[pltpu.VMEM((HEAD_DIM, HEAD_DIM), F32)], compiler_params=_params(("parallel",)),
    )(qkv, b_rows, a_rows, hp)


def _dn_bwd(qkv, b_rows, a_rows, hp, states, do, *, name):
    T = qkv.shape[1]
    C = DN_CHUNK
    nc, qkv_spec, row_spec, hp_spec, st_spec = _dn_specs(T)
    unroll = math.gcd(nc, DN_UNROLL)

    def body(qkv_ref, b_ref, a_ref, hp_ref, st_ref, do_ref, dqkv_ref, db_ref, da_ref, dhp_ref, ds_scr, acc_scr):
        ds_scr[...] = jnp.zeros_like(ds_scr)
        acc_scr[...] = jnp.zeros_like(acc_scr)

        def step(t, carry):
            order = [nc - 1 - (t * unroll + s) for s in range(unroll)]
            chunks = []
            for n in order:
                c = _dn_chunk_setup(qkv_ref, qkv_ref, qkv_ref, b_ref, a_ref, hp_ref, n)
                c["n"] = n
                c["state"] = st_ref[0, n]
                c["d_o"] = do_ref[pl.ds(c["r0"], C), :]
                c["v_new"] = c["u"] - _bdot(c["w"], c["state"])
                c["d_vnew_local"] = _bdot(c["attn"], c["d_o"], _TN)
                c["d_state_local"] = _bdot(c["qd"], c["d_o"], _TN)
                chunks.append(c)
            d_state = ds_scr[...]
            for c in chunks:
                c["d_vnew"] = c["d_vnew_local"] + _bdot(c["kd"], d_state)
                c["d_kd"] = _bdot(c["v_new"], d_state, _NT)
                c["d_el"] = jnp.sum(jnp.sum(d_state * c["state"], axis=1, keepdims=True), axis=0, keepdims=True)
                d_state = d_state * c["el"] + c["d_state_local"] - _bdot(c["w"], c["d_vnew"], _TN)
            ds_scr[...] = d_state
            for c in chunks:
                chunk_grads(c)
            return carry

        def chunk_grads(c):
            n = c["n"]
            ii, jj = c["ii"], c["jj"]
            q, k, v, kb, beta = c["q"], c["k"], c["v"], c["kb"], c["beta"]
            decay, eg, egl, el = c["decay"], c["eg"], c["egl"], c["el"]
            u, w, tinv = c["u"], c["w"], c["tinv"]
            state, d_o, v_new, d_vnew, d_kd, d_el = c["state"], c["d_o"], c["v_new"], c["d_vnew"], c["d_kd"], c["d_el"]
            d_qd = _bdot(d_o, state, _NT)
            d_attn = _bdot(d_o, v_new, _NT)
            d_w = -_bdot(d_vnew, state, _NT)
            d_rv = _hdot_tn(tinv, d_vnew)
            d_rw = _hdot_tn(tinv, d_w)
            d_amat = -(_bdot(d_rv, u, _NT) + _bdot(d_rw, w, _NT))
            d_low = jnp.where(ii > jj, d_amat, 0.0)
            d_p = d_low * decay
            d_qk = d_attn * decay
            e_mat = (d_low * c["pmat"] + d_attn * c["qk"]) * decay
            d_q = _bdot(d_qk, k) + d_qd * eg
            d_kb = _bdot(d_p, k) + d_rw * eg
            d_k = _bdot(d_qk, q, _TN) + _bdot(d_p, kb, _TN) + d_kd * egl + d_kb * beta
            d_beta = jnp.sum(d_kb * k, axis=1, keepdims=True) + jnp.sum(d_rv * v, axis=1, keepdims=True)
            d_v = d_rv * beta
            d_eg = jnp.sum(d_qd * q, axis=1, keepdims=True) + jnp.sum(d_rw * kb, axis=1, keepdims=True)
            d_egl = jnp.sum(d_kd * k, axis=1, keepdims=True)
            d_glast = jnp.sum(d_egl * egl, axis=0, keepdims=True) + d_el * el
            row_sum = jnp.sum(e_mat, axis=1, keepdims=True)
            col_sum = c["to_col"](jnp.sum(e_mat, axis=0, keepdims=True))
            d_gc = row_sum - col_sum + d_eg * eg - d_egl * egl
            d_g_row = jnp.sum(jnp.where(ii >= jj, jnp.broadcast_to(d_gc, (C, C)), 0.0), axis=0, keepdims=True) + d_glast
            beta_row = c["beta_row"]
            d_b_row = c["to_row"](d_beta) * beta_row * (1.0 - beta_row)
            d_a_row = d_g_row * c["neg_ea"] * _sigmoid(c["a_row"] + c["dt_b"])
            dqkv_ref[0, pl.ds(c["r0"], C), :] = d_q * QK_SCALE
            dqkv_ref[1, pl.ds(c["r0"], C), :] = d_k
            dqkv_ref[2, pl.ds(c["r0"], C), :] = d_v
            db_ref[0, n] = d_b_row
            da_ref[0, n] = d_a_row
            acc_scr[0:1, 0:C] += d_g_row * c["g_row"]
            acc_scr[1:2, 0:C] += d_a_row

        lax.fori_loop(0, nc // unroll, step, 0)
        tot = jnp.sum(acc_scr[...], axis=1, keepdims=True)
        dhp_ref[0] = jnp.broadcast_to(tot, (8, 128))

    return pl.pallas_call(
        body, name=name, grid=(HEADS,),
        in_specs=[qkv_spec, row_spec, row_spec, hp_spec, st_spec, pl.BlockSpec((T, HEAD_DIM), lambda h: (0, h))],
        out_specs=(qkv_spec, row_spec, row_spec, hp_spec),
        out_shape=(jax.ShapeDtypeStruct((3, T, D_MODEL), F32), jax.ShapeDtypeStruct((HEADS, nc, 1, C), F32),
                   jax.ShapeDtypeStruct((HEADS, nc, 1, C), F32), jax.ShapeDtypeStruct((HEADS, 8, 128), F32)),
        scratch_shapes=[pltpu.VMEM((HEAD_DIM, HEAD_DIM), F32), pltpu.VMEM((8, 128), F32)],
        compiler_params=_params(("parallel",)),
    )(qkv, b_rows, a_rows, hp, states, do)


COL_Z = 3 * HEADS


def _gated_norm_fwd(o, proj, gain, *, name):
    T = o.shape[0]

    def body(o_ref, z_ref, g_ref, out_ref):
        x = o_ref[...]
        r = lax.rsqrt(jnp.mean(x * x, axis=-1, keepdims=True) + RMS_EPS)
        z = z_ref[...]
        out_ref[...] = (x * r * g_ref[...] * (z * _sigmoid(z))).astype(out_ref.dtype)

    return pl.pallas_call(
        body, name=name, grid=(HEADS,),
        in_specs=[pl.BlockSpec((T, HEAD_DIM), lambda h: (0, h)), pl.BlockSpec((T, HEAD_DIM), lambda h: (0, COL_Z + h)),
                  pl.BlockSpec((1, HEAD_DIM), lambda h: (0, 0))],
        out_specs=pl.BlockSpec((T, HEAD_DIM), lambda h: (0, h)),
        out_shape=jax.ShapeDtypeStruct((T, D_MODEL), BF16), compiler_params=_params(("parallel",)),
    )(o, proj, gain)


def _gated_norm_bwd(dout, o, proj, gain, *, name):
    T = o.shape[0]

    def body(d_ref, o_ref, z_ref, g_ref, do_ref, dz_ref, dg_ref):
        x = o_ref[...]
        r = lax.rsqrt(jnp.mean(x * x, axis=-1, keepdims=True) + RMS_EPS)
        n = x * r
        z = z_ref[...]
        sg = _sigmoid(z)
        d = d_ref[...].astype(F32)
        g = g_ref[...]
        dz_ref[...] = (d * n * g * (sg * (1.0 + z * (1.0 - sg)))).astype(dz_ref.dtype)
        dy = d * (z * sg)
        dyg = dy * g
        do_ref[...] = r * (dyg - n * jnp.mean(dyg * n, axis=-1, keepdims=True))

        @pl.when(pl.program_id(0) == 0)
        def _():
            dg_ref[...] = jnp.zeros_like(dg_ref)

        dg_ref[...] += jnp.sum(dy * n, axis=0, keepdims=True)

    head = pl.BlockSpec((T, HEAD_DIM), lambda h: (0, h))
    vec = pl.BlockSpec((1, HEAD_DIM), lambda h: (0, 0))
    return pl.pallas_call(
        body, name=name, grid=(HEADS,),
        in_specs=[head, head, pl.BlockSpec((T, HEAD_DIM), lambda h: (0, COL_Z + h)), vec],
        out_specs=(head, head, vec),
        out_shape=(jax.ShapeDtypeStruct((T, D_MODEL), F32), jax.ShapeDtypeStruct((T, D_MODEL), BF16),
                   jax.ShapeDtypeStruct((1, HEAD_DIM), F32)),
        compiler_params=_params(("arbitrary",)),
    )(dout, o, proj, gain)


COL_SBQ = 4 * HEADS
COL_SBK = 5 * HEADS
COL_SBV = 6 * HEADS


def _split_dot(x, mat):
    hi = x.astype(BF16)
    lo = (x - hi.astype(F32)).astype(BF16)
    return jnp.dot(hi, mat, preferred_element_type=F32) + jnp.dot(lo, mat, preferred_element_type=F32)


def _sb_specs(T):
    return (pl.BlockSpec((T, HEAD_DIM), lambda h: (0, COL_SBQ + h)), pl.BlockSpec((T, HEAD_DIM), lambda h: (0, COL_SBK + h)),
            pl.BlockSpec((T, HEAD_DIM), lambda h: (0, COL_SBV + h)), pl.BlockSpec((1, HEAD_DIM), lambda h: (0, 0)))


def _head_rms(x, gain):
    r = lax.rsqrt(jnp.mean(x * x, axis=-1, keepdims=True) + RMS_EPS)
    return x * r, r


def _sb_fwd(proj, q_gain, k_gain, *, name):
    T = proj.shape[0]
    B = SB_BLOCK
    nb = T // B
    KT = min(SB_KEY_TILE, T)
    NS = KT // B
    q_spec, k_spec, v_spec, g_spec = _sb_specs(T)

    def body(q_ref, k_ref, v_ref, gq_ref, gk_ref, o_ref, lt_ref, qs, ks, vs):
        qs[...] = (_head_rms(q_ref[...], None)[0] * gq_ref[...]).astype(BF16)
        ks[...] = (_head_rms(k_ref[...], None)[0] * gk_ref[...]).astype(BF16)
        vs[...] = v_ref[...].astype(BF16)
        ii = lax.broadcasted_iota(jnp.int32, (B, B), 0)
        jj = lax.broadcasted_iota(jnp.int32, (B, B), 1)
        after = jnp.where(ii > jj, 1.0, 0.0).astype(BF16)
        row_t = lax.broadcasted_iota(jnp.int32, (B, KT), 0)
        col_t = lax.broadcasted_iota(jnp.int32, (B, KT), 1)

        def q_block(i, carry):
            q = qs[pl.ds(pl.multiple_of(i * B, B), B), :]

            def k_tile(step, inner):
                acc, tail = inner
                c0 = pl.multiple_of((i // NS - step) * KT, KT)
                z = lax.dot_general(q, ks[pl.ds(c0, KT), :], _NT, preferred_element_type=F32) * QK_SCALE
                causal = (c0 + col_t) < (i * B + row_t)
                sp = _softplus(z)
                log_1mb = jnp.where(causal, -sp, 0.0)
                parts = [None] * NS
                for b in reversed(range(NS)):
                    blk = log_1mb[:, b * B:(b + 1) * B]
                    parts[b] = _split_dot(blk, after) + tail
                    tail = tail + jnp.sum(blk, axis=1, keepdims=True)
                survive = parts[0] if NS == 1 else jnp.concatenate(parts, axis=1)
                wts = jnp.where(causal, jnp.exp(z - sp + survive), 0.0)
                acc = acc + jnp.dot(wts.astype(BF16), vs[pl.ds(c0, KT), :], preferred_element_type=F32)
                return acc, tail

            acc, tail = lax.fori_loop(0, i // NS + 1, k_tile, (jnp.zeros((B, HEAD_DIM), F32), jnp.zeros((B, 1), F32)))
            rows = pl.ds(pl.multiple_of(i * B, B), B)
            o_ref[rows, :] = acc.astype(o_ref.dtype)
            lt_ref[rows, :] = jnp.broadcast_to(tail, (B, HEAD_DIM))
            return carry

        lax.fori_loop(0, nb, q_block, 0)

    head = pl.BlockSpec((T, HEAD_DIM), lambda h: (0, h))
    return pl.pallas_call(
        body, name=name, grid=(HEADS,), in_specs=[q_spec, k_spec, v_spec, g_spec, g_spec],
        out_specs=(head, head),
        out_shape=(jax.ShapeDtypeStruct((T, D_MODEL), BF16), jax.ShapeDtypeStruct((T, D_MODEL), F32)),
        scratch_shapes=[pltpu.VMEM((T, HEAD_DIM), BF16)] * 3, compiler_params=_params(("parallel",)),
    )(proj, proj, proj, q_gain, k_gain)


def _sb_bwd(proj, q_gain, k_gain, ltot, do, *, name):
    T = proj.shape[0]
    B = SB_BLOCK
    nb = T // B
    KT = min(SB_KEY_TILE, T)
    NS = KT // B
    q_spec, k_spec, v_spec, g_spec = _sb_specs(T)

    def body(q_ref, k_ref, v_ref, gq_ref, gk_ref, lt_ref, do_ref, dq_ref, dk_ref, dv_ref, dgq_ref, dgk_ref,
             qs, ks, vs, dos, dq_acc, dk_acc, dv_acc):
        qn, q_r = _head_rms(q_ref[...], None)
        kn, k_r = _head_rms(k_ref[...], None)
        qs[...] = (qn * gq_ref[...]).astype(BF16)
        ks[...] = (kn * gk_ref[...]).astype(BF16)
        vs[...] = v_ref[...].astype(BF16)
        dos[...] = do_ref[...].astype(BF16)
        dk_acc[...] = jnp.zeros_like(dk_acc)
        dv_acc[...] = jnp.zeros_like(dv_acc)
        ii = lax.broadcasted_iota(jnp.int32, (B, B), 0)
        jj = lax.broadcasted_iota(jnp.int32, (B, B), 1)
        upto = jnp.where(ii <= jj, 1.0, 0.0).astype(BF16)
        before = jnp.where(ii < jj, 1.0, 0.0).astype(BF16)
        row_t = lax.broadcasted_iota(jnp.int32, (B, KT), 0)
        col_t = lax.broadcasted_iota(jnp.int32, (B, KT), 1)

        def q_block(i, carry):
            rows = pl.ds(pl.multiple_of(i * B, B), B)
            q = qs[rows, :]
            d_o = dos[rows, :]
            total = jnp.max(lt_ref[rows, :], axis=1, keepdims=True)

            def k_tile(t, inner):
                dq, head_lb, head_de = inner
                cols = pl.ds(pl.multiple_of(t * KT, KT), KT)
                k = ks[cols, :]
                v = vs[cols, :]
                z = lax.dot_general(q, k, _NT, preferred_element_type=F32) * QK_SCALE
                causal = (t * KT + col_t) < (i * B + row_t)
                sp = _softplus(z)
                log_1mb = jnp.where(causal, -sp, 0.0)
                parts = [None] * NS
                for b in range(NS):
                    blk = log_1mb[:, b * B:(b + 1) * B]
                    parts[b] = _split_dot(blk, upto) + head_lb
                    head_lb = head_lb + jnp.sum(blk, axis=1, keepdims=True)
                prefix = parts[0] if NS == 1 else jnp.concatenate(parts, axis=1)
                wts = jnp.where(causal, jnp.exp(z - sp + (total - prefix)), 0.0)
                d_w = lax.dot_general(d_o, v, _NT, preferred_element_type=F32)
                d_e = wts * d_w
                for b in range(NS):
                    blk = d_e[:, b * B:(b + 1) * B]
                    parts[b] = _split_dot(blk, before) + head_de
                    head_de = head_de + jnp.sum(blk, axis=1, keepdims=True)
                cum = parts[0] if NS == 1 else jnp.concatenate(parts, axis=1)
                sig = jnp.exp(z - sp)
                d_z = jnp.where(causal, d_e * (1.0 - sig) - sig * cum, 0.0) * QK_SCALE
                d_zb = d_z.astype(BF16)
                dq = dq + jnp.dot(d_zb, k, preferred_element_type=F32)
                dk_acc[cols, :] += lax.dot_general(d_zb, q, _TN, preferred_element_type=F32)
                dv_acc[cols, :] += lax.dot_general(wts.astype(BF16), d_o, _TN, preferred_element_type=F32)
                return dq, head_lb, head_de

            zero = jnp.zeros((B, 1), F32)
            dq, _, _ = lax.fori_loop(0, i // NS + 1, k_tile, (jnp.zeros((B, HEAD_DIM), F32), zero, zero))
            dq_acc[rows, :] = dq
            return carry

        lax.fori_loop(0, nb, q_block, 0)

        def norm_bwd(d_scaled, n, r, gain):
            dn = d_scaled * gain
            return r * (dn - n * jnp.mean(dn * n, axis=-1, keepdims=True)), jnp.sum(d_scaled * n, axis=0, keepdims=True)

        dq_raw, dgq = norm_bwd(dq_acc[...], qn, q_r, gq_ref[...])
        dk_raw, dgk = norm_bwd(dk_acc[...], kn, k_r, gk_ref[...])
        dq_ref[...] = dq_raw.astype(dq_ref.dtype)
        dk_ref[...] = dk_raw.astype(dk_ref.dtype)
        dv_ref[...] = dv_acc[...].astype(dv_ref.dtype)

        @pl.when(pl.program_id(0) == 0)
        def _():
            dgq_ref[...] = jnp.zeros_like(dgq_ref)
            dgk_ref[...] = jnp.zeros_like(dgk_ref)

        dgq_ref[...] += dgq
        dgk_ref[...] += dgk

    head = pl.BlockSpec((T, HEAD_DIM), lambda h: (0, h))
    out = jax.ShapeDtypeStruct((T, D_MODEL), BF16)
    vec = jax.ShapeDtypeStruct((1, HEAD_DIM), F32)
    return pl.pallas_call(
        body, name=name, grid=(HEADS,), in_specs=[q_spec, k_spec, v_spec, g_spec, g_spec, head, head],
        out_specs=(head, head, head, g_spec, g_spec), out_shape=(out, out, out, vec, vec),
        scratch_shapes=[pltpu.VMEM((T, HEAD_DIM), BF16)] * 4 + [pltpu.VMEM((T, HEAD_DIM), F32)] * 3,
        compiler_params=_params(("arbitrary",)),
    )(proj, proj, proj, q_gain, k_gain, ltot, do)


ADAM_ROWS = 128


def _adamw(g_layers, w, m, v, *, name):
    n_layers = len(g_layers)
    K, A, C = g_layers[0].shape
    R = n_layers * A
    tr = ADAM_ROWS if A % ADAM_ROWS == 0 else (A // 2 if A % 32 == 0 else A)
    per_layer = A // tr

    def body(*refs):
        g_refs = refs[:n_layers]
        w_ref, m_ref, v_ref, go_ref, d_ref, mo_ref, vo_ref = refs[n_layers:]
        layer = pl.program_id(0) // per_layer
        for l in range(n_layers):
            @pl.when(layer == l)
            def _(g_ref=g_refs[l]):
                g = g_ref[0].astype(F32)
                for k in range(1, K):
                    g = g + g_ref[k].astype(F32)
                go_ref[...] = g

        g = go_ref[...]
        m_new = ADAM_B1 * m_ref[...] + (1.0 - ADAM_B1) * g
        v_new = ADAM_B2 * v_ref[...] + (1.0 - ADAM_B2) * (g * g)
        m_hat = m_new / (1.0 - ADAM_B1 ** ADAM_STEP)
        v_hat = v_new / (1.0 - ADAM_B2 ** ADAM_STEP)
        d_ref[...] = -ADAM_LR * (m_hat / (jnp.sqrt(v_hat) + ADAM_EPS) + ADAM_WD * w_ref[...])
        mo_ref[...] = m_new
        vo_ref[...] = v_new

    def layer_spec(l):
        return pl.BlockSpec((K, tr, C), lambda i: (0, jnp.clip(i - l * per_layer, 0, per_layer - 1), 0))

    row = pl.BlockSpec((tr, C), lambda i: (i, 0))
    out = jax.ShapeDtypeStruct((R, C), F32)
    return pl.pallas_call(
        body, name=name, grid=(R // tr,), in_specs=[layer_spec(l) for l in range(n_layers)] + [row, row, row],
        out_specs=(row, row, row, row), out_shape=(out, out, out, out), compiler_params=_params(("arbitrary",)),
    )(*g_layers, w, m, v)


def _sum_parts(parts, *, name):
    K, R, C = parts.shape

    def body(p_ref, o_ref):
        acc = p_ref[0]
        for k in range(1, K):
            acc = acc + p_ref[k]
        o_ref[...] = acc

    return pl.pallas_call(body, name=name, out_shape=jax.ShapeDtypeStruct((R, C), F32))(parts)


def _position():
    return lax.axis_index("x"), lax.axis_index("y"), lax.axis_index("c")


def _all_gather(shards, *, name):
    n = len(shards)

    def body(*refs):
        x_refs, out_refs = refs[:n], refs[n:2 * n]
        send_sems, recv_sems, local_sems = refs[2 * n:]
        x, y, c = _position()
        me, sibling = (x, y, c), (x, y, 1 - c)
        chips = [(1 - x, y), (x, 1 - y), (1 - x, 1 - y)]

        def slot(a, px, py, pc):
            return out_refs[a].at[4 * px + 2 * py + pc]

        def copy(a, k, block, to, own=False):
            return pltpu.make_async_remote_copy(
                src_ref=x_refs[a] if own else slot(a, *block), dst_ref=slot(a, *block),
                send_sem=send_sems.at[a, k], recv_sem=recv_sems.at[a, k], device_id=to, device_id_type=MESH)

        mine = [pltpu.make_async_copy(x_refs[a], slot(a, *me), local_sems.at[a]) for a in range(n)]
        for cp in mine:
            cp.start()
        first = [copy(a, 1 + j, me, (*chip, c), own=True) for j, chip in enumerate(chips) for a in range(n)]
        first += [copy(a, 0, me, sibling, own=True) for a in range(n)]
        for cp in first:
            cp.start()
        passed = []
        for j, chip in enumerate(chips):
            for a in range(n):
                copy(a, 1 + j, (*chip, c), me).wait_recv()
                passed.append(copy(a, 4 + j, (*chip, c), sibling))
                passed[-1].start()
        for a in range(n):
            copy(a, 0, sibling, me).wait_recv()
        for j, chip in enumerate(chips):
            for a in range(n):
                copy(a, 4 + j, (*chip, 1 - c), me).wait_recv()
        for cp in first + passed:
            cp.wait_send()
        for cp in mine:
            cp.wait()

    return pl.pallas_call(
        body, name=name, in_specs=[ANY] * n, out_specs=[ANY] * n,
        out_shape=[jax.ShapeDtypeStruct((N_DEV,) + s.shape, s.dtype) for s in shards],
        scratch_shapes=[pltpu.SemaphoreType.DMA((n, 7)), pltpu.SemaphoreType.DMA((n, 7)), pltpu.SemaphoreType.DMA((n,))],
    )(*shards)


HBM = pl.BlockSpec(memory_space=pltpu.HBM)
SEM = pl.BlockSpec(memory_space=pltpu.SEMAPHORE)
DATAFLOW = pltpu.SideEffectType.DATAFLOW_SIDE_EFFECTING


def _scatter_copies(x_refs, land_refs, send_sems, recv_sems, local_sems):
    n = len(x_refs)
    x, y, c = _position()
    me = 4 * x + 2 * y + c
    mine = [pltpu.make_async_copy(x_refs[a].at[me], land_refs[a].at[me], local_sems.at[a]) for a in range(n)]
    sends, recvs = [], []
    for k in range(1, N_DEV):
        px, py, pc = (x + (k >> 2)) % 2, (y + ((k >> 1) & 1)) % 2, (c + (k & 1)) % 2
        peer = 4 * px + 2 * py + pc
        for a in range(n):
            sems = dict(send_sem=send_sems.at[7 * a + k - 1], recv_sem=recv_sems.at[7 * a + k - 1],
                        device_id=(px, py, pc), device_id_type=MESH)
            sends.append(pltpu.make_async_remote_copy(src_ref=x_refs[a].at[peer], dst_ref=land_refs[a].at[me], **sems))
            recvs.append(pltpu.make_async_remote_copy(src_ref=x_refs[a].at[me], dst_ref=land_refs[a].at[peer], **sems))
    return mine, sends, recvs


def _scatter_start(parts, *, name):
    n = len(parts)

    def body(*refs):
        x_refs, land_refs = refs[:n], refs[n:2 * n]
        send_sems, recv_sems, local_sems = refs[2 * n:2 * n + 3]
        token = refs[-1]
        mine, sends, _ = _scatter_copies(x_refs, land_refs, send_sems, recv_sems, local_sems)
        for cp in mine + sends:
            cp.start()
        token[...] = jnp.zeros_like(token)

    sems = (pltpu.SemaphoreType.DMA((7 * n,)), pltpu.SemaphoreType.DMA((7 * n,)), pltpu.SemaphoreType.DMA((n,)))
    thru = tuple(pltpu.HBM(p.shape, p.dtype) for p in parts)
    res = pl.pallas_call(
        body, name=name, in_specs=[HBM] * (2 * n),
        out_specs=(SEM, SEM, SEM) + (HBM,) * (2 * n) + (pl.BlockSpec(memory_space=pltpu.VMEM),),
        out_shape=sems + thru + thru + (jax.ShapeDtypeStruct((8, 128), F32),),
        input_output_aliases={a: 3 + a for a in range(2 * n)},
        compiler_params=pltpu.CompilerParams(has_side_effects=DATAFLOW),
    )(*[pltpu.with_memory_space_constraint(p, pltpu.HBM) for p in parts],
      *[pltpu.with_memory_space_constraint(lax.empty(p.shape, p.dtype), pltpu.HBM) for p in parts])
    return res[:3], res[3:3 + n], res[3 + n:3 + 2 * n], res[-1]


def _scatter_wait(sems, parts, landing, after, *, name):
    n = len(parts)

    def body(*refs):
        x_refs, land_refs = refs[:n], refs[n:2 * n]
        send_sems, recv_sems, local_sems = refs[2 * n:2 * n + 3]
        mine, sends, recvs = _scatter_copies(x_refs, land_refs, send_sems, recv_sems, local_sems)
        for cp in recvs:
            cp.wait_recv()
        for cp in sends:
            cp.wait_send()
        for cp in mine:
            cp.wait()

    thru = tuple(pltpu.HBM(p.shape, p.dtype) for p in parts)
    res = pl.pallas_call(
        body, name=name, in_specs=[HBM] * (2 * n) + [SEM, SEM, SEM, ANY], out_specs=(HBM,) * (2 * n),
        out_shape=thru + thru, input_output_aliases={a: a for a in range(2 * n)},
        compiler_params=pltpu.CompilerParams(has_side_effects=DATAFLOW),
    )(*parts, *landing, *sems, after)
    return res[n:]


def _ffn_fwd(x, gain, wg_in, wg_out, l, tag):
    T, D = x.shape
    fb, rb = wg_in.shape[-1], wg_out.shape[-2]
    tm, tn = min(T, 1024), 512
    h = _rmsnorm_fwd(x, gain, name=f"{tag}_norm")
    p = _mm(name=f"{tag}_in", grid=(T // tm, N_DEV, 1), tile=(tm, fb),
            a=h, a_spec=pl.BlockSpec((tm, D), lambda i, j, k: (i, 0)),
            b=wg_in, b_spec=pl.BlockSpec((None, None, D, fb), lambda i, j, k: (j, l, 0, 0)),
            out_shape=jax.ShapeDtypeStruct((N_DEV, T, fb), BF16), o_spec=pl.BlockSpec((None, tm, fb), lambda i, j, k: (j, i, 0)))
    a = _swiglu_fwd(p, name=f"{tag}_act")
    y = _mm(name=f"{tag}_out", grid=(T // tm, D // tn, FF_HALF), tile=(tm, tn), resid=x, scale=0.5,
            a=a, a_spec=pl.BlockSpec((None, tm, fb), lambda i, j, k: (k, i, 0)),
            b=wg_out, b_spec=pl.BlockSpec((2, None, rb, tn), lambda i, j, k: (k, l, 0, j)),
            out_shape=jax.ShapeDtypeStruct((T, D), F32), o_spec=pl.BlockSpec((tm, tn), lambda i, j, k: (i, j)))
    return y, (x, h, p, a)


def _ffn_bwd(dy, saved, gain, wg_in, wg_out, l, tag, on_weight_grads=None):
    x, h, p, a = saved
    T, D = x.shape
    fb, rb = wg_in.shape[-1], wg_out.shape[-2]
    tm, tn = min(T, 1024), 512
    da = _mm(name=f"{tag}_out_dx", grid=(T // tm, FF_HALF, 1), tile=(tm, fb), tb=True, scale=0.5,
             a=dy, a_spec=pl.BlockSpec((tm, D), lambda i, j, k: (i, 0)),
             b=wg_out, b_spec=pl.BlockSpec((2, None, rb, D), lambda i, j, k: (j, l, 0, 0)),
             out_shape=jax.ShapeDtypeStruct((FF_HALF, T, fb), BF16), o_spec=pl.BlockSpec((None, tm, fb), lambda i, j, k: (j, i, 0)))
    d_w_out = _mm(name=f"{tag}_out_dw", grid=(FF_HALF, D // tn, 1), tile=(fb, tn), ta=True, scale=0.5,
                  a=a, a_spec=pl.BlockSpec((None, T, fb), lambda i, j, k: (i, 0, 0)),
                  b=dy, b_spec=pl.BlockSpec((T, tn), lambda i, j, k: (0, j)),
                  out_shape=jax.ShapeDtypeStruct((FF_HALF, fb, D), BF16), o_spec=pl.BlockSpec((None, fb, tn), lambda i, j, k: (i, 0, j)))
    dp = _swiglu_bwd(da, p, name=f"{tag}_act_bwd")
    d_w_in = _mm(name=f"{tag}_in_dw", grid=(1, N_DEV, 1), tile=(D, fb), ta=True,
                 a=h, a_spec=pl.BlockSpec((T, D), lambda i, j, k: (0, 0)),
                 b=dp, b_spec=pl.BlockSpec((None, T, fb), lambda i, j, k: (j, 0, 0)),
                 out_shape=jax.ShapeDtypeStruct((N_DEV, D, fb), BF16), o_spec=pl.BlockSpec((None, D, fb), lambda i, j, k: (j, 0, 0)))
    dh = _mm(name=f"{tag}_in_dx", grid=(T // tm, D // tn, N_DEV), tile=(tm, tn), tb=True,
             a=dp, a_spec=pl.BlockSpec((None, tm, fb), lambda i, j, k: (k, i, 0)),
             b=wg_in, b_spec=pl.BlockSpec((None, None, tn, fb), lambda i, j, k: (k, l, j, 0)),
             out_shape=jax.ShapeDtypeStruct((T, D), F32), o_spec=pl.BlockSpec((tm, tn), lambda i, j, k: (i, j)))
    d_w_out = d_w_out.reshape(N_DEV, rb, D)
    if on_weight_grads is not None:
        gain = gain + on_weight_grads(d_w_in, d_w_out)[0, 0]
    dx, d_gain = _rmsnorm_bwd(dh, x, gain, dy, name=f"{tag}_norm_bwd")
    return dx, d_gain, d_w_in, d_w_out


def _square_mm(a, wg, l, *, name, transposed=False, out_dtype=F32, resid=None):
    T, D = a.shape
    rb = wg.shape[-2]
    tm, tn = min(T, 1024), 512
    if transposed:
        b_spec = pl.BlockSpec((tn // rb, None, rb, D), lambda i, j, k: (j, l, 0, 0))
    else:
        b_spec = pl.BlockSpec((N_DEV, None, rb, tn), lambda i, j, k: (0, l, 0, j))
    return _mm(name=name, grid=(T // tm, D // tn, 1), tile=(tm, tn), tb=transposed, resid=resid,
               a=a, a_spec=pl.BlockSpec((tm, D), lambda i, j, k: (i, 0)), b=wg, b_spec=b_spec,
               out_shape=jax.ShapeDtypeStruct((T, D), out_dtype), o_spec=pl.BlockSpec((tm, tn), lambda i, j, k: (i, j)))


def _layer_mm(a, w, l, *, name, transposed=False, out_dtype=F32, resid=None, tn=None, tk=None):
    T = a.shape[0]
    _, K, N = w.shape
    tm = min(T, 1024)
    if transposed:
        tn, tk = tn or min(K, 512), tk or N
        b_spec = pl.BlockSpec((None, tn, tk), lambda i, j, k: (l, j, k))
        n_out, n_k = K, N // tk
    else:
        tn, tk = tn or min(N, 512), tk or K
        b_spec = pl.BlockSpec((None, tk, tn), lambda i, j, k: (l, k, j))
        n_out, n_k = N, K // tk
    return _mm(name=name, grid=(T // tm, n_out // tn, n_k), tile=(tm, tn), tb=transposed, resid=resid,
               a=a, a_spec=pl.BlockSpec((tm, tk), lambda i, j, k: (i, k)), b=w, b_spec=b_spec,
               out_shape=jax.ShapeDtypeStruct((T, n_out), out_dtype), o_spec=pl.BlockSpec((tm, tn), lambda i, j, k: (i, j)))


def _head_rows(cols, T):
    return cols.T.reshape(HEADS, T // DN_CHUNK, 1, DN_CHUNK)


def _mixer_fwd(x, w, big, l, tag):
    T = x.shape[0]
    h = _rmsnorm_fwd(x, w["mix_norm"], name=f"{tag}_norm")
    proj = _layer_mm(h, big["w_main"], l, name=f"{tag}_proj")
    scal = _layer_mm(h, big["w_scal"], l, name=f"{tag}_proj_scal", tn=N_SCAL)
    qkv = _conv_fwd(proj, big["conv_w"][l], name=f"{tag}_conv")
    b_rows = _head_rows(scal[:, 0:HEADS], T)
    a_rows = _head_rows(scal[:, HEADS:2 * HEADS], T)
    o_a, states = _dn_fwd(qkv, b_rows, a_rows, w["hp"], name=f"{tag}_dn")
    oa_n = _gated_norm_fwd(o_a, proj, w["dn_out_norm"], name=f"{tag}_dn_norm")
    ya = _square_mm(oa_n, big["w_branch_a"], l, name=f"{tag}_branch_a")
    o_b, ltot = _sb_fwd(proj, w["sb_q_norm"], w["sb_k_norm"], name=f"{tag}_sb")
    yb = _square_mm(o_b, big["w_branch_b"], l, name=f"{tag}_branch_b")
    merged = _merge_fwd(ya, yb, proj, name=f"{tag}_merge")
    y = _square_mm(merged, big["w_out"], l, name=f"{tag}_out", resid=x)
    return y, (x, h, proj, qkv, b_rows, a_rows, o_a, states, oa_n, ya, o_b, ltot, yb, merged)


def _mixer_bwd(dy, saved, w, big, l, tag):
    x, h, proj, qkv, b_rows, a_rows, o_a, states, oa_n, ya, o_b, ltot, yb, merged = saved
    T = x.shape[0]
    g = {}
    d_merged = _square_mm(dy, big["w_out"], l, transposed=True, name=f"{tag}_out_dx", out_dtype=BF16)
    g["w_out"] = _matmul(merged, dy, ta=True, name=f"{tag}_out_dw", out_dtype=BF16)
    d_ya, d_yb, d_ga, d_gb = _merge_bwd(d_merged, ya, yb, proj, name=f"{tag}_merge_bwd")
    d_oan = _square_mm(d_ya, big["w_branch_a"], l, transposed=True, name=f"{tag}_branch_a_dx")
    g["w_branch_a"] = _matmul(oa_n, d_ya, ta=True, name=f"{tag}_branch_a_dw", out_dtype=BF16)
    d_ob = _square_mm(d_yb, big["w_branch_b"], l, transposed=True, name=f"{tag}_branch_b_dx")
    g["w_branch_b"] = _matmul(o_b, d_yb, ta=True, name=f"{tag}_branch_b_dw", out_dtype=BF16)
    d_oa, d_z, g["dn_out_norm"] = _gated_norm_bwd(d_oan, o_a, proj, w["dn_out_norm"], name=f"{tag}_dn_norm_bwd")
    d_qkv, d_b_rows, d_a_rows, d_hp = _dn_bwd(qkv, b_rows, a_rows, w["hp"], states, d_oa, name=f"{tag}_dn_bwd")
    g["dn_a_log"] = d_hp[:, 0, 0]
    g["dn_dt_bias"] = d_hp[:, 1, 0]
    d_conv_in, g["conv_w"] = _conv_bwd(d_qkv, proj, big["conv_w"][l], name=f"{tag}_conv_bwd")
    d_sbq, d_sbk, d_sbv, g["sb_q_norm"], g["sb_k_norm"] = _sb_bwd(
        proj, w["sb_q_norm"], w["sb_k_norm"], ltot, d_ob, name=f"{tag}_sb_bwd")
    d_proj = jnp.concatenate([d_conv_in, d_z, d_sbq, d_sbk, d_sbv, d_ga, d_gb], axis=1)
    d_scal = jnp.concatenate([d_b_rows.reshape(HEADS, T).T, d_a_rows.reshape(HEADS, T).T,
                              jnp.zeros((T, N_SCAL - 2 * HEADS), F32)], axis=1).astype(BF16)
    g["w_main"] = _matmul(h, d_proj, ta=True, name=f"{tag}_proj_dw", out_dtype=BF16)
    g["w_scal"] = _matmul(h, d_scal, ta=True, name=f"{tag}_proj_scal_dw", out_dtype=BF16, tn=N_SCAL)
    dh_scal = _layer_mm(d_scal, big["w_scal"], l, transposed=True, name=f"{tag}_proj_scal_dx")
    dh = _layer_mm(d_proj, big["w_main"], l, transposed=True, name=f"{tag}_proj_dx", tk=N_MAIN // 4, resid=dh_scal)
    dx, g["mix_norm"] = _rmsnorm_bwd(dh, x, w["mix_norm"], dy, name=f"{tag}_norm_bwd")
    return dx, g


def _local_step(x, target, layers, big, on_layer_grads):
    saved = []
    for l, w in enumerate(layers):
        x, s1 = _ffn_fwd(x, w["ffn1_norm"], big["ffn1_w_in"], big["ffn1_w_out"], l, f"l{l}_ffn1")
        x, s2 = _mixer_fwd(x, w, big, l, f"l{l}_mix")
        x, s3 = _ffn_fwd(x, w["ffn2_norm"], big["ffn2_w_in"], big["ffn2_w_out"], l, f"l{l}_ffn2")
        saved.append((s1, s2, s3))
    loss, dx = _loss_head(x, target, name="loss_head")
    for l in reversed(range(len(layers))):
        w = layers[l]
        s1, s2, s3 = saved[l]
        dx, g_n2, g_in2, g_out2 = _ffn_bwd(dx, s3, w["ffn2_norm"], big["ffn2_w_in"], big["ffn2_w_out"], l, f"l{l}_ffn2")
        dx, g = _mixer_bwd(dx, s2, w, big, l, f"l{l}_mix")
        g.update(ffn2_norm=g_n2, ffn2_w_in=g_in2, ffn2_w_out=g_out2)

        def last_grads(g_in1, g_out1, l=l, g=g):
            g.update(ffn1_w_in=g_in1, ffn1_w_out=g_out1)
            return on_layer_grads(l, g)

        dx, g["ffn1_norm"], _, _ = _ffn_bwd(dx, s1, w["ffn1_norm"], big["ffn1_w_in"], big["ffn1_w_out"], l, f"l{l}_ffn1",
                                            on_weight_grads=last_grads)
    return loss, dx


_BIG = ("ffn1_w_in", "ffn1_w_out", "w_in", "w_branch_a", "w_branch_b", "w_out", "ffn2_w_in", "ffn2_w_out")
_SMALL = ("ffn1_norm", "mix_norm", "ffn2_norm", "dn_a_log", "dn_dt_bias", "dn_out_norm", "sb_q_norm", "sb_k_norm")
_ORDER = ("ffn1_norm", "ffn1_w_in", "ffn1_w_out", "mix_norm", "w_in", "dn_conv_w", "dn_a_log", "dn_dt_bias", "dn_out_norm",
          "sb_q_norm", "sb_k_norm", "w_branch_a", "w_branch_b", "w_out", "ffn2_norm", "ffn2_w_in", "ffn2_w_out")
COL_SCAL = 4 * D_MODEL


def _pad_rows(a, multiple):
    pad = (-a.shape[-2]) % multiple
    return a if pad == 0 else jnp.pad(a, [(0, 0)] * (a.ndim - 2) + [(0, pad), (0, 0)])


def _lane_rows(a):
    flat = a.reshape(-1)
    flat = jnp.pad(flat, (0, (-flat.shape[0]) % 128))
    return flat.reshape(-1, 128)


def _pack_small(named):
    pieces, spans, r = [], {}, 0
    for n, a in named:
        rows = _lane_rows(a)
        spans[n] = (r, r + rows.shape[0], a.shape)
        r += rows.shape[0]
        pieces.append(rows)
    return _pad_rows(jnp.concatenate(pieces, axis=0), 8), spans


def _unpack_small(packed, spans, n):
    r0, r1, shape = spans[n]
    return packed[r0:r1].reshape(-1)[:math.prod(shape)].reshape(shape)


def kernel(x, ffn1_norm, ffn1_w_in, ffn1_w_out, mix_norm, w_in, dn_conv_w, dn_a_log, dn_dt_bias, dn_out_norm, sb_q_norm, sb_k_norm, w_branch_a, w_branch_b, w_out, ffn2_norm, ffn2_w_in, ffn2_w_out, loss_target, m_ffn1_norm, m_ffn1_w_in, m_ffn1_w_out, m_mix_norm, m_w_in, m_dn_conv_w, m_dn_a_log, m_dn_dt_bias, m_dn_out_norm, m_sb_q_norm, m_sb_k_norm, m_w_branch_a, m_w_branch_b, m_w_out, m_ffn2_norm, m_ffn2_w_in, m_ffn2_w_out, v_ffn1_norm, v_ffn1_w_in, v_ffn1_w_out, v_mix_norm, v_w_in, v_dn_conv_w, v_dn_a_log, v_dn_dt_bias, v_dn_out_norm, v_sb_q_norm, v_sb_k_norm, v_w_branch_a, v_w_branch_b, v_w_out, v_ffn2_norm, v_ffn2_w_in, v_ffn2_w_out):
    given = dict(locals())
    weights = {n: given[n] for n in _ORDER}
    mom_m = {n: given["m_" + n] for n in _ORDER}
    mom_v = {n: given["v_" + n] for n in _ORDER}
    L = ffn1_norm.shape[0]
    ax, ay, ac = _position()
    my_slot = 4 * ax + 2 * ay + ac

    conv_cols = dn_conv_w.shape[-1]
    gathered = _all_gather([weights[n].astype(BF16) for n in _BIG] + [_pad_rows(_lane_rows(dn_conv_w), 8)],
                           name="gather_weights")
    big = dict(zip(_BIG, gathered[:-1]))
    wi = big.pop("w_in").transpose(1, 2, 0, 3).reshape(L, D_MODEL, N_IN)
    big["w_main"] = jnp.concatenate([wi[..., :COL_SCAL], wi[..., COL_SCAL + 2 * HEADS:]], axis=-1)
    big["w_scal"] = jnp.pad(wi[..., COL_SCAL:COL_SCAL + 2 * HEADS], ((0, 0), (0, 0), (0, N_SCAL - 2 * HEADS)))
    conv_full = gathered[-1].reshape(N_DEV, -1)[:, :L * DN_CONV * conv_cols].reshape(N_DEV, L, DN_CONV, conv_cols)
    big["conv_w"] = conv_full.transpose(1, 2, 0, 3).reshape(L, DN_CONV, N_DEV * conv_cols)

    layers = []
    for l in range(L):
        hp = jnp.concatenate([jnp.broadcast_to(dn_a_log[l][:, None, None], (HEADS, 1, 128)),
                              jnp.broadcast_to(dn_dt_bias[l][:, None, None], (HEADS, 1, 128)),
                              jnp.zeros((HEADS, 6, 128), F32)], axis=1)
        layers.append(dict(ffn1_norm=ffn1_norm[l][None], mix_norm=mix_norm[l][None], hp=hp,
                           dn_out_norm=dn_out_norm[l][None], sb_q_norm=sb_q_norm[l][None],
                           sb_k_norm=sb_k_norm[l][None], ffn2_norm=ffn2_norm[l][None]))

    grads = [None] * L
    in_flight = [None] * L

    def on_layer_grads(l, g):
        grads[l] = g
        g_w_in = jnp.concatenate([g["w_main"][:, :COL_SCAL], g["w_scal"][:, :2 * HEADS], g["w_main"][:, COL_SCAL:]], axis=1)
        parts = dict(g, w_in=g_w_in.reshape(D_MODEL, N_DEV, N_IN // N_DEV).transpose(1, 0, 2))
        for n in ("w_branch_a", "w_branch_b", "w_out"):
            parts[n] = g[n].reshape(N_DEV, D_MODEL // N_DEV, D_MODEL)
        *in_flight[l], token = _scatter_start([parts[n] for n in _BIG], name=f"scatter_start_l{l}")
        return token

    loss_row, dx = _local_step(x[0], loss_target[0], layers, big, on_layer_grads)
    loss = lax.psum(loss_row[0, 0], ("x", "y", "c"))

    landed = [_scatter_wait(*in_flight[l], dx, name=f"scatter_wait_l{l}") for l in reversed(range(L))][::-1]
    out = {}
    for i, n in enumerate(_BIG):
        _, a, b = weights[n].shape
        res = _adamw([landed[l][i] for l in range(L)], weights[n].reshape(L * a, b), mom_m[n].reshape(L * a, b),
                     mom_v[n].reshape(L * a, b), name=f"adamw_{n}")
        out[n] = tuple(t.reshape(L, a, b) for t in res)

    small_grads = [(n, jnp.stack([g[n].reshape(weights[n].shape[1:]) for g in grads])) for n in _SMALL]
    small_packed, spans = _pack_small(small_grads + [("conv", jnp.stack([g["conv_w"] for g in grads]))])
    small_sum = _sum_parts(_all_gather([small_packed], name="gather_small_grads")[0], name="sum_small_grads")
    rep_rows = spans["conv"][0]
    pack_rep = lambda d: _pad_rows(_pack_small([(n, d[n]) for n in _SMALL])[0], 8)
    rep_pad = (-rep_rows) % 8
    g_rep = jnp.pad(small_sum[:rep_rows], ((0, rep_pad), (0, 0)))
    res = _adamw([g_rep[None]], pack_rep(weights), pack_rep(mom_m), pack_rep(mom_v), name="adamw_replicated")
    for n in _SMALL:
        out[n] = tuple(_unpack_small(t, spans, n) for t in res)
    conv_sum = _unpack_small(small_sum, spans, "conv")
    conv_mine = lax.dynamic_slice_in_dim(conv_sum, my_slot * conv_cols, conv_cols, axis=2).reshape(L * DN_CONV, conv_cols)
    flat = lambda t: t.reshape(L * DN_CONV, conv_cols)
    res = _adamw([conv_mine[None]], flat(dn_conv_w), flat(m_dn_conv_w), flat(v_dn_conv_w), name="adamw_conv")
    out["dn_conv_w"] = tuple(t.reshape(L, DN_CONV, conv_cols) for t in res)

    return (loss, dx[None], *[out[n][0] for n in _ORDER], *[out[n][1] for n in _ORDER],
            *[out[n][2] for n in _ORDER], *[out[n][3] for n in _ORDER])
```

```python
import functools
import math

import jax
import jax.numpy as jnp
from jax import lax
from jax.experimental import pallas as pl
from jax.experimental.pallas import tpu as pltpu

F32 = jnp.float32
BF16 = jnp.bfloat16

N_DEV = 8
D_MODEL = 1024
DEPTH = 4
D_FF = 2816
HEADS = 8
HEAD_DIM = 128
DN_CHUNK = 64
DN_CONV = 4
DN_UNROLL = 4
SB_BLOCK = 128
SB_KEY_TILE = 512
RMS_EPS = 1e-6
L2_EPS = 1e-6
N_IN = 9232
N_MAIN = 9216
N_SCAL = 128
QK_SCALE = HEAD_DIM ** -0.5

ADAM_LR = 0.001
ADAM_B1 = 0.9
ADAM_B2 = 0.999
ADAM_EPS = 1e-08
ADAM_WD = 0.01
ADAM_STEP = 10

V7X_VMEM_LIMIT = 56 * 1024 * 1024
MESH = pl.DeviceIdType.MESH
ANY = pl.BlockSpec(memory_space=pl.ANY)


def _params(sem=None, vmem=V7X_VMEM_LIMIT):
    return pltpu.CompilerParams(dimension_semantics=sem, vmem_limit_bytes=vmem)


def _sigmoid(x):
    return 1.0 / (1.0 + jnp.exp(-x))


def _softplus(x):
    return jnp.maximum(x, 0.0) + jnp.log(1.0 + jnp.exp(-jnp.abs(x)))


def _bdot(a, b, dims=(((1,), (0,)), ((), ()))):
    return lax.dot_general(a.astype(BF16), b.astype(BF16), dims, preferred_element_type=F32)


_NT = (((1,), (1,)), ((), ()))
_TN = (((0,), (0,)), ((), ()))


def _hdot(a, b, dims=(((1,), (0,)), ((), ()))):
    a_hi = a.astype(BF16)
    b_hi = b.astype(BF16)
    a_lo = (a - a_hi.astype(F32)).astype(BF16)
    b_lo = (b - b_hi.astype(F32)).astype(BF16)
    dot = functools.partial(lax.dot_general, dimension_numbers=dims, preferred_element_type=F32)
    return dot(a_hi, b_hi) + (dot(a_hi, b_lo) + dot(a_lo, b_hi))


def _hdot_tn(a, b):
    return _hdot(a, b, _TN)


def _mm(*, name, grid, a, a_spec, b, b_spec, out_shape, o_spec, tile, ta=False, tb=False, resid=None, scale=1.0):
    nk = grid[2]
    dims = (((0 if ta else 1,), (1 if tb else 0,)), ((), ()))

    def flat(v):
        return v if v.ndim == 2 else v.reshape(-1, v.shape[-1])

    def body(*refs):
        a_ref, b_ref = refs[:2]
        r_ref = refs[2] if resid is not None else None
        o_ref = refs[3] if resid is not None else refs[2]
        part = lax.dot_general(flat(a_ref[...]).astype(BF16), flat(b_ref[...]).astype(BF16), dims,
                               preferred_element_type=F32)

        def finish(acc):
            if scale != 1.0:
                acc = acc * scale
            if r_ref is not None:
                acc = r_ref[...] + acc
            o_ref[...] = acc.astype(o_ref.dtype)

        if nk == 1:
            finish(part)
        else:
            acc_ref = refs[-1]
            k = pl.program_id(2)

            @pl.when(k == 0)
            def _():
                acc_ref[...] = part

            @pl.when(k > 0)
            def _():
                acc_ref[...] += part

            @pl.when(k == nk - 1)
            def _():
                finish(acc_ref[...])

    in_specs = [a_spec, b_spec] + ([pl.BlockSpec(tile, lambda i, j, k: (i, j))] if resid is not None else [])
    args = (a, b) + ((resid,) if resid is not None else ())
    return pl.pallas_call(
        body, name=name, grid=grid, in_specs=in_specs, out_specs=o_spec, out_shape=out_shape,
        scratch_shapes=[pltpu.VMEM(tile, F32)] if nk > 1 else [],
        compiler_params=_params(("parallel", "parallel", "arbitrary")),
    )(*args)


def _matmul(a, b, *, name, ta=False, tb=False, out_dtype=F32, tm=None, tn=None, tk=None, resid=None, scale=1.0):
    if ta:
        K, M = a.shape
    else:
        M, K = a.shape
    N = b.shape[0] if tb else b.shape[1]
    tm = tm or min(M, 1024)
    tn = tn or min(N, 512)
    tk = tk or K
    assert M % tm == 0 and N % tn == 0 and K % tk == 0, (name, M, N, K, tm, tn, tk)
    a_spec = pl.BlockSpec((tk, tm), lambda i, j, k: (k, i)) if ta else pl.BlockSpec((tm, tk), lambda i, j, k: (i, k))
    b_spec = pl.BlockSpec((tn, tk), lambda i, j, k: (j, k)) if tb else pl.BlockSpec((tk, tn), lambda i, j, k: (k, j))
    return _mm(name=name, grid=(M // tm, N // tn, K // tk), a=a, a_spec=a_spec, b=b, b_spec=b_spec,
               out_shape=jax.ShapeDtypeStruct((M, N), out_dtype), o_spec=pl.BlockSpec((tm, tn), lambda i, j, k: (i, j)),
               tile=(tm, tn), ta=ta, tb=tb, resid=resid, scale=scale)


ROW_TILE = 256


def _rmsnorm_fwd(x, gain, *, name):
    T, D = x.shape

    def body(x_ref, g_ref, o_ref):
        xf = x_ref[...]
        r = lax.rsqrt(jnp.mean(xf * xf, axis=-1, keepdims=True) + RMS_EPS)
        o_ref[...] = (xf * r * g_ref[...]).astype(o_ref.dtype)

    return pl.pallas_call(
        body, name=name, grid=(T // ROW_TILE,),
        in_specs=[pl.BlockSpec((ROW_TILE, D), lambda i: (i, 0)), pl.BlockSpec((1, D), lambda i: (0, 0))],
        out_specs=pl.BlockSpec((ROW_TILE, D), lambda i: (i, 0)),
        out_shape=jax.ShapeDtypeStruct((T, D), BF16), compiler_params=_params(("parallel",)),
    )(x, gain)


def _rmsnorm_bwd(dh, x, gain, dres, *, name):
    T, D = x.shape

    def body(dh_ref, x_ref, g_ref, res_ref, dx_ref, dg_ref):
        xf = x_ref[...]
        r = lax.rsqrt(jnp.mean(xf * xf, axis=-1, keepdims=True) + RMS_EPS)
        y = xf * r
        dh_v = dh_ref[...].astype(F32)
        dy = dh_v * g_ref[...]
        dx_ref[...] = res_ref[...] + r * (dy - y * jnp.mean(dy * y, axis=-1, keepdims=True))

        @pl.when(pl.program_id(0) == 0)
        def _():
            dg_ref[...] = jnp.zeros_like(dg_ref)

        dg_ref[...] += jnp.sum(dh_v * y, axis=0, keepdims=True)

    row = pl.BlockSpec((ROW_TILE, D), lambda i: (i, 0))
    vec = pl.BlockSpec((1, D), lambda i: (0, 0))
    return pl.pallas_call(
        body, name=name, grid=(T // ROW_TILE,), in_specs=[row, row, vec, row], out_specs=(row, vec),
        out_shape=(jax.ShapeDtypeStruct((T, D), F32), jax.ShapeDtypeStruct((1, D), F32)),
        compiler_params=_params(("arbitrary",)),
    )(dh, x, gain, dres)


FF_HALF = N_DEV // 2


def _swiglu_fwd(p, *, name):
    _, T, fb = p.shape

    def body(g_ref, u_ref, o_ref):
        g = g_ref[...].astype(F32)
        o_ref[...] = (g * _sigmoid(g) * u_ref[...].astype(F32)).astype(o_ref.dtype)

    blk = (None, ROW_TILE, fb)
    return pl.pallas_call(
        body, name=name, grid=(T // ROW_TILE, FF_HALF),
        in_specs=[pl.BlockSpec(blk, lambda i, j: (j, i, 0)), pl.BlockSpec(blk, lambda i, j: (j + FF_HALF, i, 0))],
        out_specs=pl.BlockSpec(blk, lambda i, j: (j, i, 0)),
        out_shape=jax.ShapeDtypeStruct((FF_HALF, T, fb), BF16), compiler_params=_params(("parallel", "parallel")),
    )(p, p)


def _swiglu_bwd(da, p, *, name):
    _, T, fb = p.shape

    def body(da_ref, g_ref, u_ref, o_ref):
        g = g_ref[...].astype(F32)
        u = u_ref[...].astype(F32)
        d = da_ref[...].astype(F32)
        s = _sigmoid(g)
        o_ref[0] = (d * u * (s * (1.0 + g * (1.0 - s)))).astype(o_ref.dtype)
        o_ref[1] = (d * g * s).astype(o_ref.dtype)

    blk = (None, ROW_TILE, fb)
    out = pl.pallas_call(
        body, name=name, grid=(T // ROW_TILE, FF_HALF),
        in_specs=[pl.BlockSpec(blk, lambda i, j: (j, i, 0)), pl.BlockSpec(blk, lambda i, j: (j, i, 0)),
                  pl.BlockSpec(blk, lambda i, j: (j + FF_HALF, i, 0))],
        out_specs=pl.BlockSpec((2, None, ROW_TILE, fb), lambda i, j: (0, j, i, 0)),
        out_shape=jax.ShapeDtypeStruct((2, FF_HALF, T, fb), BF16), compiler_params=_params(("parallel", "parallel")),
    )(da, p, p)
    return out.reshape(2 * FF_HALF, T, fb)


COL_GATE_A = 7
COL_GATE_B = 8


def _merge_fwd(ya, yb, proj, *, name):
    T, D = ya.shape

    def body(ya_ref, yb_ref, ga_ref, gb_ref, o_ref):
        o_ref[...] = (_sigmoid(ga_ref[...]) * ya_ref[...] + _sigmoid(gb_ref[...]) * yb_ref[...]).astype(o_ref.dtype)

    row = pl.BlockSpec((ROW_TILE, D), lambda i: (i, 0))
    return pl.pallas_call(
        body, name=name, grid=(T // ROW_TILE,),
        in_specs=[row, row, pl.BlockSpec((ROW_TILE, D), lambda i: (i, COL_GATE_A)),
                  pl.BlockSpec((ROW_TILE, D), lambda i: (i, COL_GATE_B))],
        out_specs=row, out_shape=jax.ShapeDtypeStruct((T, D), BF16), compiler_params=_params(("parallel",)),
    )(ya, yb, proj, proj)


def _merge_bwd(dm, ya, yb, proj, *, name):
    T, D = ya.shape

    def body(dm_ref, ya_ref, yb_ref, ga_ref, gb_ref, dya_ref, dyb_ref, dga_ref, dgb_ref):
        d = dm_ref[...].astype(F32)
        sa = _sigmoid(ga_ref[...])
        sb = _sigmoid(gb_ref[...])
        dya_ref[...] = (d * sa).astype(BF16)
        dyb_ref[...] = (d * sb).astype(BF16)
        dga_ref[...] = (d * ya_ref[...] * sa * (1.0 - sa)).astype(BF16)
        dgb_ref[...] = (d * yb_ref[...] * sb * (1.0 - sb)).astype(BF16)

    row = pl.BlockSpec((ROW_TILE, D), lambda i: (i, 0))
    out = jax.ShapeDtypeStruct((T, D), BF16)
    return pl.pallas_call(
        body, name=name, grid=(T // ROW_TILE,),
        in_specs=[row, row, row, pl.BlockSpec((ROW_TILE, D), lambda i: (i, COL_GATE_A)),
                  pl.BlockSpec((ROW_TILE, D), lambda i: (i, COL_GATE_B))],
        out_specs=(row, row, row, row), out_shape=(out, out, out, out), compiler_params=_params(("parallel",)),
    )(dm, ya, yb, proj, proj)


def _loss_head(y, target, *, name):
    T, D = y.shape

    def body(y_ref, t_ref, loss_ref, dy_ref):
        err = y_ref[...] - t_ref[...]
        dy_ref[...] = err * (1.0 / D)

        @pl.when(pl.program_id(0) == 0)
        def _():
            loss_ref[...] = jnp.zeros_like(loss_ref)

        loss_ref[...] += 0.5 * jnp.sum(jnp.sum(err * err, axis=-1, keepdims=True) * (1.0 / D), axis=0, keepdims=True)

    row = pl.BlockSpec((ROW_TILE, D), lambda i: (i, 0))
    return pl.pallas_call(
        body, name=name, grid=(T // ROW_TILE,), in_specs=[row, row],
        out_specs=(pl.BlockSpec((1, 128), lambda i: (0, 0)), row),
        out_shape=(jax.ShapeDtypeStruct((1, 128), F32), jax.ShapeDtypeStruct((T, D), F32)),
        compiler_params=_params(("arbitrary",)),
    )(y, target)


CONV_PAD = 8


def _conv_taps(w, xp, T, first):
    acc = w[0:1, :] * xp[pl.ds(first, T), :]
    for i in range(1, DN_CONV):
        acc = acc + w[i:i + 1, :] * xp[pl.ds(first + i, T), :]
    return acc


def _conv_fwd(proj, conv_w, *, name):
    T = proj.shape[0]

    def body(x_ref, w_ref, o_ref, xp):
        xp[0:CONV_PAD, :] = jnp.zeros((CONV_PAD, HEAD_DIM), F32)
        xp[CONV_PAD:, :] = x_ref[...]
        y = _conv_taps(w_ref[...], xp, T, CONV_PAD - (DN_CONV - 1))
        s = y * _sigmoid(y)
        n = s * lax.rsqrt(jnp.sum(s * s, axis=-1, keepdims=True) + L2_EPS)
        o_ref[0] = jnp.where(pl.program_id(0) < 2, n, s)

    return pl.pallas_call(
        body, name=name, grid=(3, HEADS),
        in_specs=[pl.BlockSpec((T, HEAD_DIM), lambda c, h: (0, c * HEADS + h)),
                  pl.BlockSpec((DN_CONV, HEAD_DIM), lambda c, h: (0, c * HEADS + h))],
        out_specs=pl.BlockSpec((1, T, HEAD_DIM), lambda c, h: (c, 0, h)),
        out_shape=jax.ShapeDtypeStruct((3, T, D_MODEL), F32),
        scratch_shapes=[pltpu.VMEM((T + CONV_PAD, HEAD_DIM), F32)],
        compiler_params=_params(("parallel", "parallel")),
    )(proj, conv_w)


def _conv_bwd(dqkv, proj, conv_w, *, name):
    T = proj.shape[0]

    def body(d_ref, x_ref, w_ref, dx_ref, dw_ref, xp, dyp):
        xp[0:CONV_PAD, :] = jnp.zeros((CONV_PAD, HEAD_DIM), F32)
        xp[CONV_PAD:, :] = x_ref[...]
        w = w_ref[...]
        y = _conv_taps(w, xp, T, CONV_PAD - (DN_CONV - 1))
        sg = _sigmoid(y)
        s = y * sg
        r = lax.rsqrt(jnp.sum(s * s, axis=-1, keepdims=True) + L2_EPS)
        n = s * r
        d = d_ref[0]
        ds = jnp.where(pl.program_id(0) < 2, r * (d - n * jnp.sum(d * n, axis=-1, keepdims=True)), d)
        dy = ds * (sg * (1.0 + y * (1.0 - sg)))
        dyp[0:T, :] = dy
        dyp[T:, :] = jnp.zeros((CONV_PAD, HEAD_DIM), F32)
        dx = w[0:1, :] * dyp[pl.ds(DN_CONV - 1, T), :]
        for i in range(1, DN_CONV):
            dx = dx + w[i:i + 1, :] * dyp[pl.ds(DN_CONV - 1 - i, T), :]
        dx_ref[...] = dx.astype(dx_ref.dtype)
        for i in range(DN_CONV):
            dw_ref[i:i + 1, :] = jnp.sum(dy * xp[pl.ds(CONV_PAD - (DN_CONV - 1) + i, T), :], axis=0, keepdims=True)

    col = lambda c, h: (0, c * HEADS + h)
    return pl.pallas_call(
        body, name=name, grid=(3, HEADS),
        in_specs=[pl.BlockSpec((1, T, HEAD_DIM), lambda c, h: (c, 0, h)), pl.BlockSpec((T, HEAD_DIM), col),
                  pl.BlockSpec((DN_CONV, HEAD_DIM), col)],
        out_specs=(pl.BlockSpec((T, HEAD_DIM), col), pl.BlockSpec((DN_CONV, HEAD_DIM), col)),
        out_shape=(jax.ShapeDtypeStruct((T, 3 * D_MODEL), BF16), jax.ShapeDtypeStruct((DN_CONV, 3 * D_MODEL), F32)),
        scratch_shapes=[pltpu.VMEM((T + CONV_PAD, HEAD_DIM), F32), pltpu.VMEM((T + CONV_PAD, HEAD_DIM), F32)],
        compiler_params=_params(("parallel", "parallel")),
    )(dqkv, proj, conv_w)


def _inv_unit_lower(low, eye):
    x = eye - low
    power = _hdot(low, low)
    steps = int(math.log2(DN_CHUNK)) - 1
    for s in range(steps):
        x = x + _hdot(x, power)
        if s + 1 < steps:
            power = _hdot(power, power)
    return x


def _dn_chunk_setup(q_ref, k_ref, v_ref, b_ref, a_ref, hp_ref, n):
    C = DN_CHUNK
    r0 = pl.multiple_of(n * C, C)
    q = q_ref[0, pl.ds(r0, C), :] * QK_SCALE
    k = k_ref[1, pl.ds(r0, C), :]
    v = v_ref[2, pl.ds(r0, C), :]
    ii = lax.broadcasted_iota(jnp.int32, (C, C), 0)
    jj = lax.broadcasted_iota(jnp.int32, (C, C), 1)
    eye_mask = ii == jj
    eye = jnp.where(eye_mask, 1.0, 0.0).astype(F32)

    def to_col(row):
        return jnp.sum(jnp.where(eye_mask, jnp.broadcast_to(row, (C, C)), 0.0), axis=1, keepdims=True)

    def to_row(col):
        return jnp.sum(jnp.where(eye_mask, jnp.broadcast_to(col, (C, C)), 0.0), axis=0, keepdims=True)

    b_row = b_ref[0, n]
    a_row = a_ref[0, n]
    a_log = hp_ref[0, 0:1, 0:C]
    dt_b = hp_ref[0, 1:2, 0:C]
    beta_row = _sigmoid(b_row)
    neg_ea = -jnp.exp(a_log)
    g_row = neg_ea * _softplus(a_row + dt_b)
    gc_col = jnp.sum(jnp.where(jj <= ii, jnp.broadcast_to(g_row, (C, C)), 0.0), axis=1, keepdims=True)
    gc_row = to_row(gc_col)
    g_last = jnp.sum(g_row, axis=1, keepdims=True)
    beta = to_col(beta_row)
    low_incl = ii >= jj
    decay = jnp.exp(jnp.where(low_incl, gc_col - gc_row, -jnp.inf))
    eg = jnp.exp(gc_col)
    egl = jnp.exp(g_last - gc_col)
    el = jnp.exp(g_last)
    kb = k * beta
    pmat = _bdot(kb, k, _NT)
    low = jnp.where(ii > jj, pmat * decay, 0.0)
    tinv = _inv_unit_lower(low, eye)
    u = _hdot(tinv, v * beta)
    w = _hdot(tinv, kb * eg)
    qk = _bdot(q, k, _NT)
    attn = qk * decay
    return dict(q=q, k=k, v=v, ii=ii, jj=jj, to_col=to_col, to_row=to_row, b_row=b_row, a_row=a_row, dt_b=dt_b,
                beta_row=beta_row, neg_ea=neg_ea, g_row=g_row, gc_col=gc_col, g_last=g_last, beta=beta,
                decay=decay, eg=eg, egl=egl, el=el, kb=kb, pmat=pmat, tinv=tinv, u=u, w=w, qk=qk, attn=attn,
                qd=q * eg, kd=k * egl, r0=r0)


def _dn_specs(T):
    nc = T // DN_CHUNK
    qkv = pl.BlockSpec((3, T, HEAD_DIM), lambda h: (0, 0, h))
    rows = pl.BlockSpec((1, nc, 1, DN_CHUNK), lambda h: (h, 0, 0, 0))
    hp = pl.BlockSpec((1, 8, 128), lambda h: (h, 0, 0))
    states = pl.BlockSpec((1, nc, HEAD_DIM, HEAD_DIM), lambda h: (h, 0, 0, 0))
    return nc, qkv, rows, hp, states


def _dn_fwd(qkv, b_rows, a_rows, hp, *, name):
    T = qkv.shape[1]
    nc, qkv_spec, row_spec, hp_spec, st_spec = _dn_specs(T)
    unroll = math.gcd(nc, DN_UNROLL)

    def body(qkv_ref, b_ref, a_ref, hp_ref, o_ref, st_ref, s_scr):
        s_scr[...] = jnp.zeros_like(s_scr)

        def step(t, carry):
            chunks = [_dn_chunk_setup(qkv_ref, qkv_ref, qkv_ref, b_ref, a_ref, hp_ref, t * unroll + s)
                      for s in range(unroll)]
            state = s_scr[...]
            for s, c in enumerate(chunks):
                st_ref[0, t * unroll + s] = state
                v_new = c["u"] - _bdot(c["w"], state)
                o_ref[pl.ds(c["r0"], DN_CHUNK), :] = _bdot(c["qd"], state) + _bdot(c["attn"], v_new)
                state = state * c["el"] + _bdot(c["kd"], v_new, _TN)
            s_scr[...] = state
            return carry

        lax.fori_loop(0, nc // unroll, step, 0)

    return pl.pallas_call(
        body, name=name, grid=(HEADS,), in_specs=[qkv_spec, row_spec, row_spec, hp_spec],
        out_specs=(pl.BlockSpec((T, HEAD_DIM), lambda h: (0, h)), st_spec),
        out_shape=(jax.ShapeDtypeStruct((T, D_MODEL), F32),
                   jax.ShapeDtypeStruct((HEADS, nc, HEAD_DIM, HEAD_DIM), F32)),
        scratch_shapes=[pltpu.VMEM((HEAD_DIM, HEAD_DIM), F32)], compiler_params=_params(("parallel",)),
    )(qkv, b_rows, a_rows, hp)


def _dn_bwd(qkv, b_rows, a_rows, hp, states, do, *, name):
    T = qkv.shape[1]
    C = DN_CHUNK
    nc, qkv_spec, row_spec, hp_spec, st_spec = _dn_specs(T)
    unroll = math.gcd(nc, DN_UNROLL)

    def body(qkv_ref, b_ref, a_ref, hp_ref, st_ref, do_ref, dqkv_ref, db_ref, da_ref, dhp_ref, ds_scr, acc_scr):
        ds_scr[...] = jnp.zeros_like(ds_scr)
        acc_scr[...] = jnp.zeros_like(acc_scr)

        def step(t, carry):
            order = [nc - 1 - (t * unroll + s) for s in range(unroll)]
            chunks = []
            for n in order:
                c = _dn_chunk_setup(qkv_ref, qkv_ref, qkv_ref, b_ref, a_ref, hp_ref, n)
                c["n"] = n
                c["state"] = st_ref[0, n]
                c["d_o"] = do_ref[pl.ds(c["r0"], C), :]
                c["v_new"] = c["u"] - _bdot(c["w"], c["state"])
                c["d_vnew_local"] = _bdot(c["attn"], c["d_o"], _TN)
                c["d_state_local"] = _bdot(c["qd"], c["d_o"], _TN)
                chunks.append(c)
            d_state = ds_scr[...]
            for c in chunks:
                c["d_vnew"] = c["d_vnew_local"] + _bdot(c["kd"], d_state)
                c["d_kd"] = _bdot(c["v_new"], d_state, _NT)
                c["d_el"] = jnp.sum(jnp.sum(d_state * c["state"], axis=1, keepdims=True), axis=0, keepdims=True)
                d_state = d_state * c["el"] + c["d_state_local"] - _bdot(c["w"], c["d_vnew"], _TN)
            ds_scr[...] = d_state
            for c in chunks:
                chunk_grads(c)
            return carry

        def chunk_grads(c):
            n = c["n"]
            ii, jj = c["ii"], c["jj"]
            q, k, v, kb, beta = c["q"], c["k"], c["v"], c["kb"], c["beta"]
            decay, eg, egl, el = c["decay"], c["eg"], c["egl"], c["el"]
            u, w, tinv = c["u"], c["w"], c["tinv"]
            state, d_o, v_new, d_vnew, d_kd, d_el = c["state"], c["d_o"], c["v_new"], c["d_vnew"], c["d_kd"], c["d_el"]
            d_qd = _bdot(d_o, state, _NT)
            d_attn = _bdot(d_o, v_new, _NT)
            d_w = -_bdot(d_vnew, state, _NT)
            d_rv = _hdot_tn(tinv, d_vnew)
            d_rw = _hdot_tn(tinv, d_w)
            d_amat = -(_bdot(d_rv, u, _NT) + _bdot(d_rw, w, _NT))
            d_low = jnp.where(ii > jj, d_amat, 0.0)
            d_p = d_low * decay
            d_qk = d_attn * decay
            e_mat = (d_low * c["pmat"] + d_attn * c["qk"]) * decay
            d_q = _bdot(d_qk, k) + d_qd * eg
            d_kb = _bdot(d_p, k) + d_rw * eg
            d_k = _bdot(d_qk, q, _TN) + _bdot(d_p, kb, _TN) + d_kd * egl + d_kb * beta
            d_beta = jnp.sum(d_kb * k, axis=1, keepdims=True) + jnp.sum(d_rv * v, axis=1, keepdims=True)
            d_v = d_rv * beta
            d_eg = jnp.sum(d_qd * q, axis=1, keepdims=True) + jnp.sum(d_rw * kb, axis=1, keepdims=True)
            d_egl = jnp.sum(d_kd * k, axis=1, keepdims=True)
            d_glast = jnp.sum(d_egl * egl, axis=0, keepdims=True) + d_el * el
            row_sum = jnp.sum(e_mat, axis=1, keepdims=True)
            col_sum = c["to_col"](jnp.sum(e_mat, axis=0, keepdims=True))
            d_gc = row_sum - col_sum + d_eg * eg - d_egl * egl
            d_g_row = jnp.sum(jnp.where(ii >= jj, jnp.broadcast_to(d_gc, (C, C)), 0.0), axis=0, keepdims=True) + d_glast
            beta_row = c["beta_row"]
            d_b_row = c["to_row"](d_beta) * beta_row * (1.0 - beta_row)
            d_a_row = d_g_row * c["neg_ea"] * _sigmoid(c["a_row"] + c["dt_b"])
            dqkv_ref[0, pl.ds(c["r0"], C), :] = d_q * QK_SCALE
            dqkv_ref[1, pl.ds(c["r0"], C), :] = d_k
            dqkv_ref[2, pl.ds(c["r0"], C), :] = d_v
            db_ref[0, n] = d_b_row
            da_ref[0, n] = d_a_row
            acc_scr[0:1, 0:C] += d_g_row * c["g_row"]
            acc_scr[1:2, 0:C] += d_a_row

        lax.fori_loop(0, nc // unroll, step, 0)
        tot = jnp.sum(acc_scr[...], axis=1, keepdims=True)
        dhp_ref[0] = jnp.broadcast_to(tot, (8, 128))

    return pl.pallas_call(
        body, name=name, grid=(HEADS,),
        in_specs=[qkv_spec, row_spec, row_spec, hp_spec, st_spec, pl.BlockSpec((T, HEAD_DIM), lambda h: (0, h))],
        out_specs=(qkv_spec, row_spec, row_spec, hp_spec),
        out_shape=(jax.ShapeDtypeStruct((3, T, D_MODEL), F32), jax.ShapeDtypeStruct((HEADS, nc, 1, C), F32),
                   jax.ShapeDtypeStruct((HEADS, nc, 1, C), F32), jax.ShapeDtypeStruct((HEADS, 8, 128), F32)),
        scratch_shapes=[pltpu.VMEM((HEAD_DIM, HEAD_DIM), F32), pltpu.VMEM((8, 128), F32)],
        compiler_params=_params(("parallel",)),
    )(qkv, b_rows, a_rows, hp, states, do)


COL_Z = 3 * HEADS


def _gated_norm_fwd(o, proj, gain, *, name):
    T = o.shape[0]

    def body(o_ref, z_ref, g_ref, out_ref):
        x = o_ref[...]
        r = lax.rsqrt(jnp.mean(x * x, axis=-1, keepdims=True) + RMS_EPS)
        z = z_ref[...]
        out_ref[...] = (x * r * g_ref[...] * (z * _sigmoid(z))).astype(out_ref.dtype)

    return pl.pallas_call(
        body, name=name, grid=(HEADS,),
        in_specs=[pl.BlockSpec((T, HEAD_DIM), lambda h: (0, h)), pl.BlockSpec((T, HEAD_DIM), lambda h: (0, COL_Z + h)),
                  pl.BlockSpec((1, HEAD_DIM), lambda h: (0, 0))],
        out_specs=pl.BlockSpec((T, HEAD_DIM), lambda h: (0, h)),
        out_shape=jax.ShapeDtypeStruct((T, D_MODEL), BF16), compiler_params=_params(("parallel",)),
    )(o, proj, gain)


def _gated_norm_bwd(dout, o, proj, gain, *, name):
    T = o.shape[0]

    def body(d_ref, o_ref, z_ref, g_ref, do_ref, dz_ref, dg_ref):
        x = o_ref[...]
        r = lax.rsqrt(jnp.mean(x * x, axis=-1, keepdims=True) + RMS_EPS)
        n = x * r
        z = z_ref[...]
        sg = _sigmoid(z)
        d = d_ref[...].astype(F32)
        g = g_ref[...]
        dz_ref[...] = (d * n * g * (sg * (1.0 + z * (1.0 - sg)))).astype(dz_ref.dtype)
        dy = d * (z * sg)
        dyg = dy * g
        do_ref[...] = r * (dyg - n * jnp.mean(dyg * n, axis=-1, keepdims=True))

        @pl.when(pl.program_id(0) == 0)
        def _():
            dg_ref[...] = jnp.zeros_like(dg_ref)

        dg_ref[...] += jnp.sum(dy * n, axis=0, keepdims=True)

    head = pl.BlockSpec((T, HEAD_DIM), lambda h: (0, h))
    vec = pl.BlockSpec((1, HEAD_DIM), lambda h: (0, 0))
    return pl.pallas_call(
        body, name=name, grid=(HEADS,),
        in_specs=[head, head, pl.BlockSpec((T, HEAD_DIM), lambda h: (0, COL_Z + h)), vec],
        out_specs=(head, head, vec),
        out_shape=(jax.ShapeDtypeStruct((T, D_MODEL), F32), jax.ShapeDtypeStruct((T, D_MODEL), BF16),
                   jax.ShapeDtypeStruct((1, HEAD_DIM), F32)),
        compiler_params=_params(("arbitrary",)),
    )(dout, o, proj, gain)


COL_SBQ = 4 * HEADS
COL_SBK = 5 * HEADS
COL_SBV = 6 * HEADS


def _split_dot(x, mat):
    hi = x.astype(BF16)
    lo = (x - hi.astype(F32)).astype(BF16)
    return jnp.dot(hi, mat, preferred_element_type=F32) + jnp.dot(lo, mat, preferred_element_type=F32)


def _sb_specs(T):
    return (pl.BlockSpec((T, HEAD_DIM), lambda h: (0, COL_SBQ + h)), pl.BlockSpec((T, HEAD_DIM), lambda h: (0, COL_SBK + h)),
            pl.BlockSpec((T, HEAD_DIM), lambda h: (0, COL_SBV + h)), pl.BlockSpec((1, HEAD_DIM), lambda h: (0, 0)))


def _head_rms(x, gain):
    r = lax.rsqrt(jnp.mean(x * x, axis=-1, keepdims=True) + RMS_EPS)
    return x * r, r


def _sb_fwd(proj, q_gain, k_gain, *, name):
    T = proj.shape[0]
    B = SB_BLOCK
    nb = T // B
    KT = min(SB_KEY_TILE, T)
    NS = KT // B
    q_spec, k_spec, v_spec, g_spec = _sb_specs(T)

    def body(q_ref, k_ref, v_ref, gq_ref, gk_ref, o_ref, lt_ref, qs, ks, vs):
        qs[...] = (_head_rms(q_ref[...], None)[0] * gq_ref[...]).astype(BF16)
        ks[...] = (_head_rms(k_ref[...], None)[0] * gk_ref[...]).astype(BF16)
        vs[...] = v_ref[...].astype(BF16)
        ii = lax.broadcasted_iota(jnp.int32, (B, B), 0)
        jj = lax.broadcasted_iota(jnp.int32, (B, B), 1)
        after = jnp.where(ii > jj, 1.0, 0.0).astype(BF16)
        row_t = lax.broadcasted_iota(jnp.int32, (B, KT), 0)
        col_t = lax.broadcasted_iota(jnp.int32, (B, KT), 1)

        def q_block(i, carry):
            q = qs[pl.ds(pl.multiple_of(i * B, B), B), :]

            def k_tile(step, inner):
                acc, tail = inner
                c0 = pl.multiple_of((i // NS - step) * KT, KT)
                z = lax.dot_general(q, ks[pl.ds(c0, KT), :], _NT, preferred_element_type=F32) * QK_SCALE
                causal = (c0 + col_t) < (i * B + row_t)
                sp = _softplus(z)
                log_1mb = jnp.where(causal, -sp, 0.0)
                parts = [None] * NS
                for b in reversed(range(NS)):
                    blk = log_1mb[:, b * B:(b + 1) * B]
                    parts[b] = _split_dot(blk, after) + tail
                    tail = tail + jnp.sum(blk, axis=1, keepdims=True)
                survive = parts[0] if NS == 1 else jnp.concatenate(parts, axis=1)
                wts = jnp.where(causal, jnp.exp(z - sp + survive), 0.0)
                acc = acc + jnp.dot(wts.astype(BF16), vs[pl.ds(c0, KT), :], preferred_element_type=F32)
                return acc, tail

            acc, tail = lax.fori_loop(0, i // NS + 1, k_tile, (jnp.zeros((B, HEAD_DIM), F32), jnp.zeros((B, 1), F32)))
            rows = pl.ds(pl.multiple_of(i * B, B), B)
            o_ref[rows, :] = acc.astype(o_ref.dtype)
            lt_ref[rows, :] = jnp.broadcast_to(tail, (B, HEAD_DIM))
            return carry

        lax.fori_loop(0, nb, q_block, 0)

    head = pl.BlockSpec((T, HEAD_DIM), lambda h: (0, h))
    return pl.pallas_call(
        body, name=name, grid=(HEADS,), in_specs=[q_spec, k_spec, v_spec, g_spec, g_spec],
        out_specs=(head, head),
        out_shape=(jax.ShapeDtypeStruct((T, D_MODEL), BF16), jax.ShapeDtypeStruct((T, D_MODEL), F32)),
        scratch_shapes=[pltpu.VMEM((T, HEAD_DIM), BF16)] * 3, compiler_params=_params(("parallel",)),
    )(proj, proj, proj, q_gain, k_gain)


def _sb_bwd(proj, q_gain, k_gain, ltot, do, *, name):
    T = proj.shape[0]
    B = SB_BLOCK
    nb = T // B
    KT = min(SB_KEY_TILE, T)
    NS = KT // B
    q_spec, k_spec, v_spec, g_spec = _sb_specs(T)

    def body(q_ref, k_ref, v_ref, gq_ref, gk_ref, lt_ref, do_ref, dq_ref, dk_ref, dv_ref, dgq_ref, dgk_ref,
             qs, ks, vs, dos, dq_acc, dk_acc, dv_acc):
        qn, q_r = _head_rms(q_ref[...], None)
        kn, k_r = _head_rms(k_ref[...], None)
        qs[...] = (qn * gq_ref[...]).astype(BF16)
        ks[...] = (kn * gk_ref[...]).astype(BF16)
        vs[...] = v_ref[...].astype(BF16)
        dos[...] = do_ref[...].astype(BF16)
        dk_acc[...] = jnp.zeros_like(dk_acc)
        dv_acc[...] = jnp.zeros_like(dv_acc)
        ii = lax.broadcasted_iota(jnp.int32, (B, B), 0)
        jj = lax.broadcasted_iota(jnp.int32, (B, B), 1)
        upto = jnp.where(ii <= jj, 1.0, 0.0).astype(BF16)
        before = jnp.where(ii < jj, 1.0, 0.0).astype(BF16)
        row_t = lax.broadcasted_iota(jnp.int32, (B, KT), 0)
        col_t = lax.broadcasted_iota(jnp.int32, (B, KT), 1)

        def q_block(i, carry):
            rows = pl.ds(pl.multiple_of(i * B, B), B)
            q = qs[rows, :]
            d_o = dos[rows, :]
            total = jnp.max(lt_ref[rows, :], axis=1, keepdims=True)

            def k_tile(t, inner):
                dq, head_lb, head_de = inner
                cols = pl.ds(pl.multiple_of(t * KT, KT), KT)
                k = ks[cols, :]
                v = vs[cols, :]
                z = lax.dot_general(q, k, _NT, preferred_element_type=F32) * QK_SCALE
                causal = (t * KT + col_t) < (i * B + row_t)
                sp = _softplus(z)
                log_1mb = jnp.where(causal, -sp, 0.0)
                parts = [None] * NS
                for b in range(NS):
                    blk = log_1mb[:, b * B:(b + 1) * B]
                    parts[b] = _split_dot(blk, upto) + head_lb
                    head_lb = head_lb + jnp.sum(blk, axis=1, keepdims=True)
                prefix = parts[0] if NS == 1 else jnp.concatenate(parts, axis=1)
                wts = jnp.where(causal, jnp.exp(z - sp + (total - prefix)), 0.0)
                d_w = lax.dot_general(d_o, v, _NT, preferred_element_type=F32)
                d_e = wts * d_w
                for b in range(NS):
                    blk = d_e[:, b * B:(b + 1) * B]
                    parts[b] = _split_dot(blk, before) + head_de
                    head_de = head_de + jnp.sum(blk, axis=1, keepdims=True)
                cum = parts[0] if NS == 1 else jnp.concatenate(parts, axis=1)
                sig = jnp.exp(z - sp)
                d_z = jnp.where(causal, d_e * (1.0 - sig) - sig * cum, 0.0) * QK_SCALE
                d_zb = d_z.astype(BF16)
                dq = dq + jnp.dot(d_zb, k, preferred_element_type=F32)
                dk_acc[cols, :] += lax.dot_general(d_zb, q, _TN, preferred_element_type=F32)
                dv_acc[cols, :] += lax.dot_general(wts.astype(BF16), d_o, _TN, preferred_element_type=F32)
                return dq, head_lb, head_de

            zero = jnp.zeros((B, 1), F32)
            dq, _, _ = lax.fori_loop(0, i // NS + 1, k_tile, (jnp.zeros((B, HEAD_DIM), F32), zero, zero))
            dq_acc[rows, :] = dq
            return carry

        lax.fori_loop(0, nb, q_block, 0)

        def norm_bwd(d_scaled, n, r, gain):
            dn = d_scaled * gain
            return r * (dn - n * jnp.mean(dn * n, axis=-1, keepdims=True)), jnp.sum(d_scaled * n, axis=0, keepdims=True)

        dq_raw, dgq = norm_bwd(dq_acc[...], qn, q_r, gq_ref[...])
        dk_raw, dgk = norm_bwd(dk_acc[...], kn, k_r, gk_ref[...])
        dq_ref[...] = dq_raw.astype(dq_ref.dtype)
        dk_ref[...] = dk_raw.astype(dk_ref.dtype)
        dv_ref[...] = dv_acc[...].astype(dv_ref.dtype)

        @pl.when(pl.program_id(0) == 0)
        def _():
            dgq_ref[...] = jnp.zeros_like(dgq_ref)
            dgk_ref[...] = jnp.zeros_like(dgk_ref)

        dgq_ref[...] += dgq
        dgk_ref[...] += dgk

    head = pl.BlockSpec((T, HEAD_DIM), lambda h: (0, h))
    out = jax.ShapeDtypeStruct((T, D_MODEL), BF16)
    vec = jax.ShapeDtypeStruct((1, HEAD_DIM), F32)
    return pl.pallas_call(
        body, name=name, grid=(HEADS,), in_specs=[q_spec, k_spec, v_spec, g_spec, g_spec, head, head],
        out_specs=(head, head, head, g_spec, g_spec), out_shape=(out, out, out, vec, vec),
        scratch_shapes=[pltpu.VMEM((T, HEAD_DIM), BF16)] * 4 + [pltpu.VMEM((T, HEAD_DIM), F32)] * 3,
        compiler_params=_params(("arbitrary",)),
    )(proj, proj, proj, q_gain, k_gain, ltot, do)


ADAM_ROWS = 128


def _adamw(g_parts, w, m, v, *, name, layer=0, earlier=None):
    K, A, C = g_parts.shape
    R = w.shape[0]
    tr = ADAM_ROWS if A % ADAM_ROWS == 0 else (A // 2 if A % 32 == 0 else A)
    first_block = layer * (A // tr)

    def body(g_ref, w_ref, m_ref, v_ref, *rest):
        go_ref, d_ref, mo_ref, vo_ref = rest[-4:]
        g = g_ref[0].astype(F32)
        for k in range(1, K):
            g = g + g_ref[k].astype(F32)
        go_ref[...] = g
        m_new = ADAM_B1 * m_ref[...] + (1.0 - ADAM_B1) * g
        v_new = ADAM_B2 * v_ref[...] + (1.0 - ADAM_B2) * (g * g)
        m_hat = m_new / (1.0 - ADAM_B1 ** ADAM_STEP)
        v_hat = v_new / (1.0 - ADAM_B2 ** ADAM_STEP)
        d_ref[...] = -ADAM_LR * (m_hat / (jnp.sqrt(v_hat) + ADAM_EPS) + ADAM_WD * w_ref[...])
        mo_ref[...] = m_new
        vo_ref[...] = v_new

    row = pl.BlockSpec((tr, C), lambda i: (first_block + i, 0))
    out = jax.ShapeDtypeStruct((R, C), F32)
    in_specs = [pl.BlockSpec((K, tr, C), lambda i: (0, i, 0)), row, row, row]
    if earlier is None:
        return pl.pallas_call(
            body, name=name, grid=(A // tr,), in_specs=in_specs, out_specs=(row, row, row, row),
            out_shape=(out, out, out, out), compiler_params=_params(("parallel",)),
        )(g_parts, w, m, v)
    return pl.pallas_call(
        body, name=name, grid=(A // tr,), in_specs=in_specs + [ANY] * 4, out_specs=(row, row, row, row),
        out_shape=(out, out, out, out), input_output_aliases={4 + j: j for j in range(4)},
        compiler_params=_params(("parallel",)),
    )(g_parts, w, m, v, *earlier)


def _sum_parts(parts, *, name):
    K, R, C = parts.shape

    def body(p_ref, o_ref):
        acc = p_ref[0]
        for k in range(1, K):
            acc = acc + p_ref[k]
        o_ref[...] = acc

    return pl.pallas_call(body, name=name, out_shape=jax.ShapeDtypeStruct((R, C), F32))(parts)


def _position():
    return lax.axis_index("x"), lax.axis_index("y"), lax.axis_index("c")


def _all_gather(shards, *, name):
    n = len(shards)

    def body(*refs):
        x_refs, out_refs = refs[:n], refs[n:2 * n]
        send_sems, recv_sems, local_sems = refs[2 * n:]
        x, y, c = _position()
        me, sibling = (x, y, c), (x, y, 1 - c)
        chips = [(1 - x, y), (x, 1 - y), (1 - x, 1 - y)]

        def slot(a, px, py, pc):
            return out_refs[a].at[4 * px + 2 * py + pc]

        def copy(a, k, block, to, own=False):
            return pltpu.make_async_remote_copy(
                src_ref=x_refs[a] if own else slot(a, *block), dst_ref=slot(a, *block),
                send_sem=send_sems.at[a, k], recv_sem=recv_sems.at[a, k], device_id=to, device_id_type=MESH)

        mine = [pltpu.make_async_copy(x_refs[a], slot(a, *me), local_sems.at[a]) for a in range(n)]
        for cp in mine:
            cp.start()
        first = [copy(a, 1 + j, me, (*chip, c), own=True) for j, chip in enumerate(chips) for a in range(n)]
        first += [copy(a, 0, me, sibling, own=True) for a in range(n)]
        for cp in first:
            cp.start()
        passed = []
        for j, chip in enumerate(chips):
            for a in range(n):
                copy(a, 1 + j, (*chip, c), me).wait_recv()
                passed.append(copy(a, 4 + j, (*chip, c), sibling))
                passed[-1].start()
        for a in range(n):
            copy(a, 0, sibling, me).wait_recv()
        for j, chip in enumerate(chips):
            for a in range(n):
                copy(a, 4 + j, (*chip, 1 - c), me).wait_recv()
        for cp in first + passed:
            cp.wait_send()
        for cp in mine:
            cp.wait()

    return pl.pallas_call(
        body, name=name, in_specs=[ANY] * n, out_specs=[ANY] * n,
        out_shape=[jax.ShapeDtypeStruct((N_DEV,) + s.shape, s.dtype) for s in shards],
        scratch_shapes=[pltpu.SemaphoreType.DMA((n, 7)), pltpu.SemaphoreType.DMA((n, 7)), pltpu.SemaphoreType.DMA((n,))],
    )(*shards)


HBM = pl.BlockSpec(memory_space=pltpu.HBM)
SEM = pl.BlockSpec(memory_space=pltpu.SEMAPHORE)
DATAFLOW = pltpu.SideEffectType.DATAFLOW_SIDE_EFFECTING


def _exchange_copies(gather, x_refs, land_refs, send_sems, recv_sems, local_sems):
    n = len(x_refs)
    x, y, c = _position()
    me = 4 * x + 2 * y + c

    def src(a, slot):
        return x_refs[a] if gather else x_refs[a].at[slot]

    mine = [pltpu.make_async_copy(src(a, me), land_refs[a].at[me], local_sems.at[a]) for a in range(n)]
    sends, recvs = [], []
    for k in range(1, N_DEV):
        px, py, pc = (x + (k >> 2)) % 2, (y + ((k >> 1) & 1)) % 2, (c + (k & 1)) % 2
        peer = 4 * px + 2 * py + pc
        for a in range(n):
            sems = dict(send_sem=send_sems.at[7 * a + k - 1], recv_sem=recv_sems.at[7 * a + k - 1],
                        device_id=(px, py, pc), device_id_type=MESH)
            sends.append(pltpu.make_async_remote_copy(src_ref=src(a, peer), dst_ref=land_refs[a].at[me], **sems))
            recvs.append(pltpu.make_async_remote_copy(src_ref=src(a, me), dst_ref=land_refs[a].at[peer], **sems))
    return mine, sends, recvs


def _exchange_start(parts, *, gather, name):
    n = len(parts)

    def body(*refs):
        x_refs, land_refs = refs[:n], refs[n:2 * n]
        send_sems, recv_sems, local_sems = refs[2 * n:2 * n + 3]
        token = refs[-1]
        mine, sends, _ = _exchange_copies(gather, x_refs, land_refs, send_sems, recv_sems, local_sems)
        for cp in mine + sends:
            cp.start()
        token[...] = jnp.zeros_like(token)

    sems = (pltpu.SemaphoreType.DMA((7 * n,)), pltpu.SemaphoreType.DMA((7 * n,)), pltpu.SemaphoreType.DMA((n,)))
    thru = tuple(pltpu.HBM(p.shape, p.dtype) for p in parts)
    land = tuple(pltpu.HBM(((N_DEV,) if gather else ()) + p.shape, p.dtype) for p in parts)
    res = pl.pallas_call(
        body, name=name, in_specs=[HBM] * (2 * n),
        out_specs=(SEM, SEM, SEM) + (HBM,) * (2 * n) + (pl.BlockSpec(memory_space=pltpu.VMEM),),
        out_shape=sems + thru + land + (jax.ShapeDtypeStruct((8, 128), F32),),
        input_output_aliases={a: 3 + a for a in range(2 * n)},
        compiler_params=pltpu.CompilerParams(has_side_effects=DATAFLOW),
    )(*[pltpu.with_memory_space_constraint(p, pltpu.HBM) for p in parts],
      *[pltpu.with_memory_space_constraint(lax.empty(z.shape, z.dtype), pltpu.HBM) for z in land])
    return res[:3], res[3:3 + n], res[3 + n:3 + 2 * n], res[-1]


def _exchange_wait(sems, parts, landing, after, *, gather, name):
    n = len(parts)

    def body(*refs):
        x_refs, land_refs = refs[:n], refs[n:2 * n]
        send_sems, recv_sems, local_sems = refs[2 * n:2 * n + 3]
        mine, sends, recvs = _exchange_copies(gather, x_refs, land_refs, send_sems, recv_sems, local_sems)
        for cp in recvs:
            cp.wait_recv()
        for cp in sends:
            cp.wait_send()
        for cp in mine:
            cp.wait()

    thru = tuple(pltpu.HBM(p.shape, p.dtype) for p in tuple(parts) + tuple(landing))
    res = pl.pallas_call(
        body, name=name, in_specs=[HBM] * (2 * n) + [SEM, SEM, SEM, ANY], out_specs=(HBM,) * (2 * n),
        out_shape=thru, input_output_aliases={a: a for a in range(2 * n)},
        compiler_params=pltpu.CompilerParams(has_side_effects=DATAFLOW),
    )(*parts, *landing, *sems, after)
    return res[n:]


def _ffn_fwd(x, gain, wg_in, wg_out, tag):
    T, D = x.shape
    fb, rb = wg_in.shape[-1], wg_out.shape[-2]
    tm, tn = min(T, 1024), 512
    h = _rmsnorm_fwd(x, gain, name=f"{tag}_norm")
    p = _mm(name=f"{tag}_in", grid=(T // tm, N_DEV, 1), tile=(tm, fb),
            a=h, a_spec=pl.BlockSpec((tm, D), lambda i, j, k: (i, 0)),
            b=wg_in, b_spec=pl.BlockSpec((None, D, fb), lambda i, j, k: (j, 0, 0)),
            out_shape=jax.ShapeDtypeStruct((N_DEV, T, fb), BF16), o_spec=pl.BlockSpec((None, tm, fb), lambda i, j, k: (j, i, 0)))
    a = _swiglu_fwd(p, name=f"{tag}_act")
    y = _mm(name=f"{tag}_out", grid=(T // tm, D // tn, FF_HALF), tile=(tm, tn), resid=x, scale=0.5,
            a=a, a_spec=pl.BlockSpec((None, tm, fb), lambda i, j, k: (k, i, 0)),
            b=wg_out.reshape(N_DEV * rb, D), b_spec=pl.BlockSpec((fb, tn), lambda i, j, k: (k, j)),
            out_shape=jax.ShapeDtypeStruct((T, D), F32), o_spec=pl.BlockSpec((tm, tn), lambda i, j, k: (i, j)))
    return y, (x, h, p, a)


def _ffn_bwd(dy, saved, gain, wg_in, wg_out, tag, on_weight_grads=None):
    x, h, p, a = saved
    T, D = x.shape
    fb, rb = wg_in.shape[-1], wg_out.shape[-2]
    tm, tn = min(T, 1024), 512
    da = _mm(name=f"{tag}_out_dx", grid=(T // tm, FF_HALF, 1), tile=(tm, fb), tb=True, scale=0.5,
             a=dy, a_spec=pl.BlockSpec((tm, D), lambda i, j, k: (i, 0)),
             b=wg_out.reshape(N_DEV * rb, D), b_spec=pl.BlockSpec((fb, D), lambda i, j, k: (j, 0)),
             out_shape=jax.ShapeDtypeStruct((FF_HALF, T, fb), BF16), o_spec=pl.BlockSpec((None, tm, fb), lambda i, j, k: (j, i, 0)))
    d_w_out = _mm(name=f"{tag}_out_dw", grid=(FF_HALF, D // tn, 1), tile=(fb, tn), ta=True, scale=0.5,
                  a=a, a_spec=pl.BlockSpec((None, T, fb), lambda i, j, k: (i, 0, 0)),
                  b=dy, b_spec=pl.BlockSpec((T, tn), lambda i, j, k: (0, j)),
                  out_shape=jax.ShapeDtypeStruct((FF_HALF, fb, D), BF16), o_spec=pl.BlockSpec((None, fb, tn), lambda i, j, k: (i, 0, j)))
    dp = _swiglu_bwd(da, p, name=f"{tag}_act_bwd")
    d_w_in = _mm(name=f"{tag}_in_dw", grid=(1, N_DEV, 1), tile=(D, fb), ta=True,
                 a=h, a_spec=pl.BlockSpec((T, D), lambda i, j, k: (0, 0)),
                 b=dp, b_spec=pl.BlockSpec((None, T, fb), lambda i, j, k: (j, 0, 0)),
                 out_shape=jax.ShapeDtypeStruct((N_DEV, D, fb), BF16), o_spec=pl.BlockSpec((None, D, fb), lambda i, j, k: (j, 0, 0)))
    dh = _mm(name=f"{tag}_in_dx", grid=(T // tm, D // tn, N_DEV), tile=(tm, tn), tb=True,
             a=dp, a_spec=pl.BlockSpec((None, tm, fb), lambda i, j, k: (k, i, 0)),
             b=wg_in, b_spec=pl.BlockSpec((None, tn, fb), lambda i, j, k: (k, j, 0)),
             out_shape=jax.ShapeDtypeStruct((T, D), F32), o_spec=pl.BlockSpec((tm, tn), lambda i, j, k: (i, j)))
    d_w_out = d_w_out.reshape(N_DEV, rb, D)
    if on_weight_grads is not None:
        gain = gain + on_weight_grads(d_w_in, d_w_out)[0, 0]
    dx, d_gain = _rmsnorm_bwd(dh, x, gain, dy, name=f"{tag}_norm_bwd")
    return dx, d_gain, d_w_in, d_w_out


def _square_mm(a, wg, *, name, transposed=False, out_dtype=F32, resid=None):
    T, D = a.shape
    w = wg.reshape(D, D)
    return _matmul(a, w, tb=transposed, name=name, out_dtype=out_dtype, resid=resid)


def _head_rows(cols, T):
    return cols.T.reshape(HEADS, T // DN_CHUNK, 1, DN_CHUNK)


def _mixer_fwd(x, w, big, tag):
    T = x.shape[0]
    h = _rmsnorm_fwd(x, w["mix_norm"], name=f"{tag}_norm")
    proj = _matmul(h, big["w_main"], name=f"{tag}_proj")
    scal = _matmul(h, big["w_scal"], name=f"{tag}_proj_scal", tn=N_SCAL)
    qkv = _conv_fwd(proj, big["conv_w"], name=f"{tag}_conv")
    b_rows = _head_rows(scal[:, 0:HEADS], T)
    a_rows = _head_rows(scal[:, HEADS:2 * HEADS], T)
    o_a, states = _dn_fwd(qkv, b_rows, a_rows, w["hp"], name=f"{tag}_dn")
    oa_n = _gated_norm_fwd(o_a, proj, w["dn_out_norm"], name=f"{tag}_dn_norm")
    ya = _square_mm(oa_n, big["w_branch_a"], name=f"{tag}_branch_a")
    o_b, ltot = _sb_fwd(proj, w["sb_q_norm"], w["sb_k_norm"], name=f"{tag}_sb")
    yb = _square_mm(o_b, big["w_branch_b"], name=f"{tag}_branch_b")
    merged = _merge_fwd(ya, yb, proj, name=f"{tag}_merge")
    y = _square_mm(merged, big["w_out"], name=f"{tag}_out", resid=x)
    return y, (x, h, proj, qkv, b_rows, a_rows, o_a, states, oa_n, ya, o_b, ltot, yb, merged)


def _mixer_bwd(dy, saved, w, big, tag):
    x, h, proj, qkv, b_rows, a_rows, o_a, states, oa_n, ya, o_b, ltot, yb, merged = saved
    T = x.shape[0]
    g = {}
    d_merged = _square_mm(dy, big["w_out"], transposed=True, name=f"{tag}_out_dx", out_dtype=BF16)
    g["w_out"] = _matmul(merged, dy, ta=True, name=f"{tag}_out_dw", out_dtype=BF16)
    d_ya, d_yb, d_ga, d_gb = _merge_bwd(d_merged, ya, yb, proj, name=f"{tag}_merge_bwd")
    d_oan = _square_mm(d_ya, big["w_branch_a"], transposed=True, name=f"{tag}_branch_a_dx")
    g["w_branch_a"] = _matmul(oa_n, d_ya, ta=True, name=f"{tag}_branch_a_dw", out_dtype=BF16)
    d_ob = _square_mm(d_yb, big["w_branch_b"], transposed=True, name=f"{tag}_branch_b_dx")
    g["w_branch_b"] = _matmul(o_b, d_yb, ta=True, name=f"{tag}_branch_b_dw", out_dtype=BF16)
    d_oa, d_z, g["dn_out_norm"] = _gated_norm_bwd(d_oan, o_a, proj, w["dn_out_norm"], name=f"{tag}_dn_norm_bwd")
    d_qkv, d_b_rows, d_a_rows, d_hp = _dn_bwd(qkv, b_rows, a_rows, w["hp"], states, d_oa, name=f"{tag}_dn_bwd")
    g["dn_a_log"] = d_hp[:, 0, 0]
    g["dn_dt_bias"] = d_hp[:, 1, 0]
    d_conv_in, g["conv_w"] = _conv_bwd(d_qkv, proj, big["conv_w"], name=f"{tag}_conv_bwd")
    d_sbq, d_sbk, d_sbv, g["sb_q_norm"], g["sb_k_norm"] = _sb_bwd(
        proj, w["sb_q_norm"], w["sb_k_norm"], ltot, d_ob, name=f"{tag}_sb_bwd")
    d_proj = jnp.concatenate([d_conv_in, d_z, d_sbq, d_sbk, d_sbv, d_ga, d_gb], axis=1)
    d_scal = jnp.concatenate([d_b_rows.reshape(HEADS, T).T, d_a_rows.reshape(HEADS, T).T,
                              jnp.zeros((T, N_SCAL - 2 * HEADS), F32)], axis=1).astype(BF16)
    g["w_main"] = _matmul(h, d_proj, ta=True, name=f"{tag}_proj_dw", out_dtype=BF16)
    g["w_scal"] = _matmul(h, d_scal, ta=True, name=f"{tag}_proj_scal_dw", out_dtype=BF16, tn=N_SCAL)
    dh_scal = _matmul(d_scal, big["w_scal"], tb=True, name=f"{tag}_proj_scal_dx")
    dh = _matmul(d_proj, big["w_main"], tb=True, name=f"{tag}_proj_dx", tk=N_MAIN // 4, resid=dh_scal)
    dx, g["mix_norm"] = _rmsnorm_bwd(dh, x, w["mix_norm"], dy, name=f"{tag}_norm_bwd")
    return dx, g


def _local_step(x, target, layers, weights_of, on_layer_grads):
    saved, bigs = [], []
    for l, w in enumerate(layers):
        big = weights_of(l, x)
        x, s1 = _ffn_fwd(x, w["ffn1_norm"], big["ffn1_w_in"], big["ffn1_w_out"], f"l{l}_ffn1")
        x, s2 = _mixer_fwd(x, w, big, f"l{l}_mix")
        x, s3 = _ffn_fwd(x, w["ffn2_norm"], big["ffn2_w_in"], big["ffn2_w_out"], f"l{l}_ffn2")
        saved.append((s1, s2, s3))
        bigs.append(big)
    loss, dx = _loss_head(x, target, name="loss_head")
    for l in reversed(range(len(layers))):
        w, big = layers[l], bigs[l]
        s1, s2, s3 = saved[l]
        dx, g_n2, g_in2, g_out2 = _ffn_bwd(dx, s3, w["ffn2_norm"], big["ffn2_w_in"], big["ffn2_w_out"], f"l{l}_ffn2")
        dx, g = _mixer_bwd(dx, s2, w, big, f"l{l}_mix")
        g.update(ffn2_norm=g_n2, ffn2_w_in=g_in2, ffn2_w_out=g_out2)

        def last_grads(g_in1, g_out1, l=l, g=g):
            g.update(ffn1_w_in=g_in1, ffn1_w_out=g_out1)
            return on_layer_grads(l, g)

        dx, g["ffn1_norm"], _, _ = _ffn_bwd(dx, s1, w["ffn1_norm"], big["ffn1_w_in"], big["ffn1_w_out"], f"l{l}_ffn1",
                                            on_weight_grads=last_grads)
    return loss, dx


_BIG = ("ffn1_w_in", "ffn1_w_out", "w_in", "w_branch_a", "w_branch_b", "w_out", "ffn2_w_in", "ffn2_w_out")
_SMALL = ("ffn1_norm", "mix_norm", "ffn2_norm", "dn_a_log", "dn_dt_bias", "dn_out_norm", "sb_q_norm", "sb_k_norm")
_ORDER = ("ffn1_norm", "ffn1_w_in", "ffn1_w_out", "mix_norm", "w_in", "dn_conv_w", "dn_a_log", "dn_dt_bias", "dn_out_norm",
          "sb_q_norm", "sb_k_norm", "w_branch_a", "w_branch_b", "w_out", "ffn2_norm", "ffn2_w_in", "ffn2_w_out")
COL_SCAL = 4 * D_MODEL


def _pad_rows(a, multiple):
    pad = (-a.shape[-2]) % multiple
    return a if pad == 0 else jnp.pad(a, [(0, 0)] * (a.ndim - 2) + [(0, pad), (0, 0)])


def _lane_rows(a):
    flat = a.reshape(-1)
    flat = jnp.pad(flat, (0, (-flat.shape[0]) % 128))
    return flat.reshape(-1, 128)


def _pack_small(named):
    pieces, spans, r = [], {}, 0
    for n, a in named:
        rows = _lane_rows(a)
        spans[n] = (r, r + rows.shape[0], a.shape)
        r += rows.shape[0]
        pieces.append(rows)
    return _pad_rows(jnp.concatenate(pieces, axis=0), 8), spans


def _unpack_small(packed, spans, n):
    r0, r1, shape = spans[n]
    return packed[r0:r1].reshape(-1)[:math.prod(shape)].reshape(shape)


def kernel(x, ffn1_norm, ffn1_w_in, ffn1_w_out, mix_norm, w_in, dn_conv_w, dn_a_log, dn_dt_bias, dn_out_norm, sb_q_norm, sb_k_norm, w_branch_a, w_branch_b, w_out, ffn2_norm, ffn2_w_in, ffn2_w_out, loss_target, m_ffn1_norm, m_ffn1_w_in, m_ffn1_w_out, m_mix_norm, m_w_in, m_dn_conv_w, m_dn_a_log, m_dn_dt_bias, m_dn_out_norm, m_sb_q_norm, m_sb_k_norm, m_w_branch_a, m_w_branch_b, m_w_out, m_ffn2_norm, m_ffn2_w_in, m_ffn2_w_out, v_ffn1_norm, v_ffn1_w_in, v_ffn1_w_out, v_mix_norm, v_w_in, v_dn_conv_w, v_dn_a_log, v_dn_dt_bias, v_dn_out_norm, v_sb_q_norm, v_sb_k_norm, v_w_branch_a, v_w_branch_b, v_w_out, v_ffn2_norm, v_ffn2_w_in, v_ffn2_w_out):
    given = dict(locals())
    weights = {n: given[n] for n in _ORDER}
    mom_m = {n: given["m_" + n] for n in _ORDER}
    mom_v = {n: given["v_" + n] for n in _ORDER}
    L = ffn1_norm.shape[0]
    ax, ay, ac = _position()
    my_slot = 4 * ax + 2 * ay + ac

    conv_cols = dn_conv_w.shape[-1]
    shards = lambda l: [weights[n][l].astype(BF16) for n in _BIG]
    *first_layer, conv_full = _all_gather(shards(0) + [_pad_rows(_lane_rows(dn_conv_w), 8)], name="gather_l0")
    conv_full = conv_full.reshape(N_DEV, -1)[:, :L * DN_CONV * conv_cols].reshape(N_DEV, L, DN_CONV, conv_cols)
    conv_full = conv_full.transpose(1, 2, 0, 3).reshape(L, DN_CONV, N_DEV * conv_cols)
    arriving, started = {}, 0.0
    for l in range(1, L):
        *arriving[l], token = _exchange_start(shards(l), gather=True, name=f"gather_start_l{l}")
        started = started + token[0, 0]

    def weights_of(l, x_in):
        arrays = first_layer if l == 0 else _exchange_wait(*arriving[l], x_in, gather=True, name=f"gather_wait_l{l}")
        big = dict(zip(_BIG, arrays))
        wi = big.pop("w_in").transpose(1, 0, 2).reshape(D_MODEL, N_IN)
        big["w_main"] = jnp.concatenate([wi[:, :COL_SCAL], wi[:, COL_SCAL + 2 * HEADS:]], axis=1)
        big["w_scal"] = jnp.pad(wi[:, COL_SCAL:COL_SCAL + 2 * HEADS], ((0, 0), (0, N_SCAL - 2 * HEADS)))
        big["conv_w"] = conv_full[l]
        return big

    layers = []
    for l in range(L):
        hp = jnp.concatenate([jnp.broadcast_to(dn_a_log[l][:, None, None], (HEADS, 1, 128)),
                              jnp.broadcast_to(dn_dt_bias[l][:, None, None], (HEADS, 1, 128)),
                              jnp.zeros((HEADS, 6, 128), F32)], axis=1)
        layers.append(dict(ffn1_norm=ffn1_norm[l][None], mix_norm=mix_norm[l][None], hp=hp,
                           dn_out_norm=dn_out_norm[l][None], sb_q_norm=sb_q_norm[l][None],
                           sb_k_norm=sb_k_norm[l][None], ffn2_norm=ffn2_norm[l][None]))

    grads = [None] * L
    in_flight = [None] * L

    def on_layer_grads(l, g):
        grads[l] = g
        g_w_in = jnp.concatenate([g["w_main"][:, :COL_SCAL], g["w_scal"][:, :2 * HEADS], g["w_main"][:, COL_SCAL:]], axis=1)
        parts = dict(g, w_in=g_w_in.reshape(D_MODEL, N_DEV, N_IN // N_DEV).transpose(1, 0, 2))
        for n in ("w_branch_a", "w_branch_b", "w_out"):
            parts[n] = g[n].reshape(N_DEV, D_MODEL // N_DEV, D_MODEL)
        *in_flight[l], token = _exchange_start([parts[n] for n in _BIG], gather=False, name=f"scatter_start_l{l}")
        return token

    layers[0]["ffn1_norm"] = layers[0]["ffn1_norm"] + started
    loss_row, dx = _local_step(x[0], loss_target[0], layers, weights_of, on_layer_grads)
    loss = lax.psum(loss_row[0, 0], ("x", "y", "c"))

    results = {n: None for n in _BIG}
    for l in reversed(range(L)):
        landed = _exchange_wait(*in_flight[l], dx, gather=False, name=f"scatter_wait_l{l}")
        for n, parts in zip(_BIG, landed):
            _, a, b = weights[n].shape
            results[n] = _adamw(parts, weights[n].reshape(L * a, b), mom_m[n].reshape(L * a, b),
                                mom_v[n].reshape(L * a, b), layer=l, earlier=results[n], name=f"adamw_{n}_l{l}")
    out = {n: tuple(t.reshape(weights[n].shape) for t in results[n]) for n in _BIG}

    small_grads = [(n, jnp.stack([g[n].reshape(weights[n].shape[1:]) for g in grads])) for n in _SMALL]
    small_packed, spans = _pack_small(small_grads + [("conv", jnp.stack([g["conv_w"] for g in grads]))])
    small_sum = _sum_parts(_all_gather([small_packed], name="gather_small_grads")[0], name="sum_small_grads")
    rep_rows = spans["conv"][0]
    pack_rep = lambda d: _pad_rows(_pack_small([(n, d[n]) for n in _SMALL])[0], 8)
    rep_pad = (-rep_rows) % 8
    g_rep = jnp.pad(small_sum[:rep_rows], ((0, rep_pad), (0, 0)))
    res = _adamw(g_rep[None], pack_rep(weights), pack_rep(mom_m), pack_rep(mom_v), name="adamw_replicated")
    for n in _SMALL:
        out[n] = tuple(_unpack_small(t, spans, n) for t in res)
    conv_sum = _unpack_small(small_sum, spans, "conv")
    conv_mine = lax.dynamic_slice_in_dim(conv_sum, my_slot * conv_cols, conv_cols, axis=2).reshape(L * DN_CONV, conv_cols)
    flat = lambda t: t.reshape(L * DN_CONV, conv_cols)
    res = _adamw(conv_mine[None], flat(dn_conv_w), flat(m_dn_conv_w), flat(v_dn_conv_w), name="adamw_conv")
    out["dn_conv_w"] = tuple(t.reshape(L, DN_CONV, conv_cols) for t in res)

    return (loss, dx[None], *[out[n][0] for n in _ORDER], *[out[n][1] for n in _ORDER],
            *[out[n][2] for n in _ORDER], *[out[n][3] for n in _ORDER])
```

```python
import functools
import math

import jax
import jax.numpy as jnp
from jax import lax
from jax.experimental import pallas as pl
from jax.experimental.pallas import tpu as pltpu

F32 = jnp.float32
BF16 = jnp.bfloat16

N_DEV = 8
D_MODEL = 1024
DEPTH = 4
D_FF = 2816
HEADS = 8
HEAD_DIM = 128
DN_CHUNK = 64
DN_CONV = 4
DN_GROUP = 4
DN_HEADS = 2
SB_BLOCK = 128
SB_KEY_TILE = 512
RMS_EPS = 1e-6
L2_EPS = 1e-6
N_IN = 9232
N_MAIN = 9216
N_SCAL = 128
QK_SCALE = HEAD_DIM ** -0.5

ADAM_LR = 0.001
ADAM_B1 = 0.9
ADAM_B2 = 0.999
ADAM_EPS = 1e-08
ADAM_WD = 0.01
ADAM_STEP = 10

V7X_VMEM_LIMIT = 56 * 1024 * 1024
MESH = pl.DeviceIdType.MESH
ANY = pl.BlockSpec(memory_space=pl.ANY)


def _params(sem=None, vmem=V7X_VMEM_LIMIT):
    return pltpu.CompilerParams(dimension_semantics=sem, vmem_limit_bytes=vmem)


def _sigmoid(x):
    return 1.0 / (1.0 + jnp.exp(-x))


def _softplus(x):
    return jnp.maximum(x, 0.0) + jnp.log(1.0 + jnp.exp(-jnp.abs(x)))


def _bdot(a, b, dims=(((1,), (0,)), ((), ()))):
    return lax.dot_general(a.astype(BF16), b.astype(BF16), dims, preferred_element_type=F32)


_NT = (((1,), (1,)), ((), ()))
_TN = (((0,), (0,)), ((), ()))


def _hdot(a, b, dims=(((1,), (0,)), ((), ()))):
    a_hi = a.astype(BF16)
    b_hi = b.astype(BF16)
    a_lo = (a - a_hi.astype(F32)).astype(BF16)
    b_lo = (b - b_hi.astype(F32)).astype(BF16)
    dot = functools.partial(lax.dot_general, dimension_numbers=dims, preferred_element_type=F32)
    return dot(a_hi, b_hi) + (dot(a_hi, b_lo) + dot(a_lo, b_hi))


def _hdot_tn(a, b):
    return _hdot(a, b, _TN)


def _mm(*, name, grid, a, a_spec, b, b_spec, out_shape, o_spec, tile, ta=False, tb=False, resid=None, scale=1.0):
    nk = grid[2]
    dims = (((0 if ta else 1,), (1 if tb else 0,)), ((), ()))

    def flat(v):
        return v if v.ndim == 2 else v.reshape(-1, v.shape[-1])

    def body(*refs):
        a_ref, b_ref = refs[:2]
        r_ref = refs[2] if resid is not None else None
        o_ref = refs[3] if resid is not None else refs[2]
        part = lax.dot_general(flat(a_ref[...]).astype(BF16), flat(b_ref[...]).astype(BF16), dims,
                               preferred_element_type=F32)

        def finish(acc):
            if scale != 1.0:
                acc = acc * scale
            if r_ref is not None:
                acc = r_ref[...] + acc
            o_ref[...] = acc.astype(o_ref.dtype)

        if nk == 1:
            finish(part)
        else:
            acc_ref = refs[-1]
            k = pl.program_id(2)

            @pl.when(k == 0)
            def _():
                acc_ref[...] = part

            @pl.when(k > 0)
            def _():
                acc_ref[...] += part

            @pl.when(k == nk - 1)
            def _():
                finish(acc_ref[...])

    in_specs = [a_spec, b_spec] + ([pl.BlockSpec(tile, lambda i, j, k: (i, j))] if resid is not None else [])
    args = (a, b) + ((resid,) if resid is not None else ())
    return pl.pallas_call(
        body, name=name, grid=grid, in_specs=in_specs, out_specs=o_spec, out_shape=out_shape,
        scratch_shapes=[pltpu.VMEM(tile, F32)] if nk > 1 else [],
        compiler_params=_params(("parallel", "parallel", "arbitrary")),
    )(*args)


def _matmul(a, b, *, name, ta=False, tb=False, out_dtype=F32, tm=None, tn=None, tk=None, resid=None, scale=1.0):
    if ta:
        K, M = a.shape
    else:
        M, K = a.shape
    N = b.shape[0] if tb else b.shape[1]
    tm = tm or min(M, 1024)
    tn = tn or min(N, 512)
    tk = tk or K
    assert M % tm == 0 and N % tn == 0 and K % tk == 0, (name, M, N, K, tm, tn, tk)
    a_spec = pl.BlockSpec((tk, tm), lambda i, j, k: (k, i)) if ta else pl.BlockSpec((tm, tk), lambda i, j, k: (i, k))
    b_spec = pl.BlockSpec((tn, tk), lambda i, j, k: (j, k)) if tb else pl.BlockSpec((tk, tn), lambda i, j, k: (k, j))
    return _mm(name=name, grid=(M // tm, N // tn, K // tk), a=a, a_spec=a_spec, b=b, b_spec=b_spec,
               out_shape=jax.ShapeDtypeStruct((M, N), out_dtype), o_spec=pl.BlockSpec((tm, tn), lambda i, j, k: (i, j)),
               tile=(tm, tn), ta=ta, tb=tb, resid=resid, scale=scale)


ROW_TILE = 256


def _rmsnorm_fwd(x, gain, *, name):
    T, D = x.shape

    def body(x_ref, g_ref, o_ref):
        xf = x_ref[...]
        r = lax.rsqrt(jnp.mean(xf * xf, axis=-1, keepdims=True) + RMS_EPS)
        o_ref[...] = (xf * r * g_ref[...]).astype(o_ref.dtype)

    return pl.pallas_call(
        body, name=name, grid=(T // ROW_TILE,),
        in_specs=[pl.BlockSpec((ROW_TILE, D), lambda i: (i, 0)), pl.BlockSpec((1, D), lambda i: (0, 0))],
        out_specs=pl.BlockSpec((ROW_TILE, D), lambda i: (i, 0)),
        out_shape=jax.ShapeDtypeStruct((T, D), BF16), compiler_params=_params(("parallel",)),
    )(x, gain)


def _rmsnorm_bwd(dh, x, gain, dres, *, name):
    T, D = x.shape

    def body(dh_ref, x_ref, g_ref, res_ref, dx_ref, dg_ref):
        xf = x_ref[...]
        r = lax.rsqrt(jnp.mean(xf * xf, axis=-1, keepdims=True) + RMS_EPS)
        y = xf * r
        dh_v = dh_ref[...].astype(F32)
        dy = dh_v * g_ref[...]
        dx_ref[...] = res_ref[...] + r * (dy - y * jnp.mean(dy * y, axis=-1, keepdims=True))

        @pl.when(pl.program_id(0) == 0)
        def _():
            dg_ref[...] = jnp.zeros_like(dg_ref)

        dg_ref[...] += jnp.sum(dh_v * y, axis=0, keepdims=True)

    row = pl.BlockSpec((ROW_TILE, D), lambda i: (i, 0))
    vec = pl.BlockSpec((1, D), lambda i: (0, 0))
    return pl.pallas_call(
        body, name=name, grid=(T // ROW_TILE,), in_specs=[row, row, vec, row], out_specs=(row, vec),
        out_shape=(jax.ShapeDtypeStruct((T, D), F32), jax.ShapeDtypeStruct((1, D), F32)),
        compiler_params=_params(("arbitrary",)),
    )(dh, x, gain, dres)


FF_HALF = N_DEV // 2


def _swiglu_fwd(p, *, name):
    _, T, fb = p.shape

    def body(g_ref, u_ref, o_ref):
        g = g_ref[...].astype(F32)
        o_ref[...] = (g * _sigmoid(g) * u_ref[...].astype(F32)).astype(o_ref.dtype)

    blk = (None, ROW_TILE, fb)
    return pl.pallas_call(
        body, name=name, grid=(T // ROW_TILE, FF_HALF),
        in_specs=[pl.BlockSpec(blk, lambda i, j: (j, i, 0)), pl.BlockSpec(blk, lambda i, j: (j + FF_HALF, i, 0))],
        out_specs=pl.BlockSpec(blk, lambda i, j: (j, i, 0)),
        out_shape=jax.ShapeDtypeStruct((FF_HALF, T, fb), BF16), compiler_params=_params(("parallel", "parallel")),
    )(p, p)


def _swiglu_bwd(da, p, *, name):
    _, T, fb = p.shape

    def body(da_ref, g_ref, u_ref, o_ref):
        g = g_ref[...].astype(F32)
        u = u_ref[...].astype(F32)
        d = da_ref[...].astype(F32)
        s = _sigmoid(g)
        o_ref[0] = (d * u * (s * (1.0 + g * (1.0 - s)))).astype(o_ref.dtype)
        o_ref[1] = (d * g * s).astype(o_ref.dtype)

    blk = (None, ROW_TILE, fb)
    out = pl.pallas_call(
        body, name=name, grid=(T // ROW_TILE, FF_HALF),
        in_specs=[pl.BlockSpec(blk, lambda i, j: (j, i, 0)), pl.BlockSpec(blk, lambda i, j: (j, i, 0)),
                  pl.BlockSpec(blk, lambda i, j: (j + FF_HALF, i, 0))],
        out_specs=pl.BlockSpec((2, None, ROW_TILE, fb), lambda i, j: (0, j, i, 0)),
        out_shape=jax.ShapeDtypeStruct((2, FF_HALF, T, fb), BF16), compiler_params=_params(("parallel", "parallel")),
    )(da, p, p)
    return out.reshape(2 * FF_HALF, T, fb)


COL_GATE_A = 7
COL_GATE_B = 8


def _merge_fwd(ya, yb, proj, *, name):
    T, D = ya.shape

    def body(ya_ref, yb_ref, ga_ref, gb_ref, o_ref):
        o_ref[...] = (_sigmoid(ga_ref[...]) * ya_ref[...] + _sigmoid(gb_ref[...]) * yb_ref[...]).astype(o_ref.dtype)

    row = pl.BlockSpec((ROW_TILE, D), lambda i: (i, 0))
    return pl.pallas_call(
        body, name=name, grid=(T // ROW_TILE,),
        in_specs=[row, row, pl.BlockSpec((ROW_TILE, D), lambda i: (i, COL_GATE_A)),
                  pl.BlockSpec((ROW_TILE, D), lambda i: (i, COL_GATE_B))],
        out_specs=row, out_shape=jax.ShapeDtypeStruct((T, D), BF16), compiler_params=_params(("parallel",)),
    )(ya, yb, proj, proj)


def _merge_bwd(dm, ya, yb, proj, *, name):
    T, D = ya.shape

    def body(dm_ref, ya_ref, yb_ref, ga_ref, gb_ref, dya_ref, dyb_ref, dga_ref, dgb_ref):
        d = dm_ref[...].astype(F32)
        sa = _sigmoid(ga_ref[...])
        sb = _sigmoid(gb_ref[...])
        dya_ref[...] = (d * sa).astype(BF16)
        dyb_ref[...] = (d * sb).astype(BF16)
        dga_ref[...] = (d * ya_ref[...] * sa * (1.0 - sa)).astype(BF16)
        dgb_ref[...] = (d * yb_ref[...] * sb * (1.0 - sb)).astype(BF16)

    row = pl.BlockSpec((ROW_TILE, D), lambda i: (i, 0))
    out = jax.ShapeDtypeStruct((T, D), BF16)
    return pl.pallas_call(
        body, name=name, grid=(T // ROW_TILE,),
        in_specs=[row, row, row, pl.BlockSpec((ROW_TILE, D), lambda i: (i, COL_GATE_A)),
                  pl.BlockSpec((ROW_TILE, D), lambda i: (i, COL_GATE_B))],
        out_specs=(row, row, row, row), out_shape=(out, out, out, out), compiler_params=_params(("parallel",)),
    )(dm, ya, yb, proj, proj)


def _loss_head(y, target, *, name):
    T, D = y.shape

    def body(y_ref, t_ref, loss_ref, dy_ref):
        err = y_ref[...] - t_ref[...]
        dy_ref[...] = err * (1.0 / D)

        @pl.when(pl.program_id(0) == 0)
        def _():
            loss_ref[...] = jnp.zeros_like(loss_ref)

        loss_ref[...] += 0.5 * jnp.sum(jnp.sum(err * err, axis=-1, keepdims=True) * (1.0 / D), axis=0, keepdims=True)

    row = pl.BlockSpec((ROW_TILE, D), lambda i: (i, 0))
    return pl.pallas_call(
        body, name=name, grid=(T // ROW_TILE,), in_specs=[row, row],
        out_specs=(pl.BlockSpec((1, 128), lambda i: (0, 0)), row),
        out_shape=(jax.ShapeDtypeStruct((1, 128), F32), jax.ShapeDtypeStruct((T, D), F32)),
        compiler_params=_params(("arbitrary",)),
    )(y, target)


CONV_PAD = 8


def _conv_taps(w, xp, T, first):
    acc = w[0:1, :] * xp[pl.ds(first, T), :]
    for i in range(1, DN_CONV):
        acc = acc + w[i:i + 1, :] * xp[pl.ds(first + i, T), :]
    return acc


def _conv_fwd(proj, conv_w, *, name):
    T = proj.shape[0]

    def body(x_ref, w_ref, o_ref, xp):
        xp[0:CONV_PAD, :] = jnp.zeros((CONV_PAD, HEAD_DIM), F32)
        xp[CONV_PAD:, :] = x_ref[...]
        y = _conv_taps(w_ref[...], xp, T, CONV_PAD - (DN_CONV - 1))
        s = y * _sigmoid(y)
        n = s * lax.rsqrt(jnp.sum(s * s, axis=-1, keepdims=True) + L2_EPS)
        o_ref[0] = jnp.where(pl.program_id(0) < 2, n, s)

    return pl.pallas_call(
        body, name=name, grid=(3, HEADS),
        in_specs=[pl.BlockSpec((T, HEAD_DIM), lambda c, h: (0, c * HEADS + h)),
                  pl.BlockSpec((DN_CONV, HEAD_DIM), lambda c, h: (0, c * HEADS + h))],
        out_specs=pl.BlockSpec((1, T, HEAD_DIM), lambda c, h: (c, 0, h)),
        out_shape=jax.ShapeDtypeStruct((3, T, D_MODEL), F32),
        scratch_shapes=[pltpu.VMEM((T + CONV_PAD, HEAD_DIM), F32)],
        compiler_params=_params(("parallel", "parallel")),
    )(proj, conv_w)


def _conv_bwd(dqkv, proj, conv_w, *, name):
    T = proj.shape[0]

    def body(d_ref, x_ref, w_ref, dx_ref, dw_ref, xp, dyp):
        xp[0:CONV_PAD, :] = jnp.zeros((CONV_PAD, HEAD_DIM), F32)
        xp[CONV_PAD:, :] = x_ref[...]
        w = w_ref[...]
        y = _conv_taps(w, xp, T, CONV_PAD - (DN_CONV - 1))
        sg = _sigmoid(y)
        s = y * sg
        r = lax.rsqrt(jnp.sum(s * s, axis=-1, keepdims=True) + L2_EPS)
        n = s * r
        d = d_ref[0]
        ds = jnp.where(pl.program_id(0) < 2, r * (d - n * jnp.sum(d * n, axis=-1, keepdims=True)), d)
        dy = ds * (sg * (1.0 + y * (1.0 - sg)))
        dyp[0:T, :] = dy
        dyp[T:, :] = jnp.zeros((CONV_PAD, HEAD_DIM), F32)
        dx = w[0:1, :] * dyp[pl.ds(DN_CONV - 1, T), :]
        for i in range(1, DN_CONV):
            dx = dx + w[i:i + 1, :] * dyp[pl.ds(DN_CONV - 1 - i, T), :]
        dx_ref[...] = dx.astype(dx_ref.dtype)
        for i in range(DN_CONV):
            dw_ref[i:i + 1, :] = jnp.sum(dy * xp[pl.ds(CONV_PAD - (DN_CONV - 1) + i, T), :], axis=0, keepdims=True)

    col = lambda c, h: (0, c * HEADS + h)
    return pl.pallas_call(
        body, name=name, grid=(3, HEADS),
        in_specs=[pl.BlockSpec((1, T, HEAD_DIM), lambda c, h: (c, 0, h)), pl.BlockSpec((T, HEAD_DIM), col),
                  pl.BlockSpec((DN_CONV, HEAD_DIM), col)],
        out_specs=(pl.BlockSpec((T, HEAD_DIM), col), pl.BlockSpec((DN_CONV, HEAD_DIM), col)),
        out_shape=(jax.ShapeDtypeStruct((T, 3 * D_MODEL), BF16), jax.ShapeDtypeStruct((DN_CONV, 3 * D_MODEL), F32)),
        scratch_shapes=[pltpu.VMEM((T + CONV_PAD, HEAD_DIM), F32), pltpu.VMEM((T + CONV_PAD, HEAD_DIM), F32)],
        compiler_params=_params(("parallel", "parallel")),
    )(dqkv, proj, conv_w)


def _inv_unit_lower(low, eye):
    x = eye - low
    power = _hdot(low, low, _B_NN)
    steps = int(math.log2(DN_CHUNK)) - 1
    for s in range(steps):
        x = x + _hdot(x, power, _B_NN)
        if s + 1 < steps:
            power = _hdot(power, power, _B_NN)
    return x


_B_NN = (((2,), (1,)), ((0,), (0,)))
_B_NT = (((2,), (2,)), ((0,), (0,)))
_B_TN = (((1,), (1,)), ((0,), (0,)))


def _dn_load(ref, lead, r0, group):
    rows = pl.ds(r0, group * DN_CHUNK)
    cols = lambda h: slice(h * HEAD_DIM, (h + 1) * HEAD_DIM)
    per_head = [(ref[rows, cols(h)] if lead is None else ref[lead, rows, cols(h)]).reshape(group, DN_CHUNK, HEAD_DIM)
                for h in range(DN_HEADS)]
    return jnp.stack(per_head, axis=1).reshape(group * DN_HEADS, DN_CHUNK, HEAD_DIM)


def _dn_chunk_setup(qkv_ref, b_ref, a_ref, hp_ref, n0, group):
    C = DN_CHUNK
    B = group * DN_HEADS
    r0 = pl.multiple_of(n0 * C, C)
    q = _dn_load(qkv_ref, 0, r0, group) * QK_SCALE
    k = _dn_load(qkv_ref, 1, r0, group)
    v = _dn_load(qkv_ref, 2, r0, group)
    ii = lax.broadcasted_iota(jnp.int32, (B, C, C), 1)
    jj = lax.broadcasted_iota(jnp.int32, (B, C, C), 2)
    eye_mask = ii == jj
    eye = jnp.where(eye_mask, 1.0, 0.0).astype(F32)

    def to_col(row):
        return jnp.sum(jnp.where(eye_mask, jnp.broadcast_to(row, (B, C, C)), 0.0), axis=2, keepdims=True)

    def to_row(col):
        return jnp.sum(jnp.where(eye_mask, jnp.broadcast_to(col, (B, C, C)), 0.0), axis=1, keepdims=True)

    def rows(ref):
        return jnp.stack([ref[h, pl.ds(n0, group)] for h in range(DN_HEADS)], axis=1).reshape(B, 1, C)

    def per_head(row):
        return jnp.stack([hp_ref[h, row:row + 1, 0:C] for h in range(DN_HEADS)] * group, axis=0)

    b_row = rows(b_ref)
    a_row = rows(a_ref)
    a_log = per_head(0)
    dt_b = per_head(1)
    beta_row = _sigmoid(b_row)
    neg_ea = -jnp.exp(a_log)
    g_row = neg_ea * _softplus(a_row + dt_b)
    gc_col = jnp.sum(jnp.where(jj <= ii, jnp.broadcast_to(g_row, (B, C, C)), 0.0), axis=2, keepdims=True)
    gc_row = to_row(gc_col)
    g_last = jnp.sum(g_row, axis=2, keepdims=True)
    beta = to_col(beta_row)
    low_incl = ii >= jj
    decay = jnp.exp(jnp.where(low_incl, gc_col - gc_row, -jnp.inf))
    eg = jnp.exp(gc_col)
    egl = jnp.exp(g_last - gc_col)
    el = jnp.exp(g_last)
    kb = k * beta
    pmat = _bdot(kb, k, _B_NT)
    low = jnp.where(ii > jj, pmat * decay, 0.0)
    tinv = _inv_unit_lower(low, eye)
    u = _hdot(tinv, v * beta, _B_NN)
    w = _hdot(tinv, kb * eg, _B_NN)
    qk = _bdot(q, k, _B_NT)
    attn = qk * decay
    return dict(q=q, k=k, v=v, ii=ii, jj=jj, to_col=to_col, to_row=to_row, b_row=b_row, a_row=a_row, dt_b=dt_b,
                beta_row=beta_row, neg_ea=neg_ea, g_row=g_row, gc_col=gc_col, g_last=g_last, beta=beta,
                decay=decay, eg=eg, egl=egl, el=el, kb=kb, pmat=pmat, tinv=tinv, u=u, w=w, qk=qk, attn=attn,
                qd=q * eg, kd=k * egl, r0=r0)


def _dn_store(ref, lead, r0, group, value):
    value = value.reshape(group, DN_HEADS, DN_CHUNK, HEAD_DIM)
    for h in range(DN_HEADS):
        block = value[:, h].reshape(group * DN_CHUNK, HEAD_DIM)
        if lead is None:
            ref[pl.ds(r0, group * DN_CHUNK), h * HEAD_DIM:(h + 1) * HEAD_DIM] = block
        else:
            ref[lead, pl.ds(r0, group * DN_CHUNK), h * HEAD_DIM:(h + 1) * HEAD_DIM] = block


def _dn_specs(T):
    nc = T // DN_CHUNK
    qkv = pl.BlockSpec((3, T, DN_HEADS * HEAD_DIM), lambda h: (0, 0, h))
    rows = pl.BlockSpec((DN_HEADS, nc, 1, DN_CHUNK), lambda h: (h, 0, 0, 0))
    hp = pl.BlockSpec((DN_HEADS, 8, 128), lambda h: (h, 0, 0))
    states = pl.BlockSpec((DN_HEADS, nc, HEAD_DIM, HEAD_DIM), lambda h: (h, 0, 0, 0))
    return nc, qkv, rows, hp, states


def _dn_fwd(qkv, b_rows, a_rows, hp, *, name):
    T = qkv.shape[1]
    nc, qkv_spec, row_spec, hp_spec, st_spec = _dn_specs(T)
    group = math.gcd(nc, DN_GROUP)
    H = DN_HEADS

    def body(qkv_ref, b_ref, a_ref, hp_ref, o_ref, st_ref, s_scr):
        s_scr[...] = jnp.zeros_like(s_scr)

        def step(t, carry):
            n0 = t * group
            c = _dn_chunk_setup(qkv_ref, b_ref, a_ref, hp_ref, n0, group)
            state = s_scr[...]
            outs = []
            for g in range(group):
                sl = slice(g * H, (g + 1) * H)
                for h in range(H):
                    st_ref[h, n0 + g] = state[h]
                v_new = c["u"][sl] - _bdot(c["w"][sl], state, _B_NN)
                outs.append(_bdot(c["qd"][sl], state, _B_NN) + _bdot(c["attn"][sl], v_new, _B_NN))
                state = state * c["el"][sl] + _bdot(c["kd"][sl], v_new, _B_TN)
            s_scr[...] = state
            _dn_store(o_ref, None, c["r0"], group, jnp.concatenate(outs, axis=0))
            return carry

        lax.fori_loop(0, nc // group, step, 0)

    return pl.pallas_call(
        body, name=name, grid=(HEADS // H,), in_specs=[qkv_spec, row_spec, row_spec, hp_spec],
        out_specs=(pl.BlockSpec((T, H * HEAD_DIM), lambda h: (0, h)), st_spec),
        out_shape=(jax.ShapeDtypeStruct((T, D_MODEL), F32),
                   jax.ShapeDtypeStruct((HEADS, nc, HEAD_DIM, HEAD_DIM), F32)),
        scratch_shapes=[pltpu.VMEM((H, HEAD_DIM, HEAD_DIM), F32)], compiler_params=_params(("parallel",)),
    )(qkv, b_rows, a_rows, hp)


def _dn_bwd(qkv, b_rows, a_rows, hp, states, do, *, name):
    T = qkv.shape[1]
    C = DN_CHUNK
    nc, qkv_spec, row_spec, hp_spec, st_spec = _dn_specs(T)
    group = math.gcd(nc, DN_GROUP)
    H = DN_HEADS
    B = group * H

    def body(qkv_ref, b_ref, a_ref, hp_ref, st_ref, do_ref, dqkv_ref, db_ref, da_ref, dhp_ref, ds_scr, acc_scr):
        ds_scr[...] = jnp.zeros_like(ds_scr)
        acc_scr[...] = jnp.zeros_like(acc_scr)

        def step(t, carry):
            n0 = nc - (t + 1) * group
            c = _dn_chunk_setup(qkv_ref, b_ref, a_ref, hp_ref, n0, group)
            state = jnp.stack([st_ref[h, pl.ds(n0, group)] for h in range(H)], axis=1).reshape(B, HEAD_DIM, HEAD_DIM)
            d_o = _dn_load(do_ref, None, c["r0"], group)
            v_new = c["u"] - _bdot(c["w"], state, _B_NN)
            d_vnew_local = _bdot(c["attn"], d_o, _B_TN)
            d_state_local = _bdot(c["qd"], d_o, _B_TN)
            d_state = ds_scr[...]
            d_vnew, d_kd, d_el = [None] * group, [None] * group, [None] * group
            for g in reversed(range(group)):
                sl = slice(g * H, (g + 1) * H)
                d_vnew[g] = d_vnew_local[sl] + _bdot(c["kd"][sl], d_state, _B_NN)
                d_kd[g] = _bdot(v_new[sl], d_state, _B_NT)
                d_el[g] = jnp.sum(jnp.sum(d_state * state[sl], axis=2, keepdims=True), axis=1, keepdims=True)
                d_state = d_state * c["el"][sl] + d_state_local[sl] - _bdot(c["w"][sl], d_vnew[g], _B_TN)
            ds_scr[...] = d_state
            chunk_grads(c, n0, state, d_o, v_new, jnp.concatenate(d_vnew, axis=0), jnp.concatenate(d_kd, axis=0),
                        jnp.concatenate(d_el, axis=0))
            return carry

        def chunk_grads(c, n0, state, d_o, v_new, d_vnew, d_kd, d_el):
            ii, jj = c["ii"], c["jj"]
            q, k, v, kb, beta = c["q"], c["k"], c["v"], c["kb"], c["beta"]
            decay, eg, egl, el = c["decay"], c["eg"], c["egl"], c["el"]
            u, w, tinv = c["u"], c["w"], c["tinv"]
            d_qd = _bdot(d_o, state, _B_NT)
            d_attn = _bdot(d_o, v_new, _B_NT)
            d_w = -_bdot(d_vnew, state, _B_NT)
            d_rv = _hdot(tinv, d_vnew, _B_TN)
            d_rw = _hdot(tinv, d_w, _B_TN)
            d_amat = -(_bdot(d_rv, u, _B_NT) + _bdot(d_rw, w, _B_NT))
            d_low = jnp.where(ii > jj, d_amat, 0.0)
            d_p = d_low * decay
            d_qk = d_attn * decay
            e_mat = (d_low * c["pmat"] + d_attn * c["qk"]) * decay
            d_q = _bdot(d_qk, k, _B_NN) + d_qd * eg
            d_kb = _bdot(d_p, k, _B_NN) + d_rw * eg
            d_k = _bdot(d_qk, q, _B_TN) + _bdot(d_p, kb, _B_TN) + d_kd * egl + d_kb * beta
            d_beta = jnp.sum(d_kb * k, axis=2, keepdims=True) + jnp.sum(d_rv * v, axis=2, keepdims=True)
            d_v = d_rv * beta
            d_eg = jnp.sum(d_qd * q, axis=2, keepdims=True) + jnp.sum(d_rw * kb, axis=2, keepdims=True)
            d_egl = jnp.sum(d_kd * k, axis=2, keepdims=True)
            d_glast = jnp.sum(d_egl * egl, axis=1, keepdims=True) + d_el * el
            row_sum = jnp.sum(e_mat, axis=2, keepdims=True)
            col_sum = c["to_col"](jnp.sum(e_mat, axis=1, keepdims=True))
            d_gc = row_sum - col_sum + d_eg * eg - d_egl * egl
            d_g_row = jnp.sum(jnp.where(ii >= jj, jnp.broadcast_to(d_gc, (B, C, C)), 0.0), axis=1, keepdims=True) + d_glast
            beta_row = c["beta_row"]
            d_b_row = c["to_row"](d_beta) * beta_row * (1.0 - beta_row)
            d_a_row = d_g_row * c["neg_ea"] * _sigmoid(c["a_row"] + c["dt_b"])
            _dn_store(dqkv_ref, 0, c["r0"], group, d_q * QK_SCALE)
            _dn_store(dqkv_ref, 1, c["r0"], group, d_k)
            _dn_store(dqkv_ref, 2, c["r0"], group, d_v)
            d_b_row = d_b_row.reshape(group, H, 1, C)
            d_a_row = d_a_row.reshape(group, H, 1, C)
            d_a_log = jnp.sum((d_g_row * c["g_row"]).reshape(group, H, 1, C), axis=0)
            d_dt_b = jnp.sum(d_a_row, axis=0)
            for h in range(H):
                db_ref[h, pl.ds(n0, group)] = d_b_row[:, h]
                da_ref[h, pl.ds(n0, group)] = d_a_row[:, h]
                acc_scr[h, 0:1, 0:C] += d_a_log[h]
                acc_scr[h, 1:2, 0:C] += d_dt_b[h]

        lax.fori_loop(0, nc // group, step, 0)
        for h in range(H):
            tot = jnp.sum(acc_scr[h], axis=1, keepdims=True)
            dhp_ref[h] = jnp.broadcast_to(tot, (8, 128))

    return pl.pallas_call(
        body, name=name, grid=(HEADS // H,),
        in_specs=[qkv_spec, row_spec, row_spec, hp_spec, st_spec, pl.BlockSpec((T, H * HEAD_DIM), lambda h: (0, h))],
        out_specs=(qkv_spec, row_spec, row_spec, hp_spec),
        out_shape=(jax.ShapeDtypeStruct((3, T, D_MODEL), F32), jax.ShapeDtypeStruct((HEADS, nc, 1, C), F32),
                   jax.ShapeDtypeStruct((HEADS, nc, 1, C), F32), jax.ShapeDtypeStruct((HEADS, 8, 128), F32)),
        scratch_shapes=[pltpu.VMEM((H, HEAD_DIM, HEAD_DIM), F32), pltpu.VMEM((H, 8, 128), F32)],
        compiler_params=_params(("parallel",)),
    )(qkv, b_rows, a_rows, hp, states, do)


COL_Z = 3 * HEADS


def _gated_norm_fwd(o, proj, gain, *, name):
    T = o.shape[0]

    def body(o_ref, z_ref, g_ref, out_ref):
        x = o_ref[...]
        r = lax.rsqrt(jnp.mean(x * x, axis=-1, keepdims=True) + RMS_EPS)
        z = z_ref[...]
        out_ref[...] = (x * r * g_ref[...] * (z * _sigmoid(z))).astype(out_ref.dtype)

    return pl.pallas_call(
        body, name=name, grid=(HEADS,),
        in_specs=[pl.BlockSpec((T, HEAD_DIM), lambda h: (0, h)), pl.BlockSpec((T, HEAD_DIM), lambda h: (0, COL_Z + h)),
                  pl.BlockSpec((1, HEAD_DIM), lambda h: (0, 0))],
        out_specs=pl.BlockSpec((T, HEAD_DIM), lambda h: (0, h)),
        out_shape=jax.ShapeDtypeStruct((T, D_MODEL), BF16), compiler_params=_params(("parallel",)),
    )(o, proj, gain)


def _gated_norm_bwd(dout, o, proj, gain, *, name):
    T = o.shape[0]

    def body(d_ref, o_ref, z_ref, g_ref, do_ref, dz_ref, dg_ref):
        x = o_ref[...]
        r = lax.rsqrt(jnp.mean(x * x, axis=-1, keepdims=True) + RMS_EPS)
        n = x * r
        z = z_ref[...]
        sg = _sigmoid(z)
        d = d_ref[...].astype(F32)
        g = g_ref[...]
        dz_ref[...] = (d * n * g * (sg * (1.0 + z * (1.0 - sg)))).astype(dz_ref.dtype)
        dy = d * (z * sg)
        dyg = dy * g
        do_ref[...] = r * (dyg - n * jnp.mean(dyg * n, axis=-1, keepdims=True))

        @pl.when(pl.program_id(0) == 0)
        def _():
            dg_ref[...] = jnp.zeros_like(dg_ref)

        dg_ref[...] += jnp.sum(dy * n, axis=0, keepdims=True)

    head = pl.BlockSpec((T, HEAD_DIM), lambda h: (0, h))
    vec = pl.BlockSpec((1, HEAD_DIM), lambda h: (0, 0))
    return pl.pallas_call(
        body, name=name, grid=(HEADS,),
        in_specs=[head, head, pl.BlockSpec((T, HEAD_DIM), lambda h: (0, COL_Z + h)), vec],
        out_specs=(head, head, vec),
        out_shape=(jax.ShapeDtypeStruct((T, D_MODEL), F32), jax.ShapeDtypeStruct((T, D_MODEL), BF16),
                   jax.ShapeDtypeStruct((1, HEAD_DIM), F32)),
        compiler_params=_params(("arbitrary",)),
    )(dout, o, proj, gain)


COL_SBQ = 4 * HEADS
COL_SBK = 5 * HEADS
COL_SBV = 6 * HEADS


def _split_dot(x, mat):
    hi = x.astype(BF16)
    lo = (x - hi.astype(F32)).astype(BF16)
    return jnp.dot(hi, mat, preferred_element_type=F32) + jnp.dot(lo, mat, preferred_element_type=F32)


def _sb_specs(T):
    return (pl.BlockSpec((T, HEAD_DIM), lambda h: (0, COL_SBQ + h)), pl.BlockSpec((T, HEAD_DIM), lambda h: (0, COL_SBK + h)),
            pl.BlockSpec((T, HEAD_DIM), lambda h: (0, COL_SBV + h)), pl.BlockSpec((1, HEAD_DIM), lambda h: (0, 0)))


def _head_rms(x, gain):
    r = lax.rsqrt(jnp.mean(x * x, axis=-1, keepdims=True) + RMS_EPS)
    return x * r, r


def _sb_fwd(proj, q_gain, k_gain, *, name):
    T = proj.shape[0]
    B = SB_BLOCK
    nb = T // B
    KT = min(SB_KEY_TILE, T)
    NS = KT // B
    q_spec, k_spec, v_spec, g_spec = _sb_specs(T)

    def body(q_ref, k_ref, v_ref, gq_ref, gk_ref, o_ref, lt_ref, qs, ks, vs):
        qs[...] = (_head_rms(q_ref[...], None)[0] * gq_ref[...]).astype(BF16)
        ks[...] = (_head_rms(k_ref[...], None)[0] * gk_ref[...]).astype(BF16)
        vs[...] = v_ref[...].astype(BF16)
        ii = lax.broadcasted_iota(jnp.int32, (B, B), 0)
        jj = lax.broadcasted_iota(jnp.int32, (B, B), 1)
        after = jnp.where(ii > jj, 1.0, 0.0).astype(BF16)
        row_t = lax.broadcasted_iota(jnp.int32, (B, KT), 0)
        col_t = lax.broadcasted_iota(jnp.int32, (B, KT), 1)

        def q_block(i, carry):
            q = qs[pl.ds(pl.multiple_of(i * B, B), B), :]

            def k_tile(step, inner):
                acc, tail = inner
                c0 = pl.multiple_of((i // NS - step) * KT, KT)
                z = lax.dot_general(q, ks[pl.ds(c0, KT), :], _NT, preferred_element_type=F32) * QK_SCALE
                causal = (c0 + col_t) < (i * B + row_t)
                sp = _softplus(z)
                log_1mb = jnp.where(causal, -sp, 0.0)
                parts = [None] * NS
                for b in reversed(range(NS)):
                    blk = log_1mb[:, b * B:(b + 1) * B]
                    parts[b] = _split_dot(blk, after) + tail
                    tail = tail + jnp.sum(blk, axis=1, keepdims=True)
                survive = parts[0] if NS == 1 else jnp.concatenate(parts, axis=1)
                wts = jnp.where(causal, jnp.exp(z - sp + survive), 0.0)
                acc = acc + jnp.dot(wts.astype(BF16), vs[pl.ds(c0, KT), :], preferred_element_type=F32)
                return acc, tail

            acc, tail = lax.fori_loop(0, i // NS + 1, k_tile, (jnp.zeros((B, HEAD_DIM), F32), jnp.zeros((B, 1), F32)))
            rows = pl.ds(pl.multiple_of(i * B, B), B)
            o_ref[rows, :] = acc.astype(o_ref.dtype)
            lt_ref[rows, :] = jnp.broadcast_to(tail, (B, HEAD_DIM))
            return carry

        lax.fori_loop(0, nb, q_block, 0)

    head = pl.BlockSpec((T, HEAD_DIM), lambda h: (0, h))
    return pl.pallas_call(
        body, name=name, grid=(HEADS,), in_specs=[q_spec, k_spec, v_spec, g_spec, g_spec],
        out_specs=(head, head),
        out_shape=(jax.ShapeDtypeStruct((T, D_MODEL), BF16), jax.ShapeDtypeStruct((T, D_MODEL), F32)),
        scratch_shapes=[pltpu.VMEM((T, HEAD_DIM), BF16)] * 3, compiler_params=_params(("parallel",)),
    )(proj, proj, proj, q_gain, k_gain)


def _sb_bwd(proj, q_gain, k_gain, ltot, do, *, name):
    T = proj.shape[0]
    B = SB_BLOCK
    nb = T // B
    KT = min(SB_KEY_TILE, T)
    NS = KT // B
    q_spec, k_spec, v_spec, g_spec = _sb_specs(T)

    def body(q_ref, k_ref, v_ref, gq_ref, gk_ref, lt_ref, do_ref, dq_ref, dk_ref, dv_ref, dgq_ref, dgk_ref,
             qs, ks, vs, dos, dq_acc, dk_acc, dv_acc):
        qn, q_r = _head_rms(q_ref[...], None)
        kn, k_r = _head_rms(k_ref[...], None)
        qs[...] = (qn * gq_ref[...]).astype(BF16)
        ks[...] = (kn * gk_ref[...]).astype(BF16)
        vs[...] = v_ref[...].astype(BF16)
        dos[...] = do_ref[...].astype(BF16)
        dk_acc[...] = jnp.zeros_like(dk_acc)
        dv_acc[...] = jnp.zeros_like(dv_acc)
        ii = lax.broadcasted_iota(jnp.int32, (B, B), 0)
        jj = lax.broadcasted_iota(jnp.int32, (B, B), 1)
        upto = jnp.where(ii <= jj, 1.0, 0.0).astype(BF16)
        before = jnp.where(ii < jj, 1.0, 0.0).astype(BF16)
        row_t = lax.broadcasted_iota(jnp.int32, (B, KT), 0)
        col_t = lax.broadcasted_iota(jnp.int32, (B, KT), 1)

        def q_block(i, carry):
            rows = pl.ds(pl.multiple_of(i * B, B), B)
            q = qs[rows, :]
            d_o = dos[rows, :]
            total = jnp.max(lt_ref[rows, :], axis=1, keepdims=True)

            def k_tile(t, inner):
                dq, head_lb, head_de = inner
                cols = pl.ds(pl.multiple_of(t * KT, KT), KT)
                k = ks[cols, :]
                v = vs[cols, :]
                z = lax.dot_general(q, k, _NT, preferred_element_type=F32) * QK_SCALE
                causal = (t * KT + col_t) < (i * B + row_t)
                sp = _softplus(z)
                log_1mb = jnp.where(causal, -sp, 0.0)
                parts = [None] * NS
                for b in range(NS):
                    blk = log_1mb[:, b * B:(b + 1) * B]
                    parts[b] = _split_dot(blk, upto) + head_lb
                    head_lb = head_lb + jnp.sum(blk, axis=1, keepdims=True)
                prefix = parts[0] if NS == 1 else jnp.concatenate(parts, axis=1)
                wts = jnp.where(causal, jnp.exp(z - sp + (total - prefix)), 0.0)
                d_w = lax.dot_general(d_o, v, _NT, preferred_element_type=F32)
                d_e = wts * d_w
                for b in range(NS):
                    blk = d_e[:, b * B:(b + 1) * B]
                    parts[b] = _split_dot(blk, before) + head_de
                    head_de = head_de + jnp.sum(blk, axis=1, keepdims=True)
                cum = parts[0] if NS == 1 else jnp.concatenate(parts, axis=1)
                sig = jnp.exp(z - sp)
                d_z = jnp.where(causal, d_e * (1.0 - sig) - sig * cum, 0.0) * QK_SCALE
                d_zb = d_z.astype(BF16)
                dq = dq + jnp.dot(d_zb, k, preferred_element_type=F32)
                dk_acc[cols, :] += lax.dot_general(d_zb, q, _TN, preferred_element_type=F32)
                dv_acc[cols, :] += lax.dot_general(wts.astype(BF16), d_o, _TN, preferred_element_type=F32)
                return dq, head_lb, head_de

            zero = jnp.zeros((B, 1), F32)
            dq, _, _ = lax.fori_loop(0, i // NS + 1, k_tile, (jnp.zeros((B, HEAD_DIM), F32), zero, zero))
            dq_acc[rows, :] = dq
            return carry

        lax.fori_loop(0, nb, q_block, 0)

        def norm_bwd(d_scaled, n, r, gain):
            dn = d_scaled * gain
            return r * (dn - n * jnp.mean(dn * n, axis=-1, keepdims=True)), jnp.sum(d_scaled * n, axis=0, keepdims=True)

        dq_raw, dgq = norm_bwd(dq_acc[...], qn, q_r, gq_ref[...])
        dk_raw, dgk = norm_bwd(dk_acc[...], kn, k_r, gk_ref[...])
        dq_ref[...] = dq_raw.astype(dq_ref.dtype)
        dk_ref[...] = dk_raw.astype(dk_ref.dtype)
        dv_ref[...] = dv_acc[...].astype(dv_ref.dtype)

        @pl.when(pl.program_id(0) == 0)
        def _():
            dgq_ref[...] = jnp.zeros_like(dgq_ref)
            dgk_ref[...] = jnp.zeros_like(dgk_ref)

        dgq_ref[...] += dgq
        dgk_ref[...] += dgk

    head = pl.BlockSpec((T, HEAD_DIM), lambda h: (0, h))
    out = jax.ShapeDtypeStruct((T, D_MODEL), BF16)
    vec = jax.ShapeDtypeStruct((1, HEAD_DIM), F32)
    return pl.pallas_call(
        body, name=name, grid=(HEADS,), in_specs=[q_spec, k_spec, v_spec, g_spec, g_spec, head, head],
        out_specs=(head, head, head, g_spec, g_spec), out_shape=(out, out, out, vec, vec),
        scratch_shapes=[pltpu.VMEM((T, HEAD_DIM), BF16)] * 4 + [pltpu.VMEM((T, HEAD_DIM), F32)] * 3,
        compiler_params=_params(("arbitrary",)),
    )(proj, proj, proj, q_gain, k_gain, ltot, do)


ADAM_ROWS = 128


def _adamw(g_parts, w, m, v, *, name, layer=0, earlier=None):
    K, A, C = g_parts.shape
    R = w.shape[0]
    tr = ADAM_ROWS if A % ADAM_ROWS == 0 else (A // 2 if A % 32 == 0 else A)
    first_block = layer * (A // tr)

    def body(g_ref, w_ref, m_ref, v_ref, *rest):
        go_ref, d_ref, mo_ref, vo_ref = rest[-4:]
        g = g_ref[0].astype(F32)
        for k in range(1, K):
            g = g + g_ref[k].astype(F32)
        go_ref[...] = g
        m_new = ADAM_B1 * m_ref[...] + (1.0 - ADAM_B1) * g
        v_new = ADAM_B2 * v_ref[...] + (1.0 - ADAM_B2) * (g * g)
        m_hat = m_new / (1.0 - ADAM_B1 ** ADAM_STEP)
        v_hat = v_new / (1.0 - ADAM_B2 ** ADAM_STEP)
        d_ref[...] = -ADAM_LR * (m_hat / (jnp.sqrt(v_hat) + ADAM_EPS) + ADAM_WD * w_ref[...])
        mo_ref[...] = m_new
        vo_ref[...] = v_new

    row = pl.BlockSpec((tr, C), lambda i: (first_block + i, 0))
    out = jax.ShapeDtypeStruct((R, C), F32)
    in_specs = [pl.BlockSpec((K, tr, C), lambda i: (0, i, 0)), row, row, row]
    if earlier is None:
        return pl.pallas_call(
            body, name=name, grid=(A // tr,), in_specs=in_specs, out_specs=(row, row, row, row),
            out_shape=(out, out, out, out), compiler_params=_params(("parallel",)),
        )(g_parts, w, m, v)
    return pl.pallas_call(
        body, name=name, grid=(A // tr,), in_specs=in_specs + [ANY] * 4, out_specs=(row, row, row, row),
        out_shape=(out, out, out, out), input_output_aliases={4 + j: j for j in range(4)},
        compiler_params=_params(("parallel",)),
    )(g_parts, w, m, v, *earlier)


def _sum_parts(parts, *, name):
    K, R, C = parts.shape

    def body(p_ref, o_ref):
        acc = p_ref[0]
        for k in range(1, K):
            acc = acc + p_ref[k]
        o_ref[...] = acc

    return pl.pallas_call(body, name=name, out_shape=jax.ShapeDtypeStruct((R, C), F32))(parts)


def _position():
    return lax.axis_index("x"), lax.axis_index("y"), lax.axis_index("c")


def _all_gather(shards, *, name):
    n = len(shards)

    def body(*refs):
        x_refs, out_refs = refs[:n], refs[n:2 * n]
        send_sems, recv_sems, local_sems = refs[2 * n:]
        x, y, c = _position()
        me, sibling = (x, y, c), (x, y, 1 - c)
        chips = [(1 - x, y), (x, 1 - y), (1 - x, 1 - y)]

        def slot(a, px, py, pc):
            return out_refs[a].at[4 * px + 2 * py + pc]

        def copy(a, k, block, to, own=False):
            return pltpu.make_async_remote_copy(
                src_ref=x_refs[a] if own else slot(a, *block), dst_ref=slot(a, *block),
                send_sem=send_sems.at[a, k], recv_sem=recv_sems.at[a, k], device_id=to, device_id_type=MESH)

        mine = [pltpu.make_async_copy(x_refs[a], slot(a, *me), local_sems.at[a]) for a in range(n)]
        for cp in mine:
            cp.start()
        first = [copy(a, 1 + j, me, (*chip, c), own=True) for j, chip in enumerate(chips) for a in range(n)]
        first += [copy(a, 0, me, sibling, own=True) for a in range(n)]
        for cp in first:
            cp.start()
        passed = []
        for j, chip in enumerate(chips):
            for a in range(n):
                copy(a, 1 + j, (*chip, c), me).wait_recv()
                passed.append(copy(a, 4 + j, (*chip, c), sibling))
                passed[-1].start()
        for a in range(n):
            copy(a, 0, sibling, me).wait_recv()
        for j, chip in enumerate(chips):
            for a in range(n):
                copy(a, 4 + j, (*chip, 1 - c), me).wait_recv()
        for cp in first + passed:
            cp.wait_send()
        for cp in mine:
            cp.wait()

    return pl.pallas_call(
        body, name=name, in_specs=[ANY] * n, out_specs=[ANY] * n,
        out_shape=[jax.ShapeDtypeStruct((N_DEV,) + s.shape, s.dtype) for s in shards],
        scratch_shapes=[pltpu.SemaphoreType.DMA((n, 7)), pltpu.SemaphoreType.DMA((n, 7)), pltpu.SemaphoreType.DMA((n,))],
    )(*shards)


HBM = pl.BlockSpec(memory_space=pltpu.HBM)
SEM = pl.BlockSpec(memory_space=pltpu.SEMAPHORE)
DATAFLOW = pltpu.SideEffectType.DATAFLOW_SIDE_EFFECTING


def _exchange_copies(gather, x_refs, land_refs, send_sems, recv_sems, local_sems):
    n = len(x_refs)
    x, y, c = _position()
    me = 4 * x + 2 * y + c

    def src(a, slot):
        return x_refs[a] if gather else x_refs[a].at[slot]

    mine = [pltpu.make_async_copy(src(a, me), land_refs[a].at[me], local_sems.at[a]) for a in range(n)]
    sends, recvs = [], []
    for k in range(1, N_DEV):
        px, py, pc = (x + (k >> 2)) % 2, (y + ((k >> 1) & 1)) % 2, (c + (k & 1)) % 2
        peer = 4 * px + 2 * py + pc
        for a in range(n):
            sems = dict(send_sem=send_sems.at[7 * a + k - 1], recv_sem=recv_sems.at[7 * a + k - 1],
                        device_id=(px, py, pc), device_id_type=MESH)
            sends.append(pltpu.make_async_remote_copy(src_ref=src(a, peer), dst_ref=land_refs[a].at[me], **sems))
            recvs.append(pltpu.make_async_remote_copy(src_ref=src(a, me), dst_ref=land_refs[a].at[peer], **sems))
    return mine, sends, recvs


def _exchange_start(parts, *, gather, name):
    n = len(parts)

    def body(*refs):
        x_refs, land_refs = refs[:n], refs[n:2 * n]
        send_sems, recv_sems, local_sems = refs[2 * n:2 * n + 3]
        token = refs[-1]
        mine, sends, _ = _exchange_copies(gather, x_refs, land_refs, send_sems, recv_sems, local_sems)
        for cp in mine + sends:
            cp.start()
        token[...] = jnp.zeros_like(token)

    sems = (pltpu.SemaphoreType.DMA((7 * n,)), pltpu.SemaphoreType.DMA((7 * n,)), pltpu.SemaphoreType.DMA((n,)))
    thru = tuple(pltpu.HBM(p.shape, p.dtype) for p in parts)
    land = tuple(pltpu.HBM(((N_DEV,) if gather else ()) + p.shape, p.dtype) for p in parts)
    res = pl.pallas_call(
        body, name=name, in_specs=[HBM] * (2 * n),
        out_specs=(SEM, SEM, SEM) + (HBM,) * (2 * n) + (pl.BlockSpec(memory_space=pltpu.VMEM),),
        out_shape=sems + thru + land + (jax.ShapeDtypeStruct((8, 128), F32),),
        input_output_aliases={a: 3 + a for a in range(2 * n)},
        compiler_params=pltpu.CompilerParams(has_side_effects=DATAFLOW),
    )(*[pltpu.with_memory_space_constraint(p, pltpu.HBM) for p in parts],
      *[pltpu.with_memory_space_constraint(lax.empty(z.shape, z.dtype), pltpu.HBM) for z in land])
    return res[:3], res[3:3 + n], res[3 + n:3 + 2 * n], res[-1]


def _exchange_wait(sems, parts, landing, after, *, gather, name):
    n = len(parts)

    def body(*refs):
        x_refs, land_refs = refs[:n], refs[n:2 * n]
        send_sems, recv_sems, local_sems = refs[2 * n:2 * n + 3]
        token = refs[-1]
        mine, sends, recvs = _exchange_copies(gather, x_refs, land_refs, send_sems, recv_sems, local_sems)
        for cp in recvs:
            cp.wait_recv()
        for cp in sends:
            cp.wait_send()
        for cp in mine:
            cp.wait()
        token[...] = jnp.zeros_like(token)

    thru = tuple(pltpu.HBM(p.shape, p.dtype) for p in tuple(parts) + tuple(landing))
    res = pl.pallas_call(
        body, name=name, in_specs=[HBM] * (2 * n) + [SEM, SEM, SEM, ANY],
        out_specs=(HBM,) * (2 * n) + (pl.BlockSpec(memory_space=pltpu.VMEM),),
        out_shape=thru + (jax.ShapeDtypeStruct((8, 128), F32),), input_output_aliases={a: a for a in range(2 * n)},
        compiler_params=pltpu.CompilerParams(has_side_effects=DATAFLOW),
    )(*parts, *landing, *sems, after)
    return res[n:2 * n], res[-1]


def _ffn_fwd(x, gain, wg_in, wg_out, tag):
    T, D = x.shape
    fb, rb = wg_in.shape[-1], wg_out.shape[-2]
    tm, tn = min(T, 1024), 512
    h = _rmsnorm_fwd(x, gain, name=f"{tag}_norm")
    p = _mm(name=f"{tag}_in", grid=(T // tm, N_DEV, 1), tile=(tm, fb),
            a=h, a_spec=pl.BlockSpec((tm, D), lambda i, j, k: (i, 0)),
            b=wg_in, b_spec=pl.BlockSpec((None, D, fb), lambda i, j, k: (j, 0, 0)),
            out_shape=jax.ShapeDtypeStruct((N_DEV, T, fb), BF16), o_spec=pl.BlockSpec((None, tm, fb), lambda i, j, k: (j, i, 0)))
    a = _swiglu_fwd(p, name=f"{tag}_act")
    y = _mm(name=f"{tag}_out", grid=(T // tm, D // tn, FF_HALF), tile=(tm, tn), resid=x, scale=0.5,
            a=a, a_spec=pl.BlockSpec((None, tm, fb), lambda i, j, k: (k, i, 0)),
            b=wg_out.reshape(N_DEV * rb, D), b_spec=pl.BlockSpec((fb, tn), lambda i, j, k: (k, j)),
            out_shape=jax.ShapeDtypeStruct((T, D), F32), o_spec=pl.BlockSpec((tm, tn), lambda i, j, k: (i, j)))
    return y, (x, h, p, a)


def _ffn_bwd(dy, saved, gain, wg_in, wg_out, tag, on_weight_grads=None):
    x, h, p, a = saved
    T, D = x.shape
    fb, rb = wg_in.shape[-1], wg_out.shape[-2]
    tm, tn = min(T, 1024), 512
    da = _mm(name=f"{tag}_out_dx", grid=(T // tm, FF_HALF, 1), tile=(tm, fb), tb=True, scale=0.5,
             a=dy, a_spec=pl.BlockSpec((tm, D), lambda i, j, k: (i, 0)),
             b=wg_out.reshape(N_DEV * rb, D), b_spec=pl.BlockSpec((fb, D), lambda i, j, k: (j, 0)),
             out_shape=jax.ShapeDtypeStruct((FF_HALF, T, fb), BF16), o_spec=pl.BlockSpec((None, tm, fb), lambda i, j, k: (j, i, 0)))
    d_w_out = _mm(name=f"{tag}_out_dw", grid=(FF_HALF, D // tn, 1), tile=(fb, tn), ta=True, scale=0.5,
                  a=a, a_spec=pl.BlockSpec((None, T, fb), lambda i, j, k: (i, 0, 0)),
                  b=dy, b_spec=pl.BlockSpec((T, tn), lambda i, j, k: (0, j)),
                  out_shape=jax.ShapeDtypeStruct((FF_HALF, fb, D), BF16), o_spec=pl.BlockSpec((None, fb, tn), lambda i, j, k: (i, 0, j)))
    dp = _swiglu_bwd(da, p, name=f"{tag}_act_bwd")
    d_w_in = _mm(name=f"{tag}_in_dw", grid=(1, N_DEV, 1), tile=(D, fb), ta=True,
                 a=h, a_spec=pl.BlockSpec((T, D), lambda i, j, k: (0, 0)),
                 b=dp, b_spec=pl.BlockSpec((None, T, fb), lambda i, j, k: (j, 0, 0)),
                 out_shape=jax.ShapeDtypeStruct((N_DEV, D, fb), BF16), o_spec=pl.BlockSpec((None, D, fb), lambda i, j, k: (j, 0, 0)))
    dh = _mm(name=f"{tag}_in_dx", grid=(T // tm, D // tn, N_DEV), tile=(tm, tn), tb=True,
             a=dp, a_spec=pl.BlockSpec((None, tm, fb), lambda i, j, k: (k, i, 0)),
             b=wg_in, b_spec=pl.BlockSpec((None, tn, fb), lambda i, j, k: (k, j, 0)),
             out_shape=jax.ShapeDtypeStruct((T, D), F32), o_spec=pl.BlockSpec((tm, tn), lambda i, j, k: (i, j)))
    d_w_out = d_w_out.reshape(N_DEV, rb, D)
    if on_weight_grads is not None:
        gain = gain + on_weight_grads(d_w_in, d_w_out)[0, 0]
    dx, d_gain = _rmsnorm_bwd(dh, x, gain, dy, name=f"{tag}_norm_bwd")
    return dx, d_gain, d_w_in, d_w_out


def _square_mm(a, wg, *, name, transposed=False, out_dtype=F32, resid=None):
    T, D = a.shape
    w = wg.reshape(D, D)
    return _matmul(a, w, tb=transposed, name=name, out_dtype=out_dtype, resid=resid)


def _head_rows(cols, T):
    return cols.T.reshape(HEADS, T // DN_CHUNK, 1, DN_CHUNK)


def _mixer_fwd(x, w, big, tag):
    T = x.shape[0]
    h = _rmsnorm_fwd(x, w["mix_norm"], name=f"{tag}_norm")
    proj = _matmul(h, big["w_main"], name=f"{tag}_proj")
    scal = _matmul(h, big["w_scal"], name=f"{tag}_proj_scal", tn=N_SCAL)
    qkv = _conv_fwd(proj, big["conv_w"], name=f"{tag}_conv")
    b_rows = _head_rows(scal[:, 0:HEADS], T)
    a_rows = _head_rows(scal[:, HEADS:2 * HEADS], T)
    o_a, states = _dn_fwd(qkv, b_rows, a_rows, w["hp"], name=f"{tag}_dn")
    oa_n = _gated_norm_fwd(o_a, proj, w["dn_out_norm"], name=f"{tag}_dn_norm")
    ya = _square_mm(oa_n, big["w_branch_a"], name=f"{tag}_branch_a")
    o_b, ltot = _sb_fwd(proj, w["sb_q_norm"], w["sb_k_norm"], name=f"{tag}_sb")
    yb = _square_mm(o_b, big["w_branch_b"], name=f"{tag}_branch_b")
    merged = _merge_fwd(ya, yb, proj, name=f"{tag}_merge")
    y = _square_mm(merged, big["w_out"], name=f"{tag}_out", resid=x)
    return y, (x, h, proj, qkv, b_rows, a_rows, o_a, states, oa_n, ya, o_b, ltot, yb, merged)


def _mixer_bwd(dy, saved, w, big, tag):
    x, h, proj, qkv, b_rows, a_rows, o_a, states, oa_n, ya, o_b, ltot, yb, merged = saved
    T = x.shape[0]
    g = {}
    d_merged = _square_mm(dy, big["w_out"], transposed=True, name=f"{tag}_out_dx", out_dtype=BF16)
    g["w_out"] = _matmul(merged, dy, ta=True, name=f"{tag}_out_dw", out_dtype=BF16)
    d_ya, d_yb, d_ga, d_gb = _merge_bwd(d_merged, ya, yb, proj, name=f"{tag}_merge_bwd")
    d_oan = _square_mm(d_ya, big["w_branch_a"], transposed=True, name=f"{tag}_branch_a_dx")
    g["w_branch_a"] = _matmul(oa_n, d_ya, ta=True, name=f"{tag}_branch_a_dw", out_dtype=BF16)
    d_ob = _square_mm(d_yb, big["w_branch_b"], transposed=True, name=f"{tag}_branch_b_dx")
    g["w_branch_b"] = _matmul(o_b, d_yb, ta=True, name=f"{tag}_branch_b_dw", out_dtype=BF16)
    d_oa, d_z, g["dn_out_norm"] = _gated_norm_bwd(d_oan, o_a, proj, w["dn_out_norm"], name=f"{tag}_dn_norm_bwd")
    d_qkv, d_b_rows, d_a_rows, d_hp = _dn_bwd(qkv, b_rows, a_rows, w["hp"], states, d_oa, name=f"{tag}_dn_bwd")
    g["dn_a_log"] = d_hp[:, 0, 0]
    g["dn_dt_bias"] = d_hp[:, 1, 0]
    d_conv_in, g["conv_w"] = _conv_bwd(d_qkv, proj, big["conv_w"], name=f"{tag}_conv_bwd")
    d_sbq, d_sbk, d_sbv, g["sb_q_norm"], g["sb_k_norm"] = _sb_bwd(
        proj, w["sb_q_norm"], w["sb_k_norm"], ltot, d_ob, name=f"{tag}_sb_bwd")
    d_proj = jnp.concatenate([d_conv_in, d_z, d_sbq, d_sbk, d_sbv, d_ga, d_gb], axis=1)
    d_scal = jnp.concatenate([d_b_rows.reshape(HEADS, T).T, d_a_rows.reshape(HEADS, T).T,
                              jnp.zeros((T, N_SCAL - 2 * HEADS), F32)], axis=1).astype(BF16)
    g["w_main"] = _matmul(h, d_proj, ta=True, name=f"{tag}_proj_dw", out_dtype=BF16)
    g["w_scal"] = _matmul(h, d_scal, ta=True, name=f"{tag}_proj_scal_dw", out_dtype=BF16, tn=N_SCAL)
    dh_scal = _matmul(d_scal, big["w_scal"], tb=True, name=f"{tag}_proj_scal_dx")
    dh = _matmul(d_proj, big["w_main"], tb=True, name=f"{tag}_proj_dx", tk=N_MAIN // 4, resid=dh_scal)
    dx, g["mix_norm"] = _rmsnorm_bwd(dh, x, w["mix_norm"], dy, name=f"{tag}_norm_bwd")
    return dx, g


def _local_step(x, target, layers, weights_of, on_layer_grads):
    saved, bigs = [], []
    for l, w in enumerate(layers):
        big = weights_of(l, x)
        x, s1 = _ffn_fwd(x, w["ffn1_norm"] + big["started"], big["ffn1_w_in"], big["ffn1_w_out"], f"l{l}_ffn1")
        x, s2 = _mixer_fwd(x, w, big, f"l{l}_mix")
        x, s3 = _ffn_fwd(x, w["ffn2_norm"], big["ffn2_w_in"], big["ffn2_w_out"], f"l{l}_ffn2")
        saved.append((s1, s2, s3))
        bigs.append(big)
    loss, dx = _loss_head(x, target, name="loss_head")
    for l in reversed(range(len(layers))):
        w, big = layers[l], bigs[l]
        s1, s2, s3 = saved[l]
        dx, g_n2, g_in2, g_out2 = _ffn_bwd(dx, s3, w["ffn2_norm"], big["ffn2_w_in"], big["ffn2_w_out"], f"l{l}_ffn2")
        dx, g = _mixer_bwd(dx, s2, w, big, f"l{l}_mix")
        g.update(ffn2_norm=g_n2, ffn2_w_in=g_in2, ffn2_w_out=g_out2)

        def last_grads(g_in1, g_out1, l=l, g=g):
            g.update(ffn1_w_in=g_in1, ffn1_w_out=g_out1)
            return on_layer_grads(l, g)

        dx, g["ffn1_norm"], _, _ = _ffn_bwd(dx, s1, w["ffn1_norm"], big["ffn1_w_in"], big["ffn1_w_out"], f"l{l}_ffn1",
                                            on_weight_grads=last_grads)
    return loss, dx


_BIG = ("ffn1_w_in", "ffn1_w_out", "w_in", "w_branch_a", "w_branch_b", "w_out", "ffn2_w_in", "ffn2_w_out")
_SMALL = ("ffn1_norm", "mix_norm", "ffn2_norm", "dn_a_log", "dn_dt_bias", "dn_out_norm", "sb_q_norm", "sb_k_norm")
_ORDER = ("ffn1_norm", "ffn1_w_in", "ffn1_w_out", "mix_norm", "w_in", "dn_conv_w", "dn_a_log", "dn_dt_bias", "dn_out_norm",
          "sb_q_norm", "sb_k_norm", "w_branch_a", "w_branch_b", "w_out", "ffn2_norm", "ffn2_w_in", "ffn2_w_out")
COL_SCAL = 4 * D_MODEL


def _pad_rows(a, multiple):
    pad = (-a.shape[-2]) % multiple
    return a if pad == 0 else jnp.pad(a, [(0, 0)] * (a.ndim - 2) + [(0, pad), (0, 0)])


def _lane_rows(a):
    flat = a.reshape(-1)
    flat = jnp.pad(flat, (0, (-flat.shape[0]) % 128))
    return flat.reshape(-1, 128)


def _pack_small(named):
    pieces, spans, r = [], {}, 0
    for n, a in named:
        rows = _lane_rows(a)
        spans[n] = (r, r + rows.shape[0], a.shape)
        r += rows.shape[0]
        pieces.append(rows)
    return _pad_rows(jnp.concatenate(pieces, axis=0), 8), spans


def _unpack_small(packed, spans, n):
    r0, r1, shape = spans[n]
    return packed[r0:r1].reshape(-1)[:math.prod(shape)].reshape(shape)


def kernel(x, ffn1_norm, ffn1_w_in, ffn1_w_out, mix_norm, w_in, dn_conv_w, dn_a_log, dn_dt_bias, dn_out_norm, sb_q_norm, sb_k_norm, w_branch_a, w_branch_b, w_out, ffn2_norm, ffn2_w_in, ffn2_w_out, loss_target, m_ffn1_norm, m_ffn1_w_in, m_ffn1_w_out, m_mix_norm, m_w_in, m_dn_conv_w, m_dn_a_log, m_dn_dt_bias, m_dn_out_norm, m_sb_q_norm, m_sb_k_norm, m_w_branch_a, m_w_branch_b, m_w_out, m_ffn2_norm, m_ffn2_w_in, m_ffn2_w_out, v_ffn1_norm, v_ffn1_w_in, v_ffn1_w_out, v_mix_norm, v_w_in, v_dn_conv_w, v_dn_a_log, v_dn_dt_bias, v_dn_out_norm, v_sb_q_norm, v_sb_k_norm, v_w_branch_a, v_w_branch_b, v_w_out, v_ffn2_norm, v_ffn2_w_in, v_ffn2_w_out):
    given = dict(locals())
    weights = {n: given[n] for n in _ORDER}
    mom_m = {n: given["m_" + n] for n in _ORDER}
    mom_v = {n: given["v_" + n] for n in _ORDER}
    L = ffn1_norm.shape[0]
    ax, ay, ac = _position()
    my_slot = 4 * ax + 2 * ay + ac

    conv_cols = dn_conv_w.shape[-1]
    shards = lambda l, zero: [(weights[n][l] + zero).astype(BF16) for n in _BIG]
    *first_layer, conv_full = _all_gather(shards(0, 0.0) + [_pad_rows(_lane_rows(dn_conv_w), 8)], name="gather_l0")
    conv_full = conv_full.reshape(N_DEV, -1)[:, :L * DN_CONV * conv_cols].reshape(N_DEV, L, DN_CONV, conv_cols)
    conv_full = conv_full.transpose(1, 2, 0, 3).reshape(L, DN_CONV, N_DEV * conv_cols)
    arriving = {}

    def start_gather(l, zero):
        if l >= L:
            return 0.0
        *arriving[l], token = _exchange_start(shards(l, zero), gather=True, name=f"gather_start_l{l}")
        return token[0, 0]

    def weights_of(l, x_in):
        if l == 0:
            arrays, started = first_layer, start_gather(1, 0.0)
        else:
            arrays, token = _exchange_wait(*arriving[l], x_in, gather=True, name=f"gather_wait_l{l}")
            started = start_gather(l + 1, token[0, 0])
        big = dict(zip(_BIG, arrays), started=started)
        wi = big.pop("w_in").transpose(1, 0, 2).reshape(D_MODEL, N_IN)
        big["w_main"] = jnp.concatenate([wi[:, :COL_SCAL], wi[:, COL_SCAL + 2 * HEADS:]], axis=1)
        big["w_scal"] = jnp.pad(wi[:, COL_SCAL:COL_SCAL + 2 * HEADS], ((0, 0), (0, N_SCAL - 2 * HEADS)))
        big["conv_w"] = conv_full[l]
        return big

    layers = []
    for l in range(L):
        hp = jnp.concatenate([jnp.broadcast_to(dn_a_log[l][:, None, None], (HEADS, 1, 128)),
                              jnp.broadcast_to(dn_dt_bias[l][:, None, None], (HEADS, 1, 128)),
                              jnp.zeros((HEADS, 6, 128), F32)], axis=1)
        layers.append(dict(ffn1_norm=ffn1_norm[l][None], mix_norm=mix_norm[l][None], hp=hp,
                           dn_out_norm=dn_out_norm[l][None], sb_q_norm=sb_q_norm[l][None],
                           sb_k_norm=sb_k_norm[l][None], ffn2_norm=ffn2_norm[l][None]))

    grads = [None] * L
    in_flight = [None] * L

    def on_layer_grads(l, g):
        grads[l] = g
        g_w_in = jnp.concatenate([g["w_main"][:, :COL_SCAL], g["w_scal"][:, :2 * HEADS], g["w_main"][:, COL_SCAL:]], axis=1)
        parts = dict(g, w_in=g_w_in.reshape(D_MODEL, N_DEV, N_IN // N_DEV).transpose(1, 0, 2))
        for n in ("w_branch_a", "w_branch_b", "w_out"):
            parts[n] = g[n].reshape(N_DEV, D_MODEL // N_DEV, D_MODEL)
        *in_flight[l], token = _exchange_start([parts[n] for n in _BIG], gather=False, name=f"scatter_start_l{l}")
        return token

    loss_row, dx = _local_step(x[0], loss_target[0], layers, weights_of, on_layer_grads)
    loss = lax.psum(loss_row[0, 0], ("x", "y", "c"))

    results = {n: None for n in _BIG}
    after = dx
    for l in reversed(range(L)):
        landed, _ = _exchange_wait(*in_flight[l], after, gather=False, name=f"scatter_wait_l{l}")
        for n, parts in zip(_BIG, landed):
            _, a, b = weights[n].shape
            results[n] = _adamw(parts, weights[n].reshape(L * a, b), mom_m[n].reshape(L * a, b),
                                mom_v[n].reshape(L * a, b), layer=l, earlier=results[n], name=f"adamw_{n}_l{l}")
        after = results[_BIG[-1]][0]
    out = {n: tuple(t.reshape(weights[n].shape) for t in results[n]) for n in _BIG}

    small_grads = [(n, jnp.stack([g[n].reshape(weights[n].shape[1:]) for g in grads])) for n in _SMALL]
    small_packed, spans = _pack_small(small_grads + [("conv", jnp.stack([g["conv_w"] for g in grads]))])
    small_sum = _sum_parts(_all_gather([small_packed], name="gather_small_grads")[0], name="sum_small_grads")
    rep_rows = spans["conv"][0]
    pack_rep = lambda d: _pad_rows(_pack_small([(n, d[n]) for n in _SMALL])[0], 8)
    rep_pad = (-rep_rows) % 8
    g_rep = jnp.pad(small_sum[:rep_rows], ((0, rep_pad), (0, 0)))
    res = _adamw(g_rep[None], pack_rep(weights), pack_rep(mom_m), pack_rep(mom_v), name="adamw_replicated")
    for n in _SMALL:
        out[n] = tuple(_unpack_small(t, spans, n) for t in res)
    conv_sum = _unpack_small(small_sum, spans, "conv")
    conv_mine = lax.dynamic_slice_in_dim(conv_sum, my_slot * conv_cols, conv_cols, axis=2).reshape(L * DN_CONV, conv_cols)
    flat = lambda t: t.reshape(L * DN_CONV, conv_cols)
    res = _adamw(conv_mine[None], flat(dn_conv_w), flat(m_dn_conv_w), flat(v_dn_conv_w), name="adamw_conv")
    out["dn_conv_w"] = tuple(t.reshape(L, DN_CONV, conv_cols) for t in res)

    return (loss, dx[None], *[out[n][0] for n in _ORDER], *[out[n][1] for n in _ORDER],
            *[out[n][2] for n in _ORDER], *[out[n][3] for n in _ORDER])
```

```python
import functools
import math

import jax
import jax.numpy as jnp
from jax import lax
from jax.experimental import pallas as pl
from jax.experimental.pallas import tpu as pltpu

F32 = jnp.float32
BF16 = jnp.bfloat16

N_DEV = 8
D_MODEL = 1024
DEPTH = 4
D_FF = 2816
HEADS = 8
HEAD_DIM = 128
DN_CHUNK = 64
DN_CONV = 4
DN_GROUP = 4
DN_HEADS = 2
SB_BLOCK = 128
SB_KEY_TILE = 512
SB_HEADS = 2
RMS_EPS = 1e-6
L2_EPS = 1e-6
N_IN = 9232
N_MAIN = 9216
N_SCAL = 128
QK_SCALE = HEAD_DIM ** -0.5

ADAM_LR = 0.001
ADAM_B1 = 0.9
ADAM_B2 = 0.999
ADAM_EPS = 1e-08
ADAM_WD = 0.01
ADAM_STEP = 10

V7X_VMEM_LIMIT = 56 * 1024 * 1024
MESH = pl.DeviceIdType.MESH
ANY = pl.BlockSpec(memory_space=pl.ANY)


def _params(sem=None, vmem=V7X_VMEM_LIMIT):
    return pltpu.CompilerParams(dimension_semantics=sem, vmem_limit_bytes=vmem)


def _sigmoid(x):
    return 1.0 / (1.0 + jnp.exp(-x))


def _softplus(x):
    return jnp.maximum(x, 0.0) + jnp.log(1.0 + jnp.exp(-jnp.abs(x)))


def _bdot(a, b, dims=(((1,), (0,)), ((), ()))):
    return lax.dot_general(a.astype(BF16), b.astype(BF16), dims, preferred_element_type=F32)


_NT = (((1,), (1,)), ((), ()))
_TN = (((0,), (0,)), ((), ()))


def _hdot(a, b, dims=(((1,), (0,)), ((), ()))):
    a_hi = a.astype(BF16)
    b_hi = b.astype(BF16)
    a_lo = (a - a_hi.astype(F32)).astype(BF16)
    b_lo = (b - b_hi.astype(F32)).astype(BF16)
    dot = functools.partial(lax.dot_general, dimension_numbers=dims, preferred_element_type=F32)
    return dot(a_hi, b_hi) + (dot(a_hi, b_lo) + dot(a_lo, b_hi))


def _hdot_tn(a, b):
    return _hdot(a, b, _TN)


def _mm(*, name, grid, a, a_spec, b, b_spec, out_shape, o_spec, tile, ta=False, tb=False, resid=None, scale=1.0):
    nk = grid[2]
    dims = (((0 if ta else 1,), (1 if tb else 0,)), ((), ()))

    def flat(v):
        return v if v.ndim == 2 else v.reshape(-1, v.shape[-1])

    def body(*refs):
        a_ref, b_ref = refs[:2]
        r_ref = refs[2] if resid is not None else None
        o_ref = refs[3] if resid is not None else refs[2]
        part = lax.dot_general(flat(a_ref[...]).astype(BF16), flat(b_ref[...]).astype(BF16), dims,
                               preferred_element_type=F32)

        def finish(acc):
            if scale != 1.0:
                acc = acc * scale
            if r_ref is not None:
                acc = r_ref[...] + acc
            o_ref[...] = acc.astype(o_ref.dtype)

        if nk == 1:
            finish(part)
        else:
            acc_ref = refs[-1]
            k = pl.program_id(2)

            @pl.when(k == 0)
            def _():
                acc_ref[...] = part

            @pl.when(k > 0)
            def _():
                acc_ref[...] += part

            @pl.when(k == nk - 1)
            def _():
                finish(acc_ref[...])

    in_specs = [a_spec, b_spec] + ([pl.BlockSpec(tile, lambda i, j, k: (i, j))] if resid is not None else [])
    args = (a, b) + ((resid,) if resid is not None else ())
    return pl.pallas_call(
        body, name=name, grid=grid, in_specs=in_specs, out_specs=o_spec, out_shape=out_shape,
        scratch_shapes=[pltpu.VMEM(tile, F32)] if nk > 1 else [],
        compiler_params=_params(("parallel", "parallel", "arbitrary")),
    )(*args)


def _matmul(a, b, *, name, ta=False, tb=False, out_dtype=F32, tm=None, tn=None, tk=None, resid=None, scale=1.0):
    if ta:
        K, M = a.shape
    else:
        M, K = a.shape
    N = b.shape[0] if tb else b.shape[1]
    tm = tm or min(M, 1024)
    tn = tn or min(N, 512)
    tk = tk or K
    assert M % tm == 0 and N % tn == 0 and K % tk == 0, (name, M, N, K, tm, tn, tk)
    a_spec = pl.BlockSpec((tk, tm), lambda i, j, k: (k, i)) if ta else pl.BlockSpec((tm, tk), lambda i, j, k: (i, k))
    b_spec = pl.BlockSpec((tn, tk), lambda i, j, k: (j, k)) if tb else pl.BlockSpec((tk, tn), lambda i, j, k: (k, j))
    return _mm(name=name, grid=(M // tm, N // tn, K // tk), a=a, a_spec=a_spec, b=b, b_spec=b_spec,
               out_shape=jax.ShapeDtypeStruct((M, N), out_dtype), o_spec=pl.BlockSpec((tm, tn), lambda i, j, k: (i, j)),
               tile=(tm, tn), ta=ta, tb=tb, resid=resid, scale=scale)


ROW_TILE = 256


def _rmsnorm_fwd(x, gain, *, name):
    T, D = x.shape

    def body(x_ref, g_ref, o_ref):
        xf = x_ref[...]
        r = lax.rsqrt(jnp.mean(xf * xf, axis=-1, keepdims=True) + RMS_EPS)
        o_ref[...] = (xf * r * g_ref[...]).astype(o_ref.dtype)

    return pl.pallas_call(
        body, name=name, grid=(T // ROW_TILE,),
        in_specs=[pl.BlockSpec((ROW_TILE, D), lambda i: (i, 0)), pl.BlockSpec((1, D), lambda i: (0, 0))],
        out_specs=pl.BlockSpec((ROW_TILE, D), lambda i: (i, 0)),
        out_shape=jax.ShapeDtypeStruct((T, D), BF16), compiler_params=_params(("parallel",)),
    )(x, gain)


def _rmsnorm_bwd(dh, x, gain, dres, *, name):
    T, D = x.shape

    def body(dh_ref, x_ref, g_ref, res_ref, dx_ref, dg_ref):
        xf = x_ref[...]
        r = lax.rsqrt(jnp.mean(xf * xf, axis=-1, keepdims=True) + RMS_EPS)
        y = xf * r
        dh_v = dh_ref[...].astype(F32)
        dy = dh_v * g_ref[...]
        dx_ref[...] = res_ref[...] + r * (dy - y * jnp.mean(dy * y, axis=-1, keepdims=True))

        @pl.when(pl.program_id(0) == 0)
        def _():
            dg_ref[...] = jnp.zeros_like(dg_ref)

        dg_ref[...] += jnp.sum(dh_v * y, axis=0, keepdims=True)

    row = pl.BlockSpec((ROW_TILE, D), lambda i: (i, 0))
    vec = pl.BlockSpec((1, D), lambda i: (0, 0))
    return pl.pallas_call(
        body, name=name, grid=(T // ROW_TILE,), in_specs=[row, row, vec, row], out_specs=(row, vec),
        out_shape=(jax.ShapeDtypeStruct((T, D), F32), jax.ShapeDtypeStruct((1, D), F32)),
        compiler_params=_params(("arbitrary",)),
    )(dh, x, gain, dres)


FF_HALF = N_DEV // 2


def _swiglu_fwd(p, *, name):
    _, T, fb = p.shape

    def body(g_ref, u_ref, o_ref):
        g = g_ref[...].astype(F32)
        o_ref[...] = (g * _sigmoid(g) * u_ref[...].astype(F32)).astype(o_ref.dtype)

    blk = (None, ROW_TILE, fb)
    return pl.pallas_call(
        body, name=name, grid=(T // ROW_TILE, FF_HALF),
        in_specs=[pl.BlockSpec(blk, lambda i, j: (j, i, 0)), pl.BlockSpec(blk, lambda i, j: (j + FF_HALF, i, 0))],
        out_specs=pl.BlockSpec(blk, lambda i, j: (j, i, 0)),
        out_shape=jax.ShapeDtypeStruct((FF_HALF, T, fb), BF16), compiler_params=_params(("parallel", "parallel")),
    )(p, p)


def _swiglu_bwd(da, p, *, name):
    _, T, fb = p.shape

    def body(da_ref, g_ref, u_ref, o_ref):
        g = g_ref[...].astype(F32)
        u = u_ref[...].astype(F32)
        d = da_ref[...].astype(F32)
        s = _sigmoid(g)
        o_ref[0] = (d * u * (s * (1.0 + g * (1.0 - s)))).astype(o_ref.dtype)
        o_ref[1] = (d * g * s).astype(o_ref.dtype)

    blk = (None, ROW_TILE, fb)
    out = pl.pallas_call(
        body, name=name, grid=(T // ROW_TILE, FF_HALF),
        in_specs=[pl.BlockSpec(blk, lambda i, j: (j, i, 0)), pl.BlockSpec(blk, lambda i, j: (j, i, 0)),
                  pl.BlockSpec(blk, lambda i, j: (j + FF_HALF, i, 0))],
        out_specs=pl.BlockSpec((2, None, ROW_TILE, fb), lambda i, j: (0, j, i, 0)),
        out_shape=jax.ShapeDtypeStruct((2, FF_HALF, T, fb), BF16), compiler_params=_params(("parallel", "parallel")),
    )(da, p, p)
    return out.reshape(2 * FF_HALF, T, fb)


COL_GATE_A = 7
COL_GATE_B = 8


def _merge_fwd(ya, yb, proj, *, name):
    T, D = ya.shape

    def body(ya_ref, yb_ref, ga_ref, gb_ref, o_ref):
        o_ref[...] = (_sigmoid(ga_ref[...]) * ya_ref[...] + _sigmoid(gb_ref[...]) * yb_ref[...]).astype(o_ref.dtype)

    row = pl.BlockSpec((ROW_TILE, D), lambda i: (i, 0))
    return pl.pallas_call(
        body, name=name, grid=(T // ROW_TILE,),
        in_specs=[row, row, pl.BlockSpec((ROW_TILE, D), lambda i: (i, COL_GATE_A)),
                  pl.BlockSpec((ROW_TILE, D), lambda i: (i, COL_GATE_B))],
        out_specs=row, out_shape=jax.ShapeDtypeStruct((T, D), BF16), compiler_params=_params(("parallel",)),
    )(ya, yb, proj, proj)


def _merge_bwd(dm, ya, yb, proj, *, name):
    T, D = ya.shape

    def body(dm_ref, ya_ref, yb_ref, ga_ref, gb_ref, dya_ref, dyb_ref, dga_ref, dgb_ref):
        d = dm_ref[...].astype(F32)
        sa = _sigmoid(ga_ref[...])
        sb = _sigmoid(gb_ref[...])
        dya_ref[...] = (d * sa).astype(BF16)
        dyb_ref[...] = (d * sb).astype(BF16)
        dga_ref[...] = (d * ya_ref[...] * sa * (1.0 - sa)).astype(BF16)
        dgb_ref[...] = (d * yb_ref[...] * sb * (1.0 - sb)).astype(BF16)

    row = pl.BlockSpec((ROW_TILE, D), lambda i: (i, 0))
    out = jax.ShapeDtypeStruct((T, D), BF16)
    return pl.pallas_call(
        body, name=name, grid=(T // ROW_TILE,),
        in_specs=[row, row, row, pl.BlockSpec((ROW_TILE, D), lambda i: (i, COL_GATE_A)),
                  pl.BlockSpec((ROW_TILE, D), lambda i: (i, COL_GATE_B))],
        out_specs=(row, row, row, row), out_shape=(out, out, out, out), compiler_params=_params(("parallel",)),
    )(dm, ya, yb, proj, proj)


def _loss_head(y, target, *, name):
    T, D = y.shape

    def body(y_ref, t_ref, loss_ref, dy_ref):
        err = y_ref[...] - t_ref[...]
        dy_ref[...] = err * (1.0 / D)

        @pl.when(pl.program_id(0) == 0)
        def _():
            loss_ref[...] = jnp.zeros_like(loss_ref)

        loss_ref[...] += 0.5 * jnp.sum(jnp.sum(err * err, axis=-1, keepdims=True) * (1.0 / D), axis=0, keepdims=True)

    row = pl.BlockSpec((ROW_TILE, D), lambda i: (i, 0))
    return pl.pallas_call(
        body, name=name, grid=(T // ROW_TILE,), in_specs=[row, row],
        out_specs=(pl.BlockSpec((1, 128), lambda i: (0, 0)), row),
        out_shape=(jax.ShapeDtypeStruct((1, 128), F32), jax.ShapeDtypeStruct((T, D), F32)),
        compiler_params=_params(("arbitrary",)),
    )(y, target)


CONV_PAD = 8


def _conv_taps(w, xp, T, first):
    acc = w[0:1, :] * xp[pl.ds(first, T), :]
    for i in range(1, DN_CONV):
        acc = acc + w[i:i + 1, :] * xp[pl.ds(first + i, T), :]
    return acc


def _conv_fwd(proj, conv_w, *, name):
    T = proj.shape[0]

    def body(x_ref, w_ref, o_ref, xp):
        xp[0:CONV_PAD, :] = jnp.zeros((CONV_PAD, HEAD_DIM), F32)
        xp[CONV_PAD:, :] = x_ref[...]
        y = _conv_taps(w_ref[...], xp, T, CONV_PAD - (DN_CONV - 1))
        s = y * _sigmoid(y)
        n = s * lax.rsqrt(jnp.sum(s * s, axis=-1, keepdims=True) + L2_EPS)
        o_ref[0] = jnp.where(pl.program_id(0) < 2, n, s)

    return pl.pallas_call(
        body, name=name, grid=(3, HEADS),
        in_specs=[pl.BlockSpec((T, HEAD_DIM), lambda c, h: (0, c * HEADS + h)),
                  pl.BlockSpec((DN_CONV, HEAD_DIM), lambda c, h: (0, c * HEADS + h))],
        out_specs=pl.BlockSpec((1, T, HEAD_DIM), lambda c, h: (c, 0, h)),
        out_shape=jax.ShapeDtypeStruct((3, T, D_MODEL), F32),
        scratch_shapes=[pltpu.VMEM((T + CONV_PAD, HEAD_DIM), F32)],
        compiler_params=_params(("parallel", "parallel")),
    )(proj, conv_w)


def _conv_bwd(dqkv, proj, conv_w, *, name):
    T = proj.shape[0]

    def body(d_ref, x_ref, w_ref, dx_ref, dw_ref, xp, dyp):
        xp[0:CONV_PAD, :] = jnp.zeros((CONV_PAD, HEAD_DIM), F32)
        xp[CONV_PAD:, :] = x_ref[...]
        w = w_ref[...]
        y = _conv_taps(w, xp, T, CONV_PAD - (DN_CONV - 1))
        sg = _sigmoid(y)
        s = y * sg
        r = lax.rsqrt(jnp.sum(s * s, axis=-1, keepdims=True) + L2_EPS)
        n = s * r
        d = d_ref[0]
        ds = jnp.where(pl.program_id(0) < 2, r * (d - n * jnp.sum(d * n, axis=-1, keepdims=True)), d)
        dy = ds * (sg * (1.0 + y * (1.0 - sg)))
        dyp[0:T, :] = dy
        dyp[T:, :] = jnp.zeros((CONV_PAD, HEAD_DIM), F32)
        dx = w[0:1, :] * dyp[pl.ds(DN_CONV - 1, T), :]
        for i in range(1, DN_CONV):
            dx = dx + w[i:i + 1, :] * dyp[pl.ds(DN_CONV - 1 - i, T), :]
        dx_ref[...] = dx.astype(dx_ref.dtype)
        for i in range(DN_CONV):
            dw_ref[i:i + 1, :] = jnp.sum(dy * xp[pl.ds(CONV_PAD - (DN_CONV - 1) + i, T), :], axis=0, keepdims=True)

    col = lambda c, h: (0, c * HEADS + h)
    return pl.pallas_call(
        body, name=name, grid=(3, HEADS),
        in_specs=[pl.BlockSpec((1, T, HEAD_DIM), lambda c, h: (c, 0, h)), pl.BlockSpec((T, HEAD_DIM), col),
                  pl.BlockSpec((DN_CONV, HEAD_DIM), col)],
        out_specs=(pl.BlockSpec((T, HEAD_DIM), col), pl.BlockSpec((DN_CONV, HEAD_DIM), col)),
        out_shape=(jax.ShapeDtypeStruct((T, 3 * D_MODEL), BF16), jax.ShapeDtypeStruct((DN_CONV, 3 * D_MODEL), F32)),
        scratch_shapes=[pltpu.VMEM((T + CONV_PAD, HEAD_DIM), F32), pltpu.VMEM((T + CONV_PAD, HEAD_DIM), F32)],
        compiler_params=_params(("parallel", "parallel")),
    )(dqkv, proj, conv_w)


def _inv_unit_lower(low, eye):
    x = eye - low
    power = _hdot(low, low, _B_NN)
    steps = int(math.log2(DN_CHUNK)) - 1
    for s in range(steps):
        x = x + _hdot(x, power, _B_NN)
        if s + 1 < steps:
            power = _hdot(power, power, _B_NN)
    return x


_B_NN = (((2,), (1,)), ((0,), (0,)))
_B_NT = (((2,), (2,)), ((0,), (0,)))
_B_TN = (((1,), (1,)), ((0,), (0,)))


def _dn_load(ref, lead, r0, group):
    rows = pl.ds(r0, group * DN_CHUNK)
    cols = lambda h: slice(h * HEAD_DIM, (h + 1) * HEAD_DIM)
    per_head = [(ref[rows, cols(h)] if lead is None else ref[lead, rows, cols(h)]).reshape(group, DN_CHUNK, HEAD_DIM)
                for h in range(DN_HEADS)]
    return jnp.stack(per_head, axis=1).reshape(group * DN_HEADS, DN_CHUNK, HEAD_DIM)


def _dn_chunk_setup(qkv_ref, b_ref, a_ref, hp_ref, n0, group):
    C = DN_CHUNK
    B = group * DN_HEADS
    r0 = pl.multiple_of(n0 * C, C)
    q = _dn_load(qkv_ref, 0, r0, group) * QK_SCALE
    k = _dn_load(qkv_ref, 1, r0, group)
    v = _dn_load(qkv_ref, 2, r0, group)
    ii = lax.broadcasted_iota(jnp.int32, (B, C, C), 1)
    jj = lax.broadcasted_iota(jnp.int32, (B, C, C), 2)
    eye_mask = ii == jj
    eye = jnp.where(eye_mask, 1.0, 0.0).astype(F32)

    def to_col(row):
        return jnp.sum(jnp.where(eye_mask, jnp.broadcast_to(row, (B, C, C)), 0.0), axis=2, keepdims=True)

    def to_row(col):
        return jnp.sum(jnp.where(eye_mask, jnp.broadcast_to(col, (B, C, C)), 0.0), axis=1, keepdims=True)

    def rows(ref):
        return jnp.stack([ref[h, pl.ds(n0, group)] for h in range(DN_HEADS)], axis=1).reshape(B, 1, C)

    def per_head(row):
        return jnp.stack([hp_ref[h, row:row + 1, 0:C] for h in range(DN_HEADS)] * group, axis=0)

    b_row = rows(b_ref)
    a_row = rows(a_ref)
    a_log = per_head(0)
    dt_b = per_head(1)
    beta_row = _sigmoid(b_row)
    neg_ea = -jnp.exp(a_log)
    g_row = neg_ea * _softplus(a_row + dt_b)
    gc_col = jnp.sum(jnp.where(jj <= ii, jnp.broadcast_to(g_row, (B, C, C)), 0.0), axis=2, keepdims=True)
    gc_row = to_row(gc_col)
    g_last = jnp.sum(g_row, axis=2, keepdims=True)
    beta = to_col(beta_row)
    low_incl = ii >= jj
    decay = jnp.exp(jnp.where(low_incl, gc_col - gc_row, -jnp.inf))
    eg = jnp.exp(gc_col)
    egl = jnp.exp(g_last - gc_col)
    el = jnp.exp(g_last)
    kb = k * beta
    pmat = _bdot(kb, k, _B_NT)
    low = jnp.where(ii > jj, pmat * decay, 0.0)
    tinv = _inv_unit_lower(low, eye)
    u = _hdot(tinv, v * beta, _B_NN)
    w = _hdot(tinv, kb * eg, _B_NN)
    qk = _bdot(q, k, _B_NT)
    attn = qk * decay
    return dict(q=q, k=k, v=v, ii=ii, jj=jj, to_col=to_col, to_row=to_row, b_row=b_row, a_row=a_row, dt_b=dt_b,
                beta_row=beta_row, neg_ea=neg_ea, g_row=g_row, gc_col=gc_col, g_last=g_last, beta=beta,
                decay=decay, eg=eg, egl=egl, el=el, kb=kb, pmat=pmat, tinv=tinv, u=u, w=w, qk=qk, attn=attn,
                qd=q * eg, kd=k * egl, r0=r0)


def _dn_store(ref, lead, r0, group, value):
    value = value.reshape(group, DN_HEADS, DN_CHUNK, HEAD_DIM)
    for h in range(DN_HEADS):
        block = value[:, h].reshape(group * DN_CHUNK, HEAD_DIM)
        if lead is None:
            ref[pl.ds(r0, group * DN_CHUNK), h * HEAD_DIM:(h + 1) * HEAD_DIM] = block
        else:
            ref[lead, pl.ds(r0, group * DN_CHUNK), h * HEAD_DIM:(h + 1) * HEAD_DIM] = block


def _dn_specs(T):
    nc = T // DN_CHUNK
    qkv = pl.BlockSpec((3, T, DN_HEADS * HEAD_DIM), lambda h: (0, 0, h))
    rows = pl.BlockSpec((DN_HEADS, nc, 1, DN_CHUNK), lambda h: (h, 0, 0, 0))
    hp = pl.BlockSpec((DN_HEADS, 8, 128), lambda h: (h, 0, 0))
    states = pl.BlockSpec((DN_HEADS, nc, HEAD_DIM, HEAD_DIM), lambda h: (h, 0, 0, 0))
    return nc, qkv, rows, hp, states


def _dn_fwd(qkv, b_rows, a_rows, hp, *, name):
    T = qkv.shape[1]
    nc, qkv_spec, row_spec, hp_spec, st_spec = _dn_specs(T)
    group = math.gcd(nc, DN_GROUP)
    H = DN_HEADS

    def body(qkv_ref, b_ref, a_ref, hp_ref, o_ref, st_ref, s_scr):
        s_scr[...] = jnp.zeros_like(s_scr)

        def step(t, carry):
            n0 = t * group
            c = _dn_chunk_setup(qkv_ref, b_ref, a_ref, hp_ref, n0, group)
            state = s_scr[...]
            outs = []
            for g in range(group):
                sl = slice(g * H, (g + 1) * H)
                for h in range(H):
                    st_ref[h, n0 + g] = state[h]
                v_new = c["u"][sl] - _bdot(c["w"][sl], state, _B_NN)
                outs.append(_bdot(c["qd"][sl], state, _B_NN) + _bdot(c["attn"][sl], v_new, _B_NN))
                state = state * c["el"][sl] + _bdot(c["kd"][sl], v_new, _B_TN)
            s_scr[...] = state
            _dn_store(o_ref, None, c["r0"], group, jnp.concatenate(outs, axis=0))
            return carry

        lax.fori_loop(0, nc // group, step, 0)

    return pl.pallas_call(
        body, name=name, grid=(HEADS // H,), in_specs=[qkv_spec, row_spec, row_spec, hp_spec],
        out_specs=(pl.BlockSpec((T, H * HEAD_DIM), lambda h: (0, h)), st_spec),
        out_shape=(jax.ShapeDtypeStruct((T, D_MODEL), F32),
                   jax.ShapeDtypeStruct((HEADS, nc, HEAD_DIM, HEAD_DIM), F32)),
        scratch_shapes=[pltpu.VMEM((H, HEAD_DIM, HEAD_DIM), F32)], compiler_params=_params(("parallel",)),
    )(qkv, b_rows, a_rows, hp)


def _dn_bwd(qkv, b_rows, a_rows, hp, states, do, *, name):
    T = qkv.shape[1]
    C = DN_CHUNK
    nc, qkv_spec, row_spec, hp_spec, st_spec = _dn_specs(T)
    group = math.gcd(nc, DN_GROUP)
    H = DN_HEADS
    B = group * H

    def body(qkv_ref, b_ref, a_ref, hp_ref, st_ref, do_ref, dqkv_ref, db_ref, da_ref, dhp_ref, ds_scr, acc_scr):
        ds_scr[...] = jnp.zeros_like(ds_scr)
        acc_scr[...] = jnp.zeros_like(acc_scr)

        def step(t, carry):
            n0 = nc - (t + 1) * group
            c = _dn_chunk_setup(qkv_ref, b_ref, a_ref, hp_ref, n0, group)
            state = jnp.stack([st_ref[h, pl.ds(n0, group)] for h in range(H)], axis=1).reshape(B, HEAD_DIM, HEAD_DIM)
            d_o = _dn_load(do_ref, None, c["r0"], group)
            v_new = c["u"] - _bdot(c["w"], state, _B_NN)
            d_vnew_local = _bdot(c["attn"], d_o, _B_TN)
            d_state_local = _bdot(c["qd"], d_o, _B_TN)
            d_state = ds_scr[...]
            d_vnew, d_kd, d_el = [None] * group, [None] * group, [None] * group
            for g in reversed(range(group)):
                sl = slice(g * H, (g + 1) * H)
                d_vnew[g] = d_vnew_local[sl] + _bdot(c["kd"][sl], d_state, _B_NN)
                d_kd[g] = _bdot(v_new[sl], d_state, _B_NT)
                d_el[g] = jnp.sum(jnp.sum(d_state * state[sl], axis=2, keepdims=True), axis=1, keepdims=True)
                d_state = d_state * c["el"][sl] + d_state_local[sl] - _bdot(c["w"][sl], d_vnew[g], _B_TN)
            ds_scr[...] = d_state
            chunk_grads(c, n0, state, d_o, v_new, jnp.concatenate(d_vnew, axis=0), jnp.concatenate(d_kd, axis=0),
                        jnp.concatenate(d_el, axis=0))
            return carry

        def chunk_grads(c, n0, state, d_o, v_new, d_vnew, d_kd, d_el):
            ii, jj = c["ii"], c["jj"]
            q, k, v, kb, beta = c["q"], c["k"], c["v"], c["kb"], c["beta"]
            decay, eg, egl, el = c["decay"], c["eg"], c["egl"], c["el"]
            u, w, tinv = c["u"], c["w"], c["tinv"]
            d_qd = _bdot(d_o, state, _B_NT)
            d_attn = _bdot(d_o, v_new, _B_NT)
            d_w = -_bdot(d_vnew, state, _B_NT)
            d_rv = _hdot(tinv, d_vnew, _B_TN)
            d_rw = _hdot(tinv, d_w, _B_TN)
            d_amat = -(_bdot(d_rv, u, _B_NT) + _bdot(d_rw, w, _B_NT))
            d_low = jnp.where(ii > jj, d_amat, 0.0)
            d_p = d_low * decay
            d_qk = d_attn * decay
            e_mat = (d_low * c["pmat"] + d_attn * c["qk"]) * decay
            d_q = _bdot(d_qk, k, _B_NN) + d_qd * eg
            d_kb = _bdot(d_p, k, _B_NN) + d_rw * eg
            d_k = _bdot(d_qk, q, _B_TN) + _bdot(d_p, kb, _B_TN) + d_kd * egl + d_kb * beta
            d_beta = jnp.sum(d_kb * k, axis=2, keepdims=True) + jnp.sum(d_rv * v, axis=2, keepdims=True)
            d_v = d_rv * beta
            d_eg = jnp.sum(d_qd * q, axis=2, keepdims=True) + jnp.sum(d_rw * kb, axis=2, keepdims=True)
            d_egl = jnp.sum(d_kd * k, axis=2, keepdims=True)
            d_glast = jnp.sum(d_egl * egl, axis=1, keepdims=True) + d_el * el
            row_sum = jnp.sum(e_mat, axis=2, keepdims=True)
            col_sum = c["to_col"](jnp.sum(e_mat, axis=1, keepdims=True))
            d_gc = row_sum - col_sum + d_eg * eg - d_egl * egl
            d_g_row = jnp.sum(jnp.where(ii >= jj, jnp.broadcast_to(d_gc, (B, C, C)), 0.0), axis=1, keepdims=True) + d_glast
            beta_row = c["beta_row"]
            d_b_row = c["to_row"](d_beta) * beta_row * (1.0 - beta_row)
            d_a_row = d_g_row * c["neg_ea"] * _sigmoid(c["a_row"] + c["dt_b"])
            _dn_store(dqkv_ref, 0, c["r0"], group, d_q * QK_SCALE)
            _dn_store(dqkv_ref, 1, c["r0"], group, d_k)
            _dn_store(dqkv_ref, 2, c["r0"], group, d_v)
            d_b_row = d_b_row.reshape(group, H, 1, C)
            d_a_row = d_a_row.reshape(group, H, 1, C)
            d_a_log = jnp.sum((d_g_row * c["g_row"]).reshape(group, H, 1, C), axis=0)
            d_dt_b = jnp.sum(d_a_row, axis=0)
            for h in range(H):
                db_ref[h, pl.ds(n0, group)] = d_b_row[:, h]
                da_ref[h, pl.ds(n0, group)] = d_a_row[:, h]
                acc_scr[h, 0:1, 0:C] += d_a_log[h]
                acc_scr[h, 1:2, 0:C] += d_dt_b[h]

        lax.fori_loop(0, nc // group, step, 0)
        for h in range(H):
            tot = jnp.sum(acc_scr[h], axis=1, keepdims=True)
            dhp_ref[h] = jnp.broadcast_to(tot, (8, 128))

    return pl.pallas_call(
        body, name=name, grid=(HEADS // H,),
        in_specs=[qkv_spec, row_spec, row_spec, hp_spec, st_spec, pl.BlockSpec((T, H * HEAD_DIM), lambda h: (0, h))],
        out_specs=(qkv_spec, row_spec, row_spec, hp_spec),
        out_shape=(jax.ShapeDtypeStruct((3, T, D_MODEL), F32), jax.ShapeDtypeStruct((HEADS, nc, 1, C), F32),
                   jax.ShapeDtypeStruct((HEADS, nc, 1, C), F32), jax.ShapeDtypeStruct((HEADS, 8, 128), F32)),
        scratch_shapes=[pltpu.VMEM((H, HEAD_DIM, HEAD_DIM), F32), pltpu.VMEM((H, 8, 128), F32)],
        compiler_params=_params(("parallel",)),
    )(qkv, b_rows, a_rows, hp, states, do)


COL_Z = 3 * HEADS


def _gated_norm_fwd(o, proj, gain, *, name):
    T = o.shape[0]

    def body(o_ref, z_ref, g_ref, out_ref):
        x = o_ref[...]
        r = lax.rsqrt(jnp.mean(x * x, axis=-1, keepdims=True) + RMS_EPS)
        z = z_ref[...]
        out_ref[...] = (x * r * g_ref[...] * (z * _sigmoid(z))).astype(out_ref.dtype)

    return pl.pallas_call(
        body, name=name, grid=(HEADS,),
        in_specs=[pl.BlockSpec((T, HEAD_DIM), lambda h: (0, h)), pl.BlockSpec((T, HEAD_DIM), lambda h: (0, COL_Z + h)),
                  pl.BlockSpec((1, HEAD_DIM), lambda h: (0, 0))],
        out_specs=pl.BlockSpec((T, HEAD_DIM), lambda h: (0, h)),
        out_shape=jax.ShapeDtypeStruct((T, D_MODEL), BF16), compiler_params=_params(("parallel",)),
    )(o, proj, gain)


def _gated_norm_bwd(dout, o, proj, gain, *, name):
    T = o.shape[0]

    def body(d_ref, o_ref, z_ref, g_ref, do_ref, dz_ref, dg_ref):
        x = o_ref[...]
        r = lax.rsqrt(jnp.mean(x * x, axis=-1, keepdims=True) + RMS_EPS)
        n = x * r
        z = z_ref[...]
        sg = _sigmoid(z)
        d = d_ref[...].astype(F32)
        g = g_ref[...]
        dz_ref[...] = (d * n * g * (sg * (1.0 + z * (1.0 - sg)))).astype(dz_ref.dtype)
        dy = d * (z * sg)
        dyg = dy * g
        do_ref[...] = r * (dyg - n * jnp.mean(dyg * n, axis=-1, keepdims=True))

        @pl.when(pl.program_id(0) == 0)
        def _():
            dg_ref[...] = jnp.zeros_like(dg_ref)

        dg_ref[...] += jnp.sum(dy * n, axis=0, keepdims=True)

    head = pl.BlockSpec((T, HEAD_DIM), lambda h: (0, h))
    vec = pl.BlockSpec((1, HEAD_DIM), lambda h: (0, 0))
    return pl.pallas_call(
        body, name=name, grid=(HEADS,),
        in_specs=[head, head, pl.BlockSpec((T, HEAD_DIM), lambda h: (0, COL_Z + h)), vec],
        out_specs=(head, head, vec),
        out_shape=(jax.ShapeDtypeStruct((T, D_MODEL), F32), jax.ShapeDtypeStruct((T, D_MODEL), BF16),
                   jax.ShapeDtypeStruct((1, HEAD_DIM), F32)),
        compiler_params=_params(("arbitrary",)),
    )(dout, o, proj, gain)


COL_SBQ = 4 * HEADS
COL_SBK = 5 * HEADS
COL_SBV = 6 * HEADS


def _split_dot(x, mat):
    lead = x.shape[:-1]
    x = x.reshape(-1, x.shape[-1])
    hi = x.astype(BF16)
    lo = (x - hi.astype(F32)).astype(BF16)
    out = jnp.dot(hi, mat, preferred_element_type=F32) + jnp.dot(lo, mat, preferred_element_type=F32)
    return out.reshape(lead + (mat.shape[-1],))


def _sb_specs(T):
    col = lambda first: pl.BlockSpec((T, SB_HEADS * HEAD_DIM), lambda h: (0, first // SB_HEADS + h))
    return col(COL_SBQ), col(COL_SBK), col(COL_SBV), pl.BlockSpec((1, HEAD_DIM), lambda h: (0, 0))


def _heads_first(x):
    return jnp.stack([x[:, h * HEAD_DIM:(h + 1) * HEAD_DIM] for h in range(SB_HEADS)], axis=0)


def _heads_last(x):
    return jnp.concatenate([x[h] for h in range(SB_HEADS)], axis=1)


def _head_rms(x):
    r = lax.rsqrt(jnp.mean(x * x, axis=-1, keepdims=True) + RMS_EPS)
    return x * r, r


def _sb_fwd(proj, q_gain, k_gain, *, name):
    T = proj.shape[0]
    B = SB_BLOCK
    H = SB_HEADS
    nb = T // B
    KT = min(SB_KEY_TILE, T)
    NS = KT // B
    q_spec, k_spec, v_spec, g_spec = _sb_specs(T)

    def body(q_ref, k_ref, v_ref, gq_ref, gk_ref, o_ref, lt_ref, qs, ks, vs):
        qs[...] = (_head_rms(_heads_first(q_ref[...]))[0] * (gq_ref[...] * QK_SCALE)).astype(BF16)
        ks[...] = (_head_rms(_heads_first(k_ref[...]))[0] * gk_ref[...]).astype(BF16)
        vs[...] = _heads_first(v_ref[...]).astype(BF16)
        ii = lax.broadcasted_iota(jnp.int32, (B, B), 0)
        jj = lax.broadcasted_iota(jnp.int32, (B, B), 1)
        after = jnp.where(ii > jj, 1.0, 0.0).astype(BF16)
        ahead = lax.broadcasted_iota(jnp.int32, (H, B, KT), 2) - lax.broadcasted_iota(jnp.int32, (H, B, KT), 1)

        def q_block(i, carry):
            rows = pl.ds(pl.multiple_of(i * B, B), B)
            q = qs[:, rows, :]

            def tile(c0, acc, tail, masked):
                cols = pl.ds(c0, KT)
                z = lax.dot_general(q, ks[:, cols, :], _B_NT, preferred_element_type=F32)
                sp = _softplus(z)
                causal = ahead < (i * B - c0)
                log_1mb = jnp.where(causal, -sp, 0.0) if masked else -sp
                parts = [None] * NS
                for b in reversed(range(NS)):
                    blk = log_1mb[:, :, b * B:(b + 1) * B]
                    parts[b] = _split_dot(blk, after) + tail
                    tail = tail + jnp.sum(blk, axis=2, keepdims=True)
                survive = parts[0] if NS == 1 else jnp.concatenate(parts, axis=2)
                wts = jnp.exp(z - sp + survive)
                if masked:
                    wts = jnp.where(causal, wts, 0.0)
                acc = acc + lax.dot_general(wts.astype(BF16), vs[:, cols, :], _B_NN, preferred_element_type=F32)
                return acc, tail

            last = i // NS
            acc, tail = tile(pl.multiple_of(last * KT, KT), jnp.zeros((H, B, HEAD_DIM), F32), jnp.zeros((H, B, 1), F32), True)
            acc, tail = lax.fori_loop(
                1, last + 1, lambda s, c: tile(pl.multiple_of((last - s) * KT, KT), c[0], c[1], False), (acc, tail))
            o_ref[rows, :] = _heads_last(acc).astype(o_ref.dtype)
            lt_ref[rows, :] = _heads_last(jnp.broadcast_to(tail, (H, B, HEAD_DIM)))
            return carry

        lax.fori_loop(0, nb, q_block, 0)

    heads = pl.BlockSpec((T, H * HEAD_DIM), lambda h: (0, h))
    return pl.pallas_call(
        body, name=name, grid=(HEADS // H,), in_specs=[q_spec, k_spec, v_spec, g_spec, g_spec],
        out_specs=(heads, heads),
        out_shape=(jax.ShapeDtypeStruct((T, D_MODEL), BF16), jax.ShapeDtypeStruct((T, D_MODEL), F32)),
        scratch_shapes=[pltpu.VMEM((H, T, HEAD_DIM), BF16)] * 3, compiler_params=_params(("parallel",)),
    )(proj, proj, proj, q_gain, k_gain)


def _sb_bwd(proj, q_gain, k_gain, ltot, do, *, name):
    T = proj.shape[0]
    B = SB_BLOCK
    H = SB_HEADS
    nb = T // B
    KT = min(SB_KEY_TILE, T)
    NS = KT // B
    q_spec, k_spec, v_spec, g_spec = _sb_specs(T)

    def body(q_ref, k_ref, v_ref, gq_ref, gk_ref, lt_ref, do_ref, dq_ref, dk_ref, dv_ref, dgq_ref, dgk_ref,
             qs, ks, vs, dos, lts, dq_acc, dk_acc, dv_acc):
        qn, q_r = _head_rms(_heads_first(q_ref[...]))
        kn, k_r = _head_rms(_heads_first(k_ref[...]))
        qs[...] = (qn * (gq_ref[...] * QK_SCALE)).astype(BF16)
        ks[...] = (kn * gk_ref[...]).astype(BF16)
        vs[...] = _heads_first(v_ref[...]).astype(BF16)
        dos[...] = _heads_first(do_ref[...]).astype(BF16)
        lts[...] = _heads_first(lt_ref[...])
        dk_acc[...] = jnp.zeros_like(dk_acc)
        dv_acc[...] = jnp.zeros_like(dv_acc)
        ii = lax.broadcasted_iota(jnp.int32, (B, B), 0)
        jj = lax.broadcasted_iota(jnp.int32, (B, B), 1)
        upto = jnp.where(ii <= jj, 1.0, 0.0).astype(BF16)
        before = jnp.where(ii < jj, 1.0, 0.0).astype(BF16)
        ahead = lax.broadcasted_iota(jnp.int32, (H, B, KT), 2) - lax.broadcasted_iota(jnp.int32, (H, B, KT), 1)

        def q_block(i, carry):
            rows = pl.ds(pl.multiple_of(i * B, B), B)
            q = qs[:, rows, :]
            d_o = dos[:, rows, :]
            total = jnp.max(lts[:, rows, :], axis=2, keepdims=True)

            def tile(c0, dq, head_lb, head_de, masked):
                cols = pl.ds(c0, KT)
                k = ks[:, cols, :]
                v = vs[:, cols, :]
                z = lax.dot_general(q, k, _B_NT, preferred_element_type=F32)
                sp = _softplus(z)
                causal = ahead < (i * B - c0)
                log_1mb = jnp.where(causal, -sp, 0.0) if masked else -sp
                parts = [None] * NS
                for b in range(NS):
                    blk = log_1mb[:, :, b * B:(b + 1) * B]
                    parts[b] = _split_dot(blk, upto) + head_lb
                    head_lb = head_lb + jnp.sum(blk, axis=2, keepdims=True)
                prefix = parts[0] if NS == 1 else jnp.concatenate(parts, axis=2)
                wts = jnp.exp(z - sp + (total - prefix))
                if masked:
                    wts = jnp.where(causal, wts, 0.0)
                d_w = lax.dot_general(d_o, v, _B_NT, preferred_element_type=F32)
                d_e = wts * d_w
                for b in range(NS):
                    blk = d_e[:, :, b * B:(b + 1) * B]
                    parts[b] = _split_dot(blk, before) + head_de
                    head_de = head_de + jnp.sum(blk, axis=2, keepdims=True)
                cum = parts[0] if NS == 1 else jnp.concatenate(parts, axis=2)
                sig = jnp.exp(z - sp)
                d_z = d_e * (1.0 - sig) - sig * cum
                if masked:
                    d_z = jnp.where(causal, d_z, 0.0)
                d_zb = d_z.astype(BF16)
                dq = dq + lax.dot_general(d_zb, k, _B_NN, preferred_element_type=F32)
                dk_acc[:, cols, :] += lax.dot_general(d_zb, q, _B_TN, preferred_element_type=F32)
                dv_acc[:, cols, :] += lax.dot_general(wts.astype(BF16), d_o, _B_TN, preferred_element_type=F32)
                return dq, head_lb, head_de

            last = i // NS
            zero = jnp.zeros((H, B, 1), F32)
            state = lax.fori_loop(0, last, lambda t, c: tile(pl.multiple_of(t * KT, KT), *c, False),
                                  (jnp.zeros((H, B, HEAD_DIM), F32), zero, zero))
            dq, _, _ = tile(pl.multiple_of(last * KT, KT), *state, True)
            dq_acc[:, rows, :] = dq * QK_SCALE
            return carry

        lax.fori_loop(0, nb, q_block, 0)

        def norm_bwd(d_scaled, n, r, gain):
            dn = d_scaled * gain
            d_gain = jnp.sum(jnp.sum(d_scaled * n, axis=1, keepdims=True), axis=0)
            return r * (dn - n * jnp.mean(dn * n, axis=-1, keepdims=True)), d_gain

        dq_raw, dgq = norm_bwd(dq_acc[...], qn, q_r, gq_ref[...])
        dk_raw, dgk = norm_bwd(dk_acc[...], kn, k_r, gk_ref[...])
        dq_ref[...] = _heads_last(dq_raw).astype(dq_ref.dtype)
        dk_ref[...] = _heads_last(dk_raw).astype(dk_ref.dtype)
        dv_ref[...] = _heads_last(dv_acc[...]).astype(dv_ref.dtype)

        @pl.when(pl.program_id(0) == 0)
        def _():
            dgq_ref[...] = jnp.zeros_like(dgq_ref)
            dgk_ref[...] = jnp.zeros_like(dgk_ref)

        dgq_ref[...] += dgq
        dgk_ref[...] += dgk

    heads = pl.BlockSpec((T, H * HEAD_DIM), lambda h: (0, h))
    out = jax.ShapeDtypeStruct((T, D_MODEL), BF16)
    vec = jax.ShapeDtypeStruct((1, HEAD_DIM), F32)
    return pl.pallas_call(
        body, name=name, grid=(HEADS // H,), in_specs=[q_spec, k_spec, v_spec, g_spec, g_spec, heads, heads],
        out_specs=(heads, heads, heads, g_spec, g_spec), out_shape=(out, out, out, vec, vec),
        scratch_shapes=[pltpu.VMEM((H, T, HEAD_DIM), BF16)] * 4 + [pltpu.VMEM((H, T, HEAD_DIM), F32)] * 4,
        compiler_params=_params(("arbitrary",)),
    )(proj, proj, proj, q_gain, k_gain, ltot, do)


ADAM_ROWS = 128


def _adamw(g_parts, w, m, v, *, name, layer=0, earlier=None):
    K, A, C = g_parts.shape
    R = w.shape[0]
    tr = ADAM_ROWS if A % ADAM_ROWS == 0 else (A // 2 if A % 32 == 0 else A)
    first_block = layer * (A // tr)

    def body(g_ref, w_ref, m_ref, v_ref, *rest):
        go_ref, d_ref, mo_ref, vo_ref = rest[-4:]
        g = g_ref[0].astype(F32)
        for k in range(1, K):
            g = g + g_ref[k].astype(F32)
        go_ref[...] = g
        m_new = ADAM_B1 * m_ref[...] + (1.0 - ADAM_B1) * g
        v_new = ADAM_B2 * v_ref[...] + (1.0 - ADAM_B2) * (g * g)
        m_hat = m_new / (1.0 - ADAM_B1 ** ADAM_STEP)
        v_hat = v_new / (1.0 - ADAM_B2 ** ADAM_STEP)
        d_ref[...] = -ADAM_LR * (m_hat / (jnp.sqrt(v_hat) + ADAM_EPS) + ADAM_WD * w_ref[...])
        mo_ref[...] = m_new
        vo_ref[...] = v_new

    row = pl.BlockSpec((tr, C), lambda i: (first_block + i, 0))
    out = jax.ShapeDtypeStruct((R, C), F32)
    in_specs = [pl.BlockSpec((K, tr, C), lambda i: (0, i, 0)), row, row, row]
    if earlier is None:
        return pl.pallas_call(
            body, name=name, grid=(A // tr,), in_specs=in_specs, out_specs=(row, row, row, row),
            out_shape=(out, out, out, out), compiler_params=_params(("parallel",)),
        )(g_parts, w, m, v)
    return pl.pallas_call(
        body, name=name, grid=(A // tr,), in_specs=in_specs + [ANY] * 4, out_specs=(row, row, row, row),
        out_shape=(out, out, out, out), input_output_aliases={4 + j: j for j in range(4)},
        compiler_params=_params(("parallel",)),
    )(g_parts, w, m, v, *earlier)


def _sum_parts(parts, *, name):
    K, R, C = parts.shape

    def body(p_ref, o_ref):
        acc = p_ref[0]
        for k in range(1, K):
            acc = acc + p_ref[k]
        o_ref[...] = acc

    return pl.pallas_call(body, name=name, out_shape=jax.ShapeDtypeStruct((R, C), F32))(parts)


def _position():
    return lax.axis_index("x"), lax.axis_index("y"), lax.axis_index("c")


def _all_gather(shards, *, name):
    n = len(shards)

    def body(*refs):
        x_refs, out_refs = refs[:n], refs[n:2 * n]
        send_sems, recv_sems, local_sems = refs[2 * n:]
        x, y, c = _position()
        me, sibling = (x, y, c), (x, y, 1 - c)
        chips = [(1 - x, y), (x, 1 - y), (1 - x, 1 - y)]

        def slot(a, px, py, pc):
            return out_refs[a].at[4 * px + 2 * py + pc]

        def copy(a, k, block, to, own=False):
            return pltpu.make_async_remote_copy(
                src_ref=x_refs[a] if own else slot(a, *block), dst_ref=slot(a, *block),
                send_sem=send_sems.at[a, k], recv_sem=recv_sems.at[a, k], device_id=to, device_id_type=MESH)

        mine = [pltpu.make_async_copy(x_refs[a], slot(a, *me), local_sems.at[a]) for a in range(n)]
        for cp in mine:
            cp.start()
        first = [copy(a, 1 + j, me, (*chip, c), own=True) for j, chip in enumerate(chips) for a in range(n)]
        first += [copy(a, 0, me, sibling, own=True) for a in range(n)]
        for cp in first:
            cp.start()
        passed = []
        for j, chip in enumerate(chips):
            for a in range(n):
                copy(a, 1 + j, (*chip, c), me).wait_recv()
                passed.append(copy(a, 4 + j, (*chip, c), sibling))
                passed[-1].start()
        for a in range(n):
            copy(a, 0, sibling, me).wait_recv()
        for j, chip in enumerate(chips):
            for a in range(n):
                copy(a, 4 + j, (*chip, 1 - c), me).wait_recv()
        for cp in first + passed:
            cp.wait_send()
        for cp in mine:
            cp.wait()

    return pl.pallas_call(
        body, name=name, in_specs=[ANY] * n, out_specs=[ANY] * n,
        out_shape=[jax.ShapeDtypeStruct((N_DEV,) + s.shape, s.dtype) for s in shards],
        scratch_shapes=[pltpu.SemaphoreType.DMA((n, 7)), pltpu.SemaphoreType.DMA((n, 7)), pltpu.SemaphoreType.DMA((n,))],
    )(*shards)


HBM = pl.BlockSpec(memory_space=pltpu.HBM)
SEM = pl.BlockSpec(memory_space=pltpu.SEMAPHORE)
DATAFLOW = pltpu.SideEffectType.DATAFLOW_SIDE_EFFECTING


def _exchange_copies(gather, x_refs, land_refs, send_sems, recv_sems, local_sems):
    n = len(x_refs)
    x, y, c = _position()
    me = 4 * x + 2 * y + c

    def src(a, slot):
        return x_refs[a] if gather else x_refs[a].at[slot]

    mine = [pltpu.make_async_copy(src(a, me), land_refs[a].at[me], local_sems.at[a]) for a in range(n)]
    sends, recvs = [], []
    for k in range(1, N_DEV):
        px, py, pc = (x + (k >> 2)) % 2, (y + ((k >> 1) & 1)) % 2, (c + (k & 1)) % 2
        peer = 4 * px + 2 * py + pc
        for a in range(n):
            sems = dict(send_sem=send_sems.at[7 * a + k - 1], recv_sem=recv_sems.at[7 * a + k - 1],
                        device_id=(px, py, pc), device_id_type=MESH)
            sends.append(pltpu.make_async_remote_copy(src_ref=src(a, peer), dst_ref=land_refs[a].at[me], **sems))
            recvs.append(pltpu.make_async_remote_copy(src_ref=src(a, me), dst_ref=land_refs[a].at[peer], **sems))
    return mine, sends, recvs


def _exchange_start(parts, *, gather, name):
    n = len(parts)

    def body(*refs):
        x_refs, land_refs = refs[:n], refs[n:2 * n]
        send_sems, recv_sems, local_sems = refs[2 * n:2 * n + 3]
        token = refs[-1]
        mine, sends, _ = _exchange_copies(gather, x_refs, land_refs, send_sems, recv_sems, local_sems)
        for cp in mine + sends:
            cp.start()
        token[...] = jnp.zeros_like(token)

    sems = (pltpu.SemaphoreType.DMA((7 * n,)), pltpu.SemaphoreType.DMA((7 * n,)), pltpu.SemaphoreType.DMA((n,)))
    thru = tuple(pltpu.HBM(p.shape, p.dtype) for p in parts)
    land = tuple(pltpu.HBM(((N_DEV,) if gather else ()) + p.shape, p.dtype) for p in parts)
    res = pl.pallas_call(
        body, name=name, in_specs=[HBM] * (2 * n),
        out_specs=(SEM, SEM, SEM) + (HBM,) * (2 * n) + (pl.BlockSpec(memory_space=pltpu.VMEM),),
        out_shape=sems + thru + land + (jax.ShapeDtypeStruct((8, 128), F32),),
        input_output_aliases={a: 3 + a for a in range(2 * n)},
        compiler_params=pltpu.CompilerParams(has_side_effects=DATAFLOW),
    )(*[pltpu.with_memory_space_constraint(p, pltpu.HBM) for p in parts],
      *[pltpu.with_memory_space_constraint(lax.empty(z.shape, z.dtype), pltpu.HBM) for z in land])
    return res[:3], res[3:3 + n], res[3 + n:3 + 2 * n], res[-1]


def _exchange_wait(sems, parts, landing, after, *, gather, name):
    n = len(parts)
    after = list(after)

    def body(*refs):
        x_refs, land_refs = refs[:n], refs[n:2 * n]
        send_sems, recv_sems, local_sems = refs[2 * n:2 * n + 3]
        token = refs[-1]
        mine, sends, recvs = _exchange_copies(gather, x_refs, land_refs, send_sems, recv_sems, local_sems)
        for cp in recvs:
            cp.wait_recv()
        for cp in sends:
            cp.wait_send()
        for cp in mine:
            cp.wait()
        token[...] = jnp.zeros_like(token)

    thru = tuple(pltpu.HBM(p.shape, p.dtype) for p in tuple(parts) + tuple(landing))
    res = pl.pallas_call(
        body, name=name, in_specs=[HBM] * (2 * n) + [SEM, SEM, SEM] + [ANY] * len(after),
        out_specs=(HBM,) * (2 * n) + (pl.BlockSpec(memory_space=pltpu.VMEM),),
        out_shape=thru + (jax.ShapeDtypeStruct((8, 128), F32),), input_output_aliases={a: a for a in range(2 * n)},
        compiler_params=pltpu.CompilerParams(has_side_effects=DATAFLOW),
    )(*parts, *landing, *sems, *after)
    return res[n:2 * n], res[-1]


def _ffn_fwd(x, gain, wg_in, wg_out, tag):
    T, D = x.shape
    fb, rb = wg_in.shape[-1], wg_out.shape[-2]
    tm, tn = min(T, 1024), 512
    h = _rmsnorm_fwd(x, gain, name=f"{tag}_norm")
    p = _mm(name=f"{tag}_in", grid=(T // tm, N_DEV, 1), tile=(tm, fb),
            a=h, a_spec=pl.BlockSpec((tm, D), lambda i, j, k: (i, 0)),
            b=wg_in, b_spec=pl.BlockSpec((None, D, fb), lambda i, j, k: (j, 0, 0)),
            out_shape=jax.ShapeDtypeStruct((N_DEV, T, fb), BF16), o_spec=pl.BlockSpec((None, tm, fb), lambda i, j, k: (j, i, 0)))
    a = _swiglu_fwd(p, name=f"{tag}_act")
    y = _mm(name=f"{tag}_out", grid=(T // tm, D // tn, FF_HALF), tile=(tm, tn), resid=x, scale=0.5,
            a=a, a_spec=pl.BlockSpec((None, tm, fb), lambda i, j, k: (k, i, 0)),
            b=wg_out.reshape(N_DEV * rb, D), b_spec=pl.BlockSpec((fb, tn), lambda i, j, k: (k, j)),
            out_shape=jax.ShapeDtypeStruct((T, D), F32), o_spec=pl.BlockSpec((tm, tn), lambda i, j, k: (i, j)))
    return y, (x, h, p, a)


def _ffn_bwd(dy, saved, gain, wg_in, wg_out, tag, on_weight_grads=None):
    x, h, p, a = saved
    T, D = x.shape
    fb, rb = wg_in.shape[-1], wg_out.shape[-2]
    tm, tn = min(T, 1024), 512
    da = _mm(name=f"{tag}_out_dx", grid=(T // tm, FF_HALF, 1), tile=(tm, fb), tb=True, scale=0.5,
             a=dy, a_spec=pl.BlockSpec((tm, D), lambda i, j, k: (i, 0)),
             b=wg_out.reshape(N_DEV * rb, D), b_spec=pl.BlockSpec((fb, D), lambda i, j, k: (j, 0)),
             out_shape=jax.ShapeDtypeStruct((FF_HALF, T, fb), BF16), o_spec=pl.BlockSpec((None, tm, fb), lambda i, j, k: (j, i, 0)))
    d_w_out = _mm(name=f"{tag}_out_dw", grid=(FF_HALF, D // tn, 1), tile=(fb, tn), ta=True, scale=0.5,
                  a=a, a_spec=pl.BlockSpec((None, T, fb), lambda i, j, k: (i, 0, 0)),
                  b=dy, b_spec=pl.BlockSpec((T, tn), lambda i, j, k: (0, j)),
                  out_shape=jax.ShapeDtypeStruct((FF_HALF, fb, D), BF16), o_spec=pl.BlockSpec((None, fb, tn), lambda i, j, k: (i, 0, j)))
    dp = _swiglu_bwd(da, p, name=f"{tag}_act_bwd")
    d_w_in = _mm(name=f"{tag}_in_dw", grid=(1, N_DEV, 1), tile=(D, fb), ta=True,
                 a=h, a_spec=pl.BlockSpec((T, D), lambda i, j, k: (0, 0)),
                 b=dp, b_spec=pl.BlockSpec((None, T, fb), lambda i, j, k: (j, 0, 0)),
                 out_shape=jax.ShapeDtypeStruct((N_DEV, D, fb), BF16), o_spec=pl.BlockSpec((None, D, fb), lambda i, j, k: (j, 0, 0)))
    dh = _mm(name=f"{tag}_in_dx", grid=(T // tm, D // tn, N_DEV), tile=(tm, tn), tb=True,
             a=dp, a_spec=pl.BlockSpec((None, tm, fb), lambda i, j, k: (k, i, 0)),
             b=wg_in, b_spec=pl.BlockSpec((None, tn, fb), lambda i, j, k: (k, j, 0)),
             out_shape=jax.ShapeDtypeStruct((T, D), F32), o_spec=pl.BlockSpec((tm, tn), lambda i, j, k: (i, j)))
    d_w_out = d_w_out.reshape(N_DEV, rb, D)
    if on_weight_grads is not None:
        gain = gain + on_weight_grads(d_w_in, d_w_out)[0, 0]
    dx, d_gain = _rmsnorm_bwd(dh, x, gain, dy, name=f"{tag}_norm_bwd")
    return dx, d_gain, d_w_in, d_w_out


def _square_mm(a, wg, *, name, transposed=False, out_dtype=F32, resid=None):
    T, D = a.shape
    w = wg.reshape(D, D)
    return _matmul(a, w, tb=transposed, name=name, out_dtype=out_dtype, resid=resid)


def _head_rows(cols, T):
    return cols.T.reshape(HEADS, T // DN_CHUNK, 1, DN_CHUNK)


def _mixer_fwd(x, w, big, tag):
    T = x.shape[0]
    h = _rmsnorm_fwd(x, w["mix_norm"], name=f"{tag}_norm")
    proj = _matmul(h, big["w_main"], name=f"{tag}_proj")
    scal = _matmul(h, big["w_scal"], name=f"{tag}_proj_scal", tn=N_SCAL)
    qkv = _conv_fwd(proj, big["conv_w"], name=f"{tag}_conv")
    b_rows = _head_rows(scal[:, 0:HEADS], T)
    a_rows = _head_rows(scal[:, HEADS:2 * HEADS], T)
    o_a, states = _dn_fwd(qkv, b_rows, a_rows, w["hp"], name=f"{tag}_dn")
    oa_n = _gated_norm_fwd(o_a, proj, w["dn_out_norm"], name=f"{tag}_dn_norm")
    ya = _square_mm(oa_n, big["w_branch_a"], name=f"{tag}_branch_a")
    o_b, ltot = _sb_fwd(proj, w["sb_q_norm"], w["sb_k_norm"], name=f"{tag}_sb")
    yb = _square_mm(o_b, big["w_branch_b"], name=f"{tag}_branch_b")
    merged = _merge_fwd(ya, yb, proj, name=f"{tag}_merge")
    y = _square_mm(merged, big["w_out"], name=f"{tag}_out", resid=x)
    return y, (x, h, proj, qkv, b_rows, a_rows, o_a, states, oa_n, ya, o_b, ltot, yb, merged)


def _mixer_bwd(dy, saved, w, big, tag, on_weight_grads):
    x, h, proj, qkv, b_rows, a_rows, o_a, states, oa_n, ya, o_b, ltot, yb, merged = saved
    T = x.shape[0]
    g = {}
    d_merged = _square_mm(dy, big["w_out"], transposed=True, name=f"{tag}_out_dx", out_dtype=BF16)
    g["w_out"] = _matmul(merged, dy, ta=True, name=f"{tag}_out_dw", out_dtype=BF16)
    d_ya, d_yb, d_ga, d_gb = _merge_bwd(d_merged, ya, yb, proj, name=f"{tag}_merge_bwd")
    d_oan = _square_mm(d_ya, big["w_branch_a"], transposed=True, name=f"{tag}_branch_a_dx")
    g["w_branch_a"] = _matmul(oa_n, d_ya, ta=True, name=f"{tag}_branch_a_dw", out_dtype=BF16)
    d_ob = _square_mm(d_yb, big["w_branch_b"], transposed=True, name=f"{tag}_branch_b_dx")
    g["w_branch_b"] = _matmul(o_b, d_yb, ta=True, name=f"{tag}_branch_b_dw", out_dtype=BF16)
    d_oa, d_z, g["dn_out_norm"] = _gated_norm_bwd(d_oan, o_a, proj, w["dn_out_norm"], name=f"{tag}_dn_norm_bwd")
    d_qkv, d_b_rows, d_a_rows, d_hp = _dn_bwd(qkv, b_rows, a_rows, w["hp"], states, d_oa, name=f"{tag}_dn_bwd")
    g["dn_a_log"] = d_hp[:, 0, 0]
    g["dn_dt_bias"] = d_hp[:, 1, 0]
    d_conv_in, g["conv_w"] = _conv_bwd(d_qkv, proj, big["conv_w"], name=f"{tag}_conv_bwd")
    d_sbq, d_sbk, d_sbv, g["sb_q_norm"], g["sb_k_norm"] = _sb_bwd(
        proj, w["sb_q_norm"], w["sb_k_norm"], ltot, d_ob, name=f"{tag}_sb_bwd")
    d_proj = jnp.concatenate([d_conv_in, d_z, d_sbq, d_sbk, d_sbv, d_ga, d_gb], axis=1)
    d_scal = jnp.concatenate([d_b_rows.reshape(HEADS, T).T, d_a_rows.reshape(HEADS, T).T,
                              jnp.zeros((T, N_SCAL - 2 * HEADS), F32)], axis=1).astype(BF16)
    g["w_main"] = _matmul(h, d_proj, ta=True, name=f"{tag}_proj_dw", out_dtype=BF16)
    g["w_scal"] = _matmul(h, d_scal, ta=True, name=f"{tag}_proj_scal_dw", out_dtype=BF16, tn=N_SCAL)
    dh_scal = _matmul(d_scal, big["w_scal"], tb=True, name=f"{tag}_proj_scal_dx")
    dh = _matmul(d_proj, big["w_main"], tb=True, name=f"{tag}_proj_dx", tk=N_MAIN // 4, resid=dh_scal)
    gain = w["mix_norm"] + on_weight_grads(g)[0, 0]
    dx, g["mix_norm"] = _rmsnorm_bwd(dh, x, gain, dy, name=f"{tag}_norm_bwd")
    return dx, g


def _local_step(x, target, layers, weights_of, on_weight_grads):
    saved, bigs = [], []
    for l, w in enumerate(layers):
        big = weights_of(l, x)
        x, s1 = _ffn_fwd(x, w["ffn1_norm"] + big["started"], big["ffn1_w_in"], big["ffn1_w_out"], f"l{l}_ffn1")
        x, s2 = _mixer_fwd(x, w, big, f"l{l}_mix")
        x, s3 = _ffn_fwd(x, w["ffn2_norm"], big["ffn2_w_in"], big["ffn2_w_out"], f"l{l}_ffn2")
        saved.append((s1, s2, s3))
        bigs.append(big)
    loss, dx = _loss_head(x, target, name="loss_head")
    small = [None] * len(layers)
    for l in reversed(range(len(layers))):
        w, big = layers[l], bigs[l]
        s1, s2, s3 = saved[l]
        dx, g_n2, _, _ = _ffn_bwd(
            dx, s3, w["ffn2_norm"], big["ffn2_w_in"], big["ffn2_w_out"], f"l{l}_ffn2",
            on_weight_grads=lambda g_in, g_out, l=l: on_weight_grads(l, 0, dict(ffn2_w_in=g_in, ffn2_w_out=g_out)))
        dx, g = _mixer_bwd(dx, s2, w, big, f"l{l}_mix", on_weight_grads=lambda g, l=l: on_weight_grads(l, 1, g))
        dx, g_n1, _, _ = _ffn_bwd(
            dx, s1, w["ffn1_norm"], big["ffn1_w_in"], big["ffn1_w_out"], f"l{l}_ffn1",
            on_weight_grads=lambda g_in, g_out, l=l: on_weight_grads(l, 2, dict(ffn1_w_in=g_in, ffn1_w_out=g_out)))
        small[l] = dict(g, ffn1_norm=g_n1, ffn2_norm=g_n2)
    return loss, dx, small


_BIG = ("ffn1_w_in", "ffn1_w_out", "w_in", "w_branch_a", "w_branch_b", "w_out", "ffn2_w_in", "ffn2_w_out")
_STAGES = (("ffn2_w_in", "ffn2_w_out"), ("w_in", "w_branch_a", "w_branch_b", "w_out"), ("ffn1_w_in", "ffn1_w_out"))
_SMALL = ("ffn1_norm", "mix_norm", "ffn2_norm", "dn_a_log", "dn_dt_bias", "dn_out_norm", "sb_q_norm", "sb_k_norm")
_ORDER = ("ffn1_norm", "ffn1_w_in", "ffn1_w_out", "mix_norm", "w_in", "dn_conv_w", "dn_a_log", "dn_dt_bias", "dn_out_norm",
          "sb_q_norm", "sb_k_norm", "w_branch_a", "w_branch_b", "w_out", "ffn2_norm", "ffn2_w_in", "ffn2_w_out")
COL_SCAL = 4 * D_MODEL


def _pad_rows(a, multiple):
    pad = (-a.shape[-2]) % multiple
    return a if pad == 0 else jnp.pad(a, [(0, 0)] * (a.ndim - 2) + [(0, pad), (0, 0)])


def _lane_rows(a):
    flat = a.reshape(-1)
    flat = jnp.pad(flat, (0, (-flat.shape[0]) % 128))
    return flat.reshape(-1, 128)


def _pack_small(named):
    pieces, spans, r = [], {}, 0
    for n, a in named:
        rows = _lane_rows(a)
        spans[n] = (r, r + rows.shape[0], a.shape)
        r += rows.shape[0]
        pieces.append(rows)
    return _pad_rows(jnp.concatenate(pieces, axis=0), 8), spans


def _unpack_small(packed, spans, n):
    r0, r1, shape = spans[n]
    return packed[r0:r1].reshape(-1)[:math.prod(shape)].reshape(shape)


def kernel(x, ffn1_norm, ffn1_w_in, ffn1_w_out, mix_norm, w_in, dn_conv_w, dn_a_log, dn_dt_bias, dn_out_norm, sb_q_norm, sb_k_norm, w_branch_a, w_branch_b, w_out, ffn2_norm, ffn2_w_in, ffn2_w_out, loss_target, m_ffn1_norm, m_ffn1_w_in, m_ffn1_w_out, m_mix_norm, m_w_in, m_dn_conv_w, m_dn_a_log, m_dn_dt_bias, m_dn_out_norm, m_sb_q_norm, m_sb_k_norm, m_w_branch_a, m_w_branch_b, m_w_out, m_ffn2_norm, m_ffn2_w_in, m_ffn2_w_out, v_ffn1_norm, v_ffn1_w_in, v_ffn1_w_out, v_mix_norm, v_w_in, v_dn_conv_w, v_dn_a_log, v_dn_dt_bias, v_dn_out_norm, v_sb_q_norm, v_sb_k_norm, v_w_branch_a, v_w_branch_b, v_w_out, v_ffn2_norm, v_ffn2_w_in, v_ffn2_w_out):
    given = dict(locals())
    weights = {n: given[n] for n in _ORDER}
    mom_m = {n: given["m_" + n] for n in _ORDER}
    mom_v = {n: given["v_" + n] for n in _ORDER}
    L = ffn1_norm.shape[0]
    ax, ay, ac = _position()
    my_slot = 4 * ax + 2 * ay + ac

    conv_cols = dn_conv_w.shape[-1]
    shards = lambda l, zero: [(weights[n][l] + zero).astype(BF16) for n in _BIG]
    *first_layer, conv_full = _all_gather(shards(0, 0.0) + [_pad_rows(_lane_rows(dn_conv_w), 8)], name="gather_l0")
    conv_full = conv_full.reshape(N_DEV, -1)[:, :L * DN_CONV * conv_cols].reshape(N_DEV, L, DN_CONV, conv_cols)
    conv_full = conv_full.transpose(1, 2, 0, 3).reshape(L, DN_CONV, N_DEV * conv_cols)
    arriving = {}

    def start_gather(l, zero):
        if l >= L:
            return 0.0
        *arriving[l], token = _exchange_start(shards(l, zero), gather=True, name=f"gather_start_l{l}")
        return token[0, 0]

    def weights_of(l, x_in):
        if l == 0:
            arrays, started = first_layer, start_gather(1, 0.0)
        else:
            arrays, token = _exchange_wait(*arriving[l], x_in, gather=True, name=f"gather_wait_l{l}")
            started = start_gather(l + 1, token[0, 0])
        big = dict(zip(_BIG, arrays), started=started)
        wi = big.pop("w_in").transpose(1, 0, 2).reshape(D_MODEL, N_IN)
        big["w_main"] = jnp.concatenate([wi[:, :COL_SCAL], wi[:, COL_SCAL + 2 * HEADS:]], axis=1)
        big["w_scal"] = jnp.pad(wi[:, COL_SCAL:COL_SCAL + 2 * HEADS], ((0, 0), (0, N_SCAL - 2 * HEADS)))
        big["conv_w"] = conv_full[l]
        return big

    layers = []
    for l in range(L):
        hp = jnp.concatenate([jnp.broadcast_to(dn_a_log[l][:, None, None], (HEADS, 1, 128)),
                              jnp.broadcast_to(dn_dt_bias[l][:, None, None], (HEADS, 1, 128)),
                              jnp.zeros((HEADS, 6, 128), F32)], axis=1)
        layers.append(dict(ffn1_norm=ffn1_norm[l][None], mix_norm=mix_norm[l][None], hp=hp,
                           dn_out_norm=dn_out_norm[l][None], sb_q_norm=sb_q_norm[l][None],
                           sb_k_norm=sb_k_norm[l][None], ffn2_norm=ffn2_norm[l][None]))

    in_flight = {}

    def on_weight_grads(l, stage, g):
        parts = dict(g)
        if stage == 1:
            g_w_in = jnp.concatenate([g["w_main"][:, :COL_SCAL], g["w_scal"][:, :2 * HEADS], g["w_main"][:, COL_SCAL:]], axis=1)
            parts["w_in"] = g_w_in.reshape(D_MODEL, N_DEV, N_IN // N_DEV).transpose(1, 0, 2)
            for n in ("w_branch_a", "w_branch_b", "w_out"):
                parts[n] = g[n].reshape(N_DEV, D_MODEL // N_DEV, D_MODEL)
        *in_flight[l, stage], token = _exchange_start([parts[n] for n in _STAGES[stage]], gather=False,
                                                      name=f"scatter_start_l{l}_{stage}")
        return token

    loss_row, dx, grads = _local_step(x[0], loss_target[0], layers, weights_of, on_weight_grads)
    loss = lax.psum(loss_row[0, 0], ("x", "y", "c"))

    results = {n: None for n in _BIG}
    after = [dx]
    for l in reversed(range(L)):
        for stage, names in enumerate(_STAGES):
            landed, _ = _exchange_wait(*in_flight[l, stage], after, gather=False, name=f"scatter_wait_l{l}_{stage}")
            for n, parts in zip(names, landed):
                _, a, b = weights[n].shape
                results[n] = _adamw(parts, weights[n].reshape(L * a, b), mom_m[n].reshape(L * a, b),
                                    mom_v[n].reshape(L * a, b), layer=l, earlier=results[n], name=f"adamw_{n}_l{l}")
            after = [results[n][0] for n in names]
    out = {n: tuple(t.reshape(weights[n].shape) for t in results[n]) for n in _BIG}

    small_grads = [(n, jnp.stack([g[n].reshape(weights[n].shape[1:]) for g in grads])) for n in _SMALL]
    small_packed, spans = _pack_small(small_grads + [("conv", jnp.stack([g["conv_w"] for g in grads]))])
    small_sum = _sum_parts(_all_gather([small_packed], name="gather_small_grads")[0], name="sum_small_grads")
    rep_rows = spans["conv"][0]
    pack_rep = lambda d: _pad_rows(_pack_small([(n, d[n]) for n in _SMALL])[0], 8)
    rep_pad = (-rep_rows) % 8
    g_rep = jnp.pad(small_sum[:rep_rows], ((0, rep_pad), (0, 0)))
    res = _adamw(g_rep[None], pack_rep(weights), pack_rep(mom_m), pack_rep(mom_v), name="adamw_replicated")
    for n in _SMALL:
        out[n] = tuple(_unpack_small(t, spans, n) for t in res)
    conv_sum = _unpack_small(small_sum, spans, "conv")
    conv_mine = lax.dynamic_slice_in_dim(conv_sum, my_slot * conv_cols, conv_cols, axis=2).reshape(L * DN_CONV, conv_cols)
    flat = lambda t: t.reshape(L * DN_CONV, conv_cols)
    res = _adamw(conv_mine[None], flat(dn_conv_w), flat(m_dn_conv_w), flat(v_dn_conv_w), name="adamw_conv")
    out["dn_conv_w"] = tuple(t.reshape(L, DN_CONV, conv_cols) for t in res)

    return (loss, dx[None], *[out[n][0] for n in _ORDER], *[out[n][1] for n in _ORDER],
            *[out[n][2] for n in _ORDER], *[out[n][3] for n in _ORDER])
```

```python
import functools
import math

import jax
import jax.numpy as jnp
from jax import lax
from jax.experimental import pallas as pl
from jax.experimental.pallas import tpu as pltpu

F32 = jnp.float32
BF16 = jnp.bfloat16

N_DEV = 8
D_MODEL = 1024
DEPTH = 4
D_FF = 2816
HEADS = 8
HEAD_DIM = 128
DN_CHUNK = 64
DN_CONV = 4
DN_GROUP = 8
DN_HEADS = 2
SB_BLOCK = 128
SB_KEY_TILE = 512
SB_HEADS = 4
SB_HEADS_BWD = 2
RMS_EPS = 1e-6
L2_EPS = 1e-6
N_IN = 9232
N_MAIN = 9216
N_SCAL = 128
QK_SCALE = HEAD_DIM ** -0.5

ADAM_LR = 0.001
ADAM_B1 = 0.9
ADAM_B2 = 0.999
ADAM_EPS = 1e-08
ADAM_WD = 0.01
ADAM_STEP = 10

V7X_VMEM_LIMIT = 56 * 1024 * 1024
MESH = pl.DeviceIdType.MESH
ANY = pl.BlockSpec(memory_space=pl.ANY)


def _params(sem=None, vmem=V7X_VMEM_LIMIT):
    return pltpu.CompilerParams(dimension_semantics=sem, vmem_limit_bytes=vmem)


def _sigmoid(x):
    return 1.0 / (1.0 + jnp.exp(-x))


def _softplus(x):
    return jnp.maximum(x, 0.0) + jnp.log(1.0 + jnp.exp(-jnp.abs(x)))


def _bdot(a, b, dims=(((1,), (0,)), ((), ()))):
    return lax.dot_general(a.astype(BF16), b.astype(BF16), dims, preferred_element_type=F32)


_NT = (((1,), (1,)), ((), ()))
_TN = (((0,), (0,)), ((), ()))


def _hdot(a, b, dims=(((1,), (0,)), ((), ()))):
    a_hi = a.astype(BF16)
    b_hi = b.astype(BF16)
    a_lo = (a - a_hi.astype(F32)).astype(BF16)
    b_lo = (b - b_hi.astype(F32)).astype(BF16)
    dot = functools.partial(lax.dot_general, dimension_numbers=dims, preferred_element_type=F32)
    return dot(a_hi, b_hi) + (dot(a_hi, b_lo) + dot(a_lo, b_hi))


def _hdot_tn(a, b):
    return _hdot(a, b, _TN)


def _mm(*, name, grid, a, a_spec, b, b_spec, out_shape, o_spec, tile, ta=False, tb=False, resid=None, scale=1.0):
    nk = grid[2]
    dims = (((0 if ta else 1,), (1 if tb else 0,)), ((), ()))

    def flat(v):
        return v if v.ndim == 2 else v.reshape(-1, v.shape[-1])

    def body(*refs):
        a_ref, b_ref = refs[:2]
        r_ref = refs[2] if resid is not None else None
        o_ref = refs[3] if resid is not None else refs[2]
        part = lax.dot_general(flat(a_ref[...]).astype(BF16), flat(b_ref[...]).astype(BF16), dims,
                               preferred_element_type=F32)

        def finish(acc):
            if scale != 1.0:
                acc = acc * scale
            if r_ref is not None:
                acc = r_ref[...] + acc
            o_ref[...] = acc.astype(o_ref.dtype)

        if nk == 1:
            finish(part)
        else:
            acc_ref = refs[-1]
            k = pl.program_id(2)

            @pl.when(k == 0)
            def _():
                acc_ref[...] = part

            @pl.when(k > 0)
            def _():
                acc_ref[...] += part

            @pl.when(k == nk - 1)
            def _():
                finish(acc_ref[...])

    in_specs = [a_spec, b_spec] + ([pl.BlockSpec(tile, lambda i, j, k: (i, j))] if resid is not None else [])
    args = (a, b) + ((resid,) if resid is not None else ())
    return pl.pallas_call(
        body, name=name, grid=grid, in_specs=in_specs, out_specs=o_spec, out_shape=out_shape,
        scratch_shapes=[pltpu.VMEM(tile, F32)] if nk > 1 else [],
        compiler_params=_params(("parallel", "parallel", "arbitrary")),
    )(*args)


def _matmul(a, b, *, name, ta=False, tb=False, out_dtype=F32, tm=None, tn=None, tk=None, resid=None, scale=1.0):
    if ta:
        K, M = a.shape
    else:
        M, K = a.shape
    N = b.shape[0] if tb else b.shape[1]
    tm = tm or min(M, 1024)
    tn = tn or min(N, 512)
    tk = tk or K
    assert M % tm == 0 and N % tn == 0 and K % tk == 0, (name, M, N, K, tm, tn, tk)
    a_spec = pl.BlockSpec((tk, tm), lambda i, j, k: (k, i)) if ta else pl.BlockSpec((tm, tk), lambda i, j, k: (i, k))
    b_spec = pl.BlockSpec((tn, tk), lambda i, j, k: (j, k)) if tb else pl.BlockSpec((tk, tn), lambda i, j, k: (k, j))
    return _mm(name=name, grid=(M // tm, N // tn, K // tk), a=a, a_spec=a_spec, b=b, b_spec=b_spec,
               out_shape=jax.ShapeDtypeStruct((M, N), out_dtype), o_spec=pl.BlockSpec((tm, tn), lambda i, j, k: (i, j)),
               tile=(tm, tn), ta=ta, tb=tb, resid=resid, scale=scale)


ROW_TILE = 256


def _rmsnorm_fwd(x, gain, *, name):
    T, D = x.shape

    def body(x_ref, g_ref, o_ref):
        xf = x_ref[...]
        r = lax.rsqrt(jnp.mean(xf * xf, axis=-1, keepdims=True) + RMS_EPS)
        o_ref[...] = (xf * r * g_ref[...]).astype(o_ref.dtype)

    return pl.pallas_call(
        body, name=name, grid=(T // ROW_TILE,),
        in_specs=[pl.BlockSpec((ROW_TILE, D), lambda i: (i, 0)), pl.BlockSpec((1, D), lambda i: (0, 0))],
        out_specs=pl.BlockSpec((ROW_TILE, D), lambda i: (i, 0)),
        out_shape=jax.ShapeDtypeStruct((T, D), BF16), compiler_params=_params(("parallel",)),
    )(x, gain)


def _rmsnorm_bwd(dh, x, gain, dres, *, name):
    T, D = x.shape

    def body(dh_ref, x_ref, g_ref, res_ref, dx_ref, dg_ref):
        xf = x_ref[...]
        r = lax.rsqrt(jnp.mean(xf * xf, axis=-1, keepdims=True) + RMS_EPS)
        y = xf * r
        dh_v = dh_ref[...].astype(F32)
        dy = dh_v * g_ref[...]
        dx_ref[...] = res_ref[...] + r * (dy - y * jnp.mean(dy * y, axis=-1, keepdims=True))

        @pl.when(pl.program_id(0) == 0)
        def _():
            dg_ref[...] = jnp.zeros_like(dg_ref)

        dg_ref[...] += jnp.sum(dh_v * y, axis=0, keepdims=True)

    row = pl.BlockSpec((ROW_TILE, D), lambda i: (i, 0))
    vec = pl.BlockSpec((1, D), lambda i: (0, 0))
    return pl.pallas_call(
        body, name=name, grid=(T // ROW_TILE,), in_specs=[row, row, vec, row], out_specs=(row, vec),
        out_shape=(jax.ShapeDtypeStruct((T, D), F32), jax.ShapeDtypeStruct((1, D), F32)),
        compiler_params=_params(("arbitrary",)),
    )(dh, x, gain, dres)


FF_HALF = N_DEV // 2


def _swiglu_fwd(p, *, name):
    _, T, fb = p.shape

    def body(g_ref, u_ref, o_ref):
        g = g_ref[...].astype(F32)
        o_ref[...] = (g * _sigmoid(g) * u_ref[...].astype(F32)).astype(o_ref.dtype)

    blk = (None, ROW_TILE, fb)
    return pl.pallas_call(
        body, name=name, grid=(T // ROW_TILE, FF_HALF),
        in_specs=[pl.BlockSpec(blk, lambda i, j: (j, i, 0)), pl.BlockSpec(blk, lambda i, j: (j + FF_HALF, i, 0))],
        out_specs=pl.BlockSpec(blk, lambda i, j: (j, i, 0)),
        out_shape=jax.ShapeDtypeStruct((FF_HALF, T, fb), BF16), compiler_params=_params(("parallel", "parallel")),
    )(p, p)


def _swiglu_bwd(da, p, *, name):
    _, T, fb = p.shape

    def body(da_ref, g_ref, u_ref, o_ref):
        g = g_ref[...].astype(F32)
        u = u_ref[...].astype(F32)
        d = da_ref[...].astype(F32)
        s = _sigmoid(g)
        o_ref[0] = (d * u * (s * (1.0 + g * (1.0 - s)))).astype(o_ref.dtype)
        o_ref[1] = (d * g * s).astype(o_ref.dtype)

    blk = (None, ROW_TILE, fb)
    out = pl.pallas_call(
        body, name=name, grid=(T // ROW_TILE, FF_HALF),
        in_specs=[pl.BlockSpec(blk, lambda i, j: (j, i, 0)), pl.BlockSpec(blk, lambda i, j: (j, i, 0)),
                  pl.BlockSpec(blk, lambda i, j: (j + FF_HALF, i, 0))],
        out_specs=pl.BlockSpec((2, None, ROW_TILE, fb), lambda i, j: (0, j, i, 0)),
        out_shape=jax.ShapeDtypeStruct((2, FF_HALF, T, fb), BF16), compiler_params=_params(("parallel", "parallel")),
    )(da, p, p)
    return out.reshape(2 * FF_HALF, T, fb)


COL_GATE_A = 7
COL_GATE_B = 8


def _merge_fwd(ya, yb, proj, *, name):
    T, D = ya.shape

    def body(ya_ref, yb_ref, ga_ref, gb_ref, o_ref):
        o_ref[...] = (_sigmoid(ga_ref[...]) * ya_ref[...] + _sigmoid(gb_ref[...]) * yb_ref[...]).astype(o_ref.dtype)

    row = pl.BlockSpec((ROW_TILE, D), lambda i: (i, 0))
    return pl.pallas_call(
        body, name=name, grid=(T // ROW_TILE,),
        in_specs=[row, row, pl.BlockSpec((ROW_TILE, D), lambda i: (i, COL_GATE_A)),
                  pl.BlockSpec((ROW_TILE, D), lambda i: (i, COL_GATE_B))],
        out_specs=row, out_shape=jax.ShapeDtypeStruct((T, D), BF16), compiler_params=_params(("parallel",)),
    )(ya, yb, proj, proj)


def _merge_bwd(dm, ya, yb, proj, *, name):
    T, D = ya.shape

    def body(dm_ref, ya_ref, yb_ref, ga_ref, gb_ref, dya_ref, dyb_ref, dga_ref, dgb_ref):
        d = dm_ref[...].astype(F32)
        sa = _sigmoid(ga_ref[...])
        sb = _sigmoid(gb_ref[...])
        dya_ref[...] = (d * sa).astype(BF16)
        dyb_ref[...] = (d * sb).astype(BF16)
        dga_ref[...] = (d * ya_ref[...] * sa * (1.0 - sa)).astype(BF16)
        dgb_ref[...] = (d * yb_ref[...] * sb * (1.0 - sb)).astype(BF16)

    row = pl.BlockSpec((ROW_TILE, D), lambda i: (i, 0))
    out = jax.ShapeDtypeStruct((T, D), BF16)
    return pl.pallas_call(
        body, name=name, grid=(T // ROW_TILE,),
        in_specs=[row, row, row, pl.BlockSpec((ROW_TILE, D), lambda i: (i, COL_GATE_A)),
                  pl.BlockSpec((ROW_TILE, D), lambda i: (i, COL_GATE_B))],
        out_specs=(row, row, row, row), out_shape=(out, out, out, out), compiler_params=_params(("parallel",)),
    )(dm, ya, yb, proj, proj)


def _loss_head(y, target, *, name):
    T, D = y.shape

    def body(y_ref, t_ref, loss_ref, dy_ref):
        err = y_ref[...] - t_ref[...]
        dy_ref[...] = err * (1.0 / D)

        @pl.when(pl.program_id(0) == 0)
        def _():
            loss_ref[...] = jnp.zeros_like(loss_ref)

        loss_ref[...] += 0.5 * jnp.sum(jnp.sum(err * err, axis=-1, keepdims=True) * (1.0 / D), axis=0, keepdims=True)

    row = pl.BlockSpec((ROW_TILE, D), lambda i: (i, 0))
    return pl.pallas_call(
        body, name=name, grid=(T // ROW_TILE,), in_specs=[row, row],
        out_specs=(pl.BlockSpec((1, 128), lambda i: (0, 0)), row),
        out_shape=(jax.ShapeDtypeStruct((1, 128), F32), jax.ShapeDtypeStruct((T, D), F32)),
        compiler_params=_params(("arbitrary",)),
    )(y, target)


CONV_PAD = 8


def _conv_taps(w, xp, T, first):
    acc = w[0:1, :] * xp[pl.ds(first, T), :]
    for i in range(1, DN_CONV):
        acc = acc + w[i:i + 1, :] * xp[pl.ds(first + i, T), :]
    return acc


def _conv_fwd(proj, conv_w, *, name):
    T = proj.shape[0]

    def body(x_ref, w_ref, o_ref, xp):
        xp[0:CONV_PAD, :] = jnp.zeros((CONV_PAD, HEAD_DIM), F32)
        xp[CONV_PAD:, :] = x_ref[...]
        y = _conv_taps(w_ref[...], xp, T, CONV_PAD - (DN_CONV - 1))
        s = y * _sigmoid(y)
        n = s * lax.rsqrt(jnp.sum(s * s, axis=-1, keepdims=True) + L2_EPS)
        o_ref[0] = jnp.where(pl.program_id(0) < 2, n, s)

    return pl.pallas_call(
        body, name=name, grid=(3, HEADS),
        in_specs=[pl.BlockSpec((T, HEAD_DIM), lambda c, h: (0, c * HEADS + h)),
                  pl.BlockSpec((DN_CONV, HEAD_DIM), lambda c, h: (0, c * HEADS + h))],
        out_specs=pl.BlockSpec((1, T, HEAD_DIM), lambda c, h: (c, 0, h)),
        out_shape=jax.ShapeDtypeStruct((3, T, D_MODEL), F32),
        scratch_shapes=[pltpu.VMEM((T + CONV_PAD, HEAD_DIM), F32)],
        compiler_params=_params(("parallel", "parallel")),
    )(proj, conv_w)


def _conv_bwd(dqkv, proj, conv_w, *, name):
    T = proj.shape[0]

    def body(d_ref, x_ref, w_ref, dx_ref, dw_ref, xp, dyp):
        xp[0:CONV_PAD, :] = jnp.zeros((CONV_PAD, HEAD_DIM), F32)
        xp[CONV_PAD:, :] = x_ref[...]
        w = w_ref[...]
        y = _conv_taps(w, xp, T, CONV_PAD - (DN_CONV - 1))
        sg = _sigmoid(y)
        s = y * sg
        r = lax.rsqrt(jnp.sum(s * s, axis=-1, keepdims=True) + L2_EPS)
        n = s * r
        d = d_ref[0]
        ds = jnp.where(pl.program_id(0) < 2, r * (d - n * jnp.sum(d * n, axis=-1, keepdims=True)), d)
        dy = ds * (sg * (1.0 + y * (1.0 - sg)))
        dyp[0:T, :] = dy
        dyp[T:, :] = jnp.zeros((CONV_PAD, HEAD_DIM), F32)
        dx = w[0:1, :] * dyp[pl.ds(DN_CONV - 1, T), :]
        for i in range(1, DN_CONV):
            dx = dx + w[i:i + 1, :] * dyp[pl.ds(DN_CONV - 1 - i, T), :]
        dx_ref[...] = dx.astype(dx_ref.dtype)
        for i in range(DN_CONV):
            dw_ref[i:i + 1, :] = jnp.sum(dy * xp[pl.ds(CONV_PAD - (DN_CONV - 1) + i, T), :], axis=0, keepdims=True)

    col = lambda c, h: (0, c * HEADS + h)
    return pl.pallas_call(
        body, name=name, grid=(3, HEADS),
        in_specs=[pl.BlockSpec((1, T, HEAD_DIM), lambda c, h: (c, 0, h)), pl.BlockSpec((T, HEAD_DIM), col),
                  pl.BlockSpec((DN_CONV, HEAD_DIM), col)],
        out_specs=(pl.BlockSpec((T, HEAD_DIM), col), pl.BlockSpec((DN_CONV, HEAD_DIM), col)),
        out_shape=(jax.ShapeDtypeStruct((T, 3 * D_MODEL), BF16), jax.ShapeDtypeStruct((DN_CONV, 3 * D_MODEL), F32)),
        scratch_shapes=[pltpu.VMEM((T + CONV_PAD, HEAD_DIM), F32), pltpu.VMEM((T + CONV_PAD, HEAD_DIM), F32)],
        compiler_params=_params(("parallel", "parallel")),
    )(dqkv, proj, conv_w)


def _inv_unit_lower(low, eye):
    x = eye - low
    power = _hdot(low, low, _B_NN)
    steps = int(math.log2(DN_CHUNK)) - 1
    for s in range(steps):
        x = x + _hdot(x, power, _B_NN)
        if s + 1 < steps:
            power = _hdot(power, power, _B_NN)
    return x


_B_NN = (((2,), (1,)), ((0,), (0,)))
_B_NT = (((2,), (2,)), ((0,), (0,)))
_B_TN = (((1,), (1,)), ((0,), (0,)))


def _dn_load(ref, lead, r0, group):
    rows = pl.ds(r0, group * DN_CHUNK)
    cols = lambda h: slice(h * HEAD_DIM, (h + 1) * HEAD_DIM)
    per_head = [(ref[rows, cols(h)] if lead is None else ref[lead, rows, cols(h)]).reshape(group, DN_CHUNK, HEAD_DIM)
                for h in range(DN_HEADS)]
    return jnp.stack(per_head, axis=1).reshape(group * DN_HEADS, DN_CHUNK, HEAD_DIM)


def _dn_chunk_setup(qkv_ref, b_ref, a_ref, hp_ref, n0, group):
    C = DN_CHUNK
    B = group * DN_HEADS
    r0 = pl.multiple_of(n0 * C, C)
    q = _dn_load(qkv_ref, 0, r0, group) * QK_SCALE
    k = _dn_load(qkv_ref, 1, r0, group)
    v = _dn_load(qkv_ref, 2, r0, group)
    ii = lax.broadcasted_iota(jnp.int32, (B, C, C), 1)
    jj = lax.broadcasted_iota(jnp.int32, (B, C, C), 2)
    eye_mask = ii == jj
    eye = jnp.where(eye_mask, 1.0, 0.0).astype(F32)

    def to_col(row):
        return jnp.sum(jnp.where(eye_mask, jnp.broadcast_to(row, (B, C, C)), 0.0), axis=2, keepdims=True)

    def to_row(col):
        return jnp.sum(jnp.where(eye_mask, jnp.broadcast_to(col, (B, C, C)), 0.0), axis=1, keepdims=True)

    def rows(ref):
        return jnp.stack([ref[h, pl.ds(n0, group)] for h in range(DN_HEADS)], axis=1).reshape(B, 1, C)

    def per_head(row):
        return jnp.stack([hp_ref[h, row:row + 1, 0:C] for h in range(DN_HEADS)] * group, axis=0)

    b_row = rows(b_ref)
    a_row = rows(a_ref)
    a_log = per_head(0)
    dt_b = per_head(1)
    beta_row = _sigmoid(b_row)
    neg_ea = -jnp.exp(a_log)
    g_row = neg_ea * _softplus(a_row + dt_b)
    gc_col = jnp.sum(jnp.where(jj <= ii, jnp.broadcast_to(g_row, (B, C, C)), 0.0), axis=2, keepdims=True)
    gc_row = to_row(gc_col)
    g_last = jnp.sum(g_row, axis=2, keepdims=True)
    beta = to_col(beta_row)
    low_incl = ii >= jj
    decay = jnp.exp(jnp.where(low_incl, gc_col - gc_row, -jnp.inf))
    eg = jnp.exp(gc_col)
    egl = jnp.exp(g_last - gc_col)
    el = jnp.exp(g_last)
    kb = k * beta
    pmat = _bdot(kb, k, _B_NT)
    low = jnp.where(ii > jj, pmat * decay, 0.0)
    tinv = _inv_unit_lower(low, eye)
    u = _hdot(tinv, v * beta, _B_NN)
    w = _hdot(tinv, kb * eg, _B_NN)
    qk = _bdot(q, k, _B_NT)
    attn = qk * decay
    return dict(q=q, k=k, v=v, ii=ii, jj=jj, to_col=to_col, to_row=to_row, b_row=b_row, a_row=a_row, dt_b=dt_b,
                beta_row=beta_row, neg_ea=neg_ea, g_row=g_row, gc_col=gc_col, g_last=g_last, beta=beta,
                decay=decay, eg=eg, egl=egl, el=el, kb=kb, pmat=pmat, tinv=tinv, u=u, w=w, qk=qk, attn=attn,
                qd=q * eg, kd=k * egl, r0=r0)


def _dn_store(ref, lead, r0, group, value):
    value = value.reshape(group, DN_HEADS, DN_CHUNK, HEAD_DIM)
    for h in range(DN_HEADS):
        block = value[:, h].reshape(group * DN_CHUNK, HEAD_DIM)
        if lead is None:
            ref[pl.ds(r0, group * DN_CHUNK), h * HEAD_DIM:(h + 1) * HEAD_DIM] = block
        else:
            ref[lead, pl.ds(r0, group * DN_CHUNK), h * HEAD_DIM:(h + 1) * HEAD_DIM] = block


def _dn_specs(T):
    nc = T // DN_CHUNK
    qkv = pl.BlockSpec((3, T, DN_HEADS * HEAD_DIM), lambda h: (0, 0, h))
    rows = pl.BlockSpec((DN_HEADS, nc, 1, DN_CHUNK), lambda h: (h, 0, 0, 0))
    hp = pl.BlockSpec((DN_HEADS, 8, 128), lambda h: (h, 0, 0))
    states = pl.BlockSpec((DN_HEADS, nc, HEAD_DIM, HEAD_DIM), lambda h: (h, 0, 0, 0))
    return nc, qkv, rows, hp, states


def _dn_fwd(qkv, b_rows, a_rows, hp, *, name):
    T = qkv.shape[1]
    nc, qkv_spec, row_spec, hp_spec, st_spec = _dn_specs(T)
    group = math.gcd(nc, DN_GROUP)
    H = DN_HEADS

    def body(qkv_ref, b_ref, a_ref, hp_ref, o_ref, st_ref, s_scr):
        s_scr[...] = jnp.zeros_like(s_scr)

        def step(t, carry):
            n0 = t * group
            c = _dn_chunk_setup(qkv_ref, b_ref, a_ref, hp_ref, n0, group)
            state = s_scr[...]
            outs = []
            for g in range(group):
                sl = slice(g * H, (g + 1) * H)
                for h in range(H):
                    st_ref[h, n0 + g] = state[h]
                v_new = c["u"][sl] - _bdot(c["w"][sl], state, _B_NN)
                outs.append(_bdot(c["qd"][sl], state, _B_NN) + _bdot(c["attn"][sl], v_new, _B_NN))
                state = state * c["el"][sl] + _bdot(c["kd"][sl], v_new, _B_TN)
            s_scr[...] = state
            _dn_store(o_ref, None, c["r0"], group, jnp.concatenate(outs, axis=0))
            return carry

        lax.fori_loop(0, nc // group, step, 0)

    return pl.pallas_call(
        body, name=name, grid=(HEADS // H,), in_specs=[qkv_spec, row_spec, row_spec, hp_spec],
        out_specs=(pl.BlockSpec((T, H * HEAD_DIM), lambda h: (0, h)), st_spec),
        out_shape=(jax.ShapeDtypeStruct((T, D_MODEL), F32),
                   jax.ShapeDtypeStruct((HEADS, nc, HEAD_DIM, HEAD_DIM), F32)),
        scratch_shapes=[pltpu.VMEM((H, HEAD_DIM, HEAD_DIM), F32)], compiler_params=_params(("parallel",)),
    )(qkv, b_rows, a_rows, hp)


def _dn_bwd(qkv, b_rows, a_rows, hp, states, do, *, name):
    T = qkv.shape[1]
    C = DN_CHUNK
    nc, qkv_spec, row_spec, hp_spec, st_spec = _dn_specs(T)
    group = math.gcd(nc, DN_GROUP)
    H = DN_HEADS
    B = group * H

    def body(qkv_ref, b_ref, a_ref, hp_ref, st_ref, do_ref, dqkv_ref, db_ref, da_ref, dhp_ref, ds_scr, acc_scr):
        ds_scr[...] = jnp.zeros_like(ds_scr)
        acc_scr[...] = jnp.zeros_like(acc_scr)

        def step(t, carry):
            n0 = nc - (t + 1) * group
            c = _dn_chunk_setup(qkv_ref, b_ref, a_ref, hp_ref, n0, group)
            state = jnp.stack([st_ref[h, pl.ds(n0, group)] for h in range(H)], axis=1).reshape(B, HEAD_DIM, HEAD_DIM)
            d_o = _dn_load(do_ref, None, c["r0"], group)
            v_new = c["u"] - _bdot(c["w"], state, _B_NN)
            d_vnew_local = _bdot(c["attn"], d_o, _B_TN)
            d_state_local = _bdot(c["qd"], d_o, _B_TN)
            d_state = ds_scr[...]
            d_vnew, d_kd, d_el = [None] * group, [None] * group, [None] * group
            for g in reversed(range(group)):
                sl = slice(g * H, (g + 1) * H)
                d_vnew[g] = d_vnew_local[sl] + _bdot(c["kd"][sl], d_state, _B_NN)
                d_kd[g] = _bdot(v_new[sl], d_state, _B_NT)
                d_el[g] = jnp.sum(jnp.sum(d_state * state[sl], axis=2, keepdims=True), axis=1, keepdims=True)
                d_state = d_state * c["el"][sl] + d_state_local[sl] - _bdot(c["w"][sl], d_vnew[g], _B_TN)
            ds_scr[...] = d_state
            chunk_grads(c, n0, state, d_o, v_new, jnp.concatenate(d_vnew, axis=0), jnp.concatenate(d_kd, axis=0),
                        jnp.concatenate(d_el, axis=0))
            return carry

        def chunk_grads(c, n0, state, d_o, v_new, d_vnew, d_kd, d_el):
            ii, jj = c["ii"], c["jj"]
            q, k, v, kb, beta = c["q"], c["k"], c["v"], c["kb"], c["beta"]
            decay, eg, egl, el = c["decay"], c["eg"], c["egl"], c["el"]
            u, w, tinv = c["u"], c["w"], c["tinv"]
            d_qd = _bdot(d_o, state, _B_NT)
            d_attn = _bdot(d_o, v_new, _B_NT)
            d_w = -_bdot(d_vnew, state, _B_NT)
            d_rv = _hdot(tinv, d_vnew, _B_TN)
            d_rw = _hdot(tinv, d_w, _B_TN)
            d_amat = -(_bdot(d_rv, u, _B_NT) + _bdot(d_rw, w, _B_NT))
            d_low = jnp.where(ii > jj, d_amat, 0.0)
            d_p = d_low * decay
            d_qk = d_attn * decay
            e_mat = (d_low * c["pmat"] + d_attn * c["qk"]) * decay
            d_q = _bdot(d_qk, k, _B_NN) + d_qd * eg
            d_kb = _bdot(d_p, k, _B_NN) + d_rw * eg
            d_k = _bdot(d_qk, q, _B_TN) + _bdot(d_p, kb, _B_TN) + d_kd * egl + d_kb * beta
            d_beta = jnp.sum(d_kb * k, axis=2, keepdims=True) + jnp.sum(d_rv * v, axis=2, keepdims=True)
            d_v = d_rv * beta
            d_eg = jnp.sum(d_qd * q, axis=2, keepdims=True) + jnp.sum(d_rw * kb, axis=2, keepdims=True)
            d_egl = jnp.sum(d_kd * k, axis=2, keepdims=True)
            d_glast = jnp.sum(d_egl * egl, axis=1, keepdims=True) + d_el * el
            row_sum = jnp.sum(e_mat, axis=2, keepdims=True)
            col_sum = c["to_col"](jnp.sum(e_mat, axis=1, keepdims=True))
            d_gc = row_sum - col_sum + d_eg * eg - d_egl * egl
            d_g_row = jnp.sum(jnp.where(ii >= jj, jnp.broadcast_to(d_gc, (B, C, C)), 0.0), axis=1, keepdims=True) + d_glast
            beta_row = c["beta_row"]
            d_b_row = c["to_row"](d_beta) * beta_row * (1.0 - beta_row)
            d_a_row = d_g_row * c["neg_ea"] * _sigmoid(c["a_row"] + c["dt_b"])
            _dn_store(dqkv_ref, 0, c["r0"], group, d_q * QK_SCALE)
            _dn_store(dqkv_ref, 1, c["r0"], group, d_k)
            _dn_store(dqkv_ref, 2, c["r0"], group, d_v)
            d_b_row = d_b_row.reshape(group, H, 1, C)
            d_a_row = d_a_row.reshape(group, H, 1, C)
            d_a_log = jnp.sum((d_g_row * c["g_row"]).reshape(group, H, 1, C), axis=0)
            d_dt_b = jnp.sum(d_a_row, axis=0)
            for h in range(H):
                db_ref[h, pl.ds(n0, group)] = d_b_row[:, h]
                da_ref[h, pl.ds(n0, group)] = d_a_row[:, h]
                acc_scr[h, 0:1, 0:C] += d_a_log[h]
                acc_scr[h, 1:2, 0:C] += d_dt_b[h]

        lax.fori_loop(0, nc // group, step, 0)
        for h in range(H):
            tot = jnp.sum(acc_scr[h], axis=1, keepdims=True)
            dhp_ref[h] = jnp.broadcast_to(tot, (8, 128))

    return pl.pallas_call(
        body, name=name, grid=(HEADS // H,),
        in_specs=[qkv_spec, row_spec, row_spec, hp_spec, st_spec, pl.BlockSpec((T, H * HEAD_DIM), lambda h: (0, h))],
        out_specs=(qkv_spec, row_spec, row_spec, hp_spec),
        out_shape=(jax.ShapeDtypeStruct((3, T, D_MODEL), F32), jax.ShapeDtypeStruct((HEADS, nc, 1, C), F32),
                   jax.ShapeDtypeStruct((HEADS, nc, 1, C), F32), jax.ShapeDtypeStruct((HEADS, 8, 128), F32)),
        scratch_shapes=[pltpu.VMEM((H, HEAD_DIM, HEAD_DIM), F32), pltpu.VMEM((H, 8, 128), F32)],
        compiler_params=_params(("parallel",)),
    )(qkv, b_rows, a_rows, hp, states, do)


COL_Z = 3 * HEADS


def _gated_norm_fwd(o, proj, gain, *, name):
    T = o.shape[0]

    def body(o_ref, z_ref, g_ref, out_ref):
        x = o_ref[...]
        r = lax.rsqrt(jnp.mean(x * x, axis=-1, keepdims=True) + RMS_EPS)
        z = z_ref[...]
        out_ref[...] = (x * r * g_ref[...] * (z * _sigmoid(z))).astype(out_ref.dtype)

    return pl.pallas_call(
        body, name=name, grid=(HEADS,),
        in_specs=[pl.BlockSpec((T, HEAD_DIM), lambda h: (0, h)), pl.BlockSpec((T, HEAD_DIM), lambda h: (0, COL_Z + h)),
                  pl.BlockSpec((1, HEAD_DIM), lambda h: (0, 0))],
        out_specs=pl.BlockSpec((T, HEAD_DIM), lambda h: (0, h)),
        out_shape=jax.ShapeDtypeStruct((T, D_MODEL), BF16), compiler_params=_params(("parallel",)),
    )(o, proj, gain)


def _gated_norm_bwd(dout, o, proj, gain, *, name):
    T = o.shape[0]

    def body(d_ref, o_ref, z_ref, g_ref, do_ref, dz_ref, dg_ref):
        x = o_ref[...]
        r = lax.rsqrt(jnp.mean(x * x, axis=-1, keepdims=True) + RMS_EPS)
        n = x * r
        z = z_ref[...]
        sg = _sigmoid(z)
        d = d_ref[...].astype(F32)
        g = g_ref[...]
        dz_ref[...] = (d * n * g * (sg * (1.0 + z * (1.0 - sg)))).astype(dz_ref.dtype)
        dy = d * (z * sg)
        dyg = dy * g
        do_ref[...] = r * (dyg - n * jnp.mean(dyg * n, axis=-1, keepdims=True))

        @pl.when(pl.program_id(0) == 0)
        def _():
            dg_ref[...] = jnp.zeros_like(dg_ref)

        dg_ref[...] += jnp.sum(dy * n, axis=0, keepdims=True)

    head = pl.BlockSpec((T, HEAD_DIM), lambda h: (0, h))
    vec = pl.BlockSpec((1, HEAD_DIM), lambda h: (0, 0))
    return pl.pallas_call(
        body, name=name, grid=(HEADS,),
        in_specs=[head, head, pl.BlockSpec((T, HEAD_DIM), lambda h: (0, COL_Z + h)), vec],
        out_specs=(head, head, vec),
        out_shape=(jax.ShapeDtypeStruct((T, D_MODEL), F32), jax.ShapeDtypeStruct((T, D_MODEL), BF16),
                   jax.ShapeDtypeStruct((1, HEAD_DIM), F32)),
        compiler_params=_params(("arbitrary",)),
    )(dout, o, proj, gain)


COL_SBQ = 4 * HEADS
COL_SBK = 5 * HEADS
COL_SBV = 6 * HEADS


def _split_dot(x, mat):
    lead = x.shape[:-1]
    x = x.reshape(-1, x.shape[-1])
    hi = x.astype(BF16)
    lo = (x - hi.astype(F32)).astype(BF16)
    out = jnp.dot(hi, mat, preferred_element_type=F32) + jnp.dot(lo, mat, preferred_element_type=F32)
    return out.reshape(lead + (mat.shape[-1],))


def _sb_specs(T, heads):
    col = lambda first: pl.BlockSpec((T, heads * HEAD_DIM), lambda h: (0, first // heads + h))
    return col(COL_SBQ), col(COL_SBK), col(COL_SBV), pl.BlockSpec((1, HEAD_DIM), lambda h: (0, 0))


def _heads_first(x):
    return jnp.stack([x[:, c:c + HEAD_DIM] for c in range(0, x.shape[1], HEAD_DIM)], axis=0)


def _heads_last(x):
    return jnp.concatenate([x[h] for h in range(x.shape[0])], axis=1)


def _head_rms(x):
    r = lax.rsqrt(jnp.mean(x * x, axis=-1, keepdims=True) + RMS_EPS)
    return x * r, r


def _sb_fwd(proj, q_gain, k_gain, *, name):
    T = proj.shape[0]
    B = SB_BLOCK
    H = SB_HEADS
    nb = T // B
    KT = min(SB_KEY_TILE, T)
    NS = KT // B
    q_spec, k_spec, v_spec, g_spec = _sb_specs(T, H)

    def body(q_ref, k_ref, v_ref, gq_ref, gk_ref, o_ref, lt_ref, qs, ks, vs):
        qs[...] = (_head_rms(_heads_first(q_ref[...]))[0] * (gq_ref[...] * QK_SCALE)).astype(BF16)
        ks[...] = (_head_rms(_heads_first(k_ref[...]))[0] * gk_ref[...]).astype(BF16)
        vs[...] = _heads_first(v_ref[...]).astype(BF16)
        ii = lax.broadcasted_iota(jnp.int32, (B, B), 0)
        jj = lax.broadcasted_iota(jnp.int32, (B, B), 1)
        after = jnp.where(ii > jj, 1.0, 0.0).astype(BF16)
        ahead = lax.broadcasted_iota(jnp.int32, (H, B, KT), 2) - lax.broadcasted_iota(jnp.int32, (H, B, KT), 1)

        def q_block(i, carry):
            rows = pl.ds(pl.multiple_of(i * B, B), B)
            q = qs[:, rows, :]

            def tile(c0, acc, tail, masked):
                cols = pl.ds(c0, KT)
                z = lax.dot_general(q, ks[:, cols, :], _B_NT, preferred_element_type=F32)
                sp = _softplus(z)
                causal = ahead < (i * B - c0)
                log_1mb = jnp.where(causal, -sp, 0.0) if masked else -sp
                parts = [None] * NS
                for b in reversed(range(NS)):
                    blk = log_1mb[:, :, b * B:(b + 1) * B]
                    parts[b] = _split_dot(blk, after) + tail
                    tail = tail + jnp.sum(blk, axis=2, keepdims=True)
                survive = parts[0] if NS == 1 else jnp.concatenate(parts, axis=2)
                wts = jnp.exp(z - sp + survive)
                if masked:
                    wts = jnp.where(causal, wts, 0.0)
                acc = acc + lax.dot_general(wts.astype(BF16), vs[:, cols, :], _B_NN, preferred_element_type=F32)
                return acc, tail

            last = i // NS
            acc, tail = tile(pl.multiple_of(last * KT, KT), jnp.zeros((H, B, HEAD_DIM), F32), jnp.zeros((H, B, 1), F32), True)
            acc, tail = lax.fori_loop(
                1, last + 1, lambda s, c: tile(pl.multiple_of((last - s) * KT, KT), c[0], c[1], False), (acc, tail))
            o_ref[rows, :] = _heads_last(acc).astype(o_ref.dtype)
            lt_ref[rows, :] = _heads_last(jnp.broadcast_to(tail, (H, B, HEAD_DIM)))
            return carry

        lax.fori_loop(0, nb, q_block, 0)

    heads = pl.BlockSpec((T, H * HEAD_DIM), lambda h: (0, h))
    return pl.pallas_call(
        body, name=name, grid=(HEADS // H,), in_specs=[q_spec, k_spec, v_spec, g_spec, g_spec],
        out_specs=(heads, heads),
        out_shape=(jax.ShapeDtypeStruct((T, D_MODEL), BF16), jax.ShapeDtypeStruct((T, D_MODEL), F32)),
        scratch_shapes=[pltpu.VMEM((H, T, HEAD_DIM), BF16)] * 3, compiler_params=_params(("parallel",)),
    )(proj, proj, proj, q_gain, k_gain)


def _sb_bwd(proj, q_gain, k_gain, ltot, do, *, name):
    T = proj.shape[0]
    B = SB_BLOCK
    H = SB_HEADS_BWD
    nb = T // B
    KT = min(SB_KEY_TILE, T)
    NS = KT // B
    q_spec, k_spec, v_spec, g_spec = _sb_specs(T, H)

    def body(q_ref, k_ref, v_ref, gq_ref, gk_ref, lt_ref, do_ref, dq_ref, dk_ref, dv_ref, dgq_ref, dgk_ref,
             qs, ks, vs, dos, lts, dq_acc, dk_acc, dv_acc):
        qn, q_r = _head_rms(_heads_first(q_ref[...]))
        kn, k_r = _head_rms(_heads_first(k_ref[...]))
        qs[...] = (qn * (gq_ref[...] * QK_SCALE)).astype(BF16)
        ks[...] = (kn * gk_ref[...]).astype(BF16)
        vs[...] = _heads_first(v_ref[...]).astype(BF16)
        dos[...] = _heads_first(do_ref[...]).astype(BF16)
        lts[...] = _heads_first(lt_ref[...])
        dk_acc[...] = jnp.zeros_like(dk_acc)
        dv_acc[...] = jnp.zeros_like(dv_acc)
        ii = lax.broadcasted_iota(jnp.int32, (B, B), 0)
        jj = lax.broadcasted_iota(jnp.int32, (B, B), 1)
        upto = jnp.where(ii <= jj, 1.0, 0.0).astype(BF16)
        before = jnp.where(ii < jj, 1.0, 0.0).astype(BF16)
        ahead = lax.broadcasted_iota(jnp.int32, (H, B, KT), 2) - lax.broadcasted_iota(jnp.int32, (H, B, KT), 1)

        def q_block(i, carry):
            rows = pl.ds(pl.multiple_of(i * B, B), B)
            q = qs[:, rows, :]
            d_o = dos[:, rows, :]
            total = jnp.max(lts[:, rows, :], axis=2, keepdims=True)

            def tile(c0, dq, head_lb, head_de, masked):
                cols = pl.ds(c0, KT)
                k = ks[:, cols, :]
                v = vs[:, cols, :]
                z = lax.dot_general(q, k, _B_NT, preferred_element_type=F32)
                sp = _softplus(z)
                causal = ahead < (i * B - c0)
                log_1mb = jnp.where(causal, -sp, 0.0) if masked else -sp
                parts = [None] * NS
                for b in range(NS):
                    blk = log_1mb[:, :, b * B:(b + 1) * B]
                    parts[b] = _split_dot(blk, upto) + head_lb
                    head_lb = head_lb + jnp.sum(blk, axis=2, keepdims=True)
                prefix = parts[0] if NS == 1 else jnp.concatenate(parts, axis=2)
                wts = jnp.exp(z - sp + (total - prefix))
                if masked:
                    wts = jnp.where(causal, wts, 0.0)
                d_w = lax.dot_general(d_o, v, _B_NT, preferred_element_type=F32)
                d_e = wts * d_w
                for b in range(NS):
                    blk = d_e[:, :, b * B:(b + 1) * B]
                    parts[b] = _split_dot(blk, before) + head_de
                    head_de = head_de + jnp.sum(blk, axis=2, keepdims=True)
                cum = parts[0] if NS == 1 else jnp.concatenate(parts, axis=2)
                sig = jnp.exp(z - sp)
                d_z = d_e * (1.0 - sig) - sig * cum
                if masked:
                    d_z = jnp.where(causal, d_z, 0.0)
                d_zb = d_z.astype(BF16)
                dq = dq + lax.dot_general(d_zb, k, _B_NN, preferred_element_type=F32)
                dk_acc[:, cols, :] += lax.dot_general(d_zb, q, _B_TN, preferred_element_type=F32)
                dv_acc[:, cols, :] += lax.dot_general(wts.astype(BF16), d_o, _B_TN, preferred_element_type=F32)
                return dq, head_lb, head_de

            last = i // NS
            zero = jnp.zeros((H, B, 1), F32)
            state = lax.fori_loop(0, last, lambda t, c: tile(pl.multiple_of(t * KT, KT), *c, False),
                                  (jnp.zeros((H, B, HEAD_DIM), F32), zero, zero))
            dq, _, _ = tile(pl.multiple_of(last * KT, KT), *state, True)
            dq_acc[:, rows, :] = dq * QK_SCALE
            return carry

        lax.fori_loop(0, nb, q_block, 0)

        def norm_bwd(d_scaled, n, r, gain):
            dn = d_scaled * gain
            d_gain = jnp.sum(jnp.sum(d_scaled * n, axis=1, keepdims=True), axis=0)
            return r * (dn - n * jnp.mean(dn * n, axis=-1, keepdims=True)), d_gain

        dq_raw, dgq = norm_bwd(dq_acc[...], qn, q_r, gq_ref[...])
        dk_raw, dgk = norm_bwd(dk_acc[...], kn, k_r, gk_ref[...])
        dq_ref[...] = _heads_last(dq_raw).astype(dq_ref.dtype)
        dk_ref[...] = _heads_last(dk_raw).astype(dk_ref.dtype)
        dv_ref[...] = _heads_last(dv_acc[...]).astype(dv_ref.dtype)

        @pl.when(pl.program_id(0) == 0)
        def _():
            dgq_ref[...] = jnp.zeros_like(dgq_ref)
            dgk_ref[...] = jnp.zeros_like(dgk_ref)

        dgq_ref[...] += dgq
        dgk_ref[...] += dgk

    heads = pl.BlockSpec((T, H * HEAD_DIM), lambda h: (0, h))
    out = jax.ShapeDtypeStruct((T, D_MODEL), BF16)
    vec = jax.ShapeDtypeStruct((1, HEAD_DIM), F32)
    return pl.pallas_call(
        body, name=name, grid=(HEADS // H,), in_specs=[q_spec, k_spec, v_spec, g_spec, g_spec, heads, heads],
        out_specs=(heads, heads, heads, g_spec, g_spec), out_shape=(out, out, out, vec, vec),
        scratch_shapes=[pltpu.VMEM((H, T, HEAD_DIM), BF16)] * 4 + [pltpu.VMEM((H, T, HEAD_DIM), F32)] * 4,
        compiler_params=_params(("arbitrary",)),
    )(proj, proj, proj, q_gain, k_gain, ltot, do)


ADAM_ROWS = 128


def _adamw(g_parts, w, m, v, *, name, layer=0, earlier=None):
    K, A, C = g_parts.shape
    R = w.shape[0]
    tr = ADAM_ROWS if A % ADAM_ROWS == 0 else (A // 2 if A % 32 == 0 else A)
    first_block = layer * (A // tr)

    def body(g_ref, w_ref, m_ref, v_ref, *rest):
        go_ref, d_ref, mo_ref, vo_ref = rest[-4:]
        g = g_ref[0].astype(F32)
        for k in range(1, K):
            g = g + g_ref[k].astype(F32)
        go_ref[...] = g
        m_new = ADAM_B1 * m_ref[...] + (1.0 - ADAM_B1) * g
        v_new = ADAM_B2 * v_ref[...] + (1.0 - ADAM_B2) * (g * g)
        m_hat = m_new / (1.0 - ADAM_B1 ** ADAM_STEP)
        v_hat = v_new / (1.0 - ADAM_B2 ** ADAM_STEP)
        d_ref[...] = -ADAM_LR * (m_hat / (jnp.sqrt(v_hat) + ADAM_EPS) + ADAM_WD * w_ref[...])
        mo_ref[...] = m_new
        vo_ref[...] = v_new

    row = pl.BlockSpec((tr, C), lambda i: (first_block + i, 0))
    out = jax.ShapeDtypeStruct((R, C), F32)
    in_specs = [pl.BlockSpec((K, tr, C), lambda i: (0, i, 0)), row, row, row]
    if earlier is None:
        return pl.pallas_call(
            body, name=name, grid=(A // tr,), in_specs=in_specs, out_specs=(row, row, row, row),
            out_shape=(out, out, out, out), compiler_params=_params(("parallel",)),
        )(g_parts, w, m, v)
    return pl.pallas_call(
        body, name=name, grid=(A // tr,), in_specs=in_specs + [ANY] * 4, out_specs=(row, row, row, row),
        out_shape=(out, out, out, out), input_output_aliases={4 + j: j for j in range(4)},
        compiler_params=_params(("parallel",)),
    )(g_parts, w, m, v, *earlier)


def _sum_parts(parts, *, name):
    K, R, C = parts.shape

    def body(p_ref, o_ref):
        acc = p_ref[0]
        for k in range(1, K):
            acc = acc + p_ref[k]
        o_ref[...] = acc

    return pl.pallas_call(body, name=name, out_shape=jax.ShapeDtypeStruct((R, C), F32))(parts)


def _position():
    return lax.axis_index("x"), lax.axis_index("y"), lax.axis_index("c")


def _all_gather(shards, *, name):
    n = len(shards)

    def body(*refs):
        x_refs, out_refs = refs[:n], refs[n:2 * n]
        send_sems, recv_sems, local_sems = refs[2 * n:]
        x, y, c = _position()
        me, sibling = (x, y, c), (x, y, 1 - c)
        chips = [(1 - x, y), (x, 1 - y), (1 - x, 1 - y)]

        def slot(a, px, py, pc):
            return out_refs[a].at[4 * px + 2 * py + pc]

        def copy(a, k, block, to, own=False):
            return pltpu.make_async_remote_copy(
                src_ref=x_refs[a] if own else slot(a, *block), dst_ref=slot(a, *block),
                send_sem=send_sems.at[a, k], recv_sem=recv_sems.at[a, k], device_id=to, device_id_type=MESH)

        mine = [pltpu.make_async_copy(x_refs[a], slot(a, *me), local_sems.at[a]) for a in range(n)]
        for cp in mine:
            cp.start()
        first = [copy(a, 1 + j, me, (*chip, c), own=True) for j, chip in enumerate(chips) for a in range(n)]
        first += [copy(a, 0, me, sibling, own=True) for a in range(n)]
        for cp in first:
            cp.start()
        passed = []
        for j, chip in enumerate(chips):
            for a in range(n):
                copy(a, 1 + j, (*chip, c), me).wait_recv()
                passed.append(copy(a, 4 + j, (*chip, c), sibling))
                passed[-1].start()
        for a in range(n):
            copy(a, 0, sibling, me).wait_recv()
        for j, chip in enumerate(chips):
            for a in range(n):
                copy(a, 4 + j, (*chip, 1 - c), me).wait_recv()
        for cp in first + passed:
            cp.wait_send()
        for cp in mine:
            cp.wait()

    return pl.pallas_call(
        body, name=name, in_specs=[ANY] * n, out_specs=[ANY] * n,
        out_shape=[jax.ShapeDtypeStruct((N_DEV,) + s.shape, s.dtype) for s in shards],
        scratch_shapes=[pltpu.SemaphoreType.DMA((n, 7)), pltpu.SemaphoreType.DMA((n, 7)), pltpu.SemaphoreType.DMA((n,))],
    )(*shards)


HBM = pl.BlockSpec(memory_space=pltpu.HBM)
SEM = pl.BlockSpec(memory_space=pltpu.SEMAPHORE)
DATAFLOW = pltpu.SideEffectType.DATAFLOW_SIDE_EFFECTING


def _exchange_copies(gather, x_refs, land_refs, send_sems, recv_sems, local_sems):
    n = len(x_refs)
    x, y, c = _position()
    me = 4 * x + 2 * y + c

    def src(a, slot):
        return x_refs[a] if gather else x_refs[a].at[slot]

    mine = [pltpu.make_async_copy(src(a, me), land_refs[a].at[me], local_sems.at[a]) for a in range(n)]
    sends, recvs = [], []
    for k in range(1, N_DEV):
        px, py, pc = (x + (k >> 2)) % 2, (y + ((k >> 1) & 1)) % 2, (c + (k & 1)) % 2
        peer = 4 * px + 2 * py + pc
        for a in range(n):
            sems = dict(send_sem=send_sems.at[7 * a + k - 1], recv_sem=recv_sems.at[7 * a + k - 1],
                        device_id=(px, py, pc), device_id_type=MESH)
            sends.append(pltpu.make_async_remote_copy(src_ref=src(a, peer), dst_ref=land_refs[a].at[me], **sems))
            recvs.append(pltpu.make_async_remote_copy(src_ref=src(a, me), dst_ref=land_refs[a].at[peer], **sems))
    return mine, sends, recvs


def _exchange_start(parts, *, gather, name):
    n = len(parts)

    def body(*refs):
        x_refs, land_refs = refs[:n], refs[n:2 * n]
        send_sems, recv_sems, local_sems = refs[2 * n:2 * n + 3]
        token = refs[-1]
        mine, sends, _ = _exchange_copies(gather, x_refs, land_refs, send_sems, recv_sems, local_sems)
        for cp in mine + sends:
            cp.start()
        token[...] = jnp.zeros_like(token)

    sems = (pltpu.SemaphoreType.DMA((7 * n,)), pltpu.SemaphoreType.DMA((7 * n,)), pltpu.SemaphoreType.DMA((n,)))
    thru = tuple(pltpu.HBM(p.shape, p.dtype) for p in parts)
    land = tuple(pltpu.HBM(((N_DEV,) if gather else ()) + p.shape, p.dtype) for p in parts)
    res = pl.pallas_call(
        body, name=name, in_specs=[HBM] * (2 * n),
        out_specs=(SEM, SEM, SEM) + (HBM,) * (2 * n) + (pl.BlockSpec(memory_space=pltpu.VMEM),),
        out_shape=sems + thru + land + (jax.ShapeDtypeStruct((8, 128), F32),),
        input_output_aliases={a: 3 + a for a in range(2 * n)},
        compiler_params=pltpu.CompilerParams(has_side_effects=DATAFLOW),
    )(*[pltpu.with_memory_space_constraint(p, pltpu.HBM) for p in parts],
      *[pltpu.with_memory_space_constraint(lax.empty(z.shape, z.dtype), pltpu.HBM) for z in land])
    return res[:3], res[3:3 + n], res[3 + n:3 + 2 * n], res[-1]


def _exchange_wait(sems, parts, landing, after, *, gather, name):
    n = len(parts)
    after = list(after)

    def body(*refs):
        x_refs, land_refs = refs[:n], refs[n:2 * n]
        send_sems, recv_sems, local_sems = refs[2 * n:2 * n + 3]
        token = refs[-1]
        mine, sends, recvs = _exchange_copies(gather, x_refs, land_refs, send_sems, recv_sems, local_sems)
        for cp in recvs:
            cp.wait_recv()
        for cp in sends:
            cp.wait_send()
        for cp in mine:
            cp.wait()
        token[...] = jnp.zeros_like(token)

    thru = tuple(pltpu.HBM(p.shape, p.dtype) for p in tuple(parts) + tuple(landing))
    res = pl.pallas_call(
        body, name=name, in_specs=[HBM] * (2 * n) + [SEM, SEM, SEM] + [ANY] * len(after),
        out_specs=(HBM,) * (2 * n) + (pl.BlockSpec(memory_space=pltpu.VMEM),),
        out_shape=thru + (jax.ShapeDtypeStruct((8, 128), F32),), input_output_aliases={a: a for a in range(2 * n)},
        compiler_params=pltpu.CompilerParams(has_side_effects=DATAFLOW),
    )(*parts, *landing, *sems, *after)
    return res[n:2 * n], res[-1]


def _ffn_fwd(x, gain, wg_in, wg_out, tag):
    T, D = x.shape
    fb, rb = wg_in.shape[-1], wg_out.shape[-2]
    tm, tn = min(T, 1024), 512
    h = _rmsnorm_fwd(x, gain, name=f"{tag}_norm")
    p = _mm(name=f"{tag}_in", grid=(T // tm, N_DEV, 1), tile=(tm, fb),
            a=h, a_spec=pl.BlockSpec((tm, D), lambda i, j, k: (i, 0)),
            b=wg_in, b_spec=pl.BlockSpec((None, D, fb), lambda i, j, k: (j, 0, 0)),
            out_shape=jax.ShapeDtypeStruct((N_DEV, T, fb), BF16), o_spec=pl.BlockSpec((None, tm, fb), lambda i, j, k: (j, i, 0)))
    a = _swiglu_fwd(p, name=f"{tag}_act")
    y = _mm(name=f"{tag}_out", grid=(T // tm, 1, FF_HALF), tile=(tm, D), resid=x, scale=0.5,
            a=a, a_spec=pl.BlockSpec((None, tm, fb), lambda i, j, k: (k, i, 0)),
            b=wg_out.reshape(N_DEV * rb, D), b_spec=pl.BlockSpec((fb, D), lambda i, j, k: (k, 0)),
            out_shape=jax.ShapeDtypeStruct((T, D), F32), o_spec=pl.BlockSpec((tm, D), lambda i, j, k: (i, 0)))
    return y, (x, h, p, a)


def _ffn_bwd(dy, saved, gain, wg_in, wg_out, tag, on_weight_grads=None):
    x, h, p, a = saved
    T, D = x.shape
    fb, rb = wg_in.shape[-1], wg_out.shape[-2]
    tm, tn = min(T, 1024), 512
    da = _mm(name=f"{tag}_out_dx", grid=(T // tm, FF_HALF, 1), tile=(tm, fb), tb=True, scale=0.5,
             a=dy, a_spec=pl.BlockSpec((tm, D), lambda i, j, k: (i, 0)),
             b=wg_out.reshape(N_DEV * rb, D), b_spec=pl.BlockSpec((fb, D), lambda i, j, k: (j, 0)),
             out_shape=jax.ShapeDtypeStruct((FF_HALF, T, fb), BF16), o_spec=pl.BlockSpec((None, tm, fb), lambda i, j, k: (j, i, 0)))
    d_w_out = _mm(name=f"{tag}_out_dw", grid=(FF_HALF, D // tn, 1), tile=(fb, tn), ta=True, scale=0.5,
                  a=a, a_spec=pl.BlockSpec((None, T, fb), lambda i, j, k: (i, 0, 0)),
                  b=dy, b_spec=pl.BlockSpec((T, tn), lambda i, j, k: (0, j)),
                  out_shape=jax.ShapeDtypeStruct((FF_HALF, fb, D), BF16), o_spec=pl.BlockSpec((None, fb, tn), lambda i, j, k: (i, 0, j)))
    dp = _swiglu_bwd(da, p, name=f"{tag}_act_bwd")
    d_w_in = _mm(name=f"{tag}_in_dw", grid=(1, N_DEV, 1), tile=(D, fb), ta=True,
                 a=h, a_spec=pl.BlockSpec((T, D), lambda i, j, k: (0, 0)),
                 b=dp, b_spec=pl.BlockSpec((None, T, fb), lambda i, j, k: (j, 0, 0)),
                 out_shape=jax.ShapeDtypeStruct((N_DEV, D, fb), BF16), o_spec=pl.BlockSpec((None, D, fb), lambda i, j, k: (j, 0, 0)))
    dh = _mm(name=f"{tag}_in_dx", grid=(T // tm, 1, N_DEV), tile=(tm, D), tb=True,
             a=dp, a_spec=pl.BlockSpec((None, tm, fb), lambda i, j, k: (k, i, 0)),
             b=wg_in, b_spec=pl.BlockSpec((None, D, fb), lambda i, j, k: (k, 0, 0)),
             out_shape=jax.ShapeDtypeStruct((T, D), F32), o_spec=pl.BlockSpec((tm, D), lambda i, j, k: (i, 0)))
    d_w_out = d_w_out.reshape(N_DEV, rb, D)
    if on_weight_grads is not None:
        gain = gain + on_weight_grads(d_w_in, d_w_out)[0, 0]
    dx, d_gain = _rmsnorm_bwd(dh, x, gain, dy, name=f"{tag}_norm_bwd")
    return dx, d_gain, d_w_in, d_w_out


def _square_mm(a, wg, *, name, transposed=False, out_dtype=F32, resid=None):
    T, D = a.shape
    w = wg.reshape(D, D)
    return _matmul(a, w, tb=transposed, name=name, out_dtype=out_dtype, resid=resid)


def _head_rows(cols, T):
    return cols.T.reshape(HEADS, T // DN_CHUNK, 1, DN_CHUNK)


def _mixer_fwd(x, w, big, tag):
    T = x.shape[0]
    h = _rmsnorm_fwd(x, w["mix_norm"], name=f"{tag}_norm")
    proj = _matmul(h, big["w_main"], name=f"{tag}_proj")
    scal = _matmul(h, big["w_scal"], name=f"{tag}_proj_scal", tn=N_SCAL)
    qkv = _conv_fwd(proj, big["conv_w"], name=f"{tag}_conv")
    b_rows = _head_rows(scal[:, 0:HEADS], T)
    a_rows = _head_rows(scal[:, HEADS:2 * HEADS], T)
    o_a, states = _dn_fwd(qkv, b_rows, a_rows, w["hp"], name=f"{tag}_dn")
    oa_n = _gated_norm_fwd(o_a, proj, w["dn_out_norm"], name=f"{tag}_dn_norm")
    ya = _square_mm(oa_n, big["w_branch_a"], name=f"{tag}_branch_a")
    o_b, ltot = _sb_fwd(proj, w["sb_q_norm"], w["sb_k_norm"], name=f"{tag}_sb")
    yb = _square_mm(o_b, big["w_branch_b"], name=f"{tag}_branch_b")
    merged = _merge_fwd(ya, yb, proj, name=f"{tag}_merge")
    y = _square_mm(merged, big["w_out"], name=f"{tag}_out", resid=x)
    return y, (x, h, proj, qkv, b_rows, a_rows, o_a, states, oa_n, ya, o_b, ltot, yb, merged)


def _mixer_bwd(dy, saved, w, big, tag, on_weight_grads):
    x, h, proj, qkv, b_rows, a_rows, o_a, states, oa_n, ya, o_b, ltot, yb, merged = saved
    T = x.shape[0]
    g = {}
    d_merged = _square_mm(dy, big["w_out"], transposed=True, name=f"{tag}_out_dx", out_dtype=BF16)
    g["w_out"] = _matmul(merged, dy, ta=True, name=f"{tag}_out_dw", out_dtype=BF16)
    d_ya, d_yb, d_ga, d_gb = _merge_bwd(d_merged, ya, yb, proj, name=f"{tag}_merge_bwd")
    d_oan = _square_mm(d_ya, big["w_branch_a"], transposed=True, name=f"{tag}_branch_a_dx")
    g["w_branch_a"] = _matmul(oa_n, d_ya, ta=True, name=f"{tag}_branch_a_dw", out_dtype=BF16)
    d_ob = _square_mm(d_yb, big["w_branch_b"], transposed=True, name=f"{tag}_branch_b_dx")
    g["w_branch_b"] = _matmul(o_b, d_yb, ta=True, name=f"{tag}_branch_b_dw", out_dtype=BF16)
    d_oa, d_z, g["dn_out_norm"] = _gated_norm_bwd(d_oan, o_a, proj, w["dn_out_norm"], name=f"{tag}_dn_norm_bwd")
    d_qkv, d_b_rows, d_a_rows, d_hp = _dn_bwd(qkv, b_rows, a_rows, w["hp"], states, d_oa, name=f"{tag}_dn_bwd")
    g["dn_a_log"] = d_hp[:, 0, 0]
    g["dn_dt_bias"] = d_hp[:, 1, 0]
    d_conv_in, g["conv_w"] = _conv_bwd(d_qkv, proj, big["conv_w"], name=f"{tag}_conv_bwd")
    d_sbq, d_sbk, d_sbv, g["sb_q_norm"], g["sb_k_norm"] = _sb_bwd(
        proj, w["sb_q_norm"], w["sb_k_norm"], ltot, d_ob, name=f"{tag}_sb_bwd")
    d_proj = jnp.concatenate([d_conv_in, d_z, d_sbq, d_sbk, d_sbv, d_ga, d_gb], axis=1)
    d_scal = jnp.concatenate([d_b_rows.reshape(HEADS, T).T, d_a_rows.reshape(HEADS, T).T,
                              jnp.zeros((T, N_SCAL - 2 * HEADS), F32)], axis=1).astype(BF16)
    g["w_main"] = _matmul(h, d_proj, ta=True, name=f"{tag}_proj_dw", out_dtype=BF16)
    g["w_scal"] = _matmul(h, d_scal, ta=True, name=f"{tag}_proj_scal_dw", out_dtype=BF16, tn=N_SCAL)
    dh_scal = _matmul(d_scal, big["w_scal"], tb=True, name=f"{tag}_proj_scal_dx")
    dh = _matmul(d_proj, big["w_main"], tb=True, name=f"{tag}_proj_dx", tk=N_MAIN // 4, resid=dh_scal)
    gain = w["mix_norm"] + on_weight_grads(g)[0, 0]
    dx, g["mix_norm"] = _rmsnorm_bwd(dh, x, gain, dy, name=f"{tag}_norm_bwd")
    return dx, g


def _local_step(x, target, layers, weights_of, on_weight_grads):
    saved, bigs = [], []
    for l, w in enumerate(layers):
        big = weights_of(l, x)
        x, s1 = _ffn_fwd(x, w["ffn1_norm"] + big["started"], big["ffn1_w_in"], big["ffn1_w_out"], f"l{l}_ffn1")
        x, s2 = _mixer_fwd(x, w, big, f"l{l}_mix")
        x, s3 = _ffn_fwd(x, w["ffn2_norm"], big["ffn2_w_in"], big["ffn2_w_out"], f"l{l}_ffn2")
        saved.append((s1, s2, s3))
        bigs.append(big)
    loss, dx = _loss_head(x, target, name="loss_head")
    small = [None] * len(layers)
    for l in reversed(range(len(layers))):
        w, big = layers[l], bigs[l]
        s1, s2, s3 = saved[l]
        dx, g_n2, _, _ = _ffn_bwd(
            dx, s3, w["ffn2_norm"], big["ffn2_w_in"], big["ffn2_w_out"], f"l{l}_ffn2",
            on_weight_grads=lambda g_in, g_out, l=l: on_weight_grads(l, 0, dict(ffn2_w_in=g_in, ffn2_w_out=g_out)))
        dx, g = _mixer_bwd(dx, s2, w, big, f"l{l}_mix", on_weight_grads=lambda g, l=l: on_weight_grads(l, 1, g))
        dx, g_n1, _, _ = _ffn_bwd(
            dx, s1, w["ffn1_norm"], big["ffn1_w_in"], big["ffn1_w_out"], f"l{l}_ffn1",
            on_weight_grads=lambda g_in, g_out, l=l: on_weight_grads(l, 2, dict(ffn1_w_in=g_in, ffn1_w_out=g_out)))
        small[l] = dict(g, ffn1_norm=g_n1, ffn2_norm=g_n2)
    return loss, dx, small


_BIG = ("ffn1_w_in", "ffn1_w_out", "w_in", "w_branch_a", "w_branch_b", "w_out", "ffn2_w_in", "ffn2_w_out")
_STAGES = (("ffn2_w_in", "ffn2_w_out"), ("w_in", "w_branch_a", "w_branch_b", "w_out"), ("ffn1_w_in", "ffn1_w_out"))
_SMALL = ("ffn1_norm", "mix_norm", "ffn2_norm", "dn_a_log", "dn_dt_bias", "dn_out_norm", "sb_q_norm", "sb_k_norm")
_ORDER = ("ffn1_norm", "ffn1_w_in", "ffn1_w_out", "mix_norm", "w_in", "dn_conv_w", "dn_a_log", "dn_dt_bias", "dn_out_norm",
          "sb_q_norm", "sb_k_norm", "w_branch_a", "w_branch_b", "w_out", "ffn2_norm", "ffn2_w_in", "ffn2_w_out")
COL_SCAL = 4 * D_MODEL


def _pad_rows(a, multiple):
    pad = (-a.shape[-2]) % multiple
    return a if pad == 0 else jnp.pad(a, [(0, 0)] * (a.ndim - 2) + [(0, pad), (0, 0)])


def _lane_rows(a):
    flat = a.reshape(-1)
    flat = jnp.pad(flat, (0, (-flat.shape[0]) % 128))
    return flat.reshape(-1, 128)


def _pack_small(named):
    pieces, spans, r = [], {}, 0
    for n, a in named:
        rows = _lane_rows(a)
        spans[n] = (r, r + rows.shape[0], a.shape)
        r += rows.shape[0]
        pieces.append(rows)
    return _pad_rows(jnp.concatenate(pieces, axis=0), 8), spans


def _unpack_small(packed, spans, n):
    r0, r1, shape = spans[n]
    return packed[r0:r1].reshape(-1)[:math.prod(shape)].reshape(shape)


def kernel(x, ffn1_norm, ffn1_w_in, ffn1_w_out, mix_norm, w_in, dn_conv_w, dn_a_log, dn_dt_bias, dn_out_norm, sb_q_norm, sb_k_norm, w_branch_a, w_branch_b, w_out, ffn2_norm, ffn2_w_in, ffn2_w_out, loss_target, m_ffn1_norm, m_ffn1_w_in, m_ffn1_w_out, m_mix_norm, m_w_in, m_dn_conv_w, m_dn_a_log, m_dn_dt_bias, m_dn_out_norm, m_sb_q_norm, m_sb_k_norm, m_w_branch_a, m_w_branch_b, m_w_out, m_ffn2_norm, m_ffn2_w_in, m_ffn2_w_out, v_ffn1_norm, v_ffn1_w_in, v_ffn1_w_out, v_mix_norm, v_w_in, v_dn_conv_w, v_dn_a_log, v_dn_dt_bias, v_dn_out_norm, v_sb_q_norm, v_sb_k_norm, v_w_branch_a, v_w_branch_b, v_w_out, v_ffn2_norm, v_ffn2_w_in, v_ffn2_w_out):
    given = dict(locals())
    weights = {n: given[n] for n in _ORDER}
    mom_m = {n: given["m_" + n] for n in _ORDER}
    mom_v = {n: given["v_" + n] for n in _ORDER}
    L = ffn1_norm.shape[0]
    ax, ay, ac = _position()
    my_slot = 4 * ax + 2 * ay + ac

    conv_cols = dn_conv_w.shape[-1]
    shards = lambda l, zero: [(weights[n][l] + zero).astype(BF16) for n in _BIG]
    *first_layer, conv_full = _all_gather(shards(0, 0.0) + [_pad_rows(_lane_rows(dn_conv_w), 8)], name="gather_l0")
    conv_full = conv_full.reshape(N_DEV, -1)[:, :L * DN_CONV * conv_cols].reshape(N_DEV, L, DN_CONV, conv_cols)
    conv_full = conv_full.transpose(1, 2, 0, 3).reshape(L, DN_CONV, N_DEV * conv_cols)
    arriving = {}

    def start_gather(l, zero):
        if l >= L:
            return 0.0
        *arriving[l], token = _exchange_start(shards(l, zero), gather=True, name=f"gather_start_l{l}")
        return token[0, 0]

    def weights_of(l, x_in):
        if l == 0:
            arrays, started = first_layer, start_gather(1, 0.0)
        else:
            arrays, token = _exchange_wait(*arriving[l], [x_in], gather=True, name=f"gather_wait_l{l}")
            started = start_gather(l + 1, token[0, 0])
        big = dict(zip(_BIG, arrays), started=started)
        wi = big.pop("w_in").transpose(1, 0, 2).reshape(D_MODEL, N_IN)
        big["w_main"] = jnp.concatenate([wi[:, :COL_SCAL], wi[:, COL_SCAL + 2 * HEADS:]], axis=1)
        big["w_scal"] = jnp.pad(wi[:, COL_SCAL:COL_SCAL + 2 * HEADS], ((0, 0), (0, N_SCAL - 2 * HEADS)))
        big["conv_w"] = conv_full[l]
        return big

    layers = []
    for l in range(L):
        hp = jnp.concatenate([jnp.broadcast_to(dn_a_log[l][:, None, None], (HEADS, 1, 128)),
                              jnp.broadcast_to(dn_dt_bias[l][:, None, None], (HEADS, 1, 128)),
                              jnp.zeros((HEADS, 6, 128), F32)], axis=1)
        layers.append(dict(ffn1_norm=ffn1_norm[l][None], mix_norm=mix_norm[l][None], hp=hp,
                           dn_out_norm=dn_out_norm[l][None], sb_q_norm=sb_q_norm[l][None],
                           sb_k_norm=sb_k_norm[l][None], ffn2_norm=ffn2_norm[l][None]))

    in_flight = {}

    def on_weight_grads(l, stage, g):
        parts = dict(g)
        if stage == 1:
            g_w_in = jnp.concatenate([g["w_main"][:, :COL_SCAL], g["w_scal"][:, :2 * HEADS], g["w_main"][:, COL_SCAL:]], axis=1)
            parts["w_in"] = g_w_in.reshape(D_MODEL, N_DEV, N_IN // N_DEV).transpose(1, 0, 2)
            for n in ("w_branch_a", "w_branch_b", "w_out"):
                parts[n] = g[n].reshape(N_DEV, D_MODEL // N_DEV, D_MODEL)
        *in_flight[l, stage], token = _exchange_start([parts[n] for n in _STAGES[stage]], gather=False,
                                                      name=f"scatter_start_l{l}_{stage}")
        return token

    loss_row, dx, grads = _local_step(x[0], loss_target[0], layers, weights_of, on_weight_grads)
    loss = lax.psum(loss_row[0, 0], ("x", "y", "c"))

    results = {n: None for n in _BIG}
    after = [dx]
    for l in reversed(range(L)):
        for stage, names in enumerate(_STAGES):
            landed, _ = _exchange_wait(*in_flight[l, stage], after, gather=False, name=f"scatter_wait_l{l}_{stage}")
            for n, parts in zip(names, landed):
                _, a, b = weights[n].shape
                results[n] = _adamw(parts, weights[n].reshape(L * a, b), mom_m[n].reshape(L * a, b),
                                    mom_v[n].reshape(L * a, b), layer=l, earlier=results[n], name=f"adamw_{n}_l{l}")
            after = [results[n][0] for n in names]
    out = {n: tuple(t.reshape(weights[n].shape) for t in results[n]) for n in _BIG}

    small_grads = [(n, jnp.stack([g[n].reshape(weights[n].shape[1:]) for g in grads])) for n in _SMALL]
    small_packed, spans = _pack_small(small_grads + [("conv", jnp.stack([g["conv_w"] for g in grads]))])
    small_sum = _sum_parts(_all_gather([small_packed], name="gather_small_grads")[0], name="sum_small_grads")
    rep_rows = spans["conv"][0]
    pack_rep = lambda d: _pad_rows(_pack_small([(n, d[n]) for n in _SMALL])[0], 8)
    rep_pad = (-rep_rows) % 8
    g_rep = jnp.pad(small_sum[:rep_rows], ((0, rep_pad), (0, 0)))
    res = _adamw(g_rep[None], pack_rep(weights), pack_rep(mom_m), pack_rep(mom_v), name="adamw_replicated")
    for n in _SMALL:
        out[n] = tuple(_unpack_small(t, spans, n) for t in res)
    conv_sum = _unpack_small(small_sum, spans, "conv")
    conv_mine = lax.dynamic_slice_in_dim(conv_sum, my_slot * conv_cols, conv_cols, axis=2).reshape(L * DN_CONV, conv_cols)
    flat = lambda t: t.reshape(L * DN_CONV, conv_cols)
    res = _adamw(conv_mine[None], flat(dn_conv_w), flat(m_dn_conv_w), flat(v_dn_conv_w), name="adamw_conv")
    out["dn_conv_w"] = tuple(t.reshape(L, DN_CONV, conv_cols) for t in res)

    return (loss, dx[None], *[out[n][0] for n in _ORDER], *[out[n][1] for n in _ORDER],
            *[out[n][2] for n in _ORDER], *[out[n][3] for n in _ORDER])
```

```python
import functools
import math

import jax
import jax.numpy as jnp
from jax import lax
from jax.experimental import pallas as pl
from jax.experimental.pallas import tpu as pltpu

F32 = jnp.float32
BF16 = jnp.bfloat16

N_DEV = 8
D_MODEL = 1024
DEPTH = 4
D_FF = 2816
HEADS = 8
HEAD_DIM = 128
DN_CHUNK = 64
DN_CONV = 4
DN_GROUP = 8
DN_HEADS = 2
SB_BLOCK = 128
SB_KEY_TILE = 512
SB_HEADS = 4
SB_HEADS_BWD = 2
RMS_EPS = 1e-6
L2_EPS = 1e-6
N_IN = 9232
N_MAIN = 9216
N_SCAL = 128
QK_SCALE = HEAD_DIM ** -0.5

ADAM_LR = 0.001
ADAM_B1 = 0.9
ADAM_B2 = 0.999
ADAM_EPS = 1e-08
ADAM_WD = 0.01
ADAM_STEP = 10

V7X_VMEM_LIMIT = 56 * 1024 * 1024
MESH = pl.DeviceIdType.MESH
ANY = pl.BlockSpec(memory_space=pl.ANY)


def _params(sem=None, vmem=V7X_VMEM_LIMIT):
    return pltpu.CompilerParams(dimension_semantics=sem, vmem_limit_bytes=vmem)


def _sigmoid(x):
    return 1.0 / (1.0 + jnp.exp(-x))


def _softplus(x):
    return jnp.maximum(x, 0.0) + jnp.log(1.0 + jnp.exp(-jnp.abs(x)))


def _bdot(a, b, dims=(((1,), (0,)), ((), ()))):
    return lax.dot_general(a.astype(BF16), b.astype(BF16), dims, preferred_element_type=F32)


_NT = (((1,), (1,)), ((), ()))
_TN = (((0,), (0,)), ((), ()))


def _hdot(a, b, dims=(((1,), (0,)), ((), ()))):
    a_hi = a.astype(BF16)
    b_hi = b.astype(BF16)
    a_lo = (a - a_hi.astype(F32)).astype(BF16)
    b_lo = (b - b_hi.astype(F32)).astype(BF16)
    dot = functools.partial(lax.dot_general, dimension_numbers=dims, preferred_element_type=F32)
    return dot(a_hi, b_hi) + (dot(a_hi, b_lo) + dot(a_lo, b_hi))


def _hdot_tn(a, b):
    return _hdot(a, b, _TN)


def _mm(*, name, grid, a, a_spec, b, b_spec, out_shape, o_spec, tile, ta=False, tb=False, resid=None, scale=1.0):
    nk = grid[2]
    dims = (((0 if ta else 1,), (1 if tb else 0,)), ((), ()))

    def flat(v):
        return v if v.ndim == 2 else v.reshape(-1, v.shape[-1])

    def body(*refs):
        a_ref, b_ref = refs[:2]
        r_ref = refs[2] if resid is not None else None
        o_ref = refs[3] if resid is not None else refs[2]
        part = lax.dot_general(flat(a_ref[...]).astype(BF16), flat(b_ref[...]).astype(BF16), dims,
                               preferred_element_type=F32)

        def finish(acc):
            if scale != 1.0:
                acc = acc * scale
            if r_ref is not None:
                acc = r_ref[...] + acc
            o_ref[...] = acc.astype(o_ref.dtype)

        if nk == 1:
            finish(part)
        else:
            acc_ref = refs[-1]
            k = pl.program_id(2)

            @pl.when(k == 0)
            def _():
                acc_ref[...] = part

            @pl.when(k > 0)
            def _():
                acc_ref[...] += part

            @pl.when(k == nk - 1)
            def _():
                finish(acc_ref[...])

    in_specs = [a_spec, b_spec] + ([pl.BlockSpec(tile, lambda i, j, k: (i, j))] if resid is not None else [])
    args = (a, b) + ((resid,) if resid is not None else ())
    return pl.pallas_call(
        body, name=name, grid=grid, in_specs=in_specs, out_specs=o_spec, out_shape=out_shape,
        scratch_shapes=[pltpu.VMEM(tile, F32)] if nk > 1 else [],
        compiler_params=_params(("parallel", "parallel", "arbitrary")),
    )(*args)


def _matmul(a, b, *, name, ta=False, tb=False, out_dtype=F32, tm=None, tn=None, tk=None, resid=None, scale=1.0):
    if ta:
        K, M = a.shape
    else:
        M, K = a.shape
    N = b.shape[0] if tb else b.shape[1]
    tm = tm or min(M, 1024)
    tn = tn or min(N, 512)
    tk = tk or K
    assert M % tm == 0 and N % tn == 0 and K % tk == 0, (name, M, N, K, tm, tn, tk)
    a_spec = pl.BlockSpec((tk, tm), lambda i, j, k: (k, i)) if ta else pl.BlockSpec((tm, tk), lambda i, j, k: (i, k))
    b_spec = pl.BlockSpec((tn, tk), lambda i, j, k: (j, k)) if tb else pl.BlockSpec((tk, tn), lambda i, j, k: (k, j))
    return _mm(name=name, grid=(M // tm, N // tn, K // tk), a=a, a_spec=a_spec, b=b, b_spec=b_spec,
               out_shape=jax.ShapeDtypeStruct((M, N), out_dtype), o_spec=pl.BlockSpec((tm, tn), lambda i, j, k: (i, j)),
               tile=(tm, tn), ta=ta, tb=tb, resid=resid, scale=scale)


ROW_TILE = 256


def _rmsnorm_fwd(x, gain, *, name):
    T, D = x.shape

    def body(x_ref, g_ref, o_ref):
        xf = x_ref[...]
        r = lax.rsqrt(jnp.mean(xf * xf, axis=-1, keepdims=True) + RMS_EPS)
        o_ref[...] = (xf * r * g_ref[...]).astype(o_ref.dtype)

    return pl.pallas_call(
        body, name=name, grid=(T // ROW_TILE,),
        in_specs=[pl.BlockSpec((ROW_TILE, D), lambda i: (i, 0)), pl.BlockSpec((1, D), lambda i: (0, 0))],
        out_specs=pl.BlockSpec((ROW_TILE, D), lambda i: (i, 0)),
        out_shape=jax.ShapeDtypeStruct((T, D), BF16), compiler_params=_params(("parallel",)),
    )(x, gain)


def _rmsnorm_bwd(dh, x, gain, dres, *, name):
    T, D = x.shape

    def body(dh_ref, x_ref, g_ref, res_ref, dx_ref, dg_ref):
        xf = x_ref[...]
        r = lax.rsqrt(jnp.mean(xf * xf, axis=-1, keepdims=True) + RMS_EPS)
        y = xf * r
        dh_v = dh_ref[...].astype(F32)
        dy = dh_v * g_ref[...]
        dx_ref[...] = res_ref[...] + r * (dy - y * jnp.mean(dy * y, axis=-1, keepdims=True))

        @pl.when(pl.program_id(0) == 0)
        def _():
            dg_ref[...] = jnp.zeros_like(dg_ref)

        dg_ref[...] += jnp.sum(dh_v * y, axis=0, keepdims=True)

    row = pl.BlockSpec((ROW_TILE, D), lambda i: (i, 0))
    vec = pl.BlockSpec((1, D), lambda i: (0, 0))
    return pl.pallas_call(
        body, name=name, grid=(T // ROW_TILE,), in_specs=[row, row, vec, row], out_specs=(row, vec),
        out_shape=(jax.ShapeDtypeStruct((T, D), F32), jax.ShapeDtypeStruct((1, D), F32)),
        compiler_params=_params(("arbitrary",)),
    )(dh, x, gain, dres)


FF_HALF = N_DEV // 2


def _swiglu_fwd(p, *, name):
    _, T, fb = p.shape

    def body(g_ref, u_ref, o_ref):
        g = g_ref[...].astype(F32)
        o_ref[...] = (g * _sigmoid(g) * u_ref[...].astype(F32)).astype(o_ref.dtype)

    blk = (None, ROW_TILE, fb)
    return pl.pallas_call(
        body, name=name, grid=(T // ROW_TILE, FF_HALF),
        in_specs=[pl.BlockSpec(blk, lambda i, j: (j, i, 0)), pl.BlockSpec(blk, lambda i, j: (j + FF_HALF, i, 0))],
        out_specs=pl.BlockSpec(blk, lambda i, j: (j, i, 0)),
        out_shape=jax.ShapeDtypeStruct((FF_HALF, T, fb), BF16), compiler_params=_params(("parallel", "parallel")),
    )(p, p)


def _swiglu_bwd(da, p, *, name):
    _, T, fb = p.shape

    def body(da_ref, g_ref, u_ref, o_ref):
        g = g_ref[...].astype(F32)
        u = u_ref[...].astype(F32)
        d = da_ref[...].astype(F32)
        s = _sigmoid(g)
        o_ref[0] = (d * u * (s * (1.0 + g * (1.0 - s)))).astype(o_ref.dtype)
        o_ref[1] = (d * g * s).astype(o_ref.dtype)

    blk = (None, ROW_TILE, fb)
    out = pl.pallas_call(
        body, name=name, grid=(T // ROW_TILE, FF_HALF),
        in_specs=[pl.BlockSpec(blk, lambda i, j: (j, i, 0)), pl.BlockSpec(blk, lambda i, j: (j, i, 0)),
                  pl.BlockSpec(blk, lambda i, j: (j + FF_HALF, i, 0))],
        out_specs=pl.BlockSpec((2, None, ROW_TILE, fb), lambda i, j: (0, j, i, 0)),
        out_shape=jax.ShapeDtypeStruct((2, FF_HALF, T, fb), BF16), compiler_params=_params(("parallel", "parallel")),
    )(da, p, p)
    return out.reshape(2 * FF_HALF, T, fb)


COL_GATE_A = 7
COL_GATE_B = 8


def _merge_fwd(ya, yb, proj, *, name):
    T, D = ya.shape

    def body(ya_ref, yb_ref, ga_ref, gb_ref, o_ref):
        o_ref[...] = (_sigmoid(ga_ref[...]) * ya_ref[...] + _sigmoid(gb_ref[...]) * yb_ref[...]).astype(o_ref.dtype)

    row = pl.BlockSpec((ROW_TILE, D), lambda i: (i, 0))
    return pl.pallas_call(
        body, name=name, grid=(T // ROW_TILE,),
        in_specs=[row, row, pl.BlockSpec((ROW_TILE, D), lambda i: (i, COL_GATE_A)),
                  pl.BlockSpec((ROW_TILE, D), lambda i: (i, COL_GATE_B))],
        out_specs=row, out_shape=jax.ShapeDtypeStruct((T, D), BF16), compiler_params=_params(("parallel",)),
    )(ya, yb, proj, proj)


def _merge_bwd(dm, ya, yb, proj, *, name):
    T, D = ya.shape

    def body(dm_ref, ya_ref, yb_ref, ga_ref, gb_ref, dya_ref, dyb_ref, dga_ref, dgb_ref):
        d = dm_ref[...].astype(F32)
        sa = _sigmoid(ga_ref[...])
        sb = _sigmoid(gb_ref[...])
        dya_ref[...] = (d * sa).astype(BF16)
        dyb_ref[...] = (d * sb).astype(BF16)
        dga_ref[...] = (d * ya_ref[...] * sa * (1.0 - sa)).astype(BF16)
        dgb_ref[...] = (d * yb_ref[...] * sb * (1.0 - sb)).astype(BF16)

    row = pl.BlockSpec((ROW_TILE, D), lambda i: (i, 0))
    out = jax.ShapeDtypeStruct((T, D), BF16)
    return pl.pallas_call(
        body, name=name, grid=(T // ROW_TILE,),
        in_specs=[row, row, row, pl.BlockSpec((ROW_TILE, D), lambda i: (i, COL_GATE_A)),
                  pl.BlockSpec((ROW_TILE, D), lambda i: (i, COL_GATE_B))],
        out_specs=(row, row, row, row), out_shape=(out, out, out, out), compiler_params=_params(("parallel",)),
    )(dm, ya, yb, proj, proj)


def _loss_head(y, target, *, name):
    T, D = y.shape

    def body(y_ref, t_ref, loss_ref, dy_ref):
        err = y_ref[...] - t_ref[...]
        dy_ref[...] = err * (1.0 / D)

        @pl.when(pl.program_id(0) == 0)
        def _():
            loss_ref[...] = jnp.zeros_like(loss_ref)

        loss_ref[...] += 0.5 * jnp.sum(jnp.sum(err * err, axis=-1, keepdims=True) * (1.0 / D), axis=0, keepdims=True)

    row = pl.BlockSpec((ROW_TILE, D), lambda i: (i, 0))
    return pl.pallas_call(
        body, name=name, grid=(T // ROW_TILE,), in_specs=[row, row],
        out_specs=(pl.BlockSpec((1, 128), lambda i: (0, 0)), row),
        out_shape=(jax.ShapeDtypeStruct((1, 128), F32), jax.ShapeDtypeStruct((T, D), F32)),
        compiler_params=_params(("arbitrary",)),
    )(y, target)


CONV_PAD = 8


def _conv_taps(w, xp, T, first):
    acc = w[0:1, :] * xp[pl.ds(first, T), :]
    for i in range(1, DN_CONV):
        acc = acc + w[i:i + 1, :] * xp[pl.ds(first + i, T), :]
    return acc


def _conv_fwd(proj, conv_w, *, name):
    T = proj.shape[0]

    def body(x_ref, w_ref, o_ref, xp):
        xp[0:CONV_PAD, :] = jnp.zeros((CONV_PAD, HEAD_DIM), F32)
        xp[CONV_PAD:, :] = x_ref[...]
        y = _conv_taps(w_ref[...], xp, T, CONV_PAD - (DN_CONV - 1))
        s = y * _sigmoid(y)
        n = s * lax.rsqrt(jnp.sum(s * s, axis=-1, keepdims=True) + L2_EPS)
        o_ref[0] = jnp.where(pl.program_id(0) < 2, n, s)

    return pl.pallas_call(
        body, name=name, grid=(3, HEADS),
        in_specs=[pl.BlockSpec((T, HEAD_DIM), lambda c, h: (0, c * HEADS + h)),
                  pl.BlockSpec((DN_CONV, HEAD_DIM), lambda c, h: (0, c * HEADS + h))],
        out_specs=pl.BlockSpec((1, T, HEAD_DIM), lambda c, h: (c, 0, h)),
        out_shape=jax.ShapeDtypeStruct((3, T, D_MODEL), F32),
        scratch_shapes=[pltpu.VMEM((T + CONV_PAD, HEAD_DIM), F32)],
        compiler_params=_params(("parallel", "parallel")),
    )(proj, conv_w)


def _conv_bwd(dqkv, proj, conv_w, *, name):
    T = proj.shape[0]

    def body(d_ref, x_ref, w_ref, dx_ref, dw_ref, xp, dyp):
        xp[0:CONV_PAD, :] = jnp.zeros((CONV_PAD, HEAD_DIM), F32)
        xp[CONV_PAD:, :] = x_ref[...]
        w = w_ref[...]
        y = _conv_taps(w, xp, T, CONV_PAD - (DN_CONV - 1))
        sg = _sigmoid(y)
        s = y * sg
        r = lax.rsqrt(jnp.sum(s * s, axis=-1, keepdims=True) + L2_EPS)
        n = s * r
        d = d_ref[0]
        ds = jnp.where(pl.program_id(0) < 2, r * (d - n * jnp.sum(d * n, axis=-1, keepdims=True)), d)
        dy = ds * (sg * (1.0 + y * (1.0 - sg)))
        dyp[0:T, :] = dy
        dyp[T:, :] = jnp.zeros((CONV_PAD, HEAD_DIM), F32)
        dx = w[0:1, :] * dyp[pl.ds(DN_CONV - 1, T), :]
        for i in range(1, DN_CONV):
            dx = dx + w[i:i + 1, :] * dyp[pl.ds(DN_CONV - 1 - i, T), :]
        dx_ref[...] = dx.astype(dx_ref.dtype)
        for i in range(DN_CONV):
            dw_ref[i:i + 1, :] = jnp.sum(dy * xp[pl.ds(CONV_PAD - (DN_CONV - 1) + i, T), :], axis=0, keepdims=True)

    col = lambda c, h: (0, c * HEADS + h)
    return pl.pallas_call(
        body, name=name, grid=(3, HEADS),
        in_specs=[pl.BlockSpec((1, T, HEAD_DIM), lambda c, h: (c, 0, h)), pl.BlockSpec((T, HEAD_DIM), col),
                  pl.BlockSpec((DN_CONV, HEAD_DIM), col)],
        out_specs=(pl.BlockSpec((T, HEAD_DIM), col), pl.BlockSpec((DN_CONV, HEAD_DIM), col)),
        out_shape=(jax.ShapeDtypeStruct((T, 3 * D_MODEL), BF16), jax.ShapeDtypeStruct((DN_CONV, 3 * D_MODEL), F32)),
        scratch_shapes=[pltpu.VMEM((T + CONV_PAD, HEAD_DIM), F32), pltpu.VMEM((T + CONV_PAD, HEAD_DIM), F32)],
        compiler_params=_params(("parallel", "parallel")),
    )(dqkv, proj, conv_w)


def _inv_unit_lower(low, eye):
    x = eye - low
    power = _hdot(low, low, _B_NN)
    steps = int(math.log2(DN_CHUNK)) - 1
    for s in range(steps):
        x = x + _hdot(x, power, _B_NN)
        if s + 1 < steps:
            power = _hdot(power, power, _B_NN)
    return x


_B_NN = (((2,), (1,)), ((0,), (0,)))
_B_NT = (((2,), (2,)), ((0,), (0,)))
_B_TN = (((1,), (1,)), ((0,), (0,)))


def _dn_load(ref, lead, r0, group):
    rows = pl.ds(r0, group * DN_CHUNK)
    cols = lambda h: slice(h * HEAD_DIM, (h + 1) * HEAD_DIM)
    per_head = [(ref[rows, cols(h)] if lead is None else ref[lead, rows, cols(h)]).reshape(group, DN_CHUNK, HEAD_DIM)
                for h in range(DN_HEADS)]
    return jnp.stack(per_head, axis=1).reshape(group * DN_HEADS, DN_CHUNK, HEAD_DIM)


def _dn_chunk_setup(qkv_ref, b_ref, a_ref, hp_ref, n0, group, tinv=None):
    C = DN_CHUNK
    B = group * DN_HEADS
    r0 = pl.multiple_of(n0 * C, C)
    q = _dn_load(qkv_ref, 0, r0, group) * QK_SCALE
    k = _dn_load(qkv_ref, 1, r0, group)
    v = _dn_load(qkv_ref, 2, r0, group)
    ii = lax.broadcasted_iota(jnp.int32, (B, C, C), 1)
    jj = lax.broadcasted_iota(jnp.int32, (B, C, C), 2)
    eye_mask = ii == jj
    eye = jnp.where(eye_mask, 1.0, 0.0).astype(F32)

    def to_col(row):
        return jnp.sum(jnp.where(eye_mask, jnp.broadcast_to(row, (B, C, C)), 0.0), axis=2, keepdims=True)

    def to_row(col):
        return jnp.sum(jnp.where(eye_mask, jnp.broadcast_to(col, (B, C, C)), 0.0), axis=1, keepdims=True)

    def rows(ref):
        return jnp.stack([ref[h, pl.ds(n0, group)] for h in range(DN_HEADS)], axis=1).reshape(B, 1, C)

    def per_head(row):
        return jnp.stack([hp_ref[h, row:row + 1, 0:C] for h in range(DN_HEADS)] * group, axis=0)

    b_row = rows(b_ref)
    a_row = rows(a_ref)
    a_log = per_head(0)
    dt_b = per_head(1)
    beta_row = _sigmoid(b_row)
    neg_ea = -jnp.exp(a_log)
    g_row = neg_ea * _softplus(a_row + dt_b)
    gc_col = jnp.sum(jnp.where(jj <= ii, jnp.broadcast_to(g_row, (B, C, C)), 0.0), axis=2, keepdims=True)
    gc_row = to_row(gc_col)
    g_last = jnp.sum(g_row, axis=2, keepdims=True)
    beta = to_col(beta_row)
    low_incl = ii >= jj
    decay = jnp.exp(jnp.where(low_incl, gc_col - gc_row, -jnp.inf))
    eg = jnp.exp(gc_col)
    egl = jnp.exp(g_last - gc_col)
    el = jnp.exp(g_last)
    kb = k * beta
    pmat = _bdot(kb, k, _B_NT)
    low = jnp.where(ii > jj, pmat * decay, 0.0)
    if tinv is None:
        tinv = _inv_unit_lower(low, eye)
    u = _hdot(tinv, v * beta, _B_NN)
    w = _hdot(tinv, kb * eg, _B_NN)
    qk = _bdot(q, k, _B_NT)
    attn = qk * decay
    return dict(q=q, k=k, v=v, ii=ii, jj=jj, to_col=to_col, to_row=to_row, b_row=b_row, a_row=a_row, dt_b=dt_b,
                beta_row=beta_row, neg_ea=neg_ea, g_row=g_row, gc_col=gc_col, g_last=g_last, beta=beta,
                decay=decay, eg=eg, egl=egl, el=el, kb=kb, pmat=pmat, tinv=tinv, u=u, w=w, qk=qk, attn=attn,
                qd=q * eg, kd=k * egl, r0=r0)


def _dn_store(ref, lead, r0, group, value):
    value = value.reshape(group, DN_HEADS, DN_CHUNK, HEAD_DIM)
    for h in range(DN_HEADS):
        block = value[:, h].reshape(group * DN_CHUNK, HEAD_DIM)
        if lead is None:
            ref[pl.ds(r0, group * DN_CHUNK), h * HEAD_DIM:(h + 1) * HEAD_DIM] = block
        else:
            ref[lead, pl.ds(r0, group * DN_CHUNK), h * HEAD_DIM:(h + 1) * HEAD_DIM] = block


def _dn_specs(T):
    nc = T // DN_CHUNK
    qkv = pl.BlockSpec((3, T, DN_HEADS * HEAD_DIM), lambda h: (0, 0, h))
    rows = pl.BlockSpec((DN_HEADS, nc, 1, DN_CHUNK), lambda h: (h, 0, 0, 0))
    hp = pl.BlockSpec((DN_HEADS, 8, 128), lambda h: (h, 0, 0))
    states = pl.BlockSpec((DN_HEADS, nc, HEAD_DIM, HEAD_DIM), lambda h: (h, 0, 0, 0))
    return nc, qkv, rows, hp, states


def _dn_inverse_spec(T):
    return pl.BlockSpec((DN_HEADS, T // DN_CHUNK, DN_CHUNK, DN_CHUNK), lambda h: (h, 0, 0, 0))


def _dn_per_head(ref, n0, group):
    stacked = jnp.stack([ref[h, pl.ds(n0, group)] for h in range(DN_HEADS)], axis=1)
    return stacked.reshape((group * DN_HEADS,) + stacked.shape[2:])


def _dn_fwd(qkv, b_rows, a_rows, hp, *, name):
    T = qkv.shape[1]
    nc, qkv_spec, row_spec, hp_spec, st_spec = _dn_specs(T)
    group = math.gcd(nc, DN_GROUP)
    H = DN_HEADS

    def body(qkv_ref, b_ref, a_ref, hp_ref, o_ref, st_ref, inv_ref, s_scr):
        s_scr[...] = jnp.zeros_like(s_scr)

        def step(t, carry):
            n0 = t * group
            c = _dn_chunk_setup(qkv_ref, b_ref, a_ref, hp_ref, n0, group)
            tinv = c["tinv"].reshape(group, H, DN_CHUNK, DN_CHUNK)
            for h in range(H):
                inv_ref[h, pl.ds(n0, group)] = tinv[:, h]
            state = s_scr[...]
            outs = []
            for g in range(group):
                sl = slice(g * H, (g + 1) * H)
                for h in range(H):
                    st_ref[h, n0 + g] = state[h]
                v_new = c["u"][sl] - _bdot(c["w"][sl], state, _B_NN)
                outs.append(_bdot(c["qd"][sl], state, _B_NN) + _bdot(c["attn"][sl], v_new, _B_NN))
                state = state * c["el"][sl] + _bdot(c["kd"][sl], v_new, _B_TN)
            s_scr[...] = state
            _dn_store(o_ref, None, c["r0"], group, jnp.concatenate(outs, axis=0))
            return carry

        lax.fori_loop(0, nc // group, step, 0)

    return pl.pallas_call(
        body, name=name, grid=(HEADS // H,), in_specs=[qkv_spec, row_spec, row_spec, hp_spec],
        out_specs=(pl.BlockSpec((T, H * HEAD_DIM), lambda h: (0, h)), st_spec, _dn_inverse_spec(T)),
        out_shape=(jax.ShapeDtypeStruct((T, D_MODEL), F32),
                   jax.ShapeDtypeStruct((HEADS, nc, HEAD_DIM, HEAD_DIM), F32),
                   jax.ShapeDtypeStruct((HEADS, nc, DN_CHUNK, DN_CHUNK), F32)),
        scratch_shapes=[pltpu.VMEM((H, HEAD_DIM, HEAD_DIM), F32)], compiler_params=_params(("parallel",)),
    )(qkv, b_rows, a_rows, hp)


def _dn_bwd(qkv, b_rows, a_rows, hp, states, inverses, do, *, name):
    T = qkv.shape[1]
    C = DN_CHUNK
    nc, qkv_spec, row_spec, hp_spec, st_spec = _dn_specs(T)
    group = math.gcd(nc, DN_GROUP)
    H = DN_HEADS
    B = group * H

    def body(qkv_ref, b_ref, a_ref, hp_ref, st_ref, inv_ref, do_ref, dqkv_ref, db_ref, da_ref, dhp_ref, ds_scr, acc_scr):
        ds_scr[...] = jnp.zeros_like(ds_scr)
        acc_scr[...] = jnp.zeros_like(acc_scr)

        def step(t, carry):
            n0 = nc - (t + 1) * group
            c = _dn_chunk_setup(qkv_ref, b_ref, a_ref, hp_ref, n0, group, tinv=_dn_per_head(inv_ref, n0, group))
            state = _dn_per_head(st_ref, n0, group)
            d_o = _dn_load(do_ref, None, c["r0"], group)
            v_new = c["u"] - _bdot(c["w"], state, _B_NN)
            d_vnew_local = _bdot(c["attn"], d_o, _B_TN)
            d_state_local = _bdot(c["qd"], d_o, _B_TN)
            d_state = ds_scr[...]
            d_vnew, d_kd, d_el = [None] * group, [None] * group, [None] * group
            for g in reversed(range(group)):
                sl = slice(g * H, (g + 1) * H)
                d_vnew[g] = d_vnew_local[sl] + _bdot(c["kd"][sl], d_state, _B_NN)
                d_kd[g] = _bdot(v_new[sl], d_state, _B_NT)
                d_el[g] = jnp.sum(jnp.sum(d_state * state[sl], axis=2, keepdims=True), axis=1, keepdims=True)
                d_state = d_state * c["el"][sl] + d_state_local[sl] - _bdot(c["w"][sl], d_vnew[g], _B_TN)
            ds_scr[...] = d_state
            chunk_grads(c, n0, state, d_o, v_new, jnp.concatenate(d_vnew, axis=0), jnp.concatenate(d_kd, axis=0),
                        jnp.concatenate(d_el, axis=0))
            return carry

        def chunk_grads(c, n0, state, d_o, v_new, d_vnew, d_kd, d_el):
            ii, jj = c["ii"], c["jj"]
            q, k, v, kb, beta = c["q"], c["k"], c["v"], c["kb"], c["beta"]
            decay, eg, egl, el = c["decay"], c["eg"], c["egl"], c["el"]
            u, w, tinv = c["u"], c["w"], c["tinv"]
            d_qd = _bdot(d_o, state, _B_NT)
            d_attn = _bdot(d_o, v_new, _B_NT)
            d_w = -_bdot(d_vnew, state, _B_NT)
            d_rv = _hdot(tinv, d_vnew, _B_TN)
            d_rw = _hdot(tinv, d_w, _B_TN)
            d_amat = -(_bdot(d_rv, u, _B_NT) + _bdot(d_rw, w, _B_NT))
            d_low = jnp.where(ii > jj, d_amat, 0.0)
            d_p = d_low * decay
            d_qk = d_attn * decay
            e_mat = (d_low * c["pmat"] + d_attn * c["qk"]) * decay
            d_q = _bdot(d_qk, k, _B_NN) + d_qd * eg
            d_kb = _bdot(d_p, k, _B_NN) + d_rw * eg
            d_k = _bdot(d_qk, q, _B_TN) + _bdot(d_p, kb, _B_TN) + d_kd * egl + d_kb * beta
            d_beta = jnp.sum(d_kb * k, axis=2, keepdims=True) + jnp.sum(d_rv * v, axis=2, keepdims=True)
            d_v = d_rv * beta
            d_eg = jnp.sum(d_qd * q, axis=2, keepdims=True) + jnp.sum(d_rw * kb, axis=2, keepdims=True)
            d_egl = jnp.sum(d_kd * k, axis=2, keepdims=True)
            d_glast = jnp.sum(d_egl * egl, axis=1, keepdims=True) + d_el * el
            row_sum = jnp.sum(e_mat, axis=2, keepdims=True)
            col_sum = c["to_col"](jnp.sum(e_mat, axis=1, keepdims=True))
            d_gc = row_sum - col_sum + d_eg * eg - d_egl * egl
            d_g_row = jnp.sum(jnp.where(ii >= jj, jnp.broadcast_to(d_gc, (B, C, C)), 0.0), axis=1, keepdims=True) + d_glast
            beta_row = c["beta_row"]
            d_b_row = c["to_row"](d_beta) * beta_row * (1.0 - beta_row)
            d_a_row = d_g_row * c["neg_ea"] * _sigmoid(c["a_row"] + c["dt_b"])
            _dn_store(dqkv_ref, 0, c["r0"], group, d_q * QK_SCALE)
            _dn_store(dqkv_ref, 1, c["r0"], group, d_k)
            _dn_store(dqkv_ref, 2, c["r0"], group, d_v)
            d_b_row = d_b_row.reshape(group, H, 1, C)
            d_a_row = d_a_row.reshape(group, H, 1, C)
            d_a_log = jnp.sum((d_g_row * c["g_row"]).reshape(group, H, 1, C), axis=0)
            d_dt_b = jnp.sum(d_a_row, axis=0)
            for h in range(H):
                db_ref[h, pl.ds(n0, group)] = d_b_row[:, h]
                da_ref[h, pl.ds(n0, group)] = d_a_row[:, h]
                acc_scr[h, 0:1, 0:C] += d_a_log[h]
                acc_scr[h, 1:2, 0:C] += d_dt_b[h]

        lax.fori_loop(0, nc // group, step, 0)
        for h in range(H):
            tot = jnp.sum(acc_scr[h], axis=1, keepdims=True)
            dhp_ref[h] = jnp.broadcast_to(tot, (8, 128))

    return pl.pallas_call(
        body, name=name, grid=(HEADS // H,),
        in_specs=[qkv_spec, row_spec, row_spec, hp_spec, st_spec, _dn_inverse_spec(T),
                  pl.BlockSpec((T, H * HEAD_DIM), lambda h: (0, h))],
        out_specs=(qkv_spec, row_spec, row_spec, hp_spec),
        out_shape=(jax.ShapeDtypeStruct((3, T, D_MODEL), F32), jax.ShapeDtypeStruct((HEADS, nc, 1, C), F32),
                   jax.ShapeDtypeStruct((HEADS, nc, 1, C), F32), jax.ShapeDtypeStruct((HEADS, 8, 128), F32)),
        scratch_shapes=[pltpu.VMEM((H, HEAD_DIM, HEAD_DIM), F32), pltpu.VMEM((H, 8, 128), F32)],
        compiler_params=_params(("parallel",)),
    )(qkv, b_rows, a_rows, hp, states, inverses, do)


COL_Z = 3 * HEADS


def _gated_norm_fwd(o, proj, gain, *, name):
    T = o.shape[0]

    def body(o_ref, z_ref, g_ref, out_ref):
        x = o_ref[...]
        r = lax.rsqrt(jnp.mean(x * x, axis=-1, keepdims=True) + RMS_EPS)
        z = z_ref[...]
        out_ref[...] = (x * r * g_ref[...] * (z * _sigmoid(z))).astype(out_ref.dtype)

    return pl.pallas_call(
        body, name=name, grid=(HEADS,),
        in_specs=[pl.BlockSpec((T, HEAD_DIM), lambda h: (0, h)), pl.BlockSpec((T, HEAD_DIM), lambda h: (0, COL_Z + h)),
                  pl.BlockSpec((1, HEAD_DIM), lambda h: (0, 0))],
        out_specs=pl.BlockSpec((T, HEAD_DIM), lambda h: (0, h)),
        out_shape=jax.ShapeDtypeStruct((T, D_MODEL), BF16), compiler_params=_params(("parallel",)),
    )(o, proj, gain)


def _gated_norm_bwd(dout, o, proj, gain, *, name):
    T = o.shape[0]

    def body(d_ref, o_ref, z_ref, g_ref, do_ref, dz_ref, dg_ref):
        x = o_ref[...]
        r = lax.rsqrt(jnp.mean(x * x, axis=-1, keepdims=True) + RMS_EPS)
        n = x * r
        z = z_ref[...]
        sg = _sigmoid(z)
        d = d_ref[...].astype(F32)
        g = g_ref[...]
        dz_ref[...] = (d * n * g * (sg * (1.0 + z * (1.0 - sg)))).astype(dz_ref.dtype)
        dy = d * (z * sg)
        dyg = dy * g
        do_ref[...] = r * (dyg - n * jnp.mean(dyg * n, axis=-1, keepdims=True))

        @pl.when(pl.program_id(0) == 0)
        def _():
            dg_ref[...] = jnp.zeros_like(dg_ref)

        dg_ref[...] += jnp.sum(dy * n, axis=0, keepdims=True)

    head = pl.BlockSpec((T, HEAD_DIM), lambda h: (0, h))
    vec = pl.BlockSpec((1, HEAD_DIM), lambda h: (0, 0))
    return pl.pallas_call(
        body, name=name, grid=(HEADS,),
        in_specs=[head, head, pl.BlockSpec((T, HEAD_DIM), lambda h: (0, COL_Z + h)), vec],
        out_specs=(head, head, vec),
        out_shape=(jax.ShapeDtypeStruct((T, D_MODEL), F32), jax.ShapeDtypeStruct((T, D_MODEL), BF16),
                   jax.ShapeDtypeStruct((1, HEAD_DIM), F32)),
        compiler_params=_params(("arbitrary",)),
    )(dout, o, proj, gain)


COL_SBQ = 4 * HEADS
COL_SBK = 5 * HEADS
COL_SBV = 6 * HEADS


def _split_dot(x, mat):
    lead = x.shape[:-1]
    x = x.reshape(-1, x.shape[-1])
    hi = x.astype(BF16)
    lo = (x - hi.astype(F32)).astype(BF16)
    out = jnp.dot(hi, mat, preferred_element_type=F32) + jnp.dot(lo, mat, preferred_element_type=F32)
    return out.reshape(lead + (mat.shape[-1],))


def _sb_specs(T, heads):
    col = lambda first: pl.BlockSpec((T, heads * HEAD_DIM), lambda h: (0, first // heads + h))
    return col(COL_SBQ), col(COL_SBK), col(COL_SBV), pl.BlockSpec((1, HEAD_DIM), lambda h: (0, 0))


def _heads_first(x):
    return jnp.stack([x[:, c:c + HEAD_DIM] for c in range(0, x.shape[1], HEAD_DIM)], axis=0)


def _heads_last(x):
    return jnp.concatenate([x[h] for h in range(x.shape[0])], axis=1)


def _head_rms(x):
    r = lax.rsqrt(jnp.mean(x * x, axis=-1, keepdims=True) + RMS_EPS)
    return x * r, r


def _sb_fwd(proj, q_gain, k_gain, *, name):
    T = proj.shape[0]
    B = SB_BLOCK
    H = SB_HEADS
    nb = T // B
    KT = min(SB_KEY_TILE, T)
    NS = KT // B
    q_spec, k_spec, v_spec, g_spec = _sb_specs(T, H)

    def body(q_ref, k_ref, v_ref, gq_ref, gk_ref, o_ref, lt_ref, qs, ks, vs):
        qs[...] = (_head_rms(_heads_first(q_ref[...]))[0] * (gq_ref[...] * QK_SCALE)).astype(BF16)
        ks[...] = (_head_rms(_heads_first(k_ref[...]))[0] * gk_ref[...]).astype(BF16)
        vs[...] = _heads_first(v_ref[...]).astype(BF16)
        ii = lax.broadcasted_iota(jnp.int32, (B, B), 0)
        jj = lax.broadcasted_iota(jnp.int32, (B, B), 1)
        after = jnp.where(ii > jj, 1.0, 0.0).astype(BF16)
        ahead = lax.broadcasted_iota(jnp.int32, (H, B, KT), 2) - lax.broadcasted_iota(jnp.int32, (H, B, KT), 1)

        def q_block(i, carry):
            rows = pl.ds(pl.multiple_of(i * B, B), B)
            q = qs[:, rows, :]

            def tile(c0, acc, tail, masked):
                cols = pl.ds(c0, KT)
                z = lax.dot_general(q, ks[:, cols, :], _B_NT, preferred_element_type=F32)
                sp = _softplus(z)
                causal = ahead < (i * B - c0)
                log_1mb = jnp.where(causal, -sp, 0.0) if masked else -sp
                parts = [None] * NS
                for b in reversed(range(NS)):
                    blk = log_1mb[:, :, b * B:(b + 1) * B]
                    parts[b] = _split_dot(blk, after) + tail
                    tail = tail + jnp.sum(blk, axis=2, keepdims=True)
                survive = parts[0] if NS == 1 else jnp.concatenate(parts, axis=2)
                wts = jnp.exp(z - sp + survive)
                if masked:
                    wts = jnp.where(causal, wts, 0.0)
                acc = acc + lax.dot_general(wts.astype(BF16), vs[:, cols, :], _B_NN, preferred_element_type=F32)
                return acc, tail

            last = i // NS
            acc, tail = tile(pl.multiple_of(last * KT, KT), jnp.zeros((H, B, HEAD_DIM), F32), jnp.zeros((H, B, 1), F32), True)
            acc, tail = lax.fori_loop(
                1, last + 1, lambda s, c: tile(pl.multiple_of((last - s) * KT, KT), c[0], c[1], False), (acc, tail))
            o_ref[rows, :] = _heads_last(acc).astype(o_ref.dtype)
            lt_ref[rows, :] = _heads_last(jnp.broadcast_to(tail, (H, B, HEAD_DIM)))
            return carry

        lax.fori_loop(0, nb, q_block, 0)

    heads = pl.BlockSpec((T, H * HEAD_DIM), lambda h: (0, h))
    return pl.pallas_call(
        body, name=name, grid=(HEADS // H,), in_specs=[q_spec, k_spec, v_spec, g_spec, g_spec],
        out_specs=(heads, heads),
        out_shape=(jax.ShapeDtypeStruct((T, D_MODEL), BF16), jax.ShapeDtypeStruct((T, D_MODEL), F32)),
        scratch_shapes=[pltpu.VMEM((H, T, HEAD_DIM), BF16)] * 3, compiler_params=_params(("parallel",)),
    )(proj, proj, proj, q_gain, k_gain)


def _sb_bwd(proj, q_gain, k_gain, ltot, do, *, name):
    T = proj.shape[0]
    B = SB_BLOCK
    H = SB_HEADS_BWD
    nb = T // B
    KT = min(SB_KEY_TILE, T)
    NS = KT // B
    q_spec, k_spec, v_spec, g_spec = _sb_specs(T, H)

    def body(q_ref, k_ref, v_ref, gq_ref, gk_ref, lt_ref, do_ref, dq_ref, dk_ref, dv_ref, dgq_ref, dgk_ref,
             qs, ks, vs, dos, lts, dq_acc, dk_acc, dv_acc):
        qn, q_r = _head_rms(_heads_first(q_ref[...]))
        kn, k_r = _head_rms(_heads_first(k_ref[...]))
        qs[...] = (qn * (gq_ref[...] * QK_SCALE)).astype(BF16)
        ks[...] = (kn * gk_ref[...]).astype(BF16)
        vs[...] = _heads_first(v_ref[...]).astype(BF16)
        dos[...] = _heads_first(do_ref[...]).astype(BF16)
        lts[...] = _heads_first(lt_ref[...])
        dk_acc[...] = jnp.zeros_like(dk_acc)
        dv_acc[...] = jnp.zeros_like(dv_acc)
        ii = lax.broadcasted_iota(jnp.int32, (B, B), 0)
        jj = lax.broadcasted_iota(jnp.int32, (B, B), 1)
        upto = jnp.where(ii <= jj, 1.0, 0.0).astype(BF16)
        before = jnp.where(ii < jj, 1.0, 0.0).astype(BF16)
        ahead = lax.broadcasted_iota(jnp.int32, (H, B, KT), 2) - lax.broadcasted_iota(jnp.int32, (H, B, KT), 1)

        def q_block(i, carry):
            rows = pl.ds(pl.multiple_of(i * B, B), B)
            q = qs[:, rows, :]
            d_o = dos[:, rows, :]
            total = jnp.max(lts[:, rows, :], axis=2, keepdims=True)

            def tile(c0, dq, head_lb, head_de, masked):
                cols = pl.ds(c0, KT)
                k = ks[:, cols, :]
                v = vs[:, cols, :]
                z = lax.dot_general(q, k, _B_NT, preferred_element_type=F32)
                sp = _softplus(z)
                causal = ahead < (i * B - c0)
                log_1mb = jnp.where(causal, -sp, 0.0) if masked else -sp
                parts = [None] * NS
                for b in range(NS):
                    blk = log_1mb[:, :, b * B:(b + 1) * B]
                    parts[b] = _split_dot(blk, upto) + head_lb
                    head_lb = head_lb + jnp.sum(blk, axis=2, keepdims=True)
                prefix = parts[0] if NS == 1 else jnp.concatenate(parts, axis=2)
                wts = jnp.exp(z - sp + (total - prefix))
                if masked:
                    wts = jnp.where(causal, wts, 0.0)
                d_w = lax.dot_general(d_o, v, _B_NT, preferred_element_type=F32)
                d_e = wts * d_w
                d_eb = d_e.astype(BF16)
                for b in range(NS):
                    inside = jnp.dot(d_eb[:, :, b * B:(b + 1) * B].reshape(H * B, B), before, preferred_element_type=F32)
                    parts[b] = inside.reshape(H, B, B) + head_de
                    head_de = head_de + jnp.sum(d_e[:, :, b * B:(b + 1) * B], axis=2, keepdims=True)
                cum = parts[0] if NS == 1 else jnp.concatenate(parts, axis=2)
                sig = jnp.exp(z - sp)
                d_z = d_e - sig * (d_e + cum)
                if masked:
                    d_z = jnp.where(causal, d_z, 0.0)
                d_zb = d_z.astype(BF16)
                dq = dq + lax.dot_general(d_zb, k, _B_NN, preferred_element_type=F32)
                dk_acc[:, cols, :] += lax.dot_general(d_zb, q, _B_TN, preferred_element_type=F32)
                dv_acc[:, cols, :] += lax.dot_general(wts.astype(BF16), d_o, _B_TN, preferred_element_type=F32)
                return dq, head_lb, head_de

            last = i // NS
            zero = jnp.zeros((H, B, 1), F32)
            state = lax.fori_loop(0, last, lambda t, c: tile(pl.multiple_of(t * KT, KT), *c, False),
                                  (jnp.zeros((H, B, HEAD_DIM), F32), zero, zero))
            dq, _, _ = tile(pl.multiple_of(last * KT, KT), *state, True)
            dq_acc[:, rows, :] = dq * QK_SCALE
            return carry

        lax.fori_loop(0, nb, q_block, 0)

        def norm_bwd(d_scaled, n, r, gain):
            dn = d_scaled * gain
            d_gain = jnp.sum(jnp.sum(d_scaled * n, axis=1, keepdims=True), axis=0)
            return r * (dn - n * jnp.mean(dn * n, axis=-1, keepdims=True)), d_gain

        dq_raw, dgq = norm_bwd(dq_acc[...], qn, q_r, gq_ref[...])
        dk_raw, dgk = norm_bwd(dk_acc[...], kn, k_r, gk_ref[...])
        dq_ref[...] = _heads_last(dq_raw).astype(dq_ref.dtype)
        dk_ref[...] = _heads_last(dk_raw).astype(dk_ref.dtype)
        dv_ref[...] = _heads_last(dv_acc[...]).astype(dv_ref.dtype)

        @pl.when(pl.program_id(0) == 0)
        def _():
            dgq_ref[...] = jnp.zeros_like(dgq_ref)
            dgk_ref[...] = jnp.zeros_like(dgk_ref)

        dgq_ref[...] += dgq
        dgk_ref[...] += dgk

    heads = pl.BlockSpec((T, H * HEAD_DIM), lambda h: (0, h))
    out = jax.ShapeDtypeStruct((T, D_MODEL), BF16)
    vec = jax.ShapeDtypeStruct((1, HEAD_DIM), F32)
    return pl.pallas_call(
        body, name=name, grid=(HEADS // H,), in_specs=[q_spec, k_spec, v_spec, g_spec, g_spec, heads, heads],
        out_specs=(heads, heads, heads, g_spec, g_spec), out_shape=(out, out, out, vec, vec),
        scratch_shapes=[pltpu.VMEM((H, T, HEAD_DIM), BF16)] * 4 + [pltpu.VMEM((H, T, HEAD_DIM), F32)] * 4,
        compiler_params=_params(("arbitrary",)),
    )(proj, proj, proj, q_gain, k_gain, ltot, do)


ADAM_ROWS = 256


def _adamw(g_parts, w, m, v, *, name, layer=0, earlier=None):
    K, A, C = g_parts.shape
    R = w.shape[0]
    tr = next((t for t in (ADAM_ROWS, ADAM_ROWS // 2) if A % t == 0), A // 2 if A % 32 == 0 else A)
    first_block = layer * (A // tr)

    def body(g_ref, w_ref, m_ref, v_ref, *rest):
        go_ref, d_ref, mo_ref, vo_ref = rest[-4:]
        g = g_ref[0].astype(F32)
        for k in range(1, K):
            g = g + g_ref[k].astype(F32)
        go_ref[...] = g
        m_new = ADAM_B1 * m_ref[...] + (1.0 - ADAM_B1) * g
        v_new = ADAM_B2 * v_ref[...] + (1.0 - ADAM_B2) * (g * g)
        m_hat = m_new / (1.0 - ADAM_B1 ** ADAM_STEP)
        v_hat = v_new / (1.0 - ADAM_B2 ** ADAM_STEP)
        d_ref[...] = -ADAM_LR * (m_hat / (jnp.sqrt(v_hat) + ADAM_EPS) + ADAM_WD * w_ref[...])
        mo_ref[...] = m_new
        vo_ref[...] = v_new

    row = pl.BlockSpec((tr, C), lambda i: (first_block + i, 0))
    out = jax.ShapeDtypeStruct((R, C), F32)
    in_specs = [pl.BlockSpec((K, tr, C), lambda i: (0, i, 0)), row, row, row]
    if earlier is None:
        return pl.pallas_call(
            body, name=name, grid=(A // tr,), in_specs=in_specs, out_specs=(row, row, row, row),
            out_shape=(out, out, out, out), compiler_params=_params(("parallel",)),
        )(g_parts, w, m, v)
    return pl.pallas_call(
        body, name=name, grid=(A // tr,), in_specs=in_specs + [ANY] * 4, out_specs=(row, row, row, row),
        out_shape=(out, out, out, out), input_output_aliases={4 + j: j for j in range(4)},
        compiler_params=_params(("parallel",)),
    )(g_parts, w, m, v, *earlier)


def _sum_parts(parts, *, name):
    K, R, C = parts.shape

    def body(p_ref, o_ref):
        acc = p_ref[0]
        for k in range(1, K):
            acc = acc + p_ref[k]
        o_ref[...] = acc

    return pl.pallas_call(body, name=name, out_shape=jax.ShapeDtypeStruct((R, C), F32))(parts)


def _position():
    return lax.axis_index("x"), lax.axis_index("y"), lax.axis_index("c")


def _all_gather(shards, *, name):
    n = len(shards)

    def body(*refs):
        x_refs, out_refs = refs[:n], refs[n:2 * n]
        send_sems, recv_sems, local_sems = refs[2 * n:]
        x, y, c = _position()
        me, sibling = (x, y, c), (x, y, 1 - c)
        chips = [(1 - x, y), (x, 1 - y), (1 - x, 1 - y)]

        def slot(a, px, py, pc):
            return out_refs[a].at[4 * px + 2 * py + pc]

        def copy(a, k, block, to, own=False):
            return pltpu.make_async_remote_copy(
                src_ref=x_refs[a] if own else slot(a, *block), dst_ref=slot(a, *block),
                send_sem=send_sems.at[a, k], recv_sem=recv_sems.at[a, k], device_id=to, device_id_type=MESH)

        mine = [pltpu.make_async_copy(x_refs[a], slot(a, *me), local_sems.at[a]) for a in range(n)]
        for cp in mine:
            cp.start()
        first = [copy(a, 1 + j, me, (*chip, c), own=True) for j, chip in enumerate(chips) for a in range(n)]
        first += [copy(a, 0, me, sibling, own=True) for a in range(n)]
        for cp in first:
            cp.start()
        passed = []
        for j, chip in enumerate(chips):
            for a in range(n):
                copy(a, 1 + j, (*chip, c), me).wait_recv()
                passed.append(copy(a, 4 + j, (*chip, c), sibling))
                passed[-1].start()
        for a in range(n):
            copy(a, 0, sibling, me).wait_recv()
        for j, chip in enumerate(chips):
            for a in range(n):
                copy(a, 4 + j, (*chip, 1 - c), me).wait_recv()
        for cp in first + passed:
            cp.wait_send()
        for cp in mine:
            cp.wait()

    return pl.pallas_call(
        body, name=name, in_specs=[ANY] * n, out_specs=[ANY] * n,
        out_shape=[jax.ShapeDtypeStruct((N_DEV,) + s.shape, s.dtype) for s in shards],
        scratch_shapes=[pltpu.SemaphoreType.DMA((n, 7)), pltpu.SemaphoreType.DMA((n, 7)), pltpu.SemaphoreType.DMA((n,))],
    )(*shards)


HBM = pl.BlockSpec(memory_space=pltpu.HBM)
SEM = pl.BlockSpec(memory_space=pltpu.SEMAPHORE)
DATAFLOW = pltpu.SideEffectType.DATAFLOW_SIDE_EFFECTING


def _exchange_copies(gather, x_refs, land_refs, send_sems, recv_sems, local_sems):
    n = len(x_refs)
    x, y, c = _position()
    me = 4 * x + 2 * y + c

    def src(a, slot):
        return x_refs[a] if gather else x_refs[a].at[slot]

    mine = [pltpu.make_async_copy(src(a, me), land_refs[a].at[me], local_sems.at[a]) for a in range(n)]
    sends, recvs = [], []
    for k in range(1, N_DEV):
        px, py, pc = (x + (k >> 2)) % 2, (y + ((k >> 1) & 1)) % 2, (c + (k & 1)) % 2
        peer = 4 * px + 2 * py + pc
        for a in range(n):
            sems = dict(send_sem=send_sems.at[7 * a + k - 1], recv_sem=recv_sems.at[7 * a + k - 1],
                        device_id=(px, py, pc), device_id_type=MESH)
            sends.append(pltpu.make_async_remote_copy(src_ref=src(a, peer), dst_ref=land_refs[a].at[me], **sems))
            recvs.append(pltpu.make_async_remote_copy(src_ref=src(a, me), dst_ref=land_refs[a].at[peer], **sems))
    return mine, sends, recvs


def _exchange_start(parts, *, gather, name):
    n = len(parts)

    def body(*refs):
        x_refs, land_refs = refs[:n], refs[n:2 * n]
        send_sems, recv_sems, local_sems = refs[2 * n:2 * n + 3]
        token = refs[-1]
        mine, sends, _ = _exchange_copies(gather, x_refs, land_refs, send_sems, recv_sems, local_sems)
        for cp in mine + sends:
            cp.start()
        token[...] = jnp.zeros_like(token)

    sems = (pltpu.SemaphoreType.DMA((7 * n,)), pltpu.SemaphoreType.DMA((7 * n,)), pltpu.SemaphoreType.DMA((n,)))
    thru = tuple(pltpu.HBM(p.shape, p.dtype) for p in parts)
    land = tuple(pltpu.HBM(((N_DEV,) if gather else ()) + p.shape, p.dtype) for p in parts)
    res = pl.pallas_call(
        body, name=name, in_specs=[HBM] * (2 * n),
        out_specs=(SEM, SEM, SEM) + (HBM,) * (2 * n) + (pl.BlockSpec(memory_space=pltpu.VMEM),),
        out_shape=sems + thru + land + (jax.ShapeDtypeStruct((8, 128), F32),),
        input_output_aliases={a: 3 + a for a in range(2 * n)},
        compiler_params=pltpu.CompilerParams(has_side_effects=DATAFLOW),
    )(*[pltpu.with_memory_space_constraint(p, pltpu.HBM) for p in parts],
      *[pltpu.with_memory_space_constraint(lax.empty(z.shape, z.dtype), pltpu.HBM) for z in land])
    return res[:3], res[3:3 + n], res[3 + n:3 + 2 * n], res[-1]


def _exchange_wait(sems, parts, landing, after, *, gather, name):
    n = len(parts)
    after = list(after)

    def body(*refs):
        x_refs, land_refs = refs[:n], refs[n:2 * n]
        send_sems, recv_sems, local_sems = refs[2 * n:2 * n + 3]
        token = refs[-1]
        mine, sends, recvs = _exchange_copies(gather, x_refs, land_refs, send_sems, recv_sems, local_sems)
        for cp in recvs:
            cp.wait_recv()
        for cp in sends:
            cp.wait_send()
        for cp in mine:
            cp.wait()
        token[...] = jnp.zeros_like(token)

    thru = tuple(pltpu.HBM(p.shape, p.dtype) for p in tuple(parts) + tuple(landing))
    res = pl.pallas_call(
        body, name=name, in_specs=[HBM] * (2 * n) + [SEM, SEM, SEM] + [ANY] * len(after),
        out_specs=(HBM,) * (2 * n) + (pl.BlockSpec(memory_space=pltpu.VMEM),),
        out_shape=thru + (jax.ShapeDtypeStruct((8, 128), F32),), input_output_aliases={a: a for a in range(2 * n)},
        compiler_params=pltpu.CompilerParams(has_side_effects=DATAFLOW),
    )(*parts, *landing, *sems, *after)
    return res[n:2 * n], res[-1]


def _ffn_fwd(x, gain, wg_in, wg_out, tag):
    T, D = x.shape
    fb, rb = wg_in.shape[-1], wg_out.shape[-2]
    tm, tn = min(T, 1024), 512
    h = _rmsnorm_fwd(x, gain, name=f"{tag}_norm")
    p = _mm(name=f"{tag}_in", grid=(T // tm, N_DEV, 1), tile=(tm, fb),
            a=h, a_spec=pl.BlockSpec((tm, D), lambda i, j, k: (i, 0)),
            b=wg_in, b_spec=pl.BlockSpec((None, D, fb), lambda i, j, k: (j, 0, 0)),
            out_shape=jax.ShapeDtypeStruct((N_DEV, T, fb), BF16), o_spec=pl.BlockSpec((None, tm, fb), lambda i, j, k: (j, i, 0)))
    a = _swiglu_fwd(p, name=f"{tag}_act")
    y = _mm(name=f"{tag}_out", grid=(T // tm, 1, FF_HALF), tile=(tm, D), resid=x, scale=0.5,
            a=a, a_spec=pl.BlockSpec((None, tm, fb), lambda i, j, k: (k, i, 0)),
            b=wg_out.reshape(N_DEV * rb, D), b_spec=pl.BlockSpec((fb, D), lambda i, j, k: (k, 0)),
            out_shape=jax.ShapeDtypeStruct((T, D), F32), o_spec=pl.BlockSpec((tm, D), lambda i, j, k: (i, 0)))
    return y, (x, h, p, a)


def _ffn_bwd(dy, saved, gain, wg_in, wg_out, tag, on_weight_grads=None):
    x, h, p, a = saved
    T, D = x.shape
    fb, rb = wg_in.shape[-1], wg_out.shape[-2]
    tm, tn = min(T, 1024), 512
    da = _mm(name=f"{tag}_out_dx", grid=(T // tm, FF_HALF, 1), tile=(tm, fb), tb=True, scale=0.5,
             a=dy, a_spec=pl.BlockSpec((tm, D), lambda i, j, k: (i, 0)),
             b=wg_out.reshape(N_DEV * rb, D), b_spec=pl.BlockSpec((fb, D), lambda i, j, k: (j, 0)),
             out_shape=jax.ShapeDtypeStruct((FF_HALF, T, fb), BF16), o_spec=pl.BlockSpec((None, tm, fb), lambda i, j, k: (j, i, 0)))
    d_w_out = _mm(name=f"{tag}_out_dw", grid=(FF_HALF, D // tn, 1), tile=(fb, tn), ta=True, scale=0.5,
                  a=a, a_spec=pl.BlockSpec((None, T, fb), lambda i, j, k: (i, 0, 0)),
                  b=dy, b_spec=pl.BlockSpec((T, tn), lambda i, j, k: (0, j)),
                  out_shape=jax.ShapeDtypeStruct((FF_HALF, fb, D), BF16), o_spec=pl.BlockSpec((None, fb, tn), lambda i, j, k: (i, 0, j)))
    dp = _swiglu_bwd(da, p, name=f"{tag}_act_bwd")
    d_w_in = _mm(name=f"{tag}_in_dw", grid=(1, N_DEV, 1), tile=(D, fb), ta=True,
                 a=h, a_spec=pl.BlockSpec((T, D), lambda i, j, k: (0, 0)),
                 b=dp, b_spec=pl.BlockSpec((None, T, fb), lambda i, j, k: (j, 0, 0)),
                 out_shape=jax.ShapeDtypeStruct((N_DEV, D, fb), BF16), o_spec=pl.BlockSpec((None, D, fb), lambda i, j, k: (j, 0, 0)))
    dh = _mm(name=f"{tag}_in_dx", grid=(T // tm, 1, N_DEV), tile=(tm, D), tb=True,
             a=dp, a_spec=pl.BlockSpec((None, tm, fb), lambda i, j, k: (k, i, 0)),
             b=wg_in, b_spec=pl.BlockSpec((None, D, fb), lambda i, j, k: (k, 0, 0)),
             out_shape=jax.ShapeDtypeStruct((T, D), F32), o_spec=pl.BlockSpec((tm, D), lambda i, j, k: (i, 0)))
    d_w_out = d_w_out.reshape(N_DEV, rb, D)
    if on_weight_grads is not None:
        gain = gain + on_weight_grads(d_w_in, d_w_out)[0, 0]
    dx, d_gain = _rmsnorm_bwd(dh, x, gain, dy, name=f"{tag}_norm_bwd")
    return dx, d_gain, d_w_in, d_w_out


def _square_mm(a, wg, *, name, transposed=False, out_dtype=F32, resid=None):
    T, D = a.shape
    w = wg.reshape(D, D)
    return _matmul(a, w, tb=transposed, name=name, out_dtype=out_dtype, resid=resid)


def _head_rows(cols, T):
    return cols.T.reshape(HEADS, T // DN_CHUNK, 1, DN_CHUNK)


def _mixer_fwd(x, w, big, tag):
    T = x.shape[0]
    h = _rmsnorm_fwd(x, w["mix_norm"], name=f"{tag}_norm")
    proj = _matmul(h, big["w_main"], name=f"{tag}_proj")
    scal = _matmul(h, big["w_scal"], name=f"{tag}_proj_scal", tn=N_SCAL)
    qkv = _conv_fwd(proj, big["conv_w"], name=f"{tag}_conv")
    b_rows = _head_rows(scal[:, 0:HEADS], T)
    a_rows = _head_rows(scal[:, HEADS:2 * HEADS], T)
    o_a, *states = _dn_fwd(qkv, b_rows, a_rows, w["hp"], name=f"{tag}_dn")
    oa_n = _gated_norm_fwd(o_a, proj, w["dn_out_norm"], name=f"{tag}_dn_norm")
    ya = _square_mm(oa_n, big["w_branch_a"], name=f"{tag}_branch_a")
    o_b, ltot = _sb_fwd(proj, w["sb_q_norm"], w["sb_k_norm"], name=f"{tag}_sb")
    yb = _square_mm(o_b, big["w_branch_b"], name=f"{tag}_branch_b")
    merged = _merge_fwd(ya, yb, proj, name=f"{tag}_merge")
    y = _square_mm(merged, big["w_out"], name=f"{tag}_out", resid=x)
    return y, (x, h, proj, qkv, b_rows, a_rows, o_a, states, oa_n, ya, o_b, ltot, yb, merged)


def _mixer_bwd(dy, saved, w, big, tag, on_weight_grads):
    x, h, proj, qkv, b_rows, a_rows, o_a, states, oa_n, ya, o_b, ltot, yb, merged = saved
    T = x.shape[0]
    g = {}
    d_merged = _square_mm(dy, big["w_out"], transposed=True, name=f"{tag}_out_dx", out_dtype=BF16)
    g["w_out"] = _matmul(merged, dy, ta=True, name=f"{tag}_out_dw", out_dtype=BF16)
    d_ya, d_yb, d_ga, d_gb = _merge_bwd(d_merged, ya, yb, proj, name=f"{tag}_merge_bwd")
    d_oan = _square_mm(d_ya, big["w_branch_a"], transposed=True, name=f"{tag}_branch_a_dx")
    g["w_branch_a"] = _matmul(oa_n, d_ya, ta=True, name=f"{tag}_branch_a_dw", out_dtype=BF16)
    d_ob = _square_mm(d_yb, big["w_branch_b"], transposed=True, name=f"{tag}_branch_b_dx")
    g["w_branch_b"] = _matmul(o_b, d_yb, ta=True, name=f"{tag}_branch_b_dw", out_dtype=BF16)
    d_oa, d_z, g["dn_out_norm"] = _gated_norm_bwd(d_oan, o_a, proj, w["dn_out_norm"], name=f"{tag}_dn_norm_bwd")
    d_qkv, d_b_rows, d_a_rows, d_hp = _dn_bwd(qkv, b_rows, a_rows, w["hp"], *states, d_oa, name=f"{tag}_dn_bwd")
    g["dn_a_log"] = d_hp[:, 0, 0]
    g["dn_dt_bias"] = d_hp[:, 1, 0]
    d_conv_in, g["conv_w"] = _conv_bwd(d_qkv, proj, big["conv_w"], name=f"{tag}_conv_bwd")
    d_sbq, d_sbk, d_sbv, g["sb_q_norm"], g["sb_k_norm"] = _sb_bwd(
        proj, w["sb_q_norm"], w["sb_k_norm"], ltot, d_ob, name=f"{tag}_sb_bwd")
    d_proj = jnp.concatenate([d_conv_in, d_z, d_sbq, d_sbk, d_sbv, d_ga, d_gb], axis=1)
    d_scal = jnp.concatenate([d_b_rows.reshape(HEADS, T).T, d_a_rows.reshape(HEADS, T).T,
                              jnp.zeros((T, N_SCAL - 2 * HEADS), F32)], axis=1).astype(BF16)
    g["w_main"] = _matmul(h, d_proj, ta=True, name=f"{tag}_proj_dw", out_dtype=BF16)
    g["w_scal"] = _matmul(h, d_scal, ta=True, name=f"{tag}_proj_scal_dw", out_dtype=BF16, tn=N_SCAL)
    dh_scal = _matmul(d_scal, big["w_scal"], tb=True, name=f"{tag}_proj_scal_dx")
    dh = _matmul(d_proj, big["w_main"], tb=True, name=f"{tag}_proj_dx", tk=N_MAIN // 4, resid=dh_scal)
    gain = w["mix_norm"] + on_weight_grads(g)[0, 0]
    dx, g["mix_norm"] = _rmsnorm_bwd(dh, x, gain, dy, name=f"{tag}_norm_bwd")
    return dx, g


def _local_step(x, target, layers, weights_of, on_weight_grads):
    saved, bigs = [], []
    for l, w in enumerate(layers):
        big = weights_of(l, x)
        x, s1 = _ffn_fwd(x, w["ffn1_norm"] + big["started"], big["ffn1_w_in"], big["ffn1_w_out"], f"l{l}_ffn1")
        x, s2 = _mixer_fwd(x, w, big, f"l{l}_mix")
        x, s3 = _ffn_fwd(x, w["ffn2_norm"], big["ffn2_w_in"], big["ffn2_w_out"], f"l{l}_ffn2")
        saved.append((s1, s2, s3))
        bigs.append(big)
    loss, dx = _loss_head(x, target, name="loss_head")
    small = [None] * len(layers)
    for l in reversed(range(len(layers))):
        w, big = layers[l], bigs[l]
        s1, s2, s3 = saved[l]
        dx, g_n2, _, _ = _ffn_bwd(
            dx, s3, w["ffn2_norm"], big["ffn2_w_in"], big["ffn2_w_out"], f"l{l}_ffn2",
            on_weight_grads=lambda g_in, g_out, l=l: on_weight_grads(l, 0, dict(ffn2_w_in=g_in, ffn2_w_out=g_out)))
        dx, g = _mixer_bwd(dx, s2, w, big, f"l{l}_mix", on_weight_grads=lambda g, l=l: on_weight_grads(l, 1, g))
        dx, g_n1, _, _ = _ffn_bwd(
            dx, s1, w["ffn1_norm"], big["ffn1_w_in"], big["ffn1_w_out"], f"l{l}_ffn1",
            on_weight_grads=lambda g_in, g_out, l=l: on_weight_grads(l, 2, dict(ffn1_w_in=g_in, ffn1_w_out=g_out)))
        small[l] = dict(g, ffn1_norm=g_n1, ffn2_norm=g_n2)
    return loss, dx, small


_BIG = ("ffn1_w_in", "ffn1_w_out", "w_in", "w_branch_a", "w_branch_b", "w_out", "ffn2_w_in", "ffn2_w_out")
_STAGES = (("ffn2_w_in", "ffn2_w_out"), ("w_in", "w_branch_a", "w_branch_b", "w_out"), ("ffn1_w_in", "ffn1_w_out"))
_SMALL = ("ffn1_norm", "mix_norm", "ffn2_norm", "dn_a_log", "dn_dt_bias", "dn_out_norm", "sb_q_norm", "sb_k_norm")
_ORDER = ("ffn1_norm", "ffn1_w_in", "ffn1_w_out", "mix_norm", "w_in", "dn_conv_w", "dn_a_log", "dn_dt_bias", "dn_out_norm",
          "sb_q_norm", "sb_k_norm", "w_branch_a", "w_branch_b", "w_out", "ffn2_norm", "ffn2_w_in", "ffn2_w_out")
COL_SCAL = 4 * D_MODEL
SCAL_SLOT = COL_SCAL // (N_IN // N_DEV)
SCAL_AT = COL_SCAL % (N_IN // N_DEV)
assert SCAL_AT + 2 * HEADS <= N_IN // N_DEV


def _pad_rows(a, multiple):
    pad = (-a.shape[-2]) % multiple
    return a if pad == 0 else jnp.pad(a, [(0, 0)] * (a.ndim - 2) + [(0, pad), (0, 0)])


def _lane_rows(a):
    flat = a.reshape(-1)
    flat = jnp.pad(flat, (0, (-flat.shape[0]) % 128))
    return flat.reshape(-1, 128)


def _pack_small(named):
    pieces, spans, r = [], {}, 0
    for n, a in named:
        rows = _lane_rows(a)
        spans[n] = (r, r + rows.shape[0], a.shape)
        r += rows.shape[0]
        pieces.append(rows)
    return _pad_rows(jnp.concatenate(pieces, axis=0), 8), spans


def _unpack_small(packed, spans, n):
    r0, r1, shape = spans[n]
    return packed[r0:r1].reshape(-1)[:math.prod(shape)].reshape(shape)


def kernel(x, ffn1_norm, ffn1_w_in, ffn1_w_out, mix_norm, w_in, dn_conv_w, dn_a_log, dn_dt_bias, dn_out_norm, sb_q_norm, sb_k_norm, w_branch_a, w_branch_b, w_out, ffn2_norm, ffn2_w_in, ffn2_w_out, loss_target, m_ffn1_norm, m_ffn1_w_in, m_ffn1_w_out, m_mix_norm, m_w_in, m_dn_conv_w, m_dn_a_log, m_dn_dt_bias, m_dn_out_norm, m_sb_q_norm, m_sb_k_norm, m_w_branch_a, m_w_branch_b, m_w_out, m_ffn2_norm, m_ffn2_w_in, m_ffn2_w_out, v_ffn1_norm, v_ffn1_w_in, v_ffn1_w_out, v_mix_norm, v_w_in, v_dn_conv_w, v_dn_a_log, v_dn_dt_bias, v_dn_out_norm, v_sb_q_norm, v_sb_k_norm, v_w_branch_a, v_w_branch_b, v_w_out, v_ffn2_norm, v_ffn2_w_in, v_ffn2_w_out):
    given = dict(locals())
    weights = {n: given[n] for n in _ORDER}
    mom_m = {n: given["m_" + n] for n in _ORDER}
    mom_v = {n: given["v_" + n] for n in _ORDER}
    L = ffn1_norm.shape[0]
    ax, ay, ac = _position()
    my_slot = 4 * ax + 2 * ay + ac

    conv_cols = dn_conv_w.shape[-1]
    shards = lambda l, zero: [(weights[n][l] + zero).astype(BF16) for n in _BIG]
    *first_layer, conv_full = _all_gather(shards(0, 0.0) + [_pad_rows(_lane_rows(dn_conv_w), 8)], name="gather_l0")
    conv_full = conv_full.reshape(N_DEV, -1)[:, :L * DN_CONV * conv_cols].reshape(N_DEV, L, DN_CONV, conv_cols)
    conv_full = conv_full.transpose(1, 2, 0, 3).reshape(L, DN_CONV, N_DEV * conv_cols)
    arriving = {}

    def start_gather(l, zero):
        if l >= L:
            return 0.0
        *arriving[l], token = _exchange_start(shards(l, zero), gather=True, name=f"gather_start_l{l}")
        return token[0, 0]

    def weights_of(l, x_in):
        if l == 0:
            arrays, started = first_layer, start_gather(1, 0.0)
        else:
            arrays, token = _exchange_wait(*arriving[l], [x_in], gather=True, name=f"gather_wait_l{l}")
            started = start_gather(l + 1, token[0, 0])
        big = dict(zip(_BIG, arrays), started=started)
        wi = big.pop("w_in")
        pieces = [wi[d] for d in range(N_DEV)]
        pieces[SCAL_SLOT:SCAL_SLOT + 1] = [wi[SCAL_SLOT][:, :SCAL_AT], wi[SCAL_SLOT][:, SCAL_AT + 2 * HEADS:]]
        big["w_main"] = jnp.concatenate(pieces, axis=1)
        big["w_scal"] = jnp.pad(wi[SCAL_SLOT][:, SCAL_AT:SCAL_AT + 2 * HEADS], ((0, 0), (0, N_SCAL - 2 * HEADS)))
        big["conv_w"] = conv_full[l]
        return big

    layers = []
    for l in range(L):
        hp = jnp.concatenate([jnp.broadcast_to(dn_a_log[l][:, None, None], (HEADS, 1, 128)),
                              jnp.broadcast_to(dn_dt_bias[l][:, None, None], (HEADS, 1, 128)),
                              jnp.zeros((HEADS, 6, 128), F32)], axis=1)
        layers.append(dict(ffn1_norm=ffn1_norm[l][None], mix_norm=mix_norm[l][None], hp=hp,
                           dn_out_norm=dn_out_norm[l][None], sb_q_norm=sb_q_norm[l][None],
                           sb_k_norm=sb_k_norm[l][None], ffn2_norm=ffn2_norm[l][None]))

    in_flight = {}

    def on_weight_grads(l, stage, g):
        parts = dict(g)
        if stage == 1:
            gm, shard = g["w_main"], N_IN // N_DEV
            blocks = [gm[:, d * shard:(d + 1) * shard] for d in range(SCAL_SLOT)]
            blocks.append(jnp.concatenate([gm[:, SCAL_SLOT * shard:COL_SCAL], g["w_scal"][:, :2 * HEADS],
                                           gm[:, COL_SCAL:(SCAL_SLOT + 1) * shard - 2 * HEADS]], axis=1))
            blocks += [gm[:, d * shard - 2 * HEADS:(d + 1) * shard - 2 * HEADS] for d in range(SCAL_SLOT + 1, N_DEV)]
            parts["w_in"] = jnp.stack(blocks)
            for n in ("w_branch_a", "w_branch_b", "w_out"):
                parts[n] = g[n].reshape(N_DEV, D_MODEL // N_DEV, D_MODEL)
        *in_flight[l, stage], token = _exchange_start([parts[n] for n in _STAGES[stage]], gather=False,
                                                      name=f"scatter_start_l{l}_{stage}")
        return token

    loss_row, dx, grads = _local_step(x[0], loss_target[0], layers, weights_of, on_weight_grads)
    loss = lax.psum(loss_row[0, 0], ("x", "y", "c"))

    results = {n: None for n in _BIG}
    after = [dx]
    for l in reversed(range(L)):
        for stage, names in enumerate(_STAGES):
            landed, _ = _exchange_wait(*in_flight[l, stage], after, gather=False, name=f"scatter_wait_l{l}_{stage}")
            for n, parts in zip(names, landed):
                _, a, b = weights[n].shape
                results[n] = _adamw(parts, weights[n].reshape(L * a, b), mom_m[n].reshape(L * a, b),
                                    mom_v[n].reshape(L * a, b), layer=l, earlier=results[n], name=f"adamw_{n}_l{l}")
            after = [results[n][0] for n in names]
    out = {n: tuple(t.reshape(weights[n].shape) for t in results[n]) for n in _BIG}

    small_grads = [(n, jnp.stack([g[n].reshape(weights[n].shape[1:]) for g in grads])) for n in _SMALL]
    small_packed, spans = _pack_small(small_grads + [("conv", jnp.stack([g["conv_w"] for g in grads]))])
    small_sum = _sum_parts(_all_gather([small_packed], name="gather_small_grads")[0], name="sum_small_grads")
    rep_rows = spans["conv"][0]
    pack_rep = lambda d: _pad_rows(_pack_small([(n, d[n]) for n in _SMALL])[0], 8)
    rep_pad = (-rep_rows) % 8
    g_rep = jnp.pad(small_sum[:rep_rows], ((0, rep_pad), (0, 0)))
    res = _adamw(g_rep[None], pack_rep(weights), pack_rep(mom_m), pack_rep(mom_v), name="adamw_replicated")
    for n in _SMALL:
        out[n] = tuple(_unpack_small(t, spans, n) for t in res)
    conv_sum = _unpack_small(small_sum, spans, "conv")
    conv_mine = lax.dynamic_slice_in_dim(conv_sum, my_slot * conv_cols, conv_cols, axis=2).reshape(L * DN_CONV, conv_cols)
    flat = lambda t: t.reshape(L * DN_CONV, conv_cols)
    res = _adamw(conv_mine[None], flat(dn_conv_w), flat(m_dn_conv_w), flat(v_dn_conv_w), name="adamw_conv")
    out["dn_conv_w"] = tuple(t.reshape(L, DN_CONV, conv_cols) for t in res)

    return (loss, dx[None], *[out[n][0] for n in _ORDER], *[out[n][1] for n in _ORDER],
            *[out[n][2] for n in _ORDER], *[out[n][3] for n in _ORDER])
```

```python
import functools
import math

import jax
import jax.numpy as jnp
from jax import lax
from jax.experimental import pallas as pl
from jax.experimental.pallas import tpu as pltpu

F32 = jnp.float32
BF16 = jnp.bfloat16

N_DEV = 8
D_MODEL = 1024
DEPTH = 4
D_FF = 2816
HEADS = 8
HEAD_DIM = 128
DN_CHUNK = 64
DN_CONV = 4
DN_GROUP = 8
DN_HEADS = 2
SB_BLOCK = 128
SB_KEY_TILE = 512
SB_HEADS = 4
SB_HEADS_BWD = 2
SB_KEY_TILE_BWD = 512
RMS_EPS = 1e-6
L2_EPS = 1e-6
N_IN = 9232
N_MAIN = 9216
N_SCAL = 128
QK_SCALE = HEAD_DIM ** -0.5

ADAM_LR = 0.001
ADAM_B1 = 0.9
ADAM_B2 = 0.999
ADAM_EPS = 1e-08
ADAM_WD = 0.01
ADAM_STEP = 10

V7X_VMEM_LIMIT = 56 * 1024 * 1024
MESH = pl.DeviceIdType.MESH
ANY = pl.BlockSpec(memory_space=pl.ANY)


def _params(sem=None, vmem=V7X_VMEM_LIMIT):
    return pltpu.CompilerParams(dimension_semantics=sem, vmem_limit_bytes=vmem)


def _sigmoid(x):
    return 1.0 / (1.0 + jnp.exp(-x))


def _softplus(x):
    return jnp.maximum(x, 0.0) + jnp.log(1.0 + jnp.exp(-jnp.abs(x)))


def _bdot(a, b, dims=(((1,), (0,)), ((), ()))):
    return lax.dot_general(a.astype(BF16), b.astype(BF16), dims, preferred_element_type=F32)


_NT = (((1,), (1,)), ((), ()))
_TN = (((0,), (0,)), ((), ()))


def _hdot(a, b, dims=(((1,), (0,)), ((), ()))):
    a_hi = a.astype(BF16)
    b_hi = b.astype(BF16)
    a_lo = (a - a_hi.astype(F32)).astype(BF16)
    b_lo = (b - b_hi.astype(F32)).astype(BF16)
    dot = functools.partial(lax.dot_general, dimension_numbers=dims, preferred_element_type=F32)
    return dot(a_hi, b_hi) + (dot(a_hi, b_lo) + dot(a_lo, b_hi))


def _hdot_tn(a, b):
    return _hdot(a, b, _TN)


def _mm(*, name, grid, a, a_spec, b, b_spec, out_shape, o_spec, tile, ta=False, tb=False, resid=None, scale=1.0):
    nk = grid[2]
    dims = (((0 if ta else 1,), (1 if tb else 0,)), ((), ()))

    def flat(v):
        return v if v.ndim == 2 else v.reshape(-1, v.shape[-1])

    def body(*refs):
        a_ref, b_ref = refs[:2]
        r_ref = refs[2] if resid is not None else None
        o_ref = refs[3] if resid is not None else refs[2]
        part = lax.dot_general(flat(a_ref[...]).astype(BF16), flat(b_ref[...]).astype(BF16), dims,
                               preferred_element_type=F32)

        def finish(acc):
            if scale != 1.0:
                acc = acc * scale
            if r_ref is not None:
                acc = r_ref[...] + acc
            o_ref[...] = acc.astype(o_ref.dtype)

        if nk == 1:
            finish(part)
        else:
            acc_ref = refs[-1]
            k = pl.program_id(2)

            @pl.when(k == 0)
            def _():
                acc_ref[...] = part

            @pl.when(k > 0)
            def _():
                acc_ref[...] += part

            @pl.when(k == nk - 1)
            def _():
                finish(acc_ref[...])

    in_specs = [a_spec, b_spec] + ([pl.BlockSpec(tile, lambda i, j, k: (i, j))] if resid is not None else [])
    args = (a, b) + ((resid,) if resid is not None else ())
    return pl.pallas_call(
        body, name=name, grid=grid, in_specs=in_specs, out_specs=o_spec, out_shape=out_shape,
        scratch_shapes=[pltpu.VMEM(tile, F32)] if nk > 1 else [],
        compiler_params=_params(("parallel", "parallel", "arbitrary")),
    )(*args)


def _matmul(a, b, *, name, ta=False, tb=False, out_dtype=F32, tm=None, tn=None, tk=None, resid=None, scale=1.0):
    if ta:
        K, M = a.shape
    else:
        M, K = a.shape
    N = b.shape[0] if tb else b.shape[1]
    tm = tm or min(M, 1024)
    tn = tn or min(N, 512)
    tk = tk or K
    assert M % tm == 0 and N % tn == 0 and K % tk == 0, (name, M, N, K, tm, tn, tk)
    a_spec = pl.BlockSpec((tk, tm), lambda i, j, k: (k, i)) if ta else pl.BlockSpec((tm, tk), lambda i, j, k: (i, k))
    b_spec = pl.BlockSpec((tn, tk), lambda i, j, k: (j, k)) if tb else pl.BlockSpec((tk, tn), lambda i, j, k: (k, j))
    return _mm(name=name, grid=(M // tm, N // tn, K // tk), a=a, a_spec=a_spec, b=b, b_spec=b_spec,
               out_shape=jax.ShapeDtypeStruct((M, N), out_dtype), o_spec=pl.BlockSpec((tm, tn), lambda i, j, k: (i, j)),
               tile=(tm, tn), ta=ta, tb=tb, resid=resid, scale=scale)


ROW_TILE = 256


def _rmsnorm_fwd(x, gain, *, name):
    T, D = x.shape

    def body(x_ref, g_ref, o_ref):
        xf = x_ref[...]
        r = lax.rsqrt(jnp.mean(xf * xf, axis=-1, keepdims=True) + RMS_EPS)
        o_ref[...] = (xf * r * g_ref[...]).astype(o_ref.dtype)

    return pl.pallas_call(
        body, name=name, grid=(T // ROW_TILE,),
        in_specs=[pl.BlockSpec((ROW_TILE, D), lambda i: (i, 0)), pl.BlockSpec((1, D), lambda i: (0, 0))],
        out_specs=pl.BlockSpec((ROW_TILE, D), lambda i: (i, 0)),
        out_shape=jax.ShapeDtypeStruct((T, D), BF16), compiler_params=_params(("parallel",)),
    )(x, gain)


def _rmsnorm_bwd(dh, x, gain, dres, *, name):
    T, D = x.shape

    def body(dh_ref, x_ref, g_ref, res_ref, dx_ref, dg_ref):
        xf = x_ref[...]
        r = lax.rsqrt(jnp.mean(xf * xf, axis=-1, keepdims=True) + RMS_EPS)
        y = xf * r
        dh_v = dh_ref[...].astype(F32)
        dy = dh_v * g_ref[...]
        dx_ref[...] = res_ref[...] + r * (dy - y * jnp.mean(dy * y, axis=-1, keepdims=True))

        @pl.when(pl.program_id(0) == 0)
        def _():
            dg_ref[...] = jnp.zeros_like(dg_ref)

        dg_ref[...] += jnp.sum(dh_v * y, axis=0, keepdims=True)

    row = pl.BlockSpec((ROW_TILE, D), lambda i: (i, 0))
    vec = pl.BlockSpec((1, D), lambda i: (0, 0))
    return pl.pallas_call(
        body, name=name, grid=(T // ROW_TILE,), in_specs=[row, row, vec, row], out_specs=(row, vec),
        out_shape=(jax.ShapeDtypeStruct((T, D), F32), jax.ShapeDtypeStruct((1, D), F32)),
        compiler_params=_params(("arbitrary",)),
    )(dh, x, gain, dres)


FF_HALF = N_DEV // 2


def _swiglu_fwd(p, *, name):
    _, T, fb = p.shape

    def body(g_ref, u_ref, o_ref):
        g = g_ref[...].astype(F32)
        o_ref[...] = (g * _sigmoid(g) * u_ref[...].astype(F32)).astype(o_ref.dtype)

    blk = (None, ROW_TILE, fb)
    return pl.pallas_call(
        body, name=name, grid=(T // ROW_TILE, FF_HALF),
        in_specs=[pl.BlockSpec(blk, lambda i, j: (j, i, 0)), pl.BlockSpec(blk, lambda i, j: (j + FF_HALF, i, 0))],
        out_specs=pl.BlockSpec(blk, lambda i, j: (j, i, 0)),
        out_shape=jax.ShapeDtypeStruct((FF_HALF, T, fb), BF16), compiler_params=_params(("parallel", "parallel")),
    )(p, p)


def _swiglu_bwd(da, p, *, name):
    _, T, fb = p.shape

    def body(da_ref, g_ref, u_ref, o_ref):
        g = g_ref[...].astype(F32)
        u = u_ref[...].astype(F32)
        d = da_ref[...].astype(F32)
        s = _sigmoid(g)
        o_ref[0] = (d * u * (s * (1.0 + g * (1.0 - s)))).astype(o_ref.dtype)
        o_ref[1] = (d * g * s).astype(o_ref.dtype)

    blk = (None, ROW_TILE, fb)
    out = pl.pallas_call(
        body, name=name, grid=(T // ROW_TILE, FF_HALF),
        in_specs=[pl.BlockSpec(blk, lambda i, j: (j, i, 0)), pl.BlockSpec(blk, lambda i, j: (j, i, 0)),
                  pl.BlockSpec(blk, lambda i, j: (j + FF_HALF, i, 0))],
        out_specs=pl.BlockSpec((2, None, ROW_TILE, fb), lambda i, j: (0, j, i, 0)),
        out_shape=jax.ShapeDtypeStruct((2, FF_HALF, T, fb), BF16), compiler_params=_params(("parallel", "parallel")),
    )(da, p, p)
    return out.reshape(2 * FF_HALF, T, fb)


COL_GATE_A = 7
COL_GATE_B = 8


def _merge_fwd(ya, yb, proj, *, name):
    T, D = ya.shape

    def body(ya_ref, yb_ref, ga_ref, gb_ref, o_ref):
        o_ref[...] = (_sigmoid(ga_ref[...]) * ya_ref[...] + _sigmoid(gb_ref[...]) * yb_ref[...]).astype(o_ref.dtype)

    row = pl.BlockSpec((ROW_TILE, D), lambda i: (i, 0))
    return pl.pallas_call(
        body, name=name, grid=(T // ROW_TILE,),
        in_specs=[row, row, pl.BlockSpec((ROW_TILE, D), lambda i: (i, COL_GATE_A)),
                  pl.BlockSpec((ROW_TILE, D), lambda i: (i, COL_GATE_B))],
        out_specs=row, out_shape=jax.ShapeDtypeStruct((T, D), BF16), compiler_params=_params(("parallel",)),
    )(ya, yb, proj, proj)


def _merge_bwd(dm, ya, yb, proj, *, name):
    T, D = ya.shape

    def body(dm_ref, ya_ref, yb_ref, ga_ref, gb_ref, dya_ref, dyb_ref, dga_ref, dgb_ref):
        d = dm_ref[...].astype(F32)
        sa = _sigmoid(ga_ref[...])
        sb = _sigmoid(gb_ref[...])
        dya_ref[...] = (d * sa).astype(BF16)
        dyb_ref[...] = (d * sb).astype(BF16)
        dga_ref[...] = (d * ya_ref[...] * sa * (1.0 - sa)).astype(BF16)
        dgb_ref[...] = (d * yb_ref[...] * sb * (1.0 - sb)).astype(BF16)

    row = pl.BlockSpec((ROW_TILE, D), lambda i: (i, 0))
    out = jax.ShapeDtypeStruct((T, D), BF16)
    return pl.pallas_call(
        body, name=name, grid=(T // ROW_TILE,),
        in_specs=[row, row, row, pl.BlockSpec((ROW_TILE, D), lambda i: (i, COL_GATE_A)),
                  pl.BlockSpec((ROW_TILE, D), lambda i: (i, COL_GATE_B))],
        out_specs=(row, row, row, row), out_shape=(out, out, out, out), compiler_params=_params(("parallel",)),
    )(dm, ya, yb, proj, proj)


def _loss_head(y, target, *, name):
    T, D = y.shape

    def body(y_ref, t_ref, loss_ref, dy_ref):
        err = y_ref[...] - t_ref[...]
        dy_ref[...] = err * (1.0 / D)

        @pl.when(pl.program_id(0) == 0)
        def _():
            loss_ref[...] = jnp.zeros_like(loss_ref)

        loss_ref[...] += 0.5 * jnp.sum(jnp.sum(err * err, axis=-1, keepdims=True) * (1.0 / D), axis=0, keepdims=True)

    row = pl.BlockSpec((ROW_TILE, D), lambda i: (i, 0))
    return pl.pallas_call(
        body, name=name, grid=(T // ROW_TILE,), in_specs=[row, row],
        out_specs=(pl.BlockSpec((1, 128), lambda i: (0, 0)), row),
        out_shape=(jax.ShapeDtypeStruct((1, 128), F32), jax.ShapeDtypeStruct((T, D), F32)),
        compiler_params=_params(("arbitrary",)),
    )(y, target)


CONV_PAD = 8


def _conv_taps(w, xp, T, first):
    acc = w[0:1, :] * xp[pl.ds(first, T), :]
    for i in range(1, DN_CONV):
        acc = acc + w[i:i + 1, :] * xp[pl.ds(first + i, T), :]
    return acc


def _conv_fwd(proj, conv_w, *, name):
    T = proj.shape[0]

    def body(x_ref, w_ref, o_ref, xp):
        xp[0:CONV_PAD, :] = jnp.zeros((CONV_PAD, HEAD_DIM), F32)
        xp[CONV_PAD:, :] = x_ref[...]
        y = _conv_taps(w_ref[...], xp, T, CONV_PAD - (DN_CONV - 1))
        s = y * _sigmoid(y)
        n = s * lax.rsqrt(jnp.sum(s * s, axis=-1, keepdims=True) + L2_EPS)
        o_ref[0] = jnp.where(pl.program_id(0) < 2, n, s)

    return pl.pallas_call(
        body, name=name, grid=(3, HEADS),
        in_specs=[pl.BlockSpec((T, HEAD_DIM), lambda c, h: (0, c * HEADS + h)),
                  pl.BlockSpec((DN_CONV, HEAD_DIM), lambda c, h: (0, c * HEADS + h))],
        out_specs=pl.BlockSpec((1, T, HEAD_DIM), lambda c, h: (c, 0, h)),
        out_shape=jax.ShapeDtypeStruct((3, T, D_MODEL), F32),
        scratch_shapes=[pltpu.VMEM((T + CONV_PAD, HEAD_DIM), F32)],
        compiler_params=_params(("parallel", "parallel")),
    )(proj, conv_w)


def _conv_bwd(dqkv, proj, conv_w, *, name):
    T = proj.shape[0]

    def body(d_ref, x_ref, w_ref, dx_ref, dw_ref, xp, dyp):
        xp[0:CONV_PAD, :] = jnp.zeros((CONV_PAD, HEAD_DIM), F32)
        xp[CONV_PAD:, :] = x_ref[...]
        w = w_ref[...]
        y = _conv_taps(w, xp, T, CONV_PAD - (DN_CONV - 1))
        sg = _sigmoid(y)
        s = y * sg
        r = lax.rsqrt(jnp.sum(s * s, axis=-1, keepdims=True) + L2_EPS)
        n = s * r
        d = d_ref[0]
        ds = jnp.where(pl.program_id(0) < 2, r * (d - n * jnp.sum(d * n, axis=-1, keepdims=True)), d)
        dy = ds * (sg * (1.0 + y * (1.0 - sg)))
        dyp[0:T, :] = dy
        dyp[T:, :] = jnp.zeros((CONV_PAD, HEAD_DIM), F32)
        dx = w[0:1, :] * dyp[pl.ds(DN_CONV - 1, T), :]
        for i in range(1, DN_CONV):
            dx = dx + w[i:i + 1, :] * dyp[pl.ds(DN_CONV - 1 - i, T), :]
        dx_ref[...] = dx.astype(dx_ref.dtype)
        for i in range(DN_CONV):
            dw_ref[i:i + 1, :] = jnp.sum(dy * xp[pl.ds(CONV_PAD - (DN_CONV - 1) + i, T), :], axis=0, keepdims=True)

    col = lambda c, h: (0, c * HEADS + h)
    return pl.pallas_call(
        body, name=name, grid=(3, HEADS),
        in_specs=[pl.BlockSpec((1, T, HEAD_DIM), lambda c, h: (c, 0, h)), pl.BlockSpec((T, HEAD_DIM), col),
                  pl.BlockSpec((DN_CONV, HEAD_DIM), col)],
        out_specs=(pl.BlockSpec((T, HEAD_DIM), col), pl.BlockSpec((DN_CONV, HEAD_DIM), col)),
        out_shape=(jax.ShapeDtypeStruct((T, 3 * D_MODEL), BF16), jax.ShapeDtypeStruct((DN_CONV, 3 * D_MODEL), F32)),
        scratch_shapes=[pltpu.VMEM((T + CONV_PAD, HEAD_DIM), F32), pltpu.VMEM((T + CONV_PAD, HEAD_DIM), F32)],
        compiler_params=_params(("parallel", "parallel")),
    )(dqkv, proj, conv_w)


def _inv_unit_lower(low, eye):
    x = eye - low
    power = _hdot(low, low, _B_NN)
    steps = int(math.log2(DN_CHUNK)) - 1
    for s in range(steps):
        x = x + _hdot(x, power, _B_NN)
        if s + 1 < steps:
            power = _hdot(power, power, _B_NN)
    return x


_B_NN = (((2,), (1,)), ((0,), (0,)))
_B_NT = (((2,), (2,)), ((0,), (0,)))
_B_TN = (((1,), (1,)), ((0,), (0,)))


def _dn_load(ref, lead, r0, group):
    rows = pl.ds(r0, group * DN_CHUNK)
    cols = lambda h: slice(h * HEAD_DIM, (h + 1) * HEAD_DIM)
    per_head = [(ref[rows, cols(h)] if lead is None else ref[lead, rows, cols(h)]).reshape(group, DN_CHUNK, HEAD_DIM)
                for h in range(DN_HEADS)]
    return jnp.stack(per_head, axis=1).reshape(group * DN_HEADS, DN_CHUNK, HEAD_DIM)


def _dn_chunk_setup(qkv_ref, b_ref, a_ref, hp_ref, n0, group, tinv=None):
    C = DN_CHUNK
    B = group * DN_HEADS
    r0 = pl.multiple_of(n0 * C, C)
    q = _dn_load(qkv_ref, 0, r0, group) * QK_SCALE
    k = _dn_load(qkv_ref, 1, r0, group)
    v = _dn_load(qkv_ref, 2, r0, group)
    ii = lax.broadcasted_iota(jnp.int32, (B, C, C), 1)
    jj = lax.broadcasted_iota(jnp.int32, (B, C, C), 2)
    eye_mask = ii == jj
    eye = jnp.where(eye_mask, 1.0, 0.0).astype(F32)

    def to_col(row):
        return jnp.sum(jnp.where(eye_mask, jnp.broadcast_to(row, (B, C, C)), 0.0), axis=2, keepdims=True)

    def to_row(col):
        return jnp.sum(jnp.where(eye_mask, jnp.broadcast_to(col, (B, C, C)), 0.0), axis=1, keepdims=True)

    def rows(ref):
        return jnp.stack([ref[h, pl.ds(n0, group)] for h in range(DN_HEADS)], axis=1).reshape(B, 1, C)

    def per_head(row):
        return jnp.stack([hp_ref[h, row:row + 1, 0:C] for h in range(DN_HEADS)] * group, axis=0)

    b_row = rows(b_ref)
    a_row = rows(a_ref)
    a_log = per_head(0)
    dt_b = per_head(1)
    beta_row = _sigmoid(b_row)
    neg_ea = -jnp.exp(a_log)
    g_row = neg_ea * _softplus(a_row + dt_b)
    gc_col = jnp.sum(jnp.where(jj <= ii, jnp.broadcast_to(g_row, (B, C, C)), 0.0), axis=2, keepdims=True)
    gc_row = to_row(gc_col)
    g_last = jnp.sum(g_row, axis=2, keepdims=True)
    beta = to_col(beta_row)
    low_incl = ii >= jj
    decay = jnp.exp(jnp.where(low_incl, gc_col - gc_row, -jnp.inf))
    eg = jnp.exp(gc_col)
    egl = jnp.exp(g_last - gc_col)
    el = jnp.exp(g_last)
    kb = k * beta
    pmat = _bdot(kb, k, _B_NT)
    low = jnp.where(ii > jj, pmat * decay, 0.0)
    if tinv is None:
        tinv = _inv_unit_lower(low, eye)
    u = _hdot(tinv, v * beta, _B_NN)
    w = _hdot(tinv, kb * eg, _B_NN)
    qk = _bdot(q, k, _B_NT)
    attn = qk * decay
    return dict(q=q, k=k, v=v, ii=ii, jj=jj, to_col=to_col, to_row=to_row, b_row=b_row, a_row=a_row, dt_b=dt_b,
                beta_row=beta_row, neg_ea=neg_ea, g_row=g_row, gc_col=gc_col, g_last=g_last, beta=beta,
                decay=decay, eg=eg, egl=egl, el=el, kb=kb, pmat=pmat, tinv=tinv, u=u, w=w, qk=qk, attn=attn,
                qd=q * eg, kd=k * egl, r0=r0)


def _dn_store(ref, lead, r0, group, value):
    value = value.reshape(group, DN_HEADS, DN_CHUNK, HEAD_DIM)
    for h in range(DN_HEADS):
        block = value[:, h].reshape(group * DN_CHUNK, HEAD_DIM)
        if lead is None:
            ref[pl.ds(r0, group * DN_CHUNK), h * HEAD_DIM:(h + 1) * HEAD_DIM] = block
        else:
            ref[lead, pl.ds(r0, group * DN_CHUNK), h * HEAD_DIM:(h + 1) * HEAD_DIM] = block


def _dn_specs(T):
    nc = T // DN_CHUNK
    qkv = pl.BlockSpec((3, T, DN_HEADS * HEAD_DIM), lambda h: (0, 0, h))
    rows = pl.BlockSpec((DN_HEADS, nc, 1, DN_CHUNK), lambda h: (h, 0, 0, 0))
    hp = pl.BlockSpec((DN_HEADS, 8, 128), lambda h: (h, 0, 0))
    states = pl.BlockSpec((DN_HEADS, nc, HEAD_DIM, HEAD_DIM), lambda h: (h, 0, 0, 0))
    return nc, qkv, rows, hp, states


def _dn_inverse_spec(T):
    return pl.BlockSpec((DN_HEADS, T // DN_CHUNK, DN_CHUNK, DN_CHUNK), lambda h: (h, 0, 0, 0))


def _dn_per_head(ref, n0, group):
    stacked = jnp.stack([ref[h, pl.ds(n0, group)] for h in range(DN_HEADS)], axis=1)
    return stacked.reshape((group * DN_HEADS,) + stacked.shape[2:])


def _dn_fwd(qkv, b_rows, a_rows, hp, *, name):
    T = qkv.shape[1]
    nc, qkv_spec, row_spec, hp_spec, st_spec = _dn_specs(T)
    group = math.gcd(nc, DN_GROUP)
    H = DN_HEADS

    def body(qkv_ref, b_ref, a_ref, hp_ref, o_ref, st_ref, inv_ref, s_scr):
        s_scr[...] = jnp.zeros_like(s_scr)

        def step(t, carry):
            n0 = t * group
            c = _dn_chunk_setup(qkv_ref, b_ref, a_ref, hp_ref, n0, group)
            tinv = c["tinv"].reshape(group, H, DN_CHUNK, DN_CHUNK)
            for h in range(H):
                inv_ref[h, pl.ds(n0, group)] = tinv[:, h]
            state = s_scr[...]
            outs = []
            for g in range(group):
                sl = slice(g * H, (g + 1) * H)
                for h in range(H):
                    st_ref[h, n0 + g] = state[h]
                v_new = c["u"][sl] - _bdot(c["w"][sl], state, _B_NN)
                outs.append(_bdot(c["qd"][sl], state, _B_NN) + _bdot(c["attn"][sl], v_new, _B_NN))
                state = state * c["el"][sl] + _bdot(c["kd"][sl], v_new, _B_TN)
            s_scr[...] = state
            _dn_store(o_ref, None, c["r0"], group, jnp.concatenate(outs, axis=0))
            return carry

        lax.fori_loop(0, nc // group, step, 0)

    return pl.pallas_call(
        body, name=name, grid=(HEADS // H,), in_specs=[qkv_spec, row_spec, row_spec, hp_spec],
        out_specs=(pl.BlockSpec((T, H * HEAD_DIM), lambda h: (0, h)), st_spec, _dn_inverse_spec(T)),
        out_shape=(jax.ShapeDtypeStruct((T, D_MODEL), F32),
                   jax.ShapeDtypeStruct((HEADS, nc, HEAD_DIM, HEAD_DIM), F32),
                   jax.ShapeDtypeStruct((HEADS, nc, DN_CHUNK, DN_CHUNK), F32)),
        scratch_shapes=[pltpu.VMEM((H, HEAD_DIM, HEAD_DIM), F32)], compiler_params=_params(("parallel",)),
    )(qkv, b_rows, a_rows, hp)


def _dn_bwd(qkv, b_rows, a_rows, hp, states, inverses, do, *, name):
    T = qkv.shape[1]
    C = DN_CHUNK
    nc, qkv_spec, row_spec, hp_spec, st_spec = _dn_specs(T)
    group = math.gcd(nc, DN_GROUP)
    H = DN_HEADS
    B = group * H

    def body(qkv_ref, b_ref, a_ref, hp_ref, st_ref, inv_ref, do_ref, dqkv_ref, db_ref, da_ref, dhp_ref, ds_scr, acc_scr):
        ds_scr[...] = jnp.zeros_like(ds_scr)
        acc_scr[...] = jnp.zeros_like(acc_scr)

        def step(t, carry):
            n0 = nc - (t + 1) * group
            c = _dn_chunk_setup(qkv_ref, b_ref, a_ref, hp_ref, n0, group, tinv=_dn_per_head(inv_ref, n0, group))
            state = _dn_per_head(st_ref, n0, group)
            d_o = _dn_load(do_ref, None, c["r0"], group)
            v_new = c["u"] - _bdot(c["w"], state, _B_NN)
            d_vnew_local = _bdot(c["attn"], d_o, _B_TN)
            d_state_local = _bdot(c["qd"], d_o, _B_TN)
            d_state = ds_scr[...]
            d_vnew, d_kd, d_el = [None] * group, [None] * group, [None] * group
            for g in reversed(range(group)):
                sl = slice(g * H, (g + 1) * H)
                d_vnew[g] = d_vnew_local[sl] + _bdot(c["kd"][sl], d_state, _B_NN)
                d_kd[g] = _bdot(v_new[sl], d_state, _B_NT)
                d_el[g] = jnp.sum(jnp.sum(d_state * state[sl], axis=2, keepdims=True), axis=1, keepdims=True)
                d_state = d_state * c["el"][sl] + d_state_local[sl] - _bdot(c["w"][sl], d_vnew[g], _B_TN)
            ds_scr[...] = d_state
            chunk_grads(c, n0, state, d_o, v_new, jnp.concatenate(d_vnew, axis=0), jnp.concatenate(d_kd, axis=0),
                        jnp.concatenate(d_el, axis=0))
            return carry

        def chunk_grads(c, n0, state, d_o, v_new, d_vnew, d_kd, d_el):
            ii, jj = c["ii"], c["jj"]
            q, k, v, kb, beta = c["q"], c["k"], c["v"], c["kb"], c["beta"]
            decay, eg, egl, el = c["decay"], c["eg"], c["egl"], c["el"]
            u, w, tinv = c["u"], c["w"], c["tinv"]
            d_qd = _bdot(d_o, state, _B_NT)
            d_attn = _bdot(d_o, v_new, _B_NT)
            d_w = -_bdot(d_vnew, state, _B_NT)
            d_rv = _hdot(tinv, d_vnew, _B_TN)
            d_rw = _hdot(tinv, d_w, _B_TN)
            d_amat = -(_bdot(d_rv, u, _B_NT) + _bdot(d_rw, w, _B_NT))
            d_low = jnp.where(ii > jj, d_amat, 0.0)
            d_p = d_low * decay
            d_qk = d_attn * decay
            e_mat = (d_low * c["pmat"] + d_attn * c["qk"]) * decay
            d_q = _bdot(d_qk, k, _B_NN) + d_qd * eg
            d_kb = _bdot(d_p, k, _B_NN) + d_rw * eg
            d_k = _bdot(d_qk, q, _B_TN) + _bdot(d_p, kb, _B_TN) + d_kd * egl + d_kb * beta
            d_beta = jnp.sum(d_kb * k, axis=2, keepdims=True) + jnp.sum(d_rv * v, axis=2, keepdims=True)
            d_v = d_rv * beta
            d_eg = jnp.sum(d_qd * q, axis=2, keepdims=True) + jnp.sum(d_rw * kb, axis=2, keepdims=True)
            d_egl = jnp.sum(d_kd * k, axis=2, keepdims=True)
            d_glast = jnp.sum(d_egl * egl, axis=1, keepdims=True) + d_el * el
            row_sum = jnp.sum(e_mat, axis=2, keepdims=True)
            col_sum = c["to_col"](jnp.sum(e_mat, axis=1, keepdims=True))
            d_gc = row_sum - col_sum + d_eg * eg - d_egl * egl
            d_g_row = jnp.sum(jnp.where(ii >= jj, jnp.broadcast_to(d_gc, (B, C, C)), 0.0), axis=1, keepdims=True) + d_glast
            beta_row = c["beta_row"]
            d_b_row = c["to_row"](d_beta) * beta_row * (1.0 - beta_row)
            d_a_row = d_g_row * c["neg_ea"] * _sigmoid(c["a_row"] + c["dt_b"])
            _dn_store(dqkv_ref, 0, c["r0"], group, d_q * QK_SCALE)
            _dn_store(dqkv_ref, 1, c["r0"], group, d_k)
            _dn_store(dqkv_ref, 2, c["r0"], group, d_v)
            d_b_row = d_b_row.reshape(group, H, 1, C)
            d_a_row = d_a_row.reshape(group, H, 1, C)
            d_a_log = jnp.sum((d_g_row * c["g_row"]).reshape(group, H, 1, C), axis=0)
            d_dt_b = jnp.sum(d_a_row, axis=0)
            for h in range(H):
                db_ref[h, pl.ds(n0, group)] = d_b_row[:, h]
                da_ref[h, pl.ds(n0, group)] = d_a_row[:, h]
                acc_scr[h, 0:1, 0:C] += d_a_log[h]
                acc_scr[h, 1:2, 0:C] += d_dt_b[h]

        lax.fori_loop(0, nc // group, step, 0)
        for h in range(H):
            tot = jnp.sum(acc_scr[h], axis=1, keepdims=True)
            dhp_ref[h] = jnp.broadcast_to(tot, (8, 128))

    return pl.pallas_call(
        body, name=name, grid=(HEADS // H,),
        in_specs=[qkv_spec, row_spec, row_spec, hp_spec, st_spec, _dn_inverse_spec(T),
                  pl.BlockSpec((T, H * HEAD_DIM), lambda h: (0, h))],
        out_specs=(qkv_spec, row_spec, row_spec, hp_spec),
        out_shape=(jax.ShapeDtypeStruct((3, T, D_MODEL), F32), jax.ShapeDtypeStruct((HEADS, nc, 1, C), F32),
                   jax.ShapeDtypeStruct((HEADS, nc, 1, C), F32), jax.ShapeDtypeStruct((HEADS, 8, 128), F32)),
        scratch_shapes=[pltpu.VMEM((H, HEAD_DIM, HEAD_DIM), F32), pltpu.VMEM((H, 8, 128), F32)],
        compiler_params=_params(("parallel",)),
    )(qkv, b_rows, a_rows, hp, states, inverses, do)


COL_Z = 3 * HEADS


def _gated_norm_fwd(o, proj, gain, *, name):
    T = o.shape[0]

    def body(o_ref, z_ref, g_ref, out_ref):
        x = o_ref[...]
        r = lax.rsqrt(jnp.mean(x * x, axis=-1, keepdims=True) + RMS_EPS)
        z = z_ref[...]
        out_ref[...] = (x * r * g_ref[...] * (z * _sigmoid(z))).astype(out_ref.dtype)

    return pl.pallas_call(
        body, name=name, grid=(HEADS,),
        in_specs=[pl.BlockSpec((T, HEAD_DIM), lambda h: (0, h)), pl.BlockSpec((T, HEAD_DIM), lambda h: (0, COL_Z + h)),
                  pl.BlockSpec((1, HEAD_DIM), lambda h: (0, 0))],
        out_specs=pl.BlockSpec((T, HEAD_DIM), lambda h: (0, h)),
        out_shape=jax.ShapeDtypeStruct((T, D_MODEL), BF16), compiler_params=_params(("parallel",)),
    )(o, proj, gain)


def _gated_norm_bwd(dout, o, proj, gain, *, name):
    T = o.shape[0]

    def body(d_ref, o_ref, z_ref, g_ref, do_ref, dz_ref, dg_ref):
        x = o_ref[...]
        r = lax.rsqrt(jnp.mean(x * x, axis=-1, keepdims=True) + RMS_EPS)
        n = x * r
        z = z_ref[...]
        sg = _sigmoid(z)
        d = d_ref[...].astype(F32)
        g = g_ref[...]
        dz_ref[...] = (d * n * g * (sg * (1.0 + z * (1.0 - sg)))).astype(dz_ref.dtype)
        dy = d * (z * sg)
        dyg = dy * g
        do_ref[...] = r * (dyg - n * jnp.mean(dyg * n, axis=-1, keepdims=True))

        @pl.when(pl.program_id(0) == 0)
        def _():
            dg_ref[...] = jnp.zeros_like(dg_ref)

        dg_ref[...] += jnp.sum(dy * n, axis=0, keepdims=True)

    head = pl.BlockSpec((T, HEAD_DIM), lambda h: (0, h))
    vec = pl.BlockSpec((1, HEAD_DIM), lambda h: (0, 0))
    return pl.pallas_call(
        body, name=name, grid=(HEADS,),
        in_specs=[head, head, pl.BlockSpec((T, HEAD_DIM), lambda h: (0, COL_Z + h)), vec],
        out_specs=(head, head, vec),
        out_shape=(jax.ShapeDtypeStruct((T, D_MODEL), F32), jax.ShapeDtypeStruct((T, D_MODEL), BF16),
                   jax.ShapeDtypeStruct((1, HEAD_DIM), F32)),
        compiler_params=_params(("arbitrary",)),
    )(dout, o, proj, gain)


COL_SBQ = 4 * HEADS
COL_SBK = 5 * HEADS
COL_SBV = 6 * HEADS


def _split_dot(x, mat):
    lead = x.shape[:-1]
    x = x.reshape(-1, x.shape[-1])
    hi = x.astype(BF16)
    lo = (x - hi.astype(F32)).astype(BF16)
    out = jnp.dot(hi, mat, preferred_element_type=F32) + jnp.dot(lo, mat, preferred_element_type=F32)
    return out.reshape(lead + (mat.shape[-1],))


def _sb_specs(T, heads):
    col = lambda first: pl.BlockSpec((T, heads * HEAD_DIM), lambda h: (0, first // heads + h))
    return col(COL_SBQ), col(COL_SBK), col(COL_SBV), pl.BlockSpec((1, HEAD_DIM), lambda h: (0, 0))


def _heads_first(x):
    return jnp.stack([x[:, c:c + HEAD_DIM] for c in range(0, x.shape[1], HEAD_DIM)], axis=0)


def _heads_last(x):
    return jnp.concatenate([x[h] for h in range(x.shape[0])], axis=1)


def _head_rms(x):
    r = lax.rsqrt(jnp.mean(x * x, axis=-1, keepdims=True) + RMS_EPS)
    return x * r, r


def _sb_fwd(proj, q_gain, k_gain, *, name):
    T = proj.shape[0]
    B = SB_BLOCK
    H = SB_HEADS
    nb = T // B
    KT = min(SB_KEY_TILE, T)
    NS = KT // B
    q_spec, k_spec, v_spec, g_spec = _sb_specs(T, H)

    def body(q_ref, k_ref, v_ref, gq_ref, gk_ref, o_ref, lt_ref, qs, ks, vs):
        qs[...] = (_head_rms(_heads_first(q_ref[...]))[0] * (gq_ref[...] * QK_SCALE)).astype(BF16)
        ks[...] = (_head_rms(_heads_first(k_ref[...]))[0] * gk_ref[...]).astype(BF16)
        vs[...] = _heads_first(v_ref[...]).astype(BF16)
        ii = lax.broadcasted_iota(jnp.int32, (B, B), 0)
        jj = lax.broadcasted_iota(jnp.int32, (B, B), 1)
        after = jnp.where(ii > jj, 1.0, 0.0).astype(BF16)
        ahead = lax.broadcasted_iota(jnp.int32, (H, B, KT), 2) - lax.broadcasted_iota(jnp.int32, (H, B, KT), 1)

        def q_block(i, carry):
            rows = pl.ds(pl.multiple_of(i * B, B), B)
            q = qs[:, rows, :]

            def tile(c0, acc, tail, masked):
                cols = pl.ds(c0, KT)
                z = lax.dot_general(q, ks[:, cols, :], _B_NT, preferred_element_type=F32)
                sp = _softplus(z)
                causal = ahead < (i * B - c0)
                log_1mb = jnp.where(causal, -sp, 0.0) if masked else -sp
                parts = [None] * NS
                for b in reversed(range(NS)):
                    blk = log_1mb[:, :, b * B:(b + 1) * B]
                    parts[b] = _split_dot(blk, after) + tail
                    tail = tail + jnp.sum(blk, axis=2, keepdims=True)
                survive = parts[0] if NS == 1 else jnp.concatenate(parts, axis=2)
                wts = jnp.exp(z - sp + survive)
                if masked:
                    wts = jnp.where(causal, wts, 0.0)
                acc = acc + lax.dot_general(wts.astype(BF16), vs[:, cols, :], _B_NN, preferred_element_type=F32)
                return acc, tail

            last = i // NS
            acc, tail = tile(pl.multiple_of(last * KT, KT), jnp.zeros((H, B, HEAD_DIM), F32), jnp.zeros((H, B, 1), F32), True)
            acc, tail = lax.fori_loop(
                1, last + 1, lambda s, c: tile(pl.multiple_of((last - s) * KT, KT), c[0], c[1], False), (acc, tail))
            o_ref[rows, :] = _heads_last(acc).astype(o_ref.dtype)
            lt_ref[rows, :] = _heads_last(jnp.broadcast_to(tail, (H, B, HEAD_DIM)))
            return carry

        lax.fori_loop(0, nb, q_block, 0)

    heads = pl.BlockSpec((T, H * HEAD_DIM), lambda h: (0, h))
    return pl.pallas_call(
        body, name=name, grid=(HEADS // H,), in_specs=[q_spec, k_spec, v_spec, g_spec, g_spec],
        out_specs=(heads, heads),
        out_shape=(jax.ShapeDtypeStruct((T, D_MODEL), BF16), jax.ShapeDtypeStruct((T, D_MODEL), F32)),
        scratch_shapes=[pltpu.VMEM((H, T, HEAD_DIM), BF16)] * 3, compiler_params=_params(("parallel",)),
    )(proj, proj, proj, q_gain, k_gain)


def _sb_bwd(proj, q_gain, k_gain, ltot, do, *, name):
    T = proj.shape[0]
    B = SB_BLOCK
    H = SB_HEADS_BWD
    nb = T // B
    KT = min(SB_KEY_TILE_BWD, T)
    NS = KT // B
    q_spec, k_spec, v_spec, g_spec = _sb_specs(T, H)

    def body(q_ref, k_ref, v_ref, gq_ref, gk_ref, lt_ref, do_ref, dq_ref, dk_ref, dv_ref, dgq_ref, dgk_ref,
             qs, ks, vs, dos, lts, dq_acc, dk_acc, dv_acc):
        qn, q_r = _head_rms(_heads_first(q_ref[...]))
        kn, k_r = _head_rms(_heads_first(k_ref[...]))
        qs[...] = (qn * (gq_ref[...] * QK_SCALE)).astype(BF16)
        ks[...] = (kn * gk_ref[...]).astype(BF16)
        vs[...] = _heads_first(v_ref[...]).astype(BF16)
        dos[...] = _heads_first(do_ref[...]).astype(BF16)
        lts[...] = _heads_first(lt_ref[...])
        dk_acc[...] = jnp.zeros_like(dk_acc)
        dv_acc[...] = jnp.zeros_like(dv_acc)
        ii = lax.broadcasted_iota(jnp.int32, (B, B), 0)
        jj = lax.broadcasted_iota(jnp.int32, (B, B), 1)
        upto = jnp.where(ii <= jj, 1.0, 0.0).astype(BF16)
        before = jnp.where(ii < jj, 1.0, 0.0).astype(BF16)
        ahead = lax.broadcasted_iota(jnp.int32, (H, B, KT), 2) - lax.broadcasted_iota(jnp.int32, (H, B, KT), 1)

        def q_block(i, carry):
            rows = pl.ds(pl.multiple_of(i * B, B), B)
            q = qs[:, rows, :]
            d_o = dos[:, rows, :]
            total = jnp.max(lts[:, rows, :], axis=2, keepdims=True)

            def tile(c0, dq, head_lb, head_de, masked):
                cols = pl.ds(c0, KT)
                k = ks[:, cols, :]
                v = vs[:, cols, :]
                z = lax.dot_general(q, k, _B_NT, preferred_element_type=F32)
                sp = _softplus(z)
                causal = ahead < (i * B - c0)
                log_1mb = jnp.where(causal, -sp, 0.0) if masked else -sp
                parts = [None] * NS
                for b in range(NS):
                    blk = log_1mb[:, :, b * B:(b + 1) * B]
                    parts[b] = _split_dot(blk, upto) + head_lb
                    head_lb = head_lb + jnp.sum(blk, axis=2, keepdims=True)
                prefix = parts[0] if NS == 1 else jnp.concatenate(parts, axis=2)
                wts = jnp.exp(z - sp + (total - prefix))
                if masked:
                    wts = jnp.where(causal, wts, 0.0)
                d_w = lax.dot_general(d_o, v, _B_NT, preferred_element_type=F32)
                d_e = wts * d_w
                d_eb = d_e.astype(BF16)
                for b in range(NS):
                    inside = jnp.dot(d_eb[:, :, b * B:(b + 1) * B].reshape(H * B, B), before, preferred_element_type=F32)
                    parts[b] = inside.reshape(H, B, B) + head_de
                    head_de = head_de + jnp.sum(d_e[:, :, b * B:(b + 1) * B], axis=2, keepdims=True)
                cum = parts[0] if NS == 1 else jnp.concatenate(parts, axis=2)
                sig = jnp.exp(z - sp)
                d_z = d_e - sig * (d_e + cum)
                if masked:
                    d_z = jnp.where(causal, d_z, 0.0)
                d_zb = d_z.astype(BF16)
                dq = dq + lax.dot_general(d_zb, k, _B_NN, preferred_element_type=F32)
                dk_acc[:, cols, :] += lax.dot_general(d_zb, q, _B_TN, preferred_element_type=F32)
                dv_acc[:, cols, :] += lax.dot_general(wts.astype(BF16), d_o, _B_TN, preferred_element_type=F32)
                return dq, head_lb, head_de

            last = i // NS
            zero = jnp.zeros((H, B, 1), F32)
            state = lax.fori_loop(0, last, lambda t, c: tile(pl.multiple_of(t * KT, KT), *c, False),
                                  (jnp.zeros((H, B, HEAD_DIM), F32), zero, zero))
            dq, _, _ = tile(pl.multiple_of(last * KT, KT), *state, True)
            dq_acc[:, rows, :] = dq * QK_SCALE
            return carry

        lax.fori_loop(0, nb, q_block, 0)

        def norm_bwd(d_scaled, n, r, gain):
            dn = d_scaled * gain
            d_gain = jnp.sum(jnp.sum(d_scaled * n, axis=1, keepdims=True), axis=0)
            return r * (dn - n * jnp.mean(dn * n, axis=-1, keepdims=True)), d_gain

        dq_raw, dgq = norm_bwd(dq_acc[...], qn, q_r, gq_ref[...])
        dk_raw, dgk = norm_bwd(dk_acc[...], kn, k_r, gk_ref[...])
        dq_ref[...] = _heads_last(dq_raw).astype(dq_ref.dtype)
        dk_ref[...] = _heads_last(dk_raw).astype(dk_ref.dtype)
        dv_ref[...] = _heads_last(dv_acc[...]).astype(dv_ref.dtype)

        @pl.when(pl.program_id(0) == 0)
        def _():
            dgq_ref[...] = jnp.zeros_like(dgq_ref)
            dgk_ref[...] = jnp.zeros_like(dgk_ref)

        dgq_ref[...] += dgq
        dgk_ref[...] += dgk

    heads = pl.BlockSpec((T, H * HEAD_DIM), lambda h: (0, h))
    out = jax.ShapeDtypeStruct((T, D_MODEL), BF16)
    vec = jax.ShapeDtypeStruct((1, HEAD_DIM), F32)
    return pl.pallas_call(
        body, name=name, grid=(HEADS // H,), in_specs=[q_spec, k_spec, v_spec, g_spec, g_spec, heads, heads],
        out_specs=(heads, heads, heads, g_spec, g_spec), out_shape=(out, out, out, vec, vec),
        scratch_shapes=[pltpu.VMEM((H, T, HEAD_DIM), BF16)] * 4 + [pltpu.VMEM((H, T, HEAD_DIM), F32)] * 4,
        compiler_params=_params(("arbitrary",)),
    )(proj, proj, proj, q_gain, k_gain, ltot, do)


ADAM_ROWS = 256


def _adamw(g_parts, w, m, v, *, name, layer=0, earlier=None):
    K, A, C = g_parts.shape
    tr = next((t for t in (ADAM_ROWS, ADAM_ROWS // 2) if A % t == 0), A // 2 if A % 32 == 0 else A)

    def body(g_ref, w_ref, m_ref, v_ref, *rest):
        go_ref, d_ref, mo_ref, vo_ref = rest[-4:]
        g = g_ref[0].astype(F32)
        for k in range(1, K):
            g = g + g_ref[k].astype(F32)
        go_ref[...] = g
        m_new = ADAM_B1 * m_ref[...] + (1.0 - ADAM_B1) * g
        v_new = ADAM_B2 * v_ref[...] + (1.0 - ADAM_B2) * (g * g)
        m_hat = m_new / (1.0 - ADAM_B1 ** ADAM_STEP)
        v_hat = v_new / (1.0 - ADAM_B2 ** ADAM_STEP)
        d_ref[...] = -ADAM_LR * (m_hat / (jnp.sqrt(v_hat) + ADAM_EPS) + ADAM_WD * w_ref[...])
        mo_ref[...] = m_new
        vo_ref[...] = v_new

    row = pl.BlockSpec((None, tr, C), lambda i: (layer, i, 0))
    out = jax.ShapeDtypeStruct(w.shape, F32)
    in_specs = [pl.BlockSpec((K, tr, C), lambda i: (0, i, 0)), row, row, row]
    if earlier is None:
        return pl.pallas_call(
            body, name=name, grid=(A // tr,), in_specs=in_specs, out_specs=(row, row, row, row),
            out_shape=(out, out, out, out), compiler_params=_params(("parallel",)),
        )(g_parts, w, m, v)
    return pl.pallas_call(
        body, name=name, grid=(A // tr,), in_specs=in_specs + [ANY] * 4, out_specs=(row, row, row, row),
        out_shape=(out, out, out, out), input_output_aliases={4 + j: j for j in range(4)},
        compiler_params=_params(("parallel",)),
    )(g_parts, w, m, v, *earlier)


def _sum_parts(parts, *, name):
    K, R, C = parts.shape

    def body(p_ref, o_ref):
        acc = p_ref[0]
        for k in range(1, K):
            acc = acc + p_ref[k]
        o_ref[...] = acc

    return pl.pallas_call(body, name=name, out_shape=jax.ShapeDtypeStruct((R, C), F32))(parts)


def _position():
    return lax.axis_index("x"), lax.axis_index("y"), lax.axis_index("c")


def _all_gather(shards, *, name):
    n = len(shards)

    def body(*refs):
        x_refs, out_refs = refs[:n], refs[n:2 * n]
        send_sems, recv_sems, local_sems = refs[2 * n:]
        x, y, c = _position()
        me, sibling = (x, y, c), (x, y, 1 - c)
        chips = [(1 - x, y), (x, 1 - y), (1 - x, 1 - y)]

        def slot(a, px, py, pc):
            return out_refs[a].at[4 * px + 2 * py + pc]

        def copy(a, k, block, to, own=False):
            return pltpu.make_async_remote_copy(
                src_ref=x_refs[a] if own else slot(a, *block), dst_ref=slot(a, *block),
                send_sem=send_sems.at[a, k], recv_sem=recv_sems.at[a, k], device_id=to, device_id_type=MESH)

        mine = [pltpu.make_async_copy(x_refs[a], slot(a, *me), local_sems.at[a]) for a in range(n)]
        for cp in mine:
            cp.start()
        first = [copy(a, 1 + j, me, (*chip, c), own=True) for j, chip in enumerate(chips) for a in range(n)]
        first += [copy(a, 0, me, sibling, own=True) for a in range(n)]
        for cp in first:
            cp.start()
        passed = []
        for j, chip in enumerate(chips):
            for a in range(n):
                copy(a, 1 + j, (*chip, c), me).wait_recv()
                passed.append(copy(a, 4 + j, (*chip, c), sibling))
                passed[-1].start()
        for a in range(n):
            copy(a, 0, sibling, me).wait_recv()
        for j, chip in enumerate(chips):
            for a in range(n):
                copy(a, 4 + j, (*chip, 1 - c), me).wait_recv()
        for cp in first + passed:
            cp.wait_send()
        for cp in mine:
            cp.wait()

    return pl.pallas_call(
        body, name=name, in_specs=[ANY] * n, out_specs=[ANY] * n,
        out_shape=[jax.ShapeDtypeStruct((N_DEV,) + s.shape, s.dtype) for s in shards],
        scratch_shapes=[pltpu.SemaphoreType.DMA((n, 7)), pltpu.SemaphoreType.DMA((n, 7)), pltpu.SemaphoreType.DMA((n,))],
    )(*shards)


HBM = pl.BlockSpec(memory_space=pltpu.HBM)
SEM = pl.BlockSpec(memory_space=pltpu.SEMAPHORE)
DATAFLOW = pltpu.SideEffectType.DATAFLOW_SIDE_EFFECTING


def _exchange_copies(gather, x_refs, land_refs, send_sems, recv_sems, local_sems):
    n = len(x_refs)
    x, y, c = _position()
    me = 4 * x + 2 * y + c

    def src(a, slot):
        return x_refs[a] if gather else x_refs[a].at[slot]

    mine = [pltpu.make_async_copy(src(a, me), land_refs[a].at[me], local_sems.at[a]) for a in range(n)]
    sends, recvs = [], []
    for k in range(1, N_DEV):
        px, py, pc = (x + (k >> 2)) % 2, (y + ((k >> 1) & 1)) % 2, (c + (k & 1)) % 2
        peer = 4 * px + 2 * py + pc
        for a in range(n):
            sems = dict(send_sem=send_sems.at[7 * a + k - 1], recv_sem=recv_sems.at[7 * a + k - 1],
                        device_id=(px, py, pc), device_id_type=MESH)
            sends.append(pltpu.make_async_remote_copy(src_ref=src(a, peer), dst_ref=land_refs[a].at[me], **sems))
            recvs.append(pltpu.make_async_remote_copy(src_ref=src(a, me), dst_ref=land_refs[a].at[peer], **sems))
    return mine, sends, recvs


def _exchange_start(parts, *, gather, name):
    n = len(parts)

    def body(*refs):
        x_refs, land_refs = refs[:n], refs[n:2 * n]
        send_sems, recv_sems, local_sems = refs[2 * n:2 * n + 3]
        token = refs[-1]
        mine, sends, _ = _exchange_copies(gather, x_refs, land_refs, send_sems, recv_sems, local_sems)
        for cp in mine + sends:
            cp.start()
        token[...] = jnp.zeros_like(token)

    sems = (pltpu.SemaphoreType.DMA((7 * n,)), pltpu.SemaphoreType.DMA((7 * n,)), pltpu.SemaphoreType.DMA((n,)))
    thru = tuple(pltpu.HBM(p.shape, p.dtype) for p in parts)
    land = tuple(pltpu.HBM(((N_DEV,) if gather else ()) + p.shape, p.dtype) for p in parts)
    res = pl.pallas_call(
        body, name=name, in_specs=[HBM] * (2 * n),
        out_specs=(SEM, SEM, SEM) + (HBM,) * (2 * n) + (pl.BlockSpec(memory_space=pltpu.VMEM),),
        out_shape=sems + thru + land + (jax.ShapeDtypeStruct((8, 128), F32),),
        input_output_aliases={a: 3 + a for a in range(2 * n)},
        compiler_params=pltpu.CompilerParams(has_side_effects=DATAFLOW),
    )(*[pltpu.with_memory_space_constraint(p, pltpu.HBM) for p in parts],
      *[pltpu.with_memory_space_constraint(lax.empty(z.shape, z.dtype), pltpu.HBM) for z in land])
    return res[:3], res[3:3 + n], res[3 + n:3 + 2 * n], res[-1]


def _exchange_wait(sems, parts, landing, after, *, gather, name):
    n = len(parts)
    after = list(after)

    def body(*refs):
        x_refs, land_refs = refs[:n], refs[n:2 * n]
        send_sems, recv_sems, local_sems = refs[2 * n:2 * n + 3]
        token = refs[-1]
        mine, sends, recvs = _exchange_copies(gather, x_refs, land_refs, send_sems, recv_sems, local_sems)
        for cp in recvs:
            cp.wait_recv()
        for cp in sends:
            cp.wait_send()
        for cp in mine:
            cp.wait()
        token[...] = jnp.zeros_like(token)

    thru = tuple(pltpu.HBM(p.shape, p.dtype) for p in tuple(parts) + tuple(landing))
    res = pl.pallas_call(
        body, name=name, in_specs=[HBM] * (2 * n) + [SEM, SEM, SEM] + [ANY] * len(after),
        out_specs=(HBM,) * (2 * n) + (pl.BlockSpec(memory_space=pltpu.VMEM),),
        out_shape=thru + (jax.ShapeDtypeStruct((8, 128), F32),), input_output_aliases={a: a for a in range(2 * n)},
        compiler_params=pltpu.CompilerParams(has_side_effects=DATAFLOW),
    )(*parts, *landing, *sems, *after)
    return res[n:2 * n], res[-1]


def _relay_copies(x_refs, land_refs, first_send, first_recv, relay_send, relay_recv, local_sems):
    n = len(x_refs)
    x, y, c = _position()
    sibling = (x, y, 1 - c)
    chips = [(1 - x, y), (x, 1 - y), (1 - x, 1 - y)]

    def slot(a, px, py, pc):
        return land_refs[a].at[4 * px + 2 * py + pc]

    def hop(a, k, block, to, own=False):
        return pltpu.make_async_remote_copy(
            src_ref=x_refs[a] if own else slot(a, *block), dst_ref=slot(a, *block),
            send_sem=first_send.at[4 * a + k], recv_sem=first_recv.at[4 * a + k], device_id=to, device_id_type=MESH)

    def relay(a, j, block, to):
        return pltpu.make_async_remote_copy(
            src_ref=slot(a, *block), dst_ref=slot(a, *block),
            send_sem=relay_send.at[3 * a + j], recv_sem=relay_recv.at[3 * a + j], device_id=to, device_id_type=MESH)

    me = (x, y, c)
    mine = [pltpu.make_async_copy(x_refs[a], slot(a, *me), local_sems.at[a]) for a in range(n)]
    sends = [hop(a, 1 + j, me, (*chip, c), own=True) for j, chip in enumerate(chips) for a in range(n)]
    sends += [hop(a, 0, me, sibling, own=True) for a in range(n)]
    over_ici = [hop(a, 1 + j, (*chip, c), me) for j, chip in enumerate(chips) for a in range(n)]
    from_sibling = [hop(a, 0, sibling, me) for a in range(n)]
    if relay_send is None:
        return mine, sends, over_ici, from_sibling, [], []
    relays = [relay(a, j, (*chip, c), sibling) for j, chip in enumerate(chips) for a in range(n)]
    relayed = [relay(a, j, (*chip, 1 - c), me) for j, chip in enumerate(chips) for a in range(n)]
    return mine, sends, over_ici, from_sibling, relays, relayed


def _relay_call(body, n_sem_in, n_sem_out, shards, landing, sems_in, after, name):
    n = len(shards)
    thru = tuple(pltpu.HBM(p.shape, p.dtype) for p in tuple(shards) + tuple(landing))
    res = pl.pallas_call(
        body, name=name, in_specs=[HBM] * (2 * n) + [SEM] * n_sem_in + [ANY] * len(after),
        out_specs=(SEM,) * len(n_sem_out) + (HBM,) * (2 * n) + (pl.BlockSpec(memory_space=pltpu.VMEM),),
        out_shape=tuple(pltpu.SemaphoreType.DMA((k,)) for k in n_sem_out) + thru + (jax.ShapeDtypeStruct((8, 128), F32),),
        input_output_aliases={a: len(n_sem_out) + a for a in range(2 * n)},
        compiler_params=pltpu.CompilerParams(has_side_effects=DATAFLOW),
    )(*shards, *landing, *sems_in, *after)
    k = len(n_sem_out)
    return res[:k], res[k:k + n], res[k + n:k + 2 * n], res[-1]


def _relay_gather_start(shards, *, name):
    n = len(shards)

    def body(*refs):
        x_refs, land_refs = refs[:n], refs[n:2 * n]
        first_send, first_recv, local_sems = refs[2 * n:2 * n + 3]
        mine, sends, *_ = _relay_copies(x_refs, land_refs, first_send, first_recv, None, None, local_sems)
        for cp in mine + sends:
            cp.start()
        refs[-1][...] = jnp.zeros_like(refs[-1])

    landing = [pltpu.with_memory_space_constraint(lax.empty((N_DEV,) + s.shape, s.dtype), pltpu.HBM) for s in shards]
    shards = [pltpu.with_memory_space_constraint(s, pltpu.HBM) for s in shards]
    return _relay_call(body, 0, (4 * n, 4 * n, n), shards, landing, (), (), name)


def _relay_gather_pass_on(first, shards, landing, after, *, name):
    n = len(shards)

    def body(*refs):
        x_refs, land_refs = refs[:n], refs[n:2 * n]
        first_send, first_recv, local_sems = refs[2 * n:2 * n + 3]
        relay_send, relay_recv = refs[2 * n + 3 + len(after):2 * n + 5 + len(after)]
        _, _, over_ici, _, relays, _ = _relay_copies(x_refs, land_refs, first_send, first_recv, relay_send, relay_recv,
                                                   local_sems)
        for arrival, cp in zip(over_ici, relays):
            arrival.wait_recv()
            cp.start()
        refs[-1][...] = jnp.zeros_like(refs[-1])

    return _relay_call(body, 3, (3 * n, 3 * n), shards, landing, first, list(after), name)


def _relay_gather_wait(first, relay, shards, landing, after, *, name):
    n = len(shards)

    def body(*refs):
        x_refs, land_refs = refs[:n], refs[n:2 * n]
        first_send, first_recv, local_sems, relay_send, relay_recv = refs[2 * n:2 * n + 5]
        mine, sends, _, from_sibling, relays, relayed = _relay_copies(
            x_refs, land_refs, first_send, first_recv, relay_send, relay_recv, local_sems)
        for cp in from_sibling + relayed:
            cp.wait_recv()
        for cp in sends + relays:
            cp.wait_send()
        for cp in mine:
            cp.wait()
        refs[-1][...] = jnp.zeros_like(refs[-1])

    _, _, landing, token = _relay_call(body, 5, (), shards, landing, tuple(first) + tuple(relay), list(after), name)
    return landing, token


def _ffn_fwd(x, gain, wg_in, wg_out, tag):
    T, D = x.shape
    fb, rb = wg_in.shape[-1], wg_out.shape[-2]
    tm, tn = min(T, 1024), 512
    h = _rmsnorm_fwd(x, gain, name=f"{tag}_norm")
    p = _mm(name=f"{tag}_in", grid=(T // tm, N_DEV, 1), tile=(tm, fb),
            a=h, a_spec=pl.BlockSpec((tm, D), lambda i, j, k: (i, 0)),
            b=wg_in, b_spec=pl.BlockSpec((None, D, fb), lambda i, j, k: (j, 0, 0)),
            out_shape=jax.ShapeDtypeStruct((N_DEV, T, fb), BF16), o_spec=pl.BlockSpec((None, tm, fb), lambda i, j, k: (j, i, 0)))
    a = _swiglu_fwd(p, name=f"{tag}_act")
    y = _mm(name=f"{tag}_out", grid=(T // tm, 1, FF_HALF), tile=(tm, D), resid=x, scale=0.5,
            a=a, a_spec=pl.BlockSpec((None, tm, fb), lambda i, j, k: (k, i, 0)),
            b=wg_out.reshape(N_DEV * rb, D), b_spec=pl.BlockSpec((fb, D), lambda i, j, k: (k, 0)),
            out_shape=jax.ShapeDtypeStruct((T, D), F32), o_spec=pl.BlockSpec((tm, D), lambda i, j, k: (i, 0)))
    return y, (x, h, p, a)


def _ffn_bwd(dy, saved, gain, wg_in, wg_out, tag, on_weight_grads=None):
    x, h, p, a = saved
    T, D = x.shape
    fb, rb = wg_in.shape[-1], wg_out.shape[-2]
    tm, tn = min(T, 1024), 512
    da = _mm(name=f"{tag}_out_dx", grid=(T // tm, FF_HALF, 1), tile=(tm, fb), tb=True, scale=0.5,
             a=dy, a_spec=pl.BlockSpec((tm, D), lambda i, j, k: (i, 0)),
             b=wg_out.reshape(N_DEV * rb, D), b_spec=pl.BlockSpec((fb, D), lambda i, j, k: (j, 0)),
             out_shape=jax.ShapeDtypeStruct((FF_HALF, T, fb), BF16), o_spec=pl.BlockSpec((None, tm, fb), lambda i, j, k: (j, i, 0)))
    d_w_out = _mm(name=f"{tag}_out_dw", grid=(FF_HALF, D // tn, 1), tile=(fb, tn), ta=True, scale=0.5,
                  a=a, a_spec=pl.BlockSpec((None, T, fb), lambda i, j, k: (i, 0, 0)),
                  b=dy, b_spec=pl.BlockSpec((T, tn), lambda i, j, k: (0, j)),
                  out_shape=jax.ShapeDtypeStruct((FF_HALF, fb, D), BF16), o_spec=pl.BlockSpec((None, fb, tn), lambda i, j, k: (i, 0, j)))
    dp = _swiglu_bwd(da, p, name=f"{tag}_act_bwd")
    d_w_in = _mm(name=f"{tag}_in_dw", grid=(1, N_DEV, 1), tile=(D, fb), ta=True,
                 a=h, a_spec=pl.BlockSpec((T, D), lambda i, j, k: (0, 0)),
                 b=dp, b_spec=pl.BlockSpec((None, T, fb), lambda i, j, k: (j, 0, 0)),
                 out_shape=jax.ShapeDtypeStruct((N_DEV, D, fb), BF16), o_spec=pl.BlockSpec((None, D, fb), lambda i, j, k: (j, 0, 0)))
    dh = _mm(name=f"{tag}_in_dx", grid=(T // tm, 1, N_DEV), tile=(tm, D), tb=True,
             a=dp, a_spec=pl.BlockSpec((None, tm, fb), lambda i, j, k: (k, i, 0)),
             b=wg_in, b_spec=pl.BlockSpec((None, D, fb), lambda i, j, k: (k, 0, 0)),
             out_shape=jax.ShapeDtypeStruct((T, D), F32), o_spec=pl.BlockSpec((tm, D), lambda i, j, k: (i, 0)))
    d_w_out = d_w_out.reshape(N_DEV, rb, D)
    if on_weight_grads is not None:
        gain = gain + on_weight_grads(d_w_in, d_w_out)[0, 0]
    dx, d_gain = _rmsnorm_bwd(dh, x, gain, dy, name=f"{tag}_norm_bwd")
    return dx, d_gain, d_w_in, d_w_out


def _square_mm(a, wg, *, name, transposed=False, out_dtype=F32, resid=None):
    T, D = a.shape
    w = wg.reshape(D, D)
    return _matmul(a, w, tb=transposed, name=name, out_dtype=out_dtype, resid=resid)


def _head_rows(cols, T):
    return cols.T.reshape(HEADS, T // DN_CHUNK, 1, DN_CHUNK)


def _mixer_fwd(x, w, big, tag):
    T = x.shape[0]
    h = _rmsnorm_fwd(x, w["mix_norm"], name=f"{tag}_norm")
    proj = _matmul(h, big["w_main"], name=f"{tag}_proj")
    scal = _matmul(h, big["w_scal"], name=f"{tag}_proj_scal", tn=N_SCAL)
    qkv = _conv_fwd(proj, big["conv_w"], name=f"{tag}_conv")
    b_rows = _head_rows(scal[:, 0:HEADS], T)
    a_rows = _head_rows(scal[:, HEADS:2 * HEADS], T)
    o_a, *states = _dn_fwd(qkv, b_rows, a_rows, w["hp"], name=f"{tag}_dn")
    oa_n = _gated_norm_fwd(o_a, proj, w["dn_out_norm"], name=f"{tag}_dn_norm")
    ya = _square_mm(oa_n, big["w_branch_a"], name=f"{tag}_branch_a")
    o_b, ltot = _sb_fwd(proj, w["sb_q_norm"], w["sb_k_norm"], name=f"{tag}_sb")
    yb = _square_mm(o_b, big["w_branch_b"], name=f"{tag}_branch_b")
    merged = _merge_fwd(ya, yb, proj, name=f"{tag}_merge")
    y = _square_mm(merged, big["w_out"], name=f"{tag}_out", resid=x)
    return y, (x, h, proj, qkv, b_rows, a_rows, o_a, states, oa_n, ya, o_b, ltot, yb, merged)


def _mixer_bwd(dy, saved, w, big, tag, on_weight_grads):
    x, h, proj, qkv, b_rows, a_rows, o_a, states, oa_n, ya, o_b, ltot, yb, merged = saved
    T = x.shape[0]
    g = {}
    d_merged = _square_mm(dy, big["w_out"], transposed=True, name=f"{tag}_out_dx", out_dtype=BF16)
    g["w_out"] = _matmul(merged, dy, ta=True, name=f"{tag}_out_dw", out_dtype=BF16)
    d_ya, d_yb, d_ga, d_gb = _merge_bwd(d_merged, ya, yb, proj, name=f"{tag}_merge_bwd")
    d_oan = _square_mm(d_ya, big["w_branch_a"], transposed=True, name=f"{tag}_branch_a_dx")
    g["w_branch_a"] = _matmul(oa_n, d_ya, ta=True, name=f"{tag}_branch_a_dw", out_dtype=BF16)
    d_ob = _square_mm(d_yb, big["w_branch_b"], transposed=True, name=f"{tag}_branch_b_dx")
    g["w_branch_b"] = _matmul(o_b, d_yb, ta=True, name=f"{tag}_branch_b_dw", out_dtype=BF16)
    d_oa, d_z, g["dn_out_norm"] = _gated_norm_bwd(d_oan, o_a, proj, w["dn_out_norm"], name=f"{tag}_dn_norm_bwd")
    d_qkv, d_b_rows, d_a_rows, d_hp = _dn_bwd(qkv, b_rows, a_rows, w["hp"], *states, d_oa, name=f"{tag}_dn_bwd")
    g["dn_a_log"] = d_hp[:, 0, 0]
    g["dn_dt_bias"] = d_hp[:, 1, 0]
    d_conv_in, g["conv_w"] = _conv_bwd(d_qkv, proj, big["conv_w"], name=f"{tag}_conv_bwd")
    d_sbq, d_sbk, d_sbv, g["sb_q_norm"], g["sb_k_norm"] = _sb_bwd(
        proj, w["sb_q_norm"], w["sb_k_norm"], ltot, d_ob, name=f"{tag}_sb_bwd")
    d_proj = jnp.concatenate([d_conv_in, d_z, d_sbq, d_sbk, d_sbv, d_ga, d_gb], axis=1)
    d_scal = jnp.concatenate([d_b_rows.reshape(HEADS, T).T, d_a_rows.reshape(HEADS, T).T,
                              jnp.zeros((T, N_SCAL - 2 * HEADS), F32)], axis=1).astype(BF16)
    g["w_main"] = _matmul(h, d_proj, ta=True, name=f"{tag}_proj_dw", out_dtype=BF16)
    g["w_scal"] = _matmul(h, d_scal, ta=True, name=f"{tag}_proj_scal_dw", out_dtype=BF16, tn=N_SCAL)
    dh_scal = _matmul(d_scal, big["w_scal"], tb=True, name=f"{tag}_proj_scal_dx")
    dh = _matmul(d_proj, big["w_main"], tb=True, name=f"{tag}_proj_dx", tk=N_MAIN // 4, resid=dh_scal)
    gain = w["mix_norm"] + on_weight_grads(g)[0, 0]
    dx, g["mix_norm"] = _rmsnorm_bwd(dh, x, gain, dy, name=f"{tag}_norm_bwd")
    return dx, g


def _local_step(x, target, layers, weights_of, on_weight_grads):
    saved, bigs = [], []
    for l, w in enumerate(layers):
        big = weights_of(l, 0, x)
        x, s1 = _ffn_fwd(x, w["ffn1_norm"] + big["issued"], big["ffn1_w_in"], big["ffn1_w_out"], f"l{l}_ffn1")
        big.update(weights_of(l, 1, x))
        x, s2 = _mixer_fwd(x, dict(w, mix_norm=w["mix_norm"] + big["issued"]), big, f"l{l}_mix")
        big.update(weights_of(l, 2, x))
        x, s3 = _ffn_fwd(x, w["ffn2_norm"] + big["issued"], big["ffn2_w_in"], big["ffn2_w_out"], f"l{l}_ffn2")
        saved.append((s1, s2, s3))
        bigs.append(big)
    loss, dx = _loss_head(x, target, name="loss_head")
    small = [None] * len(layers)
    for l in reversed(range(len(layers))):
        w, big = layers[l], bigs[l]
        s1, s2, s3 = saved[l]
        dx, g_n2, _, _ = _ffn_bwd(
            dx, s3, w["ffn2_norm"], big["ffn2_w_in"], big["ffn2_w_out"], f"l{l}_ffn2",
            on_weight_grads=lambda g_in, g_out, l=l: on_weight_grads(l, 0, dict(ffn2_w_in=g_in, ffn2_w_out=g_out)))
        dx, g = _mixer_bwd(dx, s2, w, big, f"l{l}_mix", on_weight_grads=lambda g, l=l: on_weight_grads(l, 1, g))
        dx, g_n1, _, _ = _ffn_bwd(
            dx, s1, w["ffn1_norm"], big["ffn1_w_in"], big["ffn1_w_out"], f"l{l}_ffn1",
            on_weight_grads=lambda g_in, g_out, l=l: on_weight_grads(l, 2, dict(ffn1_w_in=g_in, ffn1_w_out=g_out)))
        small[l] = dict(g, ffn1_norm=g_n1, ffn2_norm=g_n2)
    return loss, dx, small


_BIG = ("ffn1_w_in", "ffn1_w_out", "w_in", "w_branch_a", "w_branch_b", "w_out", "ffn2_w_in", "ffn2_w_out")
_STAGES = (("ffn2_w_in", "ffn2_w_out"), ("w_in", "w_branch_a", "w_branch_b", "w_out"), ("ffn1_w_in", "ffn1_w_out"))
_SMALL = ("ffn1_norm", "mix_norm", "ffn2_norm", "dn_a_log", "dn_dt_bias", "dn_out_norm", "sb_q_norm", "sb_k_norm")
_ORDER = ("ffn1_norm", "ffn1_w_in", "ffn1_w_out", "mix_norm", "w_in", "dn_conv_w", "dn_a_log", "dn_dt_bias", "dn_out_norm",
          "sb_q_norm", "sb_k_norm", "w_branch_a", "w_branch_b", "w_out", "ffn2_norm", "ffn2_w_in", "ffn2_w_out")
COL_SCAL = 4 * D_MODEL
SCAL_SLOT = COL_SCAL // (N_IN // N_DEV)
SCAL_AT = COL_SCAL % (N_IN // N_DEV)
assert SCAL_AT + 2 * HEADS <= N_IN // N_DEV


def _pad_rows(a, multiple):
    pad = (-a.shape[-2]) % multiple
    return a if pad == 0 else jnp.pad(a, [(0, 0)] * (a.ndim - 2) + [(0, pad), (0, 0)])


def _lane_rows(a):
    flat = a.reshape(-1)
    flat = jnp.pad(flat, (0, (-flat.shape[0]) % 128))
    return flat.reshape(-1, 128)


def _pack_small(named):
    pieces, spans, r = [], {}, 0
    for n, a in named:
        rows = _lane_rows(a)
        spans[n] = (r, r + rows.shape[0], a.shape)
        r += rows.shape[0]
        pieces.append(rows)
    return _pad_rows(jnp.concatenate(pieces, axis=0), 8), spans


def _unpack_small(packed, spans, n):
    r0, r1, shape = spans[n]
    return packed[r0:r1].reshape(-1)[:math.prod(shape)].reshape(shape)


def kernel(x, ffn1_norm, ffn1_w_in, ffn1_w_out, mix_norm, w_in, dn_conv_w, dn_a_log, dn_dt_bias, dn_out_norm, sb_q_norm, sb_k_norm, w_branch_a, w_branch_b, w_out, ffn2_norm, ffn2_w_in, ffn2_w_out, loss_target, m_ffn1_norm, m_ffn1_w_in, m_ffn1_w_out, m_mix_norm, m_w_in, m_dn_conv_w, m_dn_a_log, m_dn_dt_bias, m_dn_out_norm, m_sb_q_norm, m_sb_k_norm, m_w_branch_a, m_w_branch_b, m_w_out, m_ffn2_norm, m_ffn2_w_in, m_ffn2_w_out, v_ffn1_norm, v_ffn1_w_in, v_ffn1_w_out, v_mix_norm, v_w_in, v_dn_conv_w, v_dn_a_log, v_dn_dt_bias, v_dn_out_norm, v_sb_q_norm, v_sb_k_norm, v_w_branch_a, v_w_branch_b, v_w_out, v_ffn2_norm, v_ffn2_w_in, v_ffn2_w_out):
    given = dict(locals())
    weights = {n: given[n] for n in _ORDER}
    mom_m = {n: given["m_" + n] for n in _ORDER}
    mom_v = {n: given["v_" + n] for n in _ORDER}
    L = ffn1_norm.shape[0]
    ax, ay, ac = _position()
    my_slot = 4 * ax + 2 * ay + ac

    conv_cols = dn_conv_w.shape[-1]
    first_ffn = _STAGES[2]
    later = tuple(n for n in _BIG if n not in first_ffn)
    gathers, landed = {}, {}

    def start_gather(key, names, l, zero, extra=()):
        shards = [(weights[n][l] + zero).astype(BF16) for n in names] + list(extra)
        first, shards, landing, token = _relay_gather_start(shards, name=f"gather_start_{key}")
        gathers[key] = dict(first=first, shards=shards, landing=landing, names=names)
        return token[0, 0]

    def pass_on(key, after):
        g = gathers[key]
        g["relay"], g["shards"], g["landing"], token = _relay_gather_pass_on(
            g["first"], g["shards"], g["landing"], [after], name=f"gather_pass_on_{key}")
        return token[0, 0]

    def wait_gather(key, after):
        g = gathers.pop(key)
        arrays, token = _relay_gather_wait(g["first"], g["relay"], g["shards"], g["landing"], [after],
                                           name=f"gather_wait_{key}")
        landed.update(zip(g["names"], arrays))
        if len(arrays) > len(g["names"]):
            landed["conv"] = arrays[-1]
        return token[0, 0]

    def weights_of(l, part, x_in):
        if l == 0 and part == 0:
            start_gather("l0_ffn1", first_ffn, 0, 0.0)
            pass_on("l0_ffn1", x_in)
            issued = start_gather("l0", later, 0, wait_gather("l0_ffn1", x_in), [_pad_rows(_lane_rows(dn_conv_w), 8)])
            return dict({n: landed.pop(n) for n in first_ffn}, issued=issued)
        if part == 0:
            token = wait_gather(f"l{l}", x_in)
            issued = start_gather(f"l{l + 1}", _BIG, l + 1, token) if l + 1 < L else token
            return dict({n: landed.pop(n) for n in first_ffn}, issued=issued)
        if part == 2:
            return dict(issued=pass_on(f"l{l + 1}", x_in) if l + 1 < L else 0.0)
        issued = 0.0
        if l == 0:
            pass_on("l0", x_in)
            issued = start_gather("l1", _BIG, 1, wait_gather("l0", x_in)) if L > 1 else 0.0
            conv = landed.pop("conv").reshape(N_DEV, -1)[:, :L * DN_CONV * conv_cols]
            landed["conv_w"] = conv.reshape(N_DEV, L, DN_CONV, conv_cols).transpose(1, 2, 0, 3).reshape(
                L, DN_CONV, N_DEV * conv_cols)
        big = {n: landed.pop(n) for n in later}
        wi = big.pop("w_in")
        pieces = [wi[d] for d in range(N_DEV)]
        pieces[SCAL_SLOT:SCAL_SLOT + 1] = [wi[SCAL_SLOT][:, :SCAL_AT], wi[SCAL_SLOT][:, SCAL_AT + 2 * HEADS:]]
        big["w_main"] = jnp.concatenate(pieces, axis=1)
        big["w_scal"] = jnp.pad(wi[SCAL_SLOT][:, SCAL_AT:SCAL_AT + 2 * HEADS], ((0, 0), (0, N_SCAL - 2 * HEADS)))
        big["conv_w"] = landed["conv_w"][l]
        return dict(big, issued=issued)

    layers = []
    for l in range(L):
        hp = jnp.concatenate([jnp.broadcast_to(dn_a_log[l][:, None, None], (HEADS, 1, 128)),
                              jnp.broadcast_to(dn_dt_bias[l][:, None, None], (HEADS, 1, 128)),
                              jnp.zeros((HEADS, 6, 128), F32)], axis=1)
        layers.append(dict(ffn1_norm=ffn1_norm[l][None], mix_norm=mix_norm[l][None], hp=hp,
                           dn_out_norm=dn_out_norm[l][None], sb_q_norm=sb_q_norm[l][None],
                           sb_k_norm=sb_k_norm[l][None], ffn2_norm=ffn2_norm[l][None]))

    in_flight = {}

    def on_weight_grads(l, stage, g):
        parts = dict(g)
        if stage == 1:
            gm, shard = g["w_main"], N_IN // N_DEV
            blocks = [gm[:, d * shard:(d + 1) * shard] for d in range(SCAL_SLOT)]
            blocks.append(jnp.concatenate([gm[:, SCAL_SLOT * shard:COL_SCAL], g["w_scal"][:, :2 * HEADS],
                                           gm[:, COL_SCAL:(SCAL_SLOT + 1) * shard - 2 * HEADS]], axis=1))
            blocks += [gm[:, d * shard - 2 * HEADS:(d + 1) * shard - 2 * HEADS] for d in range(SCAL_SLOT + 1, N_DEV)]
            parts["w_in"] = jnp.stack(blocks)
            for n in ("w_branch_a", "w_branch_b", "w_out"):
                parts[n] = g[n].reshape(N_DEV, D_MODEL // N_DEV, D_MODEL)
        *in_flight[l, stage], token = _exchange_start([parts[n] for n in _STAGES[stage]], gather=False,
                                                      name=f"scatter_start_l{l}_{stage}")
        return token

    loss_row, dx, grads = _local_step(x[0], loss_target[0], layers, weights_of, on_weight_grads)
    loss = lax.psum(loss_row[0, 0], ("x", "y", "c"))

    results = {n: None for n in _BIG}
    after = [dx]
    for l in reversed(range(L)):
        for stage, names in enumerate(_STAGES):
            landed, _ = _exchange_wait(*in_flight[l, stage], after, gather=False, name=f"scatter_wait_l{l}_{stage}")
            for n, parts in zip(names, landed):
                results[n] = _adamw(parts, weights[n], mom_m[n], mom_v[n], layer=l, earlier=results[n],
                                    name=f"adamw_{n}_l{l}")
            after = [results[n][0] for n in names]
    out = {n: tuple(results[n]) for n in _BIG}

    small_grads = [(n, jnp.stack([g[n].reshape(weights[n].shape[1:]) for g in grads])) for n in _SMALL]
    small_packed, spans = _pack_small(small_grads + [("conv", jnp.stack([g["conv_w"] for g in grads]))])
    small_sum = _sum_parts(_all_gather([small_packed], name="gather_small_grads")[0], name="sum_small_grads")
    rep_rows = spans["conv"][0]
    pack_rep = lambda d: _pad_rows(_pack_small([(n, d[n]) for n in _SMALL])[0], 8)
    rep_pad = (-rep_rows) % 8
    g_rep = jnp.pad(small_sum[:rep_rows], ((0, rep_pad), (0, 0)))
    res = _adamw(g_rep[None], pack_rep(weights)[None], pack_rep(mom_m)[None], pack_rep(mom_v)[None],
                 name="adamw_replicated")
    for n in _SMALL:
        out[n] = tuple(_unpack_small(t[0], spans, n) for t in res)
    conv_sum = _unpack_small(small_sum, spans, "conv")
    conv_mine = lax.dynamic_slice_in_dim(conv_sum, my_slot * conv_cols, conv_cols, axis=2).reshape(L * DN_CONV, conv_cols)
    flat = lambda t: t.reshape(1, L * DN_CONV, conv_cols)
    res = _adamw(conv_mine[None], flat(dn_conv_w), flat(m_dn_conv_w), flat(v_dn_conv_w), name="adamw_conv")
    out["dn_conv_w"] = tuple(t.reshape(L, DN_CONV, conv_cols) for t in res)

    return (loss, dx[None], *[out[n][0] for n in _ORDER], *[out[n][1] for n in _ORDER],
            *[out[n][2] for n in _ORDER], *[out[n][3] for n in _ORDER])
```

```python
import functools
import math

import jax
import jax.numpy as jnp
from jax import lax
from jax.experimental import pallas as pl
from jax.experimental.pallas import tpu as pltpu

F32 = jnp.float32
BF16 = jnp.bfloat16

N_DEV = 8
D_MODEL = 1024
DEPTH = 4
D_FF = 2816
HEADS = 8
HEAD_DIM = 128
DN_CHUNK = 64
DN_CONV = 4
DN_GROUP = 8
DN_HEADS = 2
SB_BLOCK = 128
SB_KEY_TILE = 512
SB_HEADS = 4
SB_HEADS_BWD = 2
SB_KEY_TILE_BWD = 512
RMS_EPS = 1e-6
L2_EPS = 1e-6
N_IN = 9232
N_MAIN = 9216
N_SCAL = 128
QK_SCALE = HEAD_DIM ** -0.5

ADAM_LR = 0.001
ADAM_B1 = 0.9
ADAM_B2 = 0.999
ADAM_EPS = 1e-08
ADAM_WD = 0.01
ADAM_STEP = 10

V7X_VMEM_LIMIT = 56 * 1024 * 1024
MESH = pl.DeviceIdType.MESH
ANY = pl.BlockSpec(memory_space=pl.ANY)


def _params(sem=None, vmem=V7X_VMEM_LIMIT):
    return pltpu.CompilerParams(dimension_semantics=sem, vmem_limit_bytes=vmem)


def _sigmoid(x):
    return 1.0 / (1.0 + jnp.exp(-x))


def _softplus(x):
    return jnp.maximum(x, 0.0) + jnp.log(1.0 + jnp.exp(-jnp.abs(x)))


def _bdot(a, b, dims=(((1,), (0,)), ((), ()))):
    return lax.dot_general(a.astype(BF16), b.astype(BF16), dims, preferred_element_type=F32)


_NT = (((1,), (1,)), ((), ()))
_TN = (((0,), (0,)), ((), ()))


def _hdot(a, b, dims=(((1,), (0,)), ((), ()))):
    a_hi = a.astype(BF16)
    b_hi = b.astype(BF16)
    a_lo = (a - a_hi.astype(F32)).astype(BF16)
    b_lo = (b - b_hi.astype(F32)).astype(BF16)
    dot = functools.partial(lax.dot_general, dimension_numbers=dims, preferred_element_type=F32)
    return dot(a_hi, b_hi) + (dot(a_hi, b_lo) + dot(a_lo, b_hi))


def _hdot_tn(a, b):
    return _hdot(a, b, _TN)


def _mm(*, name, grid, a, a_spec, b, b_spec, out_shape, o_spec, tile, ta=False, tb=False, resid=None, scale=1.0):
    nk = grid[2]
    dims = (((0 if ta else 1,), (1 if tb else 0,)), ((), ()))

    def flat(v):
        return v if v.ndim == 2 else v.reshape(-1, v.shape[-1])

    def body(*refs):
        a_ref, b_ref = refs[:2]
        r_ref = refs[2] if resid is not None else None
        o_ref = refs[3] if resid is not None else refs[2]
        part = lax.dot_general(flat(a_ref[...]).astype(BF16), flat(b_ref[...]).astype(BF16), dims,
                               preferred_element_type=F32)

        def finish(acc):
            if scale != 1.0:
                acc = acc * scale
            if r_ref is not None:
                acc = r_ref[...] + acc
            o_ref[...] = acc.astype(o_ref.dtype)

        if nk == 1:
            finish(part)
        else:
            acc_ref = refs[-1]
            k = pl.program_id(2)

            @pl.when(k == 0)
            def _():
                acc_ref[...] = part

            @pl.when(k > 0)
            def _():
                acc_ref[...] += part

            @pl.when(k == nk - 1)
            def _():
                finish(acc_ref[...])

    in_specs = [a_spec, b_spec] + ([pl.BlockSpec(tile, lambda i, j, k: (i, j))] if resid is not None else [])
    args = (a, b) + ((resid,) if resid is not None else ())
    return pl.pallas_call(
        body, name=name, grid=grid, in_specs=in_specs, out_specs=o_spec, out_shape=out_shape,
        scratch_shapes=[pltpu.VMEM(tile, F32)] if nk > 1 else [],
        compiler_params=_params(("parallel", "parallel", "arbitrary")),
    )(*args)


def _matmul(a, b, *, name, ta=False, tb=False, out_dtype=F32, tm=None, tn=None, tk=None, resid=None, scale=1.0):
    if ta:
        K, M = a.shape
    else:
        M, K = a.shape
    N = b.shape[0] if tb else b.shape[1]
    tm = tm or min(M, 1024)
    tn = tn or min(N, 512)
    tk = tk or K
    assert M % tm == 0 and N % tn == 0 and K % tk == 0, (name, M, N, K, tm, tn, tk)
    a_spec = pl.BlockSpec((tk, tm), lambda i, j, k: (k, i)) if ta else pl.BlockSpec((tm, tk), lambda i, j, k: (i, k))
    b_spec = pl.BlockSpec((tn, tk), lambda i, j, k: (j, k)) if tb else pl.BlockSpec((tk, tn), lambda i, j, k: (k, j))
    return _mm(name=name, grid=(M // tm, N // tn, K // tk), a=a, a_spec=a_spec, b=b, b_spec=b_spec,
               out_shape=jax.ShapeDtypeStruct((M, N), out_dtype), o_spec=pl.BlockSpec((tm, tn), lambda i, j, k: (i, j)),
               tile=(tm, tn), ta=ta, tb=tb, resid=resid, scale=scale)


ROW_TILE = 256


def _rmsnorm_fwd(x, gain, *, name):
    T, D = x.shape

    def body(x_ref, g_ref, o_ref):
        xf = x_ref[...]
        r = lax.rsqrt(jnp.mean(xf * xf, axis=-1, keepdims=True) + RMS_EPS)
        o_ref[...] = (xf * r * g_ref[...]).astype(o_ref.dtype)

    return pl.pallas_call(
        body, name=name, grid=(T // ROW_TILE,),
        in_specs=[pl.BlockSpec((ROW_TILE, D), lambda i: (i, 0)), pl.BlockSpec((1, D), lambda i: (0, 0))],
        out_specs=pl.BlockSpec((ROW_TILE, D), lambda i: (i, 0)),
        out_shape=jax.ShapeDtypeStruct((T, D), BF16), compiler_params=_params(("parallel",)),
    )(x, gain)


def _rmsnorm_bwd(dh, x, gain, dres, *, name):
    T, D = x.shape

    def body(dh_ref, x_ref, g_ref, res_ref, dx_ref, dg_ref):
        xf = x_ref[...]
        r = lax.rsqrt(jnp.mean(xf * xf, axis=-1, keepdims=True) + RMS_EPS)
        y = xf * r
        dh_v = dh_ref[...].astype(F32)
        dy = dh_v * g_ref[...]
        dx_ref[...] = res_ref[...] + r * (dy - y * jnp.mean(dy * y, axis=-1, keepdims=True))

        @pl.when(pl.program_id(0) == 0)
        def _():
            dg_ref[...] = jnp.zeros_like(dg_ref)

        dg_ref[...] += jnp.sum(dh_v * y, axis=0, keepdims=True)

    row = pl.BlockSpec((ROW_TILE, D), lambda i: (i, 0))
    vec = pl.BlockSpec((1, D), lambda i: (0, 0))
    return pl.pallas_call(
        body, name=name, grid=(T // ROW_TILE,), in_specs=[row, row, vec, row], out_specs=(row, vec),
        out_shape=(jax.ShapeDtypeStruct((T, D), F32), jax.ShapeDtypeStruct((1, D), F32)),
        compiler_params=_params(("arbitrary",)),
    )(dh, x, gain, dres)


FF_HALF = N_DEV // 2


def _swiglu_fwd(p, *, name):
    _, T, fb = p.shape

    def body(g_ref, u_ref, o_ref):
        g = g_ref[...].astype(F32)
        o_ref[...] = (g * _sigmoid(g) * u_ref[...].astype(F32)).astype(o_ref.dtype)

    blk = (None, ROW_TILE, fb)
    return pl.pallas_call(
        body, name=name, grid=(T // ROW_TILE, FF_HALF),
        in_specs=[pl.BlockSpec(blk, lambda i, j: (j, i, 0)), pl.BlockSpec(blk, lambda i, j: (j + FF_HALF, i, 0))],
        out_specs=pl.BlockSpec(blk, lambda i, j: (j, i, 0)),
        out_shape=jax.ShapeDtypeStruct((FF_HALF, T, fb), BF16), compiler_params=_params(("parallel", "parallel")),
    )(p, p)


def _swiglu_bwd(da, p, *, name):
    _, T, fb = p.shape

    def body(da_ref, g_ref, u_ref, o_ref):
        g = g_ref[...].astype(F32)
        u = u_ref[...].astype(F32)
        d = da_ref[...].astype(F32)
        s = _sigmoid(g)
        o_ref[0] = (d * u * (s * (1.0 + g * (1.0 - s)))).astype(o_ref.dtype)
        o_ref[1] = (d * g * s).astype(o_ref.dtype)

    blk = (None, ROW_TILE, fb)
    out = pl.pallas_call(
        body, name=name, grid=(T // ROW_TILE, FF_HALF),
        in_specs=[pl.BlockSpec(blk, lambda i, j: (j, i, 0)), pl.BlockSpec(blk, lambda i, j: (j, i, 0)),
                  pl.BlockSpec(blk, lambda i, j: (j + FF_HALF, i, 0))],
        out_specs=pl.BlockSpec((2, None, ROW_TILE, fb), lambda i, j: (0, j, i, 0)),
        out_shape=jax.ShapeDtypeStruct((2, FF_HALF, T, fb), BF16), compiler_params=_params(("parallel", "parallel")),
    )(da, p, p)
    return out.reshape(2 * FF_HALF, T, fb)


COL_GATE_A = 7
COL_GATE_B = 8


def _merge_fwd(ya, yb, proj, *, name):
    T, D = ya.shape

    def body(ya_ref, yb_ref, ga_ref, gb_ref, o_ref):
        o_ref[...] = (_sigmoid(ga_ref[...]) * ya_ref[...] + _sigmoid(gb_ref[...]) * yb_ref[...]).astype(o_ref.dtype)

    row = pl.BlockSpec((ROW_TILE, D), lambda i: (i, 0))
    return pl.pallas_call(
        body, name=name, grid=(T // ROW_TILE,),
        in_specs=[row, row, pl.BlockSpec((ROW_TILE, D), lambda i: (i, COL_GATE_A)),
                  pl.BlockSpec((ROW_TILE, D), lambda i: (i, COL_GATE_B))],
        out_specs=row, out_shape=jax.ShapeDtypeStruct((T, D), BF16), compiler_params=_params(("parallel",)),
    )(ya, yb, proj, proj)


def _merge_bwd(dm, ya, yb, proj, *, name):
    T, D = ya.shape

    def body(dm_ref, ya_ref, yb_ref, ga_ref, gb_ref, dya_ref, dyb_ref, dga_ref, dgb_ref):
        d = dm_ref[...].astype(F32)
        sa = _sigmoid(ga_ref[...])
        sb = _sigmoid(gb_ref[...])
        dya_ref[...] = (d * sa).astype(BF16)
        dyb_ref[...] = (d * sb).astype(BF16)
        dga_ref[...] = (d * ya_ref[...] * sa * (1.0 - sa)).astype(BF16)
        dgb_ref[...] = (d * yb_ref[...] * sb * (1.0 - sb)).astype(BF16)

    row = pl.BlockSpec((ROW_TILE, D), lambda i: (i, 0))
    out = jax.ShapeDtypeStruct((T, D), BF16)
    return pl.pallas_call(
        body, name=name, grid=(T // ROW_TILE,),
        in_specs=[row, row, row, pl.BlockSpec((ROW_TILE, D), lambda i: (i, COL_GATE_A)),
                  pl.BlockSpec((ROW_TILE, D), lambda i: (i, COL_GATE_B))],
        out_specs=(row, row, row, row), out_shape=(out, out, out, out), compiler_params=_params(("parallel",)),
    )(dm, ya, yb, proj, proj)


def _loss_head(y, target, *, name):
    T, D = y.shape

    def body(y_ref, t_ref, loss_ref, dy_ref):
        err = y_ref[...] - t_ref[...]
        dy_ref[...] = err * (1.0 / D)

        @pl.when(pl.program_id(0) == 0)
        def _():
            loss_ref[...] = jnp.zeros_like(loss_ref)

        loss_ref[...] += 0.5 * jnp.sum(jnp.sum(err * err, axis=-1, keepdims=True) * (1.0 / D), axis=0, keepdims=True)

    row = pl.BlockSpec((ROW_TILE, D), lambda i: (i, 0))
    return pl.pallas_call(
        body, name=name, grid=(T // ROW_TILE,), in_specs=[row, row],
        out_specs=(pl.BlockSpec((1, 128), lambda i: (0, 0)), row),
        out_shape=(jax.ShapeDtypeStruct((1, 128), F32), jax.ShapeDtypeStruct((T, D), F32)),
        compiler_params=_params(("arbitrary",)),
    )(y, target)


CONV_PAD = 8


def _conv_taps(w, xp, T, first):
    acc = w[0:1, :] * xp[pl.ds(first, T), :]
    for i in range(1, DN_CONV):
        acc = acc + w[i:i + 1, :] * xp[pl.ds(first + i, T), :]
    return acc


def _conv_fwd(proj, conv_w, *, name):
    T = proj.shape[0]

    def body(x_ref, w_ref, o_ref, xp):
        xp[0:CONV_PAD, :] = jnp.zeros((CONV_PAD, HEAD_DIM), F32)
        xp[CONV_PAD:, :] = x_ref[...]
        y = _conv_taps(w_ref[...], xp, T, CONV_PAD - (DN_CONV - 1))
        s = y * _sigmoid(y)
        n = s * lax.rsqrt(jnp.sum(s * s, axis=-1, keepdims=True) + L2_EPS)
        o_ref[0] = jnp.where(pl.program_id(0) < 2, n, s)

    return pl.pallas_call(
        body, name=name, grid=(3, HEADS),
        in_specs=[pl.BlockSpec((T, HEAD_DIM), lambda c, h: (0, c * HEADS + h)),
                  pl.BlockSpec((DN_CONV, HEAD_DIM), lambda c, h: (0, c * HEADS + h))],
        out_specs=pl.BlockSpec((1, T, HEAD_DIM), lambda c, h: (c, 0, h)),
        out_shape=jax.ShapeDtypeStruct((3, T, D_MODEL), F32),
        scratch_shapes=[pltpu.VMEM((T + CONV_PAD, HEAD_DIM), F32)],
        compiler_params=_params(("parallel", "parallel")),
    )(proj, conv_w)


def _conv_bwd(dqkv, proj, conv_w, *, name):
    T = proj.shape[0]

    def body(d_ref, x_ref, w_ref, dx_ref, dw_ref, xp, dyp):
        xp[0:CONV_PAD, :] = jnp.zeros((CONV_PAD, HEAD_DIM), F32)
        xp[CONV_PAD:, :] = x_ref[...]
        w = w_ref[...]
        y = _conv_taps(w, xp, T, CONV_PAD - (DN_CONV - 1))
        sg = _sigmoid(y)
        s = y * sg
        r = lax.rsqrt(jnp.sum(s * s, axis=-1, keepdims=True) + L2_EPS)
        n = s * r
        d = d_ref[0]
        ds = jnp.where(pl.program_id(0) < 2, r * (d - n * jnp.sum(d * n, axis=-1, keepdims=True)), d)
        dy = ds * (sg * (1.0 + y * (1.0 - sg)))
        dyp[0:T, :] = dy
        dyp[T:, :] = jnp.zeros((CONV_PAD, HEAD_DIM), F32)
        dx = w[0:1, :] * dyp[pl.ds(DN_CONV - 1, T), :]
        for i in range(1, DN_CONV):
            dx = dx + w[i:i + 1, :] * dyp[pl.ds(DN_CONV - 1 - i, T), :]
        dx_ref[...] = dx.astype(dx_ref.dtype)
        for i in range(DN_CONV):
            dw_ref[i:i + 1, :] = jnp.sum(dy * xp[pl.ds(CONV_PAD - (DN_CONV - 1) + i, T), :], axis=0, keepdims=True)

    col = lambda c, h: (0, c * HEADS + h)
    return pl.pallas_call(
        body, name=name, grid=(3, HEADS),
        in_specs=[pl.BlockSpec((1, T, HEAD_DIM), lambda c, h: (c, 0, h)), pl.BlockSpec((T, HEAD_DIM), col),
                  pl.BlockSpec((DN_CONV, HEAD_DIM), col)],
        out_specs=(pl.BlockSpec((T, HEAD_DIM), col), pl.BlockSpec((DN_CONV, HEAD_DIM), col)),
        out_shape=(jax.ShapeDtypeStruct((T, 3 * D_MODEL), BF16), jax.ShapeDtypeStruct((DN_CONV, 3 * D_MODEL), F32)),
        scratch_shapes=[pltpu.VMEM((T + CONV_PAD, HEAD_DIM), F32), pltpu.VMEM((T + CONV_PAD, HEAD_DIM), F32)],
        compiler_params=_params(("parallel", "parallel")),
    )(dqkv, proj, conv_w)


def _inv_unit_lower(low, eye):
    x = eye - low
    power = _hdot(low, low, _B_NN)
    steps = int(math.log2(DN_CHUNK)) - 1
    for s in range(steps):
        x = x + _hdot(x, power, _B_NN)
        if s + 1 < steps:
            power = _hdot(power, power, _B_NN)
    return x


_B_NN = (((2,), (1,)), ((0,), (0,)))
_B_NT = (((2,), (2,)), ((0,), (0,)))
_B_TN = (((1,), (1,)), ((0,), (0,)))


def _dn_load(ref, lead, r0, group):
    rows = pl.ds(r0, group * DN_CHUNK)
    cols = lambda h: slice(h * HEAD_DIM, (h + 1) * HEAD_DIM)
    per_head = [(ref[rows, cols(h)] if lead is None else ref[lead, rows, cols(h)]).reshape(group, DN_CHUNK, HEAD_DIM)
                for h in range(DN_HEADS)]
    return jnp.stack(per_head, axis=1).reshape(group * DN_HEADS, DN_CHUNK, HEAD_DIM)


def _dn_chunk_setup(qkv_ref, b_ref, a_ref, hp_ref, n0, group, tinv=None):
    C = DN_CHUNK
    B = group * DN_HEADS
    r0 = pl.multiple_of(n0 * C, C)
    q = _dn_load(qkv_ref, 0, r0, group) * QK_SCALE
    k = _dn_load(qkv_ref, 1, r0, group)
    v = _dn_load(qkv_ref, 2, r0, group)
    ii = lax.broadcasted_iota(jnp.int32, (B, C, C), 1)
    jj = lax.broadcasted_iota(jnp.int32, (B, C, C), 2)
    eye_mask = ii == jj
    eye = jnp.where(eye_mask, 1.0, 0.0).astype(F32)

    def to_col(row):
        return jnp.sum(jnp.where(eye_mask, jnp.broadcast_to(row, (B, C, C)), 0.0), axis=2, keepdims=True)

    def to_row(col):
        return jnp.sum(jnp.where(eye_mask, jnp.broadcast_to(col, (B, C, C)), 0.0), axis=1, keepdims=True)

    def rows(ref):
        return jnp.stack([ref[h, pl.ds(n0, group)] for h in range(DN_HEADS)], axis=1).reshape(B, 1, C)

    def per_head(row):
        return jnp.stack([hp_ref[h, row:row + 1, 0:C] for h in range(DN_HEADS)] * group, axis=0)

    b_row = rows(b_ref)
    a_row = rows(a_ref)
    a_log = per_head(0)
    dt_b = per_head(1)
    beta_row = _sigmoid(b_row)
    neg_ea = -jnp.exp(a_log)
    g_row = neg_ea * _softplus(a_row + dt_b)
    gc_col = jnp.sum(jnp.where(jj <= ii, jnp.broadcast_to(g_row, (B, C, C)), 0.0), axis=2, keepdims=True)
    gc_row = to_row(gc_col)
    g_last = jnp.sum(g_row, axis=2, keepdims=True)
    beta = to_col(beta_row)
    low_incl = ii >= jj
    decay = jnp.exp(jnp.where(low_incl, gc_col - gc_row, -jnp.inf))
    eg = jnp.exp(gc_col)
    egl = jnp.exp(g_last - gc_col)
    el = jnp.exp(g_last)
    kb = k * beta
    pmat = _bdot(kb, k, _B_NT)
    low = jnp.where(ii > jj, pmat * decay, 0.0)
    if tinv is None:
        tinv = _inv_unit_lower(low, eye)
    u = _hdot(tinv, v * beta, _B_NN)
    w = _hdot(tinv, kb * eg, _B_NN)
    qk = _bdot(q, k, _B_NT)
    attn = qk * decay
    return dict(q=q, k=k, v=v, ii=ii, jj=jj, to_col=to_col, to_row=to_row, b_row=b_row, a_row=a_row, dt_b=dt_b,
                beta_row=beta_row, neg_ea=neg_ea, g_row=g_row, gc_col=gc_col, g_last=g_last, beta=beta,
                decay=decay, eg=eg, egl=egl, el=el, kb=kb, pmat=pmat, tinv=tinv, u=u, w=w, qk=qk, attn=attn,
                qd=q * eg, kd=k * egl, r0=r0)


def _dn_store(ref, lead, r0, group, value):
    value = value.reshape(group, DN_HEADS, DN_CHUNK, HEAD_DIM)
    for h in range(DN_HEADS):
        block = value[:, h].reshape(group * DN_CHUNK, HEAD_DIM)
        if lead is None:
            ref[pl.ds(r0, group * DN_CHUNK), h * HEAD_DIM:(h + 1) * HEAD_DIM] = block
        else:
            ref[lead, pl.ds(r0, group * DN_CHUNK), h * HEAD_DIM:(h + 1) * HEAD_DIM] = block


def _dn_specs(T):
    nc = T // DN_CHUNK
    qkv = pl.BlockSpec((3, T, DN_HEADS * HEAD_DIM), lambda h: (0, 0, h))
    rows = pl.BlockSpec((DN_HEADS, nc, 1, DN_CHUNK), lambda h: (h, 0, 0, 0))
    hp = pl.BlockSpec((DN_HEADS, 8, 128), lambda h: (h, 0, 0))
    states = pl.BlockSpec((DN_HEADS, nc, HEAD_DIM, HEAD_DIM), lambda h: (h, 0, 0, 0))
    return nc, qkv, rows, hp, states


def _dn_inverse_spec(T):
    return pl.BlockSpec((DN_HEADS, T // DN_CHUNK, DN_CHUNK, DN_CHUNK), lambda h: (h, 0, 0, 0))


def _dn_per_head(ref, n0, group):
    stacked = jnp.stack([ref[h, pl.ds(n0, group)] for h in range(DN_HEADS)], axis=1)
    return stacked.reshape((group * DN_HEADS,) + stacked.shape[2:])


def _dn_fwd(qkv, b_rows, a_rows, hp, *, name):
    T = qkv.shape[1]
    nc, qkv_spec, row_spec, hp_spec, st_spec = _dn_specs(T)
    group = math.gcd(nc, DN_GROUP)
    H = DN_HEADS

    def body(qkv_ref, b_ref, a_ref, hp_ref, o_ref, st_ref, inv_ref, s_scr):
        s_scr[...] = jnp.zeros_like(s_scr)

        def step(t, carry):
            n0 = t * group
            c = _dn_chunk_setup(qkv_ref, b_ref, a_ref, hp_ref, n0, group)
            tinv = c["tinv"].reshape(group, H, DN_CHUNK, DN_CHUNK)
            for h in range(H):
                inv_ref[h, pl.ds(n0, group)] = tinv[:, h]
            state = s_scr[...]
            outs = []
            for g in range(group):
                sl = slice(g * H, (g + 1) * H)
                for h in range(H):
                    st_ref[h, n0 + g] = state[h]
                v_new = c["u"][sl] - _bdot(c["w"][sl], state, _B_NN)
                outs.append(_bdot(c["qd"][sl], state, _B_NN) + _bdot(c["attn"][sl], v_new, _B_NN))
                state = state * c["el"][sl] + _bdot(c["kd"][sl], v_new, _B_TN)
            s_scr[...] = state
            _dn_store(o_ref, None, c["r0"], group, jnp.concatenate(outs, axis=0))
            return carry

        lax.fori_loop(0, nc // group, step, 0)

    return pl.pallas_call(
        body, name=name, grid=(HEADS // H,), in_specs=[qkv_spec, row_spec, row_spec, hp_spec],
        out_specs=(pl.BlockSpec((T, H * HEAD_DIM), lambda h: (0, h)), st_spec, _dn_inverse_spec(T)),
        out_shape=(jax.ShapeDtypeStruct((T, D_MODEL), F32),
                   jax.ShapeDtypeStruct((HEADS, nc, HEAD_DIM, HEAD_DIM), F32),
                   jax.ShapeDtypeStruct((HEADS, nc, DN_CHUNK, DN_CHUNK), F32)),
        scratch_shapes=[pltpu.VMEM((H, HEAD_DIM, HEAD_DIM), F32)], compiler_params=_params(("parallel",)),
    )(qkv, b_rows, a_rows, hp)


def _dn_bwd(qkv, b_rows, a_rows, hp, states, inverses, do, *, name):
    T = qkv.shape[1]
    C = DN_CHUNK
    nc, qkv_spec, row_spec, hp_spec, st_spec = _dn_specs(T)
    group = math.gcd(nc, DN_GROUP)
    H = DN_HEADS
    B = group * H

    def body(qkv_ref, b_ref, a_ref, hp_ref, st_ref, inv_ref, do_ref, dqkv_ref, db_ref, da_ref, dhp_ref, ds_scr, acc_scr):
        ds_scr[...] = jnp.zeros_like(ds_scr)
        acc_scr[...] = jnp.zeros_like(acc_scr)

        def step(t, carry):
            n0 = nc - (t + 1) * group
            c = _dn_chunk_setup(qkv_ref, b_ref, a_ref, hp_ref, n0, group, tinv=_dn_per_head(inv_ref, n0, group))
            state = _dn_per_head(st_ref, n0, group)
            d_o = _dn_load(do_ref, None, c["r0"], group)
            v_new = c["u"] - _bdot(c["w"], state, _B_NN)
            d_vnew_local = _bdot(c["attn"], d_o, _B_TN)
            d_state_local = _bdot(c["qd"], d_o, _B_TN)
            d_state = ds_scr[...]
            d_vnew, d_kd, d_el = [None] * group, [None] * group, [None] * group
            for g in reversed(range(group)):
                sl = slice(g * H, (g + 1) * H)
                d_vnew[g] = d_vnew_local[sl] + _bdot(c["kd"][sl], d_state, _B_NN)
                d_kd[g] = _bdot(v_new[sl], d_state, _B_NT)
                d_el[g] = jnp.sum(jnp.sum(d_state * state[sl], axis=2, keepdims=True), axis=1, keepdims=True)
                d_state = d_state * c["el"][sl] + d_state_local[sl] - _bdot(c["w"][sl], d_vnew[g], _B_TN)
            ds_scr[...] = d_state
            chunk_grads(c, n0, state, d_o, v_new, jnp.concatenate(d_vnew, axis=0), jnp.concatenate(d_kd, axis=0),
                        jnp.concatenate(d_el, axis=0))
            return carry

        def chunk_grads(c, n0, state, d_o, v_new, d_vnew, d_kd, d_el):
            ii, jj = c["ii"], c["jj"]
            q, k, v, kb, beta = c["q"], c["k"], c["v"], c["kb"], c["beta"]
            decay, eg, egl, el = c["decay"], c["eg"], c["egl"], c["el"]
            u, w, tinv = c["u"], c["w"], c["tinv"]
            d_qd = _bdot(d_o, state, _B_NT)
            d_attn = _bdot(d_o, v_new, _B_NT)
            d_w = -_bdot(d_vnew, state, _B_NT)
            d_rv = _hdot(tinv, d_vnew, _B_TN)
            d_rw = _hdot(tinv, d_w, _B_TN)
            d_amat = -(_bdot(d_rv, u, _B_NT) + _bdot(d_rw, w, _B_NT))
            d_low = jnp.where(ii > jj, d_amat, 0.0)
            d_p = d_low * decay
            d_qk = d_attn * decay
            e_mat = (d_low * c["pmat"] + d_attn * c["qk"]) * decay
            d_q = _bdot(d_qk, k, _B_NN) + d_qd * eg
            d_kb = _bdot(d_p, k, _B_NN) + d_rw * eg
            d_k = _bdot(d_qk, q, _B_TN) + _bdot(d_p, kb, _B_TN) + d_kd * egl + d_kb * beta
            d_beta = jnp.sum(d_kb * k, axis=2, keepdims=True) + jnp.sum(d_rv * v, axis=2, keepdims=True)
            d_v = d_rv * beta
            d_eg = jnp.sum(d_qd * q, axis=2, keepdims=True) + jnp.sum(d_rw * kb, axis=2, keepdims=True)
            d_egl = jnp.sum(d_kd * k, axis=2, keepdims=True)
            d_glast = jnp.sum(d_egl * egl, axis=1, keepdims=True) + d_el * el
            row_sum = jnp.sum(e_mat, axis=2, keepdims=True)
            col_sum = c["to_col"](jnp.sum(e_mat, axis=1, keepdims=True))
            d_gc = row_sum - col_sum + d_eg * eg - d_egl * egl
            d_g_row = jnp.sum(jnp.where(ii >= jj, jnp.broadcast_to(d_gc, (B, C, C)), 0.0), axis=1, keepdims=True) + d_glast
            beta_row = c["beta_row"]
            d_b_row = c["to_row"](d_beta) * beta_row * (1.0 - beta_row)
            d_a_row = d_g_row * c["neg_ea"] * _sigmoid(c["a_row"] + c["dt_b"])
            _dn_store(dqkv_ref, 0, c["r0"], group, d_q * QK_SCALE)
            _dn_store(dqkv_ref, 1, c["r0"], group, d_k)
            _dn_store(dqkv_ref, 2, c["r0"], group, d_v)
            d_b_row = d_b_row.reshape(group, H, 1, C)
            d_a_row = d_a_row.reshape(group, H, 1, C)
            d_a_log = jnp.sum((d_g_row * c["g_row"]).reshape(group, H, 1, C), axis=0)
            d_dt_b = jnp.sum(d_a_row, axis=0)
            for h in range(H):
                db_ref[h, pl.ds(n0, group)] = d_b_row[:, h]
                da_ref[h, pl.ds(n0, group)] = d_a_row[:, h]
                acc_scr[h, 0:1, 0:C] += d_a_log[h]
                acc_scr[h, 1:2, 0:C] += d_dt_b[h]

        lax.fori_loop(0, nc // group, step, 0)
        for h in range(H):
            tot = jnp.sum(acc_scr[h], axis=1, keepdims=True)
            dhp_ref[h] = jnp.broadcast_to(tot, (8, 128))

    return pl.pallas_call(
        body, name=name, grid=(HEADS // H,),
        in_specs=[qkv_spec, row_spec, row_spec, hp_spec, st_spec, _dn_inverse_spec(T),
                  pl.BlockSpec((T, H * HEAD_DIM), lambda h: (0, h))],
        out_specs=(qkv_spec, row_spec, row_spec, hp_spec),
        out_shape=(jax.ShapeDtypeStruct((3, T, D_MODEL), F32), jax.ShapeDtypeStruct((HEADS, nc, 1, C), F32),
                   jax.ShapeDtypeStruct((HEADS, nc, 1, C), F32), jax.ShapeDtypeStruct((HEADS, 8, 128), F32)),
        scratch_shapes=[pltpu.VMEM((H, HEAD_DIM, HEAD_DIM), F32), pltpu.VMEM((H, 8, 128), F32)],
        compiler_params=_params(("parallel",)),
    )(qkv, b_rows, a_rows, hp, states, inverses, do)


COL_Z = 3 * HEADS


def _gated_norm_fwd(o, proj, gain, *, name):
    T = o.shape[0]

    def body(o_ref, z_ref, g_ref, out_ref):
        x = o_ref[...]
        r = lax.rsqrt(jnp.mean(x * x, axis=-1, keepdims=True) + RMS_EPS)
        z = z_ref[...]
        out_ref[...] = (x * r * g_ref[...] * (z * _sigmoid(z))).astype(out_ref.dtype)

    return pl.pallas_call(
        body, name=name, grid=(HEADS,),
        in_specs=[pl.BlockSpec((T, HEAD_DIM), lambda h: (0, h)), pl.BlockSpec((T, HEAD_DIM), lambda h: (0, COL_Z + h)),
                  pl.BlockSpec((1, HEAD_DIM), lambda h: (0, 0))],
        out_specs=pl.BlockSpec((T, HEAD_DIM), lambda h: (0, h)),
        out_shape=jax.ShapeDtypeStruct((T, D_MODEL), BF16), compiler_params=_params(("parallel",)),
    )(o, proj, gain)


def _gated_norm_bwd(dout, o, proj, gain, *, name):
    T = o.shape[0]

    def body(d_ref, o_ref, z_ref, g_ref, do_ref, dz_ref, dg_ref):
        x = o_ref[...]
        r = lax.rsqrt(jnp.mean(x * x, axis=-1, keepdims=True) + RMS_EPS)
        n = x * r
        z = z_ref[...]
        sg = _sigmoid(z)
        d = d_ref[...].astype(F32)
        g = g_ref[...]
        dz_ref[...] = (d * n * g * (sg * (1.0 + z * (1.0 - sg)))).astype(dz_ref.dtype)
        dy = d * (z * sg)
        dyg = dy * g
        do_ref[...] = r * (dyg - n * jnp.mean(dyg * n, axis=-1, keepdims=True))

        @pl.when(pl.program_id(0) == 0)
        def _():
            dg_ref[...] = jnp.zeros_like(dg_ref)

        dg_ref[...] += jnp.sum(dy * n, axis=0, keepdims=True)

    head = pl.BlockSpec((T, HEAD_DIM), lambda h: (0, h))
    vec = pl.BlockSpec((1, HEAD_DIM), lambda h: (0, 0))
    return pl.pallas_call(
        body, name=name, grid=(HEADS,),
        in_specs=[head, head, pl.BlockSpec((T, HEAD_DIM), lambda h: (0, COL_Z + h)), vec],
        out_specs=(head, head, vec),
        out_shape=(jax.ShapeDtypeStruct((T, D_MODEL), F32), jax.ShapeDtypeStruct((T, D_MODEL), BF16),
                   jax.ShapeDtypeStruct((1, HEAD_DIM), F32)),
        compiler_params=_params(("arbitrary",)),
    )(dout, o, proj, gain)


COL_SBQ = 4 * HEADS
COL_SBK = 5 * HEADS
COL_SBV = 6 * HEADS


def _split_dot(x, mat):
    lead = x.shape[:-1]
    x = x.reshape(-1, x.shape[-1])
    hi = x.astype(BF16)
    lo = (x - hi.astype(F32)).astype(BF16)
    out = jnp.dot(hi, mat, preferred_element_type=F32) + jnp.dot(lo, mat, preferred_element_type=F32)
    return out.reshape(lead + (mat.shape[-1],))


def _sb_specs(T, heads):
    col = lambda first: pl.BlockSpec((T, heads * HEAD_DIM), lambda h: (0, first // heads + h))
    return col(COL_SBQ), col(COL_SBK), col(COL_SBV), pl.BlockSpec((1, HEAD_DIM), lambda h: (0, 0))


def _heads_first(x):
    return jnp.stack([x[:, c:c + HEAD_DIM] for c in range(0, x.shape[1], HEAD_DIM)], axis=0)


def _heads_last(x):
    return jnp.concatenate([x[h] for h in range(x.shape[0])], axis=1)


def _head_rms(x):
    r = lax.rsqrt(jnp.mean(x * x, axis=-1, keepdims=True) + RMS_EPS)
    return x * r, r


def _sb_fwd(proj, q_gain, k_gain, *, name):
    T = proj.shape[0]
    B = SB_BLOCK
    H = SB_HEADS
    nb = T // B
    KT = min(SB_KEY_TILE, T)
    NS = KT // B
    q_spec, k_spec, v_spec, g_spec = _sb_specs(T, H)

    def body(q_ref, k_ref, v_ref, gq_ref, gk_ref, o_ref, lt_ref, qs, ks, vs):
        qs[...] = (_head_rms(_heads_first(q_ref[...]))[0] * (gq_ref[...] * QK_SCALE)).astype(BF16)
        ks[...] = (_head_rms(_heads_first(k_ref[...]))[0] * gk_ref[...]).astype(BF16)
        vs[...] = _heads_first(v_ref[...]).astype(BF16)
        ii = lax.broadcasted_iota(jnp.int32, (B, B), 0)
        jj = lax.broadcasted_iota(jnp.int32, (B, B), 1)
        after = jnp.where(ii > jj, 1.0, 0.0).astype(BF16)
        ahead = lax.broadcasted_iota(jnp.int32, (H, B, KT), 2) - lax.broadcasted_iota(jnp.int32, (H, B, KT), 1)

        def q_block(i, carry):
            rows = pl.ds(pl.multiple_of(i * B, B), B)
            q = qs[:, rows, :]

            def tile(c0, acc, tail, masked):
                cols = pl.ds(c0, KT)
                z = lax.dot_general(q, ks[:, cols, :], _B_NT, preferred_element_type=F32)
                sp = _softplus(z)
                causal = ahead < (i * B - c0)
                log_1mb = jnp.where(causal, -sp, 0.0) if masked else -sp
                parts = [None] * NS
                for b in reversed(range(NS)):
                    blk = log_1mb[:, :, b * B:(b + 1) * B]
                    parts[b] = _split_dot(blk, after) + tail
                    tail = tail + jnp.sum(blk, axis=2, keepdims=True)
                survive = parts[0] if NS == 1 else jnp.concatenate(parts, axis=2)
                wts = jnp.exp(z - sp + survive)
                if masked:
                    wts = jnp.where(causal, wts, 0.0)
                acc = acc + lax.dot_general(wts.astype(BF16), vs[:, cols, :], _B_NN, preferred_element_type=F32)
                return acc, tail

            last = i // NS
            acc, tail = tile(pl.multiple_of(last * KT, KT), jnp.zeros((H, B, HEAD_DIM), F32), jnp.zeros((H, B, 1), F32), True)
            acc, tail = lax.fori_loop(
                1, last + 1, lambda s, c: tile(pl.multiple_of((last - s) * KT, KT), c[0], c[1], False), (acc, tail))
            o_ref[rows, :] = _heads_last(acc).astype(o_ref.dtype)
            lt_ref[rows, :] = _heads_last(jnp.broadcast_to(tail, (H, B, HEAD_DIM)))
            return carry

        lax.fori_loop(0, nb, q_block, 0)

    heads = pl.BlockSpec((T, H * HEAD_DIM), lambda h: (0, h))
    return pl.pallas_call(
        body, name=name, grid=(HEADS // H,), in_specs=[q_spec, k_spec, v_spec, g_spec, g_spec],
        out_specs=(heads, heads),
        out_shape=(jax.ShapeDtypeStruct((T, D_MODEL), BF16), jax.ShapeDtypeStruct((T, D_MODEL), F32)),
        scratch_shapes=[pltpu.VMEM((H, T, HEAD_DIM), BF16)] * 3, compiler_params=_params(("parallel",)),
    )(proj, proj, proj, q_gain, k_gain)


def _sb_bwd(proj, q_gain, k_gain, ltot, do, *, name):
    T = proj.shape[0]
    B = SB_BLOCK
    H = SB_HEADS_BWD
    nb = T // B
    KT = min(SB_KEY_TILE_BWD, T)
    NS = KT // B
    q_spec, k_spec, v_spec, g_spec = _sb_specs(T, H)

    def body(q_ref, k_ref, v_ref, gq_ref, gk_ref, lt_ref, do_ref, dq_ref, dk_ref, dv_ref, dgq_ref, dgk_ref,
             qs, ks, vs, dos, lts, dq_acc, dk_acc, dv_acc):
        qn, q_r = _head_rms(_heads_first(q_ref[...]))
        kn, k_r = _head_rms(_heads_first(k_ref[...]))
        qs[...] = (qn * (gq_ref[...] * QK_SCALE)).astype(BF16)
        ks[...] = (kn * gk_ref[...]).astype(BF16)
        vs[...] = _heads_first(v_ref[...]).astype(BF16)
        dos[...] = _heads_first(do_ref[...]).astype(BF16)
        lts[...] = _heads_first(lt_ref[...])
        dk_acc[...] = jnp.zeros_like(dk_acc)
        dv_acc[...] = jnp.zeros_like(dv_acc)
        ii = lax.broadcasted_iota(jnp.int32, (B, B), 0)
        jj = lax.broadcasted_iota(jnp.int32, (B, B), 1)
        upto = jnp.where(ii <= jj, 1.0, 0.0).astype(BF16)
        before = jnp.where(ii < jj, 1.0, 0.0).astype(BF16)
        ahead = lax.broadcasted_iota(jnp.int32, (H, B, KT), 2) - lax.broadcasted_iota(jnp.int32, (H, B, KT), 1)

        def q_block(i, carry):
            rows = pl.ds(pl.multiple_of(i * B, B), B)
            q = qs[:, rows, :]
            d_o = dos[:, rows, :]
            total = jnp.max(lts[:, rows, :], axis=2, keepdims=True)

            def tile(c0, dq, head_lb, head_de, masked):
                cols = pl.ds(c0, KT)
                k = ks[:, cols, :]
                v = vs[:, cols, :]
                z = lax.dot_general(q, k, _B_NT, preferred_element_type=F32)
                sp = _softplus(z)
                causal = ahead < (i * B - c0)
                log_1mb = jnp.where(causal, -sp, 0.0) if masked else -sp
                parts = [None] * NS
                for b in range(NS):
                    blk = log_1mb[:, :, b * B:(b + 1) * B]
                    parts[b] = _split_dot(blk, upto) + head_lb
                    head_lb = head_lb + jnp.sum(blk, axis=2, keepdims=True)
                prefix = parts[0] if NS == 1 else jnp.concatenate(parts, axis=2)
                wts = jnp.exp(z - sp + (total - prefix))
                if masked:
                    wts = jnp.where(causal, wts, 0.0)
                d_w = lax.dot_general(d_o, v, _B_NT, preferred_element_type=F32)
                d_e = wts * d_w
                d_eb = d_e.astype(BF16)
                for b in range(NS):
                    inside = jnp.dot(d_eb[:, :, b * B:(b + 1) * B].reshape(H * B, B), before, preferred_element_type=F32)
                    parts[b] = inside.reshape(H, B, B) + head_de
                    head_de = head_de + jnp.sum(d_e[:, :, b * B:(b + 1) * B], axis=2, keepdims=True)
                cum = parts[0] if NS == 1 else jnp.concatenate(parts, axis=2)
                sig = jnp.exp(z - sp)
                d_z = d_e - sig * (d_e + cum)
                if masked:
                    d_z = jnp.where(causal, d_z, 0.0)
                d_zb = d_z.astype(BF16)
                dq = dq + lax.dot_general(d_zb, k, _B_NN, preferred_element_type=F32)
                dk_acc[:, cols, :] += lax.dot_general(d_zb, q, _B_TN, preferred_element_type=F32)
                dv_acc[:, cols, :] += lax.dot_general(wts.astype(BF16), d_o, _B_TN, preferred_element_type=F32)
                return dq, head_lb, head_de

            last = i // NS
            zero = jnp.zeros((H, B, 1), F32)
            state = lax.fori_loop(0, last, lambda t, c: tile(pl.multiple_of(t * KT, KT), *c, False),
                                  (jnp.zeros((H, B, HEAD_DIM), F32), zero, zero))
            dq, _, _ = tile(pl.multiple_of(last * KT, KT), *state, True)
            dq_acc[:, rows, :] = dq * QK_SCALE
            return carry

        lax.fori_loop(0, nb, q_block, 0)

        def norm_bwd(d_scaled, n, r, gain):
            dn = d_scaled * gain
            d_gain = jnp.sum(jnp.sum(d_scaled * n, axis=1, keepdims=True), axis=0)
            return r * (dn - n * jnp.mean(dn * n, axis=-1, keepdims=True)), d_gain

        dq_raw, dgq = norm_bwd(dq_acc[...], qn, q_r, gq_ref[...])
        dk_raw, dgk = norm_bwd(dk_acc[...], kn, k_r, gk_ref[...])
        dq_ref[...] = _heads_last(dq_raw).astype(dq_ref.dtype)
        dk_ref[...] = _heads_last(dk_raw).astype(dk_ref.dtype)
        dv_ref[...] = _heads_last(dv_acc[...]).astype(dv_ref.dtype)

        @pl.when(pl.program_id(0) == 0)
        def _():
            dgq_ref[...] = jnp.zeros_like(dgq_ref)
            dgk_ref[...] = jnp.zeros_like(dgk_ref)

        dgq_ref[...] += dgq
        dgk_ref[...] += dgk

    heads = pl.BlockSpec((T, H * HEAD_DIM), lambda h: (0, h))
    out = jax.ShapeDtypeStruct((T, D_MODEL), BF16)
    vec = jax.ShapeDtypeStruct((1, HEAD_DIM), F32)
    return pl.pallas_call(
        body, name=name, grid=(HEADS // H,), in_specs=[q_spec, k_spec, v_spec, g_spec, g_spec, heads, heads],
        out_specs=(heads, heads, heads, g_spec, g_spec), out_shape=(out, out, out, vec, vec),
        scratch_shapes=[pltpu.VMEM((H, T, HEAD_DIM), BF16)] * 4 + [pltpu.VMEM((H, T, HEAD_DIM), F32)] * 4,
        compiler_params=_params(("arbitrary",)),
    )(proj, proj, proj, q_gain, k_gain, ltot, do)


ADAM_ROWS = 256


def _adamw(g_parts, w, m, v, *, name, layer=0, earlier=None):
    K, A, C = g_parts.shape
    R = w.shape[0]
    tr = next((t for t in (ADAM_ROWS, ADAM_ROWS // 2) if A % t == 0), A // 2 if A % 32 == 0 else A)
    first_block = layer * (A // tr)

    def body(g_ref, w_ref, m_ref, v_ref, *rest):
        go_ref, d_ref, mo_ref, vo_ref = rest[-4:]
        g = g_ref[0].astype(F32)
        for k in range(1, K):
            g = g + g_ref[k].astype(F32)
        go_ref[...] = g
        m_new = ADAM_B1 * m_ref[...] + (1.0 - ADAM_B1) * g
        v_new = ADAM_B2 * v_ref[...] + (1.0 - ADAM_B2) * (g * g)
        m_hat = m_new / (1.0 - ADAM_B1 ** ADAM_STEP)
        v_hat = v_new / (1.0 - ADAM_B2 ** ADAM_STEP)
        d_ref[...] = -ADAM_LR * (m_hat / (jnp.sqrt(v_hat) + ADAM_EPS) + ADAM_WD * w_ref[...])
        mo_ref[...] = m_new
        vo_ref[...] = v_new

    row = pl.BlockSpec((tr, C), lambda i: (first_block + i, 0))
    out = jax.ShapeDtypeStruct((R, C), F32)
    in_specs = [pl.BlockSpec((K, tr, C), lambda i: (0, i, 0)), row, row, row]
    if earlier is None:
        return pl.pallas_call(
            body, name=name, grid=(A // tr,), in_specs=in_specs, out_specs=(row, row, row, row),
            out_shape=(out, out, out, out), compiler_params=_params(("parallel",)),
        )(g_parts, w, m, v)
    return pl.pallas_call(
        body, name=name, grid=(A // tr,), in_specs=in_specs + [ANY] * 4, out_specs=(row, row, row, row),
        out_shape=(out, out, out, out), input_output_aliases={4 + j: j for j in range(4)},
        compiler_params=_params(("parallel",)),
    )(g_parts, w, m, v, *earlier)


def _sum_parts(parts, *, name):
    K, R, C = parts.shape

    def body(p_ref, o_ref):
        acc = p_ref[0]
        for k in range(1, K):
            acc = acc + p_ref[k]
        o_ref[...] = acc

    return pl.pallas_call(body, name=name, out_shape=jax.ShapeDtypeStruct((R, C), F32))(parts)


def _position():
    return lax.axis_index("x"), lax.axis_index("y"), lax.axis_index("c")


def _all_gather(shards, *, name):
    n = len(shards)

    def body(*refs):
        x_refs, out_refs = refs[:n], refs[n:2 * n]
        send_sems, recv_sems, local_sems = refs[2 * n:]
        x, y, c = _position()
        me, sibling = (x, y, c), (x, y, 1 - c)
        chips = [(1 - x, y), (x, 1 - y), (1 - x, 1 - y)]

        def slot(a, px, py, pc):
            return out_refs[a].at[4 * px + 2 * py + pc]

        def copy(a, k, block, to, own=False):
            return pltpu.make_async_remote_copy(
                src_ref=x_refs[a] if own else slot(a, *block), dst_ref=slot(a, *block),
                send_sem=send_sems.at[a, k], recv_sem=recv_sems.at[a, k], device_id=to, device_id_type=MESH)

        mine = [pltpu.make_async_copy(x_refs[a], slot(a, *me), local_sems.at[a]) for a in range(n)]
        for cp in mine:
            cp.start()
        first = [copy(a, 1 + j, me, (*chip, c), own=True) for j, chip in enumerate(chips) for a in range(n)]
        first += [copy(a, 0, me, sibling, own=True) for a in range(n)]
        for cp in first:
            cp.start()
        passed = []
        for j, chip in enumerate(chips):
            for a in range(n):
                copy(a, 1 + j, (*chip, c), me).wait_recv()
                passed.append(copy(a, 4 + j, (*chip, c), sibling))
                passed[-1].start()
        for a in range(n):
            copy(a, 0, sibling, me).wait_recv()
        for j, chip in enumerate(chips):
            for a in range(n):
                copy(a, 4 + j, (*chip, 1 - c), me).wait_recv()
        for cp in first + passed:
            cp.wait_send()
        for cp in mine:
            cp.wait()

    return pl.pallas_call(
        body, name=name, in_specs=[ANY] * n, out_specs=[ANY] * n,
        out_shape=[jax.ShapeDtypeStruct((N_DEV,) + s.shape, s.dtype) for s in shards],
        scratch_shapes=[pltpu.SemaphoreType.DMA((n, 7)), pltpu.SemaphoreType.DMA((n, 7)), pltpu.SemaphoreType.DMA((n,))],
    )(*shards)


HBM = pl.BlockSpec(memory_space=pltpu.HBM)
SEM = pl.BlockSpec(memory_space=pltpu.SEMAPHORE)
DATAFLOW = pltpu.SideEffectType.DATAFLOW_SIDE_EFFECTING


def _exchange_copies(gather, x_refs, land_refs, send_sems, recv_sems, local_sems):
    n = len(x_refs)
    x, y, c = _position()
    me = 4 * x + 2 * y + c

    def src(a, slot):
        return x_refs[a] if gather else x_refs[a].at[slot]

    mine = [pltpu.make_async_copy(src(a, me), land_refs[a].at[me], local_sems.at[a]) for a in range(n)]
    sends, recvs = [], []
    for k in range(1, N_DEV):
        px, py, pc = (x + (k >> 2)) % 2, (y + ((k >> 1) & 1)) % 2, (c + (k & 1)) % 2
        peer = 4 * px + 2 * py + pc
        for a in range(n):
            sems = dict(send_sem=send_sems.at[7 * a + k - 1], recv_sem=recv_sems.at[7 * a + k - 1],
                        device_id=(px, py, pc), device_id_type=MESH)
            sends.append(pltpu.make_async_remote_copy(src_ref=src(a, peer), dst_ref=land_refs[a].at[me], **sems))
            recvs.append(pltpu.make_async_remote_copy(src_ref=src(a, me), dst_ref=land_refs[a].at[peer], **sems))
    return mine, sends, recvs


def _exchange_start(parts, *, gather, name):
    n = len(parts)

    def body(*refs):
        x_refs, land_refs = refs[:n], refs[n:2 * n]
        send_sems, recv_sems, local_sems = refs[2 * n:2 * n + 3]
        token = refs[-1]
        mine, sends, _ = _exchange_copies(gather, x_refs, land_refs, send_sems, recv_sems, local_sems)
        for cp in mine + sends:
            cp.start()
        token[...] = jnp.zeros_like(token)

    sems = (pltpu.SemaphoreType.DMA((7 * n,)), pltpu.SemaphoreType.DMA((7 * n,)), pltpu.SemaphoreType.DMA((n,)))
    thru = tuple(pltpu.HBM(p.shape, p.dtype) for p in parts)
    land = tuple(pltpu.HBM(((N_DEV,) if gather else ()) + p.shape, p.dtype) for p in parts)
    res = pl.pallas_call(
        body, name=name, in_specs=[HBM] * (2 * n),
        out_specs=(SEM, SEM, SEM) + (HBM,) * (2 * n) + (pl.BlockSpec(memory_space=pltpu.VMEM),),
        out_shape=sems + thru + land + (jax.ShapeDtypeStruct((8, 128), F32),),
        input_output_aliases={a: 3 + a for a in range(2 * n)},
        compiler_params=pltpu.CompilerParams(has_side_effects=DATAFLOW),
    )(*[pltpu.with_memory_space_constraint(p, pltpu.HBM) for p in parts],
      *[pltpu.with_memory_space_constraint(lax.empty(z.shape, z.dtype), pltpu.HBM) for z in land])
    return res[:3], res[3:3 + n], res[3 + n:3 + 2 * n], res[-1]


def _exchange_wait(sems, parts, landing, after, *, gather, name):
    n = len(parts)
    after = list(after)

    def body(*refs):
        x_refs, land_refs = refs[:n], refs[n:2 * n]
        send_sems, recv_sems, local_sems = refs[2 * n:2 * n + 3]
        token = refs[-1]
        mine, sends, recvs = _exchange_copies(gather, x_refs, land_refs, send_sems, recv_sems, local_sems)
        for cp in recvs:
            cp.wait_recv()
        for cp in sends:
            cp.wait_send()
        for cp in mine:
            cp.wait()
        token[...] = jnp.zeros_like(token)

    thru = tuple(pltpu.HBM(p.shape, p.dtype) for p in tuple(parts) + tuple(landing))
    res = pl.pallas_call(
        body, name=name, in_specs=[HBM] * (2 * n) + [SEM, SEM, SEM] + [ANY] * len(after),
        out_specs=(HBM,) * (2 * n) + (pl.BlockSpec(memory_space=pltpu.VMEM),),
        out_shape=thru + (jax.ShapeDtypeStruct((8, 128), F32),), input_output_aliases={a: a for a in range(2 * n)},
        compiler_params=pltpu.CompilerParams(has_side_effects=DATAFLOW),
    )(*parts, *landing, *sems, *after)
    return res[n:2 * n], res[-1]


def _relay_copies(x_refs, land_refs, first_send, first_recv, relay_send, relay_recv, local_sems):
    n = len(x_refs)
    x, y, c = _position()
    sibling = (x, y, 1 - c)
    chips = [(1 - x, y), (x, 1 - y), (1 - x, 1 - y)]

    def slot(a, px, py, pc):
        return land_refs[a].at[4 * px + 2 * py + pc]

    def hop(a, k, block, to, own=False):
        return pltpu.make_async_remote_copy(
            src_ref=x_refs[a] if own else slot(a, *block), dst_ref=slot(a, *block),
            send_sem=first_send.at[4 * a + k], recv_sem=first_recv.at[4 * a + k], device_id=to, device_id_type=MESH)

    def relay(a, j, block, to):
        return pltpu.make_async_remote_copy(
            src_ref=slot(a, *block), dst_ref=slot(a, *block),
            send_sem=relay_send.at[3 * a + j], recv_sem=relay_recv.at[3 * a + j], device_id=to, device_id_type=MESH)

    me = (x, y, c)
    mine = [pltpu.make_async_copy(x_refs[a], slot(a, *me), local_sems.at[a]) for a in range(n)]
    sends = [hop(a, 1 + j, me, (*chip, c), own=True) for j, chip in enumerate(chips) for a in range(n)]
    sends += [hop(a, 0, me, sibling, own=True) for a in range(n)]
    over_ici = [hop(a, 1 + j, (*chip, c), me) for j, chip in enumerate(chips) for a in range(n)]
    from_sibling = [hop(a, 0, sibling, me) for a in range(n)]
    if relay_send is None:
        return mine, sends, over_ici, from_sibling, [], []
    relays = [relay(a, j, (*chip, c), sibling) for j, chip in enumerate(chips) for a in range(n)]
    relayed = [relay(a, j, (*chip, 1 - c), me) for j, chip in enumerate(chips) for a in range(n)]
    return mine, sends, over_ici, from_sibling, relays, relayed


def _relay_call(body, n_sem_in, n_sem_out, shards, landing, sems_in, after, name):
    n = len(shards)
    thru = tuple(pltpu.HBM(p.shape, p.dtype) for p in tuple(shards) + tuple(landing))
    res = pl.pallas_call(
        body, name=name, in_specs=[HBM] * (2 * n) + [SEM] * n_sem_in + [ANY] * len(after),
        out_specs=(SEM,) * len(n_sem_out) + (HBM,) * (2 * n) + (pl.BlockSpec(memory_space=pltpu.VMEM),),
        out_shape=tuple(pltpu.SemaphoreType.DMA((k,)) for k in n_sem_out) + thru + (jax.ShapeDtypeStruct((8, 128), F32),),
        input_output_aliases={a: len(n_sem_out) + a for a in range(2 * n)},
        compiler_params=pltpu.CompilerParams(has_side_effects=DATAFLOW),
    )(*shards, *landing, *sems_in, *after)
    k = len(n_sem_out)
    return res[:k], res[k:k + n], res[k + n:k + 2 * n], res[-1]


def _relay_gather_start(shards, *, name):
    n = len(shards)

    def body(*refs):
        x_refs, land_refs = refs[:n], refs[n:2 * n]
        first_send, first_recv, local_sems = refs[2 * n:2 * n + 3]
        mine, sends, *_ = _relay_copies(x_refs, land_refs, first_send, first_recv, None, None, local_sems)
        for cp in mine + sends:
            cp.start()
        refs[-1][...] = jnp.zeros_like(refs[-1])

    landing = [pltpu.with_memory_space_constraint(lax.empty((N_DEV,) + s.shape, s.dtype), pltpu.HBM) for s in shards]
    shards = [pltpu.with_memory_space_constraint(s, pltpu.HBM) for s in shards]
    return _relay_call(body, 0, (4 * n, 4 * n, n), shards, landing, (), (), name)


def _relay_gather_pass_on(first, shards, landing, after, *, name):
    n = len(shards)

    def body(*refs):
        x_refs, land_refs = refs[:n], refs[n:2 * n]
        first_send, first_recv, local_sems = refs[2 * n:2 * n + 3]
        relay_send, relay_recv = refs[2 * n + 3 + len(after):2 * n + 5 + len(after)]
        _, _, over_ici, _, relays, _ = _relay_copies(x_refs, land_refs, first_send, first_recv, relay_send, relay_recv,
                                                   local_sems)
        for arrival, cp in zip(over_ici, relays):
            arrival.wait_recv()
            cp.start()
        refs[-1][...] = jnp.zeros_like(refs[-1])

    return _relay_call(body, 3, (3 * n, 3 * n), shards, landing, first, list(after), name)


def _relay_gather_wait(first, relay, shards, landing, after, *, name):
    n = len(shards)

    def body(*refs):
        x_refs, land_refs = refs[:n], refs[n:2 * n]
        first_send, first_recv, local_sems, relay_send, relay_recv = refs[2 * n:2 * n + 5]
        mine, sends, _, from_sibling, relays, relayed = _relay_copies(
            x_refs, land_refs, first_send, first_recv, relay_send, relay_recv, local_sems)
        for cp in from_sibling + relayed:
            cp.wait_recv()
        for cp in sends + relays:
            cp.wait_send()
        for cp in mine:
            cp.wait()
        refs[-1][...] = jnp.zeros_like(refs[-1])

    _, _, landing, token = _relay_call(body, 5, (), shards, landing, tuple(first) + tuple(relay), list(after), name)
    return landing, token


def _ffn_fwd(x, gain, wg_in, wg_out, tag):
    T, D = x.shape
    fb, rb = wg_in.shape[-1], wg_out.shape[-2]
    tm, tn = min(T, 1024), 512
    h = _rmsnorm_fwd(x, gain, name=f"{tag}_norm")
    p = _mm(name=f"{tag}_in", grid=(T // tm, N_DEV, 1), tile=(tm, fb),
            a=h, a_spec=pl.BlockSpec((tm, D), lambda i, j, k: (i, 0)),
            b=wg_in, b_spec=pl.BlockSpec((None, D, fb), lambda i, j, k: (j, 0, 0)),
            out_shape=jax.ShapeDtypeStruct((N_DEV, T, fb), BF16), o_spec=pl.BlockSpec((None, tm, fb), lambda i, j, k: (j, i, 0)))
    a = _swiglu_fwd(p, name=f"{tag}_act")
    y = _mm(name=f"{tag}_out", grid=(T // tm, 1, FF_HALF), tile=(tm, D), resid=x, scale=0.5,
            a=a, a_spec=pl.BlockSpec((None, tm, fb), lambda i, j, k: (k, i, 0)),
            b=wg_out.reshape(N_DEV * rb, D), b_spec=pl.BlockSpec((fb, D), lambda i, j, k: (k, 0)),
            out_shape=jax.ShapeDtypeStruct((T, D), F32), o_spec=pl.BlockSpec((tm, D), lambda i, j, k: (i, 0)))
    return y, (x, h, p, a)


def _ffn_bwd(dy, saved, gain, wg_in, wg_out, tag, on_weight_grads=None):
    x, h, p, a = saved
    T, D = x.shape
    fb, rb = wg_in.shape[-1], wg_out.shape[-2]
    tm, tn = min(T, 1024), 512
    da = _mm(name=f"{tag}_out_dx", grid=(T // tm, FF_HALF, 1), tile=(tm, fb), tb=True, scale=0.5,
             a=dy, a_spec=pl.BlockSpec((tm, D), lambda i, j, k: (i, 0)),
             b=wg_out.reshape(N_DEV * rb, D), b_spec=pl.BlockSpec((fb, D), lambda i, j, k: (j, 0)),
             out_shape=jax.ShapeDtypeStruct((FF_HALF, T, fb), BF16), o_spec=pl.BlockSpec((None, tm, fb), lambda i, j, k: (j, i, 0)))
    d_w_out = _mm(name=f"{tag}_out_dw", grid=(FF_HALF, D // tn, 1), tile=(fb, tn), ta=True, scale=0.5,
                  a=a, a_spec=pl.BlockSpec((None, T, fb), lambda i, j, k: (i, 0, 0)),
                  b=dy, b_spec=pl.BlockSpec((T, tn), lambda i, j, k: (0, j)),
                  out_shape=jax.ShapeDtypeStruct((FF_HALF, fb, D), BF16), o_spec=pl.BlockSpec((None, fb, tn), lambda i, j, k: (i, 0, j)))
    dp = _swiglu_bwd(da, p, name=f"{tag}_act_bwd")
    d_w_in = _mm(name=f"{tag}_in_dw", grid=(1, N_DEV, 1), tile=(D, fb), ta=True,
                 a=h, a_spec=pl.BlockSpec((T, D), lambda i, j, k: (0, 0)),
                 b=dp, b_spec=pl.BlockSpec((None, T, fb), lambda i, j, k: (j, 0, 0)),
                 out_shape=jax.ShapeDtypeStruct((N_DEV, D, fb), BF16), o_spec=pl.BlockSpec((None, D, fb), lambda i, j, k: (j, 0, 0)))
    dh = _mm(name=f"{tag}_in_dx", grid=(T // tm, 1, N_DEV), tile=(tm, D), tb=True,
             a=dp, a_spec=pl.BlockSpec((None, tm, fb), lambda i, j, k: (k, i, 0)),
             b=wg_in, b_spec=pl.BlockSpec((None, D, fb), lambda i, j, k: (k, 0, 0)),
             out_shape=jax.ShapeDtypeStruct((T, D), F32), o_spec=pl.BlockSpec((tm, D), lambda i, j, k: (i, 0)))
    d_w_out = d_w_out.reshape(N_DEV, rb, D)
    if on_weight_grads is not None:
        gain = gain + on_weight_grads(d_w_in, d_w_out)[0, 0]
    dx, d_gain = _rmsnorm_bwd(dh, x, gain, dy, name=f"{tag}_norm_bwd")
    return dx, d_gain, d_w_in, d_w_out


def _square_mm(a, wg, *, name, transposed=False, out_dtype=F32, resid=None):
    T, D = a.shape
    w = wg.reshape(D, D)
    return _matmul(a, w, tb=transposed, name=name, out_dtype=out_dtype, resid=resid)


def _head_rows(cols, T):
    return cols.T.reshape(HEADS, T // DN_CHUNK, 1, DN_CHUNK)


def _mixer_fwd(x, w, big, tag):
    T = x.shape[0]
    h = _rmsnorm_fwd(x, w["mix_norm"], name=f"{tag}_norm")
    proj = _matmul(h, big["w_main"], name=f"{tag}_proj")
    scal = _matmul(h, big["w_scal"], name=f"{tag}_proj_scal", tn=N_SCAL)
    qkv = _conv_fwd(proj, big["conv_w"], name=f"{tag}_conv")
    b_rows = _head_rows(scal[:, 0:HEADS], T)
    a_rows = _head_rows(scal[:, HEADS:2 * HEADS], T)
    o_a, *states = _dn_fwd(qkv, b_rows, a_rows, w["hp"], name=f"{tag}_dn")
    oa_n = _gated_norm_fwd(o_a, proj, w["dn_out_norm"], name=f"{tag}_dn_norm")
    ya = _square_mm(oa_n, big["w_branch_a"], name=f"{tag}_branch_a")
    o_b, ltot = _sb_fwd(proj, w["sb_q_norm"], w["sb_k_norm"], name=f"{tag}_sb")
    yb = _square_mm(o_b, big["w_branch_b"], name=f"{tag}_branch_b")
    merged = _merge_fwd(ya, yb, proj, name=f"{tag}_merge")
    y = _square_mm(merged, big["w_out"], name=f"{tag}_out", resid=x)
    return y, (x, h, proj, qkv, b_rows, a_rows, o_a, states, oa_n, ya, o_b, ltot, yb, merged)


def _mixer_bwd(dy, saved, w, big, tag, on_weight_grads):
    x, h, proj, qkv, b_rows, a_rows, o_a, states, oa_n, ya, o_b, ltot, yb, merged = saved
    T = x.shape[0]
    g = {}
    d_merged = _square_mm(dy, big["w_out"], transposed=True, name=f"{tag}_out_dx", out_dtype=BF16)
    g["w_out"] = _matmul(merged, dy, ta=True, name=f"{tag}_out_dw", out_dtype=BF16)
    d_ya, d_yb, d_ga, d_gb = _merge_bwd(d_merged, ya, yb, proj, name=f"{tag}_merge_bwd")
    d_oan = _square_mm(d_ya, big["w_branch_a"], transposed=True, name=f"{tag}_branch_a_dx")
    g["w_branch_a"] = _matmul(oa_n, d_ya, ta=True, name=f"{tag}_branch_a_dw", out_dtype=BF16)
    d_ob = _square_mm(d_yb, big["w_branch_b"], transposed=True, name=f"{tag}_branch_b_dx")
    g["w_branch_b"] = _matmul(o_b, d_yb, ta=True, name=f"{tag}_branch_b_dw", out_dtype=BF16)
    d_oa, d_z, g["dn_out_norm"] = _gated_norm_bwd(d_oan, o_a, proj, w["dn_out_norm"], name=f"{tag}_dn_norm_bwd")
    d_qkv, d_b_rows, d_a_rows, d_hp = _dn_bwd(qkv, b_rows, a_rows, w["hp"], *states, d_oa, name=f"{tag}_dn_bwd")
    g["dn_a_log"] = d_hp[:, 0, 0]
    g["dn_dt_bias"] = d_hp[:, 1, 0]
    d_conv_in, g["conv_w"] = _conv_bwd(d_qkv, proj, big["conv_w"], name=f"{tag}_conv_bwd")
    d_sbq, d_sbk, d_sbv, g["sb_q_norm"], g["sb_k_norm"] = _sb_bwd(
        proj, w["sb_q_norm"], w["sb_k_norm"], ltot, d_ob, name=f"{tag}_sb_bwd")
    d_proj = jnp.concatenate([d_conv_in, d_z, d_sbq, d_sbk, d_sbv, d_ga, d_gb], axis=1)
    d_scal = jnp.concatenate([d_b_rows.reshape(HEADS, T).T, d_a_rows.reshape(HEADS, T).T,
                              jnp.zeros((T, N_SCAL - 2 * HEADS), F32)], axis=1).astype(BF16)
    g["w_main"] = _matmul(h, d_proj, ta=True, name=f"{tag}_proj_dw", out_dtype=BF16)
    g["w_scal"] = _matmul(h, d_scal, ta=True, name=f"{tag}_proj_scal_dw", out_dtype=BF16, tn=N_SCAL)
    dh_scal = _matmul(d_scal, big["w_scal"], tb=True, name=f"{tag}_proj_scal_dx")
    dh = _matmul(d_proj, big["w_main"], tb=True, name=f"{tag}_proj_dx", tk=N_MAIN // 4, resid=dh_scal)
    gain = w["mix_norm"] + on_weight_grads(g)[0, 0]
    dx, g["mix_norm"] = _rmsnorm_bwd(dh, x, gain, dy, name=f"{tag}_norm_bwd")
    return dx, g


def _local_step(x, target, layers, weights_of, on_weight_grads):
    saved, bigs = [], []
    for l, w in enumerate(layers):
        big = weights_of(l, 0, x)
        x, s1 = _ffn_fwd(x, w["ffn1_norm"] + big["issued"], big["ffn1_w_in"], big["ffn1_w_out"], f"l{l}_ffn1")
        big.update(weights_of(l, 1, x))
        x, s2 = _mixer_fwd(x, dict(w, mix_norm=w["mix_norm"] + big["issued"]), big, f"l{l}_mix")
        big.update(weights_of(l, 2, x))
        x, s3 = _ffn_fwd(x, w["ffn2_norm"] + big["issued"], big["ffn2_w_in"], big["ffn2_w_out"], f"l{l}_ffn2")
        saved.append((s1, s2, s3))
        bigs.append(big)
    loss, dx = _loss_head(x, target, name="loss_head")
    small = [None] * len(layers)
    for l in reversed(range(len(layers))):
        w, big = layers[l], bigs[l]
        s1, s2, s3 = saved[l]
        dx, g_n2, _, _ = _ffn_bwd(
            dx, s3, w["ffn2_norm"], big["ffn2_w_in"], big["ffn2_w_out"], f"l{l}_ffn2",
            on_weight_grads=lambda g_in, g_out, l=l: on_weight_grads(l, 0, dict(ffn2_w_in=g_in, ffn2_w_out=g_out)))
        dx, g = _mixer_bwd(dx, s2, w, big, f"l{l}_mix", on_weight_grads=lambda g, l=l: on_weight_grads(l, 1, g))
        dx, g_n1, _, _ = _ffn_bwd(
            dx, s1, w["ffn1_norm"], big["ffn1_w_in"], big["ffn1_w_out"], f"l{l}_ffn1",
            on_weight_grads=lambda g_in, g_out, l=l: on_weight_grads(l, 2, dict(ffn1_w_in=g_in, ffn1_w_out=g_out)))
        small[l] = dict(g, ffn1_norm=g_n1, ffn2_norm=g_n2)
    return loss, dx, small


_BIG = ("ffn1_w_in", "ffn1_w_out", "w_in", "w_branch_a", "w_branch_b", "w_out", "ffn2_w_in", "ffn2_w_out")
_STAGES = (("ffn2_w_in", "ffn2_w_out"), ("w_in", "w_branch_a", "w_branch_b", "w_out"), ("ffn1_w_in", "ffn1_w_out"))
_SMALL = ("ffn1_norm", "mix_norm", "ffn2_norm", "dn_a_log", "dn_dt_bias", "dn_out_norm", "sb_q_norm", "sb_k_norm")
_ORDER = ("ffn1_norm", "ffn1_w_in", "ffn1_w_out", "mix_norm", "w_in", "dn_conv_w", "dn_a_log", "dn_dt_bias", "dn_out_norm",
          "sb_q_norm", "sb_k_norm", "w_branch_a", "w_branch_b", "w_out", "ffn2_norm", "ffn2_w_in", "ffn2_w_out")
COL_SCAL = 4 * D_MODEL
SCAL_SLOT = COL_SCAL // (N_IN // N_DEV)
SCAL_AT = COL_SCAL % (N_IN // N_DEV)
assert SCAL_AT + 2 * HEADS <= N_IN // N_DEV


def _pad_rows(a, multiple):
    pad = (-a.shape[-2]) % multiple
    return a if pad == 0 else jnp.pad(a, [(0, 0)] * (a.ndim - 2) + [(0, pad), (0, 0)])


def _lane_rows(a):
    flat = a.reshape(-1)
    flat = jnp.pad(flat, (0, (-flat.shape[0]) % 128))
    return flat.reshape(-1, 128)


def _pack_small(named):
    pieces, spans, r = [], {}, 0
    for n, a in named:
        rows = _lane_rows(a)
        spans[n] = (r, r + rows.shape[0], a.shape)
        r += rows.shape[0]
        pieces.append(rows)
    return _pad_rows(jnp.concatenate(pieces, axis=0), 8), spans


def _unpack_small(packed, spans, n):
    r0, r1, shape = spans[n]
    return packed[r0:r1].reshape(-1)[:math.prod(shape)].reshape(shape)


def kernel(x, ffn1_norm, ffn1_w_in, ffn1_w_out, mix_norm, w_in, dn_conv_w, dn_a_log, dn_dt_bias, dn_out_norm, sb_q_norm, sb_k_norm, w_branch_a, w_branch_b, w_out, ffn2_norm, ffn2_w_in, ffn2_w_out, loss_target, m_ffn1_norm, m_ffn1_w_in, m_ffn1_w_out, m_mix_norm, m_w_in, m_dn_conv_w, m_dn_a_log, m_dn_dt_bias, m_dn_out_norm, m_sb_q_norm, m_sb_k_norm, m_w_branch_a, m_w_branch_b, m_w_out, m_ffn2_norm, m_ffn2_w_in, m_ffn2_w_out, v_ffn1_norm, v_ffn1_w_in, v_ffn1_w_out, v_mix_norm, v_w_in, v_dn_conv_w, v_dn_a_log, v_dn_dt_bias, v_dn_out_norm, v_sb_q_norm, v_sb_k_norm, v_w_branch_a, v_w_branch_b, v_w_out, v_ffn2_norm, v_ffn2_w_in, v_ffn2_w_out):
    given = dict(locals())
    weights = {n: given[n] for n in _ORDER}
    mom_m = {n: given["m_" + n] for n in _ORDER}
    mom_v = {n: given["v_" + n] for n in _ORDER}
    L = ffn1_norm.shape[0]
    ax, ay, ac = _position()
    my_slot = 4 * ax + 2 * ay + ac

    conv_cols = dn_conv_w.shape[-1]
    first_ffn = _STAGES[2]
    later = tuple(n for n in _BIG if n not in first_ffn)
    gathers, landed = {}, {}

    def start_gather(key, names, l, zero, extra=()):
        shards = [(weights[n][l] + zero).astype(BF16) for n in names] + list(extra)
        first, shards, landing, token = _relay_gather_start(shards, name=f"gather_start_{key}")
        gathers[key] = dict(first=first, shards=shards, landing=landing, names=names)
        return token[0, 0]

    def pass_on(key, after):
        g = gathers[key]
        g["relay"], g["shards"], g["landing"], token = _relay_gather_pass_on(
            g["first"], g["shards"], g["landing"], [after], name=f"gather_pass_on_{key}")
        return token[0, 0]

    def wait_gather(key, after):
        g = gathers.pop(key)
        arrays, token = _relay_gather_wait(g["first"], g["relay"], g["shards"], g["landing"], [after],
                                           name=f"gather_wait_{key}")
        landed.update(zip(g["names"], arrays))
        if len(arrays) > len(g["names"]):
            landed["conv"] = arrays[-1]
        return token[0, 0]

    def weights_of(l, part, x_in):
        if l == 0 and part == 0:
            start_gather("l0_ffn1", first_ffn, 0, 0.0)
            pass_on("l0_ffn1", x_in)
            issued = start_gather("l0", later, 0, wait_gather("l0_ffn1", x_in), [_pad_rows(_lane_rows(dn_conv_w), 8)])
            return dict({n: landed.pop(n) for n in first_ffn}, issued=issued)
        if part == 0:
            token = wait_gather(f"l{l}", x_in)
            issued = start_gather(f"l{l + 1}", _BIG, l + 1, token) if l + 1 < L else token
            return dict({n: landed.pop(n) for n in first_ffn}, issued=issued)
        if part == 2:
            return dict(issued=pass_on(f"l{l + 1}", x_in) if l + 1 < L else 0.0)
        issued = 0.0
        if l == 0:
            pass_on("l0", x_in)
            issued = start_gather("l1", _BIG, 1, wait_gather("l0", x_in)) if L > 1 else 0.0
            conv = landed.pop("conv").reshape(N_DEV, -1)[:, :L * DN_CONV * conv_cols]
            landed["conv_w"] = conv.reshape(N_DEV, L, DN_CONV, conv_cols).transpose(1, 2, 0, 3).reshape(
                L, DN_CONV, N_DEV * conv_cols)
        big = {n: landed.pop(n) for n in later}
        wi = big.pop("w_in")
        pieces = [wi[d] for d in range(N_DEV)]
        pieces[SCAL_SLOT:SCAL_SLOT + 1] = [wi[SCAL_SLOT][:, :SCAL_AT], wi[SCAL_SLOT][:, SCAL_AT + 2 * HEADS:]]
        big["w_main"] = jnp.concatenate(pieces, axis=1)
        big["w_scal"] = jnp.pad(wi[SCAL_SLOT][:, SCAL_AT:SCAL_AT + 2 * HEADS], ((0, 0), (0, N_SCAL - 2 * HEADS)))
        big["conv_w"] = landed["conv_w"][l]
        return dict(big, issued=issued)

    layers = []
    for l in range(L):
        hp = jnp.concatenate([jnp.broadcast_to(dn_a_log[l][:, None, None], (HEADS, 1, 128)),
                              jnp.broadcast_to(dn_dt_bias[l][:, None, None], (HEADS, 1, 128)),
                              jnp.zeros((HEADS, 6, 128), F32)], axis=1)
        layers.append(dict(ffn1_norm=ffn1_norm[l][None], mix_norm=mix_norm[l][None], hp=hp,
                           dn_out_norm=dn_out_norm[l][None], sb_q_norm=sb_q_norm[l][None],
                           sb_k_norm=sb_k_norm[l][None], ffn2_norm=ffn2_norm[l][None]))

    in_flight = {}

    def on_weight_grads(l, stage, g):
        parts = dict(g)
        if stage == 1:
            gm, shard = g["w_main"], N_IN // N_DEV
            blocks = [gm[:, d * shard:(d + 1) * shard] for d in range(SCAL_SLOT)]
            blocks.append(jnp.concatenate([gm[:, SCAL_SLOT * shard:COL_SCAL], g["w_scal"][:, :2 * HEADS],
                                           gm[:, COL_SCAL:(SCAL_SLOT + 1) * shard - 2 * HEADS]], axis=1))
            blocks += [gm[:, d * shard - 2 * HEADS:(d + 1) * shard - 2 * HEADS] for d in range(SCAL_SLOT + 1, N_DEV)]
            parts["w_in"] = jnp.stack(blocks)
            for n in ("w_branch_a", "w_branch_b", "w_out"):
                parts[n] = g[n].reshape(N_DEV, D_MODEL // N_DEV, D_MODEL)
        *in_flight[l, stage], token = _exchange_start([parts[n] for n in _STAGES[stage]], gather=False,
                                                      name=f"scatter_start_l{l}_{stage}")
        return token

    loss_row, dx, grads = _local_step(x[0], loss_target[0], layers, weights_of, on_weight_grads)
    loss = lax.psum(loss_row[0, 0], ("x", "y", "c"))

    results = {n: None for n in _BIG}
    after = [dx]
    for l in reversed(range(L)):
        for stage, names in enumerate(_STAGES):
            landed, all_landed = _exchange_wait(*in_flight[l, stage], after, gather=False,
                                                name=f"scatter_wait_l{l}_{stage}")
            for n, parts in zip(names, landed):
                _, a, b = weights[n].shape
                results[n] = _adamw(parts, weights[n].reshape(L * a, b), mom_m[n].reshape(L * a, b),
                                    mom_v[n].reshape(L * a, b), layer=l, earlier=results[n], name=f"adamw_{n}_l{l}")
            after = [results[n][0] for n in names]
    out = {n: tuple(t.reshape(weights[n].shape) for t in results[n]) for n in _BIG}

    small_grads = [(n, jnp.stack([g[n].reshape(weights[n].shape[1:]) for g in grads])) for n in _SMALL]
    small_packed, spans = _pack_small(small_grads + [("conv", jnp.stack([g["conv_w"] for g in grads]))])
    small_packed = small_packed + all_landed[0, 0]
    small_sum = _sum_parts(_all_gather([small_packed], name="gather_small_grads")[0], name="sum_small_grads")
    rep_rows = spans["conv"][0]
    pack_rep = lambda d: _pad_rows(_pack_small([(n, d[n]) for n in _SMALL])[0], 8)
    rep_pad = (-rep_rows) % 8
    g_rep = jnp.pad(small_sum[:rep_rows], ((0, rep_pad), (0, 0)))
    res = _adamw(g_rep[None], pack_rep(weights), pack_rep(mom_m), pack_rep(mom_v), name="adamw_replicated")
    for n in _SMALL:
        out[n] = tuple(_unpack_small(t, spans, n) for t in res)
    conv_sum = _unpack_small(small_sum, spans, "conv")
    conv_mine = lax.dynamic_slice_in_dim(conv_sum, my_slot * conv_cols, conv_cols, axis=2).reshape(L * DN_CONV, conv_cols)
    flat = lambda t: t.reshape(L * DN_CONV, conv_cols)
    res = _adamw(conv_mine[None], flat(dn_conv_w), flat(m_dn_conv_w), flat(v_dn_conv_w), name="adamw_conv")
    out["dn_conv_w"] = tuple(t.reshape(L, DN_CONV, conv_cols) for t in res)

    return (loss, dx[None], *[out[n][0] for n in _ORDER], *[out[n][1] for n in _ORDER],
            *[out[n][2] for n in _ORDER], *[out[n][3] for n in _ORDER])
```

```python
import functools
import math

import jax
import jax.numpy as jnp
from jax import lax
from jax.experimental import pallas as pl
from jax.experimental.pallas import tpu as pltpu

F32 = jnp.float32
BF16 = jnp.bfloat16

N_DEV = 8
D_MODEL = 1024
DEPTH = 4
D_FF = 2816
HEADS = 8
HEAD_DIM = 128
DN_CHUNK = 64
DN_CONV = 4
DN_GROUP = 8
DN_HEADS = 2
SB_BLOCK = 128
SB_KEY_TILE = 512
SB_HEADS = 4
SB_HEADS_BWD = 2
SB_KEY_TILE_BWD = 512
RMS_EPS = 1e-6
L2_EPS = 1e-6
N_IN = 9232
N_MAIN = 9216
N_SCAL = 128
QK_SCALE = HEAD_DIM ** -0.5

ADAM_LR = 0.001
ADAM_B1 = 0.9
ADAM_B2 = 0.999
ADAM_EPS = 1e-08
ADAM_WD = 0.01
ADAM_STEP = 10

V7X_VMEM_LIMIT = 56 * 1024 * 1024
MESH = pl.DeviceIdType.MESH
ANY = pl.BlockSpec(memory_space=pl.ANY)


def _params(sem=None, vmem=V7X_VMEM_LIMIT):
    return pltpu.CompilerParams(dimension_semantics=sem, vmem_limit_bytes=vmem)


def _sigmoid(x):
    return 1.0 / (1.0 + jnp.exp(-x))


SOFTPLUS_LINEAR = 30.0


def _softplus(x):
    return jnp.maximum(x, jnp.log(1.0 + jnp.exp(jnp.minimum(x, SOFTPLUS_LINEAR))))


def _bdot(a, b, dims=(((1,), (0,)), ((), ()))):
    return lax.dot_general(a.astype(BF16), b.astype(BF16), dims, preferred_element_type=F32)


_NT = (((1,), (1,)), ((), ()))
_TN = (((0,), (0,)), ((), ()))


def _hdot(a, b, dims=(((1,), (0,)), ((), ()))):
    a_hi = a.astype(BF16)
    b_hi = b.astype(BF16)
    a_lo = (a - a_hi.astype(F32)).astype(BF16)
    b_lo = (b - b_hi.astype(F32)).astype(BF16)
    dot = functools.partial(lax.dot_general, dimension_numbers=dims, preferred_element_type=F32)
    return dot(a_hi, b_hi) + (dot(a_hi, b_lo) + dot(a_lo, b_hi))


def _hdot_tn(a, b):
    return _hdot(a, b, _TN)


def _mm(*, name, grid, a, a_spec, b, b_spec, out_shape, o_spec, tile, ta=False, tb=False, resid=None, scale=1.0):
    nk = grid[2]
    dims = (((0 if ta else 1,), (1 if tb else 0,)), ((), ()))

    def flat(v):
        return v if v.ndim == 2 else v.reshape(-1, v.shape[-1])

    def body(*refs):
        a_ref, b_ref = refs[:2]
        r_ref = refs[2] if resid is not None else None
        o_ref = refs[3] if resid is not None else refs[2]
        part = lax.dot_general(flat(a_ref[...]).astype(BF16), flat(b_ref[...]).astype(BF16), dims,
                               preferred_element_type=F32)

        def finish(acc):
            if scale != 1.0:
                acc = acc * scale
            if r_ref is not None:
                acc = r_ref[...] + acc
            o_ref[...] = acc.astype(o_ref.dtype)

        if nk == 1:
            finish(part)
        else:
            acc_ref = refs[-1]
            k = pl.program_id(2)

            @pl.when(k == 0)
            def _():
                acc_ref[...] = part

            @pl.when(k > 0)
            def _():
                acc_ref[...] += part

            @pl.when(k == nk - 1)
            def _():
                finish(acc_ref[...])

    in_specs = [a_spec, b_spec] + ([pl.BlockSpec(tile, lambda i, j, k: (i, j))] if resid is not None else [])
    args = (a, b) + ((resid,) if resid is not None else ())
    return pl.pallas_call(
        body, name=name, grid=grid, in_specs=in_specs, out_specs=o_spec, out_shape=out_shape,
        scratch_shapes=[pltpu.VMEM(tile, F32)] if nk > 1 else [],
        compiler_params=_params(("parallel", "parallel", "arbitrary")),
    )(*args)


def _matmul(a, b, *, name, ta=False, tb=False, out_dtype=F32, tm=None, tn=None, tk=None, resid=None, scale=1.0):
    if ta:
        K, M = a.shape
    else:
        M, K = a.shape
    N = b.shape[0] if tb else b.shape[1]
    tm = tm or min(M, 1024)
    tn = tn or min(N, 512)
    tk = tk or K
    assert M % tm == 0 and N % tn == 0 and K % tk == 0, (name, M, N, K, tm, tn, tk)
    a_spec = pl.BlockSpec((tk, tm), lambda i, j, k: (k, i)) if ta else pl.BlockSpec((tm, tk), lambda i, j, k: (i, k))
    b_spec = pl.BlockSpec((tn, tk), lambda i, j, k: (j, k)) if tb else pl.BlockSpec((tk, tn), lambda i, j, k: (k, j))
    return _mm(name=name, grid=(M // tm, N // tn, K // tk), a=a, a_spec=a_spec, b=b, b_spec=b_spec,
               out_shape=jax.ShapeDtypeStruct((M, N), out_dtype), o_spec=pl.BlockSpec((tm, tn), lambda i, j, k: (i, j)),
               tile=(tm, tn), ta=ta, tb=tb, resid=resid, scale=scale)


ROW_TILE = 256


def _rmsnorm_fwd(x, gain, *, name):
    T, D = x.shape

    def body(x_ref, g_ref, o_ref):
        xf = x_ref[...]
        r = lax.rsqrt(jnp.mean(xf * xf, axis=-1, keepdims=True) + RMS_EPS)
        o_ref[...] = (xf * r * g_ref[...]).astype(o_ref.dtype)

    return pl.pallas_call(
        body, name=name, grid=(T // ROW_TILE,),
        in_specs=[pl.BlockSpec((ROW_TILE, D), lambda i: (i, 0)), pl.BlockSpec((1, D), lambda i: (0, 0))],
        out_specs=pl.BlockSpec((ROW_TILE, D), lambda i: (i, 0)),
        out_shape=jax.ShapeDtypeStruct((T, D), BF16), compiler_params=_params(("parallel",)),
    )(x, gain)


def _rmsnorm_bwd(dh, x, gain, dres, *, name):
    T, D = x.shape

    def body(dh_ref, x_ref, g_ref, res_ref, dx_ref, dg_ref):
        xf = x_ref[...]
        r = lax.rsqrt(jnp.mean(xf * xf, axis=-1, keepdims=True) + RMS_EPS)
        y = xf * r
        dh_v = dh_ref[...].astype(F32)
        dy = dh_v * g_ref[...]
        dx_ref[...] = res_ref[...] + r * (dy - y * jnp.mean(dy * y, axis=-1, keepdims=True))

        @pl.when(pl.program_id(0) == 0)
        def _():
            dg_ref[...] = jnp.zeros_like(dg_ref)

        dg_ref[...] += jnp.sum(dh_v * y, axis=0, keepdims=True)

    row = pl.BlockSpec((ROW_TILE, D), lambda i: (i, 0))
    vec = pl.BlockSpec((1, D), lambda i: (0, 0))
    return pl.pallas_call(
        body, name=name, grid=(T // ROW_TILE,), in_specs=[row, row, vec, row], out_specs=(row, vec),
        out_shape=(jax.ShapeDtypeStruct((T, D), F32), jax.ShapeDtypeStruct((1, D), F32)),
        compiler_params=_params(("arbitrary",)),
    )(dh, x, gain, dres)


FF_HALF = N_DEV // 2


def _swiglu_fwd(p, *, name):
    _, T, fb = p.shape

    def body(g_ref, u_ref, o_ref):
        g = g_ref[...].astype(F32)
        o_ref[...] = (g * _sigmoid(g) * u_ref[...].astype(F32)).astype(o_ref.dtype)

    blk = (None, ROW_TILE, fb)
    return pl.pallas_call(
        body, name=name, grid=(T // ROW_TILE, FF_HALF),
        in_specs=[pl.BlockSpec(blk, lambda i, j: (j, i, 0)), pl.BlockSpec(blk, lambda i, j: (j + FF_HALF, i, 0))],
        out_specs=pl.BlockSpec(blk, lambda i, j: (j, i, 0)),
        out_shape=jax.ShapeDtypeStruct((FF_HALF, T, fb), BF16), compiler_params=_params(("parallel", "parallel")),
    )(p, p)


def _swiglu_bwd(da, p, *, name):
    _, T, fb = p.shape

    def body(da_ref, g_ref, u_ref, o_ref):
        g = g_ref[...].astype(F32)
        u = u_ref[...].astype(F32)
        d = da_ref[...].astype(F32)
        s = _sigmoid(g)
        o_ref[0] = (d * u * (s * (1.0 + g * (1.0 - s)))).astype(o_ref.dtype)
        o_ref[1] = (d * g * s).astype(o_ref.dtype)

    blk = (None, ROW_TILE, fb)
    out = pl.pallas_call(
        body, name=name, grid=(T // ROW_TILE, FF_HALF),
        in_specs=[pl.BlockSpec(blk, lambda i, j: (j, i, 0)), pl.BlockSpec(blk, lambda i, j: (j, i, 0)),
                  pl.BlockSpec(blk, lambda i, j: (j + FF_HALF, i, 0))],
        out_specs=pl.BlockSpec((2, None, ROW_TILE, fb), lambda i, j: (0, j, i, 0)),
        out_shape=jax.ShapeDtypeStruct((2, FF_HALF, T, fb), BF16), compiler_params=_params(("parallel", "parallel")),
    )(da, p, p)
    return out.reshape(2 * FF_HALF, T, fb)


COL_GATE_A = 7
COL_GATE_B = 8


def _merge_fwd(ya, yb, proj, *, name):
    T, D = ya.shape

    def body(ya_ref, yb_ref, ga_ref, gb_ref, o_ref):
        o_ref[...] = (_sigmoid(ga_ref[...]) * ya_ref[...] + _sigmoid(gb_ref[...]) * yb_ref[...]).astype(o_ref.dtype)

    row = pl.BlockSpec((ROW_TILE, D), lambda i: (i, 0))
    return pl.pallas_call(
        body, name=name, grid=(T // ROW_TILE,),
        in_specs=[row, row, pl.BlockSpec((ROW_TILE, D), lambda i: (i, COL_GATE_A)),
                  pl.BlockSpec((ROW_TILE, D), lambda i: (i, COL_GATE_B))],
        out_specs=row, out_shape=jax.ShapeDtypeStruct((T, D), BF16), compiler_params=_params(("parallel",)),
    )(ya, yb, proj, proj)


def _merge_bwd(dm, ya, yb, proj, *, name):
    T, D = ya.shape

    def body(dm_ref, ya_ref, yb_ref, ga_ref, gb_ref, dya_ref, dyb_ref, dga_ref, dgb_ref):
        d = dm_ref[...].astype(F32)
        sa = _sigmoid(ga_ref[...])
        sb = _sigmoid(gb_ref[...])
        dya_ref[...] = (d * sa).astype(BF16)
        dyb_ref[...] = (d * sb).astype(BF16)
        dga_ref[...] = (d * ya_ref[...] * sa * (1.0 - sa)).astype(BF16)
        dgb_ref[...] = (d * yb_ref[...] * sb * (1.0 - sb)).astype(BF16)

    row = pl.BlockSpec((ROW_TILE, D), lambda i: (i, 0))
    out = jax.ShapeDtypeStruct((T, D), BF16)
    return pl.pallas_call(
        body, name=name, grid=(T // ROW_TILE,),
        in_specs=[row, row, row, pl.BlockSpec((ROW_TILE, D), lambda i: (i, COL_GATE_A)),
                  pl.BlockSpec((ROW_TILE, D), lambda i: (i, COL_GATE_B))],
        out_specs=(row, row, row, row), out_shape=(out, out, out, out), compiler_params=_params(("parallel",)),
    )(dm, ya, yb, proj, proj)


def _loss_head(y, target, *, name):
    T, D = y.shape

    def body(y_ref, t_ref, loss_ref, dy_ref):
        err = y_ref[...] - t_ref[...]
        dy_ref[...] = err * (1.0 / D)

        @pl.when(pl.program_id(0) == 0)
        def _():
            loss_ref[...] = jnp.zeros_like(loss_ref)

        loss_ref[...] += 0.5 * jnp.sum(jnp.sum(err * err, axis=-1, keepdims=True) * (1.0 / D), axis=0, keepdims=True)

    row = pl.BlockSpec((ROW_TILE, D), lambda i: (i, 0))
    return pl.pallas_call(
        body, name=name, grid=(T // ROW_TILE,), in_specs=[row, row],
        out_specs=(pl.BlockSpec((1, 128), lambda i: (0, 0)), row),
        out_shape=(jax.ShapeDtypeStruct((1, 128), F32), jax.ShapeDtypeStruct((T, D), F32)),
        compiler_params=_params(("arbitrary",)),
    )(y, target)


CONV_PAD = 8


def _conv_taps(w, xp, T, first):
    acc = w[0:1, :] * xp[pl.ds(first, T), :]
    for i in range(1, DN_CONV):
        acc = acc + w[i:i + 1, :] * xp[pl.ds(first + i, T), :]
    return acc


def _conv_fwd(proj, conv_w, *, name):
    T = proj.shape[0]

    def body(x_ref, w_ref, o_ref, xp):
        xp[0:CONV_PAD, :] = jnp.zeros((CONV_PAD, HEAD_DIM), F32)
        xp[CONV_PAD:, :] = x_ref[...]
        y = _conv_taps(w_ref[...], xp, T, CONV_PAD - (DN_CONV - 1))
        s = y * _sigmoid(y)
        n = s * lax.rsqrt(jnp.sum(s * s, axis=-1, keepdims=True) + L2_EPS)
        o_ref[0] = jnp.where(pl.program_id(0) < 2, n, s)

    return pl.pallas_call(
        body, name=name, grid=(3, HEADS),
        in_specs=[pl.BlockSpec((T, HEAD_DIM), lambda c, h: (0, c * HEADS + h)),
                  pl.BlockSpec((DN_CONV, HEAD_DIM), lambda c, h: (0, c * HEADS + h))],
        out_specs=pl.BlockSpec((1, T, HEAD_DIM), lambda c, h: (c, 0, h)),
        out_shape=jax.ShapeDtypeStruct((3, T, D_MODEL), F32),
        scratch_shapes=[pltpu.VMEM((T + CONV_PAD, HEAD_DIM), F32)],
        compiler_params=_params(("parallel", "parallel")),
    )(proj, conv_w)


def _conv_bwd(dqkv, proj, conv_w, *, name):
    T = proj.shape[0]

    def body(d_ref, x_ref, w_ref, dx_ref, dw_ref, xp, dyp):
        xp[0:CONV_PAD, :] = jnp.zeros((CONV_PAD, HEAD_DIM), F32)
        xp[CONV_PAD:, :] = x_ref[...]
        w = w_ref[...]
        y = _conv_taps(w, xp, T, CONV_PAD - (DN_CONV - 1))
        sg = _sigmoid(y)
        s = y * sg
        r = lax.rsqrt(jnp.sum(s * s, axis=-1, keepdims=True) + L2_EPS)
        n = s * r
        d = d_ref[0]
        ds = jnp.where(pl.program_id(0) < 2, r * (d - n * jnp.sum(d * n, axis=-1, keepdims=True)), d)
        dy = ds * (sg * (1.0 + y * (1.0 - sg)))
        dyp[0:T, :] = dy
        dyp[T:, :] = jnp.zeros((CONV_PAD, HEAD_DIM), F32)
        dx = w[0:1, :] * dyp[pl.ds(DN_CONV - 1, T), :]
        for i in range(1, DN_CONV):
            dx = dx + w[i:i + 1, :] * dyp[pl.ds(DN_CONV - 1 - i, T), :]
        dx_ref[...] = dx.astype(dx_ref.dtype)
        for i in range(DN_CONV):
            dw_ref[i:i + 1, :] = jnp.sum(dy * xp[pl.ds(CONV_PAD - (DN_CONV - 1) + i, T), :], axis=0, keepdims=True)

    col = lambda c, h: (0, c * HEADS + h)
    return pl.pallas_call(
        body, name=name, grid=(3, HEADS),
        in_specs=[pl.BlockSpec((1, T, HEAD_DIM), lambda c, h: (c, 0, h)), pl.BlockSpec((T, HEAD_DIM), col),
                  pl.BlockSpec((DN_CONV, HEAD_DIM), col)],
        out_specs=(pl.BlockSpec((T, HEAD_DIM), col), pl.BlockSpec((DN_CONV, HEAD_DIM), col)),
        out_shape=(jax.ShapeDtypeStruct((T, 3 * D_MODEL), BF16), jax.ShapeDtypeStruct((DN_CONV, 3 * D_MODEL), F32)),
        scratch_shapes=[pltpu.VMEM((T + CONV_PAD, HEAD_DIM), F32), pltpu.VMEM((T + CONV_PAD, HEAD_DIM), F32)],
        compiler_params=_params(("parallel", "parallel")),
    )(dqkv, proj, conv_w)


def _inv_unit_lower(low, eye):
    x = eye - low
    power = _hdot(low, low, _B_NN)
    steps = int(math.log2(DN_CHUNK)) - 1
    for s in range(steps):
        x = x + _hdot(x, power, _B_NN)
        if s + 1 < steps:
            power = _hdot(power, power, _B_NN)
    return x


_B_NN = (((2,), (1,)), ((0,), (0,)))
_B_NT = (((2,), (2,)), ((0,), (0,)))
_B_TN = (((1,), (1,)), ((0,), (0,)))


def _dn_load(ref, lead, r0, group):
    rows = pl.ds(r0, group * DN_CHUNK)
    cols = lambda h: slice(h * HEAD_DIM, (h + 1) * HEAD_DIM)
    per_head = [(ref[rows, cols(h)] if lead is None else ref[lead, rows, cols(h)]).reshape(group, DN_CHUNK, HEAD_DIM)
                for h in range(DN_HEADS)]
    return jnp.stack(per_head, axis=1).reshape(group * DN_HEADS, DN_CHUNK, HEAD_DIM)


def _dn_chunk_setup(qkv_ref, b_ref, a_ref, hp_ref, n0, group, tinv=None):
    C = DN_CHUNK
    B = group * DN_HEADS
    r0 = pl.multiple_of(n0 * C, C)
    q = _dn_load(qkv_ref, 0, r0, group) * QK_SCALE
    k = _dn_load(qkv_ref, 1, r0, group)
    v = _dn_load(qkv_ref, 2, r0, group)
    ii = lax.broadcasted_iota(jnp.int32, (B, C, C), 1)
    jj = lax.broadcasted_iota(jnp.int32, (B, C, C), 2)
    eye_mask = ii == jj
    eye = jnp.where(eye_mask, 1.0, 0.0).astype(F32)

    def to_col(row):
        return jnp.sum(jnp.where(eye_mask, jnp.broadcast_to(row, (B, C, C)), 0.0), axis=2, keepdims=True)

    def to_row(col):
        return jnp.sum(jnp.where(eye_mask, jnp.broadcast_to(col, (B, C, C)), 0.0), axis=1, keepdims=True)

    def rows(ref):
        return jnp.stack([ref[h, pl.ds(n0, group)] for h in range(DN_HEADS)], axis=1).reshape(B, 1, C)

    def per_head(row):
        return jnp.stack([hp_ref[h, row:row + 1, 0:C] for h in range(DN_HEADS)] * group, axis=0)

    b_row = rows(b_ref)
    a_row = rows(a_ref)
    a_log = per_head(0)
    dt_b = per_head(1)
    beta_row = _sigmoid(b_row)
    neg_ea = -jnp.exp(a_log)
    g_row = neg_ea * _softplus(a_row + dt_b)
    gc_col = jnp.sum(jnp.where(jj <= ii, jnp.broadcast_to(g_row, (B, C, C)), 0.0), axis=2, keepdims=True)
    gc_row = to_row(gc_col)
    g_last = jnp.sum(g_row, axis=2, keepdims=True)
    beta = to_col(beta_row)
    low_incl = ii >= jj
    decay = jnp.exp(jnp.where(low_incl, gc_col - gc_row, -jnp.inf))
    eg = jnp.exp(gc_col)
    egl = jnp.exp(g_last - gc_col)
    el = jnp.exp(g_last)
    kb = k * beta
    pmat = _bdot(kb, k, _B_NT)
    low = jnp.where(ii > jj, pmat * decay, 0.0)
    if tinv is None:
        tinv = _inv_unit_lower(low, eye)
    u = _hdot(tinv, v * beta, _B_NN)
    w = _hdot(tinv, kb * eg, _B_NN)
    qk = _bdot(q, k, _B_NT)
    attn = qk * decay
    return dict(q=q, k=k, v=v, ii=ii, jj=jj, to_col=to_col, to_row=to_row, b_row=b_row, a_row=a_row, dt_b=dt_b,
                beta_row=beta_row, neg_ea=neg_ea, g_row=g_row, gc_col=gc_col, g_last=g_last, beta=beta,
                decay=decay, eg=eg, egl=egl, el=el, kb=kb, pmat=pmat, tinv=tinv, u=u, w=w, qk=qk, attn=attn,
                qd=q * eg, kd=k * egl, r0=r0)


def _dn_store(ref, lead, r0, group, value):
    value = value.reshape(group, DN_HEADS, DN_CHUNK, HEAD_DIM)
    for h in range(DN_HEADS):
        block = value[:, h].reshape(group * DN_CHUNK, HEAD_DIM)
        if lead is None:
            ref[pl.ds(r0, group * DN_CHUNK), h * HEAD_DIM:(h + 1) * HEAD_DIM] = block
        else:
            ref[lead, pl.ds(r0, group * DN_CHUNK), h * HEAD_DIM:(h + 1) * HEAD_DIM] = block


def _dn_specs(T):
    nc = T // DN_CHUNK
    qkv = pl.BlockSpec((3, T, DN_HEADS * HEAD_DIM), lambda h: (0, 0, h))
    rows = pl.BlockSpec((DN_HEADS, nc, 1, DN_CHUNK), lambda h: (h, 0, 0, 0))
    hp = pl.BlockSpec((DN_HEADS, 8, 128), lambda h: (h, 0, 0))
    states = pl.BlockSpec((DN_HEADS, nc, HEAD_DIM, HEAD_DIM), lambda h: (h, 0, 0, 0))
    return nc, qkv, rows, hp, states


def _dn_inverse_spec(T):
    return pl.BlockSpec((DN_HEADS, T // DN_CHUNK, DN_CHUNK, DN_CHUNK), lambda h: (h, 0, 0, 0))


def _dn_per_head(ref, n0, group):
    stacked = jnp.stack([ref[h, pl.ds(n0, group)] for h in range(DN_HEADS)], axis=1)
    return stacked.reshape((group * DN_HEADS,) + stacked.shape[2:])


def _dn_fwd(qkv, b_rows, a_rows, hp, *, name):
    T = qkv.shape[1]
    nc, qkv_spec, row_spec, hp_spec, st_spec = _dn_specs(T)
    group = math.gcd(nc, DN_GROUP)
    H = DN_HEADS

    def body(qkv_ref, b_ref, a_ref, hp_ref, o_ref, st_ref, inv_ref, s_scr):
        s_scr[...] = jnp.zeros_like(s_scr)

        def step(t, carry):
            n0 = t * group
            c = _dn_chunk_setup(qkv_ref, b_ref, a_ref, hp_ref, n0, group)
            tinv = c["tinv"].reshape(group, H, DN_CHUNK, DN_CHUNK)
            for h in range(H):
                inv_ref[h, pl.ds(n0, group)] = tinv[:, h]
            state = s_scr[...]
            outs = []
            for g in range(group):
                sl = slice(g * H, (g + 1) * H)
                for h in range(H):
                    st_ref[h, n0 + g] = state[h]
                v_new = c["u"][sl] - _bdot(c["w"][sl], state, _B_NN)
                outs.append(_bdot(c["qd"][sl], state, _B_NN) + _bdot(c["attn"][sl], v_new, _B_NN))
                state = state * c["el"][sl] + _bdot(c["kd"][sl], v_new, _B_TN)
            s_scr[...] = state
            _dn_store(o_ref, None, c["r0"], group, jnp.concatenate(outs, axis=0))
            return carry

        lax.fori_loop(0, nc // group, step, 0)

    return pl.pallas_call(
        body, name=name, grid=(HEADS // H,), in_specs=[qkv_spec, row_spec, row_spec, hp_spec],
        out_specs=(pl.BlockSpec((T, H * HEAD_DIM), lambda h: (0, h)), st_spec, _dn_inverse_spec(T)),
        out_shape=(jax.ShapeDtypeStruct((T, D_MODEL), F32),
                   jax.ShapeDtypeStruct((HEADS, nc, HEAD_DIM, HEAD_DIM), F32),
                   jax.ShapeDtypeStruct((HEADS, nc, DN_CHUNK, DN_CHUNK), F32)),
        scratch_shapes=[pltpu.VMEM((H, HEAD_DIM, HEAD_DIM), F32)], compiler_params=_params(("parallel",)),
    )(qkv, b_rows, a_rows, hp)


def _dn_bwd(qkv, b_rows, a_rows, hp, states, inverses, do, *, name):
    T = qkv.shape[1]
    C = DN_CHUNK
    nc, qkv_spec, row_spec, hp_spec, st_spec = _dn_specs(T)
    group = math.gcd(nc, DN_GROUP)
    H = DN_HEADS
    B = group * H

    def body(qkv_ref, b_ref, a_ref, hp_ref, st_ref, inv_ref, do_ref, dqkv_ref, db_ref, da_ref, dhp_ref, ds_scr, acc_scr):
        ds_scr[...] = jnp.zeros_like(ds_scr)
        acc_scr[...] = jnp.zeros_like(acc_scr)

        def step(t, carry):
            n0 = nc - (t + 1) * group
            c = _dn_chunk_setup(qkv_ref, b_ref, a_ref, hp_ref, n0, group, tinv=_dn_per_head(inv_ref, n0, group))
            state = _dn_per_head(st_ref, n0, group)
            d_o = _dn_load(do_ref, None, c["r0"], group)
            v_new = c["u"] - _bdot(c["w"], state, _B_NN)
            d_vnew_local = _bdot(c["attn"], d_o, _B_TN)
            d_state_local = _bdot(c["qd"], d_o, _B_TN)
            d_state = ds_scr[...]
            d_vnew, d_kd, d_el = [None] * group, [None] * group, [None] * group
            for g in reversed(range(group)):
                sl = slice(g * H, (g + 1) * H)
                d_vnew[g] = d_vnew_local[sl] + _bdot(c["kd"][sl], d_state, _B_NN)
                d_kd[g] = _bdot(v_new[sl], d_state, _B_NT)
                d_el[g] = jnp.sum(jnp.sum(d_state * state[sl], axis=2, keepdims=True), axis=1, keepdims=True)
                d_state = d_state * c["el"][sl] + d_state_local[sl] - _bdot(c["w"][sl], d_vnew[g], _B_TN)
            ds_scr[...] = d_state
            chunk_grads(c, n0, state, d_o, v_new, jnp.concatenate(d_vnew, axis=0), jnp.concatenate(d_kd, axis=0),
                        jnp.concatenate(d_el, axis=0))
            return carry

        def chunk_grads(c, n0, state, d_o, v_new, d_vnew, d_kd, d_el):
            ii, jj = c["ii"], c["jj"]
            q, k, v, kb, beta = c["q"], c["k"], c["v"], c["kb"], c["beta"]
            decay, eg, egl, el = c["decay"], c["eg"], c["egl"], c["el"]
            u, w, tinv = c["u"], c["w"], c["tinv"]
            d_qd = _bdot(d_o, state, _B_NT)
            d_attn = _bdot(d_o, v_new, _B_NT)
            d_w = -_bdot(d_vnew, state, _B_NT)
            d_rv = _hdot(tinv, d_vnew, _B_TN)
            d_rw = _hdot(tinv, d_w, _B_TN)
            d_amat = -(_bdot(d_rv, u, _B_NT) + _bdot(d_rw, w, _B_NT))
            d_low = jnp.where(ii > jj, d_amat, 0.0)
            d_p = d_low * decay
            d_qk = d_attn * decay
            e_mat = (d_low * c["pmat"] + d_attn * c["qk"]) * decay
            d_q = _bdot(d_qk, k, _B_NN) + d_qd * eg
            d_kb = _bdot(d_p, k, _B_NN) + d_rw * eg
            d_k = _bdot(d_qk, q, _B_TN) + _bdot(d_p, kb, _B_TN) + d_kd * egl + d_kb * beta
            d_beta = jnp.sum(d_kb * k, axis=2, keepdims=True) + jnp.sum(d_rv * v, axis=2, keepdims=True)
            d_v = d_rv * beta
            d_eg = jnp.sum(d_qd * q, axis=2, keepdims=True) + jnp.sum(d_rw * kb, axis=2, keepdims=True)
            d_egl = jnp.sum(d_kd * k, axis=2, keepdims=True)
            d_glast = jnp.sum(d_egl * egl, axis=1, keepdims=True) + d_el * el
            row_sum = jnp.sum(e_mat, axis=2, keepdims=True)
            col_sum = c["to_col"](jnp.sum(e_mat, axis=1, keepdims=True))
            d_gc = row_sum - col_sum + d_eg * eg - d_egl * egl
            d_g_row = jnp.sum(jnp.where(ii >= jj, jnp.broadcast_to(d_gc, (B, C, C)), 0.0), axis=1, keepdims=True) + d_glast
            beta_row = c["beta_row"]
            d_b_row = c["to_row"](d_beta) * beta_row * (1.0 - beta_row)
            d_a_row = d_g_row * c["neg_ea"] * _sigmoid(c["a_row"] + c["dt_b"])
            _dn_store(dqkv_ref, 0, c["r0"], group, d_q * QK_SCALE)
            _dn_store(dqkv_ref, 1, c["r0"], group, d_k)
            _dn_store(dqkv_ref, 2, c["r0"], group, d_v)
            d_b_row = d_b_row.reshape(group, H, 1, C)
            d_a_row = d_a_row.reshape(group, H, 1, C)
            d_a_log = jnp.sum((d_g_row * c["g_row"]).reshape(group, H, 1, C), axis=0)
            d_dt_b = jnp.sum(d_a_row, axis=0)
            for h in range(H):
                db_ref[h, pl.ds(n0, group)] = d_b_row[:, h]
                da_ref[h, pl.ds(n0, group)] = d_a_row[:, h]
                acc_scr[h, 0:1, 0:C] += d_a_log[h]
                acc_scr[h, 1:2, 0:C] += d_dt_b[h]

        lax.fori_loop(0, nc // group, step, 0)
        for h in range(H):
            tot = jnp.sum(acc_scr[h], axis=1, keepdims=True)
            dhp_ref[h] = jnp.broadcast_to(tot, (8, 128))

    return pl.pallas_call(
        body, name=name, grid=(HEADS // H,),
        in_specs=[qkv_spec, row_spec, row_spec, hp_spec, st_spec, _dn_inverse_spec(T),
                  pl.BlockSpec((T, H * HEAD_DIM), lambda h: (0, h))],
        out_specs=(qkv_spec, row_spec, row_spec, hp_spec),
        out_shape=(jax.ShapeDtypeStruct((3, T, D_MODEL), F32), jax.ShapeDtypeStruct((HEADS, nc, 1, C), F32),
                   jax.ShapeDtypeStruct((HEADS, nc, 1, C), F32), jax.ShapeDtypeStruct((HEADS, 8, 128), F32)),
        scratch_shapes=[pltpu.VMEM((H, HEAD_DIM, HEAD_DIM), F32), pltpu.VMEM((H, 8, 128), F32)],
        compiler_params=_params(("parallel",)),
    )(qkv, b_rows, a_rows, hp, states, inverses, do)


COL_Z = 3 * HEADS


def _gated_norm_fwd(o, proj, gain, *, name):
    T = o.shape[0]

    def body(o_ref, z_ref, g_ref, out_ref):
        x = o_ref[...]
        r = lax.rsqrt(jnp.mean(x * x, axis=-1, keepdims=True) + RMS_EPS)
        z = z_ref[...]
        out_ref[...] = (x * r * g_ref[...] * (z * _sigmoid(z))).astype(out_ref.dtype)

    return pl.pallas_call(
        body, name=name, grid=(HEADS,),
        in_specs=[pl.BlockSpec((T, HEAD_DIM), lambda h: (0, h)), pl.BlockSpec((T, HEAD_DIM), lambda h: (0, COL_Z + h)),
                  pl.BlockSpec((1, HEAD_DIM), lambda h: (0, 0))],
        out_specs=pl.BlockSpec((T, HEAD_DIM), lambda h: (0, h)),
        out_shape=jax.ShapeDtypeStruct((T, D_MODEL), BF16), compiler_params=_params(("parallel",)),
    )(o, proj, gain)


def _gated_norm_bwd(dout, o, proj, gain, *, name):
    T = o.shape[0]

    def body(d_ref, o_ref, z_ref, g_ref, do_ref, dz_ref, dg_ref):
        x = o_ref[...]
        r = lax.rsqrt(jnp.mean(x * x, axis=-1, keepdims=True) + RMS_EPS)
        n = x * r
        z = z_ref[...]
        sg = _sigmoid(z)
        d = d_ref[...].astype(F32)
        g = g_ref[...]
        dz_ref[...] = (d * n * g * (sg * (1.0 + z * (1.0 - sg)))).astype(dz_ref.dtype)
        dy = d * (z * sg)
        dyg = dy * g
        do_ref[...] = r * (dyg - n * jnp.mean(dyg * n, axis=-1, keepdims=True))

        @pl.when(pl.program_id(0) == 0)
        def _():
            dg_ref[...] = jnp.zeros_like(dg_ref)

        dg_ref[...] += jnp.sum(dy * n, axis=0, keepdims=True)

    head = pl.BlockSpec((T, HEAD_DIM), lambda h: (0, h))
    vec = pl.BlockSpec((1, HEAD_DIM), lambda h: (0, 0))
    return pl.pallas_call(
        body, name=name, grid=(HEADS,),
        in_specs=[head, head, pl.BlockSpec((T, HEAD_DIM), lambda h: (0, COL_Z + h)), vec],
        out_specs=(head, head, vec),
        out_shape=(jax.ShapeDtypeStruct((T, D_MODEL), F32), jax.ShapeDtypeStruct((T, D_MODEL), BF16),
                   jax.ShapeDtypeStruct((1, HEAD_DIM), F32)),
        compiler_params=_params(("arbitrary",)),
    )(dout, o, proj, gain)


COL_SBQ = 4 * HEADS
COL_SBK = 5 * HEADS
COL_SBV = 6 * HEADS


def _split_dot(x, mat):
    lead = x.shape[:-1]
    x = x.reshape(-1, x.shape[-1])
    hi = x.astype(BF16)
    lo = (x - hi.astype(F32)).astype(BF16)
    out = jnp.dot(hi, mat, preferred_element_type=F32) + jnp.dot(lo, mat, preferred_element_type=F32)
    return out.reshape(lead + (mat.shape[-1],))


def _sb_specs(T, heads):
    col = lambda first: pl.BlockSpec((T, heads * HEAD_DIM), lambda h: (0, first // heads + h))
    return col(COL_SBQ), col(COL_SBK), col(COL_SBV), pl.BlockSpec((1, HEAD_DIM), lambda h: (0, 0))


def _heads_first(x):
    return jnp.stack([x[:, c:c + HEAD_DIM] for c in range(0, x.shape[1], HEAD_DIM)], axis=0)


def _heads_last(x):
    return jnp.concatenate([x[h] for h in range(x.shape[0])], axis=1)


def _head_rms(x):
    r = lax.rsqrt(jnp.mean(x * x, axis=-1, keepdims=True) + RMS_EPS)
    return x * r, r


def _sb_fwd(proj, q_gain, k_gain, *, name):
    T = proj.shape[0]
    B = SB_BLOCK
    H = SB_HEADS
    nb = T // B
    KT = min(SB_KEY_TILE, T)
    NS = KT // B
    q_spec, k_spec, v_spec, g_spec = _sb_specs(T, H)

    def body(q_ref, k_ref, v_ref, gq_ref, gk_ref, o_ref, lt_ref, qs, ks, vs):
        qs[...] = (_head_rms(_heads_first(q_ref[...]))[0] * (gq_ref[...] * QK_SCALE)).astype(BF16)
        ks[...] = (_head_rms(_heads_first(k_ref[...]))[0] * gk_ref[...]).astype(BF16)
        vs[...] = _heads_first(v_ref[...]).astype(BF16)
        ii = lax.broadcasted_iota(jnp.int32, (B, B), 0)
        jj = lax.broadcasted_iota(jnp.int32, (B, B), 1)
        after = jnp.where(ii > jj, 1.0, 0.0).astype(BF16)
        ahead = lax.broadcasted_iota(jnp.int32, (H, B, KT), 2) - lax.broadcasted_iota(jnp.int32, (H, B, KT), 1)

        def q_block(i, carry):
            rows = pl.ds(pl.multiple_of(i * B, B), B)
            q = qs[:, rows, :]

            def tile(c0, acc, tail, masked):
                cols = pl.ds(c0, KT)
                z = lax.dot_general(q, ks[:, cols, :], _B_NT, preferred_element_type=F32)
                sp = _softplus(z)
                causal = ahead < (i * B - c0)
                loss = jnp.where(causal, sp, 0.0) if masked else sp
                parts = [None] * NS
                for b in reversed(range(NS)):
                    blk = loss[:, :, b * B:(b + 1) * B]
                    parts[b] = _split_dot(blk, after) + tail
                    tail = tail + jnp.sum(blk, axis=2, keepdims=True)
                lost = parts[0] if NS == 1 else jnp.concatenate(parts, axis=2)
                wts = jnp.exp(z - sp - lost)
                if masked:
                    wts = jnp.where(causal, wts, 0.0)
                acc = acc + lax.dot_general(wts.astype(BF16), vs[:, cols, :], _B_NN, preferred_element_type=F32)
                return acc, tail

            last = i // NS
            acc, tail = tile(pl.multiple_of(last * KT, KT), jnp.zeros((H, B, HEAD_DIM), F32), jnp.zeros((H, B, 1), F32), True)
            acc, tail = lax.fori_loop(
                1, last + 1, lambda s, c: tile(pl.multiple_of((last - s) * KT, KT), c[0], c[1], False), (acc, tail))
            o_ref[rows, :] = _heads_last(acc).astype(o_ref.dtype)
            lt_ref[rows, :] = _heads_last(jnp.broadcast_to(tail, (H, B, HEAD_DIM)))
            return carry

        lax.fori_loop(0, nb, q_block, 0)

    heads = pl.BlockSpec((T, H * HEAD_DIM), lambda h: (0, h))
    return pl.pallas_call(
        body, name=name, grid=(HEADS // H,), in_specs=[q_spec, k_spec, v_spec, g_spec, g_spec],
        out_specs=(heads, heads),
        out_shape=(jax.ShapeDtypeStruct((T, D_MODEL), BF16), jax.ShapeDtypeStruct((T, D_MODEL), F32)),
        scratch_shapes=[pltpu.VMEM((H, T, HEAD_DIM), BF16)] * 3, compiler_params=_params(("parallel",)),
    )(proj, proj, proj, q_gain, k_gain)


def _sb_bwd(proj, q_gain, k_gain, ltot, do, *, name):
    T = proj.shape[0]
    B = SB_BLOCK
    H = SB_HEADS_BWD
    nb = T // B
    KT = min(SB_KEY_TILE_BWD, T)
    NS = KT // B
    q_spec, k_spec, v_spec, g_spec = _sb_specs(T, H)

    def body(q_ref, k_ref, v_ref, gq_ref, gk_ref, lt_ref, do_ref, dq_ref, dk_ref, dv_ref, dgq_ref, dgk_ref,
             qs, ks, vs, dos, lts, dq_acc, dk_acc, dv_acc):
        qn, q_r = _head_rms(_heads_first(q_ref[...]))
        kn, k_r = _head_rms(_heads_first(k_ref[...]))
        qs[...] = (qn * (gq_ref[...] * QK_SCALE)).astype(BF16)
        ks[...] = (kn * gk_ref[...]).astype(BF16)
        vs[...] = _heads_first(v_ref[...]).astype(BF16)
        dos[...] = _heads_first(do_ref[...]).astype(BF16)
        lts[...] = _heads_first(lt_ref[...])
        dk_acc[...] = jnp.zeros_like(dk_acc)
        dv_acc[...] = jnp.zeros_like(dv_acc)
        ii = lax.broadcasted_iota(jnp.int32, (B, B), 0)
        jj = lax.broadcasted_iota(jnp.int32, (B, B), 1)
        upto = jnp.where(ii <= jj, 1.0, 0.0).astype(BF16)
        before = jnp.where(ii < jj, 1.0, 0.0).astype(BF16)
        ahead = lax.broadcasted_iota(jnp.int32, (H, B, KT), 2) - lax.broadcasted_iota(jnp.int32, (H, B, KT), 1)

        def q_block(i, carry):
            rows = pl.ds(pl.multiple_of(i * B, B), B)
            q = qs[:, rows, :]
            d_o = dos[:, rows, :]
            total = jnp.max(lts[:, rows, :], axis=2, keepdims=True)

            def tile(c0, dq, head_lb, head_de, masked):
                cols = pl.ds(c0, KT)
                k = ks[:, cols, :]
                v = vs[:, cols, :]
                z = lax.dot_general(q, k, _B_NT, preferred_element_type=F32)
                sp = _softplus(z)
                causal = ahead < (i * B - c0)
                loss = jnp.where(causal, sp, 0.0) if masked else sp
                parts = [None] * NS
                for b in range(NS):
                    blk = loss[:, :, b * B:(b + 1) * B]
                    parts[b] = _split_dot(blk, upto) + head_lb
                    head_lb = head_lb + jnp.sum(blk, axis=2, keepdims=True)
                prefix = parts[0] if NS == 1 else jnp.concatenate(parts, axis=2)
                wts = jnp.exp(z - sp + (prefix - total))
                if masked:
                    wts = jnp.where(causal, wts, 0.0)
                d_w = lax.dot_general(d_o, v, _B_NT, preferred_element_type=F32)
                d_e = wts * d_w
                d_eb = d_e.astype(BF16)
                for b in range(NS):
                    inside = jnp.dot(d_eb[:, :, b * B:(b + 1) * B].reshape(H * B, B), before, preferred_element_type=F32)
                    parts[b] = inside.reshape(H, B, B) + head_de
                    head_de = head_de + jnp.sum(d_e[:, :, b * B:(b + 1) * B], axis=2, keepdims=True)
                cum = parts[0] if NS == 1 else jnp.concatenate(parts, axis=2)
                sig = jnp.exp(z - sp)
                d_z = d_e - sig * (d_e + cum)
                if masked:
                    d_z = jnp.where(causal, d_z, 0.0)
                d_zb = d_z.astype(BF16)
                dq = dq + lax.dot_general(d_zb, k, _B_NN, preferred_element_type=F32)
                dk_acc[:, cols, :] += lax.dot_general(d_zb, q, _B_TN, preferred_element_type=F32)
                dv_acc[:, cols, :] += lax.dot_general(wts.astype(BF16), d_o, _B_TN, preferred_element_type=F32)
                return dq, head_lb, head_de

            last = i // NS
            zero = jnp.zeros((H, B, 1), F32)
            state = lax.fori_loop(0, last, lambda t, c: tile(pl.multiple_of(t * KT, KT), *c, False),
                                  (jnp.zeros((H, B, HEAD_DIM), F32), zero, zero))
            dq, _, _ = tile(pl.multiple_of(last * KT, KT), *state, True)
            dq_acc[:, rows, :] = dq * QK_SCALE
            return carry

        lax.fori_loop(0, nb, q_block, 0)

        def norm_bwd(d_scaled, n, r, gain):
            dn = d_scaled * gain
            d_gain = jnp.sum(jnp.sum(d_scaled * n, axis=1, keepdims=True), axis=0)
            return r * (dn - n * jnp.mean(dn * n, axis=-1, keepdims=True)), d_gain

        dq_raw, dgq = norm_bwd(dq_acc[...], qn, q_r, gq_ref[...])
        dk_raw, dgk = norm_bwd(dk_acc[...], kn, k_r, gk_ref[...])
        dq_ref[...] = _heads_last(dq_raw).astype(dq_ref.dtype)
        dk_ref[...] = _heads_last(dk_raw).astype(dk_ref.dtype)
        dv_ref[...] = _heads_last(dv_acc[...]).astype(dv_ref.dtype)

        @pl.when(pl.program_id(0) == 0)
        def _():
            dgq_ref[...] = jnp.zeros_like(dgq_ref)
            dgk_ref[...] = jnp.zeros_like(dgk_ref)

        dgq_ref[...] += dgq
        dgk_ref[...] += dgk

    heads = pl.BlockSpec((T, H * HEAD_DIM), lambda h: (0, h))
    out = jax.ShapeDtypeStruct((T, D_MODEL), BF16)
    vec = jax.ShapeDtypeStruct((1, HEAD_DIM), F32)
    return pl.pallas_call(
        body, name=name, grid=(HEADS // H,), in_specs=[q_spec, k_spec, v_spec, g_spec, g_spec, heads, heads],
        out_specs=(heads, heads, heads, g_spec, g_spec), out_shape=(out, out, out, vec, vec),
        scratch_shapes=[pltpu.VMEM((H, T, HEAD_DIM), BF16)] * 4 + [pltpu.VMEM((H, T, HEAD_DIM), F32)] * 4,
        compiler_params=_params(("arbitrary",)),
    )(proj, proj, proj, q_gain, k_gain, ltot, do)


ADAM_ROWS = 256


def _adamw(g_parts, w, m, v, *, name, layer=0, earlier=None):
    K, A, C = g_parts.shape
    R = w.shape[0]
    tr = next((t for t in (ADAM_ROWS, ADAM_ROWS // 2) if A % t == 0), A // 2 if A % 32 == 0 else A)
    first_block = layer * (A // tr)

    def body(g_ref, w_ref, m_ref, v_ref, *rest):
        go_ref, d_ref, mo_ref, vo_ref = rest[-4:]
        g = g_ref[0].astype(F32)
        for k in range(1, K):
            g = g + g_ref[k].astype(F32)
        go_ref[...] = g
        m_new = ADAM_B1 * m_ref[...] + (1.0 - ADAM_B1) * g
        v_new = ADAM_B2 * v_ref[...] + (1.0 - ADAM_B2) * (g * g)
        m_hat = m_new / (1.0 - ADAM_B1 ** ADAM_STEP)
        v_hat = v_new / (1.0 - ADAM_B2 ** ADAM_STEP)
        d_ref[...] = -ADAM_LR * (m_hat / (jnp.sqrt(v_hat) + ADAM_EPS) + ADAM_WD * w_ref[...])
        mo_ref[...] = m_new
        vo_ref[...] = v_new

    row = pl.BlockSpec((tr, C), lambda i: (first_block + i, 0))
    out = jax.ShapeDtypeStruct((R, C), F32)
    in_specs = [pl.BlockSpec((K, tr, C), lambda i: (0, i, 0)), row, row, row]
    if earlier is None:
        return pl.pallas_call(
            body, name=name, grid=(A // tr,), in_specs=in_specs, out_specs=(row, row, row, row),
            out_shape=(out, out, out, out), compiler_params=_params(("parallel",)),
        )(g_parts, w, m, v)
    return pl.pallas_call(
        body, name=name, grid=(A // tr,), in_specs=in_specs + [ANY] * 4, out_specs=(row, row, row, row),
        out_shape=(out, out, out, out), input_output_aliases={4 + j: j for j in range(4)},
        compiler_params=_params(("parallel",)),
    )(g_parts, w, m, v, *earlier)


def _sum_parts(parts, *, name):
    K, R, C = parts.shape

    def body(p_ref, o_ref):
        acc = p_ref[0]
        for k in range(1, K):
            acc = acc + p_ref[k]
        o_ref[...] = acc

    return pl.pallas_call(body, name=name, out_shape=jax.ShapeDtypeStruct((R, C), F32))(parts)


def _position():
    return lax.axis_index("x"), lax.axis_index("y"), lax.axis_index("c")


def _all_gather(shards, *, name):
    n = len(shards)

    def body(*refs):
        x_refs, out_refs = refs[:n], refs[n:2 * n]
        send_sems, recv_sems, local_sems = refs[2 * n:]
        x, y, c = _position()
        me, sibling = (x, y, c), (x, y, 1 - c)
        chips = [(1 - x, y), (x, 1 - y), (1 - x, 1 - y)]

        def slot(a, px, py, pc):
            return out_refs[a].at[4 * px + 2 * py + pc]

        def copy(a, k, block, to, own=False):
            return pltpu.make_async_remote_copy(
                src_ref=x_refs[a] if own else slot(a, *block), dst_ref=slot(a, *block),
                send_sem=send_sems.at[a, k], recv_sem=recv_sems.at[a, k], device_id=to, device_id_type=MESH)

        mine = [pltpu.make_async_copy(x_refs[a], slot(a, *me), local_sems.at[a]) for a in range(n)]
        for cp in mine:
            cp.start()
        first = [copy(a, 1 + j, me, (*chip, c), own=True) for j, chip in enumerate(chips) for a in range(n)]
        first += [copy(a, 0, me, sibling, own=True) for a in range(n)]
        for cp in first:
            cp.start()
        passed = []
        for j, chip in enumerate(chips):
            for a in range(n):
                copy(a, 1 + j, (*chip, c), me).wait_recv()
                passed.append(copy(a, 4 + j, (*chip, c), sibling))
                passed[-1].start()
        for a in range(n):
            copy(a, 0, sibling, me).wait_recv()
        for j, chip in enumerate(chips):
            for a in range(n):
                copy(a, 4 + j, (*chip, 1 - c), me).wait_recv()
        for cp in first + passed:
            cp.wait_send()
        for cp in mine:
            cp.wait()

    return pl.pallas_call(
        body, name=name, in_specs=[ANY] * n, out_specs=[ANY] * n,
        out_shape=[jax.ShapeDtypeStruct((N_DEV,) + s.shape, s.dtype) for s in shards],
        scratch_shapes=[pltpu.SemaphoreType.DMA((n, 7)), pltpu.SemaphoreType.DMA((n, 7)), pltpu.SemaphoreType.DMA((n,))],
    )(*shards)


HBM = pl.BlockSpec(memory_space=pltpu.HBM)
SEM = pl.BlockSpec(memory_space=pltpu.SEMAPHORE)
DATAFLOW = pltpu.SideEffectType.DATAFLOW_SIDE_EFFECTING


def _exchange_copies(gather, x_refs, land_refs, send_sems, recv_sems, local_sems):
    n = len(x_refs)
    x, y, c = _position()
    me = 4 * x + 2 * y + c

    def src(a, slot):
        return x_refs[a] if gather else x_refs[a].at[slot]

    mine = [pltpu.make_async_copy(src(a, me), land_refs[a].at[me], local_sems.at[a]) for a in range(n)]
    sends, recvs = [], []
    for k in range(1, N_DEV):
        px, py, pc = (x + (k >> 2)) % 2, (y + ((k >> 1) & 1)) % 2, (c + (k & 1)) % 2
        peer = 4 * px + 2 * py + pc
        for a in range(n):
            sems = dict(send_sem=send_sems.at[7 * a + k - 1], recv_sem=recv_sems.at[7 * a + k - 1],
                        device_id=(px, py, pc), device_id_type=MESH)
            sends.append(pltpu.make_async_remote_copy(src_ref=src(a, peer), dst_ref=land_refs[a].at[me], **sems))
            recvs.append(pltpu.make_async_remote_copy(src_ref=src(a, me), dst_ref=land_refs[a].at[peer], **sems))
    return mine, sends, recvs


def _exchange_start(parts, *, gather, name):
    n = len(parts)

    def body(*refs):
        x_refs, land_refs = refs[:n], refs[n:2 * n]
        send_sems, recv_sems, local_sems = refs[2 * n:2 * n + 3]
        token = refs[-1]
        mine, sends, _ = _exchange_copies(gather, x_refs, land_refs, send_sems, recv_sems, local_sems)
        for cp in mine + sends:
            cp.start()
        token[...] = jnp.zeros_like(token)

    sems = (pltpu.SemaphoreType.DMA((7 * n,)), pltpu.SemaphoreType.DMA((7 * n,)), pltpu.SemaphoreType.DMA((n,)))
    thru = tuple(pltpu.HBM(p.shape, p.dtype) for p in parts)
    land = tuple(pltpu.HBM(((N_DEV,) if gather else ()) + p.shape, p.dtype) for p in parts)
    res = pl.pallas_call(
        body, name=name, in_specs=[HBM] * (2 * n),
        out_specs=(SEM, SEM, SEM) + (HBM,) * (2 * n) + (pl.BlockSpec(memory_space=pltpu.VMEM),),
        out_shape=sems + thru + land + (jax.ShapeDtypeStruct((8, 128), F32),),
        input_output_aliases={a: 3 + a for a in range(2 * n)},
        compiler_params=pltpu.CompilerParams(has_side_effects=DATAFLOW),
    )(*[pltpu.with_memory_space_constraint(p, pltpu.HBM) for p in parts],
      *[pltpu.with_memory_space_constraint(lax.empty(z.shape, z.dtype), pltpu.HBM) for z in land])
    return res[:3], res[3:3 + n], res[3 + n:3 + 2 * n], res[-1]


def _exchange_wait(sems, parts, landing, after, *, gather, name):
    n = len(parts)
    after = list(after)

    def body(*refs):
        x_refs, land_refs = refs[:n], refs[n:2 * n]
        send_sems, recv_sems, local_sems = refs[2 * n:2 * n + 3]
        token = refs[-1]
        mine, sends, recvs = _exchange_copies(gather, x_refs, land_refs, send_sems, recv_sems, local_sems)
        for cp in recvs:
            cp.wait_recv()
        for cp in sends:
            cp.wait_send()
        for cp in mine:
            cp.wait()
        token[...] = jnp.zeros_like(token)

    thru = tuple(pltpu.HBM(p.shape, p.dtype) for p in tuple(parts) + tuple(landing))
    res = pl.pallas_call(
        body, name=name, in_specs=[HBM] * (2 * n) + [SEM, SEM, SEM] + [ANY] * len(after),
        out_specs=(HBM,) * (2 * n) + (pl.BlockSpec(memory_space=pltpu.VMEM),),
        out_shape=thru + (jax.ShapeDtypeStruct((8, 128), F32),), input_output_aliases={a: a for a in range(2 * n)},
        compiler_params=pltpu.CompilerParams(has_side_effects=DATAFLOW),
    )(*parts, *landing, *sems, *after)
    return res[n:2 * n], res[-1]


def _relay_copies(x_refs, land_refs, first_send, first_recv, relay_send, relay_recv, local_sems):
    n = len(x_refs)
    x, y, c = _position()
    sibling = (x, y, 1 - c)
    chips = [(1 - x, y), (x, 1 - y), (1 - x, 1 - y)]

    def slot(a, px, py, pc):
        return land_refs[a].at[4 * px + 2 * py + pc]

    def hop(a, k, block, to, own=False):
        return pltpu.make_async_remote_copy(
            src_ref=x_refs[a] if own else slot(a, *block), dst_ref=slot(a, *block),
            send_sem=first_send.at[4 * a + k], recv_sem=first_recv.at[4 * a + k], device_id=to, device_id_type=MESH)

    def relay(a, j, block, to):
        return pltpu.make_async_remote_copy(
            src_ref=slot(a, *block), dst_ref=slot(a, *block),
            send_sem=relay_send.at[3 * a + j], recv_sem=relay_recv.at[3 * a + j], device_id=to, device_id_type=MESH)

    me = (x, y, c)
    mine = [pltpu.make_async_copy(x_refs[a], slot(a, *me), local_sems.at[a]) for a in range(n)]
    sends = [hop(a, 1 + j, me, (*chip, c), own=True) for j, chip in enumerate(chips) for a in range(n)]
    sends += [hop(a, 0, me, sibling, own=True) for a in range(n)]
    over_ici = [hop(a, 1 + j, (*chip, c), me) for j, chip in enumerate(chips) for a in range(n)]
    from_sibling = [hop(a, 0, sibling, me) for a in range(n)]
    if relay_send is None:
        return mine, sends, over_ici, from_sibling, [], []
    relays = [relay(a, j, (*chip, c), sibling) for j, chip in enumerate(chips) for a in range(n)]
    relayed = [relay(a, j, (*chip, 1 - c), me) for j, chip in enumerate(chips) for a in range(n)]
    return mine, sends, over_ici, from_sibling, relays, relayed


def _relay_call(body, n_sem_in, n_sem_out, shards, landing, sems_in, after, name):
    n = len(shards)
    thru = tuple(pltpu.HBM(p.shape, p.dtype) for p in tuple(shards) + tuple(landing))
    res = pl.pallas_call(
        body, name=name, in_specs=[HBM] * (2 * n) + [SEM] * n_sem_in + [ANY] * len(after),
        out_specs=(SEM,) * len(n_sem_out) + (HBM,) * (2 * n) + (pl.BlockSpec(memory_space=pltpu.VMEM),),
        out_shape=tuple(pltpu.SemaphoreType.DMA((k,)) for k in n_sem_out) + thru + (jax.ShapeDtypeStruct((8, 128), F32),),
        input_output_aliases={a: len(n_sem_out) + a for a in range(2 * n)},
        compiler_params=pltpu.CompilerParams(has_side_effects=DATAFLOW),
    )(*shards, *landing, *sems_in, *after)
    k = len(n_sem_out)
    return res[:k], res[k:k + n], res[k + n:k + 2 * n], res[-1]


def _relay_gather_start(shards, *, name):
    n = len(shards)

    def body(*refs):
        x_refs, land_refs = refs[:n], refs[n:2 * n]
        first_send, first_recv, local_sems = refs[2 * n:2 * n + 3]
        mine, sends, *_ = _relay_copies(x_refs, land_refs, first_send, first_recv, None, None, local_sems)
        for cp in mine + sends:
            cp.start()
        refs[-1][...] = jnp.zeros_like(refs[-1])

    landing = [pltpu.with_memory_space_constraint(lax.empty((N_DEV,) + s.shape, s.dtype), pltpu.HBM) for s in shards]
    shards = [pltpu.with_memory_space_constraint(s, pltpu.HBM) for s in shards]
    return _relay_call(body, 0, (4 * n, 4 * n, n), shards, landing, (), (), name)


def _relay_gather_pass_on(first, shards, landing, after, *, name):
    n = len(shards)

    def body(*refs):
        x_refs, land_refs = refs[:n], refs[n:2 * n]
        first_send, first_recv, local_sems = refs[2 * n:2 * n + 3]
        relay_send, relay_recv = refs[2 * n + 3 + len(after):2 * n + 5 + len(after)]
        _, _, over_ici, _, relays, _ = _relay_copies(x_refs, land_refs, first_send, first_recv, relay_send, relay_recv,
                                                   local_sems)
        for arrival, cp in zip(over_ici, relays):
            arrival.wait_recv()
            cp.start()
        refs[-1][...] = jnp.zeros_like(refs[-1])

    return _relay_call(body, 3, (3 * n, 3 * n), shards, landing, first, list(after), name)


def _relay_gather_wait(first, relay, shards, landing, after, *, name):
    n = len(shards)

    def body(*refs):
        x_refs, land_refs = refs[:n], refs[n:2 * n]
        first_send, first_recv, local_sems, relay_send, relay_recv = refs[2 * n:2 * n + 5]
        mine, sends, _, from_sibling, relays, relayed = _relay_copies(
            x_refs, land_refs, first_send, first_recv, relay_send, relay_recv, local_sems)
        for cp in from_sibling + relayed:
            cp.wait_recv()
        for cp in sends + relays:
            cp.wait_send()
        for cp in mine:
            cp.wait()
        refs[-1][...] = jnp.zeros_like(refs[-1])

    _, _, landing, token = _relay_call(body, 5, (), shards, landing, tuple(first) + tuple(relay), list(after), name)
    return landing, token


def _ffn_fwd(x, gain, wg_in, wg_out, tag):
    T, D = x.shape
    fb, rb = wg_in.shape[-1], wg_out.shape[-2]
    tm, tn = min(T, 1024), 512
    h = _rmsnorm_fwd(x, gain, name=f"{tag}_norm")
    p = _mm(name=f"{tag}_in", grid=(T // tm, N_DEV, 1), tile=(tm, fb),
            a=h, a_spec=pl.BlockSpec((tm, D), lambda i, j, k: (i, 0)),
            b=wg_in, b_spec=pl.BlockSpec((None, D, fb), lambda i, j, k: (j, 0, 0)),
            out_shape=jax.ShapeDtypeStruct((N_DEV, T, fb), BF16), o_spec=pl.BlockSpec((None, tm, fb), lambda i, j, k: (j, i, 0)))
    a = _swiglu_fwd(p, name=f"{tag}_act")
    y = _mm(name=f"{tag}_out", grid=(T // tm, 1, FF_HALF), tile=(tm, D), resid=x, scale=0.5,
            a=a, a_spec=pl.BlockSpec((None, tm, fb), lambda i, j, k: (k, i, 0)),
            b=wg_out.reshape(N_DEV * rb, D), b_spec=pl.BlockSpec((fb, D), lambda i, j, k: (k, 0)),
            out_shape=jax.ShapeDtypeStruct((T, D), F32), o_spec=pl.BlockSpec((tm, D), lambda i, j, k: (i, 0)))
    return y, (x, h, p, a)


def _ffn_bwd(dy, saved, gain, wg_in, wg_out, tag, on_weight_grads=None):
    x, h, p, a = saved
    T, D = x.shape
    fb, rb = wg_in.shape[-1], wg_out.shape[-2]
    tm, tn = min(T, 1024), 512
    da = _mm(name=f"{tag}_out_dx", grid=(T // tm, FF_HALF, 1), tile=(tm, fb), tb=True, scale=0.5,
             a=dy, a_spec=pl.BlockSpec((tm, D), lambda i, j, k: (i, 0)),
             b=wg_out.reshape(N_DEV * rb, D), b_spec=pl.BlockSpec((fb, D), lambda i, j, k: (j, 0)),
             out_shape=jax.ShapeDtypeStruct((FF_HALF, T, fb), BF16), o_spec=pl.BlockSpec((None, tm, fb), lambda i, j, k: (j, i, 0)))
    d_w_out = _mm(name=f"{tag}_out_dw", grid=(FF_HALF, D // tn, 1), tile=(fb, tn), ta=True, scale=0.5,
                  a=a, a_spec=pl.BlockSpec((None, T, fb), lambda i, j, k: (i, 0, 0)),
                  b=dy, b_spec=pl.BlockSpec((T, tn), lambda i, j, k: (0, j)),
                  out_shape=jax.ShapeDtypeStruct((FF_HALF, fb, D), BF16), o_spec=pl.BlockSpec((None, fb, tn), lambda i, j, k: (i, 0, j)))
    dp = _swiglu_bwd(da, p, name=f"{tag}_act_bwd")
    d_w_in = _mm(name=f"{tag}_in_dw", grid=(1, N_DEV, 1), tile=(D, fb), ta=True,
                 a=h, a_spec=pl.BlockSpec((T, D), lambda i, j, k: (0, 0)),
                 b=dp, b_spec=pl.BlockSpec((None, T, fb), lambda i, j, k: (j, 0, 0)),
                 out_shape=jax.ShapeDtypeStruct((N_DEV, D, fb), BF16), o_spec=pl.BlockSpec((None, D, fb), lambda i, j, k: (j, 0, 0)))
    dh = _mm(name=f"{tag}_in_dx", grid=(T // tm, 1, N_DEV), tile=(tm, D), tb=True,
             a=dp, a_spec=pl.BlockSpec((None, tm, fb), lambda i, j, k: (k, i, 0)),
             b=wg_in, b_spec=pl.BlockSpec((None, D, fb), lambda i, j, k: (k, 0, 0)),
             out_shape=jax.ShapeDtypeStruct((T, D), F32), o_spec=pl.BlockSpec((tm, D), lambda i, j, k: (i, 0)))
    d_w_out = d_w_out.reshape(N_DEV, rb, D)
    if on_weight_grads is not None:
        gain = gain + on_weight_grads(d_w_in, d_w_out)[0, 0]
    dx, d_gain = _rmsnorm_bwd(dh, x, gain, dy, name=f"{tag}_norm_bwd")
    return dx, d_gain, d_w_in, d_w_out


def _square_mm(a, wg, *, name, transposed=False, out_dtype=F32, resid=None):
    T, D = a.shape
    w = wg.reshape(D, D)
    return _matmul(a, w, tb=transposed, name=name, out_dtype=out_dtype, resid=resid)


def _head_rows(cols, T):
    return cols.T.reshape(HEADS, T // DN_CHUNK, 1, DN_CHUNK)


def _mixer_fwd(x, w, big, tag):
    T = x.shape[0]
    h = _rmsnorm_fwd(x, w["mix_norm"], name=f"{tag}_norm")
    proj = _matmul(h, big["w_main"], name=f"{tag}_proj")
    scal = _matmul(h, big["w_scal"], name=f"{tag}_proj_scal", tn=N_SCAL)
    qkv = _conv_fwd(proj, big["conv_w"], name=f"{tag}_conv")
    b_rows = _head_rows(scal[:, 0:HEADS], T)
    a_rows = _head_rows(scal[:, HEADS:2 * HEADS], T)
    o_a, *states = _dn_fwd(qkv, b_rows, a_rows, w["hp"], name=f"{tag}_dn")
    oa_n = _gated_norm_fwd(o_a, proj, w["dn_out_norm"], name=f"{tag}_dn_norm")
    ya = _square_mm(oa_n, big["w_branch_a"], name=f"{tag}_branch_a")
    o_b, ltot = _sb_fwd(proj, w["sb_q_norm"], w["sb_k_norm"], name=f"{tag}_sb")
    yb = _square_mm(o_b, big["w_branch_b"], name=f"{tag}_branch_b")
    merged = _merge_fwd(ya, yb, proj, name=f"{tag}_merge")
    y = _square_mm(merged, big["w_out"], name=f"{tag}_out", resid=x)
    return y, (x, h, proj, qkv, b_rows, a_rows, o_a, states, oa_n, ya, o_b, ltot, yb, merged)


def _mixer_bwd(dy, saved, w, big, tag, on_weight_grads):
    x, h, proj, qkv, b_rows, a_rows, o_a, states, oa_n, ya, o_b, ltot, yb, merged = saved
    T = x.shape[0]
    g = {}
    d_merged = _square_mm(dy, big["w_out"], transposed=True, name=f"{tag}_out_dx", out_dtype=BF16)
    g["w_out"] = _matmul(merged, dy, ta=True, name=f"{tag}_out_dw", out_dtype=BF16)
    d_ya, d_yb, d_ga, d_gb = _merge_bwd(d_merged, ya, yb, proj, name=f"{tag}_merge_bwd")
    d_oan = _square_mm(d_ya, big["w_branch_a"], transposed=True, name=f"{tag}_branch_a_dx")
    g["w_branch_a"] = _matmul(oa_n, d_ya, ta=True, name=f"{tag}_branch_a_dw", out_dtype=BF16)
    d_ob = _square_mm(d_yb, big["w_branch_b"], transposed=True, name=f"{tag}_branch_b_dx")
    g["w_branch_b"] = _matmul(o_b, d_yb, ta=True, name=f"{tag}_branch_b_dw", out_dtype=BF16)
    d_oa, d_z, g["dn_out_norm"] = _gated_norm_bwd(d_oan, o_a, proj, w["dn_out_norm"], name=f"{tag}_dn_norm_bwd")
    d_qkv, d_b_rows, d_a_rows, d_hp = _dn_bwd(qkv, b_rows, a_rows, w["hp"], *states, d_oa, name=f"{tag}_dn_bwd")
    g["dn_a_log"] = d_hp[:, 0, 0]
    g["dn_dt_bias"] = d_hp[:, 1, 0]
    d_conv_in, g["conv_w"] = _conv_bwd(d_qkv, proj, big["conv_w"], name=f"{tag}_conv_bwd")
    d_sbq, d_sbk, d_sbv, g["sb_q_norm"], g["sb_k_norm"] = _sb_bwd(
        proj, w["sb_q_norm"], w["sb_k_norm"], ltot, d_ob, name=f"{tag}_sb_bwd")
    d_proj = jnp.concatenate([d_conv_in, d_z, d_sbq, d_sbk, d_sbv, d_ga, d_gb], axis=1)
    d_scal = jnp.concatenate([d_b_rows.reshape(HEADS, T).T, d_a_rows.reshape(HEADS, T).T,
                              jnp.zeros((T, N_SCAL - 2 * HEADS), F32)], axis=1).astype(BF16)
    g["w_main"] = _matmul(h, d_proj, ta=True, name=f"{tag}_proj_dw", out_dtype=BF16)
    g["w_scal"] = _matmul(h, d_scal, ta=True, name=f"{tag}_proj_scal_dw", out_dtype=BF16, tn=N_SCAL)
    dh_scal = _matmul(d_scal, big["w_scal"], tb=True, name=f"{tag}_proj_scal_dx")
    dh = _matmul(d_proj, big["w_main"], tb=True, name=f"{tag}_proj_dx", tk=N_MAIN // 4, resid=dh_scal)
    gain = w["mix_norm"] + on_weight_grads(g)[0, 0]
    dx, g["mix_norm"] = _rmsnorm_bwd(dh, x, gain, dy, name=f"{tag}_norm_bwd")
    return dx, g


def _local_step(x, target, layers, weights_of, on_weight_grads):
    saved, bigs = [], []
    for l, w in enumerate(layers):
        big = weights_of(l, 0, x)
        x, s1 = _ffn_fwd(x, w["ffn1_norm"] + big["issued"], big["ffn1_w_in"], big["ffn1_w_out"], f"l{l}_ffn1")
        big.update(weights_of(l, 1, x))
        x, s2 = _mixer_fwd(x, dict(w, mix_norm=w["mix_norm"] + big["issued"]), big, f"l{l}_mix")
        big.update(weights_of(l, 2, x))
        x, s3 = _ffn_fwd(x, w["ffn2_norm"] + big["issued"], big["ffn2_w_in"], big["ffn2_w_out"], f"l{l}_ffn2")
        saved.append((s1, s2, s3))
        bigs.append(big)
    loss, dx = _loss_head(x, target, name="loss_head")
    small = [None] * len(layers)
    for l in reversed(range(len(layers))):
        w, big = layers[l], bigs[l]
        s1, s2, s3 = saved[l]
        dx, g_n2, _, _ = _ffn_bwd(
            dx, s3, w["ffn2_norm"], big["ffn2_w_in"], big["ffn2_w_out"], f"l{l}_ffn2",
            on_weight_grads=lambda g_in, g_out, l=l: on_weight_grads(l, 0, dict(ffn2_w_in=g_in, ffn2_w_out=g_out)))
        dx, g = _mixer_bwd(dx, s2, w, big, f"l{l}_mix", on_weight_grads=lambda g, l=l: on_weight_grads(l, 1, g))
        dx, g_n1, _, _ = _ffn_bwd(
            dx, s1, w["ffn1_norm"], big["ffn1_w_in"], big["ffn1_w_out"], f"l{l}_ffn1",
            on_weight_grads=lambda g_in, g_out, l=l: on_weight_grads(l, 2, dict(ffn1_w_in=g_in, ffn1_w_out=g_out)))
        small[l] = dict(g, ffn1_norm=g_n1, ffn2_norm=g_n2)
    return loss, dx, small


_BIG = ("ffn1_w_in", "ffn1_w_out", "w_in", "w_branch_a", "w_branch_b", "w_out", "ffn2_w_in", "ffn2_w_out")
_STAGES = (("ffn2_w_in", "ffn2_w_out"), ("w_in", "w_branch_a", "w_branch_b", "w_out"), ("ffn1_w_in", "ffn1_w_out"))
_SMALL = ("ffn1_norm", "mix_norm", "ffn2_norm", "dn_a_log", "dn_dt_bias", "dn_out_norm", "sb_q_norm", "sb_k_norm")
_ORDER = ("ffn1_norm", "ffn1_w_in", "ffn1_w_out", "mix_norm", "w_in", "dn_conv_w", "dn_a_log", "dn_dt_bias", "dn_out_norm",
          "sb_q_norm", "sb_k_norm", "w_branch_a", "w_branch_b", "w_out", "ffn2_norm", "ffn2_w_in", "ffn2_w_out")
COL_SCAL = 4 * D_MODEL
SCAL_SLOT = COL_SCAL // (N_IN // N_DEV)
SCAL_AT = COL_SCAL % (N_IN // N_DEV)
assert SCAL_AT + 2 * HEADS <= N_IN // N_DEV


def _pad_rows(a, multiple):
    pad = (-a.shape[-2]) % multiple
    return a if pad == 0 else jnp.pad(a, [(0, 0)] * (a.ndim - 2) + [(0, pad), (0, 0)])


def _lane_rows(a):
    flat = a.reshape(-1)
    flat = jnp.pad(flat, (0, (-flat.shape[0]) % 128))
    return flat.reshape(-1, 128)


def _pack_small(named):
    pieces, spans, r = [], {}, 0
    for n, a in named:
        rows = _lane_rows(a)
        spans[n] = (r, r + rows.shape[0], a.shape)
        r += rows.shape[0]
        pieces.append(rows)
    return _pad_rows(jnp.concatenate(pieces, axis=0), 8), spans


def _unpack_small(packed, spans, n):
    r0, r1, shape = spans[n]
    return packed[r0:r1].reshape(-1)[:math.prod(shape)].reshape(shape)


def kernel(x, ffn1_norm, ffn1_w_in, ffn1_w_out, mix_norm, w_in, dn_conv_w, dn_a_log, dn_dt_bias, dn_out_norm, sb_q_norm, sb_k_norm, w_branch_a, w_branch_b, w_out, ffn2_norm, ffn2_w_in, ffn2_w_out, loss_target, m_ffn1_norm, m_ffn1_w_in, m_ffn1_w_out, m_mix_norm, m_w_in, m_dn_conv_w, m_dn_a_log, m_dn_dt_bias, m_dn_out_norm, m_sb_q_norm, m_sb_k_norm, m_w_branch_a, m_w_branch_b, m_w_out, m_ffn2_norm, m_ffn2_w_in, m_ffn2_w_out, v_ffn1_norm, v_ffn1_w_in, v_ffn1_w_out, v_mix_norm, v_w_in, v_dn_conv_w, v_dn_a_log, v_dn_dt_bias, v_dn_out_norm, v_sb_q_norm, v_sb_k_norm, v_w_branch_a, v_w_branch_b, v_w_out, v_ffn2_norm, v_ffn2_w_in, v_ffn2_w_out):
    given = dict(locals())
    weights = {n: given[n] for n in _ORDER}
    mom_m = {n: given["m_" + n] for n in _ORDER}
    mom_v = {n: given["v_" + n] for n in _ORDER}
    L = ffn1_norm.shape[0]
    ax, ay, ac = _position()
    my_slot = 4 * ax + 2 * ay + ac

    conv_cols = dn_conv_w.shape[-1]
    first_ffn = _STAGES[2]
    later = tuple(n for n in _BIG if n not in first_ffn)
    gathers, landed = {}, {}

    def start_gather(key, names, l, zero, extra=()):
        shards = [(weights[n][l] + zero).astype(BF16) for n in names] + list(extra)
        first, shards, landing, token = _relay_gather_start(shards, name=f"gather_start_{key}")
        gathers[key] = dict(first=first, shards=shards, landing=landing, names=names)
        return token[0, 0]

    def pass_on(key, after):
        g = gathers[key]
        g["relay"], g["shards"], g["landing"], token = _relay_gather_pass_on(
            g["first"], g["shards"], g["landing"], [after], name=f"gather_pass_on_{key}")
        return token[0, 0]

    def wait_gather(key, after):
        g = gathers.pop(key)
        arrays, token = _relay_gather_wait(g["first"], g["relay"], g["shards"], g["landing"], [after],
                                           name=f"gather_wait_{key}")
        landed.update(zip(g["names"], arrays))
        if len(arrays) > len(g["names"]):
            landed["conv"] = arrays[-1]
        return token[0, 0]

    def weights_of(l, part, x_in):
        if l == 0 and part == 0:
            start_gather("l0_ffn1", first_ffn, 0, 0.0)
            pass_on("l0_ffn1", x_in)
            issued = start_gather("l0", later, 0, wait_gather("l0_ffn1", x_in), [_pad_rows(_lane_rows(dn_conv_w), 8)])
            return dict({n: landed.pop(n) for n in first_ffn}, issued=issued)
        if part == 0:
            token = wait_gather(f"l{l}", x_in)
            issued = start_gather(f"l{l + 1}", _BIG, l + 1, token) if l + 1 < L else token
            return dict({n: landed.pop(n) for n in first_ffn}, issued=issued)
        if part == 2:
            return dict(issued=pass_on(f"l{l + 1}", x_in) if l + 1 < L else 0.0)
        issued = 0.0
        if l == 0:
            pass_on("l0", x_in)
            issued = start_gather("l1", _BIG, 1, wait_gather("l0", x_in)) if L > 1 else 0.0
            conv = landed.pop("conv").reshape(N_DEV, -1)[:, :L * DN_CONV * conv_cols]
            landed["conv_w"] = conv.reshape(N_DEV, L, DN_CONV, conv_cols).transpose(1, 2, 0, 3).reshape(
                L, DN_CONV, N_DEV * conv_cols)
        big = {n: landed.pop(n) for n in later}
        wi = big.pop("w_in")
        pieces = [wi[d] for d in range(N_DEV)]
        pieces[SCAL_SLOT:SCAL_SLOT + 1] = [wi[SCAL_SLOT][:, :SCAL_AT], wi[SCAL_SLOT][:, SCAL_AT + 2 * HEADS:]]
        big["w_main"] = jnp.concatenate(pieces, axis=1)
        big["w_scal"] = jnp.pad(wi[SCAL_SLOT][:, SCAL_AT:SCAL_AT + 2 * HEADS], ((0, 0), (0, N_SCAL - 2 * HEADS)))
        big["conv_w"] = landed["conv_w"][l]
        return dict(big, issued=issued)

    layers = []
    for l in range(L):
        hp = jnp.concatenate([jnp.broadcast_to(dn_a_log[l][:, None, None], (HEADS, 1, 128)),
                              jnp.broadcast_to(dn_dt_bias[l][:, None, None], (HEADS, 1, 128)),
                              jnp.zeros((HEADS, 6, 128), F32)], axis=1)
        layers.append(dict(ffn1_norm=ffn1_norm[l][None], mix_norm=mix_norm[l][None], hp=hp,
                           dn_out_norm=dn_out_norm[l][None], sb_q_norm=sb_q_norm[l][None],
                           sb_k_norm=sb_k_norm[l][None], ffn2_norm=ffn2_norm[l][None]))

    in_flight = {}

    def on_weight_grads(l, stage, g):
        parts = dict(g)
        if stage == 1:
            gm, shard = g["w_main"], N_IN // N_DEV
            blocks = [gm[:, d * shard:(d + 1) * shard] for d in range(SCAL_SLOT)]
            blocks.append(jnp.concatenate([gm[:, SCAL_SLOT * shard:COL_SCAL], g["w_scal"][:, :2 * HEADS],
                                           gm[:, COL_SCAL:(SCAL_SLOT + 1) * shard - 2 * HEADS]], axis=1))
            blocks += [gm[:, d * shard - 2 * HEADS:(d + 1) * shard - 2 * HEADS] for d in range(SCAL_SLOT + 1, N_DEV)]
            parts["w_in"] = jnp.stack(blocks)
            for n in ("w_branch_a", "w_branch_b", "w_out"):
                parts[n] = g[n].reshape(N_DEV, D_MODEL // N_DEV, D_MODEL)
        *in_flight[l, stage], token = _exchange_start([parts[n] for n in _STAGES[stage]], gather=False,
                                                      name=f"scatter_start_l{l}_{stage}")
        return token

    loss_row, dx, grads = _local_step(x[0], loss_target[0], layers, weights_of, on_weight_grads)
    loss = lax.psum(loss_row[0, 0], ("x", "y", "c"))

    results = {n: None for n in _BIG}
    after = [dx]
    for l in reversed(range(L)):
        for stage, names in enumerate(_STAGES):
            landed, all_landed = _exchange_wait(*in_flight[l, stage], after, gather=False,
                                                name=f"scatter_wait_l{l}_{stage}")
            for n, parts in zip(names, landed):
                _, a, b = weights[n].shape
                results[n] = _adamw(parts, weights[n].reshape(L * a, b), mom_m[n].reshape(L * a, b),
                                    mom_v[n].reshape(L * a, b), layer=l, earlier=results[n], name=f"adamw_{n}_l{l}")
            after = [results[n][0] for n in names]
    out = {n: tuple(t.reshape(weights[n].shape) for t in results[n]) for n in _BIG}

    small_grads = [(n, jnp.stack([g[n].reshape(weights[n].shape[1:]) for g in grads])) for n in _SMALL]
    small_packed, spans = _pack_small(small_grads + [("conv", jnp.stack([g["conv_w"] for g in grads]))])
    small_packed = small_packed + all_landed[0, 0]
    small_sum = _sum_parts(_all_gather([small_packed], name="gather_small_grads")[0], name="sum_small_grads")
    rep_rows = spans["conv"][0]
    pack_rep = lambda d: _pad_rows(_pack_small([(n, d[n]) for n in _SMALL])[0], 8)
    rep_pad = (-rep_rows) % 8
    g_rep = jnp.pad(small_sum[:rep_rows], ((0, rep_pad), (0, 0)))
    res = _adamw(g_rep[None], pack_rep(weights), pack_rep(mom_m), pack_rep(mom_v), name="adamw_replicated")
    for n in _SMALL:
        out[n] = tuple(_unpack_small(t, spans, n) for t in res)
    conv_sum = _unpack_small(small_sum, spans, "conv")
    conv_mine = lax.dynamic_slice_in_dim(conv_sum, my_slot * conv_cols, conv_cols, axis=2).reshape(L * DN_CONV, conv_cols)
    flat = lambda t: t.reshape(L * DN_CONV, conv_cols)
    res = _adamw(conv_mine[None], flat(dn_conv_w), flat(m_dn_conv_w), flat(v_dn_conv_w), name="adamw_conv")
    out["dn_conv_w"] = tuple(t.reshape(L, DN_CONV, conv_cols) for t in res)

    return (loss, dx[None], *[out[n][0] for n in _ORDER], *[out[n][1] for n in _ORDER],
            *[out[n][2] for n in _ORDER], *[out[n][3] for n in _ORDER])
```

```python
import functools
import math

import jax
import jax.numpy as jnp
from jax import lax
from jax.experimental import pallas as pl
from jax.experimental.pallas import tpu as pltpu

F32 = jnp.float32
BF16 = jnp.bfloat16

N_DEV = 8
D_MODEL = 1024
DEPTH = 4
D_FF = 2816
HEADS = 8
HEAD_DIM = 128
DN_CHUNK = 64
DN_CONV = 4
DN_GROUP = 8
DN_HEADS = 2
SB_BLOCK = 128
SB_KEY_TILE = 512
SB_HEADS = 4
SB_HEADS_BWD = 2
SB_KEY_TILE_BWD = 512
RMS_EPS = 1e-6
L2_EPS = 1e-6
N_IN = 9232
N_MAIN = 9216
N_SCAL = 128
QK_SCALE = HEAD_DIM ** -0.5

ADAM_LR = 0.001
ADAM_B1 = 0.9
ADAM_B2 = 0.999
ADAM_EPS = 1e-08
ADAM_WD = 0.01
ADAM_STEP = 10

V7X_VMEM_LIMIT = 56 * 1024 * 1024
MESH = pl.DeviceIdType.MESH
ANY = pl.BlockSpec(memory_space=pl.ANY)


def _params(sem=None, vmem=V7X_VMEM_LIMIT):
    return pltpu.CompilerParams(dimension_semantics=sem, vmem_limit_bytes=vmem)


def _sigmoid(x):
    return 1.0 / (1.0 + jnp.exp(-x))


SOFTPLUS_LINEAR = 30.0


def _softplus(x):
    return jnp.maximum(x, jnp.log(1.0 + jnp.exp(jnp.minimum(x, SOFTPLUS_LINEAR))))


def _bdot(a, b, dims=(((1,), (0,)), ((), ()))):
    return lax.dot_general(a.astype(BF16), b.astype(BF16), dims, preferred_element_type=F32)


_NT = (((1,), (1,)), ((), ()))
_TN = (((0,), (0,)), ((), ()))


def _hdot(a, b, dims=(((1,), (0,)), ((), ()))):
    a_hi = a.astype(BF16)
    b_hi = b.astype(BF16)
    a_lo = (a - a_hi.astype(F32)).astype(BF16)
    b_lo = (b - b_hi.astype(F32)).astype(BF16)
    dot = functools.partial(lax.dot_general, dimension_numbers=dims, preferred_element_type=F32)
    return dot(a_hi, b_hi) + (dot(a_hi, b_lo) + dot(a_lo, b_hi))


def _hdot_tn(a, b):
    return _hdot(a, b, _TN)


def _mm(*, name, grid, a, a_spec, b, b_spec, out_shape, o_spec, tile, ta=False, tb=False, resid=None, scale=1.0):
    nk = grid[2]
    dims = (((0 if ta else 1,), (1 if tb else 0,)), ((), ()))

    def flat(v):
        return v if v.ndim == 2 else v.reshape(-1, v.shape[-1])

    def body(*refs):
        a_ref, b_ref = refs[:2]
        r_ref = refs[2] if resid is not None else None
        o_ref = refs[3] if resid is not None else refs[2]
        part = lax.dot_general(flat(a_ref[...]).astype(BF16), flat(b_ref[...]).astype(BF16), dims,
                               preferred_element_type=F32)

        def finish(acc):
            if scale != 1.0:
                acc = acc * scale
            if r_ref is not None:
                acc = r_ref[...] + acc
            o_ref[...] = acc.astype(o_ref.dtype)

        if nk == 1:
            finish(part)
        else:
            acc_ref = refs[-1]
            k = pl.program_id(2)

            @pl.when(k == 0)
            def _():
                acc_ref[...] = part

            @pl.when(k > 0)
            def _():
                acc_ref[...] += part

            @pl.when(k == nk - 1)
            def _():
                finish(acc_ref[...])

    in_specs = [a_spec, b_spec] + ([pl.BlockSpec(tile, lambda i, j, k: (i, j))] if resid is not None else [])
    args = (a, b) + ((resid,) if resid is not None else ())
    return pl.pallas_call(
        body, name=name, grid=grid, in_specs=in_specs, out_specs=o_spec, out_shape=out_shape,
        scratch_shapes=[pltpu.VMEM(tile, F32)] if nk > 1 else [],
        compiler_params=_params(("parallel", "parallel", "arbitrary")),
    )(*args)


def _matmul(a, b, *, name, ta=False, tb=False, out_dtype=F32, tm=None, tn=None, tk=None, resid=None, scale=1.0):
    if ta:
        K, M = a.shape
    else:
        M, K = a.shape
    N = b.shape[0] if tb else b.shape[1]
    tm = tm or min(M, 1024)
    tn = tn or min(N, 512)
    tk = tk or K
    assert M % tm == 0 and N % tn == 0 and K % tk == 0, (name, M, N, K, tm, tn, tk)
    a_spec = pl.BlockSpec((tk, tm), lambda i, j, k: (k, i)) if ta else pl.BlockSpec((tm, tk), lambda i, j, k: (i, k))
    b_spec = pl.BlockSpec((tn, tk), lambda i, j, k: (j, k)) if tb else pl.BlockSpec((tk, tn), lambda i, j, k: (k, j))
    return _mm(name=name, grid=(M // tm, N // tn, K // tk), a=a, a_spec=a_spec, b=b, b_spec=b_spec,
               out_shape=jax.ShapeDtypeStruct((M, N), out_dtype), o_spec=pl.BlockSpec((tm, tn), lambda i, j, k: (i, j)),
               tile=(tm, tn), ta=ta, tb=tb, resid=resid, scale=scale)


ROW_TILE = 256


def _rmsnorm_fwd(x, gain, *, name):
    T, D = x.shape

    def body(x_ref, g_ref, o_ref):
        xf = x_ref[...]
        r = lax.rsqrt(jnp.mean(xf * xf, axis=-1, keepdims=True) + RMS_EPS)
        o_ref[...] = (xf * r * g_ref[...]).astype(o_ref.dtype)

    return pl.pallas_call(
        body, name=name, grid=(T // ROW_TILE,),
        in_specs=[pl.BlockSpec((ROW_TILE, D), lambda i: (i, 0)), pl.BlockSpec((1, D), lambda i: (0, 0))],
        out_specs=pl.BlockSpec((ROW_TILE, D), lambda i: (i, 0)),
        out_shape=jax.ShapeDtypeStruct((T, D), BF16), compiler_params=_params(("parallel",)),
    )(x, gain)


def _rmsnorm_bwd(dh, x, gain, dres, *, name):
    T, D = x.shape

    def body(dh_ref, x_ref, g_ref, res_ref, dx_ref, dg_ref):
        xf = x_ref[...]
        r = lax.rsqrt(jnp.mean(xf * xf, axis=-1, keepdims=True) + RMS_EPS)
        y = xf * r
        dh_v = dh_ref[...].astype(F32)
        dy = dh_v * g_ref[...]
        dx_ref[...] = res_ref[...] + r * (dy - y * jnp.mean(dy * y, axis=-1, keepdims=True))

        @pl.when(pl.program_id(0) == 0)
        def _():
            dg_ref[...] = jnp.zeros_like(dg_ref)

        dg_ref[...] += jnp.sum(dh_v * y, axis=0, keepdims=True)

    row = pl.BlockSpec((ROW_TILE, D), lambda i: (i, 0))
    vec = pl.BlockSpec((1, D), lambda i: (0, 0))
    return pl.pallas_call(
        body, name=name, grid=(T // ROW_TILE,), in_specs=[row, row, vec, row], out_specs=(row, vec),
        out_shape=(jax.ShapeDtypeStruct((T, D), F32), jax.ShapeDtypeStruct((1, D), F32)),
        compiler_params=_params(("arbitrary",)),
    )(dh, x, gain, dres)


FF_HALF = N_DEV // 2


def _swiglu_fwd(p, *, name):
    _, T, fb = p.shape

    def body(g_ref, u_ref, o_ref):
        g = g_ref[...].astype(F32)
        o_ref[...] = (g * _sigmoid(g) * u_ref[...].astype(F32)).astype(o_ref.dtype)

    blk = (None, ROW_TILE, fb)
    return pl.pallas_call(
        body, name=name, grid=(T // ROW_TILE, FF_HALF),
        in_specs=[pl.BlockSpec(blk, lambda i, j: (j, i, 0)), pl.BlockSpec(blk, lambda i, j: (j + FF_HALF, i, 0))],
        out_specs=pl.BlockSpec(blk, lambda i, j: (j, i, 0)),
        out_shape=jax.ShapeDtypeStruct((FF_HALF, T, fb), BF16), compiler_params=_params(("parallel", "parallel")),
    )(p, p)


def _swiglu_bwd(da, p, *, name):
    _, T, fb = p.shape

    def body(da_ref, g_ref, u_ref, o_ref):
        g = g_ref[...].astype(F32)
        u = u_ref[...].astype(F32)
        d = da_ref[...].astype(F32)
        s = _sigmoid(g)
        o_ref[0] = (d * u * (s * (1.0 + g * (1.0 - s)))).astype(o_ref.dtype)
        o_ref[1] = (d * g * s).astype(o_ref.dtype)

    blk = (None, ROW_TILE, fb)
    out = pl.pallas_call(
        body, name=name, grid=(T // ROW_TILE, FF_HALF),
        in_specs=[pl.BlockSpec(blk, lambda i, j: (j, i, 0)), pl.BlockSpec(blk, lambda i, j: (j, i, 0)),
                  pl.BlockSpec(blk, lambda i, j: (j + FF_HALF, i, 0))],
        out_specs=pl.BlockSpec((2, None, ROW_TILE, fb), lambda i, j: (0, j, i, 0)),
        out_shape=jax.ShapeDtypeStruct((2, FF_HALF, T, fb), BF16), compiler_params=_params(("parallel", "parallel")),
    )(da, p, p)
    return out.reshape(2 * FF_HALF, T, fb)


COL_GATE_A = 7
COL_GATE_B = 8


def _merge_fwd(ya, yb, proj, *, name):
    T, D = ya.shape

    def body(ya_ref, yb_ref, ga_ref, gb_ref, o_ref):
        o_ref[...] = (_sigmoid(ga_ref[...]) * ya_ref[...] + _sigmoid(gb_ref[...]) * yb_ref[...]).astype(o_ref.dtype)

    row = pl.BlockSpec((ROW_TILE, D), lambda i: (i, 0))
    return pl.pallas_call(
        body, name=name, grid=(T // ROW_TILE,),
        in_specs=[row, row, pl.BlockSpec((ROW_TILE, D), lambda i: (i, COL_GATE_A)),
                  pl.BlockSpec((ROW_TILE, D), lambda i: (i, COL_GATE_B))],
        out_specs=row, out_shape=jax.ShapeDtypeStruct((T, D), BF16), compiler_params=_params(("parallel",)),
    )(ya, yb, proj, proj)


def _merge_bwd(dm, ya, yb, proj, *, name):
    T, D = ya.shape

    def body(dm_ref, ya_ref, yb_ref, ga_ref, gb_ref, dya_ref, dyb_ref, dga_ref, dgb_ref):
        d = dm_ref[...].astype(F32)
        sa = _sigmoid(ga_ref[...])
        sb = _sigmoid(gb_ref[...])
        dya_ref[...] = (d * sa).astype(BF16)
        dyb_ref[...] = (d * sb).astype(BF16)
        dga_ref[...] = (d * ya_ref[...] * sa * (1.0 - sa)).astype(BF16)
        dgb_ref[...] = (d * yb_ref[...] * sb * (1.0 - sb)).astype(BF16)

    row = pl.BlockSpec((ROW_TILE, D), lambda i: (i, 0))
    out = jax.ShapeDtypeStruct((T, D), BF16)
    return pl.pallas_call(
        body, name=name, grid=(T // ROW_TILE,),
        in_specs=[row, row, row, pl.BlockSpec((ROW_TILE, D), lambda i: (i, COL_GATE_A)),
                  pl.BlockSpec((ROW_TILE, D), lambda i: (i, COL_GATE_B))],
        out_specs=(row, row, row, row), out_shape=(out, out, out, out), compiler_params=_params(("parallel",)),
    )(dm, ya, yb, proj, proj)


def _loss_head(y, target, *, name):
    T, D = y.shape

    def body(y_ref, t_ref, loss_ref, dy_ref):
        err = y_ref[...] - t_ref[...]
        dy_ref[...] = err * (1.0 / D)

        @pl.when(pl.program_id(0) == 0)
        def _():
            loss_ref[...] = jnp.zeros_like(loss_ref)

        loss_ref[...] += 0.5 * jnp.sum(jnp.sum(err * err, axis=-1, keepdims=True) * (1.0 / D), axis=0, keepdims=True)

    row = pl.BlockSpec((ROW_TILE, D), lambda i: (i, 0))
    return pl.pallas_call(
        body, name=name, grid=(T // ROW_TILE,), in_specs=[row, row],
        out_specs=(pl.BlockSpec((1, 128), lambda i: (0, 0)), row),
        out_shape=(jax.ShapeDtypeStruct((1, 128), F32), jax.ShapeDtypeStruct((T, D), F32)),
        compiler_params=_params(("arbitrary",)),
    )(y, target)


CONV_PAD = 8


def _conv_taps(w, xp, T, first):
    acc = w[0:1, :] * xp[pl.ds(first, T), :]
    for i in range(1, DN_CONV):
        acc = acc + w[i:i + 1, :] * xp[pl.ds(first + i, T), :]
    return acc


def _conv_fwd(proj, conv_w, *, name):
    T = proj.shape[0]

    def body(x_ref, w_ref, o_ref, xp):
        xp[0:CONV_PAD, :] = jnp.zeros((CONV_PAD, HEAD_DIM), F32)
        xp[CONV_PAD:, :] = x_ref[...]
        y = _conv_taps(w_ref[...], xp, T, CONV_PAD - (DN_CONV - 1))
        s = y * _sigmoid(y)
        n = s * lax.rsqrt(jnp.sum(s * s, axis=-1, keepdims=True) + L2_EPS)
        o_ref[0] = jnp.where(pl.program_id(0) < 2, n, s)

    return pl.pallas_call(
        body, name=name, grid=(3, HEADS),
        in_specs=[pl.BlockSpec((T, HEAD_DIM), lambda c, h: (0, c * HEADS + h)),
                  pl.BlockSpec((DN_CONV, HEAD_DIM), lambda c, h: (0, c * HEADS + h))],
        out_specs=pl.BlockSpec((1, T, HEAD_DIM), lambda c, h: (c, 0, h)),
        out_shape=jax.ShapeDtypeStruct((3, T, D_MODEL), F32),
        scratch_shapes=[pltpu.VMEM((T + CONV_PAD, HEAD_DIM), F32)],
        compiler_params=_params(("parallel", "parallel")),
    )(proj, conv_w)


def _conv_bwd(dqkv, proj, conv_w, *, name):
    T = proj.shape[0]

    def body(d_ref, x_ref, w_ref, dx_ref, dw_ref, xp, dyp):
        xp[0:CONV_PAD, :] = jnp.zeros((CONV_PAD, HEAD_DIM), F32)
        xp[CONV_PAD:, :] = x_ref[...]
        w = w_ref[...]
        y = _conv_taps(w, xp, T, CONV_PAD - (DN_CONV - 1))
        sg = _sigmoid(y)
        s = y * sg
        r = lax.rsqrt(jnp.sum(s * s, axis=-1, keepdims=True) + L2_EPS)
        n = s * r
        d = d_ref[0]
        ds = jnp.where(pl.program_id(0) < 2, r * (d - n * jnp.sum(d * n, axis=-1, keepdims=True)), d)
        dy = ds * (sg * (1.0 + y * (1.0 - sg)))
        dyp[0:T, :] = dy
        dyp[T:, :] = jnp.zeros((CONV_PAD, HEAD_DIM), F32)
        dx = w[0:1, :] * dyp[pl.ds(DN_CONV - 1, T), :]
        for i in range(1, DN_CONV):
            dx = dx + w[i:i + 1, :] * dyp[pl.ds(DN_CONV - 1 - i, T), :]
        dx_ref[...] = dx.astype(dx_ref.dtype)
        for i in range(DN_CONV):
            dw_ref[i:i + 1, :] = jnp.sum(dy * xp[pl.ds(CONV_PAD - (DN_CONV - 1) + i, T), :], axis=0, keepdims=True)

    col = lambda c, h: (0, c * HEADS + h)
    return pl.pallas_call(
        body, name=name, grid=(3, HEADS),
        in_specs=[pl.BlockSpec((1, T, HEAD_DIM), lambda c, h: (c, 0, h)), pl.BlockSpec((T, HEAD_DIM), col),
                  pl.BlockSpec((DN_CONV, HEAD_DIM), col)],
        out_specs=(pl.BlockSpec((T, HEAD_DIM), col), pl.BlockSpec((DN_CONV, HEAD_DIM), col)),
        out_shape=(jax.ShapeDtypeStruct((T, 3 * D_MODEL), BF16), jax.ShapeDtypeStruct((DN_CONV, 3 * D_MODEL), F32)),
        scratch_shapes=[pltpu.VMEM((T + CONV_PAD, HEAD_DIM), F32), pltpu.VMEM((T + CONV_PAD, HEAD_DIM), F32)],
        compiler_params=_params(("parallel", "parallel")),
    )(dqkv, proj, conv_w)


def _inv_unit_lower(low, eye):
    x = eye - low
    power = _hdot(low, low, _B_NN)
    steps = int(math.log2(DN_CHUNK)) - 1
    for s in range(steps):
        x = x + _hdot(x, power, _B_NN)
        if s + 1 < steps:
            power = _hdot(power, power, _B_NN)
    return x


_B_NN = (((2,), (1,)), ((0,), (0,)))
_B_NT = (((2,), (2,)), ((0,), (0,)))
_B_TN = (((1,), (1,)), ((0,), (0,)))


def _dn_load(ref, lead, r0, group):
    rows = pl.ds(r0, group * DN_CHUNK)
    cols = lambda h: slice(h * HEAD_DIM, (h + 1) * HEAD_DIM)
    per_head = [(ref[rows, cols(h)] if lead is None else ref[lead, rows, cols(h)]).reshape(group, DN_CHUNK, HEAD_DIM)
                for h in range(DN_HEADS)]
    return jnp.stack(per_head, axis=1).reshape(group * DN_HEADS, DN_CHUNK, HEAD_DIM)


def _dn_chunk_setup(qkv_ref, b_ref, a_ref, hp_ref, n0, group, tinv=None):
    C = DN_CHUNK
    B = group * DN_HEADS
    r0 = pl.multiple_of(n0 * C, C)
    q = _dn_load(qkv_ref, 0, r0, group) * QK_SCALE
    k = _dn_load(qkv_ref, 1, r0, group)
    v = _dn_load(qkv_ref, 2, r0, group)
    ii = lax.broadcasted_iota(jnp.int32, (B, C, C), 1)
    jj = lax.broadcasted_iota(jnp.int32, (B, C, C), 2)
    eye_mask = ii == jj
    eye = jnp.where(eye_mask, 1.0, 0.0).astype(F32)

    def to_col(row):
        return jnp.sum(jnp.where(eye_mask, jnp.broadcast_to(row, (B, C, C)), 0.0), axis=2, keepdims=True)

    def to_row(col):
        return jnp.sum(jnp.where(eye_mask, jnp.broadcast_to(col, (B, C, C)), 0.0), axis=1, keepdims=True)

    def rows(ref):
        return jnp.stack([ref[h, pl.ds(n0, group)] for h in range(DN_HEADS)], axis=1).reshape(B, 1, C)

    def per_head(row):
        return jnp.stack([hp_ref[h, row:row + 1, 0:C] for h in range(DN_HEADS)] * group, axis=0)

    b_row = rows(b_ref)
    a_row = rows(a_ref)
    a_log = per_head(0)
    dt_b = per_head(1)
    beta_row = _sigmoid(b_row)
    neg_ea = -jnp.exp(a_log)
    g_row = neg_ea * _softplus(a_row + dt_b)
    gc_col = jnp.sum(jnp.where(jj <= ii, jnp.broadcast_to(g_row, (B, C, C)), 0.0), axis=2, keepdims=True)
    gc_row = to_row(gc_col)
    g_last = jnp.sum(g_row, axis=2, keepdims=True)
    beta = to_col(beta_row)
    low_incl = ii >= jj
    decay = jnp.exp(jnp.where(low_incl, gc_col - gc_row, -jnp.inf))
    eg = jnp.exp(gc_col)
    egl = jnp.exp(g_last - gc_col)
    el = jnp.exp(g_last)
    kb = k * beta
    pmat = _bdot(kb, k, _B_NT)
    low = jnp.where(ii > jj, pmat * decay, 0.0)
    if tinv is None:
        tinv = _inv_unit_lower(low, eye)
    u = _hdot(tinv, v * beta, _B_NN)
    w = _hdot(tinv, kb * eg, _B_NN)
    qk = _bdot(q, k, _B_NT)
    attn = qk * decay
    return dict(q=q, k=k, v=v, ii=ii, jj=jj, to_col=to_col, to_row=to_row, b_row=b_row, a_row=a_row, dt_b=dt_b,
                beta_row=beta_row, neg_ea=neg_ea, g_row=g_row, gc_col=gc_col, g_last=g_last, beta=beta,
                decay=decay, eg=eg, egl=egl, el=el, kb=kb, pmat=pmat, tinv=tinv, u=u, w=w, qk=qk, attn=attn,
                qd=q * eg, kd=k * egl, r0=r0)


def _dn_store(ref, lead, r0, group, value):
    value = value.reshape(group, DN_HEADS, DN_CHUNK, HEAD_DIM)
    for h in range(DN_HEADS):
        block = value[:, h].reshape(group * DN_CHUNK, HEAD_DIM)
        if lead is None:
            ref[pl.ds(r0, group * DN_CHUNK), h * HEAD_DIM:(h + 1) * HEAD_DIM] = block
        else:
            ref[lead, pl.ds(r0, group * DN_CHUNK), h * HEAD_DIM:(h + 1) * HEAD_DIM] = block


def _dn_specs(T):
    nc = T // DN_CHUNK
    qkv = pl.BlockSpec((3, T, DN_HEADS * HEAD_DIM), lambda h: (0, 0, h))
    rows = pl.BlockSpec((DN_HEADS, nc, 1, DN_CHUNK), lambda h: (h, 0, 0, 0))
    hp = pl.BlockSpec((DN_HEADS, 8, 128), lambda h: (h, 0, 0))
    states = pl.BlockSpec((DN_HEADS, nc, HEAD_DIM, HEAD_DIM), lambda h: (h, 0, 0, 0))
    return nc, qkv, rows, hp, states


def _dn_inverse_spec(T):
    return pl.BlockSpec((DN_HEADS, T // DN_CHUNK, DN_CHUNK, DN_CHUNK), lambda h: (h, 0, 0, 0))


def _dn_per_head(ref, n0, group):
    stacked = jnp.stack([ref[h, pl.ds(n0, group)] for h in range(DN_HEADS)], axis=1)
    return stacked.reshape((group * DN_HEADS,) + stacked.shape[2:])


def _dn_fwd(qkv, b_rows, a_rows, hp, *, name):
    T = qkv.shape[1]
    nc, qkv_spec, row_spec, hp_spec, st_spec = _dn_specs(T)
    group = math.gcd(nc, DN_GROUP)
    H = DN_HEADS

    def body(qkv_ref, b_ref, a_ref, hp_ref, o_ref, st_ref, inv_ref, s_scr):
        s_scr[...] = jnp.zeros_like(s_scr)

        def step(t, carry):
            n0 = t * group
            c = _dn_chunk_setup(qkv_ref, b_ref, a_ref, hp_ref, n0, group)
            tinv = c["tinv"].reshape(group, H, DN_CHUNK, DN_CHUNK)
            for h in range(H):
                inv_ref[h, pl.ds(n0, group)] = tinv[:, h]
            state = s_scr[...]
            outs = []
            for g in range(group):
                sl = slice(g * H, (g + 1) * H)
                for h in range(H):
                    st_ref[h, n0 + g] = state[h]
                v_new = c["u"][sl] - _bdot(c["w"][sl], state, _B_NN)
                outs.append(_bdot(c["qd"][sl], state, _B_NN) + _bdot(c["attn"][sl], v_new, _B_NN))
                state = state * c["el"][sl] + _bdot(c["kd"][sl], v_new, _B_TN)
            s_scr[...] = state
            _dn_store(o_ref, None, c["r0"], group, jnp.concatenate(outs, axis=0))
            return carry

        lax.fori_loop(0, nc // group, step, 0)

    return pl.pallas_call(
        body, name=name, grid=(HEADS // H,), in_specs=[qkv_spec, row_spec, row_spec, hp_spec],
        out_specs=(pl.BlockSpec((T, H * HEAD_DIM), lambda h: (0, h)), st_spec, _dn_inverse_spec(T)),
        out_shape=(jax.ShapeDtypeStruct((T, D_MODEL), F32),
                   jax.ShapeDtypeStruct((HEADS, nc, HEAD_DIM, HEAD_DIM), F32),
                   jax.ShapeDtypeStruct((HEADS, nc, DN_CHUNK, DN_CHUNK), F32)),
        scratch_shapes=[pltpu.VMEM((H, HEAD_DIM, HEAD_DIM), F32)], compiler_params=_params(("parallel",)),
    )(qkv, b_rows, a_rows, hp)


def _dn_bwd(qkv, b_rows, a_rows, hp, states, inverses, do, *, name):
    T = qkv.shape[1]
    C = DN_CHUNK
    nc, qkv_spec, row_spec, hp_spec, st_spec = _dn_specs(T)
    group = math.gcd(nc, DN_GROUP)
    H = DN_HEADS
    B = group * H

    def body(qkv_ref, b_ref, a_ref, hp_ref, st_ref, inv_ref, do_ref, dqkv_ref, db_ref, da_ref, dhp_ref, ds_scr, acc_scr):
        ds_scr[...] = jnp.zeros_like(ds_scr)
        acc_scr[...] = jnp.zeros_like(acc_scr)

        def step(t, carry):
            n0 = nc - (t + 1) * group
            c = _dn_chunk_setup(qkv_ref, b_ref, a_ref, hp_ref, n0, group, tinv=_dn_per_head(inv_ref, n0, group))
            state = _dn_per_head(st_ref, n0, group)
            d_o = _dn_load(do_ref, None, c["r0"], group)
            v_new = c["u"] - _bdot(c["w"], state, _B_NN)
            d_vnew_local = _bdot(c["attn"], d_o, _B_TN)
            d_state_local = _bdot(c["qd"], d_o, _B_TN)
            d_state = ds_scr[...]
            d_vnew, d_kd, d_el = [None] * group, [None] * group, [None] * group
            for g in reversed(range(group)):
                sl = slice(g * H, (g + 1) * H)
                d_vnew[g] = d_vnew_local[sl] + _bdot(c["kd"][sl], d_state, _B_NN)
                d_kd[g] = _bdot(v_new[sl], d_state, _B_NT)
                d_el[g] = jnp.sum(jnp.sum(d_state * state[sl], axis=2, keepdims=True), axis=1, keepdims=True)
                d_state = d_state * c["el"][sl] + d_state_local[sl] - _bdot(c["w"][sl], d_vnew[g], _B_TN)
            ds_scr[...] = d_state
            chunk_grads(c, n0, state, d_o, v_new, jnp.concatenate(d_vnew, axis=0), jnp.concatenate(d_kd, axis=0),
                        jnp.concatenate(d_el, axis=0))
            return carry

        def chunk_grads(c, n0, state, d_o, v_new, d_vnew, d_kd, d_el):
            ii, jj = c["ii"], c["jj"]
            q, k, v, kb, beta = c["q"], c["k"], c["v"], c["kb"], c["beta"]
            decay, eg, egl, el = c["decay"], c["eg"], c["egl"], c["el"]
            u, w, tinv = c["u"], c["w"], c["tinv"]
            d_qd = _bdot(d_o, state, _B_NT)
            d_attn = _bdot(d_o, v_new, _B_NT)
            d_w = -_bdot(d_vnew, state, _B_NT)
            d_rv = _hdot(tinv, d_vnew, _B_TN)
            d_rw = _hdot(tinv, d_w, _B_TN)
            d_amat = -(_bdot(d_rv, u, _B_NT) + _bdot(d_rw, w, _B_NT))
            d_low = jnp.where(ii > jj, d_amat, 0.0)
            d_p = d_low * decay
            d_qk = d_attn * decay
            e_mat = (d_low * c["pmat"] + d_attn * c["qk"]) * decay
            d_q = _bdot(d_qk, k, _B_NN) + d_qd * eg
            d_kb = _bdot(d_p, k, _B_NN) + d_rw * eg
            d_k = _bdot(d_qk, q, _B_TN) + _bdot(d_p, kb, _B_TN) + d_kd * egl + d_kb * beta
            d_beta = jnp.sum(d_kb * k, axis=2, keepdims=True) + jnp.sum(d_rv * v, axis=2, keepdims=True)
            d_v = d_rv * beta
            d_eg = jnp.sum(d_qd * q, axis=2, keepdims=True) + jnp.sum(d_rw * kb, axis=2, keepdims=True)
            d_egl = jnp.sum(d_kd * k, axis=2, keepdims=True)
            d_glast = jnp.sum(d_egl * egl, axis=1, keepdims=True) + d_el * el
            row_sum = jnp.sum(e_mat, axis=2, keepdims=True)
            col_sum = c["to_col"](jnp.sum(e_mat, axis=1, keepdims=True))
            d_gc = row_sum - col_sum + d_eg * eg - d_egl * egl
            d_g_row = jnp.sum(jnp.where(ii >= jj, jnp.broadcast_to(d_gc, (B, C, C)), 0.0), axis=1, keepdims=True) + d_glast
            beta_row = c["beta_row"]
            d_b_row = c["to_row"](d_beta) * beta_row * (1.0 - beta_row)
            d_a_row = d_g_row * c["neg_ea"] * _sigmoid(c["a_row"] + c["dt_b"])
            _dn_store(dqkv_ref, 0, c["r0"], group, d_q * QK_SCALE)
            _dn_store(dqkv_ref, 1, c["r0"], group, d_k)
            _dn_store(dqkv_ref, 2, c["r0"], group, d_v)
            d_b_row = d_b_row.reshape(group, H, 1, C)
            d_a_row = d_a_row.reshape(group, H, 1, C)
            d_a_log = jnp.sum((d_g_row * c["g_row"]).reshape(group, H, 1, C), axis=0)
            d_dt_b = jnp.sum(d_a_row, axis=0)
            for h in range(H):
                db_ref[h, pl.ds(n0, group)] = d_b_row[:, h]
                da_ref[h, pl.ds(n0, group)] = d_a_row[:, h]
                acc_scr[h, 0:1, 0:C] += d_a_log[h]
                acc_scr[h, 1:2, 0:C] += d_dt_b[h]

        lax.fori_loop(0, nc // group, step, 0)
        for h in range(H):
            tot = jnp.sum(acc_scr[h], axis=1, keepdims=True)
            dhp_ref[h] = jnp.broadcast_to(tot, (8, 128))

    return pl.pallas_call(
        body, name=name, grid=(HEADS // H,),
        in_specs=[qkv_spec, row_spec, row_spec, hp_spec, st_spec, _dn_inverse_spec(T),
                  pl.BlockSpec((T, H * HEAD_DIM), lambda h: (0, h))],
        out_specs=(qkv_spec, row_spec, row_spec, hp_spec),
        out_shape=(jax.ShapeDtypeStruct((3, T, D_MODEL), F32), jax.ShapeDtypeStruct((HEADS, nc, 1, C), F32),
                   jax.ShapeDtypeStruct((HEADS, nc, 1, C), F32), jax.ShapeDtypeStruct((HEADS, 8, 128), F32)),
        scratch_shapes=[pltpu.VMEM((H, HEAD_DIM, HEAD_DIM), F32), pltpu.VMEM((H, 8, 128), F32)],
        compiler_params=_params(("parallel",)),
    )(qkv, b_rows, a_rows, hp, states, inverses, do)


COL_Z = 3 * HEADS


def _gated_norm_fwd(o, proj, gain, *, name):
    T = o.shape[0]

    def body(o_ref, z_ref, g_ref, out_ref):
        x = o_ref[...]
        r = lax.rsqrt(jnp.mean(x * x, axis=-1, keepdims=True) + RMS_EPS)
        z = z_ref[...]
        out_ref[...] = (x * r * g_ref[...] * (z * _sigmoid(z))).astype(out_ref.dtype)

    return pl.pallas_call(
        body, name=name, grid=(HEADS,),
        in_specs=[pl.BlockSpec((T, HEAD_DIM), lambda h: (0, h)), pl.BlockSpec((T, HEAD_DIM), lambda h: (0, COL_Z + h)),
                  pl.BlockSpec((1, HEAD_DIM), lambda h: (0, 0))],
        out_specs=pl.BlockSpec((T, HEAD_DIM), lambda h: (0, h)),
        out_shape=jax.ShapeDtypeStruct((T, D_MODEL), BF16), compiler_params=_params(("parallel",)),
    )(o, proj, gain)


def _gated_norm_bwd(dout, o, proj, gain, *, name):
    T = o.shape[0]

    def body(d_ref, o_ref, z_ref, g_ref, do_ref, dz_ref, dg_ref):
        x = o_ref[...]
        r = lax.rsqrt(jnp.mean(x * x, axis=-1, keepdims=True) + RMS_EPS)
        n = x * r
        z = z_ref[...]
        sg = _sigmoid(z)
        d = d_ref[...].astype(F32)
        g = g_ref[...]
        dz_ref[...] = (d * n * g * (sg * (1.0 + z * (1.0 - sg)))).astype(dz_ref.dtype)
        dy = d * (z * sg)
        dyg = dy * g
        do_ref[...] = r * (dyg - n * jnp.mean(dyg * n, axis=-1, keepdims=True))

        @pl.when(pl.program_id(0) == 0)
        def _():
            dg_ref[...] = jnp.zeros_like(dg_ref)

        dg_ref[...] += jnp.sum(dy * n, axis=0, keepdims=True)

    head = pl.BlockSpec((T, HEAD_DIM), lambda h: (0, h))
    vec = pl.BlockSpec((1, HEAD_DIM), lambda h: (0, 0))
    return pl.pallas_call(
        body, name=name, grid=(HEADS,),
        in_specs=[head, head, pl.BlockSpec((T, HEAD_DIM), lambda h: (0, COL_Z + h)), vec],
        out_specs=(head, head, vec),
        out_shape=(jax.ShapeDtypeStruct((T, D_MODEL), F32), jax.ShapeDtypeStruct((T, D_MODEL), BF16),
                   jax.ShapeDtypeStruct((1, HEAD_DIM), F32)),
        compiler_params=_params(("arbitrary",)),
    )(dout, o, proj, gain)


COL_SBQ = 4 * HEADS
COL_SBK = 5 * HEADS
COL_SBV = 6 * HEADS


def _split_dot(x, mat):
    lead = x.shape[:-1]
    x = x.reshape(-1, x.shape[-1])
    hi = x.astype(BF16)
    lo = (x - hi.astype(F32)).astype(BF16)
    out = jnp.dot(hi, mat, preferred_element_type=F32) + jnp.dot(lo, mat, preferred_element_type=F32)
    return out.reshape(lead + (mat.shape[-1],))


def _sb_specs(T, heads):
    col = lambda first: pl.BlockSpec((T, heads * HEAD_DIM), lambda h: (0, first // heads + h))
    return col(COL_SBQ), col(COL_SBK), col(COL_SBV), pl.BlockSpec((1, HEAD_DIM), lambda h: (0, 0))


def _heads_first(x):
    return jnp.stack([x[:, c:c + HEAD_DIM] for c in range(0, x.shape[1], HEAD_DIM)], axis=0)


def _heads_last(x):
    return jnp.concatenate([x[h] for h in range(x.shape[0])], axis=1)


def _head_rms(x):
    r = lax.rsqrt(jnp.mean(x * x, axis=-1, keepdims=True) + RMS_EPS)
    return x * r, r


def _sb_fwd(proj, q_gain, k_gain, *, name):
    T = proj.shape[0]
    B = SB_BLOCK
    H = SB_HEADS
    nb = T // B
    KT = min(SB_KEY_TILE, T)
    NS = KT // B
    q_spec, k_spec, v_spec, g_spec = _sb_specs(T, H)

    def body(q_ref, k_ref, v_ref, gq_ref, gk_ref, o_ref, lt_ref, qs, ks, vs):
        qs[...] = (_head_rms(_heads_first(q_ref[...]))[0] * (gq_ref[...] * QK_SCALE)).astype(BF16)
        ks[...] = (_head_rms(_heads_first(k_ref[...]))[0] * gk_ref[...]).astype(BF16)
        vs[...] = _heads_first(v_ref[...]).astype(BF16)
        ii = lax.broadcasted_iota(jnp.int32, (B, B), 0)
        jj = lax.broadcasted_iota(jnp.int32, (B, B), 1)
        after = jnp.where(ii > jj, 1.0, 0.0).astype(BF16)
        ahead = lax.broadcasted_iota(jnp.int32, (H, B, KT), 2) - lax.broadcasted_iota(jnp.int32, (H, B, KT), 1)

        def q_block(i, carry):
            rows = pl.ds(pl.multiple_of(i * B, B), B)
            q = qs[:, rows, :]

            def tile(c0, acc, tail, masked):
                cols = pl.ds(c0, KT)
                z = lax.dot_general(q, ks[:, cols, :], _B_NT, preferred_element_type=F32)
                sp = _softplus(z)
                causal = ahead < (i * B - c0)
                loss = jnp.where(causal, sp, 0.0) if masked else sp
                parts = [None] * NS
                for b in reversed(range(NS)):
                    blk = loss[:, :, b * B:(b + 1) * B]
                    parts[b] = _split_dot(blk, after) + tail
                    tail = tail + jnp.sum(blk, axis=2, keepdims=True)
                lost = parts[0] if NS == 1 else jnp.concatenate(parts, axis=2)
                wts = jnp.exp(z - sp - lost)
                if masked:
                    wts = jnp.where(causal, wts, 0.0)
                acc = acc + lax.dot_general(wts.astype(BF16), vs[:, cols, :], _B_NN, preferred_element_type=F32)
                return acc, tail

            last = i // NS
            acc, tail = tile(pl.multiple_of(last * KT, KT), jnp.zeros((H, B, HEAD_DIM), F32), jnp.zeros((H, B, 1), F32), True)
            acc, tail = lax.fori_loop(
                1, last + 1, lambda s, c: tile(pl.multiple_of((last - s) * KT, KT), c[0], c[1], False), (acc, tail))
            o_ref[rows, :] = _heads_last(acc).astype(o_ref.dtype)
            lt_ref[rows, :] = _heads_last(jnp.broadcast_to(tail, (H, B, HEAD_DIM)))
            return carry

        lax.fori_loop(0, nb, q_block, 0)

    heads = pl.BlockSpec((T, H * HEAD_DIM), lambda h: (0, h))
    return pl.pallas_call(
        body, name=name, grid=(HEADS // H,), in_specs=[q_spec, k_spec, v_spec, g_spec, g_spec],
        out_specs=(heads, heads),
        out_shape=(jax.ShapeDtypeStruct((T, D_MODEL), BF16), jax.ShapeDtypeStruct((T, D_MODEL), F32)),
        scratch_shapes=[pltpu.VMEM((H, T, HEAD_DIM), BF16)] * 3, compiler_params=_params(("parallel",)),
    )(proj, proj, proj, q_gain, k_gain)


def _sb_bwd(proj, q_gain, k_gain, ltot, do, *, name):
    T = proj.shape[0]
    B = SB_BLOCK
    H = SB_HEADS_BWD
    nb = T // B
    KT = min(SB_KEY_TILE_BWD, T)
    NS = KT // B
    q_spec, k_spec, v_spec, g_spec = _sb_specs(T, H)

    def body(q_ref, k_ref, v_ref, gq_ref, gk_ref, lt_ref, do_ref, dq_ref, dk_ref, dv_ref, dgq_ref, dgk_ref,
             qs, ks, vs, dos, lts, dq_acc, dk_acc, dv_acc):
        qn, q_r = _head_rms(_heads_first(q_ref[...]))
        kn, k_r = _head_rms(_heads_first(k_ref[...]))
        qs[...] = (qn * (gq_ref[...] * QK_SCALE)).astype(BF16)
        ks[...] = (kn * gk_ref[...]).astype(BF16)
        vs[...] = _heads_first(v_ref[...]).astype(BF16)
        dos[...] = _heads_first(do_ref[...]).astype(BF16)
        lts[...] = _heads_first(lt_ref[...])
        dk_acc[...] = jnp.zeros_like(dk_acc)
        dv_acc[...] = jnp.zeros_like(dv_acc)
        ii = lax.broadcasted_iota(jnp.int32, (B, B), 0)
        jj = lax.broadcasted_iota(jnp.int32, (B, B), 1)
        upto = jnp.where(ii <= jj, 1.0, 0.0).astype(BF16)
        before = jnp.where(ii < jj, 1.0, 0.0).astype(BF16)
        ahead = lax.broadcasted_iota(jnp.int32, (H, B, KT), 2) - lax.broadcasted_iota(jnp.int32, (H, B, KT), 1)

        def q_block(i, carry):
            rows = pl.ds(pl.multiple_of(i * B, B), B)
            q = qs[:, rows, :]
            d_o = dos[:, rows, :]
            total = jnp.max(lts[:, rows, :], axis=2, keepdims=True)

            def tile(c0, dq, head_lb, head_de, masked):
                cols = pl.ds(c0, KT)
                k = ks[:, cols, :]
                v = vs[:, cols, :]
                z = lax.dot_general(q, k, _B_NT, preferred_element_type=F32)
                sp = _softplus(z)
                causal = ahead < (i * B - c0)
                loss = jnp.where(causal, sp, 0.0) if masked else sp
                parts = [None] * NS
                for b in range(NS):
                    blk = loss[:, :, b * B:(b + 1) * B]
                    parts[b] = _split_dot(blk, upto) + head_lb
                    head_lb = head_lb + jnp.sum(blk, axis=2, keepdims=True)
                prefix = parts[0] if NS == 1 else jnp.concatenate(parts, axis=2)
                wts = jnp.exp(z - sp + (prefix - total))
                if masked:
                    wts = jnp.where(causal, wts, 0.0)
                d_w = lax.dot_general(d_o, v, _B_NT, preferred_element_type=F32)
                d_e = wts * d_w
                d_eb = d_e.astype(BF16)
                for b in range(NS):
                    inside = jnp.dot(d_eb[:, :, b * B:(b + 1) * B].reshape(H * B, B), before, preferred_element_type=F32)
                    parts[b] = inside.reshape(H, B, B) + head_de
                    head_de = head_de + jnp.sum(d_e[:, :, b * B:(b + 1) * B], axis=2, keepdims=True)
                cum = parts[0] if NS == 1 else jnp.concatenate(parts, axis=2)
                sig = jnp.exp(z - sp)
                d_z = d_e - sig * (d_e + cum)
                if masked:
                    d_z = jnp.where(causal, d_z, 0.0)
                d_zb = d_z.astype(BF16)
                dq = dq + lax.dot_general(d_zb, k, _B_NN, preferred_element_type=F32)
                dk_acc[:, cols, :] += lax.dot_general(d_zb, q, _B_TN, preferred_element_type=F32)
                dv_acc[:, cols, :] += lax.dot_general(wts.astype(BF16), d_o, _B_TN, preferred_element_type=F32)
                return dq, head_lb, head_de

            last = i // NS
            zero = jnp.zeros((H, B, 1), F32)
            state = lax.fori_loop(0, last, lambda t, c: tile(pl.multiple_of(t * KT, KT), *c, False),
                                  (jnp.zeros((H, B, HEAD_DIM), F32), zero, zero))
            dq, _, _ = tile(pl.multiple_of(last * KT, KT), *state, True)
            dq_acc[:, rows, :] = dq * QK_SCALE
            return carry

        lax.fori_loop(0, nb, q_block, 0)

        def norm_bwd(d_scaled, n, r, gain):
            dn = d_scaled * gain
            d_gain = jnp.sum(jnp.sum(d_scaled * n, axis=1, keepdims=True), axis=0)
            return r * (dn - n * jnp.mean(dn * n, axis=-1, keepdims=True)), d_gain

        dq_raw, dgq = norm_bwd(dq_acc[...], qn, q_r, gq_ref[...])
        dk_raw, dgk = norm_bwd(dk_acc[...], kn, k_r, gk_ref[...])
        dq_ref[...] = _heads_last(dq_raw).astype(dq_ref.dtype)
        dk_ref[...] = _heads_last(dk_raw).astype(dk_ref.dtype)
        dv_ref[...] = _heads_last(dv_acc[...]).astype(dv_ref.dtype)

        @pl.when(pl.program_id(0) == 0)
        def _():
            dgq_ref[...] = jnp.zeros_like(dgq_ref)
            dgk_ref[...] = jnp.zeros_like(dgk_ref)

        dgq_ref[...] += dgq
        dgk_ref[...] += dgk

    heads = pl.BlockSpec((T, H * HEAD_DIM), lambda h: (0, h))
    out = jax.ShapeDtypeStruct((T, D_MODEL), BF16)
    vec = jax.ShapeDtypeStruct((1, HEAD_DIM), F32)
    return pl.pallas_call(
        body, name=name, grid=(HEADS // H,), in_specs=[q_spec, k_spec, v_spec, g_spec, g_spec, heads, heads],
        out_specs=(heads, heads, heads, g_spec, g_spec), out_shape=(out, out, out, vec, vec),
        scratch_shapes=[pltpu.VMEM((H, T, HEAD_DIM), BF16)] * 4 + [pltpu.VMEM((H, T, HEAD_DIM), F32)] * 4,
        compiler_params=_params(("arbitrary",)),
    )(proj, proj, proj, q_gain, k_gain, ltot, do)


ADAM_ROWS = 256


def _adamw(g_parts, w, m, v, *, name, layer=0, earlier=None):
    K, A, C = g_parts.shape
    R = w.shape[0]
    tr = next((t for t in (ADAM_ROWS, ADAM_ROWS // 2) if A % t == 0), A // 2 if A % 32 == 0 else A)
    first_block = layer * (A // tr)

    def body(g_ref, w_ref, m_ref, v_ref, *rest):
        go_ref, d_ref, mo_ref, vo_ref = rest[-4:]
        g = g_ref[0].astype(F32)
        for k in range(1, K):
            g = g + g_ref[k].astype(F32)
        go_ref[...] = g
        m_new = ADAM_B1 * m_ref[...] + (1.0 - ADAM_B1) * g
        v_new = ADAM_B2 * v_ref[...] + (1.0 - ADAM_B2) * (g * g)
        m_hat = m_new / (1.0 - ADAM_B1 ** ADAM_STEP)
        v_hat = v_new / (1.0 - ADAM_B2 ** ADAM_STEP)
        d_ref[...] = -ADAM_LR * (m_hat / (jnp.sqrt(v_hat) + ADAM_EPS) + ADAM_WD * w_ref[...])
        mo_ref[...] = m_new
        vo_ref[...] = v_new

    row = pl.BlockSpec((tr, C), lambda i: (first_block + i, 0))
    out = jax.ShapeDtypeStruct((R, C), F32)
    in_specs = [pl.BlockSpec((K, tr, C), lambda i: (0, i, 0)), row, row, row]
    if earlier is None:
        return pl.pallas_call(
            body, name=name, grid=(A // tr,), in_specs=in_specs, out_specs=(row, row, row, row),
            out_shape=(out, out, out, out), compiler_params=_params(("parallel",)),
        )(g_parts, w, m, v)
    return pl.pallas_call(
        body, name=name, grid=(A // tr,), in_specs=in_specs + [ANY] * 4, out_specs=(row, row, row, row),
        out_shape=(out, out, out, out), input_output_aliases={4 + j: j for j in range(4)},
        compiler_params=_params(("parallel",)),
    )(g_parts, w, m, v, *earlier)


def _sum_parts(parts, *, name):
    K, R, C = parts.shape

    def body(p_ref, o_ref):
        acc = p_ref[0]
        for k in range(1, K):
            acc = acc + p_ref[k]
        o_ref[...] = acc

    return pl.pallas_call(body, name=name, out_shape=jax.ShapeDtypeStruct((R, C), F32))(parts)


def _position():
    return lax.axis_index("x"), lax.axis_index("y"), lax.axis_index("c")


def _all_gather(shards, *, name):
    n = len(shards)

    def body(*refs):
        x_refs, out_refs = refs[:n], refs[n:2 * n]
        send_sems, recv_sems, local_sems = refs[2 * n:]
        x, y, c = _position()
        me, sibling = (x, y, c), (x, y, 1 - c)
        chips = [(1 - x, y), (x, 1 - y), (1 - x, 1 - y)]

        def slot(a, px, py, pc):
            return out_refs[a].at[4 * px + 2 * py + pc]

        def copy(a, k, block, to, own=False):
            return pltpu.make_async_remote_copy(
                src_ref=x_refs[a] if own else slot(a, *block), dst_ref=slot(a, *block),
                send_sem=send_sems.at[a, k], recv_sem=recv_sems.at[a, k], device_id=to, device_id_type=MESH)

        mine = [pltpu.make_async_copy(x_refs[a], slot(a, *me), local_sems.at[a]) for a in range(n)]
        for cp in mine:
            cp.start()
        first = [copy(a, 1 + j, me, (*chip, c), own=True) for j, chip in enumerate(chips) for a in range(n)]
        first += [copy(a, 0, me, sibling, own=True) for a in range(n)]
        for cp in first:
            cp.start()
        passed = []
        for j, chip in enumerate(chips):
            for a in range(n):
                copy(a, 1 + j, (*chip, c), me).wait_recv()
                passed.append(copy(a, 4 + j, (*chip, c), sibling))
                passed[-1].start()
        for a in range(n):
            copy(a, 0, sibling, me).wait_recv()
        for j, chip in enumerate(chips):
            for a in range(n):
                copy(a, 4 + j, (*chip, 1 - c), me).wait_recv()
        for cp in first + passed:
            cp.wait_send()
        for cp in mine:
            cp.wait()

    return pl.pallas_call(
        body, name=name, in_specs=[ANY] * n, out_specs=[ANY] * n,
        out_shape=[jax.ShapeDtypeStruct((N_DEV,) + s.shape, s.dtype) for s in shards],
        scratch_shapes=[pltpu.SemaphoreType.DMA((n, 7)), pltpu.SemaphoreType.DMA((n, 7)), pltpu.SemaphoreType.DMA((n,))],
    )(*shards)


HBM = pl.BlockSpec(memory_space=pltpu.HBM)
SEM = pl.BlockSpec(memory_space=pltpu.SEMAPHORE)
DATAFLOW = pltpu.SideEffectType.DATAFLOW_SIDE_EFFECTING


def _exchange_copies(gather, x_refs, land_refs, send_sems, recv_sems, local_sems):
    n = len(x_refs)
    x, y, c = _position()
    me = 4 * x + 2 * y + c

    def src(a, slot):
        return x_refs[a] if gather else x_refs[a].at[slot]

    mine = [pltpu.make_async_copy(src(a, me), land_refs[a].at[me], local_sems.at[a]) for a in range(n)]
    sends, recvs = [], []
    for k in range(1, N_DEV):
        px, py, pc = (x + (k >> 2)) % 2, (y + ((k >> 1) & 1)) % 2, (c + (k & 1)) % 2
        peer = 4 * px + 2 * py + pc
        for a in range(n):
            sems = dict(send_sem=send_sems.at[7 * a + k - 1], recv_sem=recv_sems.at[7 * a + k - 1],
                        device_id=(px, py, pc), device_id_type=MESH)
            sends.append(pltpu.make_async_remote_copy(src_ref=src(a, peer), dst_ref=land_refs[a].at[me], **sems))
            recvs.append(pltpu.make_async_remote_copy(src_ref=src(a, me), dst_ref=land_refs[a].at[peer], **sems))
    return mine, sends, recvs


def _exchange_start(parts, *, gather, name):
    n = len(parts)

    def body(*refs):
        x_refs, land_refs = refs[:n], refs[n:2 * n]
        send_sems, recv_sems, local_sems = refs[2 * n:2 * n + 3]
        token = refs[-1]
        mine, sends, _ = _exchange_copies(gather, x_refs, land_refs, send_sems, recv_sems, local_sems)
        for cp in mine + sends:
            cp.start()
        token[...] = jnp.zeros_like(token)

    sems = (pltpu.SemaphoreType.DMA((7 * n,)), pltpu.SemaphoreType.DMA((7 * n,)), pltpu.SemaphoreType.DMA((n,)))
    thru = tuple(pltpu.HBM(p.shape, p.dtype) for p in parts)
    land = tuple(pltpu.HBM(((N_DEV,) if gather else ()) + p.shape, p.dtype) for p in parts)
    res = pl.pallas_call(
        body, name=name, in_specs=[HBM] * (2 * n),
        out_specs=(SEM, SEM, SEM) + (HBM,) * (2 * n) + (pl.BlockSpec(memory_space=pltpu.VMEM),),
        out_shape=sems + thru + land + (jax.ShapeDtypeStruct((8, 128), F32),),
        input_output_aliases={a: 3 + a for a in range(2 * n)},
        compiler_params=pltpu.CompilerParams(has_side_effects=DATAFLOW),
    )(*[pltpu.with_memory_space_constraint(p, pltpu.HBM) for p in parts],
      *[pltpu.with_memory_space_constraint(lax.empty(z.shape, z.dtype), pltpu.HBM) for z in land])
    return res[:3], res[3:3 + n], res[3 + n:3 + 2 * n], res[-1]


def _exchange_wait(sems, parts, landing, after, *, gather, name):
    n = len(parts)
    after = list(after)

    def body(*refs):
        x_refs, land_refs = refs[:n], refs[n:2 * n]
        send_sems, recv_sems, local_sems = refs[2 * n:2 * n + 3]
        token = refs[-1]
        mine, sends, recvs = _exchange_copies(gather, x_refs, land_refs, send_sems, recv_sems, local_sems)
        for cp in recvs:
            cp.wait_recv()
        for cp in sends:
            cp.wait_send()
        for cp in mine:
            cp.wait()
        token[...] = jnp.zeros_like(token)

    thru = tuple(pltpu.HBM(p.shape, p.dtype) for p in tuple(parts) + tuple(landing))
    res = pl.pallas_call(
        body, name=name, in_specs=[HBM] * (2 * n) + [SEM, SEM, SEM] + [ANY] * len(after),
        out_specs=(HBM,) * (2 * n) + (pl.BlockSpec(memory_space=pltpu.VMEM),),
        out_shape=thru + (jax.ShapeDtypeStruct((8, 128), F32),), input_output_aliases={a: a for a in range(2 * n)},
        compiler_params=pltpu.CompilerParams(has_side_effects=DATAFLOW),
    )(*parts, *landing, *sems, *after)
    return res[n:2 * n], res[-1]


def _relay_copies(x_refs, land_refs, first_send, first_recv, relay_send, relay_recv, local_sems):
    n = len(x_refs)
    x, y, c = _position()
    sibling = (x, y, 1 - c)
    chips = [(1 - x, y), (x, 1 - y), (1 - x, 1 - y)]

    def slot(a, px, py, pc):
        return land_refs[a].at[4 * px + 2 * py + pc]

    def hop(a, k, block, to, own=False):
        return pltpu.make_async_remote_copy(
            src_ref=x_refs[a] if own else slot(a, *block), dst_ref=slot(a, *block),
            send_sem=first_send.at[4 * a + k], recv_sem=first_recv.at[4 * a + k], device_id=to, device_id_type=MESH)

    def relay(a, j, block, to):
        return pltpu.make_async_remote_copy(
            src_ref=slot(a, *block), dst_ref=slot(a, *block),
            send_sem=relay_send.at[3 * a + j], recv_sem=relay_recv.at[3 * a + j], device_id=to, device_id_type=MESH)

    me = (x, y, c)
    mine = [pltpu.make_async_copy(x_refs[a], slot(a, *me), local_sems.at[a]) for a in range(n)]
    sends = [hop(a, 1 + j, me, (*chip, c), own=True) for j, chip in enumerate(chips) for a in range(n)]
    sends += [hop(a, 0, me, sibling, own=True) for a in range(n)]
    over_ici = [hop(a, 1 + j, (*chip, c), me) for j, chip in enumerate(chips) for a in range(n)]
    from_sibling = [hop(a, 0, sibling, me) for a in range(n)]
    if relay_send is None:
        return mine, sends, over_ici, from_sibling, [], []
    relays = [relay(a, j, (*chip, c), sibling) for j, chip in enumerate(chips) for a in range(n)]
    relayed = [relay(a, j, (*chip, 1 - c), me) for j, chip in enumerate(chips) for a in range(n)]
    return mine, sends, over_ici, from_sibling, relays, relayed


def _relay_call(body, n_sem_in, n_sem_out, shards, landing, sems_in, after, name):
    n = len(shards)
    thru = tuple(pltpu.HBM(p.shape, p.dtype) for p in tuple(shards) + tuple(landing))
    res = pl.pallas_call(
        body, name=name, in_specs=[HBM] * (2 * n) + [SEM] * n_sem_in + [ANY] * len(after),
        out_specs=(SEM,) * len(n_sem_out) + (HBM,) * (2 * n) + (pl.BlockSpec(memory_space=pltpu.VMEM),),
        out_shape=tuple(pltpu.SemaphoreType.DMA((k,)) for k in n_sem_out) + thru + (jax.ShapeDtypeStruct((8, 128), F32),),
        input_output_aliases={a: len(n_sem_out) + a for a in range(2 * n)},
        compiler_params=pltpu.CompilerParams(has_side_effects=DATAFLOW),
    )(*shards, *landing, *sems_in, *after)
    k = len(n_sem_out)
    return res[:k], res[k:k + n], res[k + n:k + 2 * n], res[-1]


def _layer_refs(refs, layers):
    return [r if l is None else r.at[l] for r, l in zip(refs, layers)]


def _relay_gather_start(shards, layers, after, *, name):
    n = len(shards)

    def body(*refs):
        x_refs, land_refs = _layer_refs(refs[:n], layers), refs[n:2 * n]
        first_send, first_recv, local_sems = refs[2 * n + len(after):2 * n + len(after) + 3]
        mine, sends, *_ = _relay_copies(x_refs, land_refs, first_send, first_recv, None, None, local_sems)
        for cp in mine + sends:
            cp.start()
        refs[-1][...] = jnp.zeros_like(refs[-1])

    landing = [pltpu.with_memory_space_constraint(
        lax.empty((N_DEV,) + (s.shape if l is None else s.shape[1:]), s.dtype), pltpu.HBM) for s, l in zip(shards, layers)]
    shards = [pltpu.with_memory_space_constraint(s, pltpu.HBM) for s in shards]
    return _relay_call(body, 0, (4 * n, 4 * n, n), shards, landing, (), list(after), name)


def _relay_gather_pass_on(first, shards, layers, landing, after, *, name):
    n = len(shards)

    def body(*refs):
        x_refs, land_refs = _layer_refs(refs[:n], layers), refs[n:2 * n]
        first_send, first_recv, local_sems = refs[2 * n:2 * n + 3]
        relay_send, relay_recv = refs[2 * n + 3 + len(after):2 * n + 5 + len(after)]
        _, _, over_ici, _, relays, _ = _relay_copies(x_refs, land_refs, first_send, first_recv, relay_send, relay_recv,
                                                   local_sems)
        for arrival, cp in zip(over_ici, relays):
            arrival.wait_recv()
            cp.start()
        refs[-1][...] = jnp.zeros_like(refs[-1])

    return _relay_call(body, 3, (3 * n, 3 * n), shards, landing, first, list(after), name)


def _relay_gather_wait(first, relay, shards, layers, landing, after, *, name):
    n = len(shards)

    def body(*refs):
        x_refs, land_refs = _layer_refs(refs[:n], layers), refs[n:2 * n]
        first_send, first_recv, local_sems, relay_send, relay_recv = refs[2 * n:2 * n + 5]
        mine, sends, _, from_sibling, relays, relayed = _relay_copies(
            x_refs, land_refs, first_send, first_recv, relay_send, relay_recv, local_sems)
        for cp in from_sibling + relayed:
            cp.wait_recv()
        for cp in sends + relays:
            cp.wait_send()
        for cp in mine:
            cp.wait()
        refs[-1][...] = jnp.zeros_like(refs[-1])

    _, shards, landing, token = _relay_call(body, 5, (), shards, landing, tuple(first) + tuple(relay), list(after), name)
    return shards, landing, token


def _ffn_fwd(x, gain, wg_in, wg_out, tag):
    T, D = x.shape
    fb, rb = wg_in.shape[-1], wg_out.shape[-2]
    tm, tn = min(T, 1024), 512
    h = _rmsnorm_fwd(x, gain, name=f"{tag}_norm")
    p = _mm(name=f"{tag}_in", grid=(T // tm, N_DEV, 1), tile=(tm, fb),
            a=h, a_spec=pl.BlockSpec((tm, D), lambda i, j, k: (i, 0)),
            b=wg_in, b_spec=pl.BlockSpec((None, D, fb), lambda i, j, k: (j, 0, 0)),
            out_shape=jax.ShapeDtypeStruct((N_DEV, T, fb), BF16), o_spec=pl.BlockSpec((None, tm, fb), lambda i, j, k: (j, i, 0)))
    a = _swiglu_fwd(p, name=f"{tag}_act")
    y = _mm(name=f"{tag}_out", grid=(T // tm, 1, FF_HALF), tile=(tm, D), resid=x, scale=0.5,
            a=a, a_spec=pl.BlockSpec((None, tm, fb), lambda i, j, k: (k, i, 0)),
            b=wg_out.reshape(N_DEV * rb, D), b_spec=pl.BlockSpec((fb, D), lambda i, j, k: (k, 0)),
            out_shape=jax.ShapeDtypeStruct((T, D), F32), o_spec=pl.BlockSpec((tm, D), lambda i, j, k: (i, 0)))
    return y, (x, h, p, a)


def _ffn_bwd(dy, saved, gain, wg_in, wg_out, tag, on_weight_grads=None):
    x, h, p, a = saved
    T, D = x.shape
    fb, rb = wg_in.shape[-1], wg_out.shape[-2]
    tm, tn = min(T, 1024), 512
    da = _mm(name=f"{tag}_out_dx", grid=(T // tm, FF_HALF, 1), tile=(tm, fb), tb=True, scale=0.5,
             a=dy, a_spec=pl.BlockSpec((tm, D), lambda i, j, k: (i, 0)),
             b=wg_out.reshape(N_DEV * rb, D), b_spec=pl.BlockSpec((fb, D), lambda i, j, k: (j, 0)),
             out_shape=jax.ShapeDtypeStruct((FF_HALF, T, fb), BF16), o_spec=pl.BlockSpec((None, tm, fb), lambda i, j, k: (j, i, 0)))
    d_w_out = _mm(name=f"{tag}_out_dw", grid=(FF_HALF, D // tn, 1), tile=(fb, tn), ta=True, scale=0.5,
                  a=a, a_spec=pl.BlockSpec((None, T, fb), lambda i, j, k: (i, 0, 0)),
                  b=dy, b_spec=pl.BlockSpec((T, tn), lambda i, j, k: (0, j)),
                  out_shape=jax.ShapeDtypeStruct((FF_HALF, fb, D), BF16), o_spec=pl.BlockSpec((None, fb, tn), lambda i, j, k: (i, 0, j)))
    dp = _swiglu_bwd(da, p, name=f"{tag}_act_bwd")
    d_w_in = _mm(name=f"{tag}_in_dw", grid=(1, N_DEV, 1), tile=(D, fb), ta=True,
                 a=h, a_spec=pl.BlockSpec((T, D), lambda i, j, k: (0, 0)),
                 b=dp, b_spec=pl.BlockSpec((None, T, fb), lambda i, j, k: (j, 0, 0)),
                 out_shape=jax.ShapeDtypeStruct((N_DEV, D, fb), BF16), o_spec=pl.BlockSpec((None, D, fb), lambda i, j, k: (j, 0, 0)))
    dh = _mm(name=f"{tag}_in_dx", grid=(T // tm, 1, N_DEV), tile=(tm, D), tb=True,
             a=dp, a_spec=pl.BlockSpec((None, tm, fb), lambda i, j, k: (k, i, 0)),
             b=wg_in, b_spec=pl.BlockSpec((None, D, fb), lambda i, j, k: (k, 0, 0)),
             out_shape=jax.ShapeDtypeStruct((T, D), F32), o_spec=pl.BlockSpec((tm, D), lambda i, j, k: (i, 0)))
    d_w_out = d_w_out.reshape(N_DEV, rb, D)
    if on_weight_grads is not None:
        gain = gain + on_weight_grads(d_w_in, d_w_out)[0, 0]
    dx, d_gain = _rmsnorm_bwd(dh, x, gain, dy, name=f"{tag}_norm_bwd")
    return dx, d_gain, d_w_in, d_w_out


def _square_mm(a, wg, *, name, transposed=False, out_dtype=F32, resid=None):
    T, D = a.shape
    w = wg.reshape(D, D)
    return _matmul(a, w, tb=transposed, name=name, out_dtype=out_dtype, resid=resid)


def _head_rows(cols, T):
    return cols.T.reshape(HEADS, T // DN_CHUNK, 1, DN_CHUNK)


def _mixer_fwd(x, w, big, tag):
    T = x.shape[0]
    h = _rmsnorm_fwd(x, w["mix_norm"], name=f"{tag}_norm")
    proj = _matmul(h, big["w_main"], name=f"{tag}_proj")
    scal = _matmul(h, big["w_scal"], name=f"{tag}_proj_scal", tn=N_SCAL)
    qkv = _conv_fwd(proj, big["conv_w"], name=f"{tag}_conv")
    b_rows = _head_rows(scal[:, 0:HEADS], T)
    a_rows = _head_rows(scal[:, HEADS:2 * HEADS], T)
    o_a, *states = _dn_fwd(qkv, b_rows, a_rows, w["hp"], name=f"{tag}_dn")
    oa_n = _gated_norm_fwd(o_a, proj, w["dn_out_norm"], name=f"{tag}_dn_norm")
    ya = _square_mm(oa_n, big["w_branch_a"], name=f"{tag}_branch_a")
    o_b, ltot = _sb_fwd(proj, w["sb_q_norm"], w["sb_k_norm"], name=f"{tag}_sb")
    yb = _square_mm(o_b, big["w_branch_b"], name=f"{tag}_branch_b")
    merged = _merge_fwd(ya, yb, proj, name=f"{tag}_merge")
    y = _square_mm(merged, big["w_out"], name=f"{tag}_out", resid=x)
    return y, (x, h, proj, qkv, b_rows, a_rows, o_a, states, oa_n, ya, o_b, ltot, yb, merged)


def _mixer_bwd(dy, saved, w, big, tag, on_weight_grads):
    x, h, proj, qkv, b_rows, a_rows, o_a, states, oa_n, ya, o_b, ltot, yb, merged = saved
    T = x.shape[0]
    g = {}
    d_merged = _square_mm(dy, big["w_out"], transposed=True, name=f"{tag}_out_dx", out_dtype=BF16)
    g["w_out"] = _matmul(merged, dy, ta=True, name=f"{tag}_out_dw", out_dtype=BF16)
    d_ya, d_yb, d_ga, d_gb = _merge_bwd(d_merged, ya, yb, proj, name=f"{tag}_merge_bwd")
    d_oan = _square_mm(d_ya, big["w_branch_a"], transposed=True, name=f"{tag}_branch_a_dx")
    g["w_branch_a"] = _matmul(oa_n, d_ya, ta=True, name=f"{tag}_branch_a_dw", out_dtype=BF16)
    d_ob = _square_mm(d_yb, big["w_branch_b"], transposed=True, name=f"{tag}_branch_b_dx")
    g["w_branch_b"] = _matmul(o_b, d_yb, ta=True, name=f"{tag}_branch_b_dw", out_dtype=BF16)
    d_oa, d_z, g["dn_out_norm"] = _gated_norm_bwd(d_oan, o_a, proj, w["dn_out_norm"], name=f"{tag}_dn_norm_bwd")
    d_qkv, d_b_rows, d_a_rows, d_hp = _dn_bwd(qkv, b_rows, a_rows, w["hp"], *states, d_oa, name=f"{tag}_dn_bwd")
    g["dn_a_log"] = d_hp[:, 0, 0]
    g["dn_dt_bias"] = d_hp[:, 1, 0]
    d_conv_in, g["conv_w"] = _conv_bwd(d_qkv, proj, big["conv_w"], name=f"{tag}_conv_bwd")
    d_sbq, d_sbk, d_sbv, g["sb_q_norm"], g["sb_k_norm"] = _sb_bwd(
        proj, w["sb_q_norm"], w["sb_k_norm"], ltot, d_ob, name=f"{tag}_sb_bwd")
    d_proj = jnp.concatenate([d_conv_in, d_z, d_sbq, d_sbk, d_sbv, d_ga, d_gb], axis=1)
    d_scal = jnp.concatenate([d_b_rows.reshape(HEADS, T).T, d_a_rows.reshape(HEADS, T).T,
                              jnp.zeros((T, N_SCAL - 2 * HEADS), F32)], axis=1).astype(BF16)
    g["w_main"] = _matmul(h, d_proj, ta=True, name=f"{tag}_proj_dw", out_dtype=BF16)
    g["w_scal"] = _matmul(h, d_scal, ta=True, name=f"{tag}_proj_scal_dw", out_dtype=BF16, tn=N_SCAL)
    dh_scal = _matmul(d_scal, big["w_scal"], tb=True, name=f"{tag}_proj_scal_dx")
    dh = _matmul(d_proj, big["w_main"], tb=True, name=f"{tag}_proj_dx", tk=N_MAIN // 4, resid=dh_scal)
    gain = w["mix_norm"] + on_weight_grads(g)[0, 0]
    dx, g["mix_norm"] = _rmsnorm_bwd(dh, x, gain, dy, name=f"{tag}_norm_bwd")
    return dx, g


def _local_step(x, target, layers, weights_of, on_weight_grads):
    saved, bigs = [], []
    for l, w in enumerate(layers):
        big = weights_of(l, 0, x)
        x, s1 = _ffn_fwd(x, w["ffn1_norm"] + big["issued"], big["ffn1_w_in"], big["ffn1_w_out"], f"l{l}_ffn1")
        big.update(weights_of(l, 1, x))
        x, s2 = _mixer_fwd(x, dict(w, mix_norm=w["mix_norm"] + big["issued"]), big, f"l{l}_mix")
        big.update(weights_of(l, 2, x))
        x, s3 = _ffn_fwd(x, w["ffn2_norm"] + big["issued"], big["ffn2_w_in"], big["ffn2_w_out"], f"l{l}_ffn2")
        saved.append((s1, s2, s3))
        bigs.append(big)
    loss, dx = _loss_head(x, target, name="loss_head")
    small = [None] * len(layers)
    for l in reversed(range(len(layers))):
        w, big = layers[l], bigs[l]
        s1, s2, s3 = saved[l]
        dx, g_n2, _, _ = _ffn_bwd(
            dx, s3, w["ffn2_norm"], big["ffn2_w_in"], big["ffn2_w_out"], f"l{l}_ffn2",
            on_weight_grads=lambda g_in, g_out, l=l: on_weight_grads(l, 0, dict(ffn2_w_in=g_in, ffn2_w_out=g_out)))
        dx, g = _mixer_bwd(dx, s2, w, big, f"l{l}_mix", on_weight_grads=lambda g, l=l: on_weight_grads(l, 1, g))
        dx, g_n1, _, _ = _ffn_bwd(
            dx, s1, w["ffn1_norm"], big["ffn1_w_in"], big["ffn1_w_out"], f"l{l}_ffn1",
            on_weight_grads=lambda g_in, g_out, l=l: on_weight_grads(l, 2, dict(ffn1_w_in=g_in, ffn1_w_out=g_out)))
        small[l] = dict(g, ffn1_norm=g_n1, ffn2_norm=g_n2)
    return loss, dx, small


_BIG = ("ffn1_w_in", "ffn1_w_out", "w_in", "w_branch_a", "w_branch_b", "w_out", "ffn2_w_in", "ffn2_w_out")
_STAGES = (("ffn2_w_in", "ffn2_w_out"), ("w_in", "w_branch_a", "w_branch_b", "w_out"), ("ffn1_w_in", "ffn1_w_out"))
_SMALL = ("ffn1_norm", "mix_norm", "ffn2_norm", "dn_a_log", "dn_dt_bias", "dn_out_norm", "sb_q_norm", "sb_k_norm")
_ORDER = ("ffn1_norm", "ffn1_w_in", "ffn1_w_out", "mix_norm", "w_in", "dn_conv_w", "dn_a_log", "dn_dt_bias", "dn_out_norm",
          "sb_q_norm", "sb_k_norm", "w_branch_a", "w_branch_b", "w_out", "ffn2_norm", "ffn2_w_in", "ffn2_w_out")
COL_SCAL = 4 * D_MODEL
SCAL_SLOT = COL_SCAL // (N_IN // N_DEV)
SCAL_AT = COL_SCAL % (N_IN // N_DEV)
assert SCAL_AT + 2 * HEADS <= N_IN // N_DEV


def _pad_rows(a, multiple):
    pad = (-a.shape[-2]) % multiple
    return a if pad == 0 else jnp.pad(a, [(0, 0)] * (a.ndim - 2) + [(0, pad), (0, 0)])


def _lane_rows(a):
    flat = a.reshape(-1)
    flat = jnp.pad(flat, (0, (-flat.shape[0]) % 128))
    return flat.reshape(-1, 128)


def _pack_small(named):
    pieces, spans, r = [], {}, 0
    for n, a in named:
        rows = _lane_rows(a)
        spans[n] = (r, r + rows.shape[0], a.shape)
        r += rows.shape[0]
        pieces.append(rows)
    return _pad_rows(jnp.concatenate(pieces, axis=0), 8), spans


def _unpack_small(packed, spans, n):
    r0, r1, shape = spans[n]
    return packed[r0:r1].reshape(-1)[:math.prod(shape)].reshape(shape)


def kernel(x, ffn1_norm, ffn1_w_in, ffn1_w_out, mix_norm, w_in, dn_conv_w, dn_a_log, dn_dt_bias, dn_out_norm, sb_q_norm, sb_k_norm, w_branch_a, w_branch_b, w_out, ffn2_norm, ffn2_w_in, ffn2_w_out, loss_target, m_ffn1_norm, m_ffn1_w_in, m_ffn1_w_out, m_mix_norm, m_w_in, m_dn_conv_w, m_dn_a_log, m_dn_dt_bias, m_dn_out_norm, m_sb_q_norm, m_sb_k_norm, m_w_branch_a, m_w_branch_b, m_w_out, m_ffn2_norm, m_ffn2_w_in, m_ffn2_w_out, v_ffn1_norm, v_ffn1_w_in, v_ffn1_w_out, v_mix_norm, v_w_in, v_dn_conv_w, v_dn_a_log, v_dn_dt_bias, v_dn_out_norm, v_sb_q_norm, v_sb_k_norm, v_w_branch_a, v_w_branch_b, v_w_out, v_ffn2_norm, v_ffn2_w_in, v_ffn2_w_out):
    given = dict(locals())
    weights = {n: given[n] for n in _ORDER}
    mom_m = {n: given["m_" + n] for n in _ORDER}
    mom_v = {n: given["v_" + n] for n in _ORDER}
    L = ffn1_norm.shape[0]
    ax, ay, ac = _position()
    my_slot = 4 * ax + 2 * ay + ac

    conv_cols = dn_conv_w.shape[-1]
    first_ffn = _STAGES[2]
    later = tuple(n for n in _BIG if n not in first_ffn)
    sources = {n: weights[n].astype(BF16) for n in _BIG}
    sources["conv"] = _pad_rows(_lane_rows(dn_conv_w), 8)
    gathers, landed = {}, {}

    def start_gather(key, names, l, after):
        layer_of = [None if n == "conv" else l for n in names]
        first, thru, landing, token = _relay_gather_start([sources[n] for n in names], layer_of, after,
                                                          name=f"gather_start_{key}")
        sources.update(zip(names, thru))
        gathers[key] = dict(first=first, landing=landing, names=names, layers=layer_of)
        return token

    def pass_on(key, after):
        g = gathers[key]
        g["relay"], thru, g["landing"], token = _relay_gather_pass_on(
            g["first"], [sources[n] for n in g["names"]], g["layers"], g["landing"], [after], name=f"gather_pass_on_{key}")
        sources.update(zip(g["names"], thru))
        return token

    def wait_gather(key, after):
        g = gathers.pop(key)
        thru, arrays, token = _relay_gather_wait(g["first"], g["relay"], [sources[n] for n in g["names"]], g["layers"],
                                                 g["landing"], [after], name=f"gather_wait_{key}")
        sources.update(zip(g["names"], thru))
        landed.update(zip(g["names"], arrays))
        return token

    def weights_of(l, part, x_in):
        if l == 0 and part == 0:
            start_gather("l0_ffn1", first_ffn, 0, [])
            pass_on("l0_ffn1", x_in)
            issued = start_gather("l0", later + ("conv",), 0, [wait_gather("l0_ffn1", x_in)])
            return dict({n: landed.pop(n) for n in first_ffn}, issued=issued[0, 0])
        if part == 0:
            token = wait_gather(f"l{l}", x_in)
            issued = start_gather(f"l{l + 1}", _BIG, l + 1, [token]) if l + 1 < L else token
            return dict({n: landed.pop(n) for n in first_ffn}, issued=issued[0, 0])
        if part == 2:
            return dict(issued=pass_on(f"l{l + 1}", x_in)[0, 0] if l + 1 < L else 0.0)
        issued = 0.0
        if l == 0:
            pass_on("l0", x_in)
            token = wait_gather("l0", x_in)
            issued = start_gather("l1", _BIG, 1, [token])[0, 0] if L > 1 else 0.0
            conv = landed.pop("conv").reshape(N_DEV, -1)[:, :L * DN_CONV * conv_cols]
            landed["conv_w"] = conv.reshape(N_DEV, L, DN_CONV, conv_cols).transpose(1, 2, 0, 3).reshape(
                L, DN_CONV, N_DEV * conv_cols)
        big = {n: landed.pop(n) for n in later}
        wi = big.pop("w_in")
        pieces = [wi[d] for d in range(N_DEV)]
        pieces[SCAL_SLOT:SCAL_SLOT + 1] = [wi[SCAL_SLOT][:, :SCAL_AT], wi[SCAL_SLOT][:, SCAL_AT + 2 * HEADS:]]
        big["w_main"] = jnp.concatenate(pieces, axis=1)
        big["w_scal"] = jnp.pad(wi[SCAL_SLOT][:, SCAL_AT:SCAL_AT + 2 * HEADS], ((0, 0), (0, N_SCAL - 2 * HEADS)))
        big["conv_w"] = landed["conv_w"][l]
        return dict(big, issued=issued)

    layers = []
    for l in range(L):
        hp = jnp.concatenate([jnp.broadcast_to(dn_a_log[l][:, None, None], (HEADS, 1, 128)),
                              jnp.broadcast_to(dn_dt_bias[l][:, None, None], (HEADS, 1, 128)),
                              jnp.zeros((HEADS, 6, 128), F32)], axis=1)
        layers.append(dict(ffn1_norm=ffn1_norm[l][None], mix_norm=mix_norm[l][None], hp=hp,
                           dn_out_norm=dn_out_norm[l][None], sb_q_norm=sb_q_norm[l][None],
                           sb_k_norm=sb_k_norm[l][None], ffn2_norm=ffn2_norm[l][None]))

    in_flight = {}

    def on_weight_grads(l, stage, g):
        parts = dict(g)
        if stage == 1:
            gm, shard = g["w_main"], N_IN // N_DEV
            blocks = [gm[:, d * shard:(d + 1) * shard] for d in range(SCAL_SLOT)]
            blocks.append(jnp.concatenate([gm[:, SCAL_SLOT * shard:COL_SCAL], g["w_scal"][:, :2 * HEADS],
                                           gm[:, COL_SCAL:(SCAL_SLOT + 1) * shard - 2 * HEADS]], axis=1))
            blocks += [gm[:, d * shard - 2 * HEADS:(d + 1) * shard - 2 * HEADS] for d in range(SCAL_SLOT + 1, N_DEV)]
            parts["w_in"] = jnp.stack(blocks)
            for n in ("w_branch_a", "w_branch_b", "w_out"):
                parts[n] = g[n].reshape(N_DEV, D_MODEL // N_DEV, D_MODEL)
        *in_flight[l, stage], token = _exchange_start([parts[n] for n in _STAGES[stage]], gather=False,
                                                      name=f"scatter_start_l{l}_{stage}")
        return token

    loss_row, dx, grads = _local_step(x[0], loss_target[0], layers, weights_of, on_weight_grads)
    loss = lax.psum(loss_row[0, 0], ("x", "y", "c"))

    results = {n: None for n in _BIG}
    after = [dx]
    for l in reversed(range(L)):
        for stage, names in enumerate(_STAGES):
            landed, all_landed = _exchange_wait(*in_flight[l, stage], after, gather=False,
                                                name=f"scatter_wait_l{l}_{stage}")
            for n, parts in zip(names, landed):
                _, a, b = weights[n].shape
                results[n] = _adamw(parts, weights[n].reshape(L * a, b), mom_m[n].reshape(L * a, b),
                                    mom_v[n].reshape(L * a, b), layer=l, earlier=results[n], name=f"adamw_{n}_l{l}")
            after = [results[n][0] for n in names]
    out = {n: tuple(t.reshape(weights[n].shape) for t in results[n]) for n in _BIG}

    small_grads = [(n, jnp.stack([g[n].reshape(weights[n].shape[1:]) for g in grads])) for n in _SMALL]
    small_packed, spans = _pack_small(small_grads + [("conv", jnp.stack([g["conv_w"] for g in grads]))])
    small_packed = small_packed + all_landed[0, 0]
    small_sum = _sum_parts(_all_gather([small_packed], name="gather_small_grads")[0], name="sum_small_grads")
    rep_rows = spans["conv"][0]
    pack_rep = lambda d: _pad_rows(_pack_small([(n, d[n]) for n in _SMALL])[0], 8)
    rep_pad = (-rep_rows) % 8
    g_rep = jnp.pad(small_sum[:rep_rows], ((0, rep_pad), (0, 0)))
    res = _adamw(g_rep[None], pack_rep(weights), pack_rep(mom_m), pack_rep(mom_v), name="adamw_replicated")
    for n in _SMALL:
        out[n] = tuple(_unpack_small(t, spans, n) for t in res)
    conv_sum = _unpack_small(small_sum, spans, "conv")
    conv_mine = lax.dynamic_slice_in_dim(conv_sum, my_slot * conv_cols, conv_cols, axis=2).reshape(L * DN_CONV, conv_cols)
    flat = lambda t: t.reshape(L * DN_CONV, conv_cols)
    res = _adamw(conv_mine[None], flat(dn_conv_w), flat(m_dn_conv_w), flat(v_dn_conv_w), name="adamw_conv")
    out["dn_conv_w"] = tuple(t.reshape(L, DN_CONV, conv_cols) for t in res)

    return (loss, dx[None], *[out[n][0] for n in _ORDER], *[out[n][1] for n in _ORDER],
            *[out[n][2] for n in _ORDER], *[out[n][3] for n in _ORDER])
```

```python
import functools
import math

import jax
import jax.numpy as jnp
from jax import lax
from jax.experimental import pallas as pl
from jax.experimental.pallas import tpu as pltpu

F32 = jnp.float32
BF16 = jnp.bfloat16

N_DEV = 8
D_MODEL = 1024
DEPTH = 4
D_FF = 2816
HEADS = 8
HEAD_DIM = 128
DN_CHUNK = 64
DN_CONV = 4
DN_GROUP = 8
DN_HEADS = 2
SB_BLOCK = 128
SB_KEY_TILE = 512
SB_HEADS = 4
SB_HEADS_BWD = 2
SB_KEY_TILE_BWD = 512
RMS_EPS = 1e-6
L2_EPS = 1e-6
N_IN = 9232
N_MAIN = 9216
N_SCAL = 128
QK_SCALE = HEAD_DIM ** -0.5

ADAM_LR = 0.001
ADAM_B1 = 0.9
ADAM_B2 = 0.999
ADAM_EPS = 1e-08
ADAM_WD = 0.01
ADAM_STEP = 10

V7X_VMEM_LIMIT = 56 * 1024 * 1024
MESH = pl.DeviceIdType.MESH
ANY = pl.BlockSpec(memory_space=pl.ANY)


def _params(sem=None, vmem=V7X_VMEM_LIMIT):
    return pltpu.CompilerParams(dimension_semantics=sem, vmem_limit_bytes=vmem)


def _sigmoid(x):
    return 1.0 / (1.0 + jnp.exp(-x))


SOFTPLUS_LINEAR = 30.0


def _softplus(x):
    return jnp.maximum(x, jnp.log(1.0 + jnp.exp(jnp.minimum(x, SOFTPLUS_LINEAR))))


def _bdot(a, b, dims=(((1,), (0,)), ((), ()))):
    return lax.dot_general(a.astype(BF16), b.astype(BF16), dims, preferred_element_type=F32)


_NT = (((1,), (1,)), ((), ()))
_TN = (((0,), (0,)), ((), ()))


def _hdot(a, b, dims=(((1,), (0,)), ((), ()))):
    a_hi = a.astype(BF16)
    b_hi = b.astype(BF16)
    a_lo = (a - a_hi.astype(F32)).astype(BF16)
    b_lo = (b - b_hi.astype(F32)).astype(BF16)
    dot = functools.partial(lax.dot_general, dimension_numbers=dims, preferred_element_type=F32)
    return dot(a_hi, b_hi) + (dot(a_hi, b_lo) + dot(a_lo, b_hi))


def _hdot_tn(a, b):
    return _hdot(a, b, _TN)


def _mm(*, name, grid, a, a_spec, b, b_spec, out_shape, o_spec, tile, ta=False, tb=False, resid=None, scale=1.0):
    nk = grid[2]
    dims = (((0 if ta else 1,), (1 if tb else 0,)), ((), ()))

    def flat(v):
        return v if v.ndim == 2 else v.reshape(-1, v.shape[-1])

    def body(*refs):
        a_ref, b_ref = refs[:2]
        r_ref = refs[2] if resid is not None else None
        o_ref = refs[3] if resid is not None else refs[2]
        part = lax.dot_general(flat(a_ref[...]).astype(BF16), flat(b_ref[...]).astype(BF16), dims,
                               preferred_element_type=F32)

        def finish(acc):
            if scale != 1.0:
                acc = acc * scale
            if r_ref is not None:
                acc = r_ref[...] + acc
            o_ref[...] = acc.astype(o_ref.dtype)

        if nk == 1:
            finish(part)
        else:
            acc_ref = refs[-1]
            k = pl.program_id(2)

            @pl.when(k == 0)
            def _():
                acc_ref[...] = part

            @pl.when(k > 0)
            def _():
                acc_ref[...] += part

            @pl.when(k == nk - 1)
            def _():
                finish(acc_ref[...])

    in_specs = [a_spec, b_spec] + ([pl.BlockSpec(tile, lambda i, j, k: (i, j))] if resid is not None else [])
    args = (a, b) + ((resid,) if resid is not None else ())
    return pl.pallas_call(
        body, name=name, grid=grid, in_specs=in_specs, out_specs=o_spec, out_shape=out_shape,
        scratch_shapes=[pltpu.VMEM(tile, F32)] if nk > 1 else [],
        compiler_params=_params(("parallel", "parallel", "arbitrary")),
    )(*args)


def _matmul(a, b, *, name, ta=False, tb=False, out_dtype=F32, tm=None, tn=None, tk=None, resid=None, scale=1.0):
    if ta:
        K, M = a.shape
    else:
        M, K = a.shape
    N = b.shape[0] if tb else b.shape[1]
    tm = tm or min(M, 1024)
    tn = tn or min(N, 512)
    tk = tk or K
    assert M % tm == 0 and N % tn == 0 and K % tk == 0, (name, M, N, K, tm, tn, tk)
    a_spec = pl.BlockSpec((tk, tm), lambda i, j, k: (k, i)) if ta else pl.BlockSpec((tm, tk), lambda i, j, k: (i, k))
    b_spec = pl.BlockSpec((tn, tk), lambda i, j, k: (j, k)) if tb else pl.BlockSpec((tk, tn), lambda i, j, k: (k, j))
    return _mm(name=name, grid=(M // tm, N // tn, K // tk), a=a, a_spec=a_spec, b=b, b_spec=b_spec,
               out_shape=jax.ShapeDtypeStruct((M, N), out_dtype), o_spec=pl.BlockSpec((tm, tn), lambda i, j, k: (i, j)),
               tile=(tm, tn), ta=ta, tb=tb, resid=resid, scale=scale)


ROW_TILE = 256


def _rmsnorm_fwd(x, gain, *, name):
    T, D = x.shape

    def body(x_ref, g_ref, o_ref):
        xf = x_ref[...]
        r = lax.rsqrt(jnp.mean(xf * xf, axis=-1, keepdims=True) + RMS_EPS)
        o_ref[...] = (xf * r * g_ref[...]).astype(o_ref.dtype)

    return pl.pallas_call(
        body, name=name, grid=(T // ROW_TILE,),
        in_specs=[pl.BlockSpec((ROW_TILE, D), lambda i: (i, 0)), pl.BlockSpec((1, D), lambda i: (0, 0))],
        out_specs=pl.BlockSpec((ROW_TILE, D), lambda i: (i, 0)),
        out_shape=jax.ShapeDtypeStruct((T, D), BF16), compiler_params=_params(("parallel",)),
    )(x, gain)


def _rmsnorm_bwd(dh, x, gain, dres, *, name):
    T, D = x.shape

    def body(dh_ref, x_ref, g_ref, res_ref, dx_ref, dg_ref):
        xf = x_ref[...]
        r = lax.rsqrt(jnp.mean(xf * xf, axis=-1, keepdims=True) + RMS_EPS)
        y = xf * r
        dh_v = dh_ref[...].astype(F32)
        dy = dh_v * g_ref[...]
        dx_ref[...] = res_ref[...] + r * (dy - y * jnp.mean(dy * y, axis=-1, keepdims=True))

        @pl.when(pl.program_id(0) == 0)
        def _():
            dg_ref[...] = jnp.zeros_like(dg_ref)

        dg_ref[...] += jnp.sum(dh_v * y, axis=0, keepdims=True)

    row = pl.BlockSpec((ROW_TILE, D), lambda i: (i, 0))
    vec = pl.BlockSpec((1, D), lambda i: (0, 0))
    return pl.pallas_call(
        body, name=name, grid=(T // ROW_TILE,), in_specs=[row, row, vec, row], out_specs=(row, vec),
        out_shape=(jax.ShapeDtypeStruct((T, D), F32), jax.ShapeDtypeStruct((1, D), F32)),
        compiler_params=_params(("arbitrary",)),
    )(dh, x, gain, dres)


FF_HALF = N_DEV // 2


def _swiglu_fwd(p, *, name):
    _, T, fb = p.shape

    def body(g_ref, u_ref, o_ref):
        g = g_ref[...].astype(F32)
        o_ref[...] = (g * _sigmoid(g) * u_ref[...].astype(F32)).astype(o_ref.dtype)

    blk = (None, ROW_TILE, fb)
    return pl.pallas_call(
        body, name=name, grid=(T // ROW_TILE, FF_HALF),
        in_specs=[pl.BlockSpec(blk, lambda i, j: (j, i, 0)), pl.BlockSpec(blk, lambda i, j: (j + FF_HALF, i, 0))],
        out_specs=pl.BlockSpec(blk, lambda i, j: (j, i, 0)),
        out_shape=jax.ShapeDtypeStruct((FF_HALF, T, fb), BF16), compiler_params=_params(("parallel", "parallel")),
    )(p, p)


def _swiglu_bwd(da, p, *, name):
    _, T, fb = p.shape

    def body(da_ref, g_ref, u_ref, o_ref):
        g = g_ref[...].astype(F32)
        u = u_ref[...].astype(F32)
        d = da_ref[...].astype(F32)
        s = _sigmoid(g)
        o_ref[0] = (d * u * (s * (1.0 + g * (1.0 - s)))).astype(o_ref.dtype)
        o_ref[1] = (d * g * s).astype(o_ref.dtype)

    blk = (None, ROW_TILE, fb)
    out = pl.pallas_call(
        body, name=name, grid=(T // ROW_TILE, FF_HALF),
        in_specs=[pl.BlockSpec(blk, lambda i, j: (j, i, 0)), pl.BlockSpec(blk, lambda i, j: (j, i, 0)),
                  pl.BlockSpec(blk, lambda i, j: (j + FF_HALF, i, 0))],
        out_specs=pl.BlockSpec((2, None, ROW_TILE, fb), lambda i, j: (0, j, i, 0)),
        out_shape=jax.ShapeDtypeStruct((2, FF_HALF, T, fb), BF16), compiler_params=_params(("parallel", "parallel")),
    )(da, p, p)
    return out.reshape(2 * FF_HALF, T, fb)


COL_GATE_A = 7
COL_GATE_B = 8


def _merge_fwd(ya, yb, proj, *, name):
    T, D = ya.shape

    def body(ya_ref, yb_ref, ga_ref, gb_ref, o_ref):
        o_ref[...] = (_sigmoid(ga_ref[...]) * ya_ref[...] + _sigmoid(gb_ref[...]) * yb_ref[...]).astype(o_ref.dtype)

    row = pl.BlockSpec((ROW_TILE, D), lambda i: (i, 0))
    return pl.pallas_call(
        body, name=name, grid=(T // ROW_TILE,),
        in_specs=[row, row, pl.BlockSpec((ROW_TILE, D), lambda i: (i, COL_GATE_A)),
                  pl.BlockSpec((ROW_TILE, D), lambda i: (i, COL_GATE_B))],
        out_specs=row, out_shape=jax.ShapeDtypeStruct((T, D), BF16), compiler_params=_params(("parallel",)),
    )(ya, yb, proj, proj)


def _merge_bwd(dm, ya, yb, proj, *, name):
    T, D = ya.shape

    def body(dm_ref, ya_ref, yb_ref, ga_ref, gb_ref, dya_ref, dyb_ref, dga_ref, dgb_ref):
        d = dm_ref[...].astype(F32)
        sa = _sigmoid(ga_ref[...])
        sb = _sigmoid(gb_ref[...])
        dya_ref[...] = (d * sa).astype(BF16)
        dyb_ref[...] = (d * sb).astype(BF16)
        dga_ref[...] = (d * ya_ref[...] * sa * (1.0 - sa)).astype(BF16)
        dgb_ref[...] = (d * yb_ref[...] * sb * (1.0 - sb)).astype(BF16)

    row = pl.BlockSpec((ROW_TILE, D), lambda i: (i, 0))
    out = jax.ShapeDtypeStruct((T, D), BF16)
    return pl.pallas_call(
        body, name=name, grid=(T // ROW_TILE,),
        in_specs=[row, row, row, pl.BlockSpec((ROW_TILE, D), lambda i: (i, COL_GATE_A)),
                  pl.BlockSpec((ROW_TILE, D), lambda i: (i, COL_GATE_B))],
        out_specs=(row, row, row, row), out_shape=(out, out, out, out), compiler_params=_params(("parallel",)),
    )(dm, ya, yb, proj, proj)


def _loss_head(y, target, *, name):
    T, D = y.shape

    def body(y_ref, t_ref, loss_ref, dy_ref):
        err = y_ref[...] - t_ref[...]
        dy_ref[...] = err * (1.0 / D)

        @pl.when(pl.program_id(0) == 0)
        def _():
            loss_ref[...] = jnp.zeros_like(loss_ref)

        loss_ref[...] += 0.5 * jnp.sum(jnp.sum(err * err, axis=-1, keepdims=True) * (1.0 / D), axis=0, keepdims=True)

    row = pl.BlockSpec((ROW_TILE, D), lambda i: (i, 0))
    return pl.pallas_call(
        body, name=name, grid=(T // ROW_TILE,), in_specs=[row, row],
        out_specs=(pl.BlockSpec((1, 128), lambda i: (0, 0)), row),
        out_shape=(jax.ShapeDtypeStruct((1, 128), F32), jax.ShapeDtypeStruct((T, D), F32)),
        compiler_params=_params(("arbitrary",)),
    )(y, target)


CONV_PAD = 8


def _conv_taps(w, xp, T, first):
    acc = w[0:1, :] * xp[pl.ds(first, T), :]
    for i in range(1, DN_CONV):
        acc = acc + w[i:i + 1, :] * xp[pl.ds(first + i, T), :]
    return acc


def _conv_fwd(proj, conv_w, *, name):
    T = proj.shape[0]

    def body(x_ref, w_ref, o_ref, xp):
        xp[0:CONV_PAD, :] = jnp.zeros((CONV_PAD, HEAD_DIM), F32)
        xp[CONV_PAD:, :] = x_ref[...]
        y = _conv_taps(w_ref[...], xp, T, CONV_PAD - (DN_CONV - 1))
        s = y * _sigmoid(y)
        n = s * lax.rsqrt(jnp.sum(s * s, axis=-1, keepdims=True) + L2_EPS)
        o_ref[0] = jnp.where(pl.program_id(0) < 2, n, s)

    return pl.pallas_call(
        body, name=name, grid=(3, HEADS),
        in_specs=[pl.BlockSpec((T, HEAD_DIM), lambda c, h: (0, c * HEADS + h)),
                  pl.BlockSpec((DN_CONV, HEAD_DIM), lambda c, h: (0, c * HEADS + h))],
        out_specs=pl.BlockSpec((1, T, HEAD_DIM), lambda c, h: (c, 0, h)),
        out_shape=jax.ShapeDtypeStruct((3, T, D_MODEL), F32),
        scratch_shapes=[pltpu.VMEM((T + CONV_PAD, HEAD_DIM), F32)],
        compiler_params=_params(("parallel", "parallel")),
    )(proj, conv_w)


def _conv_bwd(dqkv, proj, conv_w, *, name):
    T = proj.shape[0]

    def body(d_ref, x_ref, w_ref, dx_ref, dw_ref, xp, dyp):
        xp[0:CONV_PAD, :] = jnp.zeros((CONV_PAD, HEAD_DIM), F32)
        xp[CONV_PAD:, :] = x_ref[...]
        w = w_ref[...]
        y = _conv_taps(w, xp, T, CONV_PAD - (DN_CONV - 1))
        sg = _sigmoid(y)
        s = y * sg
        r = lax.rsqrt(jnp.sum(s * s, axis=-1, keepdims=True) + L2_EPS)
        n = s * r
        d = d_ref[0]
        ds = jnp.where(pl.program_id(0) < 2, r * (d - n * jnp.sum(d * n, axis=-1, keepdims=True)), d)
        dy = ds * (sg * (1.0 + y * (1.0 - sg)))
        dyp[0:T, :] = dy
        dyp[T:, :] = jnp.zeros((CONV_PAD, HEAD_DIM), F32)
        dx = w[0:1, :] * dyp[pl.ds(DN_CONV - 1, T), :]
        for i in range(1, DN_CONV):
            dx = dx + w[i:i + 1, :] * dyp[pl.ds(DN_CONV - 1 - i, T), :]
        dx_ref[...] = dx.astype(dx_ref.dtype)
        for i in range(DN_CONV):
            dw_ref[i:i + 1, :] = jnp.sum(dy * xp[pl.ds(CONV_PAD - (DN_CONV - 1) + i, T), :], axis=0, keepdims=True)

    col = lambda c, h: (0, c * HEADS + h)
    return pl.pallas_call(
        body, name=name, grid=(3, HEADS),
        in_specs=[pl.BlockSpec((1, T, HEAD_DIM), lambda c, h: (c, 0, h)), pl.BlockSpec((T, HEAD_DIM), col),
                  pl.BlockSpec((DN_CONV, HEAD_DIM), col)],
        out_specs=(pl.BlockSpec((T, HEAD_DIM), col), pl.BlockSpec((DN_CONV, HEAD_DIM), col)),
        out_shape=(jax.ShapeDtypeStruct((T, 3 * D_MODEL), BF16), jax.ShapeDtypeStruct((DN_CONV, 3 * D_MODEL), F32)),
        scratch_shapes=[pltpu.VMEM((T + CONV_PAD, HEAD_DIM), F32), pltpu.VMEM((T + CONV_PAD, HEAD_DIM), F32)],
        compiler_params=_params(("parallel", "parallel")),
    )(dqkv, proj, conv_w)


def _inv_unit_lower(low, eye):
    x = eye - low
    power = _hdot(low, low, _B_NN)
    steps = int(math.log2(DN_CHUNK)) - 1
    for s in range(steps):
        x = x + _hdot(x, power, _B_NN)
        if s + 1 < steps:
            power = _hdot(power, power, _B_NN)
    return x


_B_NN = (((2,), (1,)), ((0,), (0,)))
_B_NT = (((2,), (2,)), ((0,), (0,)))
_B_TN = (((1,), (1,)), ((0,), (0,)))


def _dn_load(ref, lead, r0, group):
    rows = pl.ds(r0, group * DN_CHUNK)
    cols = lambda h: slice(h * HEAD_DIM, (h + 1) * HEAD_DIM)
    per_head = [(ref[rows, cols(h)] if lead is None else ref[lead, rows, cols(h)]).reshape(group, DN_CHUNK, HEAD_DIM)
                for h in range(DN_HEADS)]
    return jnp.stack(per_head, axis=1).reshape(group * DN_HEADS, DN_CHUNK, HEAD_DIM)


def _dn_chunk_setup(qkv_ref, b_ref, a_ref, hp_ref, n0, group, tinv=None):
    C = DN_CHUNK
    B = group * DN_HEADS
    r0 = pl.multiple_of(n0 * C, C)
    q = _dn_load(qkv_ref, 0, r0, group) * QK_SCALE
    k = _dn_load(qkv_ref, 1, r0, group)
    v = _dn_load(qkv_ref, 2, r0, group)
    ii = lax.broadcasted_iota(jnp.int32, (B, C, C), 1)
    jj = lax.broadcasted_iota(jnp.int32, (B, C, C), 2)
    eye_mask = ii == jj
    eye = jnp.where(eye_mask, 1.0, 0.0).astype(F32)

    def to_col(row):
        return jnp.sum(jnp.where(eye_mask, jnp.broadcast_to(row, (B, C, C)), 0.0), axis=2, keepdims=True)

    def to_row(col):
        return jnp.sum(jnp.where(eye_mask, jnp.broadcast_to(col, (B, C, C)), 0.0), axis=1, keepdims=True)

    def rows(ref):
        return jnp.stack([ref[h, pl.ds(n0, group)] for h in range(DN_HEADS)], axis=1).reshape(B, 1, C)

    def per_head(row):
        return jnp.stack([hp_ref[h, row:row + 1, 0:C] for h in range(DN_HEADS)] * group, axis=0)

    b_row = rows(b_ref)
    a_row = rows(a_ref)
    a_log = per_head(0)
    dt_b = per_head(1)
    beta_row = _sigmoid(b_row)
    neg_ea = -jnp.exp(a_log)
    g_row = neg_ea * _softplus(a_row + dt_b)
    gc_col = jnp.sum(jnp.where(jj <= ii, jnp.broadcast_to(g_row, (B, C, C)), 0.0), axis=2, keepdims=True)
    gc_row = to_row(gc_col)
    g_last = jnp.sum(g_row, axis=2, keepdims=True)
    beta = to_col(beta_row)
    low_incl = ii >= jj
    decay = jnp.exp(jnp.where(low_incl, gc_col - gc_row, -jnp.inf))
    eg = jnp.exp(gc_col)
    egl = jnp.exp(g_last - gc_col)
    el = jnp.exp(g_last)
    kb = k * beta
    pmat = _bdot(kb, k, _B_NT)
    low = jnp.where(ii > jj, pmat * decay, 0.0)
    if tinv is None:
        tinv = _inv_unit_lower(low, eye)
    u = _hdot(tinv, v * beta, _B_NN)
    w = _hdot(tinv, kb * eg, _B_NN)
    qk = _bdot(q, k, _B_NT)
    attn = qk * decay
    return dict(q=q, k=k, v=v, ii=ii, jj=jj, to_col=to_col, to_row=to_row, b_row=b_row, a_row=a_row, dt_b=dt_b,
                beta_row=beta_row, neg_ea=neg_ea, g_row=g_row, gc_col=gc_col, g_last=g_last, beta=beta,
                decay=decay, eg=eg, egl=egl, el=el, kb=kb, pmat=pmat, tinv=tinv, u=u, w=w, qk=qk, attn=attn,
                qd=q * eg, kd=k * egl, r0=r0)


def _dn_store(ref, lead, r0, group, value):
    value = value.reshape(group, DN_HEADS, DN_CHUNK, HEAD_DIM)
    for h in range(DN_HEADS):
        block = value[:, h].reshape(group * DN_CHUNK, HEAD_DIM)
        if lead is None:
            ref[pl.ds(r0, group * DN_CHUNK), h * HEAD_DIM:(h + 1) * HEAD_DIM] = block
        else:
            ref[lead, pl.ds(r0, group * DN_CHUNK), h * HEAD_DIM:(h + 1) * HEAD_DIM] = block


def _dn_specs(T):
    nc = T // DN_CHUNK
    qkv = pl.BlockSpec((3, T, DN_HEADS * HEAD_DIM), lambda h: (0, 0, h))
    rows = pl.BlockSpec((DN_HEADS, nc, 1, DN_CHUNK), lambda h: (h, 0, 0, 0))
    hp = pl.BlockSpec((DN_HEADS, 8, 128), lambda h: (h, 0, 0))
    states = pl.BlockSpec((DN_HEADS, nc, HEAD_DIM, HEAD_DIM), lambda h: (h, 0, 0, 0))
    return nc, qkv, rows, hp, states


def _dn_inverse_spec(T):
    return pl.BlockSpec((DN_HEADS, T // DN_CHUNK, DN_CHUNK, DN_CHUNK), lambda h: (h, 0, 0, 0))


def _dn_per_head(ref, n0, group):
    stacked = jnp.stack([ref[h, pl.ds(n0, group)] for h in range(DN_HEADS)], axis=1)
    return stacked.reshape((group * DN_HEADS,) + stacked.shape[2:])


def _dn_fwd(qkv, b_rows, a_rows, hp, *, name):
    T = qkv.shape[1]
    nc, qkv_spec, row_spec, hp_spec, st_spec = _dn_specs(T)
    group = math.gcd(nc, DN_GROUP)
    H = DN_HEADS

    def body(qkv_ref, b_ref, a_ref, hp_ref, o_ref, st_ref, inv_ref, s_scr):
        s_scr[...] = jnp.zeros_like(s_scr)

        def step(t, carry):
            n0 = t * group
            c = _dn_chunk_setup(qkv_ref, b_ref, a_ref, hp_ref, n0, group)
            tinv = c["tinv"].reshape(group, H, DN_CHUNK, DN_CHUNK)
            for h in range(H):
                inv_ref[h, pl.ds(n0, group)] = tinv[:, h]
            state = s_scr[...]
            outs = []
            for g in range(group):
                sl = slice(g * H, (g + 1) * H)
                for h in range(H):
                    st_ref[h, n0 + g] = state[h]
                v_new = c["u"][sl] - _bdot(c["w"][sl], state, _B_NN)
                outs.append(_bdot(c["qd"][sl], state, _B_NN) + _bdot(c["attn"][sl], v_new, _B_NN))
                state = state * c["el"][sl] + _bdot(c["kd"][sl], v_new, _B_TN)
            s_scr[...] = state
            _dn_store(o_ref, None, c["r0"], group, jnp.concatenate(outs, axis=0))
            return carry

        lax.fori_loop(0, nc // group, step, 0)

    return pl.pallas_call(
        body, name=name, grid=(HEADS // H,), in_specs=[qkv_spec, row_spec, row_spec, hp_spec],
        out_specs=(pl.BlockSpec((T, H * HEAD_DIM), lambda h: (0, h)), st_spec, _dn_inverse_spec(T)),
        out_shape=(jax.ShapeDtypeStruct((T, D_MODEL), F32),
                   jax.ShapeDtypeStruct((HEADS, nc, HEAD_DIM, HEAD_DIM), F32),
                   jax.ShapeDtypeStruct((HEADS, nc, DN_CHUNK, DN_CHUNK), F32)),
        scratch_shapes=[pltpu.VMEM((H, HEAD_DIM, HEAD_DIM), F32)], compiler_params=_params(("parallel",)),
    )(qkv, b_rows, a_rows, hp)


def _dn_bwd(qkv, b_rows, a_rows, hp, states, inverses, do, *, name):
    T = qkv.shape[1]
    C = DN_CHUNK
    nc, qkv_spec, row_spec, hp_spec, st_spec = _dn_specs(T)
    group = math.gcd(nc, DN_GROUP)
    H = DN_HEADS
    B = group * H

    def body(qkv_ref, b_ref, a_ref, hp_ref, st_ref, inv_ref, do_ref, dqkv_ref, db_ref, da_ref, dhp_ref, ds_scr, acc_scr):
        ds_scr[...] = jnp.zeros_like(ds_scr)
        acc_scr[...] = jnp.zeros_like(acc_scr)

        def step(t, carry):
            n0 = nc - (t + 1) * group
            c = _dn_chunk_setup(qkv_ref, b_ref, a_ref, hp_ref, n0, group, tinv=_dn_per_head(inv_ref, n0, group))
            state = _dn_per_head(st_ref, n0, group)
            d_o = _dn_load(do_ref, None, c["r0"], group)
            v_new = c["u"] - _bdot(c["w"], state, _B_NN)
            d_vnew_local = _bdot(c["attn"], d_o, _B_TN)
            d_state_local = _bdot(c["qd"], d_o, _B_TN)
            d_state = ds_scr[...]
            d_vnew, d_kd, d_el = [None] * group, [None] * group, [None] * group
            for g in reversed(range(group)):
                sl = slice(g * H, (g + 1) * H)
                d_vnew[g] = d_vnew_local[sl] + _bdot(c["kd"][sl], d_state, _B_NN)
                d_kd[g] = _bdot(v_new[sl], d_state, _B_NT)
                d_el[g] = jnp.sum(jnp.sum(d_state * state[sl], axis=2, keepdims=True), axis=1, keepdims=True)
                d_state = d_state * c["el"][sl] + d_state_local[sl] - _bdot(c["w"][sl], d_vnew[g], _B_TN)
            ds_scr[...] = d_state
            chunk_grads(c, n0, state, d_o, v_new, jnp.concatenate(d_vnew, axis=0), jnp.concatenate(d_kd, axis=0),
                        jnp.concatenate(d_el, axis=0))
            return carry

        def chunk_grads(c, n0, state, d_o, v_new, d_vnew, d_kd, d_el):
            ii, jj = c["ii"], c["jj"]
            q, k, v, kb, beta = c["q"], c["k"], c["v"], c["kb"], c["beta"]
            decay, eg, egl, el = c["decay"], c["eg"], c["egl"], c["el"]
            u, w, tinv = c["u"], c["w"], c["tinv"]
            d_qd = _bdot(d_o, state, _B_NT)
            d_attn = _bdot(d_o, v_new, _B_NT)
            d_w = -_bdot(d_vnew, state, _B_NT)
            d_rv = _hdot(tinv, d_vnew, _B_TN)
            d_rw = _hdot(tinv, d_w, _B_TN)
            d_amat = -(_bdot(d_rv, u, _B_NT) + _bdot(d_rw, w, _B_NT))
            d_low = jnp.where(ii > jj, d_amat, 0.0)
            d_p = d_low * decay
            d_qk = d_attn * decay
            e_mat = (d_low * c["pmat"] + d_attn * c["qk"]) * decay
            d_q = _bdot(d_qk, k, _B_NN) + d_qd * eg
            d_kb = _bdot(d_p, k, _B_NN) + d_rw * eg
            d_k = _bdot(d_qk, q, _B_TN) + _bdot(d_p, kb, _B_TN) + d_kd * egl + d_kb * beta
            d_beta = jnp.sum(d_kb * k, axis=2, keepdims=True) + jnp.sum(d_rv * v, axis=2, keepdims=True)
            d_v = d_rv * beta
            d_eg = jnp.sum(d_qd * q, axis=2, keepdims=True) + jnp.sum(d_rw * kb, axis=2, keepdims=True)
            d_egl = jnp.sum(d_kd * k, axis=2, keepdims=True)
            d_glast = jnp.sum(d_egl * egl, axis=1, keepdims=True) + d_el * el
            row_sum = jnp.sum(e_mat, axis=2, keepdims=True)
            col_sum = c["to_col"](jnp.sum(e_mat, axis=1, keepdims=True))
            d_gc = row_sum - col_sum + d_eg * eg - d_egl * egl
            d_g_row = jnp.sum(jnp.where(ii >= jj, jnp.broadcast_to(d_gc, (B, C, C)), 0.0), axis=1, keepdims=True) + d_glast
            beta_row = c["beta_row"]
            d_b_row = c["to_row"](d_beta) * beta_row * (1.0 - beta_row)
            d_a_row = d_g_row * c["neg_ea"] * _sigmoid(c["a_row"] + c["dt_b"])
            _dn_store(dqkv_ref, 0, c["r0"], group, d_q * QK_SCALE)
            _dn_store(dqkv_ref, 1, c["r0"], group, d_k)
            _dn_store(dqkv_ref, 2, c["r0"], group, d_v)
            d_b_row = d_b_row.reshape(group, H, 1, C)
            d_a_row = d_a_row.reshape(group, H, 1, C)
            d_a_log = jnp.sum((d_g_row * c["g_row"]).reshape(group, H, 1, C), axis=0)
            d_dt_b = jnp.sum(d_a_row, axis=0)
            for h in range(H):
                db_ref[h, pl.ds(n0, group)] = d_b_row[:, h]
                da_ref[h, pl.ds(n0, group)] = d_a_row[:, h]
                acc_scr[h, 0:1, 0:C] += d_a_log[h]
                acc_scr[h, 1:2, 0:C] += d_dt_b[h]

        lax.fori_loop(0, nc // group, step, 0)
        for h in range(H):
            tot = jnp.sum(acc_scr[h], axis=1, keepdims=True)
            dhp_ref[h] = jnp.broadcast_to(tot, (8, 128))

    return pl.pallas_call(
        body, name=name, grid=(HEADS // H,),
        in_specs=[qkv_spec, row_spec, row_spec, hp_spec, st_spec, _dn_inverse_spec(T),
                  pl.BlockSpec((T, H * HEAD_DIM), lambda h: (0, h))],
        out_specs=(qkv_spec, row_spec, row_spec, hp_spec),
        out_shape=(jax.ShapeDtypeStruct((3, T, D_MODEL), F32), jax.ShapeDtypeStruct((HEADS, nc, 1, C), F32),
                   jax.ShapeDtypeStruct((HEADS, nc, 1, C), F32), jax.ShapeDtypeStruct((HEADS, 8, 128), F32)),
        scratch_shapes=[pltpu.VMEM((H, HEAD_DIM, HEAD_DIM), F32), pltpu.VMEM((H, 8, 128), F32)],
        compiler_params=_params(("parallel",)),
    )(qkv, b_rows, a_rows, hp, states, inverses, do)


COL_Z = 3 * HEADS


def _gated_norm_fwd(o, proj, gain, *, name):
    T = o.shape[0]

    def body(o_ref, z_ref, g_ref, out_ref):
        x = o_ref[...]
        r = lax.rsqrt(jnp.mean(x * x, axis=-1, keepdims=True) + RMS_EPS)
        z = z_ref[...]
        out_ref[...] = (x * r * g_ref[...] * (z * _sigmoid(z))).astype(out_ref.dtype)

    return pl.pallas_call(
        body, name=name, grid=(HEADS,),
        in_specs=[pl.BlockSpec((T, HEAD_DIM), lambda h: (0, h)), pl.BlockSpec((T, HEAD_DIM), lambda h: (0, COL_Z + h)),
                  pl.BlockSpec((1, HEAD_DIM), lambda h: (0, 0))],
        out_specs=pl.BlockSpec((T, HEAD_DIM), lambda h: (0, h)),
        out_shape=jax.ShapeDtypeStruct((T, D_MODEL), BF16), compiler_params=_params(("parallel",)),
    )(o, proj, gain)


def _gated_norm_bwd(dout, o, proj, gain, *, name):
    T = o.shape[0]

    def body(d_ref, o_ref, z_ref, g_ref, do_ref, dz_ref, dg_ref):
        x = o_ref[...]
        r = lax.rsqrt(jnp.mean(x * x, axis=-1, keepdims=True) + RMS_EPS)
        n = x * r
        z = z_ref[...]
        sg = _sigmoid(z)
        d = d_ref[...].astype(F32)
        g = g_ref[...]
        dz_ref[...] = (d * n * g * (sg * (1.0 + z * (1.0 - sg)))).astype(dz_ref.dtype)
        dy = d * (z * sg)
        dyg = dy * g
        do_ref[...] = r * (dyg - n * jnp.mean(dyg * n, axis=-1, keepdims=True))

        @pl.when(pl.program_id(0) == 0)
        def _():
            dg_ref[...] = jnp.zeros_like(dg_ref)

        dg_ref[...] += jnp.sum(dy * n, axis=0, keepdims=True)

    head = pl.BlockSpec((T, HEAD_DIM), lambda h: (0, h))
    vec = pl.BlockSpec((1, HEAD_DIM), lambda h: (0, 0))
    return pl.pallas_call(
        body, name=name, grid=(HEADS,),
        in_specs=[head, head, pl.BlockSpec((T, HEAD_DIM), lambda h: (0, COL_Z + h)), vec],
        out_specs=(head, head, vec),
        out_shape=(jax.ShapeDtypeStruct((T, D_MODEL), F32), jax.ShapeDtypeStruct((T, D_MODEL), BF16),
                   jax.ShapeDtypeStruct((1, HEAD_DIM), F32)),
        compiler_params=_params(("arbitrary",)),
    )(dout, o, proj, gain)


COL_SBQ = 4 * HEADS
COL_SBK = 5 * HEADS
COL_SBV = 6 * HEADS


def _split_dot(x, mat):
    lead = x.shape[:-1]
    x = x.reshape(-1, x.shape[-1])
    hi = x.astype(BF16)
    lo = (x - hi.astype(F32)).astype(BF16)
    out = jnp.dot(hi, mat, preferred_element_type=F32) + jnp.dot(lo, mat, preferred_element_type=F32)
    return out.reshape(lead + (mat.shape[-1],))


def _sb_specs(T, heads):
    col = lambda first: pl.BlockSpec((T, heads * HEAD_DIM), lambda h: (0, first // heads + h))
    return col(COL_SBQ), col(COL_SBK), col(COL_SBV), pl.BlockSpec((1, HEAD_DIM), lambda h: (0, 0))


def _heads_first(x):
    return jnp.stack([x[:, c:c + HEAD_DIM] for c in range(0, x.shape[1], HEAD_DIM)], axis=0)


def _heads_last(x):
    return jnp.concatenate([x[h] for h in range(x.shape[0])], axis=1)


def _head_rms(x):
    r = lax.rsqrt(jnp.mean(x * x, axis=-1, keepdims=True) + RMS_EPS)
    return x * r, r


def _sb_fwd(proj, q_gain, k_gain, *, name):
    T = proj.shape[0]
    B = SB_BLOCK
    H = SB_HEADS
    nb = T // B
    KT = min(SB_KEY_TILE, T)
    NS = KT // B
    q_spec, k_spec, v_spec, g_spec = _sb_specs(T, H)

    def body(q_ref, k_ref, v_ref, gq_ref, gk_ref, o_ref, lt_ref, qs, ks, vs):
        qs[...] = (_head_rms(_heads_first(q_ref[...]))[0] * (gq_ref[...] * QK_SCALE)).astype(BF16)
        ks[...] = (_head_rms(_heads_first(k_ref[...]))[0] * gk_ref[...]).astype(BF16)
        vs[...] = _heads_first(v_ref[...]).astype(BF16)
        ii = lax.broadcasted_iota(jnp.int32, (B, B), 0)
        jj = lax.broadcasted_iota(jnp.int32, (B, B), 1)
        after = jnp.where(ii > jj, 1.0, 0.0).astype(BF16)
        ahead = lax.broadcasted_iota(jnp.int32, (H, B, KT), 2) - lax.broadcasted_iota(jnp.int32, (H, B, KT), 1)

        def q_block(i, carry):
            rows = pl.ds(pl.multiple_of(i * B, B), B)
            q = qs[:, rows, :]

            def tile(c0, acc, tail, masked):
                cols = pl.ds(c0, KT)
                z = lax.dot_general(q, ks[:, cols, :], _B_NT, preferred_element_type=F32)
                sp = _softplus(z)
                causal = ahead < (i * B - c0)
                loss = jnp.where(causal, sp, 0.0) if masked else sp
                parts = [None] * NS
                for b in reversed(range(NS)):
                    blk = loss[:, :, b * B:(b + 1) * B]
                    parts[b] = _split_dot(blk, after) + tail
                    tail = tail + jnp.sum(blk, axis=2, keepdims=True)
                lost = parts[0] if NS == 1 else jnp.concatenate(parts, axis=2)
                wts = jnp.exp(z - sp - lost)
                if masked:
                    wts = jnp.where(causal, wts, 0.0)
                acc = acc + lax.dot_general(wts.astype(BF16), vs[:, cols, :], _B_NN, preferred_element_type=F32)
                return acc, tail

            last = i // NS
            acc, tail = tile(pl.multiple_of(last * KT, KT), jnp.zeros((H, B, HEAD_DIM), F32), jnp.zeros((H, B, 1), F32), True)
            acc, tail = lax.fori_loop(
                1, last + 1, lambda s, c: tile(pl.multiple_of((last - s) * KT, KT), c[0], c[1], False), (acc, tail))
            o_ref[rows, :] = _heads_last(acc).astype(o_ref.dtype)
            lt_ref[rows, :] = _heads_last(jnp.broadcast_to(tail, (H, B, HEAD_DIM)))
            return carry

        lax.fori_loop(0, nb, q_block, 0)

    heads = pl.BlockSpec((T, H * HEAD_DIM), lambda h: (0, h))
    return pl.pallas_call(
        body, name=name, grid=(HEADS // H,), in_specs=[q_spec, k_spec, v_spec, g_spec, g_spec],
        out_specs=(heads, heads),
        out_shape=(jax.ShapeDtypeStruct((T, D_MODEL), BF16), jax.ShapeDtypeStruct((T, D_MODEL), F32)),
        scratch_shapes=[pltpu.VMEM((H, T, HEAD_DIM), BF16)] * 3, compiler_params=_params(("parallel",)),
    )(proj, proj, proj, q_gain, k_gain)


def _sb_bwd(proj, q_gain, k_gain, ltot, do, *, name):
    T = proj.shape[0]
    B = SB_BLOCK
    H = SB_HEADS_BWD
    nb = T // B
    KT = min(SB_KEY_TILE_BWD, T)
    NS = KT // B
    q_spec, k_spec, v_spec, g_spec = _sb_specs(T, H)

    def body(q_ref, k_ref, v_ref, gq_ref, gk_ref, lt_ref, do_ref, dq_ref, dk_ref, dv_ref, dgq_ref, dgk_ref,
             qs, ks, vs, dos, lts, dq_acc, dk_acc, dv_acc):
        qn, q_r = _head_rms(_heads_first(q_ref[...]))
        kn, k_r = _head_rms(_heads_first(k_ref[...]))
        qs[...] = (qn * (gq_ref[...] * QK_SCALE)).astype(BF16)
        ks[...] = (kn * gk_ref[...]).astype(BF16)
        vs[...] = _heads_first(v_ref[...]).astype(BF16)
        dos[...] = _heads_first(do_ref[...]).astype(BF16)
        lts[...] = _heads_first(lt_ref[...])
        dk_acc[...] = jnp.zeros_like(dk_acc)
        dv_acc[...] = jnp.zeros_like(dv_acc)
        ii = lax.broadcasted_iota(jnp.int32, (B, B), 0)
        jj = lax.broadcasted_iota(jnp.int32, (B, B), 1)
        upto = jnp.where(ii <= jj, 1.0, 0.0).astype(BF16)
        before = jnp.where(ii < jj, 1.0, 0.0).astype(BF16)
        ahead = lax.broadcasted_iota(jnp.int32, (H, B, KT), 2) - lax.broadcasted_iota(jnp.int32, (H, B, KT), 1)

        def q_block(i, carry):
            rows = pl.ds(pl.multiple_of(i * B, B), B)
            q = qs[:, rows, :]
            d_o = dos[:, rows, :]
            total = jnp.max(lts[:, rows, :], axis=2, keepdims=True)

            def tile(c0, dq, head_lb, head_de, masked):
                cols = pl.ds(c0, KT)
                k = ks[:, cols, :]
                v = vs[:, cols, :]
                z = lax.dot_general(q, k, _B_NT, preferred_element_type=F32)
                sp = _softplus(z)
                causal = ahead < (i * B - c0)
                loss = jnp.where(causal, sp, 0.0) if masked else sp
                parts = [None] * NS
                for b in range(NS):
                    blk = loss[:, :, b * B:(b + 1) * B]
                    parts[b] = _split_dot(blk, upto) + head_lb
                    head_lb = head_lb + jnp.sum(blk, axis=2, keepdims=True)
                prefix = parts[0] if NS == 1 else jnp.concatenate(parts, axis=2)
                wts = jnp.exp(z - sp + (prefix - total))
                if masked:
                    wts = jnp.where(causal, wts, 0.0)
                d_w = lax.dot_general(d_o, v, _B_NT, preferred_element_type=F32)
                d_e = wts * d_w
                d_eb = d_e.astype(BF16)
                for b in range(NS):
                    inside = jnp.dot(d_eb[:, :, b * B:(b + 1) * B].reshape(H * B, B), before, preferred_element_type=F32)
                    parts[b] = inside.reshape(H, B, B) + head_de
                    head_de = head_de + jnp.sum(d_e[:, :, b * B:(b + 1) * B], axis=2, keepdims=True)
                cum = parts[0] if NS == 1 else jnp.concatenate(parts, axis=2)
                sig = jnp.exp(z - sp)
                d_z = d_e - sig * (d_e + cum)
                if masked:
                    d_z = jnp.where(causal, d_z, 0.0)
                d_zb = d_z.astype(BF16)
                dq = dq + lax.dot_general(d_zb, k, _B_NN, preferred_element_type=F32)
                dk_acc[:, cols, :] += lax.dot_general(d_zb, q, _B_TN, preferred_element_type=F32)
                dv_acc[:, cols, :] += lax.dot_general(wts.astype(BF16), d_o, _B_TN, preferred_element_type=F32)
                return dq, head_lb, head_de

            last = i // NS
            zero = jnp.zeros((H, B, 1), F32)
            state = lax.fori_loop(0, last, lambda t, c: tile(pl.multiple_of(t * KT, KT), *c, False),
                                  (jnp.zeros((H, B, HEAD_DIM), F32), zero, zero))
            dq, _, _ = tile(pl.multiple_of(last * KT, KT), *state, True)
            dq_acc[:, rows, :] = dq * QK_SCALE
            return carry

        lax.fori_loop(0, nb, q_block, 0)

        def norm_bwd(d_scaled, n, r, gain):
            dn = d_scaled * gain
            d_gain = jnp.sum(jnp.sum(d_scaled * n, axis=1, keepdims=True), axis=0)
            return r * (dn - n * jnp.mean(dn * n, axis=-1, keepdims=True)), d_gain

        dq_raw, dgq = norm_bwd(dq_acc[...], qn, q_r, gq_ref[...])
        dk_raw, dgk = norm_bwd(dk_acc[...], kn, k_r, gk_ref[...])
        dq_ref[...] = _heads_last(dq_raw).astype(dq_ref.dtype)
        dk_ref[...] = _heads_last(dk_raw).astype(dk_ref.dtype)
        dv_ref[...] = _heads_last(dv_acc[...]).astype(dv_ref.dtype)

        @pl.when(pl.program_id(0) == 0)
        def _():
            dgq_ref[...] = jnp.zeros_like(dgq_ref)
            dgk_ref[...] = jnp.zeros_like(dgk_ref)

        dgq_ref[...] += dgq
        dgk_ref[...] += dgk

    heads = pl.BlockSpec((T, H * HEAD_DIM), lambda h: (0, h))
    out = jax.ShapeDtypeStruct((T, D_MODEL), BF16)
    vec = jax.ShapeDtypeStruct((1, HEAD_DIM), F32)
    return pl.pallas_call(
        body, name=name, grid=(HEADS // H,), in_specs=[q_spec, k_spec, v_spec, g_spec, g_spec, heads, heads],
        out_specs=(heads, heads, heads, g_spec, g_spec), out_shape=(out, out, out, vec, vec),
        scratch_shapes=[pltpu.VMEM((H, T, HEAD_DIM), BF16)] * 4 + [pltpu.VMEM((H, T, HEAD_DIM), F32)] * 4,
        compiler_params=_params(("arbitrary",)),
    )(proj, proj, proj, q_gain, k_gain, ltot, do)


ADAM_ROWS = 256


def _adamw(g_parts, w, m, v, *, name, layer=0, earlier=None):
    K, A, C = g_parts.shape
    R = w.shape[0]
    tr = next((t for t in (ADAM_ROWS, ADAM_ROWS // 2) if A % t == 0), A // 2 if A % 32 == 0 else A)
    first_block = layer * (A // tr)

    def body(g_ref, w_ref, m_ref, v_ref, *rest):
        go_ref, d_ref, mo_ref, vo_ref = rest[-4:]
        g = g_ref[0].astype(F32)
        for k in range(1, K):
            g = g + g_ref[k].astype(F32)
        go_ref[...] = g
        m_new = ADAM_B1 * m_ref[...] + (1.0 - ADAM_B1) * g
        v_new = ADAM_B2 * v_ref[...] + (1.0 - ADAM_B2) * (g * g)
        m_hat = m_new / (1.0 - ADAM_B1 ** ADAM_STEP)
        v_hat = v_new / (1.0 - ADAM_B2 ** ADAM_STEP)
        d_ref[...] = -ADAM_LR * (m_hat / (jnp.sqrt(v_hat) + ADAM_EPS) + ADAM_WD * w_ref[...])
        mo_ref[...] = m_new
        vo_ref[...] = v_new

    row = pl.BlockSpec((tr, C), lambda i: (first_block + i, 0))
    out = jax.ShapeDtypeStruct((R, C), F32)
    in_specs = [pl.BlockSpec((K, tr, C), lambda i: (0, i, 0)), row, row, row]
    if earlier is None:
        return pl.pallas_call(
            body, name=name, grid=(A // tr,), in_specs=in_specs, out_specs=(row, row, row, row),
            out_shape=(out, out, out, out), compiler_params=_params(("parallel",)),
        )(g_parts, w, m, v)
    return pl.pallas_call(
        body, name=name, grid=(A // tr,), in_specs=in_specs + [ANY] * 4, out_specs=(row, row, row, row),
        out_shape=(out, out, out, out), input_output_aliases={4 + j: j for j in range(4)},
        compiler_params=_params(("parallel",)),
    )(g_parts, w, m, v, *earlier)


def _sum_parts(parts, *, name):
    K, R, C = parts.shape

    def body(p_ref, o_ref):
        acc = p_ref[0]
        for k in range(1, K):
            acc = acc + p_ref[k]
        o_ref[...] = acc

    return pl.pallas_call(body, name=name, out_shape=jax.ShapeDtypeStruct((R, C), F32))(parts)


def _position():
    return lax.axis_index("x"), lax.axis_index("y"), lax.axis_index("c")


def _all_gather(shards, *, name):
    n = len(shards)

    def body(*refs):
        x_refs, out_refs = refs[:n], refs[n:2 * n]
        send_sems, recv_sems, local_sems = refs[2 * n:]
        x, y, c = _position()
        me, sibling = (x, y, c), (x, y, 1 - c)
        chips = [(1 - x, y), (x, 1 - y), (1 - x, 1 - y)]

        def slot(a, px, py, pc):
            return out_refs[a].at[4 * px + 2 * py + pc]

        def copy(a, k, block, to, own=False):
            return pltpu.make_async_remote_copy(
                src_ref=x_refs[a] if own else slot(a, *block), dst_ref=slot(a, *block),
                send_sem=send_sems.at[a, k], recv_sem=recv_sems.at[a, k], device_id=to, device_id_type=MESH)

        mine = [pltpu.make_async_copy(x_refs[a], slot(a, *me), local_sems.at[a]) for a in range(n)]
        for cp in mine:
            cp.start()
        first = [copy(a, 1 + j, me, (*chip, c), own=True) for j, chip in enumerate(chips) for a in range(n)]
        first += [copy(a, 0, me, sibling, own=True) for a in range(n)]
        for cp in first:
            cp.start()
        passed = []
        for j, chip in enumerate(chips):
            for a in range(n):
                copy(a, 1 + j, (*chip, c), me).wait_recv()
                passed.append(copy(a, 4 + j, (*chip, c), sibling))
                passed[-1].start()
        for a in range(n):
            copy(a, 0, sibling, me).wait_recv()
        for j, chip in enumerate(chips):
            for a in range(n):
                copy(a, 4 + j, (*chip, 1 - c), me).wait_recv()
        for cp in first + passed:
            cp.wait_send()
        for cp in mine:
            cp.wait()

    return pl.pallas_call(
        body, name=name, in_specs=[ANY] * n, out_specs=[ANY] * n,
        out_shape=[jax.ShapeDtypeStruct((N_DEV,) + s.shape, s.dtype) for s in shards],
        scratch_shapes=[pltpu.SemaphoreType.DMA((n, 7)), pltpu.SemaphoreType.DMA((n, 7)), pltpu.SemaphoreType.DMA((n,))],
    )(*shards)


HBM = pl.BlockSpec(memory_space=pltpu.HBM)
SEM = pl.BlockSpec(memory_space=pltpu.SEMAPHORE)
DATAFLOW = pltpu.SideEffectType.DATAFLOW_SIDE_EFFECTING


def _exchange_copies(gather, x_refs, land_refs, send_sems, recv_sems, local_sems):
    n = len(x_refs)
    x, y, c = _position()
    me = 4 * x + 2 * y + c

    def src(a, slot):
        return x_refs[a] if gather else x_refs[a].at[slot]

    mine = [pltpu.make_async_copy(src(a, me), land_refs[a].at[me], local_sems.at[a]) for a in range(n)]
    sends, recvs = [], []
    for k in range(1, N_DEV):
        px, py, pc = (x + (k >> 2)) % 2, (y + ((k >> 1) & 1)) % 2, (c + (k & 1)) % 2
        peer = 4 * px + 2 * py + pc
        for a in range(n):
            sems = dict(send_sem=send_sems.at[7 * a + k - 1], recv_sem=recv_sems.at[7 * a + k - 1],
                        device_id=(px, py, pc), device_id_type=MESH)
            sends.append(pltpu.make_async_remote_copy(src_ref=src(a, peer), dst_ref=land_refs[a].at[me], **sems))
            recvs.append(pltpu.make_async_remote_copy(src_ref=src(a, me), dst_ref=land_refs[a].at[peer], **sems))
    return mine, sends, recvs


def _exchange_start(parts, *, gather, name):
    n = len(parts)

    def body(*refs):
        x_refs, land_refs = refs[:n], refs[n:2 * n]
        send_sems, recv_sems, local_sems = refs[2 * n:2 * n + 3]
        token = refs[-1]
        mine, sends, _ = _exchange_copies(gather, x_refs, land_refs, send_sems, recv_sems, local_sems)
        for cp in mine + sends:
            cp.start()
        token[...] = jnp.zeros_like(token)

    sems = (pltpu.SemaphoreType.DMA((7 * n,)), pltpu.SemaphoreType.DMA((7 * n,)), pltpu.SemaphoreType.DMA((n,)))
    thru = tuple(pltpu.HBM(p.shape, p.dtype) for p in parts)
    land = tuple(pltpu.HBM(((N_DEV,) if gather else ()) + p.shape, p.dtype) for p in parts)
    res = pl.pallas_call(
        body, name=name, in_specs=[HBM] * (2 * n),
        out_specs=(SEM, SEM, SEM) + (HBM,) * (2 * n) + (pl.BlockSpec(memory_space=pltpu.VMEM),),
        out_shape=sems + thru + land + (jax.ShapeDtypeStruct((8, 128), F32),),
        input_output_aliases={a: 3 + a for a in range(2 * n)},
        compiler_params=pltpu.CompilerParams(has_side_effects=DATAFLOW),
    )(*[pltpu.with_memory_space_constraint(p, pltpu.HBM) for p in parts],
      *[pltpu.with_memory_space_constraint(lax.empty(z.shape, z.dtype), pltpu.HBM) for z in land])
    return res[:3], res[3:3 + n], res[3 + n:3 + 2 * n], res[-1]


def _exchange_wait(sems, parts, landing, after, *, gather, name):
    n = len(parts)
    after = list(after)

    def body(*refs):
        x_refs, land_refs = refs[:n], refs[n:2 * n]
        send_sems, recv_sems, local_sems = refs[2 * n:2 * n + 3]
        token = refs[-1]
        mine, sends, recvs = _exchange_copies(gather, x_refs, land_refs, send_sems, recv_sems, local_sems)
        for cp in recvs:
            cp.wait_recv()
        for cp in sends:
            cp.wait_send()
        for cp in mine:
            cp.wait()
        token[...] = jnp.zeros_like(token)

    thru = tuple(pltpu.HBM(p.shape, p.dtype) for p in tuple(parts) + tuple(landing))
    res = pl.pallas_call(
        body, name=name, in_specs=[HBM] * (2 * n) + [SEM, SEM, SEM] + [ANY] * len(after),
        out_specs=(HBM,) * (2 * n) + (pl.BlockSpec(memory_space=pltpu.VMEM),),
        out_shape=thru + (jax.ShapeDtypeStruct((8, 128), F32),), input_output_aliases={a: a for a in range(2 * n)},
        compiler_params=pltpu.CompilerParams(has_side_effects=DATAFLOW),
    )(*parts, *landing, *sems, *after)
    return res[n:2 * n], res[-1]


def _relay_copies(x_refs, land_refs, first_send, first_recv, relay_send, relay_recv, local_sems):
    n = len(x_refs)
    x, y, c = _position()
    sibling = (x, y, 1 - c)
    chips = [(1 - x, y), (x, 1 - y), (1 - x, 1 - y)]

    def slot(a, px, py, pc):
        return land_refs[a].at[4 * px + 2 * py + pc]

    def hop(a, k, block, to, own=False):
        return pltpu.make_async_remote_copy(
            src_ref=x_refs[a] if own else slot(a, *block), dst_ref=slot(a, *block),
            send_sem=first_send.at[4 * a + k], recv_sem=first_recv.at[4 * a + k], device_id=to, device_id_type=MESH)

    def relay(a, j, block, to):
        return pltpu.make_async_remote_copy(
            src_ref=slot(a, *block), dst_ref=slot(a, *block),
            send_sem=relay_send.at[3 * a + j], recv_sem=relay_recv.at[3 * a + j], device_id=to, device_id_type=MESH)

    me = (x, y, c)
    mine = [pltpu.make_async_copy(x_refs[a], slot(a, *me), local_sems.at[a]) for a in range(n)]
    sends = [hop(a, 1 + j, me, (*chip, c), own=True) for j, chip in enumerate(chips) for a in range(n)]
    sends += [hop(a, 0, me, sibling, own=True) for a in range(n)]
    over_ici = [hop(a, 1 + j, (*chip, c), me) for j, chip in enumerate(chips) for a in range(n)]
    from_sibling = [hop(a, 0, sibling, me) for a in range(n)]
    if relay_send is None:
        return mine, sends, over_ici, from_sibling, [], []
    relays = [relay(a, j, (*chip, c), sibling) for j, chip in enumerate(chips) for a in range(n)]
    relayed = [relay(a, j, (*chip, 1 - c), me) for j, chip in enumerate(chips) for a in range(n)]
    return mine, sends, over_ici, from_sibling, relays, relayed


def _relay_call(body, n_sem_in, n_sem_out, shards, landing, sems_in, after, name):
    n = len(shards)
    thru = tuple(pltpu.HBM(p.shape, p.dtype) for p in tuple(shards) + tuple(landing))
    res = pl.pallas_call(
        body, name=name, in_specs=[HBM] * (2 * n) + [SEM] * n_sem_in + [ANY] * len(after),
        out_specs=(SEM,) * len(n_sem_out) + (HBM,) * (2 * n) + (pl.BlockSpec(memory_space=pltpu.VMEM),),
        out_shape=tuple(pltpu.SemaphoreType.DMA((k,)) for k in n_sem_out) + thru + (jax.ShapeDtypeStruct((8, 128), F32),),
        input_output_aliases={a: len(n_sem_out) + a for a in range(2 * n)},
        compiler_params=pltpu.CompilerParams(has_side_effects=DATAFLOW),
    )(*shards, *landing, *sems_in, *after)
    k = len(n_sem_out)
    return res[:k], res[k:k + n], res[k + n:k + 2 * n], res[-1]


def _layer_refs(refs, layers):
    return [r if l is None else r.at[l] for r, l in zip(refs, layers)]


def _relay_gather_start(shards, layers, after, *, name):
    n = len(shards)

    def body(*refs):
        x_refs, land_refs = _layer_refs(refs[:n], layers), refs[n:2 * n]
        first_send, first_recv, local_sems = refs[2 * n + len(after):2 * n + len(after) + 3]
        mine, sends, *_ = _relay_copies(x_refs, land_refs, first_send, first_recv, None, None, local_sems)
        for cp in mine + sends:
            cp.start()
        refs[-1][...] = jnp.zeros_like(refs[-1])

    landing = [pltpu.with_memory_space_constraint(
        lax.empty((N_DEV,) + (s.shape if l is None else s.shape[1:]), s.dtype), pltpu.HBM) for s, l in zip(shards, layers)]
    shards = [pltpu.with_memory_space_constraint(s, pltpu.HBM) for s in shards]
    return _relay_call(body, 0, (4 * n, 4 * n, n), shards, landing, (), list(after), name)


def _relay_gather_pass_on(first, shards, layers, landing, after, *, name):
    n = len(shards)

    def body(*refs):
        x_refs, land_refs = _layer_refs(refs[:n], layers), refs[n:2 * n]
        first_send, first_recv, local_sems = refs[2 * n:2 * n + 3]
        relay_send, relay_recv = refs[2 * n + 3 + len(after):2 * n + 5 + len(after)]
        _, _, over_ici, _, relays, _ = _relay_copies(x_refs, land_refs, first_send, first_recv, relay_send, relay_recv,
                                                   local_sems)
        for arrival, cp in zip(over_ici, relays):
            arrival.wait_recv()
            cp.start()
        refs[-1][...] = jnp.zeros_like(refs[-1])

    return _relay_call(body, 3, (3 * n, 3 * n), shards, landing, first, list(after), name)


def _relay_gather_wait(first, relay, shards, layers, landing, after, *, name):
    n = len(shards)

    def body(*refs):
        x_refs, land_refs = _layer_refs(refs[:n], layers), refs[n:2 * n]
        first_send, first_recv, local_sems, relay_send, relay_recv = refs[2 * n:2 * n + 5]
        mine, sends, _, from_sibling, relays, relayed = _relay_copies(
            x_refs, land_refs, first_send, first_recv, relay_send, relay_recv, local_sems)
        for cp in from_sibling + relayed:
            cp.wait_recv()
        for cp in sends + relays:
            cp.wait_send()
        for cp in mine:
            cp.wait()
        refs[-1][...] = jnp.zeros_like(refs[-1])

    _, shards, landing, token = _relay_call(body, 5, (), shards, landing, tuple(first) + tuple(relay), list(after), name)
    return shards, landing, token


def _ffn_fwd(x, gain, wg_in, wg_out, tag):
    T, D = x.shape
    fb, rb = wg_in.shape[-1], wg_out.shape[-2]
    tm, tn = min(T, 1024), 512
    h = _rmsnorm_fwd(x, gain, name=f"{tag}_norm")
    p = _mm(name=f"{tag}_in", grid=(T // tm, N_DEV, 1), tile=(tm, fb),
            a=h, a_spec=pl.BlockSpec((tm, D), lambda i, j, k: (i, 0)),
            b=wg_in, b_spec=pl.BlockSpec((None, D, fb), lambda i, j, k: (j, 0, 0)),
            out_shape=jax.ShapeDtypeStruct((N_DEV, T, fb), BF16), o_spec=pl.BlockSpec((None, tm, fb), lambda i, j, k: (j, i, 0)))
    a = _swiglu_fwd(p, name=f"{tag}_act")
    y = _mm(name=f"{tag}_out", grid=(T // tm, 1, FF_HALF), tile=(tm, D), resid=x, scale=0.5,
            a=a, a_spec=pl.BlockSpec((None, tm, fb), lambda i, j, k: (k, i, 0)),
            b=wg_out.reshape(N_DEV * rb, D), b_spec=pl.BlockSpec((fb, D), lambda i, j, k: (k, 0)),
            out_shape=jax.ShapeDtypeStruct((T, D), F32), o_spec=pl.BlockSpec((tm, D), lambda i, j, k: (i, 0)))
    return y, (x, h, p, a)


def _ffn_bwd(dy, saved, gain, wg_in, wg_out, tag, on_weight_grads=None):
    x, h, p, a = saved
    T, D = x.shape
    fb, rb = wg_in.shape[-1], wg_out.shape[-2]
    tm, tn = min(T, 1024), 512
    da = _mm(name=f"{tag}_out_dx", grid=(T // tm, FF_HALF, 1), tile=(tm, fb), tb=True, scale=0.5,
             a=dy, a_spec=pl.BlockSpec((tm, D), lambda i, j, k: (i, 0)),
             b=wg_out.reshape(N_DEV * rb, D), b_spec=pl.BlockSpec((fb, D), lambda i, j, k: (j, 0)),
             out_shape=jax.ShapeDtypeStruct((FF_HALF, T, fb), BF16), o_spec=pl.BlockSpec((None, tm, fb), lambda i, j, k: (j, i, 0)))
    d_w_out = _mm(name=f"{tag}_out_dw", grid=(FF_HALF, D // tn, 1), tile=(fb, tn), ta=True, scale=0.5,
                  a=a, a_spec=pl.BlockSpec((None, T, fb), lambda i, j, k: (i, 0, 0)),
                  b=dy, b_spec=pl.BlockSpec((T, tn), lambda i, j, k: (0, j)),
                  out_shape=jax.ShapeDtypeStruct((FF_HALF, fb, D), BF16), o_spec=pl.BlockSpec((None, fb, tn), lambda i, j, k: (i, 0, j)))
    dp = _swiglu_bwd(da, p, name=f"{tag}_act_bwd")
    d_w_in = _mm(name=f"{tag}_in_dw", grid=(1, N_DEV, 1), tile=(D, fb), ta=True,
                 a=h, a_spec=pl.BlockSpec((T, D), lambda i, j, k: (0, 0)),
                 b=dp, b_spec=pl.BlockSpec((None, T, fb), lambda i, j, k: (j, 0, 0)),
                 out_shape=jax.ShapeDtypeStruct((N_DEV, D, fb), BF16), o_spec=pl.BlockSpec((None, D, fb), lambda i, j, k: (j, 0, 0)))
    dh = _mm(name=f"{tag}_in_dx", grid=(T // tm, 1, N_DEV), tile=(tm, D), tb=True,
             a=dp, a_spec=pl.BlockSpec((None, tm, fb), lambda i, j, k: (k, i, 0)),
             b=wg_in, b_spec=pl.BlockSpec((None, D, fb), lambda i, j, k: (k, 0, 0)),
             out_shape=jax.ShapeDtypeStruct((T, D), F32), o_spec=pl.BlockSpec((tm, D), lambda i, j, k: (i, 0)))
    d_w_out = d_w_out.reshape(N_DEV, rb, D)
    if on_weight_grads is not None:
        gain = gain + on_weight_grads(d_w_in, d_w_out)[0, 0]
    dx, d_gain = _rmsnorm_bwd(dh, x, gain, dy, name=f"{tag}_norm_bwd")
    return dx, d_gain, d_w_in, d_w_out


def _square_mm(a, wg, *, name, transposed=False, out_dtype=F32, resid=None):
    T, D = a.shape
    w = wg.reshape(D, D)
    return _matmul(a, w, tb=transposed, name=name, out_dtype=out_dtype, resid=resid)


def _head_rows(cols, T):
    return cols.T.reshape(HEADS, T // DN_CHUNK, 1, DN_CHUNK)


def _mixer_fwd(x, w, big, tag):
    T = x.shape[0]
    h = _rmsnorm_fwd(x, w["mix_norm"], name=f"{tag}_norm")
    proj = _matmul(h, big["w_main"], name=f"{tag}_proj")
    scal = _matmul(h, big["w_scal"], name=f"{tag}_proj_scal", tn=N_SCAL)
    qkv = _conv_fwd(proj, big["conv_w"], name=f"{tag}_conv")
    b_rows = _head_rows(scal[:, 0:HEADS], T)
    a_rows = _head_rows(scal[:, HEADS:2 * HEADS], T)
    o_a, *states = _dn_fwd(qkv, b_rows, a_rows, w["hp"], name=f"{tag}_dn")
    oa_n = _gated_norm_fwd(o_a, proj, w["dn_out_norm"], name=f"{tag}_dn_norm")
    ya = _square_mm(oa_n, big["w_branch_a"], name=f"{tag}_branch_a")
    o_b, ltot = _sb_fwd(proj, w["sb_q_norm"], w["sb_k_norm"], name=f"{tag}_sb")
    yb = _square_mm(o_b, big["w_branch_b"], name=f"{tag}_branch_b")
    merged = _merge_fwd(ya, yb, proj, name=f"{tag}_merge")
    y = _square_mm(merged, big["w_out"], name=f"{tag}_out", resid=x)
    return y, (x, h, proj, qkv, b_rows, a_rows, o_a, states, oa_n, ya, o_b, ltot, yb, merged)


def _mixer_bwd(dy, saved, w, big, tag, on_weight_grads):
    x, h, proj, qkv, b_rows, a_rows, o_a, states, oa_n, ya, o_b, ltot, yb, merged = saved
    T = x.shape[0]
    g = {}
    d_merged = _square_mm(dy, big["w_out"], transposed=True, name=f"{tag}_out_dx", out_dtype=BF16)
    g["w_out"] = _matmul(merged, dy, ta=True, name=f"{tag}_out_dw", out_dtype=BF16)
    d_ya, d_yb, d_ga, d_gb = _merge_bwd(d_merged, ya, yb, proj, name=f"{tag}_merge_bwd")
    d_oan = _square_mm(d_ya, big["w_branch_a"], transposed=True, name=f"{tag}_branch_a_dx")
    g["w_branch_a"] = _matmul(oa_n, d_ya, ta=True, name=f"{tag}_branch_a_dw", out_dtype=BF16)
    d_ob = _square_mm(d_yb, big["w_branch_b"], transposed=True, name=f"{tag}_branch_b_dx")
    g["w_branch_b"] = _matmul(o_b, d_yb, ta=True, name=f"{tag}_branch_b_dw", out_dtype=BF16)
    d_oa, d_z, g["dn_out_norm"] = _gated_norm_bwd(d_oan, o_a, proj, w["dn_out_norm"], name=f"{tag}_dn_norm_bwd")
    d_qkv, d_b_rows, d_a_rows, d_hp = _dn_bwd(qkv, b_rows, a_rows, w["hp"], *states, d_oa, name=f"{tag}_dn_bwd")
    g["dn_a_log"] = d_hp[:, 0, 0]
    g["dn_dt_bias"] = d_hp[:, 1, 0]
    d_conv_in, g["conv_w"] = _conv_bwd(d_qkv, proj, big["conv_w"], name=f"{tag}_conv_bwd")
    d_sbq, d_sbk, d_sbv, g["sb_q_norm"], g["sb_k_norm"] = _sb_bwd(
        proj, w["sb_q_norm"], w["sb_k_norm"], ltot, d_ob, name=f"{tag}_sb_bwd")
    d_proj = jnp.concatenate([d_conv_in, d_z, d_sbq, d_sbk, d_sbv, d_ga, d_gb], axis=1)
    d_scal = jnp.concatenate([d_b_rows.reshape(HEADS, T).T, d_a_rows.reshape(HEADS, T).T,
                              jnp.zeros((T, N_SCAL - 2 * HEADS), F32)], axis=1).astype(BF16)
    g["w_main"] = _matmul(h, d_proj, ta=True, name=f"{tag}_proj_dw", out_dtype=BF16)
    g["w_scal"] = _matmul(h, d_scal, ta=True, name=f"{tag}_proj_scal_dw", out_dtype=BF16, tn=N_SCAL)
    dh_scal = _matmul(d_scal, big["w_scal"], tb=True, name=f"{tag}_proj_scal_dx")
    dh = _matmul(d_proj, big["w_main"], tb=True, name=f"{tag}_proj_dx", tk=N_MAIN // 4, resid=dh_scal)
    gain = w["mix_norm"] + on_weight_grads(g)[0, 0]
    dx, g["mix_norm"] = _rmsnorm_bwd(dh, x, gain, dy, name=f"{tag}_norm_bwd")
    return dx, g


def _local_step(x, target, layers, weights_of, on_weight_grads):
    saved, bigs = [], []
    for l, w in enumerate(layers):
        big = weights_of(l, 0, x)
        x, s1 = _ffn_fwd(x, w["ffn1_norm"] + big["issued"], big["ffn1_w_in"], big["ffn1_w_out"], f"l{l}_ffn1")
        big.update(weights_of(l, 1, x))
        x, s2 = _mixer_fwd(x, dict(w, mix_norm=w["mix_norm"] + big["issued"]), big, f"l{l}_mix")
        big.update(weights_of(l, 2, x))
        x, s3 = _ffn_fwd(x, w["ffn2_norm"] + big["issued"], big["ffn2_w_in"], big["ffn2_w_out"], f"l{l}_ffn2")
        saved.append((s1, s2, s3))
        bigs.append(big)
    loss, dx = _loss_head(x, target, name="loss_head")
    small = [None] * len(layers)
    for l in reversed(range(len(layers))):
        w, big = layers[l], bigs[l]
        s1, s2, s3 = saved[l]
        dx, g_n2, _, _ = _ffn_bwd(
            dx, s3, w["ffn2_norm"], big["ffn2_w_in"], big["ffn2_w_out"], f"l{l}_ffn2",
            on_weight_grads=lambda g_in, g_out, l=l: on_weight_grads(l, 0, dict(ffn2_w_in=g_in, ffn2_w_out=g_out)))
        dx, g = _mixer_bwd(dx, s2, w, big, f"l{l}_mix", on_weight_grads=lambda g, l=l: on_weight_grads(l, 1, g))
        dx, g_n1, _, _ = _ffn_bwd(
            dx, s1, w["ffn1_norm"], big["ffn1_w_in"], big["ffn1_w_out"], f"l{l}_ffn1",
            on_weight_grads=lambda g_in, g_out, l=l: on_weight_grads(l, 2, dict(ffn1_w_in=g_in, ffn1_w_out=g_out)))
        small[l] = dict(g, ffn1_norm=g_n1, ffn2_norm=g_n2)
    return loss, dx, small


_BIG = ("ffn1_w_in", "ffn1_w_out", "w_in", "w_branch_a", "w_branch_b", "w_out", "ffn2_w_in", "ffn2_w_out")
_STAGES = (("ffn2_w_in", "ffn2_w_out"), ("w_in", "w_branch_a", "w_branch_b", "w_out"), ("ffn1_w_in", "ffn1_w_out"))
_SMALL = ("ffn1_norm", "mix_norm", "ffn2_norm", "dn_a_log", "dn_dt_bias", "dn_out_norm", "sb_q_norm", "sb_k_norm")
_ORDER = ("ffn1_norm", "ffn1_w_in", "ffn1_w_out", "mix_norm", "w_in", "dn_conv_w", "dn_a_log", "dn_dt_bias", "dn_out_norm",
          "sb_q_norm", "sb_k_norm", "w_branch_a", "w_branch_b", "w_out", "ffn2_norm", "ffn2_w_in", "ffn2_w_out")
COL_SCAL = 4 * D_MODEL
SCAL_SLOT = COL_SCAL // (N_IN // N_DEV)
SCAL_AT = COL_SCAL % (N_IN // N_DEV)
assert SCAL_AT + 2 * HEADS <= N_IN // N_DEV


def _pad_rows(a, multiple):
    pad = (-a.shape[-2]) % multiple
    return a if pad == 0 else jnp.pad(a, [(0, 0)] * (a.ndim - 2) + [(0, pad), (0, 0)])


def _lane_rows(a):
    flat = a.reshape(-1)
    flat = jnp.pad(flat, (0, (-flat.shape[0]) % 128))
    return flat.reshape(-1, 128)


def _pack_small(named):
    pieces, spans, r = [], {}, 0
    for n, a in named:
        rows = _lane_rows(a)
        spans[n] = (r, r + rows.shape[0], a.shape)
        r += rows.shape[0]
        pieces.append(rows)
    return _pad_rows(jnp.concatenate(pieces, axis=0), 8), spans


def _unpack_small(packed, spans, n):
    r0, r1, shape = spans[n]
    return packed[r0:r1].reshape(-1)[:math.prod(shape)].reshape(shape)


def kernel(x, ffn1_norm, ffn1_w_in, ffn1_w_out, mix_norm, w_in, dn_conv_w, dn_a_log, dn_dt_bias, dn_out_norm, sb_q_norm, sb_k_norm, w_branch_a, w_branch_b, w_out, ffn2_norm, ffn2_w_in, ffn2_w_out, loss_target, m_ffn1_norm, m_ffn1_w_in, m_ffn1_w_out, m_mix_norm, m_w_in, m_dn_conv_w, m_dn_a_log, m_dn_dt_bias, m_dn_out_norm, m_sb_q_norm, m_sb_k_norm, m_w_branch_a, m_w_branch_b, m_w_out, m_ffn2_norm, m_ffn2_w_in, m_ffn2_w_out, v_ffn1_norm, v_ffn1_w_in, v_ffn1_w_out, v_mix_norm, v_w_in, v_dn_conv_w, v_dn_a_log, v_dn_dt_bias, v_dn_out_norm, v_sb_q_norm, v_sb_k_norm, v_w_branch_a, v_w_branch_b, v_w_out, v_ffn2_norm, v_ffn2_w_in, v_ffn2_w_out):
    given = dict(locals())
    weights = {n: given[n] for n in _ORDER}
    mom_m = {n: given["m_" + n] for n in _ORDER}
    mom_v = {n: given["v_" + n] for n in _ORDER}
    L = ffn1_norm.shape[0]
    ax, ay, ac = _position()
    my_slot = 4 * ax + 2 * ay + ac

    conv_cols = dn_conv_w.shape[-1]
    second_ffn, mixer, first_ffn = _STAGES
    later = mixer + second_ffn
    sources = {n: weights[n].astype(BF16) for n in _BIG}
    sources["conv"] = _pad_rows(_lane_rows(dn_conv_w), 8)
    gathers, landed = {}, {}

    def start_gather(key, names, l, after):
        layer_of = [None if n == "conv" else l for n in names]
        first, thru, landing, token = _relay_gather_start([sources[n] for n in names], layer_of, after,
                                                          name=f"gather_start_{key}")
        sources.update(zip(names, thru))
        gathers[key] = dict(first=first, landing=landing, names=names, layers=layer_of)
        return token

    def pass_on(key, after):
        g = gathers[key]
        g["relay"], thru, g["landing"], token = _relay_gather_pass_on(
            g["first"], [sources[n] for n in g["names"]], g["layers"], g["landing"], [after], name=f"gather_pass_on_{key}")
        sources.update(zip(g["names"], thru))
        return token

    def wait_gather(key, after):
        g = gathers.pop(key)
        thru, arrays, token = _relay_gather_wait(g["first"], g["relay"], [sources[n] for n in g["names"]], g["layers"],
                                                 g["landing"], [after], name=f"gather_wait_{key}")
        sources.update(zip(g["names"], thru))
        landed.update(zip(g["names"], arrays))
        return token

    def weights_of(l, part, x_in):
        if l == 0 and part == 0:
            start_gather("l0_ffn1", first_ffn, 0, [])
            pass_on("l0_ffn1", x_in)
            issued = start_gather("l0_mix", mixer + ("conv",), 0, [wait_gather("l0_ffn1", x_in)])
            return dict({n: landed.pop(n) for n in first_ffn}, issued=issued[0, 0])
        if part == 0:
            token = wait_gather(f"l{l}", x_in)
            issued = start_gather(f"l{l + 1}", _BIG, l + 1, [token]) if l + 1 < L else token
            return dict({n: landed.pop(n) for n in first_ffn}, issued=issued[0, 0])
        if part == 2:
            issued = 0.0
            if l == 0:
                pass_on("l0_ffn2", x_in)
                issued = wait_gather("l0_ffn2", x_in)[0, 0]
            if l + 1 < L:
                issued = pass_on(f"l{l + 1}", x_in)[0, 0]
            return dict({n: landed.pop(n) for n in second_ffn if n in landed}, issued=issued)
        issued = 0.0
        if l == 0:
            pass_on("l0_mix", x_in)
            token = start_gather("l0_ffn2", second_ffn, 0, [wait_gather("l0_mix", x_in)])
            issued = (start_gather("l1", _BIG, 1, [token]) if L > 1 else token)[0, 0]
            conv = landed.pop("conv").reshape(N_DEV, -1)[:, :L * DN_CONV * conv_cols]
            landed["conv_w"] = conv.reshape(N_DEV, L, DN_CONV, conv_cols).transpose(1, 2, 0, 3).reshape(
                L, DN_CONV, N_DEV * conv_cols)
        big = {n: landed.pop(n) for n in later if n in landed}
        wi = big.pop("w_in")
        pieces = [wi[d] for d in range(N_DEV)]
        pieces[SCAL_SLOT:SCAL_SLOT + 1] = [wi[SCAL_SLOT][:, :SCAL_AT], wi[SCAL_SLOT][:, SCAL_AT + 2 * HEADS:]]
        big["w_main"] = jnp.concatenate(pieces, axis=1)
        big["w_scal"] = jnp.pad(wi[SCAL_SLOT][:, SCAL_AT:SCAL_AT + 2 * HEADS], ((0, 0), (0, N_SCAL - 2 * HEADS)))
        big["conv_w"] = landed["conv_w"][l]
        return dict(big, issued=issued)

    layers = []
    for l in range(L):
        hp = jnp.concatenate([jnp.broadcast_to(dn_a_log[l][:, None, None], (HEADS, 1, 128)),
                              jnp.broadcast_to(dn_dt_bias[l][:, None, None], (HEADS, 1, 128)),
                              jnp.zeros((HEADS, 6, 128), F32)], axis=1)
        layers.append(dict(ffn1_norm=ffn1_norm[l][None], mix_norm=mix_norm[l][None], hp=hp,
                           dn_out_norm=dn_out_norm[l][None], sb_q_norm=sb_q_norm[l][None],
                           sb_k_norm=sb_k_norm[l][None], ffn2_norm=ffn2_norm[l][None]))

    in_flight = {}

    def on_weight_grads(l, stage, g):
        parts = dict(g)
        if stage == 1:
            gm, shard = g["w_main"], N_IN // N_DEV
            blocks = [gm[:, d * shard:(d + 1) * shard] for d in range(SCAL_SLOT)]
            blocks.append(jnp.concatenate([gm[:, SCAL_SLOT * shard:COL_SCAL], g["w_scal"][:, :2 * HEADS],
                                           gm[:, COL_SCAL:(SCAL_SLOT + 1) * shard - 2 * HEADS]], axis=1))
            blocks += [gm[:, d * shard - 2 * HEADS:(d + 1) * shard - 2 * HEADS] for d in range(SCAL_SLOT + 1, N_DEV)]
            parts["w_in"] = jnp.stack(blocks)
            for n in ("w_branch_a", "w_branch_b", "w_out"):
                parts[n] = g[n].reshape(N_DEV, D_MODEL // N_DEV, D_MODEL)
        *in_flight[l, stage], token = _exchange_start([parts[n] for n in _STAGES[stage]], gather=False,
                                                      name=f"scatter_start_l{l}_{stage}")
        return token

    loss_row, dx, grads = _local_step(x[0], loss_target[0], layers, weights_of, on_weight_grads)
    loss = lax.psum(loss_row[0, 0], ("x", "y", "c"))

    results = {n: None for n in _BIG}
    after = [dx]
    for l in reversed(range(L)):
        for stage, names in enumerate(_STAGES):
            landed, all_landed = _exchange_wait(*in_flight[l, stage], after, gather=False,
                                                name=f"scatter_wait_l{l}_{stage}")
            for n, parts in zip(names, landed):
                _, a, b = weights[n].shape
                results[n] = _adamw(parts, weights[n].reshape(L * a, b), mom_m[n].reshape(L * a, b),
                                    mom_v[n].reshape(L * a, b), layer=l, earlier=results[n], name=f"adamw_{n}_l{l}")
            after = [results[n][0] for n in names]
    out = {n: tuple(t.reshape(weights[n].shape) for t in results[n]) for n in _BIG}

    small_grads = [(n, jnp.stack([g[n].reshape(weights[n].shape[1:]) for g in grads])) for n in _SMALL]
    small_packed, spans = _pack_small(small_grads + [("conv", jnp.stack([g["conv_w"] for g in grads]))])
    small_packed = small_packed + all_landed[0, 0]
    small_sum = _sum_parts(_all_gather([small_packed], name="gather_small_grads")[0], name="sum_small_grads")
    rep_rows = spans["conv"][0]
    pack_rep = lambda d: _pad_rows(_pack_small([(n, d[n]) for n in _SMALL])[0], 8)
    rep_pad = (-rep_rows) % 8
    g_rep = jnp.pad(small_sum[:rep_rows], ((0, rep_pad), (0, 0)))
    res = _adamw(g_rep[None], pack_rep(weights), pack_rep(mom_m), pack_rep(mom_v), name="adamw_replicated")
    for n in _SMALL:
        out[n] = tuple(_unpack_small(t, spans, n) for t in res)
    conv_sum = _unpack_small(small_sum, spans, "conv")
    conv_mine = lax.dynamic_slice_in_dim(conv_sum, my_slot * conv_cols, conv_cols, axis=2).reshape(L * DN_CONV, conv_cols)
    flat = lambda t: t.reshape(L * DN_CONV, conv_cols)
    res = _adamw(conv_mine[None], flat(dn_conv_w), flat(m_dn_conv_w), flat(v_dn_conv_w), name="adamw_conv")
    out["dn_conv_w"] = tuple(t.reshape(L, DN_CONV, conv_cols) for t in res)

    return (loss, dx[None], *[out[n][0] for n in _ORDER], *[out[n][1] for n in _ORDER],
            *[out[n][2] for n in _ORDER], *[out[n][3] for n in _ORDER])
```

```python
import functools
import math

import jax
import jax.numpy as jnp
from jax import lax
from jax.experimental import pallas as pl
from jax.experimental.pallas import tpu as pltpu

F32 = jnp.float32
BF16 = jnp.bfloat16

N_DEV = 8
D_MODEL = 1024
DEPTH = 4
D_FF = 2816
HEADS = 8
HEAD_DIM = 128
DN_CHUNK = 64
DN_CONV = 4
DN_GROUP = 8
DN_HEADS = 2
SB_BLOCK = 128
SB_KEY_TILE = 512
SB_HEADS = 4
SB_HEADS_BWD = 4
SB_KEY_TILE_BWD = 512
RMS_EPS = 1e-6
L2_EPS = 1e-6
N_IN = 9232
N_MAIN = 9216
N_SCAL = 128
QK_SCALE = HEAD_DIM ** -0.5

ADAM_LR = 0.001
ADAM_B1 = 0.9
ADAM_B2 = 0.999
ADAM_EPS = 1e-08
ADAM_WD = 0.01
ADAM_STEP = 10

V7X_VMEM_LIMIT = 56 * 1024 * 1024
V7X_VMEM_LIMIT_HIGH = 60 * 1024 * 1024
MESH = pl.DeviceIdType.MESH
ANY = pl.BlockSpec(memory_space=pl.ANY)


def _params(sem=None, vmem=V7X_VMEM_LIMIT):
    return pltpu.CompilerParams(dimension_semantics=sem, vmem_limit_bytes=vmem)


def _sigmoid(x):
    return 1.0 / (1.0 + jnp.exp(-x))


SOFTPLUS_LINEAR = 30.0


def _softplus(x):
    return jnp.maximum(x, jnp.log(1.0 + jnp.exp(jnp.minimum(x, SOFTPLUS_LINEAR))))


def _bdot(a, b, dims=(((1,), (0,)), ((), ()))):
    return lax.dot_general(a.astype(BF16), b.astype(BF16), dims, preferred_element_type=F32)


_NT = (((1,), (1,)), ((), ()))
_TN = (((0,), (0,)), ((), ()))


def _hdot(a, b, dims=(((1,), (0,)), ((), ()))):
    a_hi = a.astype(BF16)
    b_hi = b.astype(BF16)
    a_lo = (a - a_hi.astype(F32)).astype(BF16)
    b_lo = (b - b_hi.astype(F32)).astype(BF16)
    dot = functools.partial(lax.dot_general, dimension_numbers=dims, preferred_element_type=F32)
    return dot(a_hi, b_hi) + (dot(a_hi, b_lo) + dot(a_lo, b_hi))


def _hdot_tn(a, b):
    return _hdot(a, b, _TN)


def _mm(*, name, grid, a, a_spec, b, b_spec, out_shape, o_spec, tile, ta=False, tb=False, resid=None, scale=1.0):
    nk = grid[2]
    dims = (((0 if ta else 1,), (1 if tb else 0,)), ((), ()))

    def flat(v):
        return v if v.ndim == 2 else v.reshape(-1, v.shape[-1])

    def body(*refs):
        a_ref, b_ref = refs[:2]
        r_ref = refs[2] if resid is not None else None
        o_ref = refs[3] if resid is not None else refs[2]
        part = lax.dot_general(flat(a_ref[...]).astype(BF16), flat(b_ref[...]).astype(BF16), dims,
                               preferred_element_type=F32)

        def finish(acc):
            if scale != 1.0:
                acc = acc * scale
            if r_ref is not None:
                acc = r_ref[...] + acc
            o_ref[...] = acc.astype(o_ref.dtype)

        if nk == 1:
            finish(part)
        else:
            acc_ref = refs[-1]
            k = pl.program_id(2)

            @pl.when(k == 0)
            def _():
                acc_ref[...] = part

            @pl.when(k > 0)
            def _():
                acc_ref[...] += part

            @pl.when(k == nk - 1)
            def _():
                finish(acc_ref[...])

    in_specs = [a_spec, b_spec] + ([pl.BlockSpec(tile, lambda i, j, k: (i, j))] if resid is not None else [])
    args = (a, b) + ((resid,) if resid is not None else ())
    return pl.pallas_call(
        body, name=name, grid=grid, in_specs=in_specs, out_specs=o_spec, out_shape=out_shape,
        scratch_shapes=[pltpu.VMEM(tile, F32)] if nk > 1 else [],
        compiler_params=_params(("parallel", "parallel", "arbitrary")),
    )(*args)


def _matmul(a, b, *, name, ta=False, tb=False, out_dtype=F32, tm=None, tn=None, tk=None, resid=None, scale=1.0):
    if ta:
        K, M = a.shape
    else:
        M, K = a.shape
    N = b.shape[0] if tb else b.shape[1]
    tm = tm or min(M, 1024)
    tn = tn or min(N, 512)
    tk = tk or K
    assert M % tm == 0 and N % tn == 0 and K % tk == 0, (name, M, N, K, tm, tn, tk)
    a_spec = pl.BlockSpec((tk, tm), lambda i, j, k: (k, i)) if ta else pl.BlockSpec((tm, tk), lambda i, j, k: (i, k))
    b_spec = pl.BlockSpec((tn, tk), lambda i, j, k: (j, k)) if tb else pl.BlockSpec((tk, tn), lambda i, j, k: (k, j))
    return _mm(name=name, grid=(M // tm, N // tn, K // tk), a=a, a_spec=a_spec, b=b, b_spec=b_spec,
               out_shape=jax.ShapeDtypeStruct((M, N), out_dtype), o_spec=pl.BlockSpec((tm, tn), lambda i, j, k: (i, j)),
               tile=(tm, tn), ta=ta, tb=tb, resid=resid, scale=scale)


ROW_TILE = 256


def _rmsnorm_fwd(x, gain, *, name):
    T, D = x.shape

    def body(x_ref, g_ref, o_ref):
        xf = x_ref[...]
        r = lax.rsqrt(jnp.mean(xf * xf, axis=-1, keepdims=True) + RMS_EPS)
        o_ref[...] = (xf * r * g_ref[...]).astype(o_ref.dtype)

    return pl.pallas_call(
        body, name=name, grid=(T // ROW_TILE,),
        in_specs=[pl.BlockSpec((ROW_TILE, D), lambda i: (i, 0)), pl.BlockSpec((1, D), lambda i: (0, 0))],
        out_specs=pl.BlockSpec((ROW_TILE, D), lambda i: (i, 0)),
        out_shape=jax.ShapeDtypeStruct((T, D), BF16), compiler_params=_params(("parallel",)),
    )(x, gain)


def _rmsnorm_bwd(dh, x, gain, dres, *, name):
    T, D = x.shape

    def body(dh_ref, x_ref, g_ref, res_ref, dx_ref, dg_ref):
        xf = x_ref[...]
        r = lax.rsqrt(jnp.mean(xf * xf, axis=-1, keepdims=True) + RMS_EPS)
        y = xf * r
        dh_v = dh_ref[...].astype(F32)
        dy = dh_v * g_ref[...]
        dx_ref[...] = res_ref[...] + r * (dy - y * jnp.mean(dy * y, axis=-1, keepdims=True))

        @pl.when(pl.program_id(0) == 0)
        def _():
            dg_ref[...] = jnp.zeros_like(dg_ref)

        dg_ref[...] += jnp.sum(dh_v * y, axis=0, keepdims=True)

    row = pl.BlockSpec((ROW_TILE, D), lambda i: (i, 0))
    vec = pl.BlockSpec((1, D), lambda i: (0, 0))
    return pl.pallas_call(
        body, name=name, grid=(T // ROW_TILE,), in_specs=[row, row, vec, row], out_specs=(row, vec),
        out_shape=(jax.ShapeDtypeStruct((T, D), F32), jax.ShapeDtypeStruct((1, D), F32)),
        compiler_params=_params(("arbitrary",)),
    )(dh, x, gain, dres)


FF_HALF = N_DEV // 2


def _swiglu_fwd(p, *, name):
    _, T, fb = p.shape

    def body(g_ref, u_ref, o_ref):
        g = g_ref[...].astype(F32)
        o_ref[...] = (g * _sigmoid(g) * u_ref[...].astype(F32)).astype(o_ref.dtype)

    blk = (None, ROW_TILE, fb)
    return pl.pallas_call(
        body, name=name, grid=(T // ROW_TILE, FF_HALF),
        in_specs=[pl.BlockSpec(blk, lambda i, j: (j, i, 0)), pl.BlockSpec(blk, lambda i, j: (j + FF_HALF, i, 0))],
        out_specs=pl.BlockSpec(blk, lambda i, j: (j, i, 0)),
        out_shape=jax.ShapeDtypeStruct((FF_HALF, T, fb), BF16), compiler_params=_params(("parallel", "parallel")),
    )(p, p)


def _swiglu_bwd(da, p, *, name):
    _, T, fb = p.shape

    def body(da_ref, g_ref, u_ref, o_ref):
        g = g_ref[...].astype(F32)
        u = u_ref[...].astype(F32)
        d = da_ref[...].astype(F32)
        s = _sigmoid(g)
        o_ref[0] = (d * u * (s * (1.0 + g * (1.0 - s)))).astype(o_ref.dtype)
        o_ref[1] = (d * g * s).astype(o_ref.dtype)

    blk = (None, ROW_TILE, fb)
    out = pl.pallas_call(
        body, name=name, grid=(T // ROW_TILE, FF_HALF),
        in_specs=[pl.BlockSpec(blk, lambda i, j: (j, i, 0)), pl.BlockSpec(blk, lambda i, j: (j, i, 0)),
                  pl.BlockSpec(blk, lambda i, j: (j + FF_HALF, i, 0))],
        out_specs=pl.BlockSpec((2, None, ROW_TILE, fb), lambda i, j: (0, j, i, 0)),
        out_shape=jax.ShapeDtypeStruct((2, FF_HALF, T, fb), BF16), compiler_params=_params(("parallel", "parallel")),
    )(da, p, p)
    return out.reshape(2 * FF_HALF, T, fb)


COL_GATE_A = 7
COL_GATE_B = 8


def _merge_fwd(ya, yb, proj, *, name):
    T, D = ya.shape

    def body(ya_ref, yb_ref, ga_ref, gb_ref, o_ref):
        o_ref[...] = (_sigmoid(ga_ref[...]) * ya_ref[...] + _sigmoid(gb_ref[...]) * yb_ref[...]).astype(o_ref.dtype)

    row = pl.BlockSpec((ROW_TILE, D), lambda i: (i, 0))
    return pl.pallas_call(
        body, name=name, grid=(T // ROW_TILE,),
        in_specs=[row, row, pl.BlockSpec((ROW_TILE, D), lambda i: (i, COL_GATE_A)),
                  pl.BlockSpec((ROW_TILE, D), lambda i: (i, COL_GATE_B))],
        out_specs=row, out_shape=jax.ShapeDtypeStruct((T, D), BF16), compiler_params=_params(("parallel",)),
    )(ya, yb, proj, proj)


def _merge_bwd(dm, ya, yb, proj, *, name):
    T, D = ya.shape

    def body(dm_ref, ya_ref, yb_ref, ga_ref, gb_ref, dya_ref, dyb_ref, dga_ref, dgb_ref):
        d = dm_ref[...].astype(F32)
        sa = _sigmoid(ga_ref[...])
        sb = _sigmoid(gb_ref[...])
        dya_ref[...] = (d * sa).astype(BF16)
        dyb_ref[...] = (d * sb).astype(BF16)
        dga_ref[...] = (d * ya_ref[...] * sa * (1.0 - sa)).astype(BF16)
        dgb_ref[...] = (d * yb_ref[...] * sb * (1.0 - sb)).astype(BF16)

    row = pl.BlockSpec((ROW_TILE, D), lambda i: (i, 0))
    out = jax.ShapeDtypeStruct((T, D), BF16)
    return pl.pallas_call(
        body, name=name, grid=(T // ROW_TILE,),
        in_specs=[row, row, row, pl.BlockSpec((ROW_TILE, D), lambda i: (i, COL_GATE_A)),
                  pl.BlockSpec((ROW_TILE, D), lambda i: (i, COL_GATE_B))],
        out_specs=(row, row, row, row), out_shape=(out, out, out, out), compiler_params=_params(("parallel",)),
    )(dm, ya, yb, proj, proj)


def _loss_head(y, target, *, name):
    T, D = y.shape

    def body(y_ref, t_ref, loss_ref, dy_ref):
        err = y_ref[...] - t_ref[...]
        dy_ref[...] = err * (1.0 / D)

        @pl.when(pl.program_id(0) == 0)
        def _():
            loss_ref[...] = jnp.zeros_like(loss_ref)

        loss_ref[...] += 0.5 * jnp.sum(jnp.sum(err * err, axis=-1, keepdims=True) * (1.0 / D), axis=0, keepdims=True)

    row = pl.BlockSpec((ROW_TILE, D), lambda i: (i, 0))
    return pl.pallas_call(
        body, name=name, grid=(T // ROW_TILE,), in_specs=[row, row],
        out_specs=(pl.BlockSpec((1, 128), lambda i: (0, 0)), row),
        out_shape=(jax.ShapeDtypeStruct((1, 128), F32), jax.ShapeDtypeStruct((T, D), F32)),
        compiler_params=_params(("arbitrary",)),
    )(y, target)


CONV_PAD = 8


def _conv_taps(w, xp, T, first):
    acc = w[0:1, :] * xp[pl.ds(first, T), :]
    for i in range(1, DN_CONV):
        acc = acc + w[i:i + 1, :] * xp[pl.ds(first + i, T), :]
    return acc


def _conv_fwd(proj, conv_w, *, name):
    T = proj.shape[0]

    def body(x_ref, w_ref, o_ref, xp):
        xp[0:CONV_PAD, :] = jnp.zeros((CONV_PAD, HEAD_DIM), F32)
        xp[CONV_PAD:, :] = x_ref[...]
        y = _conv_taps(w_ref[...], xp, T, CONV_PAD - (DN_CONV - 1))
        s = y * _sigmoid(y)
        n = s * lax.rsqrt(jnp.sum(s * s, axis=-1, keepdims=True) + L2_EPS)
        o_ref[0] = jnp.where(pl.program_id(0) < 2, n, s)

    return pl.pallas_call(
        body, name=name, grid=(3, HEADS),
        in_specs=[pl.BlockSpec((T, HEAD_DIM), lambda c, h: (0, c * HEADS + h)),
                  pl.BlockSpec((DN_CONV, HEAD_DIM), lambda c, h: (0, c * HEADS + h))],
        out_specs=pl.BlockSpec((1, T, HEAD_DIM), lambda c, h: (c, 0, h)),
        out_shape=jax.ShapeDtypeStruct((3, T, D_MODEL), F32),
        scratch_shapes=[pltpu.VMEM((T + CONV_PAD, HEAD_DIM), F32)],
        compiler_params=_params(("parallel", "parallel")),
    )(proj, conv_w)


def _conv_bwd(dqkv, proj, conv_w, *, name):
    T = proj.shape[0]

    def body(d_ref, x_ref, w_ref, dx_ref, dw_ref, xp, dyp):
        xp[0:CONV_PAD, :] = jnp.zeros((CONV_PAD, HEAD_DIM), F32)
        xp[CONV_PAD:, :] = x_ref[...]
        w = w_ref[...]
        y = _conv_taps(w, xp, T, CONV_PAD - (DN_CONV - 1))
        sg = _sigmoid(y)
        s = y * sg
        r = lax.rsqrt(jnp.sum(s * s, axis=-1, keepdims=True) + L2_EPS)
        n = s * r
        d = d_ref[0]
        ds = jnp.where(pl.program_id(0) < 2, r * (d - n * jnp.sum(d * n, axis=-1, keepdims=True)), d)
        dy = ds * (sg * (1.0 + y * (1.0 - sg)))
        dyp[0:T, :] = dy
        dyp[T:, :] = jnp.zeros((CONV_PAD, HEAD_DIM), F32)
        dx = w[0:1, :] * dyp[pl.ds(DN_CONV - 1, T), :]
        for i in range(1, DN_CONV):
            dx = dx + w[i:i + 1, :] * dyp[pl.ds(DN_CONV - 1 - i, T), :]
        dx_ref[...] = dx.astype(dx_ref.dtype)
        for i in range(DN_CONV):
            dw_ref[i:i + 1, :] = jnp.sum(dy * xp[pl.ds(CONV_PAD - (DN_CONV - 1) + i, T), :], axis=0, keepdims=True)

    col = lambda c, h: (0, c * HEADS + h)
    return pl.pallas_call(
        body, name=name, grid=(3, HEADS),
        in_specs=[pl.BlockSpec((1, T, HEAD_DIM), lambda c, h: (c, 0, h)), pl.BlockSpec((T, HEAD_DIM), col),
                  pl.BlockSpec((DN_CONV, HEAD_DIM), col)],
        out_specs=(pl.BlockSpec((T, HEAD_DIM), col), pl.BlockSpec((DN_CONV, HEAD_DIM), col)),
        out_shape=(jax.ShapeDtypeStruct((T, 3 * D_MODEL), BF16), jax.ShapeDtypeStruct((DN_CONV, 3 * D_MODEL), F32)),
        scratch_shapes=[pltpu.VMEM((T + CONV_PAD, HEAD_DIM), F32), pltpu.VMEM((T + CONV_PAD, HEAD_DIM), F32)],
        compiler_params=_params(("parallel", "parallel")),
    )(dqkv, proj, conv_w)


def _inv_unit_lower(low, eye):
    x = eye - low
    power = _hdot(low, low, _B_NN)
    steps = int(math.log2(DN_CHUNK)) - 1
    for s in range(steps):
        x = x + _hdot(x, power, _B_NN)
        if s + 1 < steps:
            power = _hdot(power, power, _B_NN)
    return x


_B_NN = (((2,), (1,)), ((0,), (0,)))
_B_NT = (((2,), (2,)), ((0,), (0,)))
_B_TN = (((1,), (1,)), ((0,), (0,)))


def _dn_load(ref, lead, r0, group):
    rows = pl.ds(r0, group * DN_CHUNK)
    cols = lambda h: slice(h * HEAD_DIM, (h + 1) * HEAD_DIM)
    per_head = [(ref[rows, cols(h)] if lead is None else ref[lead, rows, cols(h)]).reshape(group, DN_CHUNK, HEAD_DIM)
                for h in range(DN_HEADS)]
    return jnp.stack(per_head, axis=1).reshape(group * DN_HEADS, DN_CHUNK, HEAD_DIM)


def _dn_chunk_setup(qkv_ref, b_ref, a_ref, hp_ref, n0, group, tinv=None):
    C = DN_CHUNK
    B = group * DN_HEADS
    r0 = pl.multiple_of(n0 * C, C)
    q = _dn_load(qkv_ref, 0, r0, group) * QK_SCALE
    k = _dn_load(qkv_ref, 1, r0, group)
    v = _dn_load(qkv_ref, 2, r0, group)
    ii = lax.broadcasted_iota(jnp.int32, (B, C, C), 1)
    jj = lax.broadcasted_iota(jnp.int32, (B, C, C), 2)
    eye_mask = ii == jj
    eye = jnp.where(eye_mask, 1.0, 0.0).astype(F32)

    def to_col(row):
        return jnp.sum(jnp.where(eye_mask, jnp.broadcast_to(row, (B, C, C)), 0.0), axis=2, keepdims=True)

    def to_row(col):
        return jnp.sum(jnp.where(eye_mask, jnp.broadcast_to(col, (B, C, C)), 0.0), axis=1, keepdims=True)

    def rows(ref):
        return jnp.stack([ref[h, pl.ds(n0, group)] for h in range(DN_HEADS)], axis=1).reshape(B, 1, C)

    def per_head(row):
        return jnp.stack([hp_ref[h, row:row + 1, 0:C] for h in range(DN_HEADS)] * group, axis=0)

    b_row = rows(b_ref)
    a_row = rows(a_ref)
    a_log = per_head(0)
    dt_b = per_head(1)
    beta_row = _sigmoid(b_row)
    neg_ea = -jnp.exp(a_log)
    g_row = neg_ea * _softplus(a_row + dt_b)
    gc_col = jnp.sum(jnp.where(jj <= ii, jnp.broadcast_to(g_row, (B, C, C)), 0.0), axis=2, keepdims=True)
    gc_row = to_row(gc_col)
    g_last = jnp.sum(g_row, axis=2, keepdims=True)
    beta = to_col(beta_row)
    low_incl = ii >= jj
    decay = jnp.exp(jnp.where(low_incl, gc_col - gc_row, -jnp.inf))
    eg = jnp.exp(gc_col)
    egl = jnp.exp(g_last - gc_col)
    el = jnp.exp(g_last)
    kb = k * beta
    pmat = _bdot(kb, k, _B_NT)
    low = jnp.where(ii > jj, pmat * decay, 0.0)
    if tinv is None:
        tinv = _inv_unit_lower(low, eye)
    u = _hdot(tinv, v * beta, _B_NN)
    w = _hdot(tinv, kb * eg, _B_NN)
    qk = _bdot(q, k, _B_NT)
    attn = qk * decay
    return dict(q=q, k=k, v=v, ii=ii, jj=jj, to_col=to_col, to_row=to_row, b_row=b_row, a_row=a_row, dt_b=dt_b,
                beta_row=beta_row, neg_ea=neg_ea, g_row=g_row, gc_col=gc_col, g_last=g_last, beta=beta,
                decay=decay, eg=eg, egl=egl, el=el, kb=kb, pmat=pmat, tinv=tinv, u=u, w=w, qk=qk, attn=attn,
                qd=q * eg, kd=k * egl, r0=r0)


def _dn_store(ref, lead, r0, group, value):
    value = value.reshape(group, DN_HEADS, DN_CHUNK, HEAD_DIM)
    for h in range(DN_HEADS):
        block = value[:, h].reshape(group * DN_CHUNK, HEAD_DIM)
        if lead is None:
            ref[pl.ds(r0, group * DN_CHUNK), h * HEAD_DIM:(h + 1) * HEAD_DIM] = block
        else:
            ref[lead, pl.ds(r0, group * DN_CHUNK), h * HEAD_DIM:(h + 1) * HEAD_DIM] = block


def _dn_specs(T):
    nc = T // DN_CHUNK
    qkv = pl.BlockSpec((3, T, DN_HEADS * HEAD_DIM), lambda h: (0, 0, h))
    rows = pl.BlockSpec((DN_HEADS, nc, 1, DN_CHUNK), lambda h: (h, 0, 0, 0))
    hp = pl.BlockSpec((DN_HEADS, 8, 128), lambda h: (h, 0, 0))
    states = pl.BlockSpec((DN_HEADS, nc, HEAD_DIM, HEAD_DIM), lambda h: (h, 0, 0, 0))
    return nc, qkv, rows, hp, states


def _dn_inverse_spec(T):
    return pl.BlockSpec((DN_HEADS, T // DN_CHUNK, DN_CHUNK, DN_CHUNK), lambda h: (h, 0, 0, 0))


def _dn_per_head(ref, n0, group):
    stacked = jnp.stack([ref[h, pl.ds(n0, group)] for h in range(DN_HEADS)], axis=1)
    return stacked.reshape((group * DN_HEADS,) + stacked.shape[2:])


def _dn_fwd(qkv, b_rows, a_rows, hp, *, name):
    T = qkv.shape[1]
    nc, qkv_spec, row_spec, hp_spec, st_spec = _dn_specs(T)
    group = math.gcd(nc, DN_GROUP)
    H = DN_HEADS

    def body(qkv_ref, b_ref, a_ref, hp_ref, o_ref, st_ref, inv_ref, s_scr):
        s_scr[...] = jnp.zeros_like(s_scr)

        def step(t, carry):
            n0 = t * group
            c = _dn_chunk_setup(qkv_ref, b_ref, a_ref, hp_ref, n0, group)
            tinv = c["tinv"].reshape(group, H, DN_CHUNK, DN_CHUNK)
            for h in range(H):
                inv_ref[h, pl.ds(n0, group)] = tinv[:, h]
            state = s_scr[...]
            outs = []
            for g in range(group):
                sl = slice(g * H, (g + 1) * H)
                for h in range(H):
                    st_ref[h, n0 + g] = state[h]
                v_new = c["u"][sl] - _bdot(c["w"][sl], state, _B_NN)
                outs.append(_bdot(c["qd"][sl], state, _B_NN) + _bdot(c["attn"][sl], v_new, _B_NN))
                state = state * c["el"][sl] + _bdot(c["kd"][sl], v_new, _B_TN)
            s_scr[...] = state
            _dn_store(o_ref, None, c["r0"], group, jnp.concatenate(outs, axis=0))
            return carry

        lax.fori_loop(0, nc // group, step, 0)

    return pl.pallas_call(
        body, name=name, grid=(HEADS // H,), in_specs=[qkv_spec, row_spec, row_spec, hp_spec],
        out_specs=(pl.BlockSpec((T, H * HEAD_DIM), lambda h: (0, h)), st_spec, _dn_inverse_spec(T)),
        out_shape=(jax.ShapeDtypeStruct((T, D_MODEL), F32),
                   jax.ShapeDtypeStruct((HEADS, nc, HEAD_DIM, HEAD_DIM), F32),
                   jax.ShapeDtypeStruct((HEADS, nc, DN_CHUNK, DN_CHUNK), F32)),
        scratch_shapes=[pltpu.VMEM((H, HEAD_DIM, HEAD_DIM), F32)], compiler_params=_params(("parallel",)),
    )(qkv, b_rows, a_rows, hp)


def _dn_bwd(qkv, b_rows, a_rows, hp, states, inverses, do, *, name):
    T = qkv.shape[1]
    C = DN_CHUNK
    nc, qkv_spec, row_spec, hp_spec, st_spec = _dn_specs(T)
    group = math.gcd(nc, DN_GROUP)
    H = DN_HEADS
    B = group * H

    def body(qkv_ref, b_ref, a_ref, hp_ref, st_ref, inv_ref, do_ref, dqkv_ref, db_ref, da_ref, dhp_ref, ds_scr, acc_scr):
        ds_scr[...] = jnp.zeros_like(ds_scr)
        acc_scr[...] = jnp.zeros_like(acc_scr)

        def step(t, carry):
            n0 = nc - (t + 1) * group
            c = _dn_chunk_setup(qkv_ref, b_ref, a_ref, hp_ref, n0, group, tinv=_dn_per_head(inv_ref, n0, group))
            state = _dn_per_head(st_ref, n0, group)
            d_o = _dn_load(do_ref, None, c["r0"], group)
            v_new = c["u"] - _bdot(c["w"], state, _B_NN)
            d_vnew_local = _bdot(c["attn"], d_o, _B_TN)
            d_state_local = _bdot(c["qd"], d_o, _B_TN)
            d_state = ds_scr[...]
            d_vnew, d_kd, d_el = [None] * group, [None] * group, [None] * group
            for g in reversed(range(group)):
                sl = slice(g * H, (g + 1) * H)
                d_vnew[g] = d_vnew_local[sl] + _bdot(c["kd"][sl], d_state, _B_NN)
                d_kd[g] = _bdot(v_new[sl], d_state, _B_NT)
                d_el[g] = jnp.sum(jnp.sum(d_state * state[sl], axis=2, keepdims=True), axis=1, keepdims=True)
                d_state = d_state * c["el"][sl] + d_state_local[sl] - _bdot(c["w"][sl], d_vnew[g], _B_TN)
            ds_scr[...] = d_state
            chunk_grads(c, n0, state, d_o, v_new, jnp.concatenate(d_vnew, axis=0), jnp.concatenate(d_kd, axis=0),
                        jnp.concatenate(d_el, axis=0))
            return carry

        def chunk_grads(c, n0, state, d_o, v_new, d_vnew, d_kd, d_el):
            ii, jj = c["ii"], c["jj"]
            q, k, v, kb, beta = c["q"], c["k"], c["v"], c["kb"], c["beta"]
            decay, eg, egl, el = c["decay"], c["eg"], c["egl"], c["el"]
            u, w, tinv = c["u"], c["w"], c["tinv"]
            d_qd = _bdot(d_o, state, _B_NT)
            d_attn = _bdot(d_o, v_new, _B_NT)
            d_w = -_bdot(d_vnew, state, _B_NT)
            d_rv = _hdot(tinv, d_vnew, _B_TN)
            d_rw = _hdot(tinv, d_w, _B_TN)
            d_amat = -(_bdot(d_rv, u, _B_NT) + _bdot(d_rw, w, _B_NT))
            d_low = jnp.where(ii > jj, d_amat, 0.0)
            d_p = d_low * decay
            d_qk = d_attn * decay
            e_mat = (d_low * c["pmat"] + d_attn * c["qk"]) * decay
            d_q = _bdot(d_qk, k, _B_NN) + d_qd * eg
            d_kb = _bdot(d_p, k, _B_NN) + d_rw * eg
            d_k = _bdot(d_qk, q, _B_TN) + _bdot(d_p, kb, _B_TN) + d_kd * egl + d_kb * beta
            d_beta = jnp.sum(d_kb * k, axis=2, keepdims=True) + jnp.sum(d_rv * v, axis=2, keepdims=True)
            d_v = d_rv * beta
            d_eg = jnp.sum(d_qd * q, axis=2, keepdims=True) + jnp.sum(d_rw * kb, axis=2, keepdims=True)
            d_egl = jnp.sum(d_kd * k, axis=2, keepdims=True)
            d_glast = jnp.sum(d_egl * egl, axis=1, keepdims=True) + d_el * el
            row_sum = jnp.sum(e_mat, axis=2, keepdims=True)
            col_sum = c["to_col"](jnp.sum(e_mat, axis=1, keepdims=True))
            d_gc = row_sum - col_sum + d_eg * eg - d_egl * egl
            d_g_row = jnp.sum(jnp.where(ii >= jj, jnp.broadcast_to(d_gc, (B, C, C)), 0.0), axis=1, keepdims=True) + d_glast
            beta_row = c["beta_row"]
            d_b_row = c["to_row"](d_beta) * beta_row * (1.0 - beta_row)
            d_a_row = d_g_row * c["neg_ea"] * _sigmoid(c["a_row"] + c["dt_b"])
            _dn_store(dqkv_ref, 0, c["r0"], group, d_q * QK_SCALE)
            _dn_store(dqkv_ref, 1, c["r0"], group, d_k)
            _dn_store(dqkv_ref, 2, c["r0"], group, d_v)
            d_b_row = d_b_row.reshape(group, H, 1, C)
            d_a_row = d_a_row.reshape(group, H, 1, C)
            d_a_log = jnp.sum((d_g_row * c["g_row"]).reshape(group, H, 1, C), axis=0)
            d_dt_b = jnp.sum(d_a_row, axis=0)
            for h in range(H):
                db_ref[h, pl.ds(n0, group)] = d_b_row[:, h]
                da_ref[h, pl.ds(n0, group)] = d_a_row[:, h]
                acc_scr[h, 0:1, 0:C] += d_a_log[h]
                acc_scr[h, 1:2, 0:C] += d_dt_b[h]

        lax.fori_loop(0, nc // group, step, 0)
        for h in range(H):
            tot = jnp.sum(acc_scr[h], axis=1, keepdims=True)
            dhp_ref[h] = jnp.broadcast_to(tot, (8, 128))

    return pl.pallas_call(
        body, name=name, grid=(HEADS // H,),
        in_specs=[qkv_spec, row_spec, row_spec, hp_spec, st_spec, _dn_inverse_spec(T),
                  pl.BlockSpec((T, H * HEAD_DIM), lambda h: (0, h))],
        out_specs=(qkv_spec, row_spec, row_spec, hp_spec),
        out_shape=(jax.ShapeDtypeStruct((3, T, D_MODEL), F32), jax.ShapeDtypeStruct((HEADS, nc, 1, C), F32),
                   jax.ShapeDtypeStruct((HEADS, nc, 1, C), F32), jax.ShapeDtypeStruct((HEADS, 8, 128), F32)),
        scratch_shapes=[pltpu.VMEM((H, HEAD_DIM, HEAD_DIM), F32), pltpu.VMEM((H, 8, 128), F32)],
        compiler_params=_params(("parallel",)),
    )(qkv, b_rows, a_rows, hp, states, inverses, do)


COL_Z = 3 * HEADS


def _gated_norm_fwd(o, proj, gain, *, name):
    T = o.shape[0]

    def body(o_ref, z_ref, g_ref, out_ref):
        x = o_ref[...]
        r = lax.rsqrt(jnp.mean(x * x, axis=-1, keepdims=True) + RMS_EPS)
        z = z_ref[...]
        out_ref[...] = (x * r * g_ref[...] * (z * _sigmoid(z))).astype(out_ref.dtype)

    return pl.pallas_call(
        body, name=name, grid=(HEADS,),
        in_specs=[pl.BlockSpec((T, HEAD_DIM), lambda h: (0, h)), pl.BlockSpec((T, HEAD_DIM), lambda h: (0, COL_Z + h)),
                  pl.BlockSpec((1, HEAD_DIM), lambda h: (0, 0))],
        out_specs=pl.BlockSpec((T, HEAD_DIM), lambda h: (0, h)),
        out_shape=jax.ShapeDtypeStruct((T, D_MODEL), BF16), compiler_params=_params(("parallel",)),
    )(o, proj, gain)


def _gated_norm_bwd(dout, o, proj, gain, *, name):
    T = o.shape[0]

    def body(d_ref, o_ref, z_ref, g_ref, do_ref, dz_ref, dg_ref):
        x = o_ref[...]
        r = lax.rsqrt(jnp.mean(x * x, axis=-1, keepdims=True) + RMS_EPS)
        n = x * r
        z = z_ref[...]
        sg = _sigmoid(z)
        d = d_ref[...].astype(F32)
        g = g_ref[...]
        dz_ref[...] = (d * n * g * (sg * (1.0 + z * (1.0 - sg)))).astype(dz_ref.dtype)
        dy = d * (z * sg)
        dyg = dy * g
        do_ref[...] = r * (dyg - n * jnp.mean(dyg * n, axis=-1, keepdims=True))

        @pl.when(pl.program_id(0) == 0)
        def _():
            dg_ref[...] = jnp.zeros_like(dg_ref)

        dg_ref[...] += jnp.sum(dy * n, axis=0, keepdims=True)

    head = pl.BlockSpec((T, HEAD_DIM), lambda h: (0, h))
    vec = pl.BlockSpec((1, HEAD_DIM), lambda h: (0, 0))
    return pl.pallas_call(
        body, name=name, grid=(HEADS,),
        in_specs=[head, head, pl.BlockSpec((T, HEAD_DIM), lambda h: (0, COL_Z + h)), vec],
        out_specs=(head, head, vec),
        out_shape=(jax.ShapeDtypeStruct((T, D_MODEL), F32), jax.ShapeDtypeStruct((T, D_MODEL), BF16),
                   jax.ShapeDtypeStruct((1, HEAD_DIM), F32)),
        compiler_params=_params(("arbitrary",)),
    )(dout, o, proj, gain)


COL_SBQ = 4 * HEADS
COL_SBK = 5 * HEADS
COL_SBV = 6 * HEADS


def _split_dot(x, mat):
    lead = x.shape[:-1]
    x = x.reshape(-1, x.shape[-1])
    hi = x.astype(BF16)
    lo = (x - hi.astype(F32)).astype(BF16)
    out = jnp.dot(hi, mat, preferred_element_type=F32) + jnp.dot(lo, mat, preferred_element_type=F32)
    return out.reshape(lead + (mat.shape[-1],))


def _sb_specs(T, heads, buffers=None):
    col = lambda first: pl.BlockSpec((T, heads * HEAD_DIM), lambda h: (0, first // heads + h), pipeline_mode=buffers)
    return col(COL_SBQ), col(COL_SBK), col(COL_SBV), pl.BlockSpec((1, HEAD_DIM), lambda h: (0, 0))


def _heads_first(x):
    return jnp.stack([x[:, c:c + HEAD_DIM] for c in range(0, x.shape[1], HEAD_DIM)], axis=0)


def _heads_last(x):
    return jnp.concatenate([x[h] for h in range(x.shape[0])], axis=1)


def _head_rms(x):
    r = lax.rsqrt(jnp.mean(x * x, axis=-1, keepdims=True) + RMS_EPS)
    return x * r, r


def _sb_fwd(proj, q_gain, k_gain, *, name):
    T = proj.shape[0]
    B = SB_BLOCK
    H = SB_HEADS
    nb = T // B
    KT = min(SB_KEY_TILE, T)
    NS = KT // B
    q_spec, k_spec, v_spec, g_spec = _sb_specs(T, H)

    def body(q_ref, k_ref, v_ref, gq_ref, gk_ref, o_ref, lt_ref, qs, ks, vs):
        qs[...] = (_head_rms(_heads_first(q_ref[...]))[0] * (gq_ref[...] * QK_SCALE)).astype(BF16)
        ks[...] = (_head_rms(_heads_first(k_ref[...]))[0] * gk_ref[...]).astype(BF16)
        vs[...] = _heads_first(v_ref[...]).astype(BF16)
        ii = lax.broadcasted_iota(jnp.int32, (B, B), 0)
        jj = lax.broadcasted_iota(jnp.int32, (B, B), 1)
        after = jnp.where(ii > jj, 1.0, 0.0).astype(BF16)
        ahead = lax.broadcasted_iota(jnp.int32, (H, B, KT), 2) - lax.broadcasted_iota(jnp.int32, (H, B, KT), 1)

        def q_block(i, carry):
            rows = pl.ds(pl.multiple_of(i * B, B), B)
            q = qs[:, rows, :]

            def tile(c0, acc, tail, masked):
                cols = pl.ds(c0, KT)
                z = lax.dot_general(q, ks[:, cols, :], _B_NT, preferred_element_type=F32)
                sp = _softplus(z)
                causal = ahead < (i * B - c0)
                loss = jnp.where(causal, sp, 0.0) if masked else sp
                parts = [None] * NS
                for b in reversed(range(NS)):
                    blk = loss[:, :, b * B:(b + 1) * B]
                    parts[b] = _split_dot(blk, after) + tail
                    tail = tail + jnp.sum(blk, axis=2, keepdims=True)
                lost = parts[0] if NS == 1 else jnp.concatenate(parts, axis=2)
                wts = jnp.exp(z - sp - lost)
                if masked:
                    wts = jnp.where(causal, wts, 0.0)
                acc = acc + lax.dot_general(wts.astype(BF16), vs[:, cols, :], _B_NN, preferred_element_type=F32)
                return acc, tail

            last = i // NS
            acc, tail = tile(pl.multiple_of(last * KT, KT), jnp.zeros((H, B, HEAD_DIM), F32), jnp.zeros((H, B, 1), F32), True)
            acc, tail = lax.fori_loop(
                1, last + 1, lambda s, c: tile(pl.multiple_of((last - s) * KT, KT), c[0], c[1], False), (acc, tail))
            o_ref[rows, :] = _heads_last(acc).astype(o_ref.dtype)
            lt_ref[rows, :] = _heads_last(jnp.broadcast_to(tail, (H, B, HEAD_DIM)))
            return carry

        lax.fori_loop(0, nb, q_block, 0)

    heads = pl.BlockSpec((T, H * HEAD_DIM), lambda h: (0, h))
    return pl.pallas_call(
        body, name=name, grid=(HEADS // H,), in_specs=[q_spec, k_spec, v_spec, g_spec, g_spec],
        out_specs=(heads, heads),
        out_shape=(jax.ShapeDtypeStruct((T, D_MODEL), BF16), jax.ShapeDtypeStruct((T, D_MODEL), F32)),
        scratch_shapes=[pltpu.VMEM((H, T, HEAD_DIM), BF16)] * 3, compiler_params=_params(("parallel",)),
    )(proj, proj, proj, q_gain, k_gain)


def _sb_bwd(proj, q_gain, k_gain, ltot, do, *, name):
    T = proj.shape[0]
    B = SB_BLOCK
    H = SB_HEADS_BWD
    nb = T // B
    KT = min(SB_KEY_TILE_BWD, T)
    NS = KT // B
    q_spec, k_spec, v_spec, g_spec = _sb_specs(T, H, pl.Buffered(1))

    def body(q_ref, k_ref, v_ref, gq_ref, gk_ref, lt_ref, do_ref, dq_ref, dk_ref, dv_ref, dgq_ref, dgk_ref,
             qs, ks, vs, dos, dq_acc, dk_acc, dv_acc):
        head_cols = [slice(h * HEAD_DIM, (h + 1) * HEAD_DIM) for h in range(H)]
        for h, cols in enumerate(head_cols):
            qs[h] = (_head_rms(q_ref[:, cols])[0] * (gq_ref[...] * QK_SCALE)).astype(BF16)
            ks[h] = (_head_rms(k_ref[:, cols])[0] * gk_ref[...]).astype(BF16)
            vs[h] = v_ref[:, cols].astype(BF16)
            dos[h] = do_ref[:, cols].astype(BF16)
        dk_acc[...] = jnp.zeros_like(dk_acc)
        dv_acc[...] = jnp.zeros_like(dv_acc)
        ii = lax.broadcasted_iota(jnp.int32, (B, B), 0)
        jj = lax.broadcasted_iota(jnp.int32, (B, B), 1)
        upto = jnp.where(ii <= jj, 1.0, 0.0).astype(BF16)
        before = jnp.where(ii < jj, 1.0, 0.0).astype(BF16)
        ahead = lax.broadcasted_iota(jnp.int32, (H, B, KT), 2) - lax.broadcasted_iota(jnp.int32, (H, B, KT), 1)

        def q_block(i, carry):
            rows = pl.ds(pl.multiple_of(i * B, B), B)
            q = qs[:, rows, :]
            d_o = dos[:, rows, :]
            total = jnp.max(_heads_first(lt_ref[rows, :]), axis=2, keepdims=True)

            def tile(c0, dq, head_lb, head_de, masked):
                cols = pl.ds(c0, KT)
                k = ks[:, cols, :]
                v = vs[:, cols, :]
                z = lax.dot_general(q, k, _B_NT, preferred_element_type=F32)
                sp = _softplus(z)
                causal = ahead < (i * B - c0)
                loss = jnp.where(causal, sp, 0.0) if masked else sp
                parts = [None] * NS
                for b in range(NS):
                    blk = loss[:, :, b * B:(b + 1) * B]
                    parts[b] = _split_dot(blk, upto) + head_lb
                    head_lb = head_lb + jnp.sum(blk, axis=2, keepdims=True)
                prefix = parts[0] if NS == 1 else jnp.concatenate(parts, axis=2)
                wts = jnp.exp(z - sp + (prefix - total))
                if masked:
                    wts = jnp.where(causal, wts, 0.0)
                d_w = lax.dot_general(d_o, v, _B_NT, preferred_element_type=F32)
                d_e = wts * d_w
                d_eb = d_e.astype(BF16)
                for b in range(NS):
                    inside = jnp.dot(d_eb[:, :, b * B:(b + 1) * B].reshape(H * B, B), before, preferred_element_type=F32)
                    parts[b] = inside.reshape(H, B, B) + head_de
                    head_de = head_de + jnp.sum(d_e[:, :, b * B:(b + 1) * B], axis=2, keepdims=True)
                cum = parts[0] if NS == 1 else jnp.concatenate(parts, axis=2)
                sig = jnp.exp(z - sp)
                d_z = d_e - sig * (d_e + cum)
                if masked:
                    d_z = jnp.where(causal, d_z, 0.0)
                d_zb = d_z.astype(BF16)
                dq = dq + lax.dot_general(d_zb, k, _B_NN, preferred_element_type=F32)
                dk_acc[:, cols, :] += lax.dot_general(d_zb, q, _B_TN, preferred_element_type=F32)
                dv_acc[:, cols, :] += lax.dot_general(wts.astype(BF16), d_o, _B_TN, preferred_element_type=F32)
                return dq, head_lb, head_de

            last = i // NS
            zero = jnp.zeros((H, B, 1), F32)
            state = lax.fori_loop(0, last, lambda t, c: tile(pl.multiple_of(t * KT, KT), *c, False),
                                  (jnp.zeros((H, B, HEAD_DIM), F32), zero, zero))
            dq, _, _ = tile(pl.multiple_of(last * KT, KT), *state, True)
            dq_acc[:, rows, :] = dq * QK_SCALE
            return carry

        lax.fori_loop(0, nb, q_block, 0)

        def norm_bwd(d_scaled, x, gain):
            n, r = _head_rms(x)
            dn = d_scaled * gain
            return r * (dn - n * jnp.mean(dn * n, axis=-1, keepdims=True)), jnp.sum(d_scaled * n, axis=0, keepdims=True)

        @pl.when(pl.program_id(0) == 0)
        def _():
            dgq_ref[...] = jnp.zeros_like(dgq_ref)
            dgk_ref[...] = jnp.zeros_like(dgk_ref)

        for h, cols in enumerate(head_cols):
            dq_raw, dgq = norm_bwd(dq_acc[h], q_ref[:, cols], gq_ref[...])
            dk_raw, dgk = norm_bwd(dk_acc[h], k_ref[:, cols], gk_ref[...])
            dq_ref[:, cols] = dq_raw.astype(dq_ref.dtype)
            dk_ref[:, cols] = dk_raw.astype(dk_ref.dtype)
            dv_ref[:, cols] = dv_acc[h].astype(dv_ref.dtype)
            dgq_ref[...] += dgq
            dgk_ref[...] += dgk

    heads = pl.BlockSpec((T, H * HEAD_DIM), lambda h: (0, h), pipeline_mode=pl.Buffered(1))
    out = jax.ShapeDtypeStruct((T, D_MODEL), BF16)
    vec = jax.ShapeDtypeStruct((1, HEAD_DIM), F32)
    return pl.pallas_call(
        body, name=name, grid=(HEADS // H,), in_specs=[q_spec, k_spec, v_spec, g_spec, g_spec, heads, heads],
        out_specs=(heads, heads, heads, g_spec, g_spec), out_shape=(out, out, out, vec, vec),
        scratch_shapes=[pltpu.VMEM((H, T, HEAD_DIM), BF16)] * 4 + [pltpu.VMEM((H, T, HEAD_DIM), F32)] * 3,
        compiler_params=_params(("arbitrary",), vmem=V7X_VMEM_LIMIT_HIGH),
    )(proj, proj, proj, q_gain, k_gain, ltot, do)


ADAM_ROWS = 256


def _adamw(g_parts, w, m, v, *, name, layer=0, earlier=None):
    K, A, C = g_parts.shape
    R = w.shape[0]
    tr = next((t for t in (ADAM_ROWS, ADAM_ROWS // 2) if A % t == 0), A // 2 if A % 32 == 0 else A)
    first_block = layer * (A // tr)

    def body(g_ref, w_ref, m_ref, v_ref, *rest):
        go_ref, d_ref, mo_ref, vo_ref = rest[-4:]
        g = g_ref[0].astype(F32)
        for k in range(1, K):
            g = g + g_ref[k].astype(F32)
        go_ref[...] = g
        m_new = ADAM_B1 * m_ref[...] + (1.0 - ADAM_B1) * g
        v_new = ADAM_B2 * v_ref[...] + (1.0 - ADAM_B2) * (g * g)
        m_hat = m_new / (1.0 - ADAM_B1 ** ADAM_STEP)
        v_hat = v_new / (1.0 - ADAM_B2 ** ADAM_STEP)
        d_ref[...] = -ADAM_LR * (m_hat / (jnp.sqrt(v_hat) + ADAM_EPS) + ADAM_WD * w_ref[...])
        mo_ref[...] = m_new
        vo_ref[...] = v_new

    row = pl.BlockSpec((tr, C), lambda i: (first_block + i, 0))
    out = jax.ShapeDtypeStruct((R, C), F32)
    in_specs = [pl.BlockSpec((K, tr, C), lambda i: (0, i, 0)), row, row, row]
    if earlier is None:
        return pl.pallas_call(
            body, name=name, grid=(A // tr,), in_specs=in_specs, out_specs=(row, row, row, row),
            out_shape=(out, out, out, out), compiler_params=_params(("parallel",)),
        )(g_parts, w, m, v)
    return pl.pallas_call(
        body, name=name, grid=(A // tr,), in_specs=in_specs + [ANY] * 4, out_specs=(row, row, row, row),
        out_shape=(out, out, out, out), input_output_aliases={4 + j: j for j in range(4)},
        compiler_params=_params(("parallel",)),
    )(g_parts, w, m, v, *earlier)


def _sum_parts(parts, *, name):
    K, R, C = parts.shape

    def body(p_ref, o_ref):
        acc = p_ref[0]
        for k in range(1, K):
            acc = acc + p_ref[k]
        o_ref[...] = acc

    return pl.pallas_call(body, name=name, out_shape=jax.ShapeDtypeStruct((R, C), F32))(parts)


def _position():
    return lax.axis_index("x"), lax.axis_index("y"), lax.axis_index("c")


def _all_gather(shards, *, name):
    n = len(shards)

    def body(*refs):
        x_refs, out_refs = refs[:n], refs[n:2 * n]
        send_sems, recv_sems, local_sems = refs[2 * n:]
        x, y, c = _position()
        me, sibling = (x, y, c), (x, y, 1 - c)
        chips = [(1 - x, y), (x, 1 - y), (1 - x, 1 - y)]

        def slot(a, px, py, pc):
            return out_refs[a].at[4 * px + 2 * py + pc]

        def copy(a, k, block, to, own=False):
            return pltpu.make_async_remote_copy(
                src_ref=x_refs[a] if own else slot(a, *block), dst_ref=slot(a, *block),
                send_sem=send_sems.at[a, k], recv_sem=recv_sems.at[a, k], device_id=to, device_id_type=MESH)

        mine = [pltpu.make_async_copy(x_refs[a], slot(a, *me), local_sems.at[a]) for a in range(n)]
        for cp in mine:
            cp.start()
        first = [copy(a, 1 + j, me, (*chip, c), own=True) for j, chip in enumerate(chips) for a in range(n)]
        first += [copy(a, 0, me, sibling, own=True) for a in range(n)]
        for cp in first:
            cp.start()
        passed = []
        for j, chip in enumerate(chips):
            for a in range(n):
                copy(a, 1 + j, (*chip, c), me).wait_recv()
                passed.append(copy(a, 4 + j, (*chip, c), sibling))
                passed[-1].start()
        for a in range(n):
            copy(a, 0, sibling, me).wait_recv()
        for j, chip in enumerate(chips):
            for a in range(n):
                copy(a, 4 + j, (*chip, 1 - c), me).wait_recv()
        for cp in first + passed:
            cp.wait_send()
        for cp in mine:
            cp.wait()

    return pl.pallas_call(
        body, name=name, in_specs=[ANY] * n, out_specs=[ANY] * n,
        out_shape=[jax.ShapeDtypeStruct((N_DEV,) + s.shape, s.dtype) for s in shards],
        scratch_shapes=[pltpu.SemaphoreType.DMA((n, 7)), pltpu.SemaphoreType.DMA((n, 7)), pltpu.SemaphoreType.DMA((n,))],
    )(*shards)


HBM = pl.BlockSpec(memory_space=pltpu.HBM)
SEM = pl.BlockSpec(memory_space=pltpu.SEMAPHORE)
DATAFLOW = pltpu.SideEffectType.DATAFLOW_SIDE_EFFECTING


def _exchange_copies(gather, x_refs, land_refs, send_sems, recv_sems, local_sems):
    n = len(x_refs)
    x, y, c = _position()
    me = 4 * x + 2 * y + c

    def src(a, slot):
        return x_refs[a] if gather else x_refs[a].at[slot]

    mine = [pltpu.make_async_copy(src(a, me), land_refs[a].at[me], local_sems.at[a]) for a in range(n)]
    sends, recvs = [], []
    for k in range(1, N_DEV):
        px, py, pc = (x + (k >> 2)) % 2, (y + ((k >> 1) & 1)) % 2, (c + (k & 1)) % 2
        peer = 4 * px + 2 * py + pc
        for a in range(n):
            sems = dict(send_sem=send_sems.at[7 * a + k - 1], recv_sem=recv_sems.at[7 * a + k - 1],
                        device_id=(px, py, pc), device_id_type=MESH)
            sends.append(pltpu.make_async_remote_copy(src_ref=src(a, peer), dst_ref=land_refs[a].at[me], **sems))
            recvs.append(pltpu.make_async_remote_copy(src_ref=src(a, me), dst_ref=land_refs[a].at[peer], **sems))
    return mine, sends, recvs


def _exchange_start(parts, *, gather, name):
    n = len(parts)

    def body(*refs):
        x_refs, land_refs = refs[:n], refs[n:2 * n]
        send_sems, recv_sems, local_sems = refs[2 * n:2 * n + 3]
        token = refs[-1]
        mine, sends, _ = _exchange_copies(gather, x_refs, land_refs, send_sems, recv_sems, local_sems)
        for cp in mine + sends:
            cp.start()
        token[...] = jnp.zeros_like(token)

    sems = (pltpu.SemaphoreType.DMA((7 * n,)), pltpu.SemaphoreType.DMA((7 * n,)), pltpu.SemaphoreType.DMA((n,)))
    thru = tuple(pltpu.HBM(p.shape, p.dtype) for p in parts)
    land = tuple(pltpu.HBM(((N_DEV,) if gather else ()) + p.shape, p.dtype) for p in parts)
    res = pl.pallas_call(
        body, name=name, in_specs=[HBM] * (2 * n),
        out_specs=(SEM, SEM, SEM) + (HBM,) * (2 * n) + (pl.BlockSpec(memory_space=pltpu.VMEM),),
        out_shape=sems + thru + land + (jax.ShapeDtypeStruct((8, 128), F32),),
        input_output_aliases={a: 3 + a for a in range(2 * n)},
        compiler_params=pltpu.CompilerParams(has_side_effects=DATAFLOW),
    )(*[pltpu.with_memory_space_constraint(p, pltpu.HBM) for p in parts],
      *[pltpu.with_memory_space_constraint(lax.empty(z.shape, z.dtype), pltpu.HBM) for z in land])
    return res[:3], res[3:3 + n], res[3 + n:3 + 2 * n], res[-1]


def _exchange_wait(sems, parts, landing, after, *, gather, name):
    n = len(parts)
    after = list(after)

    def body(*refs):
        x_refs, land_refs = refs[:n], refs[n:2 * n]
        send_sems, recv_sems, local_sems = refs[2 * n:2 * n + 3]
        token = refs[-1]
        mine, sends, recvs = _exchange_copies(gather, x_refs, land_refs, send_sems, recv_sems, local_sems)
        for cp in recvs:
            cp.wait_recv()
        for cp in sends:
            cp.wait_send()
        for cp in mine:
            cp.wait()
        token[...] = jnp.zeros_like(token)

    thru = tuple(pltpu.HBM(p.shape, p.dtype) for p in tuple(parts) + tuple(landing))
    res = pl.pallas_call(
        body, name=name, in_specs=[HBM] * (2 * n) + [SEM, SEM, SEM] + [ANY] * len(after),
        out_specs=(HBM,) * (2 * n) + (pl.BlockSpec(memory_space=pltpu.VMEM),),
        out_shape=thru + (jax.ShapeDtypeStruct((8, 128), F32),), input_output_aliases={a: a for a in range(2 * n)},
        compiler_params=pltpu.CompilerParams(has_side_effects=DATAFLOW),
    )(*parts, *landing, *sems, *after)
    return res[n:2 * n], res[-1]


def _relay_copies(x_refs, land_refs, first_send, first_recv, relay_send, relay_recv, local_sems):
    n = len(x_refs)
    x, y, c = _position()
    sibling = (x, y, 1 - c)
    chips = [(1 - x, y), (x, 1 - y), (1 - x, 1 - y)]

    def slot(a, px, py, pc):
        return land_refs[a].at[4 * px + 2 * py + pc]

    def hop(a, k, block, to, own=False):
        return pltpu.make_async_remote_copy(
            src_ref=x_refs[a] if own else slot(a, *block), dst_ref=slot(a, *block),
            send_sem=first_send.at[4 * a + k], recv_sem=first_recv.at[4 * a + k], device_id=to, device_id_type=MESH)

    def relay(a, j, block, to):
        return pltpu.make_async_remote_copy(
            src_ref=slot(a, *block), dst_ref=slot(a, *block),
            send_sem=relay_send.at[3 * a + j], recv_sem=relay_recv.at[3 * a + j], device_id=to, device_id_type=MESH)

    me = (x, y, c)
    mine = [pltpu.make_async_copy(x_refs[a], slot(a, *me), local_sems.at[a]) for a in range(n)]
    sends = [hop(a, 1 + j, me, (*chip, c), own=True) for j, chip in enumerate(chips) for a in range(n)]
    sends += [hop(a, 0, me, sibling, own=True) for a in range(n)]
    over_ici = [hop(a, 1 + j, (*chip, c), me) for j, chip in enumerate(chips) for a in range(n)]
    from_sibling = [hop(a, 0, sibling, me) for a in range(n)]
    if relay_send is None:
        return mine, sends, over_ici, from_sibling, [], []
    relays = [relay(a, j, (*chip, c), sibling) for j, chip in enumerate(chips) for a in range(n)]
    relayed = [relay(a, j, (*chip, 1 - c), me) for j, chip in enumerate(chips) for a in range(n)]
    return mine, sends, over_ici, from_sibling, relays, relayed


def _relay_call(body, n_sem_in, n_sem_out, shards, landing, sems_in, after, name):
    n = len(shards)
    thru = tuple(pltpu.HBM(p.shape, p.dtype) for p in tuple(shards) + tuple(landing))
    res = pl.pallas_call(
        body, name=name, in_specs=[HBM] * (2 * n) + [SEM] * n_sem_in + [ANY] * len(after),
        out_specs=(SEM,) * len(n_sem_out) + (HBM,) * (2 * n) + (pl.BlockSpec(memory_space=pltpu.VMEM),),
        out_shape=tuple(pltpu.SemaphoreType.DMA((k,)) for k in n_sem_out) + thru + (jax.ShapeDtypeStruct((8, 128), F32),),
        input_output_aliases={a: len(n_sem_out) + a for a in range(2 * n)},
        compiler_params=pltpu.CompilerParams(has_side_effects=DATAFLOW),
    )(*shards, *landing, *sems_in, *after)
    k = len(n_sem_out)
    return res[:k], res[k:k + n], res[k + n:k + 2 * n], res[-1]


def _layer_refs(refs, layers):
    return [r if l is None else r.at[l] for r, l in zip(refs, layers)]


def _relay_gather_start(shards, layers, after, *, name):
    n = len(shards)

    def body(*refs):
        x_refs, land_refs = _layer_refs(refs[:n], layers), refs[n:2 * n]
        first_send, first_recv, local_sems = refs[2 * n + len(after):2 * n + len(after) + 3]
        mine, sends, *_ = _relay_copies(x_refs, land_refs, first_send, first_recv, None, None, local_sems)
        for cp in mine + sends:
            cp.start()
        refs[-1][...] = jnp.zeros_like(refs[-1])

    landing = [pltpu.with_memory_space_constraint(
        lax.empty((N_DEV,) + (s.shape if l is None else s.shape[1:]), s.dtype), pltpu.HBM) for s, l in zip(shards, layers)]
    shards = [pltpu.with_memory_space_constraint(s, pltpu.HBM) for s in shards]
    return _relay_call(body, 0, (4 * n, 4 * n, n), shards, landing, (), list(after), name)


def _relay_gather_pass_on(first, shards, layers, landing, after, *, name):
    n = len(shards)

    def body(*refs):
        x_refs, land_refs = _layer_refs(refs[:n], layers), refs[n:2 * n]
        first_send, first_recv, local_sems = refs[2 * n:2 * n + 3]
        relay_send, relay_recv = refs[2 * n + 3 + len(after):2 * n + 5 + len(after)]
        _, _, over_ici, _, relays, _ = _relay_copies(x_refs, land_refs, first_send, first_recv, relay_send, relay_recv,
                                                   local_sems)
        for arrival, cp in zip(over_ici, relays):
            arrival.wait_recv()
            cp.start()
        refs[-1][...] = jnp.zeros_like(refs[-1])

    return _relay_call(body, 3, (3 * n, 3 * n), shards, landing, first, list(after), name)


def _relay_gather_wait(first, relay, shards, layers, landing, after, *, name):
    n = len(shards)

    def body(*refs):
        x_refs, land_refs = _layer_refs(refs[:n], layers), refs[n:2 * n]
        first_send, first_recv, local_sems, relay_send, relay_recv = refs[2 * n:2 * n + 5]
        mine, sends, _, from_sibling, relays, relayed = _relay_copies(
            x_refs, land_refs, first_send, first_recv, relay_send, relay_recv, local_sems)
        for cp in from_sibling + relayed:
            cp.wait_recv()
        for cp in sends + relays:
            cp.wait_send()
        for cp in mine:
            cp.wait()
        refs[-1][...] = jnp.zeros_like(refs[-1])

    _, shards, landing, token = _relay_call(body, 5, (), shards, landing, tuple(first) + tuple(relay), list(after), name)
    return shards, landing, token


def _ffn_fwd(x, gain, wg_in, wg_out, tag):
    T, D = x.shape
    fb, rb = wg_in.shape[-1], wg_out.shape[-2]
    tm, tn = min(T, 1024), 512
    h = _rmsnorm_fwd(x, gain, name=f"{tag}_norm")
    p = _mm(name=f"{tag}_in", grid=(T // tm, N_DEV, 1), tile=(tm, fb),
            a=h, a_spec=pl.BlockSpec((tm, D), lambda i, j, k: (i, 0)),
            b=wg_in, b_spec=pl.BlockSpec((None, D, fb), lambda i, j, k: (j, 0, 0)),
            out_shape=jax.ShapeDtypeStruct((N_DEV, T, fb), BF16), o_spec=pl.BlockSpec((None, tm, fb), lambda i, j, k: (j, i, 0)))
    a = _swiglu_fwd(p, name=f"{tag}_act")
    y = _mm(name=f"{tag}_out", grid=(T // tm, 1, FF_HALF), tile=(tm, D), resid=x, scale=0.5,
            a=a, a_spec=pl.BlockSpec((None, tm, fb), lambda i, j, k: (k, i, 0)),
            b=wg_out.reshape(N_DEV * rb, D), b_spec=pl.BlockSpec((fb, D), lambda i, j, k: (k, 0)),
            out_shape=jax.ShapeDtypeStruct((T, D), F32), o_spec=pl.BlockSpec((tm, D), lambda i, j, k: (i, 0)))
    return y, (x, h, p, a)


def _ffn_bwd(dy, saved, gain, wg_in, wg_out, tag, on_weight_grads=None):
    x, h, p, a = saved
    T, D = x.shape
    fb, rb = wg_in.shape[-1], wg_out.shape[-2]
    tm, tn = min(T, 1024), 512
    da = _mm(name=f"{tag}_out_dx", grid=(T // tm, FF_HALF, 1), tile=(tm, fb), tb=True, scale=0.5,
             a=dy, a_spec=pl.BlockSpec((tm, D), lambda i, j, k: (i, 0)),
             b=wg_out.reshape(N_DEV * rb, D), b_spec=pl.BlockSpec((fb, D), lambda i, j, k: (j, 0)),
             out_shape=jax.ShapeDtypeStruct((FF_HALF, T, fb), BF16), o_spec=pl.BlockSpec((None, tm, fb), lambda i, j, k: (j, i, 0)))
    d_w_out = _mm(name=f"{tag}_out_dw", grid=(FF_HALF, D // tn, 1), tile=(fb, tn), ta=True, scale=0.5,
                  a=a, a_spec=pl.BlockSpec((None, T, fb), lambda i, j, k: (i, 0, 0)),
                  b=dy, b_spec=pl.BlockSpec((T, tn), lambda i, j, k: (0, j)),
                  out_shape=jax.ShapeDtypeStruct((FF_HALF, fb, D), BF16), o_spec=pl.BlockSpec((None, fb, tn), lambda i, j, k: (i, 0, j)))
    dp = _swiglu_bwd(da, p, name=f"{tag}_act_bwd")
    d_w_in = _mm(name=f"{tag}_in_dw", grid=(1, N_DEV, 1), tile=(D, fb), ta=True,
                 a=h, a_spec=pl.BlockSpec((T, D), lambda i, j, k: (0, 0)),
                 b=dp, b_spec=pl.BlockSpec((None, T, fb), lambda i, j, k: (j, 0, 0)),
                 out_shape=jax.ShapeDtypeStruct((N_DEV, D, fb), BF16), o_spec=pl.BlockSpec((None, D, fb), lambda i, j, k: (j, 0, 0)))
    dh = _mm(name=f"{tag}_in_dx", grid=(T // tm, 1, N_DEV), tile=(tm, D), tb=True,
             a=dp, a_spec=pl.BlockSpec((None, tm, fb), lambda i, j, k: (k, i, 0)),
             b=wg_in, b_spec=pl.BlockSpec((None, D, fb), lambda i, j, k: (k, 0, 0)),
             out_shape=jax.ShapeDtypeStruct((T, D), F32), o_spec=pl.BlockSpec((tm, D), lambda i, j, k: (i, 0)))
    d_w_out = d_w_out.reshape(N_DEV, rb, D)
    if on_weight_grads is not None:
        gain = gain + on_weight_grads(d_w_in, d_w_out)[0, 0]
    dx, d_gain = _rmsnorm_bwd(dh, x, gain, dy, name=f"{tag}_norm_bwd")
    return dx, d_gain, d_w_in, d_w_out


def _square_mm(a, wg, *, name, transposed=False, out_dtype=F32, resid=None):
    T, D = a.shape
    w = wg.reshape(D, D)
    return _matmul(a, w, tb=transposed, name=name, out_dtype=out_dtype, resid=resid)


def _head_rows(cols, T):
    return cols.T.reshape(HEADS, T // DN_CHUNK, 1, DN_CHUNK)


def _mixer_fwd(x, w, big, tag):
    T = x.shape[0]
    h = _rmsnorm_fwd(x, w["mix_norm"], name=f"{tag}_norm")
    proj = _matmul(h, big["w_main"], name=f"{tag}_proj")
    scal = _matmul(h, big["w_scal"], name=f"{tag}_proj_scal", tn=N_SCAL)
    qkv = _conv_fwd(proj, big["conv_w"], name=f"{tag}_conv")
    b_rows = _head_rows(scal[:, 0:HEADS], T)
    a_rows = _head_rows(scal[:, HEADS:2 * HEADS], T)
    o_a, *states = _dn_fwd(qkv, b_rows, a_rows, w["hp"], name=f"{tag}_dn")
    oa_n = _gated_norm_fwd(o_a, proj, w["dn_out_norm"], name=f"{tag}_dn_norm")
    ya = _square_mm(oa_n, big["w_branch_a"], name=f"{tag}_branch_a")
    o_b, ltot = _sb_fwd(proj, w["sb_q_norm"], w["sb_k_norm"], name=f"{tag}_sb")
    yb = _square_mm(o_b, big["w_branch_b"], name=f"{tag}_branch_b")
    merged = _merge_fwd(ya, yb, proj, name=f"{tag}_merge")
    y = _square_mm(merged, big["w_out"], name=f"{tag}_out", resid=x)
    return y, (x, h, proj, qkv, b_rows, a_rows, o_a, states, oa_n, ya, o_b, ltot, yb, merged)


def _mixer_bwd(dy, saved, w, big, tag, on_weight_grads):
    x, h, proj, qkv, b_rows, a_rows, o_a, states, oa_n, ya, o_b, ltot, yb, merged = saved
    T = x.shape[0]
    g = {}
    d_merged = _square_mm(dy, big["w_out"], transposed=True, name=f"{tag}_out_dx", out_dtype=BF16)
    g["w_out"] = _matmul(merged, dy, ta=True, name=f"{tag}_out_dw", out_dtype=BF16)
    d_ya, d_yb, d_ga, d_gb = _merge_bwd(d_merged, ya, yb, proj, name=f"{tag}_merge_bwd")
    d_oan = _square_mm(d_ya, big["w_branch_a"], transposed=True, name=f"{tag}_branch_a_dx")
    g["w_branch_a"] = _matmul(oa_n, d_ya, ta=True, name=f"{tag}_branch_a_dw", out_dtype=BF16)
    d_ob = _square_mm(d_yb, big["w_branch_b"], transposed=True, name=f"{tag}_branch_b_dx")
    g["w_branch_b"] = _matmul(o_b, d_yb, ta=True, name=f"{tag}_branch_b_dw", out_dtype=BF16)
    d_oa, d_z, g["dn_out_norm"] = _gated_norm_bwd(d_oan, o_a, proj, w["dn_out_norm"], name=f"{tag}_dn_norm_bwd")
    d_qkv, d_b_rows, d_a_rows, d_hp = _dn_bwd(qkv, b_rows, a_rows, w["hp"], *states, d_oa, name=f"{tag}_dn_bwd")
    g["dn_a_log"] = d_hp[:, 0, 0]
    g["dn_dt_bias"] = d_hp[:, 1, 0]
    d_conv_in, g["conv_w"] = _conv_bwd(d_qkv, proj, big["conv_w"], name=f"{tag}_conv_bwd")
    d_sbq, d_sbk, d_sbv, g["sb_q_norm"], g["sb_k_norm"] = _sb_bwd(
        proj, w["sb_q_norm"], w["sb_k_norm"], ltot, d_ob, name=f"{tag}_sb_bwd")
    d_proj = jnp.concatenate([d_conv_in, d_z, d_sbq, d_sbk, d_sbv, d_ga, d_gb], axis=1)
    d_scal = jnp.concatenate([d_b_rows.reshape(HEADS, T).T, d_a_rows.reshape(HEADS, T).T,
                              jnp.zeros((T, N_SCAL - 2 * HEADS), F32)], axis=1).astype(BF16)
    g["w_main"] = _matmul(h, d_proj, ta=True, name=f"{tag}_proj_dw", out_dtype=BF16)
    g["w_scal"] = _matmul(h, d_scal, ta=True, name=f"{tag}_proj_scal_dw", out_dtype=BF16, tn=N_SCAL)
    dh_scal = _matmul(d_scal, big["w_scal"], tb=True, name=f"{tag}_proj_scal_dx")
    dh = _matmul(d_proj, big["w_main"], tb=True, name=f"{tag}_proj_dx", tk=N_MAIN // 4, resid=dh_scal)
    gain = w["mix_norm"] + on_weight_grads(g)[0, 0]
    dx, g["mix_norm"] = _rmsnorm_bwd(dh, x, gain, dy, name=f"{tag}_norm_bwd")
    return dx, g


def _local_step(x, target, layers, weights_of, on_weight_grads):
    saved, bigs = [], []
    for l, w in enumerate(layers):
        big = weights_of(l, 0, x)
        x, s1 = _ffn_fwd(x, w["ffn1_norm"] + big["issued"], big["ffn1_w_in"], big["ffn1_w_out"], f"l{l}_ffn1")
        big.update(weights_of(l, 1, x))
        x, s2 = _mixer_fwd(x, dict(w, mix_norm=w["mix_norm"] + big["issued"]), big, f"l{l}_mix")
        big.update(weights_of(l, 2, x))
        x, s3 = _ffn_fwd(x, w["ffn2_norm"] + big["issued"], big["ffn2_w_in"], big["ffn2_w_out"], f"l{l}_ffn2")
        saved.append((s1, s2, s3))
        bigs.append(big)
    loss, dx = _loss_head(x, target, name="loss_head")
    small = [None] * len(layers)
    for l in reversed(range(len(layers))):
        w, big = layers[l], bigs[l]
        s1, s2, s3 = saved[l]
        dx, g_n2, _, _ = _ffn_bwd(
            dx, s3, w["ffn2_norm"], big["ffn2_w_in"], big["ffn2_w_out"], f"l{l}_ffn2",
            on_weight_grads=lambda g_in, g_out, l=l: on_weight_grads(l, 0, dict(ffn2_w_in=g_in, ffn2_w_out=g_out)))
        dx, g = _mixer_bwd(dx, s2, w, big, f"l{l}_mix", on_weight_grads=lambda g, l=l: on_weight_grads(l, 1, g))
        dx, g_n1, _, _ = _ffn_bwd(
            dx, s1, w["ffn1_norm"], big["ffn1_w_in"], big["ffn1_w_out"], f"l{l}_ffn1",
            on_weight_grads=lambda g_in, g_out, l=l: on_weight_grads(l, 2, dict(ffn1_w_in=g_in, ffn1_w_out=g_out)))
        small[l] = dict(g, ffn1_norm=g_n1, ffn2_norm=g_n2)
    return loss, dx, small


_BIG = ("ffn1_w_in", "ffn1_w_out", "w_in", "w_branch_a", "w_branch_b", "w_out", "ffn2_w_in", "ffn2_w_out")
_STAGES = (("ffn2_w_in", "ffn2_w_out"), ("w_in", "w_branch_a", "w_branch_b", "w_out"), ("ffn1_w_in", "ffn1_w_out"))
_SMALL = ("ffn1_norm", "mix_norm", "ffn2_norm", "dn_a_log", "dn_dt_bias", "dn_out_norm", "sb_q_norm", "sb_k_norm")
_ORDER = ("ffn1_norm", "ffn1_w_in", "ffn1_w_out", "mix_norm", "w_in", "dn_conv_w", "dn_a_log", "dn_dt_bias", "dn_out_norm",
          "sb_q_norm", "sb_k_norm", "w_branch_a", "w_branch_b", "w_out", "ffn2_norm", "ffn2_w_in", "ffn2_w_out")
COL_SCAL = 4 * D_MODEL
SCAL_SLOT = COL_SCAL // (N_IN // N_DEV)
SCAL_AT = COL_SCAL % (N_IN // N_DEV)
assert SCAL_AT + 2 * HEADS <= N_IN // N_DEV


def _pad_rows(a, multiple):
    pad = (-a.shape[-2]) % multiple
    return a if pad == 0 else jnp.pad(a, [(0, 0)] * (a.ndim - 2) + [(0, pad), (0, 0)])


def _lane_rows(a):
    flat = a.reshape(-1)
    flat = jnp.pad(flat, (0, (-flat.shape[0]) % 128))
    return flat.reshape(-1, 128)


def _pack_small(named):
    pieces, spans, r = [], {}, 0
    for n, a in named:
        rows = _lane_rows(a)
        spans[n] = (r, r + rows.shape[0], a.shape)
        r += rows.shape[0]
        pieces.append(rows)
    return _pad_rows(jnp.concatenate(pieces, axis=0), 8), spans


def _unpack_small(packed, spans, n):
    r0, r1, shape = spans[n]
    return packed[r0:r1].reshape(-1)[:math.prod(shape)].reshape(shape)


def kernel(x, ffn1_norm, ffn1_w_in, ffn1_w_out, mix_norm, w_in, dn_conv_w, dn_a_log, dn_dt_bias, dn_out_norm, sb_q_norm, sb_k_norm, w_branch_a, w_branch_b, w_out, ffn2_norm, ffn2_w_in, ffn2_w_out, loss_target, m_ffn1_norm, m_ffn1_w_in, m_ffn1_w_out, m_mix_norm, m_w_in, m_dn_conv_w, m_dn_a_log, m_dn_dt_bias, m_dn_out_norm, m_sb_q_norm, m_sb_k_norm, m_w_branch_a, m_w_branch_b, m_w_out, m_ffn2_norm, m_ffn2_w_in, m_ffn2_w_out, v_ffn1_norm, v_ffn1_w_in, v_ffn1_w_out, v_mix_norm, v_w_in, v_dn_conv_w, v_dn_a_log, v_dn_dt_bias, v_dn_out_norm, v_sb_q_norm, v_sb_k_norm, v_w_branch_a, v_w_branch_b, v_w_out, v_ffn2_norm, v_ffn2_w_in, v_ffn2_w_out):
    given = dict(locals())
    weights = {n: given[n] for n in _ORDER}
    mom_m = {n: given["m_" + n] for n in _ORDER}
    mom_v = {n: given["v_" + n] for n in _ORDER}
    L = ffn1_norm.shape[0]
    ax, ay, ac = _position()
    my_slot = 4 * ax + 2 * ay + ac

    conv_cols = dn_conv_w.shape[-1]
    second_ffn, mixer, first_ffn = _STAGES
    later = mixer + second_ffn
    sources = {n: weights[n].astype(BF16) for n in _BIG}
    sources["conv"] = _pad_rows(_lane_rows(dn_conv_w), 8)
    gathers, landed = {}, {}

    def start_gather(key, names, l, after):
        layer_of = [None if n == "conv" else l for n in names]
        first, thru, landing, token = _relay_gather_start([sources[n] for n in names], layer_of, after,
                                                          name=f"gather_start_{key}")
        sources.update(zip(names, thru))
        gathers[key] = dict(first=first, landing=landing, names=names, layers=layer_of)
        return token

    def pass_on(key, after):
        g = gathers[key]
        g["relay"], thru, g["landing"], token = _relay_gather_pass_on(
            g["first"], [sources[n] for n in g["names"]], g["layers"], g["landing"], [after], name=f"gather_pass_on_{key}")
        sources.update(zip(g["names"], thru))
        return token

    def wait_gather(key, after):
        g = gathers.pop(key)
        thru, arrays, token = _relay_gather_wait(g["first"], g["relay"], [sources[n] for n in g["names"]], g["layers"],
                                                 g["landing"], [after], name=f"gather_wait_{key}")
        sources.update(zip(g["names"], thru))
        landed.update(zip(g["names"], arrays))
        return token

    def weights_of(l, part, x_in):
        if l == 0 and part == 0:
            start_gather("l0_ffn1", first_ffn, 0, [])
            pass_on("l0_ffn1", x_in)
            issued = start_gather("l0_mix", mixer + ("conv",), 0, [wait_gather("l0_ffn1", x_in)])
            return dict({n: landed.pop(n) for n in first_ffn}, issued=issued[0, 0])
        if part == 0:
            token = wait_gather(f"l{l}", x_in)
            issued = start_gather(f"l{l + 1}", _BIG, l + 1, [token]) if l + 1 < L else token
            return dict({n: landed.pop(n) for n in first_ffn}, issued=issued[0, 0])
        if part == 2:
            issued = 0.0
            if l == 0:
                pass_on("l0_ffn2", x_in)
                issued = wait_gather("l0_ffn2", x_in)[0, 0]
            if l + 1 < L:
                issued = pass_on(f"l{l + 1}", x_in)[0, 0]
            return dict({n: landed.pop(n) for n in second_ffn if n in landed}, issued=issued)
        issued = 0.0
        if l == 0:
            pass_on("l0_mix", x_in)
            token = start_gather("l0_ffn2", second_ffn, 0, [wait_gather("l0_mix", x_in)])
            issued = (start_gather("l1", _BIG, 1, [token]) if L > 1 else token)[0, 0]
            conv = landed.pop("conv").reshape(N_DEV, -1)[:, :L * DN_CONV * conv_cols]
            landed["conv_w"] = conv.reshape(N_DEV, L, DN_CONV, conv_cols).transpose(1, 2, 0, 3).reshape(
                L, DN_CONV, N_DEV * conv_cols)
        big = {n: landed.pop(n) for n in later if n in landed}
        wi = big.pop("w_in")
        pieces = [wi[d] for d in range(N_DEV)]
        pieces[SCAL_SLOT:SCAL_SLOT + 1] = [wi[SCAL_SLOT][:, :SCAL_AT], wi[SCAL_SLOT][:, SCAL_AT + 2 * HEADS:]]
        big["w_main"] = jnp.concatenate(pieces, axis=1)
        big["w_scal"] = jnp.pad(wi[SCAL_SLOT][:, SCAL_AT:SCAL_AT + 2 * HEADS], ((0, 0), (0, N_SCAL - 2 * HEADS)))
        big["conv_w"] = landed["conv_w"][l]
        return dict(big, issued=issued)

    layers = []
    for l in range(L):
        hp = jnp.concatenate([jnp.broadcast_to(dn_a_log[l][:, None, None], (HEADS, 1, 128)),
                              jnp.broadcast_to(dn_dt_bias[l][:, None, None], (HEADS, 1, 128)),
                              jnp.zeros((HEADS, 6, 128), F32)], axis=1)
        layers.append(dict(ffn1_norm=ffn1_norm[l][None], mix_norm=mix_norm[l][None], hp=hp,
                           dn_out_norm=dn_out_norm[l][None], sb_q_norm=sb_q_norm[l][None],
                           sb_k_norm=sb_k_norm[l][None], ffn2_norm=ffn2_norm[l][None]))

    in_flight = {}

    def on_weight_grads(l, stage, g):
        parts = dict(g)
        if stage == 1:
            gm, shard = g["w_main"], N_IN // N_DEV
            blocks = [gm[:, d * shard:(d + 1) * shard] for d in range(SCAL_SLOT)]
            blocks.append(jnp.concatenate([gm[:, SCAL_SLOT * shard:COL_SCAL], g["w_scal"][:, :2 * HEADS],
                                           gm[:, COL_SCAL:(SCAL_SLOT + 1) * shard - 2 * HEADS]], axis=1))
            blocks += [gm[:, d * shard - 2 * HEADS:(d + 1) * shard - 2 * HEADS] for d in range(SCAL_SLOT + 1, N_DEV)]
            parts["w_in"] = jnp.stack(blocks)
            for n in ("w_branch_a", "w_branch_b", "w_out"):
                parts[n] = g[n].reshape(N_DEV, D_MODEL // N_DEV, D_MODEL)
        *in_flight[l, stage], token = _exchange_start([parts[n] for n in _STAGES[stage]], gather=False,
                                                      name=f"scatter_start_l{l}_{stage}")
        return token

    loss_row, dx, grads = _local_step(x[0], loss_target[0], layers, weights_of, on_weight_grads)
    loss = lax.psum(loss_row[0, 0], ("x", "y", "c"))

    results = {n: None for n in _BIG}
    after = [dx]
    for l in reversed(range(L)):
        for stage, names in enumerate(_STAGES):
            landed, all_landed = _exchange_wait(*in_flight[l, stage], after, gather=False,
                                                name=f"scatter_wait_l{l}_{stage}")
            for n, parts in zip(names, landed):
                _, a, b = weights[n].shape
                results[n] = _adamw(parts, weights[n].reshape(L * a, b), mom_m[n].reshape(L * a, b),
                                    mom_v[n].reshape(L * a, b), layer=l, earlier=results[n], name=f"adamw_{n}_l{l}")
            after = [results[n][0] for n in names]
    out = {n: tuple(t.reshape(weights[n].shape) for t in results[n]) for n in _BIG}

    small_grads = [(n, jnp.stack([g[n].reshape(weights[n].shape[1:]) for g in grads])) for n in _SMALL]
    small_packed, spans = _pack_small(small_grads + [("conv", jnp.stack([g["conv_w"] for g in grads]))])
    small_packed = small_packed + all_landed[0, 0]
    small_sum = _sum_parts(_all_gather([small_packed], name="gather_small_grads")[0], name="sum_small_grads")
    rep_rows = spans["conv"][0]
    pack_rep = lambda d: _pad_rows(_pack_small([(n, d[n]) for n in _SMALL])[0], 8)
    rep_pad = (-rep_rows) % 8
    g_rep = jnp.pad(small_sum[:rep_rows], ((0, rep_pad), (0, 0)))
    res = _adamw(g_rep[None], pack_rep(weights), pack_rep(mom_m), pack_rep(mom_v), name="adamw_replicated")
    for n in _SMALL:
        out[n] = tuple(_unpack_small(t, spans, n) for t in res)
    conv_sum = _unpack_small(small_sum, spans, "conv")
    conv_mine = lax.dynamic_slice_in_dim(conv_sum, my_slot * conv_cols, conv_cols, axis=2).reshape(L * DN_CONV, conv_cols)
    flat = lambda t: t.reshape(L * DN_CONV, conv_cols)
    res = _adamw(conv_mine[None], flat(dn_conv_w), flat(m_dn_conv_w), flat(v_dn_conv_w), name="adamw_conv")
    out["dn_conv_w"] = tuple(t.reshape(L, DN_CONV, conv_cols) for t in res)

    return (loss, dx[None], *[out[n][0] for n in _ORDER], *[out[n][1] for n in _ORDER],
            *[out[n][2] for n in _ORDER], *[out[n][3] for n in _ORDER])
```

```python
import functools
import math

import jax
import jax.numpy as jnp
from jax import lax
from jax.experimental import pallas as pl
from jax.experimental.pallas import tpu as pltpu

F32 = jnp.float32
BF16 = jnp.bfloat16

N_DEV = 8
D_MODEL = 1024
DEPTH = 4
D_FF = 2816
HEADS = 8
HEAD_DIM = 128
DN_CHUNK = 64
DN_CONV = 4
DN_GROUP = 8
DN_HEADS = 2
SB_BLOCK = 128
SB_KEY_TILE = 512
SB_HEADS = 4
SB_HEADS_BWD = 4
SB_KEY_TILE_BWD = 512
RMS_EPS = 1e-6
L2_EPS = 1e-6
N_IN = 9232
N_MAIN = 9216
N_SCAL = 128
QK_SCALE = HEAD_DIM ** -0.5

ADAM_LR = 0.001
ADAM_B1 = 0.9
ADAM_B2 = 0.999
ADAM_EPS = 1e-08
ADAM_WD = 0.01
ADAM_STEP = 10

V7X_VMEM_LIMIT = 56 * 1024 * 1024
V7X_VMEM_LIMIT_HIGH = 60 * 1024 * 1024
MESH = pl.DeviceIdType.MESH
ANY = pl.BlockSpec(memory_space=pl.ANY)


def _params(sem=None, vmem=V7X_VMEM_LIMIT):
    return pltpu.CompilerParams(dimension_semantics=sem, vmem_limit_bytes=vmem)


def _sigmoid(x):
    return 1.0 / (1.0 + jnp.exp(-x))


SOFTPLUS_LINEAR = 30.0


def _softplus(x):
    return jnp.maximum(x, jnp.log(1.0 + jnp.exp(jnp.minimum(x, SOFTPLUS_LINEAR))))


def _bdot(a, b, dims=(((1,), (0,)), ((), ()))):
    return lax.dot_general(a.astype(BF16), b.astype(BF16), dims, preferred_element_type=F32)


_NT = (((1,), (1,)), ((), ()))
_TN = (((0,), (0,)), ((), ()))


def _hdot(a, b, dims=(((1,), (0,)), ((), ()))):
    a_hi = a.astype(BF16)
    b_hi = b.astype(BF16)
    a_lo = (a - a_hi.astype(F32)).astype(BF16)
    b_lo = (b - b_hi.astype(F32)).astype(BF16)
    dot = functools.partial(lax.dot_general, dimension_numbers=dims, preferred_element_type=F32)
    return dot(a_hi, b_hi) + (dot(a_hi, b_lo) + dot(a_lo, b_hi))


def _hdot_tn(a, b):
    return _hdot(a, b, _TN)


def _mm(*, name, grid, a, a_spec, b, b_spec, out_shape, o_spec, tile, ta=False, tb=False, resid=None, scale=1.0):
    nk = grid[2]
    dims = (((0 if ta else 1,), (1 if tb else 0,)), ((), ()))

    def flat(v):
        return v if v.ndim == 2 else v.reshape(-1, v.shape[-1])

    def body(*refs):
        a_ref, b_ref = refs[:2]
        r_ref = refs[2] if resid is not None else None
        o_ref = refs[3] if resid is not None else refs[2]
        part = lax.dot_general(flat(a_ref[...]).astype(BF16), flat(b_ref[...]).astype(BF16), dims,
                               preferred_element_type=F32)

        def finish(acc):
            if scale != 1.0:
                acc = acc * scale
            if r_ref is not None:
                acc = r_ref[...] + acc
            o_ref[...] = acc.astype(o_ref.dtype)

        if nk == 1:
            finish(part)
        else:
            acc_ref = refs[-1]
            k = pl.program_id(2)

            @pl.when(k == 0)
            def _():
                acc_ref[...] = part

            @pl.when(k > 0)
            def _():
                acc_ref[...] += part

            @pl.when(k == nk - 1)
            def _():
                finish(acc_ref[...])

    in_specs = [a_spec, b_spec] + ([pl.BlockSpec(tile, lambda i, j, k: (i, j))] if resid is not None else [])
    args = (a, b) + ((resid,) if resid is not None else ())
    return pl.pallas_call(
        body, name=name, grid=grid, in_specs=in_specs, out_specs=o_spec, out_shape=out_shape,
        scratch_shapes=[pltpu.VMEM(tile, F32)] if nk > 1 else [],
        compiler_params=_params(("parallel", "parallel", "arbitrary")),
    )(*args)


def _matmul(a, b, *, name, ta=False, tb=False, out_dtype=F32, tm=None, tn=None, tk=None, resid=None, scale=1.0):
    if ta:
        K, M = a.shape
    else:
        M, K = a.shape
    N = b.shape[0] if tb else b.shape[1]
    tm = tm or min(M, 1024)
    tn = tn or min(N, 512)
    tk = tk or K
    assert M % tm == 0 and N % tn == 0 and K % tk == 0, (name, M, N, K, tm, tn, tk)
    a_spec = pl.BlockSpec((tk, tm), lambda i, j, k: (k, i)) if ta else pl.BlockSpec((tm, tk), lambda i, j, k: (i, k))
    b_spec = pl.BlockSpec((tn, tk), lambda i, j, k: (j, k)) if tb else pl.BlockSpec((tk, tn), lambda i, j, k: (k, j))
    return _mm(name=name, grid=(M // tm, N // tn, K // tk), a=a, a_spec=a_spec, b=b, b_spec=b_spec,
               out_shape=jax.ShapeDtypeStruct((M, N), out_dtype), o_spec=pl.BlockSpec((tm, tn), lambda i, j, k: (i, j)),
               tile=(tm, tn), ta=ta, tb=tb, resid=resid, scale=scale)


ROW_TILE = 256


def _rmsnorm_fwd(x, gain, *, name):
    T, D = x.shape

    def body(x_ref, g_ref, o_ref):
        xf = x_ref[...]
        r = lax.rsqrt(jnp.mean(xf * xf, axis=-1, keepdims=True) + RMS_EPS)
        o_ref[...] = (xf * r * g_ref[...]).astype(o_ref.dtype)

    return pl.pallas_call(
        body, name=name, grid=(T // ROW_TILE,),
        in_specs=[pl.BlockSpec((ROW_TILE, D), lambda i: (i, 0)), pl.BlockSpec((1, D), lambda i: (0, 0))],
        out_specs=pl.BlockSpec((ROW_TILE, D), lambda i: (i, 0)),
        out_shape=jax.ShapeDtypeStruct((T, D), BF16), compiler_params=_params(("parallel",)),
    )(x, gain)


def _rmsnorm_bwd(dh, x, gain, dres, *, name):
    T, D = x.shape

    def body(dh_ref, x_ref, g_ref, res_ref, dx_ref, dg_ref):
        xf = x_ref[...]
        r = lax.rsqrt(jnp.mean(xf * xf, axis=-1, keepdims=True) + RMS_EPS)
        y = xf * r
        dh_v = dh_ref[...].astype(F32)
        dy = dh_v * g_ref[...]
        dx_ref[...] = res_ref[...] + r * (dy - y * jnp.mean(dy * y, axis=-1, keepdims=True))

        @pl.when(pl.program_id(0) == 0)
        def _():
            dg_ref[...] = jnp.zeros_like(dg_ref)

        dg_ref[...] += jnp.sum(dh_v * y, axis=0, keepdims=True)

    row = pl.BlockSpec((ROW_TILE, D), lambda i: (i, 0))
    vec = pl.BlockSpec((1, D), lambda i: (0, 0))
    return pl.pallas_call(
        body, name=name, grid=(T // ROW_TILE,), in_specs=[row, row, vec, row], out_specs=(row, vec),
        out_shape=(jax.ShapeDtypeStruct((T, D), F32), jax.ShapeDtypeStruct((1, D), F32)),
        compiler_params=_params(("arbitrary",)),
    )(dh, x, gain, dres)


FF_HALF = N_DEV // 2


def _swiglu_fwd(p, *, name):
    _, T, fb = p.shape

    def body(g_ref, u_ref, o_ref):
        g = g_ref[...].astype(F32)
        o_ref[...] = (g * _sigmoid(g) * u_ref[...].astype(F32)).astype(o_ref.dtype)

    blk = (None, ROW_TILE, fb)
    return pl.pallas_call(
        body, name=name, grid=(T // ROW_TILE, FF_HALF),
        in_specs=[pl.BlockSpec(blk, lambda i, j: (j, i, 0)), pl.BlockSpec(blk, lambda i, j: (j + FF_HALF, i, 0))],
        out_specs=pl.BlockSpec(blk, lambda i, j: (j, i, 0)),
        out_shape=jax.ShapeDtypeStruct((FF_HALF, T, fb), BF16), compiler_params=_params(("parallel", "parallel")),
    )(p, p)


def _swiglu_bwd(da, p, *, name):
    _, T, fb = p.shape

    def body(da_ref, g_ref, u_ref, o_ref):
        g = g_ref[...].astype(F32)
        u = u_ref[...].astype(F32)
        d = da_ref[...].astype(F32)
        s = _sigmoid(g)
        o_ref[0] = (d * u * (s * (1.0 + g * (1.0 - s)))).astype(o_ref.dtype)
        o_ref[1] = (d * g * s).astype(o_ref.dtype)

    blk = (None, ROW_TILE, fb)
    out = pl.pallas_call(
        body, name=name, grid=(T // ROW_TILE, FF_HALF),
        in_specs=[pl.BlockSpec(blk, lambda i, j: (j, i, 0)), pl.BlockSpec(blk, lambda i, j: (j, i, 0)),
                  pl.BlockSpec(blk, lambda i, j: (j + FF_HALF, i, 0))],
        out_specs=pl.BlockSpec((2, None, ROW_TILE, fb), lambda i, j: (0, j, i, 0)),
        out_shape=jax.ShapeDtypeStruct((2, FF_HALF, T, fb), BF16), compiler_params=_params(("parallel", "parallel")),
    )(da, p, p)
    return out.reshape(2 * FF_HALF, T, fb)


COL_GATE_A = 7
COL_GATE_B = 8


def _merge_fwd(ya, yb, proj, *, name):
    T, D = ya.shape

    def body(ya_ref, yb_ref, ga_ref, gb_ref, o_ref):
        o_ref[...] = (_sigmoid(ga_ref[...]) * ya_ref[...] + _sigmoid(gb_ref[...]) * yb_ref[...]).astype(o_ref.dtype)

    row = pl.BlockSpec((ROW_TILE, D), lambda i: (i, 0))
    return pl.pallas_call(
        body, name=name, grid=(T // ROW_TILE,),
        in_specs=[row, row, pl.BlockSpec((ROW_TILE, D), lambda i: (i, COL_GATE_A)),
                  pl.BlockSpec((ROW_TILE, D), lambda i: (i, COL_GATE_B))],
        out_specs=row, out_shape=jax.ShapeDtypeStruct((T, D), BF16), compiler_params=_params(("parallel",)),
    )(ya, yb, proj, proj)


def _merge_bwd(dm, ya, yb, proj, *, name):
    T, D = ya.shape

    def body(dm_ref, ya_ref, yb_ref, ga_ref, gb_ref, dya_ref, dyb_ref, dga_ref, dgb_ref):
        d = dm_ref[...].astype(F32)
        sa = _sigmoid(ga_ref[...])
        sb = _sigmoid(gb_ref[...])
        dya_ref[...] = (d * sa).astype(BF16)
        dyb_ref[...] = (d * sb).astype(BF16)
        dga_ref[...] = (d * ya_ref[...] * sa * (1.0 - sa)).astype(BF16)
        dgb_ref[...] = (d * yb_ref[...] * sb * (1.0 - sb)).astype(BF16)

    row = pl.BlockSpec((ROW_TILE, D), lambda i: (i, 0))
    out = jax.ShapeDtypeStruct((T, D), BF16)
    return pl.pallas_call(
        body, name=name, grid=(T // ROW_TILE,),
        in_specs=[row, row, row, pl.BlockSpec((ROW_TILE, D), lambda i: (i, COL_GATE_A)),
                  pl.BlockSpec((ROW_TILE, D), lambda i: (i, COL_GATE_B))],
        out_specs=(row, row, row, row), out_shape=(out, out, out, out), compiler_params=_params(("parallel",)),
    )(dm, ya, yb, proj, proj)


def _loss_head(y, target, *, name):
    T, D = y.shape

    def body(y_ref, t_ref, loss_ref, dy_ref):
        err = y_ref[...] - t_ref[...]
        dy_ref[...] = err * (1.0 / D)

        @pl.when(pl.program_id(0) == 0)
        def _():
            loss_ref[...] = jnp.zeros_like(loss_ref)

        loss_ref[...] += 0.5 * jnp.sum(jnp.sum(err * err, axis=-1, keepdims=True) * (1.0 / D), axis=0, keepdims=True)

    row = pl.BlockSpec((ROW_TILE, D), lambda i: (i, 0))
    return pl.pallas_call(
        body, name=name, grid=(T // ROW_TILE,), in_specs=[row, row],
        out_specs=(pl.BlockSpec((1, 128), lambda i: (0, 0)), row),
        out_shape=(jax.ShapeDtypeStruct((1, 128), F32), jax.ShapeDtypeStruct((T, D), F32)),
        compiler_params=_params(("arbitrary",)),
    )(y, target)


CONV_PAD = 8


def _conv_taps(w, xp, T, first):
    acc = w[0:1, :] * xp[pl.ds(first, T), :]
    for i in range(1, DN_CONV):
        acc = acc + w[i:i + 1, :] * xp[pl.ds(first + i, T), :]
    return acc


def _conv_fwd(proj, conv_w, *, name):
    T = proj.shape[0]

    def body(x_ref, w_ref, o_ref, xp):
        xp[0:CONV_PAD, :] = jnp.zeros((CONV_PAD, HEAD_DIM), F32)
        xp[CONV_PAD:, :] = x_ref[...]
        y = _conv_taps(w_ref[...], xp, T, CONV_PAD - (DN_CONV - 1))
        s = y * _sigmoid(y)
        n = s * lax.rsqrt(jnp.sum(s * s, axis=-1, keepdims=True) + L2_EPS)
        o_ref[0] = jnp.where(pl.program_id(0) < 2, n, s)

    return pl.pallas_call(
        body, name=name, grid=(3, HEADS),
        in_specs=[pl.BlockSpec((T, HEAD_DIM), lambda c, h: (0, c * HEADS + h)),
                  pl.BlockSpec((DN_CONV, HEAD_DIM), lambda c, h: (0, c * HEADS + h))],
        out_specs=pl.BlockSpec((1, T, HEAD_DIM), lambda c, h: (c, 0, h)),
        out_shape=jax.ShapeDtypeStruct((3, T, D_MODEL), F32),
        scratch_shapes=[pltpu.VMEM((T + CONV_PAD, HEAD_DIM), F32)],
        compiler_params=_params(("parallel", "parallel")),
    )(proj, conv_w)


def _conv_bwd(dqkv, proj, conv_w, *, name):
    T = proj.shape[0]

    def body(d_ref, x_ref, w_ref, dx_ref, dw_ref, xp, dyp):
        xp[0:CONV_PAD, :] = jnp.zeros((CONV_PAD, HEAD_DIM), F32)
        xp[CONV_PAD:, :] = x_ref[...]
        w = w_ref[...]
        y = _conv_taps(w, xp, T, CONV_PAD - (DN_CONV - 1))
        sg = _sigmoid(y)
        s = y * sg
        r = lax.rsqrt(jnp.sum(s * s, axis=-1, keepdims=True) + L2_EPS)
        n = s * r
        d = d_ref[0]
        ds = jnp.where(pl.program_id(0) < 2, r * (d - n * jnp.sum(d * n, axis=-1, keepdims=True)), d)
        dy = ds * (sg * (1.0 + y * (1.0 - sg)))
        dyp[0:T, :] = dy
        dyp[T:, :] = jnp.zeros((CONV_PAD, HEAD_DIM), F32)
        dx = w[0:1, :] * dyp[pl.ds(DN_CONV - 1, T), :]
        for i in range(1, DN_CONV):
            dx = dx + w[i:i + 1, :] * dyp[pl.ds(DN_CONV - 1 - i, T), :]
        dx_ref[...] = dx.astype(dx_ref.dtype)
        for i in range(DN_CONV):
            dw_ref[i:i + 1, :] = jnp.sum(dy * xp[pl.ds(CONV_PAD - (DN_CONV - 1) + i, T), :], axis=0, keepdims=True)

    col = lambda c, h: (0, c * HEADS + h)
    return pl.pallas_call(
        body, name=name, grid=(3, HEADS),
        in_specs=[pl.BlockSpec((1, T, HEAD_DIM), lambda c, h: (c, 0, h)), pl.BlockSpec((T, HEAD_DIM), col),
                  pl.BlockSpec((DN_CONV, HEAD_DIM), col)],
        out_specs=(pl.BlockSpec((T, HEAD_DIM), col), pl.BlockSpec((DN_CONV, HEAD_DIM), col)),
        out_shape=(jax.ShapeDtypeStruct((T, 3 * D_MODEL), BF16), jax.ShapeDtypeStruct((DN_CONV, 3 * D_MODEL), F32)),
        scratch_shapes=[pltpu.VMEM((T + CONV_PAD, HEAD_DIM), F32), pltpu.VMEM((T + CONV_PAD, HEAD_DIM), F32)],
        compiler_params=_params(("parallel", "parallel")),
    )(dqkv, proj, conv_w)


def _inv_unit_lower(low, eye):
    x = eye - low
    power = _hdot(low, low, _B_NN)
    steps = int(math.log2(DN_CHUNK)) - 1
    for s in range(steps):
        x = x + _hdot(x, power, _B_NN)
        if s + 1 < steps:
            power = _hdot(power, power, _B_NN)
    return x


_B_NN = (((2,), (1,)), ((0,), (0,)))
_B_NT = (((2,), (2,)), ((0,), (0,)))
_B_TN = (((1,), (1,)), ((0,), (0,)))


def _dn_load(ref, lead, r0, group):
    rows = pl.ds(r0, group * DN_CHUNK)
    cols = lambda h: slice(h * HEAD_DIM, (h + 1) * HEAD_DIM)
    per_head = [(ref[rows, cols(h)] if lead is None else ref[lead, rows, cols(h)]).reshape(group, DN_CHUNK, HEAD_DIM)
                for h in range(DN_HEADS)]
    return jnp.stack(per_head, axis=1).reshape(group * DN_HEADS, DN_CHUNK, HEAD_DIM)


def _dn_chunk_setup(qkv_ref, b_ref, a_ref, hp_ref, n0, group, tinv=None):
    C = DN_CHUNK
    B = group * DN_HEADS
    r0 = pl.multiple_of(n0 * C, C)
    q = _dn_load(qkv_ref, 0, r0, group) * QK_SCALE
    k = _dn_load(qkv_ref, 1, r0, group)
    v = _dn_load(qkv_ref, 2, r0, group)
    ii = lax.broadcasted_iota(jnp.int32, (B, C, C), 1)
    jj = lax.broadcasted_iota(jnp.int32, (B, C, C), 2)
    eye_mask = ii == jj
    eye = jnp.where(eye_mask, 1.0, 0.0).astype(F32)

    def to_col(row):
        return jnp.sum(jnp.where(eye_mask, jnp.broadcast_to(row, (B, C, C)), 0.0), axis=2, keepdims=True)

    def to_row(col):
        return jnp.sum(jnp.where(eye_mask, jnp.broadcast_to(col, (B, C, C)), 0.0), axis=1, keepdims=True)

    def rows(ref):
        return jnp.stack([ref[h, pl.ds(n0, group)] for h in range(DN_HEADS)], axis=1).reshape(B, 1, C)

    def per_head(row):
        return jnp.stack([hp_ref[h, row:row + 1, 0:C] for h in range(DN_HEADS)] * group, axis=0)

    b_row = rows(b_ref)
    a_row = rows(a_ref)
    a_log = per_head(0)
    dt_b = per_head(1)
    beta_row = _sigmoid(b_row)
    neg_ea = -jnp.exp(a_log)
    g_row = neg_ea * _softplus(a_row + dt_b)
    gc_col = jnp.sum(jnp.where(jj <= ii, jnp.broadcast_to(g_row, (B, C, C)), 0.0), axis=2, keepdims=True)
    gc_row = to_row(gc_col)
    g_last = jnp.sum(g_row, axis=2, keepdims=True)
    beta = to_col(beta_row)
    low_incl = ii >= jj
    decay = jnp.exp(jnp.where(low_incl, gc_col - gc_row, -jnp.inf))
    eg = jnp.exp(gc_col)
    egl = jnp.exp(g_last - gc_col)
    el = jnp.exp(g_last)
    kb = k * beta
    pmat = _bdot(kb, k, _B_NT)
    low = jnp.where(ii > jj, pmat * decay, 0.0)
    if tinv is None:
        tinv = _inv_unit_lower(low, eye)
    u = _hdot(tinv, v * beta, _B_NN)
    w = _hdot(tinv, kb * eg, _B_NN)
    qk = _bdot(q, k, _B_NT)
    attn = qk * decay
    return dict(q=q, k=k, v=v, ii=ii, jj=jj, to_col=to_col, to_row=to_row, b_row=b_row, a_row=a_row, dt_b=dt_b,
                beta_row=beta_row, neg_ea=neg_ea, g_row=g_row, gc_col=gc_col, g_last=g_last, beta=beta,
                decay=decay, eg=eg, egl=egl, el=el, kb=kb, pmat=pmat, tinv=tinv, u=u, w=w, qk=qk, attn=attn,
                qd=q * eg, kd=k * egl, r0=r0)


def _dn_store(ref, lead, r0, group, value):
    value = value.reshape(group, DN_HEADS, DN_CHUNK, HEAD_DIM)
    for h in range(DN_HEADS):
        block = value[:, h].reshape(group * DN_CHUNK, HEAD_DIM)
        if lead is None:
            ref[pl.ds(r0, group * DN_CHUNK), h * HEAD_DIM:(h + 1) * HEAD_DIM] = block
        else:
            ref[lead, pl.ds(r0, group * DN_CHUNK), h * HEAD_DIM:(h + 1) * HEAD_DIM] = block


def _dn_specs(T):
    nc = T // DN_CHUNK
    qkv = pl.BlockSpec((3, T, DN_HEADS * HEAD_DIM), lambda h: (0, 0, h))
    rows = pl.BlockSpec((DN_HEADS, nc, 1, DN_CHUNK), lambda h: (h, 0, 0, 0))
    hp = pl.BlockSpec((DN_HEADS, 8, 128), lambda h: (h, 0, 0))
    states = pl.BlockSpec((DN_HEADS, nc, HEAD_DIM, HEAD_DIM), lambda h: (h, 0, 0, 0))
    return nc, qkv, rows, hp, states


def _dn_inverse_spec(T):
    return pl.BlockSpec((DN_HEADS, T // DN_CHUNK, DN_CHUNK, DN_CHUNK), lambda h: (h, 0, 0, 0))


def _dn_per_head(ref, n0, group):
    stacked = jnp.stack([ref[h, pl.ds(n0, group)] for h in range(DN_HEADS)], axis=1)
    return stacked.reshape((group * DN_HEADS,) + stacked.shape[2:])


def _dn_fwd(qkv, b_rows, a_rows, hp, *, name):
    T = qkv.shape[1]
    nc, qkv_spec, row_spec, hp_spec, st_spec = _dn_specs(T)
    group = math.gcd(nc, DN_GROUP)
    H = DN_HEADS

    def body(qkv_ref, b_ref, a_ref, hp_ref, o_ref, st_ref, inv_ref, s_scr):
        s_scr[...] = jnp.zeros_like(s_scr)

        def step(t, carry):
            n0 = t * group
            c = _dn_chunk_setup(qkv_ref, b_ref, a_ref, hp_ref, n0, group)
            tinv = c["tinv"].reshape(group, H, DN_CHUNK, DN_CHUNK)
            for h in range(H):
                inv_ref[h, pl.ds(n0, group)] = tinv[:, h]
            state = s_scr[...]
            outs = []
            for g in range(group):
                sl = slice(g * H, (g + 1) * H)
                for h in range(H):
                    st_ref[h, n0 + g] = state[h]
                v_new = c["u"][sl] - _bdot(c["w"][sl], state, _B_NN)
                outs.append(_bdot(c["qd"][sl], state, _B_NN) + _bdot(c["attn"][sl], v_new, _B_NN))
                state = state * c["el"][sl] + _bdot(c["kd"][sl], v_new, _B_TN)
            s_scr[...] = state
            _dn_store(o_ref, None, c["r0"], group, jnp.concatenate(outs, axis=0))
            return carry

        lax.fori_loop(0, nc // group, step, 0)

    return pl.pallas_call(
        body, name=name, grid=(HEADS // H,), in_specs=[qkv_spec, row_spec, row_spec, hp_spec],
        out_specs=(pl.BlockSpec((T, H * HEAD_DIM), lambda h: (0, h)), st_spec, _dn_inverse_spec(T)),
        out_shape=(jax.ShapeDtypeStruct((T, D_MODEL), F32),
                   jax.ShapeDtypeStruct((HEADS, nc, HEAD_DIM, HEAD_DIM), F32),
                   jax.ShapeDtypeStruct((HEADS, nc, DN_CHUNK, DN_CHUNK), F32)),
        scratch_shapes=[pltpu.VMEM((H, HEAD_DIM, HEAD_DIM), F32)], compiler_params=_params(("parallel",)),
    )(qkv, b_rows, a_rows, hp)


def _dn_bwd(qkv, b_rows, a_rows, hp, states, inverses, do, *, name):
    T = qkv.shape[1]
    C = DN_CHUNK
    nc, qkv_spec, row_spec, hp_spec, st_spec = _dn_specs(T)
    group = math.gcd(nc, DN_GROUP)
    H = DN_HEADS
    B = group * H

    def body(qkv_ref, b_ref, a_ref, hp_ref, st_ref, inv_ref, do_ref, dqkv_ref, db_ref, da_ref, dhp_ref, ds_scr, acc_scr):
        ds_scr[...] = jnp.zeros_like(ds_scr)
        acc_scr[...] = jnp.zeros_like(acc_scr)

        def step(t, carry):
            n0 = nc - (t + 1) * group
            c = _dn_chunk_setup(qkv_ref, b_ref, a_ref, hp_ref, n0, group, tinv=_dn_per_head(inv_ref, n0, group))
            state = _dn_per_head(st_ref, n0, group)
            d_o = _dn_load(do_ref, None, c["r0"], group)
            v_new = c["u"] - _bdot(c["w"], state, _B_NN)
            d_vnew_local = _bdot(c["attn"], d_o, _B_TN)
            d_state_local = _bdot(c["qd"], d_o, _B_TN)
            d_state = ds_scr[...]
            d_vnew, d_kd, d_el = [None] * group, [None] * group, [None] * group
            for g in reversed(range(group)):
                sl = slice(g * H, (g + 1) * H)
                d_vnew[g] = d_vnew_local[sl] + _bdot(c["kd"][sl], d_state, _B_NN)
                d_kd[g] = _bdot(v_new[sl], d_state, _B_NT)
                d_el[g] = jnp.sum(jnp.sum(d_state * state[sl], axis=2, keepdims=True), axis=1, keepdims=True)
                d_state = d_state * c["el"][sl] + d_state_local[sl] - _bdot(c["w"][sl], d_vnew[g], _B_TN)
            ds_scr[...] = d_state
            chunk_grads(c, n0, state, d_o, v_new, jnp.concatenate(d_vnew, axis=0), jnp.concatenate(d_kd, axis=0),
                        jnp.concatenate(d_el, axis=0))
            return carry

        def chunk_grads(c, n0, state, d_o, v_new, d_vnew, d_kd, d_el):
            ii, jj = c["ii"], c["jj"]
            q, k, v, kb, beta = c["q"], c["k"], c["v"], c["kb"], c["beta"]
            decay, eg, egl, el = c["decay"], c["eg"], c["egl"], c["el"]
            u, w, tinv = c["u"], c["w"], c["tinv"]
            d_qd = _bdot(d_o, state, _B_NT)
            d_attn = _bdot(d_o, v_new, _B_NT)
            d_w = -_bdot(d_vnew, state, _B_NT)
            d_rv = _hdot(tinv, d_vnew, _B_TN)
            d_rw = _hdot(tinv, d_w, _B_TN)
            d_amat = -(_bdot(d_rv, u, _B_NT) + _bdot(d_rw, w, _B_NT))
            d_low = jnp.where(ii > jj, d_amat, 0.0)
            d_p = d_low * decay
            d_qk = d_attn * decay
            e_mat = (d_low * c["pmat"] + d_attn * c["qk"]) * decay
            d_q = _bdot(d_qk, k, _B_NN) + d_qd * eg
            d_kb = _bdot(d_p, k, _B_NN) + d_rw * eg
            d_k = _bdot(d_qk, q, _B_TN) + _bdot(d_p, kb, _B_TN) + d_kd * egl + d_kb * beta
            d_beta = jnp.sum(d_kb * k, axis=2, keepdims=True) + jnp.sum(d_rv * v, axis=2, keepdims=True)
            d_v = d_rv * beta
            d_eg = jnp.sum(d_qd * q, axis=2, keepdims=True) + jnp.sum(d_rw * kb, axis=2, keepdims=True)
            d_egl = jnp.sum(d_kd * k, axis=2, keepdims=True)
            d_glast = jnp.sum(d_egl * egl, axis=1, keepdims=True) + d_el * el
            row_sum = jnp.sum(e_mat, axis=2, keepdims=True)
            col_sum = c["to_col"](jnp.sum(e_mat, axis=1, keepdims=True))
            d_gc = row_sum - col_sum + d_eg * eg - d_egl * egl
            d_g_row = jnp.sum(jnp.where(ii >= jj, jnp.broadcast_to(d_gc, (B, C, C)), 0.0), axis=1, keepdims=True) + d_glast
            beta_row = c["beta_row"]
            d_b_row = c["to_row"](d_beta) * beta_row * (1.0 - beta_row)
            d_a_row = d_g_row * c["neg_ea"] * _sigmoid(c["a_row"] + c["dt_b"])
            _dn_store(dqkv_ref, 0, c["r0"], group, d_q * QK_SCALE)
            _dn_store(dqkv_ref, 1, c["r0"], group, d_k)
            _dn_store(dqkv_ref, 2, c["r0"], group, d_v)
            d_b_row = d_b_row.reshape(group, H, 1, C)
            d_a_row = d_a_row.reshape(group, H, 1, C)
            d_a_log = jnp.sum((d_g_row * c["g_row"]).reshape(group, H, 1, C), axis=0)
            d_dt_b = jnp.sum(d_a_row, axis=0)
            for h in range(H):
                db_ref[h, pl.ds(n0, group)] = d_b_row[:, h]
                da_ref[h, pl.ds(n0, group)] = d_a_row[:, h]
                acc_scr[h, 0:1, 0:C] += d_a_log[h]
                acc_scr[h, 1:2, 0:C] += d_dt_b[h]

        lax.fori_loop(0, nc // group, step, 0)
        for h in range(H):
            tot = jnp.sum(acc_scr[h], axis=1, keepdims=True)
            dhp_ref[h] = jnp.broadcast_to(tot, (8, 128))

    return pl.pallas_call(
        body, name=name, grid=(HEADS // H,),
        in_specs=[qkv_spec, row_spec, row_spec, hp_spec, st_spec, _dn_inverse_spec(T),
                  pl.BlockSpec((T, H * HEAD_DIM), lambda h: (0, h))],
        out_specs=(qkv_spec, row_spec, row_spec, hp_spec),
        out_shape=(jax.ShapeDtypeStruct((3, T, D_MODEL), F32), jax.ShapeDtypeStruct((HEADS, nc, 1, C), F32),
                   jax.ShapeDtypeStruct((HEADS, nc, 1, C), F32), jax.ShapeDtypeStruct((HEADS, 8, 128), F32)),
        scratch_shapes=[pltpu.VMEM((H, HEAD_DIM, HEAD_DIM), F32), pltpu.VMEM((H, 8, 128), F32)],
        compiler_params=_params(("parallel",)),
    )(qkv, b_rows, a_rows, hp, states, inverses, do)


COL_Z = 3 * HEADS


def _gated_norm_fwd(o, proj, gain, *, name):
    T = o.shape[0]

    def body(o_ref, z_ref, g_ref, out_ref):
        x = o_ref[...]
        r = lax.rsqrt(jnp.mean(x * x, axis=-1, keepdims=True) + RMS_EPS)
        z = z_ref[...]
        out_ref[...] = (x * r * g_ref[...] * (z * _sigmoid(z))).astype(out_ref.dtype)

    return pl.pallas_call(
        body, name=name, grid=(HEADS,),
        in_specs=[pl.BlockSpec((T, HEAD_DIM), lambda h: (0, h)), pl.BlockSpec((T, HEAD_DIM), lambda h: (0, COL_Z + h)),
                  pl.BlockSpec((1, HEAD_DIM), lambda h: (0, 0))],
        out_specs=pl.BlockSpec((T, HEAD_DIM), lambda h: (0, h)),
        out_shape=jax.ShapeDtypeStruct((T, D_MODEL), BF16), compiler_params=_params(("parallel",)),
    )(o, proj, gain)


def _gated_norm_bwd(dout, o, proj, gain, *, name):
    T = o.shape[0]

    def body(d_ref, o_ref, z_ref, g_ref, do_ref, dz_ref, dg_ref):
        x = o_ref[...]
        r = lax.rsqrt(jnp.mean(x * x, axis=-1, keepdims=True) + RMS_EPS)
        n = x * r
        z = z_ref[...]
        sg = _sigmoid(z)
        d = d_ref[...].astype(F32)
        g = g_ref[...]
        dz_ref[...] = (d * n * g * (sg * (1.0 + z * (1.0 - sg)))).astype(dz_ref.dtype)
        dy = d * (z * sg)
        dyg = dy * g
        do_ref[...] = r * (dyg - n * jnp.mean(dyg * n, axis=-1, keepdims=True))

        @pl.when(pl.program_id(0) == 0)
        def _():
            dg_ref[...] = jnp.zeros_like(dg_ref)

        dg_ref[...] += jnp.sum(dy * n, axis=0, keepdims=True)

    head = pl.BlockSpec((T, HEAD_DIM), lambda h: (0, h))
    vec = pl.BlockSpec((1, HEAD_DIM), lambda h: (0, 0))
    return pl.pallas_call(
        body, name=name, grid=(HEADS,),
        in_specs=[head, head, pl.BlockSpec((T, HEAD_DIM), lambda h: (0, COL_Z + h)), vec],
        out_specs=(head, head, vec),
        out_shape=(jax.ShapeDtypeStruct((T, D_MODEL), F32), jax.ShapeDtypeStruct((T, D_MODEL), BF16),
                   jax.ShapeDtypeStruct((1, HEAD_DIM), F32)),
        compiler_params=_params(("arbitrary",)),
    )(dout, o, proj, gain)


COL_SBQ = 4 * HEADS
COL_SBK = 5 * HEADS
COL_SBV = 6 * HEADS


def _split_dot(x, mat):
    lead = x.shape[:-1]
    x = x.reshape(-1, x.shape[-1])
    hi = x.astype(BF16)
    lo = (x - hi.astype(F32)).astype(BF16)
    out = jnp.dot(hi, mat, preferred_element_type=F32) + jnp.dot(lo, mat, preferred_element_type=F32)
    return out.reshape(lead + (mat.shape[-1],))


def _sb_specs(T, heads, buffers=None):
    col = lambda first: pl.BlockSpec((T, heads * HEAD_DIM), lambda h: (0, first // heads + h), pipeline_mode=buffers)
    return col(COL_SBQ), col(COL_SBK), col(COL_SBV), pl.BlockSpec((1, HEAD_DIM), lambda h: (0, 0))


def _heads_first(x):
    return jnp.stack([x[:, c:c + HEAD_DIM] for c in range(0, x.shape[1], HEAD_DIM)], axis=0)


def _heads_last(x):
    return jnp.concatenate([x[h] for h in range(x.shape[0])], axis=1)


def _head_rms(x):
    r = lax.rsqrt(jnp.mean(x * x, axis=-1, keepdims=True) + RMS_EPS)
    return x * r, r


def _sb_fwd(proj, q_gain, k_gain, *, name):
    T = proj.shape[0]
    B = SB_BLOCK
    H = SB_HEADS
    nb = T // B
    KT = min(SB_KEY_TILE, T)
    NS = KT // B
    q_spec, k_spec, v_spec, g_spec = _sb_specs(T, H)

    def body(q_ref, k_ref, v_ref, gq_ref, gk_ref, o_ref, lt_ref, qs, ks, vs):
        qs[...] = (_head_rms(_heads_first(q_ref[...]))[0] * (gq_ref[...] * QK_SCALE)).astype(BF16)
        ks[...] = (_head_rms(_heads_first(k_ref[...]))[0] * gk_ref[...]).astype(BF16)
        vs[...] = _heads_first(v_ref[...]).astype(BF16)
        ii = lax.broadcasted_iota(jnp.int32, (B, B), 0)
        jj = lax.broadcasted_iota(jnp.int32, (B, B), 1)
        after = jnp.where(ii > jj, 1.0, 0.0).astype(BF16)
        ahead = lax.broadcasted_iota(jnp.int32, (H, B, KT), 2) - lax.broadcasted_iota(jnp.int32, (H, B, KT), 1)

        def q_block(i, carry):
            rows = pl.ds(pl.multiple_of(i * B, B), B)
            q = qs[:, rows, :]

            def tile(c0, acc, tail, masked):
                cols = pl.ds(c0, KT)
                z = lax.dot_general(q, ks[:, cols, :], _B_NT, preferred_element_type=F32)
                sp = _softplus(z)
                causal = ahead < (i * B - c0)
                loss = jnp.where(causal, sp, 0.0) if masked else sp
                parts = [None] * NS
                for b in reversed(range(NS)):
                    blk = loss[:, :, b * B:(b + 1) * B]
                    parts[b] = _split_dot(blk, after) + tail
                    tail = tail + jnp.sum(blk, axis=2, keepdims=True)
                lost = parts[0] if NS == 1 else jnp.concatenate(parts, axis=2)
                wts = jnp.exp(z - sp - lost)
                if masked:
                    wts = jnp.where(causal, wts, 0.0)
                acc = acc + lax.dot_general(wts.astype(BF16), vs[:, cols, :], _B_NN, preferred_element_type=F32)
                return acc, tail

            last = i // NS
            acc, tail = tile(pl.multiple_of(last * KT, KT), jnp.zeros((H, B, HEAD_DIM), F32), jnp.zeros((H, B, 1), F32), True)
            acc, tail = lax.fori_loop(
                1, last + 1, lambda s, c: tile(pl.multiple_of((last - s) * KT, KT), c[0], c[1], False), (acc, tail))
            o_ref[rows, :] = _heads_last(acc).astype(o_ref.dtype)
            lt_ref[rows, :] = _heads_last(jnp.broadcast_to(tail, (H, B, HEAD_DIM)))
            return carry

        lax.fori_loop(0, nb, q_block, 0)

    heads = pl.BlockSpec((T, H * HEAD_DIM), lambda h: (0, h))
    return pl.pallas_call(
        body, name=name, grid=(HEADS // H,), in_specs=[q_spec, k_spec, v_spec, g_spec, g_spec],
        out_specs=(heads, heads),
        out_shape=(jax.ShapeDtypeStruct((T, D_MODEL), BF16), jax.ShapeDtypeStruct((T, D_MODEL), F32)),
        scratch_shapes=[pltpu.VMEM((H, T, HEAD_DIM), BF16)] * 3, compiler_params=_params(("parallel",)),
    )(proj, proj, proj, q_gain, k_gain)


def _sb_bwd(proj, q_gain, k_gain, ltot, do, *, name):
    T = proj.shape[0]
    B = SB_BLOCK
    H = SB_HEADS_BWD
    nb = T // B
    KT = min(SB_KEY_TILE_BWD, T)
    NS = KT // B
    q_spec, k_spec, v_spec, g_spec = _sb_specs(T, H, pl.Buffered(1))

    def body(q_ref, k_ref, v_ref, gq_ref, gk_ref, lt_ref, do_ref, dq_ref, dk_ref, dv_ref, dgq_ref, dgk_ref,
             qs, ks, vs, dos, dq_acc, dk_acc, dv_acc):
        head_cols = [slice(h * HEAD_DIM, (h + 1) * HEAD_DIM) for h in range(H)]
        for h, cols in enumerate(head_cols):
            qs[h] = (_head_rms(q_ref[:, cols])[0] * (gq_ref[...] * QK_SCALE)).astype(BF16)
            ks[h] = (_head_rms(k_ref[:, cols])[0] * gk_ref[...]).astype(BF16)
            vs[h] = v_ref[:, cols].astype(BF16)
            dos[h] = do_ref[:, cols].astype(BF16)
        dk_acc[...] = jnp.zeros_like(dk_acc)
        dv_acc[...] = jnp.zeros_like(dv_acc)
        ii = lax.broadcasted_iota(jnp.int32, (B, B), 0)
        jj = lax.broadcasted_iota(jnp.int32, (B, B), 1)
        upto = jnp.where(ii <= jj, 1.0, 0.0).astype(BF16)
        before = jnp.where(ii < jj, 1.0, 0.0).astype(BF16)
        ahead = lax.broadcasted_iota(jnp.int32, (H, B, KT), 2) - lax.broadcasted_iota(jnp.int32, (H, B, KT), 1)

        def q_block(i, carry):
            rows = pl.ds(pl.multiple_of(i * B, B), B)
            q = qs[:, rows, :]
            d_o = dos[:, rows, :]
            total = jnp.max(_heads_first(lt_ref[rows, :]), axis=2, keepdims=True)

            def tile(c0, dq, head_lb, head_de, masked):
                cols = pl.ds(c0, KT)
                k = ks[:, cols, :]
                v = vs[:, cols, :]
                z = lax.dot_general(q, k, _B_NT, preferred_element_type=F32)
                sp = _softplus(z)
                causal = ahead < (i * B - c0)
                loss = jnp.where(causal, sp, 0.0) if masked else sp
                parts = [None] * NS
                for b in range(NS):
                    blk = loss[:, :, b * B:(b + 1) * B]
                    parts[b] = _split_dot(blk, upto) + head_lb
                    head_lb = head_lb + jnp.sum(blk, axis=2, keepdims=True)
                prefix = parts[0] if NS == 1 else jnp.concatenate(parts, axis=2)
                wts = jnp.exp(z - sp + (prefix - total))
                if masked:
                    wts = jnp.where(causal, wts, 0.0)
                d_w = lax.dot_general(d_o, v, _B_NT, preferred_element_type=F32)
                d_e = wts * d_w
                d_eb = d_e.astype(BF16)
                for b in range(NS):
                    inside = jnp.dot(d_eb[:, :, b * B:(b + 1) * B].reshape(H * B, B), before, preferred_element_type=F32)
                    parts[b] = inside.reshape(H, B, B) + head_de
                    head_de = head_de + jnp.sum(d_e[:, :, b * B:(b + 1) * B], axis=2, keepdims=True)
                cum = parts[0] if NS == 1 else jnp.concatenate(parts, axis=2)
                sig = jnp.exp(z - sp)
                d_z = d_e - sig * (d_e + cum)
                if masked:
                    d_z = jnp.where(causal, d_z, 0.0)
                d_zb = d_z.astype(BF16)
                dq = dq + lax.dot_general(d_zb, k, _B_NN, preferred_element_type=F32)
                dk_acc[:, cols, :] += lax.dot_general(d_zb, q, _B_TN, preferred_element_type=F32)
                dv_acc[:, cols, :] += lax.dot_general(wts.astype(BF16), d_o, _B_TN, preferred_element_type=F32)
                return dq, head_lb, head_de

            last = i // NS
            zero = jnp.zeros((H, B, 1), F32)
            state = lax.fori_loop(0, last, lambda t, c: tile(pl.multiple_of(t * KT, KT), *c, False),
                                  (jnp.zeros((H, B, HEAD_DIM), F32), zero, zero))
            dq, _, _ = tile(pl.multiple_of(last * KT, KT), *state, True)
            dq_acc[:, rows, :] = dq * QK_SCALE
            return carry

        lax.fori_loop(0, nb, q_block, 0)

        def norm_bwd(d_scaled, x, gain):
            n, r = _head_rms(x)
            dn = d_scaled * gain
            return r * (dn - n * jnp.mean(dn * n, axis=-1, keepdims=True)), jnp.sum(d_scaled * n, axis=0, keepdims=True)

        @pl.when(pl.program_id(0) == 0)
        def _():
            dgq_ref[...] = jnp.zeros_like(dgq_ref)
            dgk_ref[...] = jnp.zeros_like(dgk_ref)

        for h, cols in enumerate(head_cols):
            dq_raw, dgq = norm_bwd(dq_acc[h], q_ref[:, cols], gq_ref[...])
            dk_raw, dgk = norm_bwd(dk_acc[h], k_ref[:, cols], gk_ref[...])
            dq_ref[:, cols] = dq_raw.astype(dq_ref.dtype)
            dk_ref[:, cols] = dk_raw.astype(dk_ref.dtype)
            dv_ref[:, cols] = dv_acc[h].astype(dv_ref.dtype)
            dgq_ref[...] += dgq
            dgk_ref[...] += dgk

    heads = pl.BlockSpec((T, H * HEAD_DIM), lambda h: (0, h), pipeline_mode=pl.Buffered(1))
    out = jax.ShapeDtypeStruct((T, D_MODEL), BF16)
    vec = jax.ShapeDtypeStruct((1, HEAD_DIM), F32)
    return pl.pallas_call(
        body, name=name, grid=(HEADS // H,), in_specs=[q_spec, k_spec, v_spec, g_spec, g_spec, heads, heads],
        out_specs=(heads, heads, heads, g_spec, g_spec), out_shape=(out, out, out, vec, vec),
        scratch_shapes=[pltpu.VMEM((H, T, HEAD_DIM), BF16)] * 4 + [pltpu.VMEM((H, T, HEAD_DIM), F32)] * 3,
        compiler_params=_params(("arbitrary",), vmem=V7X_VMEM_LIMIT_HIGH),
    )(proj, proj, proj, q_gain, k_gain, ltot, do)


ADAM_ROWS = 256


def _adamw(g_parts, w, m, v, *, name, layer=0, earlier=None):
    K, A, C = g_parts.shape
    R = w.shape[0]
    tr = next((t for t in (ADAM_ROWS, ADAM_ROWS // 2) if A % t == 0), A // 2 if A % 32 == 0 else A)
    first_block = layer * (A // tr)

    def body(g_ref, w_ref, m_ref, v_ref, *rest):
        go_ref, d_ref, mo_ref, vo_ref = rest[-4:]
        g = g_ref[0].astype(F32)
        for k in range(1, K):
            g = g + g_ref[k].astype(F32)
        go_ref[...] = g
        m_new = ADAM_B1 * m_ref[...] + (1.0 - ADAM_B1) * g
        v_new = ADAM_B2 * v_ref[...] + (1.0 - ADAM_B2) * (g * g)
        m_hat = m_new / (1.0 - ADAM_B1 ** ADAM_STEP)
        v_hat = v_new / (1.0 - ADAM_B2 ** ADAM_STEP)
        d_ref[...] = -ADAM_LR * (m_hat / (jnp.sqrt(v_hat) + ADAM_EPS) + ADAM_WD * w_ref[...])
        mo_ref[...] = m_new
        vo_ref[...] = v_new

    row = pl.BlockSpec((tr, C), lambda i: (first_block + i, 0))
    out = jax.ShapeDtypeStruct((R, C), F32)
    in_specs = [pl.BlockSpec((K, tr, C), lambda i: (0, i, 0)), row, row, row]
    if earlier is None:
        return pl.pallas_call(
            body, name=name, grid=(A // tr,), in_specs=in_specs, out_specs=(row, row, row, row),
            out_shape=(out, out, out, out), compiler_params=_params(("parallel",)),
        )(g_parts, w, m, v)
    return pl.pallas_call(
        body, name=name, grid=(A // tr,), in_specs=in_specs + [ANY] * 4, out_specs=(row, row, row, row),
        out_shape=(out, out, out, out), input_output_aliases={4 + j: j for j in range(4)},
        compiler_params=_params(("parallel",)),
    )(g_parts, w, m, v, *earlier)


def _sum_parts(parts, *, name):
    K, R, C = parts.shape

    def body(p_ref, o_ref):
        acc = p_ref[0]
        for k in range(1, K):
            acc = acc + p_ref[k]
        o_ref[...] = acc

    return pl.pallas_call(body, name=name, out_shape=jax.ShapeDtypeStruct((R, C), F32))(parts)


def _position():
    return lax.axis_index("x"), lax.axis_index("y"), lax.axis_index("c")


def _all_gather(shards, *, name):
    n = len(shards)

    def body(*refs):
        x_refs, out_refs = refs[:n], refs[n:2 * n]
        send_sems, recv_sems, local_sems = refs[2 * n:]
        x, y, c = _position()
        me, sibling = (x, y, c), (x, y, 1 - c)
        chips = [(1 - x, y), (x, 1 - y), (1 - x, 1 - y)]

        def slot(a, px, py, pc):
            return out_refs[a].at[4 * px + 2 * py + pc]

        def copy(a, k, block, to, own=False):
            return pltpu.make_async_remote_copy(
                src_ref=x_refs[a] if own else slot(a, *block), dst_ref=slot(a, *block),
                send_sem=send_sems.at[a, k], recv_sem=recv_sems.at[a, k], device_id=to, device_id_type=MESH)

        mine = [pltpu.make_async_copy(x_refs[a], slot(a, *me), local_sems.at[a]) for a in range(n)]
        for cp in mine:
            cp.start()
        first = [copy(a, 1 + j, me, (*chip, c), own=True) for j, chip in enumerate(chips) for a in range(n)]
        first += [copy(a, 0, me, sibling, own=True) for a in range(n)]
        for cp in first:
            cp.start()
        passed = []
        for j, chip in enumerate(chips):
            for a in range(n):
                copy(a, 1 + j, (*chip, c), me).wait_recv()
                passed.append(copy(a, 4 + j, (*chip, c), sibling))
                passed[-1].start()
        for a in range(n):
            copy(a, 0, sibling, me).wait_recv()
        for j, chip in enumerate(chips):
            for a in range(n):
                copy(a, 4 + j, (*chip, 1 - c), me).wait_recv()
        for cp in first + passed:
            cp.wait_send()
        for cp in mine:
            cp.wait()

    return pl.pallas_call(
        body, name=name, in_specs=[ANY] * n, out_specs=[ANY] * n,
        out_shape=[jax.ShapeDtypeStruct((N_DEV,) + s.shape, s.dtype) for s in shards],
        scratch_shapes=[pltpu.SemaphoreType.DMA((n, 7)), pltpu.SemaphoreType.DMA((n, 7)), pltpu.SemaphoreType.DMA((n,))],
    )(*shards)


HBM = pl.BlockSpec(memory_space=pltpu.HBM)
SEM = pl.BlockSpec(memory_space=pltpu.SEMAPHORE)
DATAFLOW = pltpu.SideEffectType.DATAFLOW_SIDE_EFFECTING


def _exchange_copies(gather, x_refs, land_refs, send_sems, recv_sems, local_sems):
    n = len(x_refs)
    x, y, c = _position()
    me = 4 * x + 2 * y + c

    def src(a, slot):
        return x_refs[a] if gather else x_refs[a].at[slot]

    mine = [pltpu.make_async_copy(src(a, me), land_refs[a].at[me], local_sems.at[a]) for a in range(n)]
    sends, recvs = [], []
    for k in range(1, N_DEV):
        px, py, pc = (x + (k >> 2)) % 2, (y + ((k >> 1) & 1)) % 2, (c + (k & 1)) % 2
        peer = 4 * px + 2 * py + pc
        for a in range(n):
            sems = dict(send_sem=send_sems.at[7 * a + k - 1], recv_sem=recv_sems.at[7 * a + k - 1],
                        device_id=(px, py, pc), device_id_type=MESH)
            sends.append(pltpu.make_async_remote_copy(src_ref=src(a, peer), dst_ref=land_refs[a].at[me], **sems))
            recvs.append(pltpu.make_async_remote_copy(src_ref=src(a, me), dst_ref=land_refs[a].at[peer], **sems))
    return mine, sends, recvs


def _exchange_start(parts, *, gather, name):
    n = len(parts)

    def body(*refs):
        x_refs, land_refs = refs[:n], refs[n:2 * n]
        send_sems, recv_sems, local_sems = refs[2 * n:2 * n + 3]
        token = refs[-1]
        mine, sends, _ = _exchange_copies(gather, x_refs, land_refs, send_sems, recv_sems, local_sems)
        for cp in mine + sends:
            cp.start()
        token[...] = jnp.zeros_like(token)

    sems = (pltpu.SemaphoreType.DMA((7 * n,)), pltpu.SemaphoreType.DMA((7 * n,)), pltpu.SemaphoreType.DMA((n,)))
    thru = tuple(pltpu.HBM(p.shape, p.dtype) for p in parts)
    land = tuple(pltpu.HBM(((N_DEV,) if gather else ()) + p.shape, p.dtype) for p in parts)
    res = pl.pallas_call(
        body, name=name, in_specs=[HBM] * (2 * n),
        out_specs=(SEM, SEM, SEM) + (HBM,) * (2 * n) + (pl.BlockSpec(memory_space=pltpu.VMEM),),
        out_shape=sems + thru + land + (jax.ShapeDtypeStruct((8, 128), F32),),
        input_output_aliases={a: 3 + a for a in range(2 * n)},
        compiler_params=pltpu.CompilerParams(has_side_effects=DATAFLOW),
    )(*[pltpu.with_memory_space_constraint(p, pltpu.HBM) for p in parts],
      *[pltpu.with_memory_space_constraint(lax.empty(z.shape, z.dtype), pltpu.HBM) for z in land])
    return res[:3], res[3:3 + n], res[3 + n:3 + 2 * n], res[-1]


def _exchange_wait(sems, parts, landing, after, *, gather, name):
    n = len(parts)
    after = list(after)

    def body(*refs):
        x_refs, land_refs = refs[:n], refs[n:2 * n]
        send_sems, recv_sems, local_sems = refs[2 * n:2 * n + 3]
        token = refs[-1]
        mine, sends, recvs = _exchange_copies(gather, x_refs, land_refs, send_sems, recv_sems, local_sems)
        for cp in recvs:
            cp.wait_recv()
        for cp in sends:
            cp.wait_send()
        for cp in mine:
            cp.wait()
        token[...] = jnp.zeros_like(token)

    thru = tuple(pltpu.HBM(p.shape, p.dtype) for p in tuple(parts) + tuple(landing))
    res = pl.pallas_call(
        body, name=name, in_specs=[HBM] * (2 * n) + [SEM, SEM, SEM] + [ANY] * len(after),
        out_specs=(HBM,) * (2 * n) + (pl.BlockSpec(memory_space=pltpu.VMEM),),
        out_shape=thru + (jax.ShapeDtypeStruct((8, 128), F32),), input_output_aliases={a: a for a in range(2 * n)},
        compiler_params=pltpu.CompilerParams(has_side_effects=DATAFLOW),
    )(*parts, *landing, *sems, *after)
    return res[n:2 * n], res[-1]


def _relay_copies(x_refs, land_refs, first_send, first_recv, relay_send, relay_recv, local_sems):
    n = len(x_refs)
    x, y, c = _position()
    sibling = (x, y, 1 - c)
    chips = [(1 - x, y), (x, 1 - y), (1 - x, 1 - y)]

    def slot(a, px, py, pc):
        return land_refs[a].at[4 * px + 2 * py + pc]

    def hop(a, k, block, to, own=False):
        return pltpu.make_async_remote_copy(
            src_ref=x_refs[a] if own else slot(a, *block), dst_ref=slot(a, *block),
            send_sem=first_send.at[4 * a + k], recv_sem=first_recv.at[4 * a + k], device_id=to, device_id_type=MESH)

    def relay(a, j, block, to):
        return pltpu.make_async_remote_copy(
            src_ref=slot(a, *block), dst_ref=slot(a, *block),
            send_sem=relay_send.at[3 * a + j], recv_sem=relay_recv.at[3 * a + j], device_id=to, device_id_type=MESH)

    me = (x, y, c)
    mine = [pltpu.make_async_copy(x_refs[a], slot(a, *me), local_sems.at[a]) for a in range(n)]
    sends = [hop(a, 1 + j, me, (*chip, c), own=True) for j, chip in enumerate(chips) for a in range(n)]
    sends += [hop(a, 0, me, sibling, own=True) for a in range(n)]
    over_ici = [hop(a, 1 + j, (*chip, c), me) for j, chip in enumerate(chips) for a in range(n)]
    from_sibling = [hop(a, 0, sibling, me) for a in range(n)]
    if relay_send is None:
        return mine, sends, over_ici, from_sibling, [], []
    relays = [relay(a, j, (*chip, c), sibling) for j, chip in enumerate(chips) for a in range(n)]
    relayed = [relay(a, j, (*chip, 1 - c), me) for j, chip in enumerate(chips) for a in range(n)]
    return mine, sends, over_ici, from_sibling, relays, relayed


def _relay_call(body, n_sem_in, n_sem_out, shards, landing, sems_in, after, name):
    n = len(shards)
    thru = tuple(pltpu.HBM(p.shape, p.dtype) for p in tuple(shards) + tuple(landing))
    res = pl.pallas_call(
        body, name=name, in_specs=[HBM] * (2 * n) + [SEM] * n_sem_in + [ANY] * len(after),
        out_specs=(SEM,) * len(n_sem_out) + (HBM,) * (2 * n) + (pl.BlockSpec(memory_space=pltpu.VMEM),),
        out_shape=tuple(pltpu.SemaphoreType.DMA((k,)) for k in n_sem_out) + thru + (jax.ShapeDtypeStruct((8, 128), F32),),
        input_output_aliases={a: len(n_sem_out) + a for a in range(2 * n)},
        compiler_params=pltpu.CompilerParams(has_side_effects=DATAFLOW),
    )(*shards, *landing, *sems_in, *after)
    k = len(n_sem_out)
    return res[:k], res[k:k + n], res[k + n:k + 2 * n], res[-1]


def _layer_refs(refs, layers):
    return [r if l is None else r.at[l] for r, l in zip(refs, layers)]


def _relay_gather_start(shards, layers, after, *, name):
    n = len(shards)

    def body(*refs):
        x_refs, land_refs = _layer_refs(refs[:n], layers), refs[n:2 * n]
        first_send, first_recv, local_sems = refs[2 * n + len(after):2 * n + len(after) + 3]
        mine, sends, *_ = _relay_copies(x_refs, land_refs, first_send, first_recv, None, None, local_sems)
        for cp in mine + sends:
            cp.start()
        refs[-1][...] = jnp.zeros_like(refs[-1])

    landing = [pltpu.with_memory_space_constraint(
        lax.empty((N_DEV,) + (s.shape if l is None else s.shape[1:]), s.dtype), pltpu.HBM) for s, l in zip(shards, layers)]
    shards = [pltpu.with_memory_space_constraint(s, pltpu.HBM) for s in shards]
    return _relay_call(body, 0, (4 * n, 4 * n, n), shards, landing, (), list(after), name)


def _relay_gather_pass_on(first, shards, layers, landing, after, *, name):
    n = len(shards)

    def body(*refs):
        x_refs, land_refs = _layer_refs(refs[:n], layers), refs[n:2 * n]
        first_send, first_recv, local_sems = refs[2 * n:2 * n + 3]
        relay_send, relay_recv = refs[2 * n + 3 + len(after):2 * n + 5 + len(after)]
        _, _, over_ici, _, relays, _ = _relay_copies(x_refs, land_refs, first_send, first_recv, relay_send, relay_recv,
                                                   local_sems)
        for arrival, cp in zip(over_ici, relays):
            arrival.wait_recv()
            cp.start()
        refs[-1][...] = jnp.zeros_like(refs[-1])

    return _relay_call(body, 3, (3 * n, 3 * n), shards, landing, first, list(after), name)


def _relay_gather_wait(first, relay, shards, layers, landing, after, *, name):
    n = len(shards)

    def body(*refs):
        x_refs, land_refs = _layer_refs(refs[:n], layers), refs[n:2 * n]
        first_send, first_recv, local_sems, relay_send, relay_recv = refs[2 * n:2 * n + 5]
        mine, sends, _, from_sibling, relays, relayed = _relay_copies(
            x_refs, land_refs, first_send, first_recv, relay_send, relay_recv, local_sems)
        for cp in from_sibling + relayed:
            cp.wait_recv()
        for cp in sends + relays:
            cp.wait_send()
        for cp in mine:
            cp.wait()
        refs[-1][...] = jnp.zeros_like(refs[-1])

    _, shards, landing, token = _relay_call(body, 5, (), shards, landing, tuple(first) + tuple(relay), list(after), name)
    return shards, landing, token


def _ffn_in_act(h, wg_in, *, name):
    T, D = h.shape
    fb = wg_in.shape[-1]
    tm = min(T, 1024)

    def body(h_ref, wg_ref, wu_ref, p_ref, a_ref):
        hv = h_ref[...]
        gate = jnp.dot(hv, wg_ref[...], preferred_element_type=F32)
        up = jnp.dot(hv, wu_ref[...], preferred_element_type=F32)
        p_ref[0] = gate.astype(BF16)
        p_ref[1] = up.astype(BF16)
        a_ref[...] = (gate * _sigmoid(gate) * up).astype(BF16)

    p, a = pl.pallas_call(
        body, name=name, grid=(T // tm, FF_HALF),
        in_specs=[pl.BlockSpec((tm, D), lambda i, j: (i, 0)), pl.BlockSpec((None, D, fb), lambda i, j: (j, 0, 0)),
                  pl.BlockSpec((None, D, fb), lambda i, j: (j + FF_HALF, 0, 0))],
        out_specs=(pl.BlockSpec((2, None, tm, fb), lambda i, j: (0, j, i, 0)), pl.BlockSpec((None, tm, fb), lambda i, j: (j, i, 0))),
        out_shape=(jax.ShapeDtypeStruct((2, FF_HALF, T, fb), BF16), jax.ShapeDtypeStruct((FF_HALF, T, fb), BF16)),
        compiler_params=_params(("parallel", "parallel")),
    )(h, wg_in, wg_in)
    return p.reshape(2 * FF_HALF, T, fb), a


def _ffn_fwd(x, gain, wg_in, wg_out, tag):
    T, D = x.shape
    fb, rb = wg_in.shape[-1], wg_out.shape[-2]
    tm, tn = min(T, 1024), 512
    h = _rmsnorm_fwd(x, gain, name=f"{tag}_norm")
    p, a = _ffn_in_act(h, wg_in, name=f"{tag}_in_act")
    y = _mm(name=f"{tag}_out", grid=(T // tm, 1, FF_HALF), tile=(tm, D), resid=x, scale=0.5,
            a=a, a_spec=pl.BlockSpec((None, tm, fb), lambda i, j, k: (k, i, 0)),
            b=wg_out.reshape(N_DEV * rb, D), b_spec=pl.BlockSpec((fb, D), lambda i, j, k: (k, 0)),
            out_shape=jax.ShapeDtypeStruct((T, D), F32), o_spec=pl.BlockSpec((tm, D), lambda i, j, k: (i, 0)))
    return y, (x, h, p, a)


def _ffn_bwd(dy, saved, gain, wg_in, wg_out, tag, on_weight_grads=None):
    x, h, p, a = saved
    T, D = x.shape
    fb, rb = wg_in.shape[-1], wg_out.shape[-2]
    tm, tn = min(T, 1024), 512
    da = _mm(name=f"{tag}_out_dx", grid=(T // tm, FF_HALF, 1), tile=(tm, fb), tb=True, scale=0.5,
             a=dy, a_spec=pl.BlockSpec((tm, D), lambda i, j, k: (i, 0)),
             b=wg_out.reshape(N_DEV * rb, D), b_spec=pl.BlockSpec((fb, D), lambda i, j, k: (j, 0)),
             out_shape=jax.ShapeDtypeStruct((FF_HALF, T, fb), BF16), o_spec=pl.BlockSpec((None, tm, fb), lambda i, j, k: (j, i, 0)))
    d_w_out = _mm(name=f"{tag}_out_dw", grid=(FF_HALF, D // tn, 1), tile=(fb, tn), ta=True, scale=0.5,
                  a=a, a_spec=pl.BlockSpec((None, T, fb), lambda i, j, k: (i, 0, 0)),
                  b=dy, b_spec=pl.BlockSpec((T, tn), lambda i, j, k: (0, j)),
                  out_shape=jax.ShapeDtypeStruct((FF_HALF, fb, D), BF16), o_spec=pl.BlockSpec((None, fb, tn), lambda i, j, k: (i, 0, j)))
    dp = _swiglu_bwd(da, p, name=f"{tag}_act_bwd")
    d_w_in = _mm(name=f"{tag}_in_dw", grid=(1, N_DEV, 1), tile=(D, fb), ta=True,
                 a=h, a_spec=pl.BlockSpec((T, D), lambda i, j, k: (0, 0)),
                 b=dp, b_spec=pl.BlockSpec((None, T, fb), lambda i, j, k: (j, 0, 0)),
                 out_shape=jax.ShapeDtypeStruct((N_DEV, D, fb), BF16), o_spec=pl.BlockSpec((None, D, fb), lambda i, j, k: (j, 0, 0)))
    dh = _mm(name=f"{tag}_in_dx", grid=(T // tm, 1, N_DEV), tile=(tm, D), tb=True,
             a=dp, a_spec=pl.BlockSpec((None, tm, fb), lambda i, j, k: (k, i, 0)),
             b=wg_in, b_spec=pl.BlockSpec((None, D, fb), lambda i, j, k: (k, 0, 0)),
             out_shape=jax.ShapeDtypeStruct((T, D), F32), o_spec=pl.BlockSpec((tm, D), lambda i, j, k: (i, 0)))
    d_w_out = d_w_out.reshape(N_DEV, rb, D)
    if on_weight_grads is not None:
        gain = gain + on_weight_grads(d_w_in, d_w_out)[0, 0]
    dx, d_gain = _rmsnorm_bwd(dh, x, gain, dy, name=f"{tag}_norm_bwd")
    return dx, d_gain, d_w_in, d_w_out


def _square_mm(a, wg, *, name, transposed=False, out_dtype=F32, resid=None):
    T, D = a.shape
    w = wg.reshape(D, D)
    return _matmul(a, w, tb=transposed, name=name, out_dtype=out_dtype, resid=resid)


def _head_rows(cols, T):
    return cols.T.reshape(HEADS, T // DN_CHUNK, 1, DN_CHUNK)


def _mixer_fwd(x, w, big, tag):
    T = x.shape[0]
    h = _rmsnorm_fwd(x, w["mix_norm"], name=f"{tag}_norm")
    proj = _matmul(h, big["w_main"], name=f"{tag}_proj")
    scal = _matmul(h, big["w_scal"], name=f"{tag}_proj_scal", tn=N_SCAL)
    qkv = _conv_fwd(proj, big["conv_w"], name=f"{tag}_conv")
    b_rows = _head_rows(scal[:, 0:HEADS], T)
    a_rows = _head_rows(scal[:, HEADS:2 * HEADS], T)
    o_a, *states = _dn_fwd(qkv, b_rows, a_rows, w["hp"], name=f"{tag}_dn")
    oa_n = _gated_norm_fwd(o_a, proj, w["dn_out_norm"], name=f"{tag}_dn_norm")
    ya = _square_mm(oa_n, big["w_branch_a"], name=f"{tag}_branch_a")
    o_b, ltot = _sb_fwd(proj, w["sb_q_norm"], w["sb_k_norm"], name=f"{tag}_sb")
    yb = _square_mm(o_b, big["w_branch_b"], name=f"{tag}_branch_b")
    merged = _merge_fwd(ya, yb, proj, name=f"{tag}_merge")
    y = _square_mm(merged, big["w_out"], name=f"{tag}_out", resid=x)
    return y, (x, h, proj, qkv, b_rows, a_rows, o_a, states, oa_n, ya, o_b, ltot, yb, merged)


def _mixer_bwd(dy, saved, w, big, tag, on_weight_grads):
    x, h, proj, qkv, b_rows, a_rows, o_a, states, oa_n, ya, o_b, ltot, yb, merged = saved
    T = x.shape[0]
    g = {}
    d_merged = _square_mm(dy, big["w_out"], transposed=True, name=f"{tag}_out_dx", out_dtype=BF16)
    g["w_out"] = _matmul(merged, dy, ta=True, name=f"{tag}_out_dw", out_dtype=BF16)
    d_ya, d_yb, d_ga, d_gb = _merge_bwd(d_merged, ya, yb, proj, name=f"{tag}_merge_bwd")
    d_oan = _square_mm(d_ya, big["w_branch_a"], transposed=True, name=f"{tag}_branch_a_dx")
    g["w_branch_a"] = _matmul(oa_n, d_ya, ta=True, name=f"{tag}_branch_a_dw", out_dtype=BF16)
    d_ob = _square_mm(d_yb, big["w_branch_b"], transposed=True, name=f"{tag}_branch_b_dx")
    g["w_branch_b"] = _matmul(o_b, d_yb, ta=True, name=f"{tag}_branch_b_dw", out_dtype=BF16)
    d_oa, d_z, g["dn_out_norm"] = _gated_norm_bwd(d_oan, o_a, proj, w["dn_out_norm"], name=f"{tag}_dn_norm_bwd")
    d_qkv, d_b_rows, d_a_rows, d_hp = _dn_bwd(qkv, b_rows, a_rows, w["hp"], *states, d_oa, name=f"{tag}_dn_bwd")
    g["dn_a_log"] = d_hp[:, 0, 0]
    g["dn_dt_bias"] = d_hp[:, 1, 0]
    d_conv_in, g["conv_w"] = _conv_bwd(d_qkv, proj, big["conv_w"], name=f"{tag}_conv_bwd")
    d_sbq, d_sbk, d_sbv, g["sb_q_norm"], g["sb_k_norm"] = _sb_bwd(
        proj, w["sb_q_norm"], w["sb_k_norm"], ltot, d_ob, name=f"{tag}_sb_bwd")
    d_proj = jnp.concatenate([d_conv_in, d_z, d_sbq, d_sbk, d_sbv, d_ga, d_gb], axis=1)
    d_scal = jnp.concatenate([d_b_rows.reshape(HEADS, T).T, d_a_rows.reshape(HEADS, T).T,
                              jnp.zeros((T, N_SCAL - 2 * HEADS), F32)], axis=1).astype(BF16)
    g["w_main"] = _matmul(h, d_proj, ta=True, name=f"{tag}_proj_dw", out_dtype=BF16)
    g["w_scal"] = _matmul(h, d_scal, ta=True, name=f"{tag}_proj_scal_dw", out_dtype=BF16, tn=N_SCAL)
    dh_scal = _matmul(d_scal, big["w_scal"], tb=True, name=f"{tag}_proj_scal_dx")
    dh = _matmul(d_proj, big["w_main"], tb=True, name=f"{tag}_proj_dx", tk=N_MAIN // 4, resid=dh_scal)
    gain = w["mix_norm"] + on_weight_grads(g)[0, 0]
    dx, g["mix_norm"] = _rmsnorm_bwd(dh, x, gain, dy, name=f"{tag}_norm_bwd")
    return dx, g


def _local_step(x, target, layers, weights_of, on_weight_grads):
    saved, bigs = [], []
    for l, w in enumerate(layers):
        big = weights_of(l, 0, x)
        x, s1 = _ffn_fwd(x, w["ffn1_norm"] + big["issued"], big["ffn1_w_in"], big["ffn1_w_out"], f"l{l}_ffn1")
        big.update(weights_of(l, 1, x))
        x, s2 = _mixer_fwd(x, dict(w, mix_norm=w["mix_norm"] + big["issued"]), big, f"l{l}_mix")
        big.update(weights_of(l, 2, x))
        x, s3 = _ffn_fwd(x, w["ffn2_norm"] + big["issued"], big["ffn2_w_in"], big["ffn2_w_out"], f"l{l}_ffn2")
        saved.append((s1, s2, s3))
        bigs.append(big)
    loss, dx = _loss_head(x, target, name="loss_head")
    small = [None] * len(layers)
    for l in reversed(range(len(layers))):
        w, big = layers[l], bigs[l]
        s1, s2, s3 = saved[l]
        dx, g_n2, _, _ = _ffn_bwd(
            dx, s3, w["ffn2_norm"], big["ffn2_w_in"], big["ffn2_w_out"], f"l{l}_ffn2",
            on_weight_grads=lambda g_in, g_out, l=l: on_weight_grads(l, 0, dict(ffn2_w_in=g_in, ffn2_w_out=g_out)))
        dx, g = _mixer_bwd(dx, s2, w, big, f"l{l}_mix", on_weight_grads=lambda g, l=l: on_weight_grads(l, 1, g))
        dx, g_n1, _, _ = _ffn_bwd(
            dx, s1, w["ffn1_norm"], big["ffn1_w_in"], big["ffn1_w_out"], f"l{l}_ffn1",
            on_weight_grads=lambda g_in, g_out, l=l: on_weight_grads(l, 2, dict(ffn1_w_in=g_in, ffn1_w_out=g_out)))
        small[l] = dict(g, ffn1_norm=g_n1, ffn2_norm=g_n2)
    return loss, dx, small


_BIG = ("ffn1_w_in", "ffn1_w_out", "w_in", "w_branch_a", "w_branch_b", "w_out", "ffn2_w_in", "ffn2_w_out")
_STAGES = (("ffn2_w_in", "ffn2_w_out"), ("w_in", "w_branch_a", "w_branch_b", "w_out"), ("ffn1_w_in", "ffn1_w_out"))
_SMALL = ("ffn1_norm", "mix_norm", "ffn2_norm", "dn_a_log", "dn_dt_bias", "dn_out_norm", "sb_q_norm", "sb_k_norm")
_ORDER = ("ffn1_norm", "ffn1_w_in", "ffn1_w_out", "mix_norm", "w_in", "dn_conv_w", "dn_a_log", "dn_dt_bias", "dn_out_norm",
          "sb_q_norm", "sb_k_norm", "w_branch_a", "w_branch_b", "w_out", "ffn2_norm", "ffn2_w_in", "ffn2_w_out")
COL_SCAL = 4 * D_MODEL
SCAL_SLOT = COL_SCAL // (N_IN // N_DEV)
SCAL_AT = COL_SCAL % (N_IN // N_DEV)
assert SCAL_AT + 2 * HEADS <= N_IN // N_DEV


def _pad_rows(a, multiple):
    pad = (-a.shape[-2]) % multiple
    return a if pad == 0 else jnp.pad(a, [(0, 0)] * (a.ndim - 2) + [(0, pad), (0, 0)])


def _lane_rows(a):
    flat = a.reshape(-1)
    flat = jnp.pad(flat, (0, (-flat.shape[0]) % 128))
    return flat.reshape(-1, 128)


def _pack_small(named):
    pieces, spans, r = [], {}, 0
    for n, a in named:
        rows = _lane_rows(a)
        spans[n] = (r, r + rows.shape[0], a.shape)
        r += rows.shape[0]
        pieces.append(rows)
    return _pad_rows(jnp.concatenate(pieces, axis=0), 8), spans


def _unpack_small(packed, spans, n):
    r0, r1, shape = spans[n]
    return packed[r0:r1].reshape(-1)[:math.prod(shape)].reshape(shape)


def kernel(x, ffn1_norm, ffn1_w_in, ffn1_w_out, mix_norm, w_in, dn_conv_w, dn_a_log, dn_dt_bias, dn_out_norm, sb_q_norm, sb_k_norm, w_branch_a, w_branch_b, w_out, ffn2_norm, ffn2_w_in, ffn2_w_out, loss_target, m_ffn1_norm, m_ffn1_w_in, m_ffn1_w_out, m_mix_norm, m_w_in, m_dn_conv_w, m_dn_a_log, m_dn_dt_bias, m_dn_out_norm, m_sb_q_norm, m_sb_k_norm, m_w_branch_a, m_w_branch_b, m_w_out, m_ffn2_norm, m_ffn2_w_in, m_ffn2_w_out, v_ffn1_norm, v_ffn1_w_in, v_ffn1_w_out, v_mix_norm, v_w_in, v_dn_conv_w, v_dn_a_log, v_dn_dt_bias, v_dn_out_norm, v_sb_q_norm, v_sb_k_norm, v_w_branch_a, v_w_branch_b, v_w_out, v_ffn2_norm, v_ffn2_w_in, v_ffn2_w_out):
    given = dict(locals())
    weights = {n: given[n] for n in _ORDER}
    mom_m = {n: given["m_" + n] for n in _ORDER}
    mom_v = {n: given["v_" + n] for n in _ORDER}
    L = ffn1_norm.shape[0]
    ax, ay, ac = _position()
    my_slot = 4 * ax + 2 * ay + ac

    conv_cols = dn_conv_w.shape[-1]
    second_ffn, mixer, first_ffn = _STAGES
    later = mixer + second_ffn
    sources = {n: weights[n].astype(BF16) for n in _BIG}
    sources["conv"] = _pad_rows(_lane_rows(dn_conv_w), 8)
    gathers, landed = {}, {}

    def start_gather(key, names, l, after):
        layer_of = [None if n == "conv" else l for n in names]
        first, thru, landing, token = _relay_gather_start([sources[n] for n in names], layer_of, after,
                                                          name=f"gather_start_{key}")
        sources.update(zip(names, thru))
        gathers[key] = dict(first=first, landing=landing, names=names, layers=layer_of)
        return token

    def pass_on(key, after):
        g = gathers[key]
        g["relay"], thru, g["landing"], token = _relay_gather_pass_on(
            g["first"], [sources[n] for n in g["names"]], g["layers"], g["landing"], [after], name=f"gather_pass_on_{key}")
        sources.update(zip(g["names"], thru))
        return token

    def wait_gather(key, after):
        g = gathers.pop(key)
        thru, arrays, token = _relay_gather_wait(g["first"], g["relay"], [sources[n] for n in g["names"]], g["layers"],
                                                 g["landing"], [after], name=f"gather_wait_{key}")
        sources.update(zip(g["names"], thru))
        landed.update(zip(g["names"], arrays))
        return token

    def weights_of(l, part, x_in):
        if l == 0 and part == 0:
            start_gather("l0_ffn1", first_ffn, 0, [])
            pass_on("l0_ffn1", x_in)
            issued = start_gather("l0_mix", mixer + ("conv",), 0, [wait_gather("l0_ffn1", x_in)])
            return dict({n: landed.pop(n) for n in first_ffn}, issued=issued[0, 0])
        if part == 0:
            token = wait_gather(f"l{l}", x_in)
            issued = start_gather(f"l{l + 1}", _BIG, l + 1, [token]) if l + 1 < L else token
            return dict({n: landed.pop(n) for n in first_ffn}, issued=issued[0, 0])
        if part == 2:
            issued = 0.0
            if l == 0:
                pass_on("l0_ffn2", x_in)
                issued = wait_gather("l0_ffn2", x_in)[0, 0]
            if l + 1 < L:
                issued = pass_on(f"l{l + 1}", x_in)[0, 0]
            return dict({n: landed.pop(n) for n in second_ffn if n in landed}, issued=issued)
        issued = 0.0
        if l == 0:
            pass_on("l0_mix", x_in)
            token = start_gather("l0_ffn2", second_ffn, 0, [wait_gather("l0_mix", x_in)])
            issued = (start_gather("l1", _BIG, 1, [token]) if L > 1 else token)[0, 0]
            conv = landed.pop("conv").reshape(N_DEV, -1)[:, :L * DN_CONV * conv_cols]
            landed["conv_w"] = conv.reshape(N_DEV, L, DN_CONV, conv_cols).transpose(1, 2, 0, 3).reshape(
                L, DN_CONV, N_DEV * conv_cols)
        big = {n: landed.pop(n) for n in later if n in landed}
        wi = big.pop("w_in")
        pieces = [wi[d] for d in range(N_DEV)]
        pieces[SCAL_SLOT:SCAL_SLOT + 1] = [wi[SCAL_SLOT][:, :SCAL_AT], wi[SCAL_SLOT][:, SCAL_AT + 2 * HEADS:]]
        big["w_main"] = jnp.concatenate(pieces, axis=1)
        big["w_scal"] = jnp.pad(wi[SCAL_SLOT][:, SCAL_AT:SCAL_AT + 2 * HEADS], ((0, 0), (0, N_SCAL - 2 * HEADS)))
        big["conv_w"] = landed["conv_w"][l]
        return dict(big, issued=issued)

    layers = []
    for l in range(L):
        hp = jnp.concatenate([jnp.broadcast_to(dn_a_log[l][:, None, None], (HEADS, 1, 128)),
                              jnp.broadcast_to(dn_dt_bias[l][:, None, None], (HEADS, 1, 128)),
                              jnp.zeros((HEADS, 6, 128), F32)], axis=1)
        layers.append(dict(ffn1_norm=ffn1_norm[l][None], mix_norm=mix_norm[l][None], hp=hp,
                           dn_out_norm=dn_out_norm[l][None], sb_q_norm=sb_q_norm[l][None],
                           sb_k_norm=sb_k_norm[l][None], ffn2_norm=ffn2_norm[l][None]))

    in_flight = {}

    def on_weight_grads(l, stage, g):
        parts = dict(g)
        if stage == 1:
            gm, shard = g["w_main"], N_IN // N_DEV
            blocks = [gm[:, d * shard:(d + 1) * shard] for d in range(SCAL_SLOT)]
            blocks.append(jnp.concatenate([gm[:, SCAL_SLOT * shard:COL_SCAL], g["w_scal"][:, :2 * HEADS],
                                           gm[:, COL_SCAL:(SCAL_SLOT + 1) * shard - 2 * HEADS]], axis=1))
            blocks += [gm[:, d * shard - 2 * HEADS:(d + 1) * shard - 2 * HEADS] for d in range(SCAL_SLOT + 1, N_DEV)]
            parts["w_in"] = jnp.stack(blocks)
            for n in ("w_branch_a", "w_branch_b", "w_out"):
                parts[n] = g[n].reshape(N_DEV, D_MODEL // N_DEV, D_MODEL)
        *in_flight[l, stage], token = _exchange_start([parts[n] for n in _STAGES[stage]], gather=False,
                                                      name=f"scatter_start_l{l}_{stage}")
        return token

    loss_row, dx, grads = _local_step(x[0], loss_target[0], layers, weights_of, on_weight_grads)
    loss = lax.psum(loss_row[0, 0], ("x", "y", "c"))

    results = {n: None for n in _BIG}
    after = [dx]
    for l in reversed(range(L)):
        for stage, names in enumerate(_STAGES):
            landed, all_landed = _exchange_wait(*in_flight[l, stage], after, gather=False,
                                                name=f"scatter_wait_l{l}_{stage}")
            for n, parts in zip(names, landed):
                _, a, b = weights[n].shape
                results[n] = _adamw(parts, weights[n].reshape(L * a, b), mom_m[n].reshape(L * a, b),
                                    mom_v[n].reshape(L * a, b), layer=l, earlier=results[n], name=f"adamw_{n}_l{l}")
            after = [results[n][0] for n in names]
    out = {n: tuple(t.reshape(weights[n].shape) for t in results[n]) for n in _BIG}

    small_grads = [(n, jnp.stack([g[n].reshape(weights[n].shape[1:]) for g in grads])) for n in _SMALL]
    small_packed, spans = _pack_small(small_grads + [("conv", jnp.stack([g["conv_w"] for g in grads]))])
    small_packed = small_packed + all_landed[0, 0]
    small_sum = _sum_parts(_all_gather([small_packed], name="gather_small_grads")[0], name="sum_small_grads")
    rep_rows = spans["conv"][0]
    pack_rep = lambda d: _pad_rows(_pack_small([(n, d[n]) for n in _SMALL])[0], 8)
    rep_pad = (-rep_rows) % 8
    g_rep = jnp.pad(small_sum[:rep_rows], ((0, rep_pad), (0, 0)))
    res = _adamw(g_rep[None], pack_rep(weights), pack_rep(mom_m), pack_rep(mom_v), name="adamw_replicated")
    for n in _SMALL:
        out[n] = tuple(_unpack_small(t, spans, n) for t in res)
    conv_sum = _unpack_small(small_sum, spans, "conv")
    conv_mine = lax.dynamic_slice_in_dim(conv_sum, my_slot * conv_cols, conv_cols, axis=2).reshape(L * DN_CONV, conv_cols)
    flat = lambda t: t.reshape(L * DN_CONV, conv_cols)
    res = _adamw(conv_mine[None], flat(dn_conv_w), flat(m_dn_conv_w), flat(v_dn_conv_w), name="adamw_conv")
    out["dn_conv_w"] = tuple(t.reshape(L, DN_CONV, conv_cols) for t in res)

    return (loss, dx[None], *[out[n][0] for n in _ORDER], *[out[n][1] for n in _ORDER],
            *[out[n][2] for n in _ORDER], *[out[n][3] for n in _ORDER])
```
